```python
import jax, jax.numpy as jnp
from jax import lax
import numpy as np

D_MODEL = 1024
BATCH = 8
SEQ = 2048
DEPTH = 1
DEC_BATCH = 128
DEC_SEQ = 4
PAST_LEN = 16384
PAGE_SIZE = 128

POOL_WIDTH = D_MODEL // 2
POOL_WINDOWS = (2, 4, 8, 16)
POOL_GROUPS = len(POOL_WINDOWS)
POOL_GROUP_W = POOL_WIDTH // POOL_GROUPS
POOL_PAST = max(POOL_WINDOWS) - 1
GLA_HEADS = 4
GLA_WIDTH = D_MODEL // 2
GLA_DV = GLA_WIDTH // GLA_HEADS
GLA_DK = GLA_DV // 2
GLA_GATE_RANK = 16
GLA_GATE_TEMP = 16.0
GLA_CHUNK = 64
N_EXPERTS = 256
TOP_K = 8
EXPERT_WIDTH = D_MODEL // 4
SHARED_WIDTH = EXPERT_WIDTH
ROUTED_SCALE = 2.5
EXPERT_BLOCK = 128
N_MOD = 6
EPS = 1e-6
IN_SIZES = (POOL_WIDTH, GLA_HEADS * GLA_DK, GLA_HEADS * GLA_DK, GLA_WIDTH, GLA_WIDTH, GLA_GATE_RANK, D_MODEL, D_MODEL)
IN_WIDTH = sum(IN_SIZES)

kernel_name = "pool_gla_gated_moe_adaln_step"


def _rms(x, g):
    xf = x.astype(jnp.float32)
    y = xf * lax.rsqrt(jnp.mean(xf * xf, axis=-1, keepdims=True) + EPS)
    return (y * g.astype(jnp.float32)).astype(x.dtype)


def _pool_mixer(u, buf, pos0, w_grp, scale):
    B, T, _ = u.shape
    z = jnp.concatenate([buf.astype(u.dtype), u], axis=1).astype(jnp.float32)
    cs = jnp.pad(jnp.cumsum(z, axis=1), ((0, 0), (1, 0), (0, 0)))
    uf = u.astype(jnp.float32)
    pos = pos0 + jnp.arange(T)
    outs = []
    for g, w in enumerate(POOL_WINDOWS):
        ch = slice(g * POOL_GROUP_W, (g + 1) * POOL_GROUP_W)
        end = cs[:, POOL_PAST + 1:POOL_PAST + 1 + T, ch]
        start = cs[:, POOL_PAST + 1 - w:POOL_PAST + 1 - w + T, ch]
        cnt = jnp.minimum(w, pos + 1).astype(jnp.float32)[None, :, None]
        outs.append((end - start) / cnt - uf[:, :, ch])
    mixed = jnp.stack(outs, axis=2)
    y = jnp.einsum('btgc,gcd->btgd', mixed, w_grp.astype(jnp.float32)).reshape(B, T, POOL_WIDTH)
    y = y * scale.astype(jnp.float32)
    new_buf = z[:, -POOL_PAST:]
    return y.astype(u.dtype), new_buf.astype(buf.dtype)


def _gla_chunk(S, blk):
    q, k, v, la = blk
    C = q.shape[2]
    b = jnp.cumsum(la, axis=2)
    o_inter = jnp.einsum('bhtd,bhde->bhte', q * jnp.exp(b), S)
    causal = jnp.tril(jnp.ones((C, C), dtype=bool))
    diff = b[:, :, :, None, :] - b[:, :, None, :, :]
    decay = jnp.exp(jnp.where(causal[:, :, None], diff, -jnp.inf))
    scores = jnp.einsum('bhtjd,bhjd->bhtj', q[:, :, :, None, :] * decay, k)
    o = o_inter + jnp.einsum('bhtj,bhje->bhte', scores, v)
    b_last = b[:, :, -1:, :]
    S_new = jnp.exp(b_last[:, :, 0, :])[..., None] * S + jnp.einsum('bhjd,bhje->bhde', k * jnp.exp(b_last - b), v)
    return S_new, o


def _gla_mixer(q, k, v, go, alr, S0, w_alpha, b_alpha, norm_g):
    B, T, _ = q.shape
    chunk = GLA_CHUNK if T % GLA_CHUNK == 0 else T
    n = T // chunk
    la = jax.nn.log_sigmoid(alr.astype(jnp.float32) @ w_alpha.astype(jnp.float32) + b_alpha.astype(jnp.float32)) / GLA_GATE_TEMP

    def heads(t, d):
        return t.astype(jnp.float32).reshape(B, T, GLA_HEADS, d).transpose(0, 2, 1, 3)

    def chunks(t):
        return t.reshape(B, GLA_HEADS, n, chunk, t.shape[-1]).transpose(2, 0, 1, 3, 4)

    qh = heads(q, GLA_DK) * (GLA_DK ** -0.5)
    kh = heads(k, GLA_DK)
    vh = heads(v, GLA_DV)
    lah = heads(la, GLA_DK)
    S, o = lax.scan(_gla_chunk, S0.astype(jnp.float32), (chunks(qh), chunks(kh), chunks(vh), chunks(lah)))
    o = o.transpose(1, 2, 0, 3, 4).reshape(B, GLA_HEADS, T, GLA_DV).transpose(0, 2, 1, 3)
    o = _rms(o, norm_g).reshape(B, T, GLA_WIDTH) * jax.nn.silu(go.astype(jnp.float32))
    return o.astype(q.dtype), S.astype(S0.dtype)


def _routed_experts(h, idx, wts, w_gate, w_up, w_down):
    n_tok, D = h.shape
    n_asg = n_tok * TOP_K
    n_blocks = -(-n_asg // EXPERT_BLOCK) + N_EXPERTS
    n_rows = n_blocks * EXPERT_BLOCK
    flat_e = idx.reshape(-1)
    flat_t = jnp.repeat(jnp.arange(n_tok, dtype=jnp.int32), TOP_K)
    flat_w = wts.reshape(-1)
    order = jnp.argsort(flat_e)
    se, st, sw = flat_e[order], flat_t[order], flat_w[order]
    counts = jnp.bincount(flat_e, length=N_EXPERTS)
    padded = (counts + EXPERT_BLOCK - 1) // EXPERT_BLOCK * EXPERT_BLOCK
    pad_end = jnp.cumsum(padded)
    pad_start = pad_end - padded
    cnt_start = jnp.cumsum(counts) - counts
    dest = pad_start[se] + jnp.arange(n_asg) - cnt_start[se]
    row_tok = jnp.full((n_rows,), n_tok, jnp.int32).at[dest].set(st)
    row_w = jnp.zeros((n_rows,), jnp.float32).at[dest].set(sw)
    block_e = jnp.minimum(jnp.searchsorted(pad_end, jnp.arange(n_blocks) * EXPERT_BLOCK, side='right'), N_EXPERTS - 1)
    h_pad = jnp.concatenate([h, jnp.zeros((1, D), h.dtype)], axis=0)

    def step(acc, blk):
        toks, e, wrow = blk
        xb = h_pad[toks]
        yb = (jax.nn.silu(xb @ w_gate[e]) * (xb @ w_up[e])) @ w_down[e]
        return acc.at[toks].add(yb.astype(jnp.float32) * wrow[:, None]), None

    acc0 = jnp.zeros((n_tok + 1, D), jnp.float32)
    acc, _ = lax.scan(step, acc0, (row_tok.reshape(n_blocks, EXPERT_BLOCK), block_e, row_w.reshape(n_blocks, EXPERT_BLOCK)))
    return acc[:n_tok].astype(h.dtype)


def _moe(h, w_router, router_bias, w_exp_gate, w_exp_up, w_exp_down, w_sh_gate, w_sh_up, w_sh_down):
    s = jax.nn.sigmoid(h.astype(jnp.float32) @ w_router.astype(jnp.float32))
    _, idx = lax.top_k(s + router_bias.astype(jnp.float32), TOP_K)
    w = jnp.take_along_axis(s, idx, axis=-1)
    w = w / jnp.sum(w, axis=-1, keepdims=True) * ROUTED_SCALE
    routed = _routed_experts(h, idx, w, w_exp_gate, w_exp_up, w_exp_down)
    shared = (jax.nn.silu(h @ w_sh_gate) * (h @ w_sh_up)) @ w_sh_down
    return routed + shared


def _layer(x, c, pool_buf, gla_S, pos0, p):
    B, T, D = x.shape
    mod = (jax.nn.silu(c) @ p['w_ada'] + p['b_ada']).reshape(B, N_MOD, D)
    sh1, sc1, g1, sh2, sc2, g2 = [mod[:, i, None, :] for i in range(N_MOD)]
    h = _rms(x, p['norm1']) * (1 + sc1) + sh1
    z = h @ p['w_in']
    u, q, k, v, go, alr, ga, gb = jnp.split(z, np.cumsum(IN_SIZES)[:-1].tolist(), axis=-1)
    ya, new_buf = _pool_mixer(u, pool_buf, pos0, p['pool_w'], p['pool_scale'])
    yb, new_S = _gla_mixer(q, k, v, go, alr, gla_S, p['gla_w_alpha'], p['gla_b_alpha'], p['gla_norm'])
    m = jax.nn.sigmoid(ga) * (ya @ p['w_branch_a']) + jax.nn.sigmoid(gb) * (yb @ p['w_branch_b'])
    x = x + g1 * (m @ p['w_out'])
    h2 = (_rms(x, p['norm2']) * (1 + sc2) + sh2).reshape(B * T, D)
    f = _moe(h2, p['w_router'], p['router_bias'], p['w_exp_gate'], p['w_exp_up'], p['w_exp_down'],
             p['w_sh_gate'], p['w_sh_up'], p['w_sh_down']).reshape(B, T, D)
    x = x + g2 * f
    return x, new_buf, new_S


def _trunk(x, c, pool_state, gla_state, pos0, params, norm_f):
    bufs, states = [], []
    for l in range(DEPTH):
        p = {name: arr[l] for name, arr in params.items()}
        x, nb, ns = _layer(x, c, pool_state[l], gla_state[l], pos0, p)
        bufs.append(nb)
        states.append(ns)
    return _rms(x, norm_f), jnp.stack(bufs), jnp.stack(states)


def setup_inputs(seed: int = 0) -> dict:
    key = jax.random.key(seed)
    ks = jax.random.split(key, 32)
    nrm = jax.random.normal
    D = D_MODEL
    return {
        "x_prompt": nrm(ks[0], (BATCH, SEQ, D), jnp.float32),
        "x_sample": nrm(ks[1], (DEC_BATCH, DEC_SEQ, D), jnp.float32),
        "c_prompt": nrm(ks[2], (BATCH, D), jnp.float32),
        "c_sample": nrm(ks[3], (DEC_BATCH, D), jnp.float32),
        "state_pool": nrm(ks[4], (DEPTH, DEC_BATCH, POOL_PAST, POOL_WIDTH), jnp.float32),
        "state_gla": 0.5 * nrm(ks[5], (DEPTH, DEC_BATCH, GLA_HEADS, GLA_DK, GLA_DV), jnp.float32),
        "w_ada": nrm(ks[6], (DEPTH, D, N_MOD * D), jnp.float32) * (0.5 * D ** -0.5),
        "b_ada": 0.01 * nrm(ks[7], (DEPTH, N_MOD * D), jnp.float32),
        "norm1": 1.0 + 0.05 * nrm(ks[8], (DEPTH, D), jnp.float32),
        "w_in": nrm(ks[9], (DEPTH, D, IN_WIDTH), jnp.float32) * D ** -0.5,
        "pool_w": nrm(ks[10], (DEPTH, POOL_GROUPS, POOL_GROUP_W, POOL_GROUP_W), jnp.float32) * POOL_GROUP_W ** -0.5,
        "pool_scale": 1.0 + 0.1 * nrm(ks[11], (DEPTH, POOL_WIDTH), jnp.float32),
        "gla_w_alpha": nrm(ks[12], (DEPTH, GLA_GATE_RANK, GLA_HEADS * GLA_DK), jnp.float32) * GLA_GATE_RANK ** -0.5,
        "gla_b_alpha": 0.01 * nrm(ks[13], (DEPTH, GLA_HEADS * GLA_DK), jnp.float32),
        "gla_norm": 1.0 + 0.05 * nrm(ks[14], (DEPTH, GLA_HEADS, GLA_DV), jnp.float32),
        "w_branch_a": nrm(ks[15], (DEPTH, POOL_WIDTH, D), jnp.float32) * POOL_WIDTH ** -0.5,
        "w_branch_b": nrm(ks[16], (DEPTH, GLA_WIDTH, D), jnp.float32) * GLA_WIDTH ** -0.5,
        "w_out": nrm(ks[17], (DEPTH, D, D), jnp.float32) * D ** -0.5,
        "norm2": 1.0 + 0.05 * nrm(ks[18], (DEPTH, D), jnp.float32),
        "w_router": nrm(ks[19], (DEPTH, D, N_EXPERTS), jnp.float32) * D ** -0.5,
        "router_bias": 0.01 * nrm(ks[20], (DEPTH, N_EXPERTS), jnp.float32),
        "w_exp_gate": nrm(ks[21], (DEPTH, N_EXPERTS, D, EXPERT_WIDTH), jnp.float32) * D ** -0.5,
        "w_exp_up": nrm(ks[22], (DEPTH, N_EXPERTS, D, EXPERT_WIDTH), jnp.float32) * D ** -0.5,
        "w_exp_down": nrm(ks[23], (DEPTH, N_EXPERTS, EXPERT_WIDTH, D), jnp.float32) * EXPERT_WIDTH ** -0.5,
        "w_sh_gate": nrm(ks[24], (DEPTH, D, SHARED_WIDTH), jnp.float32) * D ** -0.5,
        "w_sh_up": nrm(ks[25], (DEPTH, D, SHARED_WIDTH), jnp.float32) * D ** -0.5,
        "w_sh_down": nrm(ks[26], (DEPTH, SHARED_WIDTH, D), jnp.float32) * SHARED_WIDTH ** -0.5,
        "norm_f": 1.0 + 0.05 * nrm(ks[27], (D,), jnp.float32),
    }


def reference(x_prompt, x_sample, c_prompt, c_sample, state_pool, state_gla, w_ada, b_ada, norm1, w_in,
              pool_w, pool_scale, gla_w_alpha, gla_b_alpha, gla_norm, w_branch_a, w_branch_b, w_out, norm2,
              w_router, router_bias, w_exp_gate, w_exp_up, w_exp_down, w_sh_gate, w_sh_up, w_sh_down, norm_f):
    params = dict(w_ada=w_ada, b_ada=b_ada, norm1=norm1, w_in=w_in, pool_w=pool_w, pool_scale=pool_scale,
                  gla_w_alpha=gla_w_alpha, gla_b_alpha=gla_b_alpha, gla_norm=gla_norm, w_branch_a=w_branch_a,
                  w_branch_b=w_branch_b, w_out=w_out, norm2=norm2, w_router=w_router, router_bias=router_bias,
                  w_exp_gate=w_exp_gate, w_exp_up=w_exp_up, w_exp_down=w_exp_down, w_sh_gate=w_sh_gate,
                  w_sh_up=w_sh_up, w_sh_down=w_sh_down)
    bp = x_prompt.shape[0]
    pool0 = jnp.zeros((DEPTH, bp, POOL_PAST, POOL_WIDTH), state_pool.dtype)
    gla0 = jnp.zeros((DEPTH, bp, GLA_HEADS, GLA_DK, GLA_DV), state_gla.dtype)
    y_prompt, new_pool_prompt, new_gla_prompt = _trunk(x_prompt, c_prompt, pool0, gla0, 0, params, norm_f)
    y_sample, new_pool_sample, new_gla_sample = _trunk(x_sample, c_sample, state_pool, state_gla, PAST_LEN, params, norm_f)
    return (y_prompt, y_sample, new_pool_prompt, new_gla_prompt, new_pool_sample, new_gla_sample)
```

```python
import functools

import jax
import jax.numpy as jnp
from jax import lax
from jax.experimental import pallas as pl
from jax.experimental.pallas import tpu as pltpu

F32 = jnp.float32
BF16 = jnp.bfloat16
I32 = jnp.int32

EPS = 1e-6
N_MOD = 6
POOL_WINDOWS = (2, 4, 8, 16)
POOL_HIST = 16
GLA_GATE_TEMP = 16.0
TOP_K = 8
ROUTED_SCALE = 2.5
PAST_LEN = 16384
EXPERT_BLOCK = 128
LANE = 128
GLA_FAST_MAX_DECAY = 40.0
VMEM_LIMIT = 56 * 1024 * 1024


def _cp(sem, vmem=VMEM_LIMIT):
    return pltpu.CompilerParams(dimension_semantics=sem, vmem_limit_bytes=vmem)


def _rms(x, g):
    return x * lax.rsqrt(jnp.mean(x * x, axis=-1, keepdims=True) + EPS) * g


def _silu(x):
    return x * jax.nn.sigmoid(x)


def _dot(a, b):
    return jnp.dot(a, b, preferred_element_type=F32)


def _split3(a):
    a0 = a.astype(BF16)
    r1 = a - a0.astype(F32)
    a1 = r1.astype(BF16)
    a2 = (r1 - a1.astype(F32)).astype(BF16)
    return a0, a1, a2


def _mod_parts(mod, d):
    return [mod[:, i * d:(i + 1) * d] for i in range(N_MOD)]


def _ada_kernel(c_ref, w_ref, b_ref, o_ref):
    a = _silu(c_ref[...]).astype(BF16)
    o_ref[...] = _dot(a, w_ref[...].astype(BF16)) + b_ref[...]


def _ada(c_all, w_ada, b_ada):
    n, d = c_all.shape
    cols = w_ada.shape[1]
    bc = d
    return pl.pallas_call(
        _ada_kernel,
        grid=(cols // bc,),
        in_specs=[pl.BlockSpec((n, d), lambda i: (0, 0)),
                  pl.BlockSpec((d, bc), lambda i: (0, i)),
                  pl.BlockSpec((1, bc), lambda i: (0, i))],
        out_specs=pl.BlockSpec((n, bc), lambda i: (0, i)),
        out_shape=jax.ShapeDtypeStruct((n, cols), F32),
        compiler_params=_cp(("arbitrary",)),
        name="ada",
    )(c_all, w_ada, b_ada.reshape(1, cols))


class _Dims:
    def __init__(self, d, pw, heads, dk, dv):
        self.d, self.pw, self.h, self.dk, self.dv = d, pw, heads, dk, dv
        self.hk, self.gw = heads * dk, heads * dv
        self.q0 = pw
        self.k0 = self.q0 + self.hk
        self.v0 = self.k0 + self.hk
        self.go0 = self.v0 + self.gw
        self.ga0 = self.go0 + self.gw
        self.gb0 = self.ga0 + d
        self.al0 = self.gb0 + d
        self.zw = self.al0 + LANE


def _log_decay(z_al, wal_ref, bal_ref):
    xal = _dot(z_al.astype(BF16), wal_ref[...]) + bal_ref[...]
    return jax.nn.log_sigmoid(xal) / GLA_GATE_TEMP


def _pool_project(mixed_groups, pw_ref, ps_ref):
    ys = [_dot(m.astype(BF16), pw_ref[g]) for g, m in enumerate(mixed_groups)]
    return jnp.concatenate(ys, axis=1) * ps_ref[...]


def _post_mix(dm, x, mod, ya, o, go, ga, gb, gn_ref, wa_ref, wb_ref, wo_ref, n2_ref, wrh_ref, wrl_ref):
    _, _, g1, sh2, sc2, _ = _mod_parts(mod, dm.d)
    parts = []
    for h in range(dm.h):
        oh = o[:, h * dm.dv:(h + 1) * dm.dv]
        parts.append(oh * lax.rsqrt(jnp.mean(oh * oh, axis=-1, keepdims=True) + EPS))
    yb = jnp.concatenate(parts, axis=1) * gn_ref[...] * _silu(go)
    m = (jax.nn.sigmoid(ga) * _dot(ya.astype(BF16), wa_ref[...])
         + jax.nn.sigmoid(gb) * _dot(yb.astype(BF16), wb_ref[...]))
    x1 = x + g1 * _dot(m.astype(BF16), wo_ref[...])
    h2 = _rms(x1, n2_ref[...]) * (1 + sc2) + sh2
    hi = h2.astype(BF16)
    lo = (h2 - hi.astype(F32)).astype(BF16)
    logits = _dot(hi, wrh_ref[...]) + (_dot(hi, wrl_ref[...]) + _dot(lo, wrh_ref[...]))
    return x1, h2, logits


def _mixp_kernel(x_ref, mod_ref, n1_ref, win_ref, wal_ref, bal_ref, pw_ref, ps_ref, gn_ref,
                 wa_ref, wb_ref, wo_ref, n2_ref, wrh_ref, wrl_ref,
                 x1_ref, h2_ref, lg_ref, np_ref, ng_ref,
                 z_ref, uext_ref, s_ref, b_ref, oi_ref, *, dm, tt, n_t):
    j = pl.program_id(1)
    x = x_ref[...]
    mod = mod_ref[...]
    sh1, sc1 = mod[:, 0:dm.d], mod[:, dm.d:2 * dm.d]
    h = _rms(x, n1_ref[...]) * (1 + sc1) + sh1
    z_ref[...] = _dot(h.astype(BF16), win_ref[...])

    @pl.when(j == 0)
    def _():
        uext_ref[0:POOL_HIST, :] = jnp.zeros((POOL_HIST, dm.pw), F32)
        s_ref[...] = jnp.zeros(s_ref.shape, F32)

    uext_ref[POOL_HIST:POOL_HIST + tt, :] = z_ref[:, 0:dm.pw]
    pos = j * tt + lax.broadcasted_iota(I32, (tt, 1), 0)
    gw = dm.pw // len(POOL_WINDOWS)
    mixed = []
    for g, w in enumerate(POOL_WINDOWS):
        c0 = g * gw
        cur = uext_ref[POOL_HIST:POOL_HIST + tt, c0:c0 + gw]
        acc = cur
        for i in range(1, w):
            acc = acc + uext_ref[POOL_HIST - i:POOL_HIST - i + tt, c0:c0 + gw]
        cnt = jnp.minimum(w, pos + 1).astype(F32)
        mixed.append(acc / cnt - cur)
    ya = _pool_project(mixed, pw_ref, ps_ref)

    @pl.when(j == n_t - 1)
    def _():
        np_ref[...] = uext_ref[tt + 1:tt + POOL_HIST, :]

    uext_ref[0:POOL_HIST, :] = uext_ref[tt:tt + POOL_HIST, :]

    la = _log_decay(z_ref[:, dm.al0:dm.al0 + LANE], wal_ref, bal_ref)
    r_i = lax.broadcasted_iota(I32, (tt, tt), 0)
    c_i = lax.broadcasted_iota(I32, (tt, tt), 1)
    causal = r_i >= c_i
    tri = jnp.where(causal, 1.0, 0.0).astype(BF16)
    a0, a1, a2 = _split3(la)
    bc = _dot(tri, a0) + (_dot(tri, a1) + _dot(tri, a2))
    b_ref[...] = bc
    q = z_ref[:, dm.q0:dm.q0 + dm.hk]
    k = z_ref[:, dm.k0:dm.k0 + dm.hk]
    v = z_ref[:, dm.v0:dm.v0 + dm.gw]
    scale = dm.dk ** -0.5
    qe = (q * scale) * jnp.exp(bc)
    blast = bc[tt - 1:tt, :]
    kd = k * jnp.exp(blast - bc)
    s_old = s_ref[...]
    s_bf = s_old.astype(BF16)
    vb = v.astype(BF16)
    lane = lax.broadcasted_iota(I32, (1, dm.hk), 1)
    head_masks = [(lane >= h * dm.dk) & (lane < (h + 1) * dm.dk) for h in range(dm.h)]
    o_inter = [_dot(jnp.where(head_masks[h], qe, 0.0).astype(BF16), s_bf) for h in range(dm.h)]

    fast = jnp.max(-blast) <= GLA_FAST_MAX_DECAY

    @pl.when(fast)
    def _():
        ke = (k * jnp.exp(-bc)).astype(BF16)
        for h in range(dm.h):
            qh = jnp.where(head_masks[h], qe, 0.0).astype(BF16)
            sc = lax.dot_general(qh, ke, (((1,), (1,)), ((), ())), preferred_element_type=F32)
            sc = jnp.where(causal, sc, 0.0).astype(BF16)
            oi_ref[:, h * dm.dv:(h + 1) * dm.dv] = _dot(sc, vb[:, h * dm.dv:(h + 1) * dm.dv])

    @pl.when(jnp.logical_not(fast))
    def _():
        rows = lax.broadcasted_iota(I32, (tt, 1), 0)
        for h in range(dm.h):
            def body(tb, carry, h=h):
                base = pl.multiple_of(tb * 8, 8)
                q8 = z_ref[pl.ds(base, 8), dm.q0:dm.q0 + dm.hk] * scale
                b8 = b_ref[pl.ds(base, 8), :]
                vh = z_ref[:, dm.v0 + h * dm.dv:dm.v0 + (h + 1) * dm.dv]
                out_rows = []
                for r in range(8):
                    keep = (rows <= base + r) & head_masks[h]
                    dec = jnp.exp(jnp.where(keep, b8[r:r + 1, :] - b_ref[...], -jnp.inf))
                    wgt = (q8[r:r + 1, :] * dec) * z_ref[:, dm.k0:dm.k0 + dm.hk]
                    s_col = jnp.sum(wgt, axis=-1, keepdims=True)
                    out_rows.append(jnp.sum(s_col * vh, axis=0, keepdims=True))
                oi_ref[pl.ds(base, 8), h * dm.dv:(h + 1) * dm.dv] = jnp.concatenate(out_rows, axis=0)
                return carry
            lax.fori_loop(0, tt // 8, body, 0)

    o = jnp.concatenate(o_inter, axis=1) + oi_ref[...]

    u_all = _dot(kd.T.astype(BF16), vb)
    dec_col = jnp.broadcast_to(jnp.exp(blast), (dm.hk, dm.hk)).T[:, 0:dm.dv]
    upd = jnp.concatenate(
        [u_all[h * dm.dk:(h + 1) * dm.dk, h * dm.dv:(h + 1) * dm.dv] for h in range(dm.h)], axis=0)
    s_new = dec_col * s_old + upd
    s_ref[...] = s_new

    @pl.when(j == n_t - 1)
    def _():
        ng_ref[...] = s_new

    go = z_ref[:, dm.go0:dm.go0 + dm.gw]
    ga = z_ref[:, dm.ga0:dm.ga0 + dm.d]
    gb = z_ref[:, dm.gb0:dm.gb0 + dm.d]
    x1, h2, logits = _post_mix(dm, x, mod, ya, o, go, ga, gb, gn_ref, wa_ref, wb_ref, wo_ref,
                               n2_ref, wrh_ref, wrl_ref)
    x1_ref[...] = x1
    h2_ref[...] = h2
    lg_ref[...] = logits


def _const_spec(shape):
    nd = len(shape)
    return pl.BlockSpec(shape, lambda *_: (0,) * nd, pipeline_mode=pl.Buffered(1))


def _mix_prompt(dm, x2d, mod3, wts, bsz, seq, n_exp):
    tt = 256 if seq % 256 == 0 else seq
    n_t = seq // tt
    n = bsz * seq
    d = dm.d
    kern = functools.partial(_mixp_kernel, dm=dm, tt=tt, n_t=n_t)
    row = lambda b, j: (b * n_t + j, 0)
    in_specs = [pl.BlockSpec((tt, d), row),
                pl.BlockSpec((None, 1, N_MOD * d), lambda b, j: (b, 0, 0))]
    in_specs += [_const_spec(w.shape) for w in wts]
    out_specs = [pl.BlockSpec((tt, d), row), pl.BlockSpec((tt, d), row), pl.BlockSpec((tt, n_exp), row),
                 pl.BlockSpec((None, POOL_HIST - 1, dm.pw), lambda b, j: (b, 0, 0)),
                 pl.BlockSpec((None, dm.hk, dm.dv), lambda b, j: (b, 0, 0))]
    out_shape = [jax.ShapeDtypeStruct((n, d), F32), jax.ShapeDtypeStruct((n, d), F32),
                 jax.ShapeDtypeStruct((n, n_exp), F32),
                 jax.ShapeDtypeStruct((bsz, POOL_HIST - 1, dm.pw), F32),
                 jax.ShapeDtypeStruct((bsz, dm.hk, dm.dv), F32)]
    scratch = [pltpu.VMEM((tt, dm.zw), F32), pltpu.VMEM((tt + POOL_HIST, dm.pw), F32),
               pltpu.VMEM((dm.hk, dm.dv), F32), pltpu.VMEM((tt, dm.hk), F32), pltpu.VMEM((tt, dm.gw), F32)]
    return pl.pallas_call(
        kern, grid=(bsz, n_t), in_specs=in_specs, out_specs=out_specs, out_shape=out_shape,
        scratch_shapes=scratch, compiler_params=_cp(("arbitrary", "arbitrary")), name="mixp",
    )(x2d, mod3, *wts)


def _sin_kernel(x_ref, mod_ref, n1_ref, win_ref, z_ref, *, d):
    mod = mod_ref[...]
    h = _rms(x_ref[...], n1_ref[...]) * (1 + mod[:, d:2 * d]) + mod[:, 0:d]
    z_ref[...] = _dot(h.astype(BF16), win_ref[...])


def _sample_in(dm, xs2d, mod_s, n1, win, bs, t_s):
    d = dm.d
    return pl.pallas_call(
        functools.partial(_sin_kernel, d=d),
        grid=(t_s,),
        in_specs=[pl.BlockSpec((bs, d), lambda t: (0, t)),
                  _const_spec(mod_s.shape), _const_spec(n1.shape), _const_spec(win.shape)],
        out_specs=pl.BlockSpec((bs, dm.zw), lambda t: (t, 0)),
        out_shape=jax.ShapeDtypeStruct((t_s * bs, dm.zw), F32),
        compiler_params=_cp(("arbitrary",)), name="sample_in",
    )(xs2d, mod_s, n1, win)


def _sst_kernel(z_ref, pool_ref, gs_ref, wal_ref, bal_ref,
                mixed_ref, o_ref, npool_ref, ngs_ref,
                qt_ref, kt_ref, dt_ref, vt_ref, ot_ref, *, dm, bs, t_s, ch, pos0):
    i = pl.program_id(0)
    n_i = pl.num_programs(0)
    scale = dm.dk ** -0.5
    past = POOL_HIST - 1

    @pl.when(i == 0)
    def _():
        for t in range(t_s):
            r0 = t * bs
            la = _log_decay(z_ref[r0:r0 + bs, dm.al0:dm.al0 + LANE], wal_ref, bal_ref)
            dt_ref[t] = jnp.exp(la).T
            qt_ref[t] = (z_ref[r0:r0 + bs, dm.q0:dm.q0 + dm.hk] * scale).T
            kt_ref[t] = z_ref[r0:r0 + bs, dm.k0:dm.k0 + dm.hk].T
            vt_ref[t] = z_ref[r0:r0 + bs, dm.v0:dm.v0 + dm.gw].T
        ot_ref[...] = jnp.zeros(ot_ref.shape, F32)

        def item(s, c0, c1):
            if s >= 0:
                return z_ref[s * bs:(s + 1) * bs, c0:c1]
            return pool_ref[:, (past + s) * dm.pw + c0:(past + s) * dm.pw + c1]

        gw = dm.pw // len(POOL_WINDOWS)
        for t in range(t_s):
            for g, w in enumerate(POOL_WINDOWS):
                c0, c1 = g * gw, (g + 1) * gw
                cur = item(t, c0, c1)
                acc = cur
                for r in range(1, w):
                    acc = acc + item(t - r, c0, c1)
                cnt = float(min(w, pos0 + t + 1))
                mixed_ref[t * bs:(t + 1) * bs, c0:c1] = acc / cnt - cur
        for r in range(past):
            npool_ref[:, r * dm.pw:(r + 1) * dm.pw] = item(r - past + t_s, 0, dm.pw)

    per_head = dm.dk // ch
    hoff = pl.multiple_of((i // per_head) * dm.dv, dm.dv)
    for p in range(ch):
        hd = i * ch + p
        st = gs_ref[:, p * dm.dv:(p + 1) * dm.dv].T
        for t in range(t_s):
            st = dt_ref[t, pl.ds(hd, 1), :] * st + kt_ref[t, pl.ds(hd, 1), :] * vt_ref[t, pl.ds(hoff, dm.dv), :]
            ot_ref[t, pl.ds(hoff, dm.dv), :] += qt_ref[t, pl.ds(hd, 1), :] * st
        ngs_ref[:, p * dm.dv:(p + 1) * dm.dv] = st.T

    @pl.when(i == n_i - 1)
    def _():
        for t in range(t_s):
            for h in range(dm.h):
                o_ref[t * bs:(t + 1) * bs, h * dm.dv:(h + 1) * dm.dv] = ot_ref[t, h * dm.dv:(h + 1) * dm.dv, :].T


def _sample_state(dm, z_s, pool2d, gs2d, wal, bal, bs, t_s):
    ch = 16
    n_i = dm.hk // ch
    past = POOL_HIST - 1
    kern = functools.partial(_sst_kernel, dm=dm, bs=bs, t_s=t_s, ch=ch, pos0=PAST_LEN)
    full = lambda shape: pl.BlockSpec(shape, lambda i: (0,) * len(shape))
    return pl.pallas_call(
        kern, grid=(n_i,),
        in_specs=[_const_spec(z_s.shape), _const_spec(pool2d.shape),
                  pl.BlockSpec((bs, ch * dm.dv), lambda i: (0, i)),
                  _const_spec(wal.shape), _const_spec(bal.shape)],
        out_specs=[full((t_s * bs, dm.pw)), full((t_s * bs, dm.gw)), full((bs, past * dm.pw)),
                   pl.BlockSpec((bs, ch * dm.dv), lambda i: (0, i))],
        out_shape=[jax.ShapeDtypeStruct((t_s * bs, dm.pw), F32), jax.ShapeDtypeStruct((t_s * bs, dm.gw), F32),
                   jax.ShapeDtypeStruct((bs, past * dm.pw), F32), jax.ShapeDtypeStruct(gs2d.shape, F32)],
        scratch_shapes=[pltpu.VMEM((t_s, dm.hk, bs), F32), pltpu.VMEM((t_s, dm.hk, bs), F32),
                        pltpu.VMEM((t_s, dm.hk, bs), F32), pltpu.VMEM((t_s, dm.gw, bs), F32),
                        pltpu.VMEM((t_s, dm.gw, bs), F32)],
        compiler_params=_cp(("arbitrary",)), name="sample_state",
    )(z_s, pool2d, gs2d, wal, bal)


def _spost_kernel(x_ref, mod_ref, z_ref, mixed_ref, o_ref, pw_ref, ps_ref, gn_ref,
                  wa_ref, wb_ref, wo_ref, n2_ref, wrh_ref, wrl_ref,
                  x1_ref, h2_ref, lg_ref, *, dm):
    gw = dm.pw // len(POOL_WINDOWS)
    ya = _pool_project([mixed_ref[:, g * gw:(g + 1) * gw] for g in range(len(POOL_WINDOWS))], pw_ref, ps_ref)
    x1, h2, logits = _post_mix(
        dm, x_ref[...], mod_ref[...], ya, o_ref[...],
        z_ref[:, dm.go0:dm.go0 + dm.gw], z_ref[:, dm.ga0:dm.ga0 + dm.d], z_ref[:, dm.gb0:dm.gb0 + dm.d],
        gn_ref, wa_ref, wb_ref, wo_ref, n2_ref, wrh_ref, wrl_ref)
    x1_ref[...] = x1
    h2_ref[...] = h2
    lg_ref[...] = logits


def _sample_post(dm, xs2d, mod_s, z_s, mixed_s, o_s, wts, bs, t_s, n_exp):
    d = dm.d
    rows = lambda t: (t, 0)
    in_specs = [pl.BlockSpec((bs, d), lambda t: (0, t)), _const_spec(mod_s.shape),
                pl.BlockSpec((bs, dm.zw), rows), pl.BlockSpec((bs, dm.pw), rows), pl.BlockSpec((bs, dm.gw), rows)]
    in_specs += [_const_spec(w.shape) for w in wts]
    return pl.pallas_call(
        functools.partial(_spost_kernel, dm=dm), grid=(t_s,), in_specs=in_specs,
        out_specs=[pl.BlockSpec((bs, d), rows), pl.BlockSpec((bs, d), rows), pl.BlockSpec((bs, n_exp), rows)],
        out_shape=[jax.ShapeDtypeStruct((t_s * bs, d), F32), jax.ShapeDtypeStruct((t_s * bs, d), F32),
                   jax.ShapeDtypeStruct((t_s * bs, n_exp), F32)],
        compiler_params=_cp(("arbitrary",)), name="sample_post",
    )(xs2d, mod_s, z_s, mixed_s, o_s, *wts)


def _route_kernel(lg_ref, bias_ref, idx_ref, w_ref, cnt_ref, *, n_exp):
    i = pl.program_id(0)
    s = jax.nn.sigmoid(lg_ref[...])
    cur = s + bias_ref[...]
    tr = s.shape[0]
    lane = lax.broadcasted_iota(I32, (1, n_exp), 1).astype(F32)
    slot = lax.broadcasted_iota(I32, (1, LANE), 1)
    idx8 = jnp.zeros((tr, LANE), F32)
    w8 = jnp.zeros((tr, LANE), F32)
    wsum = jnp.zeros((tr, 1), F32)
    picked = jnp.zeros((tr, n_exp), F32)
    for k in range(TOP_K):
        m = jnp.max(cur, axis=-1, keepdims=True)
        am = jnp.min(jnp.where(cur == m, lane, float(n_exp)), axis=-1, keepdims=True)
        oh = lane == am
        sk = jnp.sum(jnp.where(oh, s, 0.0), axis=-1, keepdims=True)
        cur = jnp.where(oh, -jnp.inf, cur)
        idx8 = jnp.where(slot == k, am, idx8)
        w8 = jnp.where(slot == k, sk, w8)
        wsum = wsum + sk
        picked = picked + jnp.where(oh, 1.0, 0.0)
    idx_ref[...] = idx8.astype(I32)
    w_ref[...] = w8 / wsum * ROUTED_SCALE

    @pl.when(i == 0)
    def _():
        cnt_ref[...] = jnp.zeros(cnt_ref.shape, F32)

    cnt_ref[...] += jnp.sum(picked, axis=0, keepdims=True)


def _route(logits, bias, tr):
    n, n_exp = logits.shape
    rows = lambda i: (i, 0)
    return pl.pallas_call(
        functools.partial(_route_kernel, n_exp=n_exp), grid=(n // tr,),
        in_specs=[pl.BlockSpec((tr, n_exp), rows), _const_spec((1, n_exp))],
        out_specs=[pl.BlockSpec((tr, LANE), rows), pl.BlockSpec((tr, LANE), rows),
                   pl.BlockSpec((8, n_exp), lambda i: (0, 0))],
        out_shape=[jax.ShapeDtypeStruct((n, LANE), I32), jax.ShapeDtypeStruct((n, LANE), F32),
                   jax.ShapeDtypeStruct((8, n_exp), F32)],
        compiler_params=_cp(("arbitrary",)), name="route",
    )(logits, bias.reshape(1, n_exp))


def _plan_kernel(idx_ref, cnt_ref, pos_ref, be_ref, meta_ref, start_ref, run_ref, *, n_exp, nbp):
    i = pl.program_id(0)
    lane = lax.broadcasted_iota(I32, (1, n_exp), 1)

    @pl.when(i == 0)
    def _():
        cnt = cnt_ref[...].astype(I32)
        padded = ((cnt + (EXPERT_BLOCK - 1)) // EXPERT_BLOCK) * EXPERT_BLOCK
        pe = padded
        s = 1
        while s < n_exp:
            pe = pe + jnp.where(lane >= s, pltpu.roll(pe, s, 1), 0)
            s *= 2
        start_ref[...] = (pe - padded).astype(F32)
        run_ref[...] = jnp.zeros(run_ref.shape, F32)
        blk0 = lax.broadcasted_iota(I32, (nbp, 1), 0) * EXPERT_BLOCK
        be = jnp.sum(jnp.where(pe[0:1, :] <= blk0, 1.0, 0.0), axis=-1, keepdims=True)
        be_ref[...] = jnp.broadcast_to(jnp.minimum(be, n_exp - 1.0), (nbp, LANE)).astype(I32)
        total = jnp.max(pe.astype(F32), axis=-1, keepdims=True)
        meta_ref[...] = jnp.broadcast_to(total * (1.0 / EXPERT_BLOCK), (8, LANE)).astype(I32)

    idx = idx_ref[...]
    tr = idx.shape[0]
    onehots = [lane == idx[:, k:k + 1] for k in range(TOP_K)]
    member = onehots[0]
    for k in range(1, TOP_K):
        member = member | onehots[k]
    mb = jnp.where(member, 1.0, 0.0).astype(BF16)
    r_i = lax.broadcasted_iota(I32, (tr, tr), 0)
    c_i = lax.broadcasted_iota(I32, (tr, tr), 1)
    earlier = jnp.where(r_i > c_i, 1.0, 0.0).astype(BF16)
    dest = _dot(earlier, mb) + (start_ref[0:1, :] + run_ref[0:1, :])
    slot = lax.broadcasted_iota(I32, (1, LANE), 1)
    pos8 = jnp.zeros((tr, LANE), F32)
    for k in range(TOP_K):
        pk = jnp.sum(jnp.where(onehots[k], dest, 0.0), axis=-1, keepdims=True)
        pos8 = jnp.where(slot == k, pk, pos8)
    pos_ref[...] = pos8.astype(I32)
    run_ref[...] += jnp.sum(mb.astype(F32), axis=0, keepdims=True)


def _plan(idx8, counts, tr, nblk):
    n = idx8.shape[0]
    n_exp = counts.shape[1]
    nbp = -(-nblk // 8) * 8
    rows = lambda i: (i, 0)
    return pl.pallas_call(
        functools.partial(_plan_kernel, n_exp=n_exp, nbp=nbp), grid=(n // tr,),
        in_specs=[pl.BlockSpec((tr, LANE), rows), _const_spec((8, n_exp))],
        out_specs=[pl.BlockSpec((tr, LANE), rows), pl.BlockSpec((nbp, LANE), lambda i: (0, 0)),
                   pl.BlockSpec((8, LANE), lambda i: (0, 0))],
        out_shape=[jax.ShapeDtypeStruct((n, LANE), I32), jax.ShapeDtypeStruct((nbp, LANE), I32),
                   jax.ShapeDtypeStruct((8, LANE), I32)],
        scratch_shapes=[pltpu.VMEM((8, n_exp), F32), pltpu.VMEM((8, n_exp), F32)],
        compiler_params=_cp(("arbitrary",)), name="plan",
    )(idx8, counts)


def _row_copy(src_ref, src_row, dst_ref, dst_row, sem):
    return pltpu.make_async_copy(src_ref.at[pl.ds(src_row, 1)], dst_ref.at[pl.ds(dst_row, 1)], sem)


def _dispatch_kernel(pos_ref, h_ref, xs_ref, sem, *, td):
    def issue(t, carry):
        for k in range(TOP_K):
            _row_copy(h_ref, t, xs_ref, pos_ref[0, 0, t * TOP_K + k], sem).start()
        return carry
    lax.fori_loop(0, td, issue, 0)

    def drain(t, carry):
        for k in range(TOP_K):
            _row_copy(h_ref, 0, xs_ref, 0, sem).wait()
        return carry
    lax.fori_loop(0, td, drain, 0)


def _dispatch(h2_all, pos_tiles, n_rows, td):
    n, d = h2_all.shape
    return pl.pallas_call(
        functools.partial(_dispatch_kernel, td=td), grid=(n // td,),
        in_specs=[pl.BlockSpec((1, 1, td * TOP_K), lambda i: (i, 0, 0), memory_space=pltpu.SMEM),
                  pl.BlockSpec((td, d), lambda i: (i, 0))],
        out_specs=pl.BlockSpec(memory_space=pl.ANY),
        out_shape=jax.ShapeDtypeStruct((n_rows, d), F32),
        scratch_shapes=[pltpu.SemaphoreType.DMA],
        compiler_params=_cp(("arbitrary",)), name="dispatch",
    )(pos_tiles, h2_all)


def _moe_kernel(be_ref, meta_ref, xs_ref, wg_ref, wu_ref, wd_ref, ys_ref, wgb, wub, wdb):
    i = pl.program_id(0)

    @pl.when(i < meta_ref[0])
    def _():
        e = be_ref[i]
        prev = be_ref[jnp.maximum(i - 1, 0)]

        @pl.when((i == 0) | (e != prev))
        def _():
            wgb[...] = wg_ref[...].astype(BF16)
            wub[...] = wu_ref[...].astype(BF16)
            wdb[...] = wd_ref[...].astype(BF16)

        x = xs_ref[...].astype(BF16)
        mid = _silu(_dot(x, wgb[...])) * _dot(x, wub[...])
        ys_ref[...] = _dot(mid.astype(BF16), wdb[...])


def _moe(block_e, meta, xs, wg, wu, wd, nblk):
    n_rows, d = xs.shape
    f = wg.shape[2]
    last = lambda i, be, meta: jnp.minimum(i, meta[0] - 1)
    grid_spec = pltpu.PrefetchScalarGridSpec(
        num_scalar_prefetch=2, grid=(nblk,),
        in_specs=[pl.BlockSpec((EXPERT_BLOCK, d), lambda i, be, meta: (last(i, be, meta), 0)),
                  pl.BlockSpec((None, d, f), lambda i, be, meta: (be[last(i, be, meta)], 0, 0)),
                  pl.BlockSpec((None, d, f), lambda i, be, meta: (be[last(i, be, meta)], 0, 0)),
                  pl.BlockSpec((None, f, d), lambda i, be, meta: (be[last(i, be, meta)], 0, 0))],
        out_specs=pl.BlockSpec((EXPERT_BLOCK, d), lambda i, be, meta: (last(i, be, meta), 0)),
        scratch_shapes=[pltpu.VMEM((d, f), BF16), pltpu.VMEM((d, f), BF16), pltpu.VMEM((f, d), BF16)])
    return pl.pallas_call(
        _moe_kernel, grid_spec=grid_spec, out_shape=jax.ShapeDtypeStruct((n_rows, d), F32),
        compiler_params=_cp(("arbitrary",)), name="moe",
    )(block_e, meta, xs, wg, wu, wd)


def _final_kernel(pos_ref, w8_ref, x1_ref, h2_ref, mod_ref, wsg_ref, wsu_ref, wsd_ref, nf_ref, ys_ref,
                  y_ref, ybuf, sem, *, tf, d):
    def issue(t, carry):
        for k in range(TOP_K):
            _row_copy(ys_ref, pos_ref[0, 0, t * TOP_K + k], ybuf.at[k], t, sem).start()
        return carry
    lax.fori_loop(0, tf, issue, 0)

    hb = h2_ref[...].astype(BF16)
    shared = _dot((_silu(_dot(hb, wsg_ref[...])) * _dot(hb, wsu_ref[...])).astype(BF16), wsd_ref[...])

    def drain(t, carry):
        for k in range(TOP_K):
            _row_copy(ys_ref, 0, ybuf.at[k], 0, sem).wait()
        return carry
    lax.fori_loop(0, tf, drain, 0)

    w8 = w8_ref[...]
    routed = ybuf[0] * w8[:, 0:1]
    for k in range(1, TOP_K):
        routed = routed + ybuf[k] * w8[:, k:k + 1]
    g2 = mod_ref[...][:, (N_MOD - 1) * d:N_MOD * d]
    x2 = x1_ref[...] + g2 * (routed + shared)
    y_ref[...] = _rms(x2, nf_ref[...])


def _final(pos_tiles, w8, x1, h2, mod, mod_spec, ys, wts, nf, tf, n_tiles, tile0, out_shape, out_spec):
    d = x1.shape[1]
    rows = lambda i: (i, 0)
    in_specs = [pl.BlockSpec((1, 1, tf * TOP_K), lambda i: (i, 0, 0), memory_space=pltpu.SMEM),
                pl.BlockSpec((tf, LANE), lambda i: (tile0 + i, 0)),
                pl.BlockSpec((tf, d), rows), pl.BlockSpec((tf, d), rows), mod_spec]
    in_specs += [_const_spec(w.shape) for w in wts]
    in_specs += [_const_spec(nf.shape), pl.BlockSpec(memory_space=pl.ANY)]
    return pl.pallas_call(
        functools.partial(_final_kernel, tf=tf, d=d), grid=(n_tiles,), in_specs=in_specs,
        out_specs=out_spec, out_shape=out_shape,
        scratch_shapes=[pltpu.VMEM((TOP_K, tf, d), F32), pltpu.SemaphoreType.DMA],
        compiler_params=_cp(("arbitrary",)), name="final",
    )(pos_tiles, w8, x1, h2, mod, *wts, nf, ys)


def kernel(x_prompt, x_sample, c_prompt, c_sample, state_pool, state_gla, w_ada, b_ada, norm1, w_in,
           pool_w, pool_scale, gla_w_alpha, gla_b_alpha, gla_norm, w_branch_a, w_branch_b, w_out, norm2,
           w_router, router_bias, w_exp_gate, w_exp_up, w_exp_down, w_sh_gate, w_sh_up, w_sh_down, norm_f):
    depth = w_in.shape[0]
    assert depth == 1, "single-layer trunk"
    bp, seq, d = x_prompt.shape
    bs, t_s, _ = x_sample.shape
    _, _, past, pw = state_pool.shape
    _, _, heads, dk, dv = state_gla.shape
    rank = gla_w_alpha.shape[1]
    n_exp = w_router.shape[2]
    assert past == POOL_HIST - 1 and pw == len(POOL_WINDOWS) * LANE and dv == LANE and rank <= LANE
    dm = _Dims(d, pw, heads, dk, dv)
    n_p, n_s = bp * seq, bs * t_s
    n_all = n_p + n_s

    off = [0]
    for sz in (pw, dm.hk, dm.hk, dm.gw, dm.gw, rank, d, d):
        off.append(off[-1] + sz)
    wi = w_in[0]
    win = jnp.concatenate([wi[:, :off[5]], wi[:, off[6]:], wi[:, off[5]:off[6]],
                           jnp.zeros((d, LANE - rank), F32)], axis=1).astype(BF16)
    wal = jnp.concatenate([gla_w_alpha[0], jnp.zeros((LANE - rank, dm.hk), F32)], axis=0).astype(BF16)
    bal = gla_b_alpha[0].reshape(1, dm.hk)
    wr = w_router[0]
    wrh = wr.astype(BF16)
    wrl = (wr - wrh.astype(F32)).astype(BF16)
    post_w = (pool_w[0].astype(BF16), pool_scale[0].reshape(1, pw), gla_norm[0].reshape(1, dm.gw),
              w_branch_a[0].astype(BF16), w_branch_b[0].astype(BF16), w_out[0].astype(BF16),
              norm2[0].reshape(1, d), wrh, wrl)
    n1 = norm1[0].reshape(1, d)

    mod = _ada(jnp.concatenate([c_prompt, c_sample], axis=0), w_ada[0], b_ada[0])
    mod_p = mod[:bp].reshape(bp, 1, N_MOD * d)
    mod_s = mod[bp:]

    x1_p, h2_p, lg_p, npool_p, ngla_p = _mix_prompt(
        dm, x_prompt.reshape(n_p, d), mod_p, (n1, win, wal, bal) + post_w, bp, seq, n_exp)

    xs2d = x_sample.reshape(bs, t_s * d)
    z_s = _sample_in(dm, xs2d, mod_s, n1, win, bs, t_s)
    mixed_s, o_s, npool_s, ngla_s = _sample_state(
        dm, z_s, state_pool[0].reshape(bs, past * pw), state_gla[0].reshape(bs, dm.hk * dv), wal, bal, bs, t_s)
    x1_s, h2_s, lg_s = _sample_post(dm, xs2d, mod_s, z_s, mixed_s, o_s, post_w, bs, t_s, n_exp)

    h2_all = jnp.concatenate([h2_p, h2_s], axis=0)
    logits = jnp.concatenate([lg_p, lg_s], axis=0)
    tr = 512 if n_all % 512 == 0 else LANE
    idx8, w8, counts = _route(logits, router_bias[0], tr)
    nblk = (n_all * TOP_K) // EXPERT_BLOCK + n_exp
    pos8, be2d, meta2d = _plan(idx8, counts, tr, nblk)
    block_e = be2d[:nblk, 0]
    meta = meta2d[0, 0:1]
    posflat = pos8[:, :TOP_K].reshape(n_all * TOP_K)

    td = 256 if n_all % 256 == 0 else LANE
    xs = _dispatch(h2_all, posflat.reshape(n_all // td, 1, td * TOP_K), nblk * EXPERT_BLOCK, td)
    ys = _moe(block_e, meta, xs, w_exp_gate[0], w_exp_up[0], w_exp_down[0], nblk)

    sh_w = (w_sh_gate[0].astype(BF16), w_sh_up[0].astype(BF16), w_sh_down[0].astype(BF16))
    nf = norm_f.reshape(1, d)
    tf = 256 if seq % 256 == 0 else seq
    n_tp = seq // tf
    y_p = _final(
        posflat[:n_p * TOP_K].reshape(n_p // tf, 1, tf * TOP_K), w8, x1_p, h2_p, mod_p,
        pl.BlockSpec((None, 1, N_MOD * d), lambda i: (i // n_tp, 0, 0)), ys, sh_w, nf, tf, n_p // tf, 0,
        jax.ShapeDtypeStruct((n_p, d), F32), pl.BlockSpec((tf, d), lambda i: (i, 0)))
    assert n_p % bs == 0
    y_s = _final(
        posflat[n_p * TOP_K:].reshape(t_s, 1, bs * TOP_K), w8, x1_s, h2_s, mod_s,
        _const_spec(mod_s.shape), ys, sh_w, nf, bs, t_s, n_p // bs,
        jax.ShapeDtypeStruct((bs, t_s * d), F32), pl.BlockSpec((bs, d), lambda i: (0, i)))

    return (y_p.reshape(bp, seq, d), y_s.reshape(bs, t_s, d),
            npool_p.reshape(depth, bp, past, pw), ngla_p.reshape(depth, bp, heads, dk, dv),
            npool_s.reshape(depth, bs, past, pw), ngla_s.reshape(depth, bs, heads, dk, dv))
```

```python
import functools

import jax
import jax.numpy as jnp
from jax import lax
from jax.experimental import pallas as pl
from jax.experimental.pallas import tpu as pltpu

F32 = jnp.float32
BF16 = jnp.bfloat16
I32 = jnp.int32

EPS = 1e-6
N_MOD = 6
POOL_WINDOWS = (2, 4, 8, 16)
POOL_HIST = 16
GLA_GATE_TEMP = 16.0
TOP_K = 8
ROUTED_SCALE = 2.5
PAST_LEN = 16384
EXPERT_BLOCK = 128
LANE = 128
GLA_FAST_MAX_DECAY = 40.0
VMEM_LIMIT = 56 * 1024 * 1024


def _cp(sem, vmem=VMEM_LIMIT):
    return pltpu.CompilerParams(dimension_semantics=sem, vmem_limit_bytes=vmem)


def _rms(x, g):
    return x * lax.rsqrt(jnp.mean(x * x, axis=-1, keepdims=True) + EPS) * g


def _silu(x):
    return x * jax.nn.sigmoid(x)


def _dot(a, b):
    return jnp.dot(a, b, preferred_element_type=F32)


def _split3(a):
    a0 = a.astype(BF16)
    r1 = a - a0.astype(F32)
    a1 = r1.astype(BF16)
    a2 = (r1 - a1.astype(F32)).astype(BF16)
    return a0, a1, a2


def _mod_parts(mod, d):
    return [mod[:, i * d:(i + 1) * d] for i in range(N_MOD)]


def _ada_kernel(c_ref, w_ref, b_ref, o_ref):
    a = _silu(c_ref[...]).astype(BF16)
    o_ref[...] = _dot(a, w_ref[...].astype(BF16)) + b_ref[...]


def _ada(c_all, w_ada, b_ada):
    n, d = c_all.shape
    cols = w_ada.shape[1]
    bc = d
    return pl.pallas_call(
        _ada_kernel,
        grid=(cols // bc,),
        in_specs=[pl.BlockSpec((n, d), lambda i: (0, 0)),
                  pl.BlockSpec((d, bc), lambda i: (0, i)),
                  pl.BlockSpec((1, bc), lambda i: (0, i))],
        out_specs=pl.BlockSpec((n, bc), lambda i: (0, i)),
        out_shape=jax.ShapeDtypeStruct((n, cols), F32),
        compiler_params=_cp(("arbitrary",)),
        name="ada",
    )(c_all, w_ada, b_ada.reshape(1, cols))


class _Dims:
    def __init__(self, d, pw, heads, dk, dv):
        self.d, self.pw, self.h, self.dk, self.dv = d, pw, heads, dk, dv
        self.hk, self.gw = heads * dk, heads * dv
        self.q0 = pw
        self.k0 = self.q0 + self.hk
        self.v0 = self.k0 + self.hk
        self.go0 = self.v0 + self.gw
        self.ga0 = self.go0 + self.gw
        self.gb0 = self.ga0 + d
        self.al0 = self.gb0 + d
        self.zw = self.al0 + LANE


def _log_decay(z_al, wal_ref, bal_ref):
    xal = _dot(z_al.astype(BF16), wal_ref[...]) + bal_ref[...]
    return jax.nn.log_sigmoid(xal) / GLA_GATE_TEMP


def _pool_project(mixed_groups, pw_ref, ps_ref):
    ys = [_dot(m.astype(BF16), pw_ref[g]) for g, m in enumerate(mixed_groups)]
    return jnp.concatenate(ys, axis=1) * ps_ref[...]


def _post_mix(dm, x, mod, ya, o, go, ga, gb, gn_ref, wa_ref, wb_ref, wo_ref, n2_ref, wrh_ref, wrl_ref):
    _, _, g1, sh2, sc2, _ = _mod_parts(mod, dm.d)
    parts = []
    for h in range(dm.h):
        oh = o[:, h * dm.dv:(h + 1) * dm.dv]
        parts.append(oh * lax.rsqrt(jnp.mean(oh * oh, axis=-1, keepdims=True) + EPS))
    yb = jnp.concatenate(parts, axis=1) * gn_ref[...] * _silu(go)
    m = (jax.nn.sigmoid(ga) * _dot(ya.astype(BF16), wa_ref[...])
         + jax.nn.sigmoid(gb) * _dot(yb.astype(BF16), wb_ref[...]))
    x1 = x + g1 * _dot(m.astype(BF16), wo_ref[...])
    h2 = _rms(x1, n2_ref[...]) * (1 + sc2) + sh2
    hi = h2.astype(BF16)
    lo = (h2 - hi.astype(F32)).astype(BF16)
    logits = _dot(hi, wrh_ref[...]) + (_dot(hi, wrl_ref[...]) + _dot(lo, wrh_ref[...]))
    return x1, h2, logits


def _mixp_kernel(x_ref, mod_ref, n1_ref, win_ref, wal_ref, bal_ref, pw_ref, ps_ref, gn_ref,
                 wa_ref, wb_ref, wo_ref, n2_ref, wrh_ref, wrl_ref,
                 x1_ref, h2_ref, lg_ref, np_ref, ng_ref,
                 z_ref, uext_ref, s_ref, b_ref, oi_ref, *, dm, tt, n_t):
    j = pl.program_id(1)
    x = x_ref[...]
    mod = mod_ref[...]
    sh1, sc1 = mod[:, 0:dm.d], mod[:, dm.d:2 * dm.d]
    h = _rms(x, n1_ref[...]) * (1 + sc1) + sh1
    z_ref[...] = _dot(h.astype(BF16), win_ref[...])

    @pl.when(j == 0)
    def _():
        uext_ref[0:POOL_HIST, :] = jnp.zeros((POOL_HIST, dm.pw), F32)
        s_ref[...] = jnp.zeros(s_ref.shape, F32)

    uext_ref[POOL_HIST:POOL_HIST + tt, :] = z_ref[:, 0:dm.pw]
    pos = j * tt + lax.broadcasted_iota(I32, (tt, 1), 0)
    gw = dm.pw // len(POOL_WINDOWS)
    mixed = []
    for g, w in enumerate(POOL_WINDOWS):
        c0 = g * gw
        cur = uext_ref[POOL_HIST:POOL_HIST + tt, c0:c0 + gw]
        acc = cur
        for i in range(1, w):
            acc = acc + uext_ref[POOL_HIST - i:POOL_HIST - i + tt, c0:c0 + gw]
        cnt = jnp.minimum(w, pos + 1).astype(F32)
        mixed.append(acc / cnt - cur)
    ya = _pool_project(mixed, pw_ref, ps_ref)

    @pl.when(j == n_t - 1)
    def _():
        np_ref[...] = uext_ref[tt + 1:tt + POOL_HIST, :]

    uext_ref[0:POOL_HIST, :] = uext_ref[tt:tt + POOL_HIST, :]

    la = _log_decay(z_ref[:, dm.al0:dm.al0 + LANE], wal_ref, bal_ref)
    r_i = lax.broadcasted_iota(I32, (tt, tt), 0)
    c_i = lax.broadcasted_iota(I32, (tt, tt), 1)
    causal = r_i >= c_i
    tri = jnp.where(causal, 1.0, 0.0).astype(BF16)
    a0, a1, a2 = _split3(la)
    bc = _dot(tri, a0) + (_dot(tri, a1) + _dot(tri, a2))
    b_ref[...] = bc
    q = z_ref[:, dm.q0:dm.q0 + dm.hk]
    k = z_ref[:, dm.k0:dm.k0 + dm.hk]
    v = z_ref[:, dm.v0:dm.v0 + dm.gw]
    scale = dm.dk ** -0.5
    qe = (q * scale) * jnp.exp(bc)
    blast = bc[tt - 1:tt, :]
    kd = k * jnp.exp(blast - bc)
    s_old = s_ref[...]
    s_bf = s_old.astype(BF16)
    vb = v.astype(BF16)
    lane = lax.broadcasted_iota(I32, (1, dm.hk), 1)
    head_masks = [(lane >= h * dm.dk) & (lane < (h + 1) * dm.dk) for h in range(dm.h)]
    o_inter = [_dot(jnp.where(head_masks[h], qe, 0.0).astype(BF16), s_bf) for h in range(dm.h)]

    fast = jnp.max(-blast) <= GLA_FAST_MAX_DECAY

    @pl.when(fast)
    def _():
        ke = (k * jnp.exp(-bc)).astype(BF16)
        for h in range(dm.h):
            qh = jnp.where(head_masks[h], qe, 0.0).astype(BF16)
            sc = lax.dot_general(qh, ke, (((1,), (1,)), ((), ())), preferred_element_type=F32)
            sc = jnp.where(causal, sc, 0.0).astype(BF16)
            oi_ref[:, h * dm.dv:(h + 1) * dm.dv] = _dot(sc, vb[:, h * dm.dv:(h + 1) * dm.dv])

    @pl.when(jnp.logical_not(fast))
    def _():
        rows = lax.broadcasted_iota(I32, (tt, 1), 0)
        for h in range(dm.h):
            def body(tb, carry, h=h):
                base = pl.multiple_of(tb * 8, 8)
                q8 = z_ref[pl.ds(base, 8), dm.q0:dm.q0 + dm.hk] * scale
                b8 = b_ref[pl.ds(base, 8), :]
                vh = z_ref[:, dm.v0 + h * dm.dv:dm.v0 + (h + 1) * dm.dv]
                out_rows = []
                for r in range(8):
                    keep = (rows <= base + r) & head_masks[h]
                    dec = jnp.exp(jnp.where(keep, b8[r:r + 1, :] - b_ref[...], -jnp.inf))
                    wgt = (q8[r:r + 1, :] * dec) * z_ref[:, dm.k0:dm.k0 + dm.hk]
                    s_col = jnp.sum(wgt, axis=-1, keepdims=True)
                    out_rows.append(jnp.sum(s_col * vh, axis=0, keepdims=True))
                oi_ref[pl.ds(base, 8), h * dm.dv:(h + 1) * dm.dv] = jnp.concatenate(out_rows, axis=0)
                return carry
            lax.fori_loop(0, tt // 8, body, 0)

    o = jnp.concatenate(o_inter, axis=1) + oi_ref[...]

    u_all = _dot(kd.T.astype(BF16), vb)
    dec_col = jnp.broadcast_to(jnp.exp(blast), (dm.hk, dm.hk)).T[:, 0:dm.dv]
    upd = jnp.concatenate(
        [u_all[h * dm.dk:(h + 1) * dm.dk, h * dm.dv:(h + 1) * dm.dv] for h in range(dm.h)], axis=0)
    s_new = dec_col * s_old + upd
    s_ref[...] = s_new

    @pl.when(j == n_t - 1)
    def _():
        ng_ref[...] = s_new

    go = z_ref[:, dm.go0:dm.go0 + dm.gw]
    ga = z_ref[:, dm.ga0:dm.ga0 + dm.d]
    gb = z_ref[:, dm.gb0:dm.gb0 + dm.d]
    x1, h2, logits = _post_mix(dm, x, mod, ya, o, go, ga, gb, gn_ref, wa_ref, wb_ref, wo_ref,
                               n2_ref, wrh_ref, wrl_ref)
    x1_ref[...] = x1
    h2_ref[...] = h2
    lg_ref[...] = logits


def _const_spec(shape):
    nd = len(shape)
    return pl.BlockSpec(shape, lambda *_: (0,) * nd, pipeline_mode=pl.Buffered(1))


def _mix_prompt(dm, x2d, mod3, wts, bsz, seq, n_exp):
    tt = 256 if seq % 256 == 0 else seq
    n_t = seq // tt
    n = bsz * seq
    d = dm.d
    kern = functools.partial(_mixp_kernel, dm=dm, tt=tt, n_t=n_t)
    row = lambda b, j: (b * n_t + j, 0)
    in_specs = [pl.BlockSpec((tt, d), row),
                pl.BlockSpec((None, 1, N_MOD * d), lambda b, j: (b, 0, 0))]
    in_specs += [_const_spec(w.shape) for w in wts]
    out_specs = [pl.BlockSpec((tt, d), row), pl.BlockSpec((tt, d), row), pl.BlockSpec((tt, n_exp), row),
                 pl.BlockSpec((None, POOL_HIST - 1, dm.pw), lambda b, j: (b, 0, 0)),
                 pl.BlockSpec((None, dm.hk, dm.dv), lambda b, j: (b, 0, 0))]
    out_shape = [jax.ShapeDtypeStruct((n, d), F32), jax.ShapeDtypeStruct((n, d), F32),
                 jax.ShapeDtypeStruct((n, n_exp), F32),
                 jax.ShapeDtypeStruct((bsz, POOL_HIST - 1, dm.pw), F32),
                 jax.ShapeDtypeStruct((bsz, dm.hk, dm.dv), F32)]
    scratch = [pltpu.VMEM((tt, dm.zw), F32), pltpu.VMEM((tt + POOL_HIST, dm.pw), F32),
               pltpu.VMEM((dm.hk, dm.dv), F32), pltpu.VMEM((tt, dm.hk), F32), pltpu.VMEM((tt, dm.gw), F32)]
    return pl.pallas_call(
        kern, grid=(bsz, n_t), in_specs=in_specs, out_specs=out_specs, out_shape=out_shape,
        scratch_shapes=scratch, compiler_params=_cp(("arbitrary", "arbitrary")), name="mixp",
    )(x2d, mod3, *wts)


def _sin_kernel(x_ref, mod_ref, n1_ref, win_ref, z_ref, *, d):
    mod = mod_ref[...]
    h = _rms(x_ref[...], n1_ref[...]) * (1 + mod[:, d:2 * d]) + mod[:, 0:d]
    z_ref[...] = _dot(h.astype(BF16), win_ref[...])


def _sample_in(dm, xs2d, mod_s, n1, win, bs, t_s):
    d = dm.d
    return pl.pallas_call(
        functools.partial(_sin_kernel, d=d),
        grid=(t_s,),
        in_specs=[pl.BlockSpec((bs, d), lambda t: (0, t)),
                  _const_spec(mod_s.shape), _const_spec(n1.shape), _const_spec(win.shape)],
        out_specs=pl.BlockSpec((bs, dm.zw), lambda t: (t, 0)),
        out_shape=jax.ShapeDtypeStruct((t_s * bs, dm.zw), F32),
        compiler_params=_cp(("arbitrary",)), name="sample_in",
    )(xs2d, mod_s, n1, win)


def _sst_kernel(z_ref, pool_ref, gs_ref, wal_ref, bal_ref,
                mixed_ref, o_ref, npool_ref, ngs_ref,
                qt_ref, kt_ref, dt_ref, vt_ref, ot_ref, *, dm, bs, t_s, ch, pos0):
    i = pl.program_id(0)
    n_i = pl.num_programs(0)
    scale = dm.dk ** -0.5
    past = POOL_HIST - 1

    @pl.when(i == 0)
    def _():
        for t in range(t_s):
            r0 = t * bs
            la = _log_decay(z_ref[r0:r0 + bs, dm.al0:dm.al0 + LANE], wal_ref, bal_ref)
            dt_ref[t] = jnp.exp(la).T
            qt_ref[t] = (z_ref[r0:r0 + bs, dm.q0:dm.q0 + dm.hk] * scale).T
            kt_ref[t] = z_ref[r0:r0 + bs, dm.k0:dm.k0 + dm.hk].T
            vt_ref[t] = z_ref[r0:r0 + bs, dm.v0:dm.v0 + dm.gw].T
        ot_ref[...] = jnp.zeros(ot_ref.shape, F32)

        def item(s, c0, c1):
            if s >= 0:
                return z_ref[s * bs:(s + 1) * bs, c0:c1]
            return pool_ref[:, (past + s) * dm.pw + c0:(past + s) * dm.pw + c1]

        gw = dm.pw // len(POOL_WINDOWS)
        for t in range(t_s):
            for g, w in enumerate(POOL_WINDOWS):
                c0, c1 = g * gw, (g + 1) * gw
                cur = item(t, c0, c1)
                acc = cur
                for r in range(1, w):
                    acc = acc + item(t - r, c0, c1)
                cnt = float(min(w, pos0 + t + 1))
                mixed_ref[t * bs:(t + 1) * bs, c0:c1] = acc / cnt - cur
        for r in range(past):
            npool_ref[:, r * dm.pw:(r + 1) * dm.pw] = item(r - past + t_s, 0, dm.pw)

    per_head = dm.dk // ch
    hoff = pl.multiple_of((i // per_head) * dm.dv, dm.dv)
    for p in range(ch):
        hd = i * ch + p
        st = gs_ref[:, p * dm.dv:(p + 1) * dm.dv].T
        for t in range(t_s):
            st = dt_ref[t, pl.ds(hd, 1), :] * st + kt_ref[t, pl.ds(hd, 1), :] * vt_ref[t, pl.ds(hoff, dm.dv), :]
            ot_ref[t, pl.ds(hoff, dm.dv), :] += qt_ref[t, pl.ds(hd, 1), :] * st
        ngs_ref[:, p * dm.dv:(p + 1) * dm.dv] = st.T

    @pl.when(i == n_i - 1)
    def _():
        for t in range(t_s):
            for h in range(dm.h):
                o_ref[t * bs:(t + 1) * bs, h * dm.dv:(h + 1) * dm.dv] = ot_ref[t, h * dm.dv:(h + 1) * dm.dv, :].T


def _sample_state(dm, z_s, pool2d, gs2d, wal, bal, bs, t_s):
    ch = 16
    n_i = dm.hk // ch
    past = POOL_HIST - 1
    kern = functools.partial(_sst_kernel, dm=dm, bs=bs, t_s=t_s, ch=ch, pos0=PAST_LEN)
    full = lambda shape: pl.BlockSpec(shape, lambda i: (0,) * len(shape))
    return pl.pallas_call(
        kern, grid=(n_i,),
        in_specs=[_const_spec(z_s.shape), _const_spec(pool2d.shape),
                  pl.BlockSpec((bs, ch * dm.dv), lambda i: (0, i)),
                  _const_spec(wal.shape), _const_spec(bal.shape)],
        out_specs=[full((t_s * bs, dm.pw)), full((t_s * bs, dm.gw)), full((bs, past * dm.pw)),
                   pl.BlockSpec((bs, ch * dm.dv), lambda i: (0, i))],
        out_shape=[jax.ShapeDtypeStruct((t_s * bs, dm.pw), F32), jax.ShapeDtypeStruct((t_s * bs, dm.gw), F32),
                   jax.ShapeDtypeStruct((bs, past * dm.pw), F32), jax.ShapeDtypeStruct(gs2d.shape, F32)],
        scratch_shapes=[pltpu.VMEM((t_s, dm.hk, bs), F32), pltpu.VMEM((t_s, dm.hk, bs), F32),
                        pltpu.VMEM((t_s, dm.hk, bs), F32), pltpu.VMEM((t_s, dm.gw, bs), F32),
                        pltpu.VMEM((t_s, dm.gw, bs), F32)],
        compiler_params=_cp(("arbitrary",)), name="sample_state",
    )(z_s, pool2d, gs2d, wal, bal)


def _spost_kernel(x_ref, mod_ref, z_ref, mixed_ref, o_ref, pw_ref, ps_ref, gn_ref,
                  wa_ref, wb_ref, wo_ref, n2_ref, wrh_ref, wrl_ref,
                  x1_ref, h2_ref, lg_ref, *, dm):
    gw = dm.pw // len(POOL_WINDOWS)
    ya = _pool_project([mixed_ref[:, g * gw:(g + 1) * gw] for g in range(len(POOL_WINDOWS))], pw_ref, ps_ref)
    x1, h2, logits = _post_mix(
        dm, x_ref[...], mod_ref[...], ya, o_ref[...],
        z_ref[:, dm.go0:dm.go0 + dm.gw], z_ref[:, dm.ga0:dm.ga0 + dm.d], z_ref[:, dm.gb0:dm.gb0 + dm.d],
        gn_ref, wa_ref, wb_ref, wo_ref, n2_ref, wrh_ref, wrl_ref)
    x1_ref[...] = x1
    h2_ref[...] = h2
    lg_ref[...] = logits


def _sample_post(dm, xs2d, mod_s, z_s, mixed_s, o_s, wts, bs, t_s, n_exp):
    d = dm.d
    rows = lambda t: (t, 0)
    in_specs = [pl.BlockSpec((bs, d), lambda t: (0, t)), _const_spec(mod_s.shape),
                pl.BlockSpec((bs, dm.zw), rows), pl.BlockSpec((bs, dm.pw), rows), pl.BlockSpec((bs, dm.gw), rows)]
    in_specs += [_const_spec(w.shape) for w in wts]
    return pl.pallas_call(
        functools.partial(_spost_kernel, dm=dm), grid=(t_s,), in_specs=in_specs,
        out_specs=[pl.BlockSpec((bs, d), rows), pl.BlockSpec((bs, d), rows), pl.BlockSpec((bs, n_exp), rows)],
        out_shape=[jax.ShapeDtypeStruct((t_s * bs, d), F32), jax.ShapeDtypeStruct((t_s * bs, d), F32),
                   jax.ShapeDtypeStruct((t_s * bs, n_exp), F32)],
        compiler_params=_cp(("arbitrary",)), name="sample_post",
    )(xs2d, mod_s, z_s, mixed_s, o_s, *wts)


def _route_kernel(lgp_ref, lgs_ref, bias_ref, idx_ref, w_ref, cnt_ref, *, n_exp, tiles_p):
    i = pl.program_id(0)
    s = jax.nn.sigmoid(jnp.where(i < tiles_p, lgp_ref[...], lgs_ref[...]))
    cur = s + bias_ref[...]
    tr = s.shape[0]
    lane = lax.broadcasted_iota(I32, (1, n_exp), 1).astype(F32)
    slot = lax.broadcasted_iota(I32, (1, LANE), 1)
    idx8 = jnp.zeros((tr, LANE), F32)
    w8 = jnp.zeros((tr, LANE), F32)
    wsum = jnp.zeros((tr, 1), F32)
    picked = jnp.zeros((tr, n_exp), F32)
    for k in range(TOP_K):
        m = jnp.max(cur, axis=-1, keepdims=True)
        am = jnp.min(jnp.where(cur == m, lane, float(n_exp)), axis=-1, keepdims=True)
        oh = lane == am
        sk = jnp.sum(jnp.where(oh, s, 0.0), axis=-1, keepdims=True)
        cur = jnp.where(oh, -jnp.inf, cur)
        idx8 = jnp.where(slot == k, am, idx8)
        w8 = jnp.where(slot == k, sk, w8)
        wsum = wsum + sk
        picked = picked + jnp.where(oh, 1.0, 0.0)
    idx_ref[...] = idx8.astype(I32)
    w_ref[...] = w8 / wsum * ROUTED_SCALE

    @pl.when(i == 0)
    def _():
        cnt_ref[...] = jnp.zeros(cnt_ref.shape, F32)

    cnt_ref[...] += jnp.sum(picked, axis=0, keepdims=True)


def _two_source_specs(block, tiles_p, tiles_s):
    return [pl.BlockSpec(block, lambda i: (jnp.minimum(i, tiles_p - 1), 0)),
            pl.BlockSpec(block, lambda i: (jnp.clip(i - tiles_p, 0, tiles_s - 1), 0))]


def _route(lg_p, lg_s, bias, tr):
    n_exp = lg_p.shape[1]
    tiles_p, tiles_s = lg_p.shape[0] // tr, lg_s.shape[0] // tr
    n = (tiles_p + tiles_s) * tr
    rows = lambda i: (i, 0)
    return pl.pallas_call(
        functools.partial(_route_kernel, n_exp=n_exp, tiles_p=tiles_p), grid=(tiles_p + tiles_s,),
        in_specs=_two_source_specs((tr, n_exp), tiles_p, tiles_s) + [_const_spec((1, n_exp))],
        out_specs=[pl.BlockSpec((tr, LANE), rows), pl.BlockSpec((tr, LANE), rows),
                   pl.BlockSpec((8, n_exp), lambda i: (0, 0))],
        out_shape=[jax.ShapeDtypeStruct((n, LANE), I32), jax.ShapeDtypeStruct((n, LANE), F32),
                   jax.ShapeDtypeStruct((8, n_exp), F32)],
        compiler_params=_cp(("arbitrary",)), name="route",
    )(lg_p, lg_s, bias.reshape(1, n_exp))


def _plan_kernel(idx_ref, cnt_ref, pos_ref, bounds_ref, start_ref, run_ref, *, n_exp):
    i = pl.program_id(0)
    lane = lax.broadcasted_iota(I32, (1, n_exp), 1)

    @pl.when(i == 0)
    def _():
        cnt = cnt_ref[...].astype(I32)
        padded = ((cnt + (EXPERT_BLOCK - 1)) // EXPERT_BLOCK) * EXPERT_BLOCK
        pe = padded
        s = 1
        while s < n_exp:
            pe = pe + jnp.where(lane >= s, pltpu.roll(pe, s, 1), 0)
            s *= 2
        start_ref[...] = (pe - padded).astype(F32)
        run_ref[...] = jnp.zeros(run_ref.shape, F32)
        row = lax.broadcasted_iota(I32, (8, n_exp), 0)
        bounds_ref[...] = jnp.where(row == 0, pe - padded, pe)

    idx = idx_ref[...]
    tr = idx.shape[0]
    onehots = [lane == idx[:, k:k + 1] for k in range(TOP_K)]
    member = onehots[0]
    for k in range(1, TOP_K):
        member = member | onehots[k]
    mb = jnp.where(member, 1.0, 0.0).astype(BF16)
    r_i = lax.broadcasted_iota(I32, (tr, tr), 0)
    c_i = lax.broadcasted_iota(I32, (tr, tr), 1)
    earlier = jnp.where(r_i > c_i, 1.0, 0.0).astype(BF16)
    dest = _dot(earlier, mb) + (start_ref[0:1, :] + run_ref[0:1, :])
    slot = lax.broadcasted_iota(I32, (1, LANE), 1)
    pos8 = jnp.zeros((tr, LANE), F32)
    for k in range(TOP_K):
        pk = jnp.sum(jnp.where(onehots[k], dest, 0.0), axis=-1, keepdims=True)
        pos8 = jnp.where(slot == k, pk, pos8)
    pos_ref[...] = pos8.T[0:TOP_K, :].astype(I32)
    run_ref[...] += jnp.sum(mb.astype(F32), axis=0, keepdims=True)


def _plan(idx8, counts, tr):
    n = idx8.shape[0]
    n_exp = counts.shape[1]
    rows = lambda i: (i, 0)
    return pl.pallas_call(
        functools.partial(_plan_kernel, n_exp=n_exp), grid=(n // tr,),
        in_specs=[pl.BlockSpec((tr, LANE), rows), _const_spec((8, n_exp))],
        out_specs=[pl.BlockSpec((TOP_K, tr), lambda i: (0, i)), pl.BlockSpec((8, n_exp), lambda i: (0, 0))],
        out_shape=[jax.ShapeDtypeStruct((TOP_K, n), I32), jax.ShapeDtypeStruct((8, n_exp), I32)],
        scratch_shapes=[pltpu.VMEM((8, n_exp), F32), pltpu.VMEM((8, n_exp), F32)],
        compiler_params=_cp(("arbitrary",)), name="plan",
    )(idx8, counts)


def _row_copy(src_ref, src_row, dst_ref, dst_row, sem):
    return pltpu.make_async_copy(src_ref.at[pl.ds(src_row, 1)], dst_ref.at[pl.ds(dst_row, 1)], sem)


def _dispatch_kernel(pos_ref, hp_ref, hs_ref, xs_ref, sem, *, td, tiles_p):
    i = pl.program_id(0)

    def scatter_rows(h_ref):
        def issue(t, carry):
            for k in range(TOP_K):
                _row_copy(h_ref, t, xs_ref, pos_ref[k, t], sem).start(priority=k % 2)
            return carry
        lax.fori_loop(0, td, issue, 0)

        def drain(t, carry):
            for k in range(TOP_K):
                _row_copy(h_ref, 0, xs_ref, 0, sem).wait()
            return carry
        lax.fori_loop(0, td, drain, 0)

    @pl.when(i < tiles_p)
    def _():
        scatter_rows(hp_ref)

    @pl.when(i >= tiles_p)
    def _():
        scatter_rows(hs_ref)


def _dispatch(h2_p, h2_s, pos_t, n_rows, td):
    d = h2_p.shape[1]
    tiles_p, tiles_s = h2_p.shape[0] // td, h2_s.shape[0] // td
    return pl.pallas_call(
        functools.partial(_dispatch_kernel, td=td, tiles_p=tiles_p), grid=(tiles_p + tiles_s,),
        in_specs=[pl.BlockSpec((TOP_K, td), lambda i: (0, i), memory_space=pltpu.SMEM)]
        + _two_source_specs((td, d), tiles_p, tiles_s),
        out_specs=pl.BlockSpec(memory_space=pl.ANY),
        out_shape=jax.ShapeDtypeStruct((n_rows, d), F32),
        scratch_shapes=[pltpu.SemaphoreType.DMA],
        compiler_params=_cp(("arbitrary",)), name="dispatch",
    )(pos_t, h2_p, h2_s)


def _moe_kernel(st_ref, wg_ref, wu_ref, wd_ref, xs_ref, ys_ref, xbuf, ybuf, wgb, wub, wdb, semx, semy):
    e = pl.program_id(0)
    r0 = st_ref[e]
    nchunks = (st_ref[e + 1] - r0) // EXPERT_BLOCK

    def x_copy(c, slot):
        rows = pl.ds(pl.multiple_of(r0 + c * EXPERT_BLOCK, EXPERT_BLOCK), EXPERT_BLOCK)
        return pltpu.make_async_copy(xs_ref.at[rows], xbuf.at[slot], semx.at[slot])

    def y_copy(c, slot):
        rows = pl.ds(pl.multiple_of(r0 + c * EXPERT_BLOCK, EXPERT_BLOCK), EXPERT_BLOCK)
        return pltpu.make_async_copy(ybuf.at[slot], ys_ref.at[rows], semy.at[slot])

    @pl.when(nchunks > 0)
    def _():
        x_copy(0, 0).start()

    wgb[...] = wg_ref[...].astype(BF16)
    wub[...] = wu_ref[...].astype(BF16)
    wdb[...] = wd_ref[...].astype(BF16)

    def body(c, carry):
        slot = c % 2

        @pl.when(c + 1 < nchunks)
        def _():
            x_copy(c + 1, 1 - slot).start()

        x_copy(c, slot).wait()

        @pl.when(c >= 2)
        def _():
            y_copy(c - 2, slot).wait()

        x = xbuf[slot].astype(BF16)
        mid = _silu(_dot(x, wgb[...])) * _dot(x, wub[...])
        ybuf[slot] = _dot(mid.astype(BF16), wdb[...])
        y_copy(c, slot).start()
        return carry

    lax.fori_loop(0, nchunks, body, 0)

    @pl.when(nchunks >= 2)
    def _():
        y_copy(nchunks - 2, nchunks % 2).wait()

    @pl.when(nchunks >= 1)
    def _():
        y_copy(nchunks - 1, (nchunks - 1) % 2).wait()


def _moe(starts, xs, wg, wu, wd):
    n_rows, d = xs.shape
    n_exp, _, f = wg.shape
    grid_spec = pltpu.PrefetchScalarGridSpec(
        num_scalar_prefetch=1, grid=(n_exp,),
        in_specs=[pl.BlockSpec((None, d, f), lambda e, st: (e, 0, 0)),
                  pl.BlockSpec((None, d, f), lambda e, st: (e, 0, 0)),
                  pl.BlockSpec((None, f, d), lambda e, st: (e, 0, 0)),
                  pl.BlockSpec(memory_space=pl.ANY)],
        out_specs=pl.BlockSpec(memory_space=pl.ANY),
        scratch_shapes=[pltpu.VMEM((2, EXPERT_BLOCK, d), F32), pltpu.VMEM((2, EXPERT_BLOCK, d), F32),
                        pltpu.VMEM((d, f), BF16), pltpu.VMEM((d, f), BF16), pltpu.VMEM((f, d), BF16),
                        pltpu.SemaphoreType.DMA((2,)), pltpu.SemaphoreType.DMA((2,))])
    return pl.pallas_call(
        _moe_kernel, grid_spec=grid_spec, out_shape=jax.ShapeDtypeStruct((n_rows, d), F32),
        compiler_params=_cp(("arbitrary",)), name="moe",
    )(starts, wg, wu, wd, xs)


def _final_kernel(pos_ref, w8_ref, x1_ref, h2_ref, mod_ref, wsg_ref, wsu_ref, wsd_ref, nf_ref, ys_ref,
                  y_ref, ybuf, sem, *, tf, d):
    def issue(t, carry):
        for k in range(TOP_K):
            _row_copy(ys_ref, pos_ref[k, t], ybuf.at[k], t, sem).start(priority=k % 2)
        return carry
    lax.fori_loop(0, tf, issue, 0)

    hb = h2_ref[...].astype(BF16)
    shared = _dot((_silu(_dot(hb, wsg_ref[...])) * _dot(hb, wsu_ref[...])).astype(BF16), wsd_ref[...])

    def drain(t, carry):
        for k in range(TOP_K):
            _row_copy(ys_ref, 0, ybuf.at[k], 0, sem).wait()
        return carry
    lax.fori_loop(0, tf, drain, 0)

    w8 = w8_ref[...]
    routed = ybuf[0] * w8[:, 0:1]
    for k in range(1, TOP_K):
        routed = routed + ybuf[k] * w8[:, k:k + 1]
    g2 = mod_ref[...][:, (N_MOD - 1) * d:N_MOD * d]
    x2 = x1_ref[...] + g2 * (routed + shared)
    y_ref[...] = _rms(x2, nf_ref[...])


def _final(pos_t, w8, x1, h2, mod, mod_spec, ys, wts, nf, tf, n_tiles, tile0, out_shape, out_spec):
    d = x1.shape[1]
    rows = lambda i: (i, 0)
    in_specs = [pl.BlockSpec((TOP_K, tf), lambda i: (0, tile0 + i), memory_space=pltpu.SMEM),
                pl.BlockSpec((tf, LANE), lambda i: (tile0 + i, 0)),
                pl.BlockSpec((tf, d), rows), pl.BlockSpec((tf, d), rows), mod_spec]
    in_specs += [_const_spec(w.shape) for w in wts]
    in_specs += [_const_spec(nf.shape), pl.BlockSpec(memory_space=pl.ANY)]
    return pl.pallas_call(
        functools.partial(_final_kernel, tf=tf, d=d), grid=(n_tiles,), in_specs=in_specs,
        out_specs=out_spec, out_shape=out_shape,
        scratch_shapes=[pltpu.VMEM((TOP_K, tf, d), F32), pltpu.SemaphoreType.DMA],
        compiler_params=_cp(("arbitrary",)), name="final",
    )(pos_t, w8, x1, h2, mod, *wts, nf, ys)


def kernel(x_prompt, x_sample, c_prompt, c_sample, state_pool, state_gla, w_ada, b_ada, norm1, w_in,
           pool_w, pool_scale, gla_w_alpha, gla_b_alpha, gla_norm, w_branch_a, w_branch_b, w_out, norm2,
           w_router, router_bias, w_exp_gate, w_exp_up, w_exp_down, w_sh_gate, w_sh_up, w_sh_down, norm_f):
    depth = w_in.shape[0]
    assert depth == 1, "single-layer trunk"
    bp, seq, d = x_prompt.shape
    bs, t_s, _ = x_sample.shape
    _, _, past, pw = state_pool.shape
    _, _, heads, dk, dv = state_gla.shape
    rank = gla_w_alpha.shape[1]
    n_exp = w_router.shape[2]
    assert past == POOL_HIST - 1 and pw == len(POOL_WINDOWS) * LANE and dv == LANE and rank <= LANE
    dm = _Dims(d, pw, heads, dk, dv)
    n_p, n_s = bp * seq, bs * t_s
    n_all = n_p + n_s

    off = [0]
    for sz in (pw, dm.hk, dm.hk, dm.gw, dm.gw, rank, d, d):
        off.append(off[-1] + sz)
    wi = w_in[0]
    win = jnp.concatenate([wi[:, :off[5]], wi[:, off[6]:], wi[:, off[5]:off[6]],
                           jnp.zeros((d, LANE - rank), F32)], axis=1).astype(BF16)
    wal = jnp.concatenate([gla_w_alpha[0], jnp.zeros((LANE - rank, dm.hk), F32)], axis=0).astype(BF16)
    bal = gla_b_alpha[0].reshape(1, dm.hk)
    wr = w_router[0]
    wrh = wr.astype(BF16)
    wrl = (wr - wrh.astype(F32)).astype(BF16)
    post_w = (pool_w[0].astype(BF16), pool_scale[0].reshape(1, pw), gla_norm[0].reshape(1, dm.gw),
              w_branch_a[0].astype(BF16), w_branch_b[0].astype(BF16), w_out[0].astype(BF16),
              norm2[0].reshape(1, d), wrh, wrl)
    n1 = norm1[0].reshape(1, d)

    mod = _ada(jnp.concatenate([c_prompt, c_sample], axis=0), w_ada[0], b_ada[0])
    mod_p = mod[:bp].reshape(bp, 1, N_MOD * d)
    mod_s = mod[bp:]

    x1_p, h2_p, lg_p, npool_p, ngla_p = _mix_prompt(
        dm, x_prompt.reshape(n_p, d), mod_p, (n1, win, wal, bal) + post_w, bp, seq, n_exp)

    xs2d = x_sample.reshape(bs, t_s * d)
    z_s = _sample_in(dm, xs2d, mod_s, n1, win, bs, t_s)
    mixed_s, o_s, npool_s, ngla_s = _sample_state(
        dm, z_s, state_pool[0].reshape(bs, past * pw), state_gla[0].reshape(bs, dm.hk * dv), wal, bal, bs, t_s)
    x1_s, h2_s, lg_s = _sample_post(dm, xs2d, mod_s, z_s, mixed_s, o_s, post_w, bs, t_s, n_exp)

    tr = 512 if (n_p % 512 == 0 and n_s % 512 == 0) else LANE
    assert n_p % tr == 0 and n_s % tr == 0 and n_p % bs == 0
    idx8, w8, counts = _route(lg_p, lg_s, router_bias[0], tr)
    nblk = (n_all * TOP_K) // EXPERT_BLOCK + n_exp
    pos_t, bounds = _plan(idx8, counts, tr)
    starts = jnp.concatenate([bounds[0], bounds[1, n_exp - 1:]])

    td = 256 if (n_p % 256 == 0 and n_s % 256 == 0) else LANE
    xs = _dispatch(h2_p, h2_s, pos_t, nblk * EXPERT_BLOCK, td)
    ys = _moe(starts, xs, w_exp_gate[0], w_exp_up[0], w_exp_down[0])

    sh_w = (w_sh_gate[0].astype(BF16), w_sh_up[0].astype(BF16), w_sh_down[0].astype(BF16))
    nf = norm_f.reshape(1, d)
    tf = 256 if seq % 256 == 0 else seq
    n_tp = seq // tf
    y_p = _final(
        pos_t, w8, x1_p, h2_p, mod_p,
        pl.BlockSpec((None, 1, N_MOD * d), lambda i: (i // n_tp, 0, 0)), ys, sh_w, nf, tf, n_p // tf, 0,
        jax.ShapeDtypeStruct((n_p, d), F32), pl.BlockSpec((tf, d), lambda i: (i, 0)))
    y_s = _final(
        pos_t, w8, x1_s, h2_s, mod_s,
        _const_spec(mod_s.shape), ys, sh_w, nf, bs, t_s, n_p // bs,
        jax.ShapeDtypeStruct((bs, t_s * d), F32), pl.BlockSpec((bs, d), lambda i: (0, i)))

    return (y_p.reshape(bp, seq, d), y_s.reshape(bs, t_s, d),
            npool_p.reshape(depth, bp, past, pw), ngla_p.reshape(depth, bp, heads, dk, dv),
            npool_s.reshape(depth, bs, past, pw), ngla_s.reshape(depth, bs, heads, dk, dv))
```

```python
import functools

import jax
import jax.numpy as jnp
from jax import lax
from jax.experimental import pallas as pl
from jax.experimental.pallas import tpu as pltpu

F32 = jnp.float32
BF16 = jnp.bfloat16
I32 = jnp.int32

EPS = 1e-6
N_MOD = 6
POOL_WINDOWS = (2, 4, 8, 16)
POOL_HIST = 16
GLA_GATE_TEMP = 16.0
TOP_K = 8
ROUTED_SCALE = 2.5
PAST_LEN = 16384
EXPERT_BLOCK = 128
LANE = 128
GLA_FAST_MAX_DECAY = 40.0
VMEM_LIMIT = 56 * 1024 * 1024


def _cp(sem, vmem=VMEM_LIMIT):
    return pltpu.CompilerParams(dimension_semantics=sem, vmem_limit_bytes=vmem)


def _rms(x, g):
    return x * lax.rsqrt(jnp.mean(x * x, axis=-1, keepdims=True) + EPS) * g


def _silu(x):
    return x * jax.nn.sigmoid(x)


def _dot(a, b):
    return jnp.dot(a, b, preferred_element_type=F32)


def _split3(a):
    a0 = a.astype(BF16)
    r1 = a - a0.astype(F32)
    a1 = r1.astype(BF16)
    a2 = (r1 - a1.astype(F32)).astype(BF16)
    return a0, a1, a2


def _mod_parts(mod, d):
    return [mod[:, i * d:(i + 1) * d] for i in range(N_MOD)]


def _ada_kernel(c_ref, w_ref, b_ref, o_ref):
    a = _silu(c_ref[...]).astype(BF16)
    o_ref[...] = _dot(a, w_ref[...].astype(BF16)) + b_ref[...]


def _ada(c_all, w_ada, b_ada):
    n, d = c_all.shape
    cols = w_ada.shape[1]
    bc = d
    return pl.pallas_call(
        _ada_kernel,
        grid=(cols // bc,),
        in_specs=[pl.BlockSpec((n, d), lambda i: (0, 0)),
                  pl.BlockSpec((d, bc), lambda i: (0, i)),
                  pl.BlockSpec((1, bc), lambda i: (0, i))],
        out_specs=pl.BlockSpec((n, bc), lambda i: (0, i)),
        out_shape=jax.ShapeDtypeStruct((n, cols), F32),
        compiler_params=_cp(("arbitrary",)),
        name="ada",
    )(c_all, w_ada, b_ada.reshape(1, cols))


class _Dims:
    def __init__(self, d, pw, heads, dk, dv):
        self.d, self.pw, self.h, self.dk, self.dv = d, pw, heads, dk, dv
        self.hk, self.gw = heads * dk, heads * dv
        self.q0 = pw
        self.k0 = self.q0 + self.hk
        self.v0 = self.k0 + self.hk
        self.go0 = self.v0 + self.gw
        self.ga0 = self.go0 + self.gw
        self.gb0 = self.ga0 + d
        self.al0 = self.gb0 + d
        self.zw = self.al0 + LANE


def _log_decay(z_al, wal_ref, bal_ref):
    xal = _dot(z_al.astype(BF16), wal_ref[...]) + bal_ref[...]
    return jax.nn.log_sigmoid(xal) / GLA_GATE_TEMP


def _pool_project(mixed_groups, pw_ref, ps_ref):
    ys = [_dot(m.astype(BF16), pw_ref[g]) for g, m in enumerate(mixed_groups)]
    return jnp.concatenate(ys, axis=1) * ps_ref[...]


def _post_mix(dm, x, mod, ya, o, go, ga, gb, gn_ref, wa_ref, wb_ref, wo_ref, n2_ref, wrh_ref, wrl_ref):
    _, _, g1, sh2, sc2, _ = _mod_parts(mod, dm.d)
    parts = []
    for h in range(dm.h):
        oh = o[:, h * dm.dv:(h + 1) * dm.dv]
        parts.append(oh * lax.rsqrt(jnp.mean(oh * oh, axis=-1, keepdims=True) + EPS))
    yb = jnp.concatenate(parts, axis=1) * gn_ref[...] * _silu(go)
    m = (jax.nn.sigmoid(ga) * _dot(ya.astype(BF16), wa_ref[...])
         + jax.nn.sigmoid(gb) * _dot(yb.astype(BF16), wb_ref[...]))
    x1 = x + g1 * _dot(m.astype(BF16), wo_ref[...])
    h2 = _rms(x1, n2_ref[...]) * (1 + sc2) + sh2
    hi = h2.astype(BF16)
    lo = (h2 - hi.astype(F32)).astype(BF16)
    logits = _dot(hi, wrh_ref[...]) + (_dot(hi, wrl_ref[...]) + _dot(lo, wrh_ref[...]))
    return x1, h2, logits


def _mixp_kernel(x_ref, mod_ref, n1_ref, win_ref, wal_ref, bal_ref, pw_ref, ps_ref, gn_ref,
                 wa_ref, wb_ref, wo_ref, n2_ref, wrh_ref, wrl_ref,
                 x1_ref, h2_ref, lg_ref, np_ref, ng_ref,
                 z_ref, uext_ref, s_ref, b_ref, oi_ref, *, dm, tt, n_t):
    j = pl.program_id(1)
    x = x_ref[...]
    mod = mod_ref[...]
    sh1, sc1 = mod[:, 0:dm.d], mod[:, dm.d:2 * dm.d]
    h = _rms(x, n1_ref[...]) * (1 + sc1) + sh1
    z_ref[...] = _dot(h.astype(BF16), win_ref[...])

    @pl.when(j == 0)
    def _():
        uext_ref[0:POOL_HIST, :] = jnp.zeros((POOL_HIST, dm.pw), F32)
        s_ref[...] = jnp.zeros(s_ref.shape, F32)

    uext_ref[POOL_HIST:POOL_HIST + tt, :] = z_ref[:, 0:dm.pw]
    pos = j * tt + lax.broadcasted_iota(I32, (tt, 1), 0)
    gw = dm.pw // len(POOL_WINDOWS)
    mixed = []
    for g, w in enumerate(POOL_WINDOWS):
        c0 = g * gw
        cur = uext_ref[POOL_HIST:POOL_HIST + tt, c0:c0 + gw]
        acc = cur
        for i in range(1, w):
            acc = acc + uext_ref[POOL_HIST - i:POOL_HIST - i + tt, c0:c0 + gw]
        cnt = jnp.minimum(w, pos + 1).astype(F32)
        mixed.append(acc / cnt - cur)
    ya = _pool_project(mixed, pw_ref, ps_ref)

    @pl.when(j == n_t - 1)
    def _():
        np_ref[...] = uext_ref[tt + 1:tt + POOL_HIST, :]

    uext_ref[0:POOL_HIST, :] = uext_ref[tt:tt + POOL_HIST, :]

    la = _log_decay(z_ref[:, dm.al0:dm.al0 + LANE], wal_ref, bal_ref)
    r_i = lax.broadcasted_iota(I32, (tt, tt), 0)
    c_i = lax.broadcasted_iota(I32, (tt, tt), 1)
    causal = r_i >= c_i
    tri = jnp.where(causal, 1.0, 0.0).astype(BF16)
    a0, a1, a2 = _split3(la)
    bc = _dot(tri, a0) + (_dot(tri, a1) + _dot(tri, a2))
    b_ref[...] = bc
    q = z_ref[:, dm.q0:dm.q0 + dm.hk]
    k = z_ref[:, dm.k0:dm.k0 + dm.hk]
    v = z_ref[:, dm.v0:dm.v0 + dm.gw]
    scale = dm.dk ** -0.5
    qe = (q * scale) * jnp.exp(bc)
    blast = bc[tt - 1:tt, :]
    kd = k * jnp.exp(blast - bc)
    s_old = s_ref[...]
    s_bf = s_old.astype(BF16)
    vb = v.astype(BF16)
    lane = lax.broadcasted_iota(I32, (1, dm.hk), 1)
    head_masks = [(lane >= h * dm.dk) & (lane < (h + 1) * dm.dk) for h in range(dm.h)]
    o_inter = [_dot(jnp.where(head_masks[h], qe, 0.0).astype(BF16), s_bf) for h in range(dm.h)]

    fast = jnp.max(-blast) <= GLA_FAST_MAX_DECAY

    @pl.when(fast)
    def _():
        ke = (k * jnp.exp(-bc)).astype(BF16)
        for h in range(dm.h):
            qh = jnp.where(head_masks[h], qe, 0.0).astype(BF16)
            sc = lax.dot_general(qh, ke, (((1,), (1,)), ((), ())), preferred_element_type=F32)
            sc = jnp.where(causal, sc, 0.0).astype(BF16)
            oi_ref[:, h * dm.dv:(h + 1) * dm.dv] = _dot(sc, vb[:, h * dm.dv:(h + 1) * dm.dv])

    @pl.when(jnp.logical_not(fast))
    def _():
        rows = lax.broadcasted_iota(I32, (tt, 1), 0)
        for h in range(dm.h):
            def body(tb, carry, h=h):
                base = pl.multiple_of(tb * 8, 8)
                q8 = z_ref[pl.ds(base, 8), dm.q0:dm.q0 + dm.hk] * scale
                b8 = b_ref[pl.ds(base, 8), :]
                vh = z_ref[:, dm.v0 + h * dm.dv:dm.v0 + (h + 1) * dm.dv]
                out_rows = []
                for r in range(8):
                    keep = (rows <= base + r) & head_masks[h]
                    dec = jnp.exp(jnp.where(keep, b8[r:r + 1, :] - b_ref[...], -jnp.inf))
                    wgt = (q8[r:r + 1, :] * dec) * z_ref[:, dm.k0:dm.k0 + dm.hk]
                    s_col = jnp.sum(wgt, axis=-1, keepdims=True)
                    out_rows.append(jnp.sum(s_col * vh, axis=0, keepdims=True))
                oi_ref[pl.ds(base, 8), h * dm.dv:(h + 1) * dm.dv] = jnp.concatenate(out_rows, axis=0)
                return carry
            lax.fori_loop(0, tt // 8, body, 0)

    o = jnp.concatenate(o_inter, axis=1) + oi_ref[...]

    u_all = _dot(kd.T.astype(BF16), vb)
    dec_col = jnp.broadcast_to(jnp.exp(blast), (dm.hk, dm.hk)).T[:, 0:dm.dv]
    upd = jnp.concatenate(
        [u_all[h * dm.dk:(h + 1) * dm.dk, h * dm.dv:(h + 1) * dm.dv] for h in range(dm.h)], axis=0)
    s_new = dec_col * s_old + upd
    s_ref[...] = s_new

    @pl.when(j == n_t - 1)
    def _():
        ng_ref[...] = s_new

    go = z_ref[:, dm.go0:dm.go0 + dm.gw]
    ga = z_ref[:, dm.ga0:dm.ga0 + dm.d]
    gb = z_ref[:, dm.gb0:dm.gb0 + dm.d]
    x1, h2, logits = _post_mix(dm, x, mod, ya, o, go, ga, gb, gn_ref, wa_ref, wb_ref, wo_ref,
                               n2_ref, wrh_ref, wrl_ref)
    x1_ref[...] = x1
    h2_ref[...] = h2
    lg_ref[...] = logits


def _const_spec(shape):
    nd = len(shape)
    return pl.BlockSpec(shape, lambda *_: (0,) * nd, pipeline_mode=pl.Buffered(1))


def _mix_prompt(dm, x2d, mod3, wts, bsz, seq, n_exp):
    tt = 256 if seq % 256 == 0 else seq
    n_t = seq // tt
    n = bsz * seq
    d = dm.d
    kern = functools.partial(_mixp_kernel, dm=dm, tt=tt, n_t=n_t)
    row = lambda b, j: (b * n_t + j, 0)
    in_specs = [pl.BlockSpec((tt, d), row),
                pl.BlockSpec((None, 1, N_MOD * d), lambda b, j: (b, 0, 0))]
    in_specs += [_const_spec(w.shape) for w in wts]
    out_specs = [pl.BlockSpec((tt, d), row), pl.BlockSpec((tt, d), row), pl.BlockSpec((tt, n_exp), row),
                 pl.BlockSpec((None, POOL_HIST - 1, dm.pw), lambda b, j: (b, 0, 0)),
                 pl.BlockSpec((None, dm.hk, dm.dv), lambda b, j: (b, 0, 0))]
    out_shape = [jax.ShapeDtypeStruct((n, d), F32), jax.ShapeDtypeStruct((n, d), F32),
                 jax.ShapeDtypeStruct((n, n_exp), F32),
                 jax.ShapeDtypeStruct((bsz, POOL_HIST - 1, dm.pw), F32),
                 jax.ShapeDtypeStruct((bsz, dm.hk, dm.dv), F32)]
    scratch = [pltpu.VMEM((tt, dm.zw), F32), pltpu.VMEM((tt + POOL_HIST, dm.pw), F32),
               pltpu.VMEM((dm.hk, dm.dv), F32), pltpu.VMEM((tt, dm.hk), F32), pltpu.VMEM((tt, dm.gw), F32)]
    return pl.pallas_call(
        kern, grid=(bsz, n_t), in_specs=in_specs, out_specs=out_specs, out_shape=out_shape,
        scratch_shapes=scratch, compiler_params=_cp(("arbitrary", "arbitrary")), name="mixp",
    )(x2d, mod3, *wts)


def _sin_kernel(x_ref, mod_ref, n1_ref, win_ref, z_ref, *, d):
    mod = mod_ref[...]
    h = _rms(x_ref[...], n1_ref[...]) * (1 + mod[:, d:2 * d]) + mod[:, 0:d]
    z_ref[...] = _dot(h.astype(BF16), win_ref[...])


def _sample_in(dm, xs2d, mod_s, n1, win, bs, t_s):
    d = dm.d
    return pl.pallas_call(
        functools.partial(_sin_kernel, d=d),
        grid=(t_s,),
        in_specs=[pl.BlockSpec((bs, d), lambda t: (0, t)),
                  _const_spec(mod_s.shape), _const_spec(n1.shape), _const_spec(win.shape)],
        out_specs=pl.BlockSpec((bs, dm.zw), lambda t: (t, 0)),
        out_shape=jax.ShapeDtypeStruct((t_s * bs, dm.zw), F32),
        compiler_params=_cp(("arbitrary",)), name="sample_in",
    )(xs2d, mod_s, n1, win)


def _sst_kernel(z_ref, pool_ref, gs_ref, wal_ref, bal_ref,
                mixed_ref, o_ref, npool_ref, ngs_ref,
                qt_ref, kt_ref, dt_ref, vt_ref, ot_ref, *, dm, bs, t_s, ch, pos0):
    i = pl.program_id(0)
    n_i = pl.num_programs(0)
    scale = dm.dk ** -0.5
    past = POOL_HIST - 1

    @pl.when(i == 0)
    def _():
        for t in range(t_s):
            r0 = t * bs
            la = _log_decay(z_ref[r0:r0 + bs, dm.al0:dm.al0 + LANE], wal_ref, bal_ref)
            dt_ref[t] = jnp.exp(la).T
            qt_ref[t] = (z_ref[r0:r0 + bs, dm.q0:dm.q0 + dm.hk] * scale).T
            kt_ref[t] = z_ref[r0:r0 + bs, dm.k0:dm.k0 + dm.hk].T
            vt_ref[t] = z_ref[r0:r0 + bs, dm.v0:dm.v0 + dm.gw].T
        ot_ref[...] = jnp.zeros(ot_ref.shape, F32)

        def item(s, c0, c1):
            if s >= 0:
                return z_ref[s * bs:(s + 1) * bs, c0:c1]
            return pool_ref[:, (past + s) * dm.pw + c0:(past + s) * dm.pw + c1]

        gw = dm.pw // len(POOL_WINDOWS)
        for t in range(t_s):
            for g, w in enumerate(POOL_WINDOWS):
                c0, c1 = g * gw, (g + 1) * gw
                cur = item(t, c0, c1)
                acc = cur
                for r in range(1, w):
                    acc = acc + item(t - r, c0, c1)
                cnt = float(min(w, pos0 + t + 1))
                mixed_ref[t * bs:(t + 1) * bs, c0:c1] = acc / cnt - cur
        for r in range(past):
            npool_ref[:, r * dm.pw:(r + 1) * dm.pw] = item(r - past + t_s, 0, dm.pw)

    per_head = dm.dk // ch
    hoff = pl.multiple_of((i // per_head) * dm.dv, dm.dv)
    for p in range(ch):
        hd = i * ch + p
        st = gs_ref[:, p * dm.dv:(p + 1) * dm.dv].T
        for t in range(t_s):
            st = dt_ref[t, pl.ds(hd, 1), :] * st + kt_ref[t, pl.ds(hd, 1), :] * vt_ref[t, pl.ds(hoff, dm.dv), :]
            ot_ref[t, pl.ds(hoff, dm.dv), :] += qt_ref[t, pl.ds(hd, 1), :] * st
        ngs_ref[:, p * dm.dv:(p + 1) * dm.dv] = st.T

    @pl.when(i == n_i - 1)
    def _():
        for t in range(t_s):
            for h in range(dm.h):
                o_ref[t * bs:(t + 1) * bs, h * dm.dv:(h + 1) * dm.dv] = ot_ref[t, h * dm.dv:(h + 1) * dm.dv, :].T


def _sample_state(dm, z_s, pool2d, gs2d, wal, bal, bs, t_s):
    ch = 16
    n_i = dm.hk // ch
    past = POOL_HIST - 1
    kern = functools.partial(_sst_kernel, dm=dm, bs=bs, t_s=t_s, ch=ch, pos0=PAST_LEN)
    full = lambda shape: pl.BlockSpec(shape, lambda i: (0,) * len(shape))
    return pl.pallas_call(
        kern, grid=(n_i,),
        in_specs=[_const_spec(z_s.shape), _const_spec(pool2d.shape),
                  pl.BlockSpec((bs, ch * dm.dv), lambda i: (0, i)),
                  _const_spec(wal.shape), _const_spec(bal.shape)],
        out_specs=[full((t_s * bs, dm.pw)), full((t_s * bs, dm.gw)), full((bs, past * dm.pw)),
                   pl.BlockSpec((bs, ch * dm.dv), lambda i: (0, i))],
        out_shape=[jax.ShapeDtypeStruct((t_s * bs, dm.pw), F32), jax.ShapeDtypeStruct((t_s * bs, dm.gw), F32),
                   jax.ShapeDtypeStruct((bs, past * dm.pw), F32), jax.ShapeDtypeStruct(gs2d.shape, F32)],
        scratch_shapes=[pltpu.VMEM((t_s, dm.hk, bs), F32), pltpu.VMEM((t_s, dm.hk, bs), F32),
                        pltpu.VMEM((t_s, dm.hk, bs), F32), pltpu.VMEM((t_s, dm.gw, bs), F32),
                        pltpu.VMEM((t_s, dm.gw, bs), F32)],
        compiler_params=_cp(("arbitrary",)), name="sample_state",
    )(z_s, pool2d, gs2d, wal, bal)


def _spost_kernel(x_ref, mod_ref, z_ref, mixed_ref, o_ref, pw_ref, ps_ref, gn_ref,
                  wa_ref, wb_ref, wo_ref, n2_ref, wrh_ref, wrl_ref,
                  x1_ref, h2_ref, lg_ref, *, dm):
    gw = dm.pw // len(POOL_WINDOWS)
    ya = _pool_project([mixed_ref[:, g * gw:(g + 1) * gw] for g in range(len(POOL_WINDOWS))], pw_ref, ps_ref)
    x1, h2, logits = _post_mix(
        dm, x_ref[...], mod_ref[...], ya, o_ref[...],
        z_ref[:, dm.go0:dm.go0 + dm.gw], z_ref[:, dm.ga0:dm.ga0 + dm.d], z_ref[:, dm.gb0:dm.gb0 + dm.d],
        gn_ref, wa_ref, wb_ref, wo_ref, n2_ref, wrh_ref, wrl_ref)
    x1_ref[...] = x1
    h2_ref[...] = h2
    lg_ref[...] = logits


def _sample_post(dm, xs2d, mod_s, z_s, mixed_s, o_s, wts, bs, t_s, n_exp):
    d = dm.d
    rows = lambda t: (t, 0)
    in_specs = [pl.BlockSpec((bs, d), lambda t: (0, t)), _const_spec(mod_s.shape),
                pl.BlockSpec((bs, dm.zw), rows), pl.BlockSpec((bs, dm.pw), rows), pl.BlockSpec((bs, dm.gw), rows)]
    in_specs += [_const_spec(w.shape) for w in wts]
    return pl.pallas_call(
        functools.partial(_spost_kernel, dm=dm), grid=(t_s,), in_specs=in_specs,
        out_specs=[pl.BlockSpec((bs, d), rows), pl.BlockSpec((bs, d), rows), pl.BlockSpec((bs, n_exp), rows)],
        out_shape=[jax.ShapeDtypeStruct((t_s * bs, d), F32), jax.ShapeDtypeStruct((t_s * bs, d), F32),
                   jax.ShapeDtypeStruct((t_s * bs, n_exp), F32)],
        compiler_params=_cp(("arbitrary",)), name="sample_post",
    )(xs2d, mod_s, z_s, mixed_s, o_s, *wts)


def _route_kernel(lgp_ref, lgs_ref, bias_ref, idx_ref, w_ref, cnt_ref, *, n_exp, tiles_p):
    i = pl.program_id(0)
    s = jax.nn.sigmoid(jnp.where(i < tiles_p, lgp_ref[...], lgs_ref[...]))
    cur = s + bias_ref[...]
    tr = s.shape[0]
    lane = lax.broadcasted_iota(I32, (1, n_exp), 1).astype(F32)
    slot = lax.broadcasted_iota(I32, (1, LANE), 1)
    idx8 = jnp.zeros((tr, LANE), F32)
    w8 = jnp.zeros((tr, LANE), F32)
    wsum = jnp.zeros((tr, 1), F32)
    picked = jnp.zeros((tr, n_exp), F32)
    for k in range(TOP_K):
        m = jnp.max(cur, axis=-1, keepdims=True)
        am = jnp.min(jnp.where(cur == m, lane, float(n_exp)), axis=-1, keepdims=True)
        oh = lane == am
        sk = jnp.sum(jnp.where(oh, s, 0.0), axis=-1, keepdims=True)
        cur = jnp.where(oh, -jnp.inf, cur)
        idx8 = jnp.where(slot == k, am, idx8)
        w8 = jnp.where(slot == k, sk, w8)
        wsum = wsum + sk
        picked = picked + jnp.where(oh, 1.0, 0.0)
    idx_ref[...] = idx8.astype(I32)
    w_ref[...] = w8 / wsum * ROUTED_SCALE

    @pl.when(i == 0)
    def _():
        cnt_ref[...] = jnp.zeros(cnt_ref.shape, F32)

    cnt_ref[...] += jnp.sum(picked, axis=0, keepdims=True)


def _two_source_specs(block, tiles_p, tiles_s):
    return [pl.BlockSpec(block, lambda i: (jnp.minimum(i, tiles_p - 1), 0)),
            pl.BlockSpec(block, lambda i: (jnp.clip(i - tiles_p, 0, tiles_s - 1), 0))]


def _route(lg_p, lg_s, bias, tr):
    n_exp = lg_p.shape[1]
    tiles_p, tiles_s = lg_p.shape[0] // tr, lg_s.shape[0] // tr
    n = (tiles_p + tiles_s) * tr
    rows = lambda i: (i, 0)
    return pl.pallas_call(
        functools.partial(_route_kernel, n_exp=n_exp, tiles_p=tiles_p), grid=(tiles_p + tiles_s,),
        in_specs=_two_source_specs((tr, n_exp), tiles_p, tiles_s) + [_const_spec((1, n_exp))],
        out_specs=[pl.BlockSpec((tr, LANE), rows), pl.BlockSpec((tr, LANE), rows),
                   pl.BlockSpec((8, n_exp), lambda i: (0, 0))],
        out_shape=[jax.ShapeDtypeStruct((n, LANE), I32), jax.ShapeDtypeStruct((n, LANE), F32),
                   jax.ShapeDtypeStruct((8, n_exp), F32)],
        compiler_params=_cp(("arbitrary",)), name="route",
    )(lg_p, lg_s, bias.reshape(1, n_exp))


def _plan_kernel(idx_ref, cnt_ref, pos_ref, bounds_ref, start_ref, run_ref, *, n_exp):
    i = pl.program_id(0)
    lane = lax.broadcasted_iota(I32, (1, n_exp), 1)

    @pl.when(i == 0)
    def _():
        cnt = cnt_ref[...].astype(I32)
        padded = ((cnt + (EXPERT_BLOCK - 1)) // EXPERT_BLOCK) * EXPERT_BLOCK
        pe = padded
        s = 1
        while s < n_exp:
            pe = pe + jnp.where(lane >= s, pltpu.roll(pe, s, 1), 0)
            s *= 2
        start_ref[...] = (pe - padded).astype(F32)
        run_ref[...] = jnp.zeros(run_ref.shape, F32)
        row = lax.broadcasted_iota(I32, (8, n_exp), 0)
        bounds_ref[...] = jnp.where(row == 0, pe - padded, pe)

    idx = idx_ref[...]
    tr = idx.shape[0]
    onehots = [lane == idx[:, k:k + 1] for k in range(TOP_K)]
    member = onehots[0]
    for k in range(1, TOP_K):
        member = member | onehots[k]
    mb = jnp.where(member, 1.0, 0.0).astype(BF16)
    r_i = lax.broadcasted_iota(I32, (tr, tr), 0)
    c_i = lax.broadcasted_iota(I32, (tr, tr), 1)
    earlier = jnp.where(r_i > c_i, 1.0, 0.0).astype(BF16)
    dest = _dot(earlier, mb) + (start_ref[0:1, :] + run_ref[0:1, :])
    slot = lax.broadcasted_iota(I32, (1, LANE), 1)
    pos8 = jnp.zeros((tr, LANE), F32)
    for k in range(TOP_K):
        pk = jnp.sum(jnp.where(onehots[k], dest, 0.0), axis=-1, keepdims=True)
        pos8 = jnp.where(slot == k, pk, pos8)
    pos_ref[...] = pos8.T[0:TOP_K, :].astype(I32)
    run_ref[...] += jnp.sum(mb.astype(F32), axis=0, keepdims=True)


def _plan(idx8, counts, tr):
    n = idx8.shape[0]
    n_exp = counts.shape[1]
    rows = lambda i: (i, 0)
    return pl.pallas_call(
        functools.partial(_plan_kernel, n_exp=n_exp), grid=(n // tr,),
        in_specs=[pl.BlockSpec((tr, LANE), rows), _const_spec((8, n_exp))],
        out_specs=[pl.BlockSpec((TOP_K, tr), lambda i: (0, i)), pl.BlockSpec((8, n_exp), lambda i: (0, 0))],
        out_shape=[jax.ShapeDtypeStruct((TOP_K, n), I32), jax.ShapeDtypeStruct((8, n_exp), I32)],
        scratch_shapes=[pltpu.VMEM((8, n_exp), F32), pltpu.VMEM((8, n_exp), F32)],
        compiler_params=_cp(("arbitrary",)), name="plan",
    )(idx8, counts)


def _row_copy(src_ref, src_row, dst_ref, dst_row, sem):
    return pltpu.make_async_copy(src_ref.at[pl.ds(src_row, 1)], dst_ref.at[pl.ds(dst_row, 1)], sem)


def _dispatch_kernel(pos_ref, hp_ref, hs_ref, xs_ref, sem, *, td, tiles_p):
    i = pl.program_id(0)

    def scatter_rows(h_ref):
        def issue(t, carry):
            for k in range(TOP_K):
                _row_copy(h_ref, t, xs_ref, pos_ref[k, t], sem).start()
            return carry
        lax.fori_loop(0, td, issue, 0)

        def drain(t, carry):
            for k in range(TOP_K):
                _row_copy(h_ref, 0, xs_ref, 0, sem).wait()
            return carry
        lax.fori_loop(0, td, drain, 0)

    @pl.when(i < tiles_p)
    def _():
        scatter_rows(hp_ref)

    @pl.when(i >= tiles_p)
    def _():
        scatter_rows(hs_ref)


def _dispatch(h2_p, h2_s, pos_t, n_rows, td):
    d = h2_p.shape[1]
    tiles_p, tiles_s = h2_p.shape[0] // td, h2_s.shape[0] // td
    return pl.pallas_call(
        functools.partial(_dispatch_kernel, td=td, tiles_p=tiles_p), grid=(tiles_p + tiles_s,),
        in_specs=[pl.BlockSpec((TOP_K, td), lambda i: (0, i), memory_space=pltpu.SMEM)]
        + _two_source_specs((td, d), tiles_p, tiles_s),
        out_specs=pl.BlockSpec(memory_space=pl.ANY),
        out_shape=jax.ShapeDtypeStruct((n_rows, d), F32),
        scratch_shapes=[pltpu.SemaphoreType.DMA],
        compiler_params=_cp(("arbitrary",)), name="dispatch",
    )(pos_t, h2_p, h2_s)


MOE_RING = 6


def _moe_kernel(st_ref, wg_ref, wu_ref, wd_ref, xs_ref, ys_ref, xbuf, ybuf, wgb, wub, wdb, semx, semy):
    e = pl.program_id(0)
    n_e = pl.num_programs(0)
    g0 = st_ref[e] // EXPERT_BLOCK
    nchunks = st_ref[e + 1] // EXPERT_BLOCK - g0
    ntot = st_ref[n_e] // EXPERT_BLOCK

    def rows(g):
        return pl.ds(pl.multiple_of(g * EXPERT_BLOCK, EXPERT_BLOCK), EXPERT_BLOCK)

    def x_copy(g):
        slot = g % MOE_RING
        return pltpu.make_async_copy(xs_ref.at[rows(g)], xbuf.at[slot], semx.at[slot])

    def y_copy(g):
        slot = g % MOE_RING
        return pltpu.make_async_copy(ybuf.at[slot], ys_ref.at[rows(g)], semy.at[slot])

    @pl.when(e == 0)
    def _():
        for g in range(MOE_RING - 1):
            @pl.when(g < ntot)
            def _(g=g):
                x_copy(g).start()

    wgb[...] = wg_ref[...].astype(BF16)
    wub[...] = wu_ref[...].astype(BF16)
    wdb[...] = wd_ref[...].astype(BF16)

    def body(c, carry):
        g = g0 + c
        slot = g % MOE_RING

        @pl.when(g + (MOE_RING - 1) < ntot)
        def _():
            x_copy(g + (MOE_RING - 1)).start()

        x_copy(g).wait()

        @pl.when(g >= MOE_RING)
        def _():
            y_copy(g - MOE_RING).wait()

        x = xbuf[slot].astype(BF16)
        mid = _silu(_dot(x, wgb[...])) * _dot(x, wub[...])
        ybuf[slot] = _dot(mid.astype(BF16), wdb[...])
        y_copy(g).start()
        return carry

    lax.fori_loop(0, nchunks, body, 0)

    @pl.when(e == n_e - 1)
    def _():
        for j in range(MOE_RING):
            g = ntot - MOE_RING + j

            @pl.when(g >= 0)
            def _(g=g):
                y_copy(g).wait()


def _moe(starts, xs, wg, wu, wd):
    n_rows, d = xs.shape
    n_exp, _, f = wg.shape
    grid_spec = pltpu.PrefetchScalarGridSpec(
        num_scalar_prefetch=1, grid=(n_exp,),
        in_specs=[pl.BlockSpec((None, d, f), lambda e, st: (e, 0, 0)),
                  pl.BlockSpec((None, d, f), lambda e, st: (e, 0, 0)),
                  pl.BlockSpec((None, f, d), lambda e, st: (e, 0, 0)),
                  pl.BlockSpec(memory_space=pl.ANY)],
        out_specs=pl.BlockSpec(memory_space=pl.ANY),
        scratch_shapes=[pltpu.VMEM((MOE_RING, EXPERT_BLOCK, d), F32), pltpu.VMEM((MOE_RING, EXPERT_BLOCK, d), F32),
                        pltpu.VMEM((d, f), BF16), pltpu.VMEM((d, f), BF16), pltpu.VMEM((f, d), BF16),
                        pltpu.SemaphoreType.DMA((MOE_RING,)), pltpu.SemaphoreType.DMA((MOE_RING,))])
    return pl.pallas_call(
        _moe_kernel, grid_spec=grid_spec, out_shape=jax.ShapeDtypeStruct((n_rows, d), F32),
        compiler_params=_cp(("arbitrary",)), name="moe",
    )(starts, wg, wu, wd, xs)


def _final_kernel(pos_ref, w8_ref, x1_ref, h2_ref, mod_ref, wsg_ref, wsu_ref, wsd_ref, nf_ref, ys_ref,
                  y_ref, ybuf, sem, *, tf, d):
    def issue(t, carry):
        for k in range(TOP_K):
            _row_copy(ys_ref, pos_ref[k, t], ybuf.at[k], t, sem).start()
        return carry
    lax.fori_loop(0, tf, issue, 0)

    hb = h2_ref[...].astype(BF16)
    shared = _dot((_silu(_dot(hb, wsg_ref[...])) * _dot(hb, wsu_ref[...])).astype(BF16), wsd_ref[...])

    def drain(t, carry):
        for k in range(TOP_K):
            _row_copy(ys_ref, 0, ybuf.at[k], 0, sem).wait()
        return carry
    lax.fori_loop(0, tf, drain, 0)

    w8 = w8_ref[...]
    routed = ybuf[0] * w8[:, 0:1]
    for k in range(1, TOP_K):
        routed = routed + ybuf[k] * w8[:, k:k + 1]
    g2 = mod_ref[...][:, (N_MOD - 1) * d:N_MOD * d]
    x2 = x1_ref[...] + g2 * (routed + shared)
    y_ref[...] = _rms(x2, nf_ref[...])


def _final(pos_t, w8, x1, h2, mod, mod_spec, ys, wts, nf, tf, n_tiles, tile0, out_shape, out_spec):
    d = x1.shape[1]
    rows = lambda i: (i, 0)
    in_specs = [pl.BlockSpec((TOP_K, tf), lambda i: (0, tile0 + i), memory_space=pltpu.SMEM),
                pl.BlockSpec((tf, LANE), lambda i: (tile0 + i, 0)),
                pl.BlockSpec((tf, d), rows), pl.BlockSpec((tf, d), rows), mod_spec]
    in_specs += [_const_spec(w.shape) for w in wts]
    in_specs += [_const_spec(nf.shape), pl.BlockSpec(memory_space=pl.ANY)]
    return pl.pallas_call(
        functools.partial(_final_kernel, tf=tf, d=d), grid=(n_tiles,), in_specs=in_specs,
        out_specs=out_spec, out_shape=out_shape,
        scratch_shapes=[pltpu.VMEM((TOP_K, tf, d), F32), pltpu.SemaphoreType.DMA],
        compiler_params=_cp(("arbitrary",)), name="final",
    )(pos_t, w8, x1, h2, mod, *wts, nf, ys)


def kernel(x_prompt, x_sample, c_prompt, c_sample, state_pool, state_gla, w_ada, b_ada, norm1, w_in,
           pool_w, pool_scale, gla_w_alpha, gla_b_alpha, gla_norm, w_branch_a, w_branch_b, w_out, norm2,
           w_router, router_bias, w_exp_gate, w_exp_up, w_exp_down, w_sh_gate, w_sh_up, w_sh_down, norm_f):
    depth = w_in.shape[0]
    assert depth == 1, "single-layer trunk"
    bp, seq, d = x_prompt.shape
    bs, t_s, _ = x_sample.shape
    _, _, past, pw = state_pool.shape
    _, _, heads, dk, dv = state_gla.shape
    rank = gla_w_alpha.shape[1]
    n_exp = w_router.shape[2]
    assert past == POOL_HIST - 1 and pw == len(POOL_WINDOWS) * LANE and dv == LANE and rank <= LANE
    dm = _Dims(d, pw, heads, dk, dv)
    n_p, n_s = bp * seq, bs * t_s
    n_all = n_p + n_s

    off = [0]
    for sz in (pw, dm.hk, dm.hk, dm.gw, dm.gw, rank, d, d):
        off.append(off[-1] + sz)
    wi = w_in[0]
    win = jnp.concatenate([wi[:, :off[5]], wi[:, off[6]:], wi[:, off[5]:off[6]],
                           jnp.zeros((d, LANE - rank), F32)], axis=1).astype(BF16)
    wal = jnp.concatenate([gla_w_alpha[0], jnp.zeros((LANE - rank, dm.hk), F32)], axis=0).astype(BF16)
    bal = gla_b_alpha[0].reshape(1, dm.hk)
    wr = w_router[0]
    wrh = wr.astype(BF16)
    wrl = (wr - wrh.astype(F32)).astype(BF16)
    post_w = (pool_w[0].astype(BF16), pool_scale[0].reshape(1, pw), gla_norm[0].reshape(1, dm.gw),
              w_branch_a[0].astype(BF16), w_branch_b[0].astype(BF16), w_out[0].astype(BF16),
              norm2[0].reshape(1, d), wrh, wrl)
    n1 = norm1[0].reshape(1, d)

    mod = _ada(jnp.concatenate([c_prompt, c_sample], axis=0), w_ada[0], b_ada[0])
    mod_p = mod[:bp].reshape(bp, 1, N_MOD * d)
    mod_s = mod[bp:]

    x1_p, h2_p, lg_p, npool_p, ngla_p = _mix_prompt(
        dm, x_prompt.reshape(n_p, d), mod_p, (n1, win, wal, bal) + post_w, bp, seq, n_exp)

    xs2d = x_sample.reshape(bs, t_s * d)
    z_s = _sample_in(dm, xs2d, mod_s, n1, win, bs, t_s)
    mixed_s, o_s, npool_s, ngla_s = _sample_state(
        dm, z_s, state_pool[0].reshape(bs, past * pw), state_gla[0].reshape(bs, dm.hk * dv), wal, bal, bs, t_s)
    x1_s, h2_s, lg_s = _sample_post(dm, xs2d, mod_s, z_s, mixed_s, o_s, post_w, bs, t_s, n_exp)

    tr = 512 if (n_p % 512 == 0 and n_s % 512 == 0) else LANE
    assert n_p % tr == 0 and n_s % tr == 0 and n_p % bs == 0
    idx8, w8, counts = _route(lg_p, lg_s, router_bias[0], tr)
    nblk = (n_all * TOP_K) // EXPERT_BLOCK + n_exp
    pos_t, bounds = _plan(idx8, counts, tr)
    starts = jnp.concatenate([bounds[0], bounds[1, n_exp - 1:]])

    td = 256 if (n_p % 256 == 0 and n_s % 256 == 0) else LANE
    xs = _dispatch(h2_p, h2_s, pos_t, nblk * EXPERT_BLOCK, td)
    ys = _moe(starts, xs, w_exp_gate[0], w_exp_up[0], w_exp_down[0])

    sh_w = (w_sh_gate[0].astype(BF16), w_sh_up[0].astype(BF16), w_sh_down[0].astype(BF16))
    nf = norm_f.reshape(1, d)
    tf = 256 if seq % 256 == 0 else seq
    n_tp = seq // tf
    y_p = _final(
        pos_t, w8, x1_p, h2_p, mod_p,
        pl.BlockSpec((None, 1, N_MOD * d), lambda i: (i // n_tp, 0, 0)), ys, sh_w, nf, tf, n_p // tf, 0,
        jax.ShapeDtypeStruct((n_p, d), F32), pl.BlockSpec((tf, d), lambda i: (i, 0)))
    y_s = _final(
        pos_t, w8, x1_s, h2_s, mod_s,
        _const_spec(mod_s.shape), ys, sh_w, nf, bs, t_s, n_p // bs,
        jax.ShapeDtypeStruct((bs, t_s * d), F32), pl.BlockSpec((bs, d), lambda i: (0, i)))

    return (y_p.reshape(bp, seq, d), y_s.reshape(bs, t_s, d),
            npool_p.reshape(depth, bp, past, pw), ngla_p.reshape(depth, bp, heads, dk, dv),
            npool_s.reshape(depth, bs, past, pw), ngla_s.reshape(depth, bs, heads, dk, dv))
```

```python
import functools

import jax
import jax.numpy as jnp
from jax import lax
from jax.experimental import pallas as pl
from jax.experimental.pallas import tpu as pltpu
from jax.experimental.pallas import tpu_sc as plsc

F32 = jnp.float32
BF16 = jnp.bfloat16
I32 = jnp.int32

EPS = 1e-6
N_MOD = 6
POOL_WINDOWS = (2, 4, 8, 16)
POOL_HIST = 16
GLA_GATE_TEMP = 16.0
TOP_K = 8
ROUTED_SCALE = 2.5
PAST_LEN = 16384
EXPERT_BLOCK = 128
LANE = 128
GLA_FAST_MAX_DECAY = 40.0
VMEM_LIMIT = 56 * 1024 * 1024


def _cp(sem, vmem=VMEM_LIMIT):
    return pltpu.CompilerParams(dimension_semantics=sem, vmem_limit_bytes=vmem)


def _rms(x, g):
    return x * lax.rsqrt(jnp.mean(x * x, axis=-1, keepdims=True) + EPS) * g


def _silu(x):
    return x * jax.nn.sigmoid(x)


def _dot(a, b):
    return jnp.dot(a, b, preferred_element_type=F32)


def _split3(a):
    a0 = a.astype(BF16)
    r1 = a - a0.astype(F32)
    a1 = r1.astype(BF16)
    a2 = (r1 - a1.astype(F32)).astype(BF16)
    return a0, a1, a2


def _mod_parts(mod, d):
    return [mod[:, i * d:(i + 1) * d] for i in range(N_MOD)]


_HI16 = -65536


def _pack_rows(v):
    c = v.shape[1] // 2
    lo = lax.bitcast_convert_type(v[:, :c].astype(BF16).astype(F32), I32)
    hi = lax.bitcast_convert_type(v[:, c:].astype(BF16).astype(F32), I32)
    return (hi & _HI16) | lax.shift_right_logical(lo, 16)


def _unpack_rows(w):
    lo = lax.bitcast_convert_type(lax.shift_left(w, 16), F32)
    hi = lax.bitcast_convert_type(w & _HI16, F32)
    return jnp.concatenate([lo, hi], axis=1)


def _ada_kernel(c_ref, w_ref, b_ref, o_ref):
    a = _silu(c_ref[...]).astype(BF16)
    o_ref[...] = _dot(a, w_ref[...].astype(BF16)) + b_ref[...]


def _ada(c_all, w_ada, b_ada):
    n, d = c_all.shape
    cols = w_ada.shape[1]
    bc = d
    return pl.pallas_call(
        _ada_kernel,
        grid=(cols // bc,),
        in_specs=[pl.BlockSpec((n, d), lambda i: (0, 0)),
                  pl.BlockSpec((d, bc), lambda i: (0, i)),
                  pl.BlockSpec((1, bc), lambda i: (0, i))],
        out_specs=pl.BlockSpec((n, bc), lambda i: (0, i)),
        out_shape=jax.ShapeDtypeStruct((n, cols), F32),
        compiler_params=_cp(("arbitrary",)),
        name="ada",
    )(c_all, w_ada, b_ada.reshape(1, cols))


class _Dims:
    def __init__(self, d, pw, heads, dk, dv):
        self.d, self.pw, self.h, self.dk, self.dv = d, pw, heads, dk, dv
        self.hk, self.gw = heads * dk, heads * dv
        self.q0 = pw
        self.k0 = self.q0 + self.hk
        self.v0 = self.k0 + self.hk
        self.go0 = self.v0 + self.gw
        self.ga0 = self.go0 + self.gw
        self.gb0 = self.ga0 + d
        self.al0 = self.gb0 + d
        self.zw = self.al0 + LANE


def _log_decay(z_al, wal_ref, bal_ref):
    xal = _dot(z_al.astype(BF16), wal_ref[...]) + bal_ref[...]
    return jax.nn.log_sigmoid(xal) / GLA_GATE_TEMP


def _pool_project(mixed_groups, pw_ref, ps_ref):
    ys = [_dot(m.astype(BF16), pw_ref[g]) for g, m in enumerate(mixed_groups)]
    return jnp.concatenate(ys, axis=1) * ps_ref[...]


def _post_mix(dm, x, mod, ya, o, go, ga, gb, gn_ref, wa_ref, wb_ref, wo_ref, n2_ref, wrh_ref, wrl_ref):
    _, _, g1, sh2, sc2, _ = _mod_parts(mod, dm.d)
    parts = []
    for h in range(dm.h):
        oh = o[:, h * dm.dv:(h + 1) * dm.dv]
        parts.append(oh * lax.rsqrt(jnp.mean(oh * oh, axis=-1, keepdims=True) + EPS))
    yb = jnp.concatenate(parts, axis=1) * gn_ref[...] * _silu(go)
    m = (jax.nn.sigmoid(ga) * _dot(ya.astype(BF16), wa_ref[...])
         + jax.nn.sigmoid(gb) * _dot(yb.astype(BF16), wb_ref[...]))
    x1 = x + g1 * _dot(m.astype(BF16), wo_ref[...])
    h2 = _rms(x1, n2_ref[...]) * (1 + sc2) + sh2
    hi = h2.astype(BF16)
    lo = (h2 - hi.astype(F32)).astype(BF16)
    logits = _dot(hi, wrh_ref[...]) + (_dot(hi, wrl_ref[...]) + _dot(lo, wrh_ref[...]))
    return x1, _pack_rows(h2), logits


def _mixp_kernel(x_ref, mod_ref, n1_ref, win_ref, wal_ref, bal_ref, pw_ref, ps_ref, gn_ref,
                 wa_ref, wb_ref, wo_ref, n2_ref, wrh_ref, wrl_ref,
                 x1_ref, h2_ref, lg_ref, np_ref, ng_ref,
                 z_ref, uext_ref, s_ref, b_ref, oi_ref, *, dm, tt, n_t):
    j = pl.program_id(1)
    x = x_ref[...]
    mod = mod_ref[...]
    sh1, sc1 = mod[:, 0:dm.d], mod[:, dm.d:2 * dm.d]
    h = _rms(x, n1_ref[...]) * (1 + sc1) + sh1
    z_ref[...] = _dot(h.astype(BF16), win_ref[...])

    @pl.when(j == 0)
    def _():
        uext_ref[0:POOL_HIST, :] = jnp.zeros((POOL_HIST, dm.pw), F32)
        s_ref[...] = jnp.zeros(s_ref.shape, F32)

    uext_ref[POOL_HIST:POOL_HIST + tt, :] = z_ref[:, 0:dm.pw]
    pos = j * tt + lax.broadcasted_iota(I32, (tt, 1), 0)
    gw = dm.pw // len(POOL_WINDOWS)
    mixed = []
    for g, w in enumerate(POOL_WINDOWS):
        c0 = g * gw
        cur = uext_ref[POOL_HIST:POOL_HIST + tt, c0:c0 + gw]
        acc = cur
        for i in range(1, w):
            acc = acc + uext_ref[POOL_HIST - i:POOL_HIST - i + tt, c0:c0 + gw]
        cnt = jnp.minimum(w, pos + 1).astype(F32)
        mixed.append(acc / cnt - cur)
    ya = _pool_project(mixed, pw_ref, ps_ref)

    @pl.when(j == n_t - 1)
    def _():
        np_ref[...] = uext_ref[tt + 1:tt + POOL_HIST, :]

    uext_ref[0:POOL_HIST, :] = uext_ref[tt:tt + POOL_HIST, :]

    la = _log_decay(z_ref[:, dm.al0:dm.al0 + LANE], wal_ref, bal_ref)
    r_i = lax.broadcasted_iota(I32, (tt, tt), 0)
    c_i = lax.broadcasted_iota(I32, (tt, tt), 1)
    causal = r_i >= c_i
    tri = jnp.where(causal, 1.0, 0.0).astype(BF16)
    a0, a1, a2 = _split3(la)
    bc = _dot(tri, a0) + (_dot(tri, a1) + _dot(tri, a2))
    b_ref[...] = bc
    q = z_ref[:, dm.q0:dm.q0 + dm.hk]
    k = z_ref[:, dm.k0:dm.k0 + dm.hk]
    v = z_ref[:, dm.v0:dm.v0 + dm.gw]
    scale = dm.dk ** -0.5
    qe = (q * scale) * jnp.exp(bc)
    blast = bc[tt - 1:tt, :]
    kd = k * jnp.exp(blast - bc)
    s_old = s_ref[...]
    s_bf = s_old.astype(BF16)
    vb = v.astype(BF16)
    lane = lax.broadcasted_iota(I32, (1, dm.hk), 1)
    head_masks = [(lane >= h * dm.dk) & (lane < (h + 1) * dm.dk) for h in range(dm.h)]
    o_inter = [_dot(jnp.where(head_masks[h], qe, 0.0).astype(BF16), s_bf) for h in range(dm.h)]

    fast = jnp.max(-blast) <= GLA_FAST_MAX_DECAY

    @pl.when(fast)
    def _():
        ke = (k * jnp.exp(-bc)).astype(BF16)
        for h in range(dm.h):
            qh = jnp.where(head_masks[h], qe, 0.0).astype(BF16)
            sc = lax.dot_general(qh, ke, (((1,), (1,)), ((), ())), preferred_element_type=F32)
            sc = jnp.where(causal, sc, 0.0).astype(BF16)
            oi_ref[:, h * dm.dv:(h + 1) * dm.dv] = _dot(sc, vb[:, h * dm.dv:(h + 1) * dm.dv])

    @pl.when(jnp.logical_not(fast))
    def _():
        rows = lax.broadcasted_iota(I32, (tt, 1), 0)
        for h in range(dm.h):
            def body(tb, carry, h=h):
                base = pl.multiple_of(tb * 8, 8)
                q8 = z_ref[pl.ds(base, 8), dm.q0:dm.q0 + dm.hk] * scale
                b8 = b_ref[pl.ds(base, 8), :]
                vh = z_ref[:, dm.v0 + h * dm.dv:dm.v0 + (h + 1) * dm.dv]
                out_rows = []
                for r in range(8):
                    keep = (rows <= base + r) & head_masks[h]
                    dec = jnp.exp(jnp.where(keep, b8[r:r + 1, :] - b_ref[...], -jnp.inf))
                    wgt = (q8[r:r + 1, :] * dec) * z_ref[:, dm.k0:dm.k0 + dm.hk]
                    s_col = jnp.sum(wgt, axis=-1, keepdims=True)
                    out_rows.append(jnp.sum(s_col * vh, axis=0, keepdims=True))
                oi_ref[pl.ds(base, 8), h * dm.dv:(h + 1) * dm.dv] = jnp.concatenate(out_rows, axis=0)
                return carry
            lax.fori_loop(0, tt // 8, body, 0)

    o = jnp.concatenate(o_inter, axis=1) + oi_ref[...]

    u_all = _dot(kd.T.astype(BF16), vb)
    dec_col = jnp.broadcast_to(jnp.exp(blast), (dm.hk, dm.hk)).T[:, 0:dm.dv]
    upd = jnp.concatenate(
        [u_all[h * dm.dk:(h + 1) * dm.dk, h * dm.dv:(h + 1) * dm.dv] for h in range(dm.h)], axis=0)
    s_new = dec_col * s_old + upd
    s_ref[...] = s_new

    @pl.when(j == n_t - 1)
    def _():
        ng_ref[...] = s_new

    go = z_ref[:, dm.go0:dm.go0 + dm.gw]
    ga = z_ref[:, dm.ga0:dm.ga0 + dm.d]
    gb = z_ref[:, dm.gb0:dm.gb0 + dm.d]
    x1, h2, logits = _post_mix(dm, x, mod, ya, o, go, ga, gb, gn_ref, wa_ref, wb_ref, wo_ref,
                               n2_ref, wrh_ref, wrl_ref)
    x1_ref[...] = x1
    h2_ref[...] = h2
    lg_ref[...] = logits


def _const_spec(shape):
    nd = len(shape)
    return pl.BlockSpec(shape, lambda *_: (0,) * nd, pipeline_mode=pl.Buffered(1))


def _mix_prompt(dm, x2d, mod3, wts, bsz, seq, n_exp):
    tt = 256 if seq % 256 == 0 else seq
    n_t = seq // tt
    n = bsz * seq
    d = dm.d
    kern = functools.partial(_mixp_kernel, dm=dm, tt=tt, n_t=n_t)
    row = lambda b, j: (b * n_t + j, 0)
    in_specs = [pl.BlockSpec((tt, d), row),
                pl.BlockSpec((None, 1, N_MOD * d), lambda b, j: (b, 0, 0))]
    in_specs += [_const_spec(w.shape) for w in wts]
    out_specs = [pl.BlockSpec((tt, d), row), pl.BlockSpec((tt, d // 2), row), pl.BlockSpec((tt, n_exp), row),
                 pl.BlockSpec((None, POOL_HIST - 1, dm.pw), lambda b, j: (b, 0, 0)),
                 pl.BlockSpec((None, dm.hk, dm.dv), lambda b, j: (b, 0, 0))]
    out_shape = [jax.ShapeDtypeStruct((n, d), F32), jax.ShapeDtypeStruct((n, d // 2), I32),
                 jax.ShapeDtypeStruct((n, n_exp), F32),
                 jax.ShapeDtypeStruct((bsz, POOL_HIST - 1, dm.pw), F32),
                 jax.ShapeDtypeStruct((bsz, dm.hk, dm.dv), F32)]
    scratch = [pltpu.VMEM((tt, dm.zw), F32), pltpu.VMEM((tt + POOL_HIST, dm.pw), F32),
               pltpu.VMEM((dm.hk, dm.dv), F32), pltpu.VMEM((tt, dm.hk), F32), pltpu.VMEM((tt, dm.gw), F32)]
    return pl.pallas_call(
        kern, grid=(bsz, n_t), in_specs=in_specs, out_specs=out_specs, out_shape=out_shape,
        scratch_shapes=scratch, compiler_params=_cp(("arbitrary", "arbitrary")), name="mixp",
    )(x2d, mod3, *wts)


def _sin_kernel(x_ref, mod_ref, n1_ref, win_ref, z_ref, *, d):
    mod = mod_ref[...]
    h = _rms(x_ref[...], n1_ref[...]) * (1 + mod[:, d:2 * d]) + mod[:, 0:d]
    z_ref[...] = _dot(h.astype(BF16), win_ref[...])


def _sample_in(dm, xs2d, mod_s, n1, win, bs, t_s):
    d = dm.d
    return pl.pallas_call(
        functools.partial(_sin_kernel, d=d),
        grid=(t_s,),
        in_specs=[pl.BlockSpec((bs, d), lambda t: (0, t)),
                  _const_spec(mod_s.shape), _const_spec(n1.shape), _const_spec(win.shape)],
        out_specs=pl.BlockSpec((bs, dm.zw), lambda t: (t, 0)),
        out_shape=jax.ShapeDtypeStruct((t_s * bs, dm.zw), F32),
        compiler_params=_cp(("arbitrary",)), name="sample_in",
    )(xs2d, mod_s, n1, win)


def _sst_kernel(z_ref, pool_ref, gs_ref, wal_ref, bal_ref,
                mixed_ref, o_ref, npool_ref, ngs_ref,
                qt_ref, kt_ref, dt_ref, vt_ref, ot_ref, *, dm, bs, t_s, ch, pos0):
    i = pl.program_id(0)
    n_i = pl.num_programs(0)
    scale = dm.dk ** -0.5
    past = POOL_HIST - 1

    @pl.when(i == 0)
    def _():
        for t in range(t_s):
            r0 = t * bs
            la = _log_decay(z_ref[r0:r0 + bs, dm.al0:dm.al0 + LANE], wal_ref, bal_ref)
            dt_ref[t] = jnp.exp(la).T
            qt_ref[t] = (z_ref[r0:r0 + bs, dm.q0:dm.q0 + dm.hk] * scale).T
            kt_ref[t] = z_ref[r0:r0 + bs, dm.k0:dm.k0 + dm.hk].T
            vt_ref[t] = z_ref[r0:r0 + bs, dm.v0:dm.v0 + dm.gw].T
        ot_ref[...] = jnp.zeros(ot_ref.shape, F32)

        def item(s, c0, c1):
            if s >= 0:
                return z_ref[s * bs:(s + 1) * bs, c0:c1]
            return pool_ref[:, (past + s) * dm.pw + c0:(past + s) * dm.pw + c1]

        gw = dm.pw // len(POOL_WINDOWS)
        for t in range(t_s):
            for g, w in enumerate(POOL_WINDOWS):
                c0, c1 = g * gw, (g + 1) * gw
                cur = item(t, c0, c1)
                acc = cur
                for r in range(1, w):
                    acc = acc + item(t - r, c0, c1)
                cnt = float(min(w, pos0 + t + 1))
                mixed_ref[t * bs:(t + 1) * bs, c0:c1] = acc / cnt - cur
        for r in range(past):
            npool_ref[:, r * dm.pw:(r + 1) * dm.pw] = item(r - past + t_s, 0, dm.pw)

    per_head = dm.dk // ch
    hoff = pl.multiple_of((i // per_head) * dm.dv, dm.dv)
    for p in range(ch):
        hd = i * ch + p
        st = gs_ref[:, p * dm.dv:(p + 1) * dm.dv].T
        for t in range(t_s):
            st = dt_ref[t, pl.ds(hd, 1), :] * st + kt_ref[t, pl.ds(hd, 1), :] * vt_ref[t, pl.ds(hoff, dm.dv), :]
            ot_ref[t, pl.ds(hoff, dm.dv), :] += qt_ref[t, pl.ds(hd, 1), :] * st
        ngs_ref[:, p * dm.dv:(p + 1) * dm.dv] = st.T

    @pl.when(i == n_i - 1)
    def _():
        for t in range(t_s):
            for h in range(dm.h):
                o_ref[t * bs:(t + 1) * bs, h * dm.dv:(h + 1) * dm.dv] = ot_ref[t, h * dm.dv:(h + 1) * dm.dv, :].T


def _sample_state(dm, z_s, pool2d, gs2d, wal, bal, bs, t_s):
    ch = 16
    n_i = dm.hk // ch
    past = POOL_HIST - 1
    kern = functools.partial(_sst_kernel, dm=dm, bs=bs, t_s=t_s, ch=ch, pos0=PAST_LEN)
    full = lambda shape: pl.BlockSpec(shape, lambda i: (0,) * len(shape))
    return pl.pallas_call(
        kern, grid=(n_i,),
        in_specs=[_const_spec(z_s.shape), _const_spec(pool2d.shape),
                  pl.BlockSpec((bs, ch * dm.dv), lambda i: (0, i)),
                  _const_spec(wal.shape), _const_spec(bal.shape)],
        out_specs=[full((t_s * bs, dm.pw)), full((t_s * bs, dm.gw)), full((bs, past * dm.pw)),
                   pl.BlockSpec((bs, ch * dm.dv), lambda i: (0, i))],
        out_shape=[jax.ShapeDtypeStruct((t_s * bs, dm.pw), F32), jax.ShapeDtypeStruct((t_s * bs, dm.gw), F32),
                   jax.ShapeDtypeStruct((bs, past * dm.pw), F32), jax.ShapeDtypeStruct(gs2d.shape, F32)],
        scratch_shapes=[pltpu.VMEM((t_s, dm.hk, bs), F32), pltpu.VMEM((t_s, dm.hk, bs), F32),
                        pltpu.VMEM((t_s, dm.hk, bs), F32), pltpu.VMEM((t_s, dm.gw, bs), F32),
                        pltpu.VMEM((t_s, dm.gw, bs), F32)],
        compiler_params=_cp(("arbitrary",)), name="sample_state",
    )(z_s, pool2d, gs2d, wal, bal)


def _spost_kernel(x_ref, mod_ref, z_ref, mixed_ref, o_ref, pw_ref, ps_ref, gn_ref,
                  wa_ref, wb_ref, wo_ref, n2_ref, wrh_ref, wrl_ref,
                  x1_ref, h2_ref, lg_ref, *, dm):
    gw = dm.pw // len(POOL_WINDOWS)
    ya = _pool_project([mixed_ref[:, g * gw:(g + 1) * gw] for g in range(len(POOL_WINDOWS))], pw_ref, ps_ref)
    x1, h2, logits = _post_mix(
        dm, x_ref[...], mod_ref[...], ya, o_ref[...],
        z_ref[:, dm.go0:dm.go0 + dm.gw], z_ref[:, dm.ga0:dm.ga0 + dm.d], z_ref[:, dm.gb0:dm.gb0 + dm.d],
        gn_ref, wa_ref, wb_ref, wo_ref, n2_ref, wrh_ref, wrl_ref)
    x1_ref[...] = x1
    h2_ref[...] = h2
    lg_ref[...] = logits


def _sample_post(dm, xs2d, mod_s, z_s, mixed_s, o_s, wts, bs, t_s, n_exp):
    d = dm.d
    rows = lambda t: (t, 0)
    in_specs = [pl.BlockSpec((bs, d), lambda t: (0, t)), _const_spec(mod_s.shape),
                pl.BlockSpec((bs, dm.zw), rows), pl.BlockSpec((bs, dm.pw), rows), pl.BlockSpec((bs, dm.gw), rows)]
    in_specs += [_const_spec(w.shape) for w in wts]
    return pl.pallas_call(
        functools.partial(_spost_kernel, dm=dm), grid=(t_s,), in_specs=in_specs,
        out_specs=[pl.BlockSpec((bs, d), rows), pl.BlockSpec((bs, d // 2), rows), pl.BlockSpec((bs, n_exp), rows)],
        out_shape=[jax.ShapeDtypeStruct((t_s * bs, d), F32), jax.ShapeDtypeStruct((t_s * bs, d // 2), I32),
                   jax.ShapeDtypeStruct((t_s * bs, n_exp), F32)],
        compiler_params=_cp(("arbitrary",)), name="sample_post",
    )(xs2d, mod_s, z_s, mixed_s, o_s, *wts)


def _route_kernel(lgp_ref, lgs_ref, bias_ref, idx_ref, w_ref, cnt_ref, *, n_exp, tiles_p):
    i = pl.program_id(0)
    s = jax.nn.sigmoid(jnp.where(i < tiles_p, lgp_ref[...], lgs_ref[...]))
    cur = s + bias_ref[...]
    tr = s.shape[0]
    lane = lax.broadcasted_iota(I32, (1, n_exp), 1).astype(F32)
    slot = lax.broadcasted_iota(I32, (1, LANE), 1)
    idx8 = jnp.zeros((tr, LANE), F32)
    w8 = jnp.zeros((tr, LANE), F32)
    wsum = jnp.zeros((tr, 1), F32)
    picked = jnp.zeros((tr, n_exp), F32)
    for k in range(TOP_K):
        m = jnp.max(cur, axis=-1, keepdims=True)
        am = jnp.min(jnp.where(cur == m, lane, float(n_exp)), axis=-1, keepdims=True)
        oh = lane == am
        sk = jnp.sum(jnp.where(oh, s, 0.0), axis=-1, keepdims=True)
        cur = jnp.where(oh, -jnp.inf, cur)
        idx8 = jnp.where(slot == k, am, idx8)
        w8 = jnp.where(slot == k, sk, w8)
        wsum = wsum + sk
        picked = picked + jnp.where(oh, 1.0, 0.0)
    idx_ref[...] = idx8.astype(I32)
    w_ref[...] = w8 / wsum * ROUTED_SCALE

    @pl.when(i == 0)
    def _():
        cnt_ref[...] = jnp.zeros(cnt_ref.shape, F32)

    cnt_ref[...] += jnp.sum(picked, axis=0, keepdims=True)


def _two_source_specs(block, tiles_p, tiles_s):
    return [pl.BlockSpec(block, lambda i: (jnp.minimum(i, tiles_p - 1), 0)),
            pl.BlockSpec(block, lambda i: (jnp.clip(i - tiles_p, 0, tiles_s - 1), 0))]


def _route(lg_p, lg_s, bias, tr):
    n_exp = lg_p.shape[1]
    tiles_p, tiles_s = lg_p.shape[0] // tr, lg_s.shape[0] // tr
    n = (tiles_p + tiles_s) * tr
    rows = lambda i: (i, 0)
    return pl.pallas_call(
        functools.partial(_route_kernel, n_exp=n_exp, tiles_p=tiles_p), grid=(tiles_p + tiles_s,),
        in_specs=_two_source_specs((tr, n_exp), tiles_p, tiles_s) + [_const_spec((1, n_exp))],
        out_specs=[pl.BlockSpec((tr, LANE), rows), pl.BlockSpec((tr, LANE), rows),
                   pl.BlockSpec((8, n_exp), lambda i: (0, 0))],
        out_shape=[jax.ShapeDtypeStruct((n, LANE), I32), jax.ShapeDtypeStruct((n, LANE), F32),
                   jax.ShapeDtypeStruct((8, n_exp), F32)],
        compiler_params=_cp(("arbitrary",)), name="route",
    )(lg_p, lg_s, bias.reshape(1, n_exp))


def _plan_kernel(idx_ref, cnt_ref, pos_ref, bounds_ref, start_ref, run_ref, *, n_exp):
    i = pl.program_id(0)
    lane = lax.broadcasted_iota(I32, (1, n_exp), 1)

    @pl.when(i == 0)
    def _():
        cnt = cnt_ref[...].astype(I32)
        padded = ((cnt + (EXPERT_BLOCK - 1)) // EXPERT_BLOCK) * EXPERT_BLOCK
        pe = padded
        s = 1
        while s < n_exp:
            pe = pe + jnp.where(lane >= s, pltpu.roll(pe, s, 1), 0)
            s *= 2
        start_ref[...] = (pe - padded).astype(F32)
        run_ref[...] = jnp.zeros(run_ref.shape, F32)
        row = lax.broadcasted_iota(I32, (8, n_exp), 0)
        bounds_ref[...] = jnp.where(row == 0, pe - padded, pe)

    idx = idx_ref[...]
    tr = idx.shape[0]
    onehots = [lane == idx[:, k:k + 1] for k in range(TOP_K)]
    member = onehots[0]
    for k in range(1, TOP_K):
        member = member | onehots[k]
    mb = jnp.where(member, 1.0, 0.0).astype(BF16)
    r_i = lax.broadcasted_iota(I32, (tr, tr), 0)
    c_i = lax.broadcasted_iota(I32, (tr, tr), 1)
    earlier = jnp.where(r_i > c_i, 1.0, 0.0).astype(BF16)
    dest = _dot(earlier, mb) + (start_ref[0:1, :] + run_ref[0:1, :])
    slot = lax.broadcasted_iota(I32, (1, LANE), 1)
    pos8 = jnp.zeros((tr, LANE), F32)
    for k in range(TOP_K):
        pk = jnp.sum(jnp.where(onehots[k], dest, 0.0), axis=-1, keepdims=True)
        pos8 = jnp.where(slot == k, pk, pos8)
    pos_ref[...] = pos8.T[0:TOP_K, :].astype(I32)
    run_ref[...] += jnp.sum(mb.astype(F32), axis=0, keepdims=True)


def _plan(idx8, counts, tr):
    n = idx8.shape[0]
    n_exp = counts.shape[1]
    rows = lambda i: (i, 0)
    return pl.pallas_call(
        functools.partial(_plan_kernel, n_exp=n_exp), grid=(n // tr,),
        in_specs=[pl.BlockSpec((tr, LANE), rows), _const_spec((8, n_exp))],
        out_specs=[pl.BlockSpec((TOP_K, tr), lambda i: (0, i)), pl.BlockSpec((8, n_exp), lambda i: (0, 0))],
        out_shape=[jax.ShapeDtypeStruct((TOP_K, n), I32), jax.ShapeDtypeStruct((8, n_exp), I32)],
        scratch_shapes=[pltpu.VMEM((8, n_exp), F32), pltpu.VMEM((8, n_exp), F32)],
        compiler_params=_cp(("arbitrary",)), name="plan",
    )(idx8, counts)


SC_CORES = 2
SC_SUBCORES = 16
SC_AXES = ("core", "subcore")
SC_WORKERS = SC_CORES * SC_SUBCORES
SC_WINDOW = 128


def _sc_mesh():
    return plsc.VectorSubcoreMesh(core_axis_name=SC_AXES[0], subcore_axis_name=SC_AXES[1],
                                  num_cores=SC_CORES, num_subcores=SC_SUBCORES)


def _sc_worker():
    return lax.axis_index(SC_AXES[1]) * SC_CORES + lax.axis_index(SC_AXES[0])


def _dispatch(h_p, h_s, pos_t, n_rows):
    n_p, c = h_p.shape
    n_s = h_s.shape[0]
    win_p, win_s = n_p // SC_WINDOW, n_s // SC_WINDOW
    assert n_p % (SC_WINDOW * SC_WORKERS) == 0 and n_s % SC_WINDOW == 0 and win_s <= SC_WORKERS

    @functools.partial(
        pl.kernel, out_type=jax.ShapeDtypeStruct((n_rows, c), h_p.dtype), mesh=_sc_mesh(), name="dispatch",
        scratch_types=[pltpu.VMEM((TOP_K, SC_WINDOW), I32), pltpu.VMEM((SC_WINDOW, c), h_p.dtype)])
    def run(hp_hbm, hs_hbm, pos_hbm, xs_hbm, idx_v, rows_v):
        wid = _sc_worker()

        def scatter_window(h_hbm, row0, tok0):
            pltpu.sync_copy(h_hbm.at[pl.ds(row0, SC_WINDOW)], rows_v)
            pltpu.sync_copy(pos_hbm.at[:, pl.ds(tok0, SC_WINDOW)], idx_v)
            for k in range(TOP_K):
                pltpu.sync_copy(rows_v, xs_hbm.at[idx_v.at[k]])

        def prompt_window(j, carry):
            row0 = pl.multiple_of((wid * (win_p // SC_WORKERS) + j) * SC_WINDOW, SC_WINDOW)
            scatter_window(hp_hbm, row0, row0)
            return carry
        lax.fori_loop(0, win_p // SC_WORKERS, prompt_window, 0)

        @pl.when(wid < win_s)
        def _():
            row0 = pl.multiple_of(wid * SC_WINDOW, SC_WINDOW)
            scatter_window(hs_hbm, row0, n_p + row0)

    return run(h_p, h_s, pos_t)


def _combine_gather(ys, pos_t):
    k_n = pos_t.shape[0] * pos_t.shape[1]
    c = ys.shape[1]
    per_worker = k_n // (SC_WINDOW * SC_WORKERS)
    assert k_n % (SC_WINDOW * SC_WORKERS) == 0

    @functools.partial(
        pl.kernel, out_type=jax.ShapeDtypeStruct((k_n, c), ys.dtype), mesh=_sc_mesh(), name="combine_gather",
        scratch_types=[pltpu.VMEM((SC_WINDOW,), I32), pltpu.VMEM((SC_WINDOW, c), ys.dtype)])
    def run(ys_hbm, idx_hbm, out_hbm, idx_v, rows_v):
        wid = _sc_worker()

        def window(j, carry):
            row0 = pl.multiple_of((wid * per_worker + j) * SC_WINDOW, SC_WINDOW)
            pltpu.sync_copy(idx_hbm.at[pl.ds(row0, SC_WINDOW)], idx_v)
            pltpu.sync_copy(ys_hbm.at[idx_v], rows_v)
            pltpu.sync_copy(rows_v, out_hbm.at[pl.ds(row0, SC_WINDOW)])
            return carry
        lax.fori_loop(0, per_worker, window, 0)

    return run(ys, pos_t.reshape(k_n))


MOE_RING = 6


def _moe_kernel(st_ref, wg_ref, wu_ref, wd_ref, xs_ref, ys_ref, xbuf, ybuf, wgb, wub, wdb, semx, semy):
    e = pl.program_id(0)
    n_e = pl.num_programs(0)
    g0 = st_ref[e] // EXPERT_BLOCK
    nchunks = st_ref[e + 1] // EXPERT_BLOCK - g0
    ntot = st_ref[n_e] // EXPERT_BLOCK

    def rows(g):
        return pl.ds(pl.multiple_of(g * EXPERT_BLOCK, EXPERT_BLOCK), EXPERT_BLOCK)

    def x_copy(g):
        slot = g % MOE_RING
        return pltpu.make_async_copy(xs_ref.at[rows(g)], xbuf.at[slot], semx.at[slot])

    def y_copy(g):
        slot = g % MOE_RING
        return pltpu.make_async_copy(ybuf.at[slot], ys_ref.at[rows(g)], semy.at[slot])

    @pl.when(e == 0)
    def _():
        for g in range(MOE_RING - 1):
            @pl.when(g < ntot)
            def _(g=g):
                x_copy(g).start()

    wgb[...] = wg_ref[...].astype(BF16)
    wub[...] = wu_ref[...].astype(BF16)
    wdb[...] = wd_ref[...].astype(BF16)

    def body(c, carry):
        g = g0 + c
        slot = g % MOE_RING

        @pl.when(g + (MOE_RING - 1) < ntot)
        def _():
            x_copy(g + (MOE_RING - 1)).start()

        x_copy(g).wait()

        @pl.when(g >= MOE_RING)
        def _():
            y_copy(g - MOE_RING).wait()

        x = _unpack_rows(xbuf[slot]).astype(BF16)
        mid = _silu(_dot(x, wgb[...])) * _dot(x, wub[...])
        ybuf[slot] = _pack_rows(_dot(mid.astype(BF16), wdb[...]))
        y_copy(g).start()
        return carry

    lax.fori_loop(0, nchunks, body, 0)

    @pl.when(e == n_e - 1)
    def _():
        for j in range(MOE_RING):
            g = ntot - MOE_RING + j

            @pl.when(g >= 0)
            def _(g=g):
                y_copy(g).wait()


def _moe(starts, xs, wg, wu, wd):
    n_rows, c = xs.shape
    n_exp, d, f = wg.shape
    grid_spec = pltpu.PrefetchScalarGridSpec(
        num_scalar_prefetch=1, grid=(n_exp,),
        in_specs=[pl.BlockSpec((None, d, f), lambda e, st: (e, 0, 0)),
                  pl.BlockSpec((None, d, f), lambda e, st: (e, 0, 0)),
                  pl.BlockSpec((None, f, d), lambda e, st: (e, 0, 0)),
                  pl.BlockSpec(memory_space=pl.ANY)],
        out_specs=pl.BlockSpec(memory_space=pl.ANY),
        scratch_shapes=[pltpu.VMEM((MOE_RING, EXPERT_BLOCK, c), I32), pltpu.VMEM((MOE_RING, EXPERT_BLOCK, c), I32),
                        pltpu.VMEM((d, f), BF16), pltpu.VMEM((d, f), BF16), pltpu.VMEM((f, d), BF16),
                        pltpu.SemaphoreType.DMA((MOE_RING,)), pltpu.SemaphoreType.DMA((MOE_RING,))])
    return pl.pallas_call(
        _moe_kernel, grid_spec=grid_spec, out_shape=jax.ShapeDtypeStruct((n_rows, c), I32),
        compiler_params=_cp(("arbitrary",)), name="moe",
    )(starts, wg, wu, wd, xs)


def _final_kernel(yg_ref, w8_ref, x1_ref, h2_ref, mod_ref, wsg_ref, wsu_ref, wsd_ref, nf_ref, y_ref, *, d):
    hb = _unpack_rows(h2_ref[...]).astype(BF16)
    shared = _dot((_silu(_dot(hb, wsg_ref[...])) * _dot(hb, wsu_ref[...])).astype(BF16), wsd_ref[...])
    w8 = w8_ref[...]
    routed = _unpack_rows(yg_ref[0]) * w8[:, 0:1]
    for k in range(1, TOP_K):
        routed = routed + _unpack_rows(yg_ref[k]) * w8[:, k:k + 1]
    g2 = mod_ref[...][:, (N_MOD - 1) * d:N_MOD * d]
    x2 = x1_ref[...] + g2 * (routed + shared)
    y_ref[...] = _rms(x2, nf_ref[...])


def _final(yg3, w8, x1, h2, mod, mod_spec, wts, nf, tf, n_tiles, tile0, out_shape, out_spec):
    d = x1.shape[1]
    rows = lambda i: (i, 0)
    in_specs = [pl.BlockSpec((TOP_K, tf, d // 2), lambda i: (0, tile0 + i, 0)),
                pl.BlockSpec((tf, LANE), lambda i: (tile0 + i, 0)),
                pl.BlockSpec((tf, d), rows), pl.BlockSpec((tf, d // 2), rows), mod_spec]
    in_specs += [_const_spec(w.shape) for w in wts]
    in_specs += [_const_spec(nf.shape)]
    return pl.pallas_call(
        functools.partial(_final_kernel, d=d), grid=(n_tiles,), in_specs=in_specs,
        out_specs=out_spec, out_shape=out_shape,
        compiler_params=_cp(("arbitrary",)), name="final",
    )(yg3, w8, x1, h2, mod, *wts, nf)


def kernel(x_prompt, x_sample, c_prompt, c_sample, state_pool, state_gla, w_ada, b_ada, norm1, w_in,
           pool_w, pool_scale, gla_w_alpha, gla_b_alpha, gla_norm, w_branch_a, w_branch_b, w_out, norm2,
           w_router, router_bias, w_exp_gate, w_exp_up, w_exp_down, w_sh_gate, w_sh_up, w_sh_down, norm_f):
    depth = w_in.shape[0]
    assert depth == 1, "single-layer trunk"
    bp, seq, d = x_prompt.shape
    bs, t_s, _ = x_sample.shape
    _, _, past, pw = state_pool.shape
    _, _, heads, dk, dv = state_gla.shape
    rank = gla_w_alpha.shape[1]
    n_exp = w_router.shape[2]
    assert past == POOL_HIST - 1 and pw == len(POOL_WINDOWS) * LANE and dv == LANE and rank <= LANE
    dm = _Dims(d, pw, heads, dk, dv)
    n_p, n_s = bp * seq, bs * t_s
    n_all = n_p + n_s

    off = [0]
    for sz in (pw, dm.hk, dm.hk, dm.gw, dm.gw, rank, d, d):
        off.append(off[-1] + sz)
    wi = w_in[0]
    win = jnp.concatenate([wi[:, :off[5]], wi[:, off[6]:], wi[:, off[5]:off[6]],
                           jnp.zeros((d, LANE - rank), F32)], axis=1).astype(BF16)
    wal = jnp.concatenate([gla_w_alpha[0], jnp.zeros((LANE - rank, dm.hk), F32)], axis=0).astype(BF16)
    bal = gla_b_alpha[0].reshape(1, dm.hk)
    wr = w_router[0]
    wrh = wr.astype(BF16)
    wrl = (wr - wrh.astype(F32)).astype(BF16)
    post_w = (pool_w[0].astype(BF16), pool_scale[0].reshape(1, pw), gla_norm[0].reshape(1, dm.gw),
              w_branch_a[0].astype(BF16), w_branch_b[0].astype(BF16), w_out[0].astype(BF16),
              norm2[0].reshape(1, d), wrh, wrl)
    n1 = norm1[0].reshape(1, d)

    mod = _ada(jnp.concatenate([c_prompt, c_sample], axis=0), w_ada[0], b_ada[0])
    mod_p = mod[:bp].reshape(bp, 1, N_MOD * d)
    mod_s = mod[bp:]

    x1_p, h2_p, lg_p, npool_p, ngla_p = _mix_prompt(
        dm, x_prompt.reshape(n_p, d), mod_p, (n1, win, wal, bal) + post_w, bp, seq, n_exp)

    xs2d = x_sample.reshape(bs, t_s * d)
    z_s = _sample_in(dm, xs2d, mod_s, n1, win, bs, t_s)
    mixed_s, o_s, npool_s, ngla_s = _sample_state(
        dm, z_s, state_pool[0].reshape(bs, past * pw), state_gla[0].reshape(bs, dm.hk * dv), wal, bal, bs, t_s)
    x1_s, h2_s, lg_s = _sample_post(dm, xs2d, mod_s, z_s, mixed_s, o_s, post_w, bs, t_s, n_exp)

    tr = 512 if (n_p % 512 == 0 and n_s % 512 == 0) else LANE
    assert n_p % tr == 0 and n_s % tr == 0 and n_p % bs == 0
    idx8, w8, counts = _route(lg_p, lg_s, router_bias[0], tr)
    nblk = (n_all * TOP_K) // EXPERT_BLOCK + n_exp
    pos_t, bounds = _plan(idx8, counts, tr)
    starts = jnp.concatenate([bounds[0], bounds[1, n_exp - 1:]])

    xs = _dispatch(h2_p, h2_s, pos_t, nblk * EXPERT_BLOCK)
    ys = _moe(starts, xs, w_exp_gate[0], w_exp_up[0], w_exp_down[0])
    yg3 = _combine_gather(ys, pos_t).reshape(TOP_K, n_all, d // 2)

    sh_w = (w_sh_gate[0].astype(BF16), w_sh_up[0].astype(BF16), w_sh_down[0].astype(BF16))
    nf = norm_f.reshape(1, d)
    tf = 256 if seq % 256 == 0 else seq
    n_tp = seq // tf
    y_p = _final(
        yg3, w8, x1_p, h2_p, mod_p,
        pl.BlockSpec((None, 1, N_MOD * d), lambda i: (i // n_tp, 0, 0)), sh_w, nf, tf, n_p // tf, 0,
        jax.ShapeDtypeStruct((n_p, d), F32), pl.BlockSpec((tf, d), lambda i: (i, 0)))
    y_s = _final(
        yg3, w8, x1_s, h2_s, mod_s,
        _const_spec(mod_s.shape), sh_w, nf, bs, t_s, n_p // bs,
        jax.ShapeDtypeStruct((bs, t_s * d), F32), pl.BlockSpec((bs, d), lambda i: (0, i)))

    return (y_p.reshape(bp, seq, d), y_s.reshape(bs, t_s, d),
            npool_p.reshape(depth, bp, past, pw), ngla_p.reshape(depth, bp, heads, dk, dv),
            npool_s.reshape(depth, bs, past, pw), ngla_s.reshape(depth, bs, heads, dk, dv))
```

```python
import functools

import jax
import jax.numpy as jnp
from jax import lax
from jax.experimental import pallas as pl
from jax.experimental.pallas import tpu as pltpu
from jax.experimental.pallas import tpu_sc as plsc

F32 = jnp.float32
BF16 = jnp.bfloat16
I32 = jnp.int32

EPS = 1e-6
N_MOD = 6
POOL_WINDOWS = (2, 4, 8, 16)
POOL_HIST = 16
GLA_GATE_TEMP = 16.0
TOP_K = 8
ROUTED_SCALE = 2.5
PAST_LEN = 16384
EXPERT_BLOCK = 128
LANE = 128
GLA_FAST_MAX_DECAY = 40.0
VMEM_LIMIT = 56 * 1024 * 1024


def _cp(sem, vmem=VMEM_LIMIT):
    return pltpu.CompilerParams(dimension_semantics=sem, vmem_limit_bytes=vmem)


def _rms(x, g):
    return x * lax.rsqrt(jnp.mean(x * x, axis=-1, keepdims=True) + EPS) * g


def _silu(x):
    return x * jax.nn.sigmoid(x)


def _dot(a, b):
    return jnp.dot(a, b, preferred_element_type=F32)


def _split3(a):
    a0 = a.astype(BF16)
    r1 = a - a0.astype(F32)
    a1 = r1.astype(BF16)
    a2 = (r1 - a1.astype(F32)).astype(BF16)
    return a0, a1, a2


def _mod_parts(mod, d):
    return [mod[:, i * d:(i + 1) * d] for i in range(N_MOD)]


_HI16 = -65536


def _pack_rows(v):
    c = v.shape[1] // 2
    lo = lax.bitcast_convert_type(v[:, :c].astype(BF16).astype(F32), I32)
    hi = lax.bitcast_convert_type(v[:, c:].astype(BF16).astype(F32), I32)
    return (hi & _HI16) | lax.shift_right_logical(lo, 16)


def _unpack_rows(w):
    lo = lax.bitcast_convert_type(lax.shift_left(w, 16), F32)
    hi = lax.bitcast_convert_type(w & _HI16, F32)
    return jnp.concatenate([lo, hi], axis=1)


def _ada_kernel(c_ref, w_ref, b_ref, o_ref):
    a = _silu(c_ref[...]).astype(BF16)
    o_ref[...] = _dot(a, w_ref[...].astype(BF16)) + b_ref[...]


def _ada(c_all, w_ada, b_ada):
    n, d = c_all.shape
    cols = w_ada.shape[1]
    bc = d
    return pl.pallas_call(
        _ada_kernel,
        grid=(cols // bc,),
        in_specs=[pl.BlockSpec((n, d), lambda i: (0, 0)),
                  pl.BlockSpec((d, bc), lambda i: (0, i)),
                  pl.BlockSpec((1, bc), lambda i: (0, i))],
        out_specs=pl.BlockSpec((n, bc), lambda i: (0, i)),
        out_shape=jax.ShapeDtypeStruct((n, cols), F32),
        compiler_params=_cp(("arbitrary",)),
        name="ada",
    )(c_all, w_ada, b_ada.reshape(1, cols))


class _Dims:
    def __init__(self, d, pw, heads, dk, dv):
        self.d, self.pw, self.h, self.dk, self.dv = d, pw, heads, dk, dv
        self.hk, self.gw = heads * dk, heads * dv
        self.q0 = pw
        self.k0 = self.q0 + self.hk
        self.v0 = self.k0 + self.hk
        self.go0 = self.v0 + self.gw
        self.ga0 = self.go0 + self.gw
        self.gb0 = self.ga0 + d
        self.al0 = self.gb0 + d
        self.zw = self.al0 + LANE


def _log_decay(z_al, wal_ref, bal_ref):
    xal = _dot(z_al.astype(BF16), wal_ref[...]) + bal_ref[...]
    return jax.nn.log_sigmoid(xal) / GLA_GATE_TEMP


def _pool_project(mixed_groups, pw_ref, ps_ref):
    ys = [_dot(m.astype(BF16), pw_ref[g]) for g, m in enumerate(mixed_groups)]
    return jnp.concatenate(ys, axis=1) * ps_ref[...]


def _post_mix(dm, x, mod, ya, o, go, ga, gb, gn_ref, wa_ref, wb_ref, wo_ref, n2_ref, wrh_ref, wrl_ref):
    _, _, g1, sh2, sc2, _ = _mod_parts(mod, dm.d)
    parts = []
    for h in range(dm.h):
        oh = o[:, h * dm.dv:(h + 1) * dm.dv]
        parts.append(oh * lax.rsqrt(jnp.mean(oh * oh, axis=-1, keepdims=True) + EPS))
    yb = jnp.concatenate(parts, axis=1) * gn_ref[...] * _silu(go)
    m = (jax.nn.sigmoid(ga) * _dot(ya.astype(BF16), wa_ref[...])
         + jax.nn.sigmoid(gb) * _dot(yb.astype(BF16), wb_ref[...]))
    x1 = x + g1 * _dot(m.astype(BF16), wo_ref[...])
    h2 = _rms(x1, n2_ref[...]) * (1 + sc2) + sh2
    hi = h2.astype(BF16)
    lo = (h2 - hi.astype(F32)).astype(BF16)
    logits = _dot(hi, wrh_ref[...]) + (_dot(hi, wrl_ref[...]) + _dot(lo, wrh_ref[...]))
    return x1, _pack_rows(h2), logits


def _in_project(dm, x, mod, n1_ref, win_refs, z_ref):
    h = (_rms(x, n1_ref[...]) * (1 + mod[:, dm.d:2 * dm.d]) + mod[:, 0:dm.d]).astype(BF16)
    c0 = 0
    for w_ref in win_refs:
        z_ref[:, c0:c0 + w_ref.shape[1]] = _dot(h, w_ref[...])
        c0 += w_ref.shape[1]


def _mixp_kernel(x_ref, mod_ref, n1_ref, wia_ref, wig_ref, wil_ref, wal_ref, bal_ref, pw_ref, ps_ref, gn_ref,
                 wa_ref, wb_ref, wo_ref, n2_ref, wrh_ref, wrl_ref,
                 x1_ref, h2_ref, lg_ref, np_ref, ng_ref,
                 z_ref, uext_ref, s_ref, b_ref, oi_ref, *, dm, tt, n_t):
    j = pl.program_id(1)
    x = x_ref[...]
    mod = mod_ref[...]
    _in_project(dm, x, mod, n1_ref, (wia_ref, wig_ref, wil_ref), z_ref)

    @pl.when(j == 0)
    def _():
        uext_ref[0:POOL_HIST, :] = jnp.zeros((POOL_HIST, dm.pw), F32)
        s_ref[...] = jnp.zeros(s_ref.shape, F32)

    uext_ref[POOL_HIST:POOL_HIST + tt, :] = z_ref[:, 0:dm.pw]
    pos = j * tt + lax.broadcasted_iota(I32, (tt, 1), 0)
    gw = dm.pw // len(POOL_WINDOWS)
    mixed = []
    for g, w in enumerate(POOL_WINDOWS):
        c0 = g * gw
        cur = uext_ref[POOL_HIST:POOL_HIST + tt, c0:c0 + gw]
        acc = cur
        for i in range(1, w):
            acc = acc + uext_ref[POOL_HIST - i:POOL_HIST - i + tt, c0:c0 + gw]
        cnt = jnp.minimum(w, pos + 1).astype(F32)
        mixed.append(acc / cnt - cur)
    ya = _pool_project(mixed, pw_ref, ps_ref)

    @pl.when(j == n_t - 1)
    def _():
        np_ref[...] = uext_ref[tt + 1:tt + POOL_HIST, :]

    uext_ref[0:POOL_HIST, :] = uext_ref[tt:tt + POOL_HIST, :]

    la = _log_decay(z_ref[:, dm.al0:dm.al0 + LANE], wal_ref, bal_ref)
    r_i = lax.broadcasted_iota(I32, (tt, tt), 0)
    c_i = lax.broadcasted_iota(I32, (tt, tt), 1)
    causal = r_i >= c_i
    tri = jnp.where(causal, 1.0, 0.0).astype(BF16)
    a0, a1, a2 = _split3(la)
    bc = _dot(tri, a0) + (_dot(tri, a1) + _dot(tri, a2))
    b_ref[...] = bc
    q = z_ref[:, dm.q0:dm.q0 + dm.hk]
    k = z_ref[:, dm.k0:dm.k0 + dm.hk]
    v = z_ref[:, dm.v0:dm.v0 + dm.gw]
    scale = dm.dk ** -0.5
    qe = (q * scale) * jnp.exp(bc)
    blast = bc[tt - 1:tt, :]
    kd = k * jnp.exp(blast - bc)
    s_old = s_ref[...]
    s_bf = s_old.astype(BF16)
    vb = v.astype(BF16)
    lane = lax.broadcasted_iota(I32, (1, dm.hk), 1)
    head_masks = [(lane >= h * dm.dk) & (lane < (h + 1) * dm.dk) for h in range(dm.h)]
    o_inter = [_dot(jnp.where(head_masks[h], qe, 0.0).astype(BF16), s_bf) for h in range(dm.h)]

    fast = jnp.max(-blast) <= GLA_FAST_MAX_DECAY

    @pl.when(fast)
    def _():
        ke = (k * jnp.exp(-bc)).astype(BF16)
        for h in range(dm.h):
            qh = jnp.where(head_masks[h], qe, 0.0).astype(BF16)
            sc = lax.dot_general(qh, ke, (((1,), (1,)), ((), ())), preferred_element_type=F32)
            sc = jnp.where(causal, sc, 0.0).astype(BF16)
            oi_ref[:, h * dm.dv:(h + 1) * dm.dv] = _dot(sc, vb[:, h * dm.dv:(h + 1) * dm.dv])

    @pl.when(jnp.logical_not(fast))
    def _():
        rows = lax.broadcasted_iota(I32, (tt, 1), 0)
        for h in range(dm.h):
            def body(tb, carry, h=h):
                base = pl.multiple_of(tb * 8, 8)
                q8 = z_ref[pl.ds(base, 8), dm.q0:dm.q0 + dm.hk] * scale
                b8 = b_ref[pl.ds(base, 8), :]
                vh = z_ref[:, dm.v0 + h * dm.dv:dm.v0 + (h + 1) * dm.dv]
                out_rows = []
                for r in range(8):
                    keep = (rows <= base + r) & head_masks[h]
                    dec = jnp.exp(jnp.where(keep, b8[r:r + 1, :] - b_ref[...], -jnp.inf))
                    wgt = (q8[r:r + 1, :] * dec) * z_ref[:, dm.k0:dm.k0 + dm.hk]
                    s_col = jnp.sum(wgt, axis=-1, keepdims=True)
                    out_rows.append(jnp.sum(s_col * vh, axis=0, keepdims=True))
                oi_ref[pl.ds(base, 8), h * dm.dv:(h + 1) * dm.dv] = jnp.concatenate(out_rows, axis=0)
                return carry
            lax.fori_loop(0, tt // 8, body, 0)

    o = jnp.concatenate(o_inter, axis=1) + oi_ref[...]

    u_all = _dot(kd.T.astype(BF16), vb)
    dec_col = jnp.broadcast_to(jnp.exp(blast), (dm.hk, dm.hk)).T[:, 0:dm.dv]
    upd = jnp.concatenate(
        [u_all[h * dm.dk:(h + 1) * dm.dk, h * dm.dv:(h + 1) * dm.dv] for h in range(dm.h)], axis=0)
    s_new = dec_col * s_old + upd
    s_ref[...] = s_new

    @pl.when(j == n_t - 1)
    def _():
        ng_ref[...] = s_new

    go = z_ref[:, dm.go0:dm.go0 + dm.gw]
    ga = z_ref[:, dm.ga0:dm.ga0 + dm.d]
    gb = z_ref[:, dm.gb0:dm.gb0 + dm.d]
    x1, h2, logits = _post_mix(dm, x, mod, ya, o, go, ga, gb, gn_ref, wa_ref, wb_ref, wo_ref,
                               n2_ref, wrh_ref, wrl_ref)
    x1_ref[...] = x1
    h2_ref[...] = h2
    lg_ref[...] = logits


def _const_spec(shape):
    nd = len(shape)
    return pl.BlockSpec(shape, lambda *_: (0,) * nd, pipeline_mode=pl.Buffered(1))


def _mix_prompt(dm, x2d, mod3, wts, bsz, seq, n_exp):
    tt = 256 if seq % 256 == 0 else seq
    n_t = seq // tt
    n = bsz * seq
    d = dm.d
    kern = functools.partial(_mixp_kernel, dm=dm, tt=tt, n_t=n_t)
    row = lambda b, j: (b * n_t + j, 0)
    in_specs = [pl.BlockSpec((tt, d), row),
                pl.BlockSpec((None, 1, N_MOD * d), lambda b, j: (b, 0, 0))]
    in_specs += [_const_spec(w.shape) for w in wts]
    out_specs = [pl.BlockSpec((tt, d), row), pl.BlockSpec((tt, d // 2), row), pl.BlockSpec((tt, n_exp), row),
                 pl.BlockSpec((None, POOL_HIST - 1, dm.pw), lambda b, j: (b, 0, 0)),
                 pl.BlockSpec((None, dm.hk, dm.dv), lambda b, j: (b, 0, 0))]
    out_shape = [jax.ShapeDtypeStruct((n, d), F32), jax.ShapeDtypeStruct((n, d // 2), I32),
                 jax.ShapeDtypeStruct((n, n_exp), F32),
                 jax.ShapeDtypeStruct((bsz, POOL_HIST - 1, dm.pw), F32),
                 jax.ShapeDtypeStruct((bsz, dm.hk, dm.dv), F32)]
    scratch = [pltpu.VMEM((tt, dm.zw), F32), pltpu.VMEM((tt + POOL_HIST, dm.pw), F32),
               pltpu.VMEM((dm.hk, dm.dv), F32), pltpu.VMEM((tt, dm.hk), F32), pltpu.VMEM((tt, dm.gw), F32)]
    return pl.pallas_call(
        kern, grid=(bsz, n_t), in_specs=in_specs, out_specs=out_specs, out_shape=out_shape,
        scratch_shapes=scratch, compiler_params=_cp(("arbitrary", "arbitrary")), name="mixp",
    )(x2d, mod3, *wts)


def _sin_kernel(x_ref, mod_ref, n1_ref, wia_ref, wig_ref, wil_ref, z_ref, *, dm):
    _in_project(dm, x_ref[...], mod_ref[...], n1_ref, (wia_ref, wig_ref, wil_ref), z_ref)


def _sample_in(dm, xs2d, mod_s, n1, win, bs, t_s):
    d = dm.d
    return pl.pallas_call(
        functools.partial(_sin_kernel, dm=dm),
        grid=(t_s,),
        in_specs=[pl.BlockSpec((bs, d), lambda t: (0, t)), _const_spec(mod_s.shape), _const_spec(n1.shape)]
        + [_const_spec(w.shape) for w in win],
        out_specs=pl.BlockSpec((bs, dm.zw), lambda t: (t, 0)),
        out_shape=jax.ShapeDtypeStruct((t_s * bs, dm.zw), F32),
        compiler_params=_cp(("arbitrary",)), name="sample_in",
    )(xs2d, mod_s, n1, *win)


def _sst_kernel(z_ref, pool_ref, gs_ref, wal_ref, bal_ref,
                mixed_ref, o_ref, npool_ref, ngs_ref,
                qt_ref, kt_ref, dt_ref, vt_ref, ot_ref, *, dm, bs, t_s, ch, pos0):
    i = pl.program_id(0)
    n_i = pl.num_programs(0)
    scale = dm.dk ** -0.5
    past = POOL_HIST - 1

    @pl.when(i == 0)
    def _():
        for t in range(t_s):
            r0 = t * bs
            la = _log_decay(z_ref[r0:r0 + bs, dm.al0:dm.al0 + LANE], wal_ref, bal_ref)
            dt_ref[t] = jnp.exp(la).T
            qt_ref[t] = (z_ref[r0:r0 + bs, dm.q0:dm.q0 + dm.hk] * scale).T
            kt_ref[t] = z_ref[r0:r0 + bs, dm.k0:dm.k0 + dm.hk].T
            vt_ref[t] = z_ref[r0:r0 + bs, dm.v0:dm.v0 + dm.gw].T
        ot_ref[...] = jnp.zeros(ot_ref.shape, F32)

        def item(s, c0, c1):
            if s >= 0:
                return z_ref[s * bs:(s + 1) * bs, c0:c1]
            return pool_ref[:, (past + s) * dm.pw + c0:(past + s) * dm.pw + c1]

        gw = dm.pw // len(POOL_WINDOWS)
        for t in range(t_s):
            for g, w in enumerate(POOL_WINDOWS):
                c0, c1 = g * gw, (g + 1) * gw
                cur = item(t, c0, c1)
                acc = cur
                for r in range(1, w):
                    acc = acc + item(t - r, c0, c1)
                cnt = float(min(w, pos0 + t + 1))
                mixed_ref[t * bs:(t + 1) * bs, c0:c1] = acc / cnt - cur
        for r in range(past):
            npool_ref[:, r * dm.pw:(r + 1) * dm.pw] = item(r - past + t_s, 0, dm.pw)

    per_head = dm.dk // ch
    hoff = pl.multiple_of((i // per_head) * dm.dv, dm.dv)
    for p in range(ch):
        hd = i * ch + p
        st = gs_ref[:, p * dm.dv:(p + 1) * dm.dv].T
        for t in range(t_s):
            st = dt_ref[t, pl.ds(hd, 1), :] * st + kt_ref[t, pl.ds(hd, 1), :] * vt_ref[t, pl.ds(hoff, dm.dv), :]
            ot_ref[t, pl.ds(hoff, dm.dv), :] += qt_ref[t, pl.ds(hd, 1), :] * st
        ngs_ref[:, p * dm.dv:(p + 1) * dm.dv] = st.T

    @pl.when(i == n_i - 1)
    def _():
        for t in range(t_s):
            for h in range(dm.h):
                o_ref[t * bs:(t + 1) * bs, h * dm.dv:(h + 1) * dm.dv] = ot_ref[t, h * dm.dv:(h + 1) * dm.dv, :].T


def _sample_state(dm, z_s, pool2d, gs2d, wal, bal, bs, t_s):
    ch = 16
    n_i = dm.hk // ch
    past = POOL_HIST - 1
    kern = functools.partial(_sst_kernel, dm=dm, bs=bs, t_s=t_s, ch=ch, pos0=PAST_LEN)
    full = lambda shape: pl.BlockSpec(shape, lambda i: (0,) * len(shape))
    return pl.pallas_call(
        kern, grid=(n_i,),
        in_specs=[_const_spec(z_s.shape), _const_spec(pool2d.shape),
                  pl.BlockSpec((bs, ch * dm.dv), lambda i: (0, i)),
                  _const_spec(wal.shape), _const_spec(bal.shape)],
        out_specs=[full((t_s * bs, dm.pw)), full((t_s * bs, dm.gw)), full((bs, past * dm.pw)),
                   pl.BlockSpec((bs, ch * dm.dv), lambda i: (0, i))],
        out_shape=[jax.ShapeDtypeStruct((t_s * bs, dm.pw), F32), jax.ShapeDtypeStruct((t_s * bs, dm.gw), F32),
                   jax.ShapeDtypeStruct((bs, past * dm.pw), F32), jax.ShapeDtypeStruct(gs2d.shape, F32)],
        scratch_shapes=[pltpu.VMEM((t_s, dm.hk, bs), F32), pltpu.VMEM((t_s, dm.hk, bs), F32),
                        pltpu.VMEM((t_s, dm.hk, bs), F32), pltpu.VMEM((t_s, dm.gw, bs), F32),
                        pltpu.VMEM((t_s, dm.gw, bs), F32)],
        compiler_params=_cp(("arbitrary",)), name="sample_state",
    )(z_s, pool2d, gs2d, wal, bal)


def _spost_kernel(x_ref, mod_ref, z_ref, mixed_ref, o_ref, pw_ref, ps_ref, gn_ref,
                  wa_ref, wb_ref, wo_ref, n2_ref, wrh_ref, wrl_ref,
                  x1_ref, h2_ref, lg_ref, *, dm):
    gw = dm.pw // len(POOL_WINDOWS)
    ya = _pool_project([mixed_ref[:, g * gw:(g + 1) * gw] for g in range(len(POOL_WINDOWS))], pw_ref, ps_ref)
    x1, h2, logits = _post_mix(
        dm, x_ref[...], mod_ref[...], ya, o_ref[...],
        z_ref[:, dm.go0:dm.go0 + dm.gw], z_ref[:, dm.ga0:dm.ga0 + dm.d], z_ref[:, dm.gb0:dm.gb0 + dm.d],
        gn_ref, wa_ref, wb_ref, wo_ref, n2_ref, wrh_ref, wrl_ref)
    x1_ref[...] = x1
    h2_ref[...] = h2
    lg_ref[...] = logits


def _sample_post(dm, xs2d, mod_s, z_s, mixed_s, o_s, wts, bs, t_s, n_exp):
    d = dm.d
    rows = lambda t: (t, 0)
    in_specs = [pl.BlockSpec((bs, d), lambda t: (0, t)), _const_spec(mod_s.shape),
                pl.BlockSpec((bs, dm.zw), rows), pl.BlockSpec((bs, dm.pw), rows), pl.BlockSpec((bs, dm.gw), rows)]
    in_specs += [_const_spec(w.shape) for w in wts]
    return pl.pallas_call(
        functools.partial(_spost_kernel, dm=dm), grid=(t_s,), in_specs=in_specs,
        out_specs=[pl.BlockSpec((bs, d), rows), pl.BlockSpec((bs, d // 2), rows), pl.BlockSpec((bs, n_exp), rows)],
        out_shape=[jax.ShapeDtypeStruct((t_s * bs, d), F32), jax.ShapeDtypeStruct((t_s * bs, d // 2), I32),
                   jax.ShapeDtypeStruct((t_s * bs, n_exp), F32)],
        compiler_params=_cp(("arbitrary",)), name="sample_post",
    )(xs2d, mod_s, z_s, mixed_s, o_s, *wts)


def _route_kernel(lgp_ref, lgs_ref, bias_ref, idx_ref, w_ref, cnt_ref, *, n_exp, tiles_p):
    i = pl.program_id(0)
    s = jax.nn.sigmoid(jnp.where(i < tiles_p, lgp_ref[...], lgs_ref[...]))
    cur = s + bias_ref[...]
    tr = s.shape[0]
    lane = lax.broadcasted_iota(I32, (1, n_exp), 1).astype(F32)
    slot = lax.broadcasted_iota(I32, (1, LANE), 1)
    idx8 = jnp.zeros((tr, LANE), F32)
    w8 = jnp.zeros((tr, LANE), F32)
    wsum = jnp.zeros((tr, 1), F32)
    picked = jnp.zeros((tr, n_exp), F32)
    for k in range(TOP_K):
        m = jnp.max(cur, axis=-1, keepdims=True)
        am = jnp.min(jnp.where(cur == m, lane, float(n_exp)), axis=-1, keepdims=True)
        oh = lane == am
        sk = jnp.sum(jnp.where(oh, s, 0.0), axis=-1, keepdims=True)
        cur = jnp.where(oh, -jnp.inf, cur)
        idx8 = jnp.where(slot == k, am, idx8)
        w8 = jnp.where(slot == k, sk, w8)
        wsum = wsum + sk
        picked = picked + jnp.where(oh, 1.0, 0.0)
    idx_ref[...] = idx8.astype(I32)
    w_ref[...] = w8 / wsum * ROUTED_SCALE

    @pl.when(i == 0)
    def _():
        cnt_ref[...] = jnp.zeros(cnt_ref.shape, F32)

    cnt_ref[...] += jnp.sum(picked, axis=0, keepdims=True)


def _two_source_specs(block, tiles_p, tiles_s):
    return [pl.BlockSpec(block, lambda i: (jnp.minimum(i, tiles_p - 1), 0)),
            pl.BlockSpec(block, lambda i: (jnp.clip(i - tiles_p, 0, tiles_s - 1), 0))]


def _route(lg_p, lg_s, bias, tr):
    n_exp = lg_p.shape[1]
    tiles_p, tiles_s = lg_p.shape[0] // tr, lg_s.shape[0] // tr
    n = (tiles_p + tiles_s) * tr
    rows = lambda i: (i, 0)
    return pl.pallas_call(
        functools.partial(_route_kernel, n_exp=n_exp, tiles_p=tiles_p), grid=(tiles_p + tiles_s,),
        in_specs=_two_source_specs((tr, n_exp), tiles_p, tiles_s) + [_const_spec((1, n_exp))],
        out_specs=[pl.BlockSpec((tr, LANE), rows), pl.BlockSpec((tr, LANE), rows),
                   pl.BlockSpec((8, n_exp), lambda i: (0, 0))],
        out_shape=[jax.ShapeDtypeStruct((n, LANE), I32), jax.ShapeDtypeStruct((n, LANE), F32),
                   jax.ShapeDtypeStruct((8, n_exp), F32)],
        compiler_params=_cp(("arbitrary",)), name="route",
    )(lg_p, lg_s, bias.reshape(1, n_exp))


def _plan_kernel(idx_ref, cnt_ref, pos_ref, bounds_ref, start_ref, run_ref, *, n_exp):
    i = pl.program_id(0)
    lane = lax.broadcasted_iota(I32, (1, n_exp), 1)

    @pl.when(i == 0)
    def _():
        cnt = cnt_ref[...].astype(I32)
        padded = ((cnt + (EXPERT_BLOCK - 1)) // EXPERT_BLOCK) * EXPERT_BLOCK
        pe = padded
        s = 1
        while s < n_exp:
            pe = pe + jnp.where(lane >= s, pltpu.roll(pe, s, 1), 0)
            s *= 2
        start_ref[...] = (pe - padded).astype(F32)
        run_ref[...] = jnp.zeros(run_ref.shape, F32)
        row = lax.broadcasted_iota(I32, (8, n_exp), 0)
        bounds_ref[...] = jnp.where(row == 0, pe - padded, pe)

    idx = idx_ref[...]
    tr = idx.shape[0]
    onehots = [lane == idx[:, k:k + 1] for k in range(TOP_K)]
    member = onehots[0]
    for k in range(1, TOP_K):
        member = member | onehots[k]
    mb = jnp.where(member, 1.0, 0.0).astype(BF16)
    r_i = lax.broadcasted_iota(I32, (tr, tr), 0)
    c_i = lax.broadcasted_iota(I32, (tr, tr), 1)
    earlier = jnp.where(r_i > c_i, 1.0, 0.0).astype(BF16)
    dest = _dot(earlier, mb) + (start_ref[0:1, :] + run_ref[0:1, :])
    slot = lax.broadcasted_iota(I32, (1, LANE), 1)
    pos8 = jnp.zeros((tr, LANE), F32)
    for k in range(TOP_K):
        pk = jnp.sum(jnp.where(onehots[k], dest, 0.0), axis=-1, keepdims=True)
        pos8 = jnp.where(slot == k, pk, pos8)
    pos_ref[...] = pos8.T[0:TOP_K, :].astype(I32)
    run_ref[...] += jnp.sum(mb.astype(F32), axis=0, keepdims=True)


def _plan(idx8, counts, tr):
    n = idx8.shape[0]
    n_exp = counts.shape[1]
    rows = lambda i: (i, 0)
    return pl.pallas_call(
        functools.partial(_plan_kernel, n_exp=n_exp), grid=(n // tr,),
        in_specs=[pl.BlockSpec((tr, LANE), rows), _const_spec((8, n_exp))],
        out_specs=[pl.BlockSpec((TOP_K, tr), lambda i: (0, i)), pl.BlockSpec((8, n_exp), lambda i: (0, 0))],
        out_shape=[jax.ShapeDtypeStruct((TOP_K, n), I32), jax.ShapeDtypeStruct((8, n_exp), I32)],
        scratch_shapes=[pltpu.VMEM((8, n_exp), F32), pltpu.VMEM((8, n_exp), F32)],
        compiler_params=_cp(("arbitrary",)), name="plan",
    )(idx8, counts)


SC_CORES = 2
SC_SUBCORES = 16
SC_AXES = ("core", "subcore")
SC_WORKERS = SC_CORES * SC_SUBCORES
SC_WINDOW = 128


def _sc_mesh():
    return plsc.VectorSubcoreMesh(core_axis_name=SC_AXES[0], subcore_axis_name=SC_AXES[1],
                                  num_cores=SC_CORES, num_subcores=SC_SUBCORES)


def _sc_worker():
    return lax.axis_index(SC_AXES[1]) * SC_CORES + lax.axis_index(SC_AXES[0])


def _dispatch(h_p, h_s, pos_t, n_rows):
    n_p, c = h_p.shape
    n_s = h_s.shape[0]
    win_p, win_s = n_p // SC_WINDOW, n_s // SC_WINDOW
    assert n_p % (SC_WINDOW * SC_WORKERS) == 0 and n_s % SC_WINDOW == 0 and win_s <= SC_WORKERS

    @functools.partial(
        pl.kernel, out_type=jax.ShapeDtypeStruct((n_rows, c), h_p.dtype), mesh=_sc_mesh(), name="dispatch",
        scratch_types=[pltpu.VMEM((TOP_K, SC_WINDOW), I32), pltpu.VMEM((SC_WINDOW, c), h_p.dtype),
                       pltpu.SemaphoreType.DMA])
    def run(hp_hbm, hs_hbm, pos_hbm, xs_hbm, idx_v, rows_v, sem):
        wid = _sc_worker()

        def scatter_window(h_hbm, row0, tok0):
            pltpu.sync_copy(h_hbm.at[pl.ds(row0, SC_WINDOW)], rows_v)
            pltpu.sync_copy(pos_hbm.at[:, pl.ds(tok0, SC_WINDOW)], idx_v)
            copies = [pltpu.async_copy(rows_v, xs_hbm.at[idx_v.at[k]], sem) for k in range(TOP_K)]
            for cp in copies:
                cp.wait()

        def prompt_window(j, carry):
            row0 = pl.multiple_of((wid * (win_p // SC_WORKERS) + j) * SC_WINDOW, SC_WINDOW)
            scatter_window(hp_hbm, row0, row0)
            return carry
        lax.fori_loop(0, win_p // SC_WORKERS, prompt_window, 0)

        @pl.when(wid < win_s)
        def _():
            row0 = pl.multiple_of(wid * SC_WINDOW, SC_WINDOW)
            scatter_window(hs_hbm, row0, n_p + row0)

    return run(h_p, h_s, pos_t)


def _combine_gather(ys, pos_t):
    k_n = pos_t.shape[0] * pos_t.shape[1]
    c = ys.shape[1]
    win = SC_WINDOW // 2
    per_worker = k_n // (win * SC_WORKERS)
    assert k_n % (win * SC_WORKERS) == 0 and per_worker % 2 == 0
    dma = pltpu.SemaphoreType.DMA

    @functools.partial(
        pl.kernel, out_type=jax.ShapeDtypeStruct((k_n, c), ys.dtype), mesh=_sc_mesh(), name="combine_gather",
        scratch_types=[pltpu.VMEM((win,), I32), pltpu.VMEM((win,), I32),
                       pltpu.VMEM((win, c), ys.dtype), pltpu.VMEM((win, c), ys.dtype), dma, dma, dma, dma])
    def run(ys_hbm, idx_hbm, out_hbm, idx0, idx1, buf0, buf1, gsem0, gsem1, ssem0, ssem1):
        wid = _sc_worker()

        def rows(j):
            return pl.ds(pl.multiple_of((wid * per_worker + j) * win, win), win)

        def gather(j, idx_v, buf, sem):
            pltpu.sync_copy(idx_hbm.at[rows(j)], idx_v)
            pltpu.async_copy(ys_hbm.at[idx_v], buf, sem)

        def gather_wait(idx_v, buf, sem):
            pltpu.make_async_copy(ys_hbm.at[idx_v], buf, sem).wait()

        def store(j, buf, sem):
            pltpu.async_copy(buf, out_hbm.at[rows(j)], sem)

        def store_wait(buf, sem):
            pltpu.make_async_copy(buf, out_hbm.at[rows(0)], sem).wait()

        gather(0, idx0, buf0, gsem0)

        def two_windows(i, carry):
            j = 2 * i

            @pl.when(i > 0)
            def _():
                store_wait(buf1, ssem1)

            gather(j + 1, idx1, buf1, gsem1)
            gather_wait(idx0, buf0, gsem0)
            store(j, buf0, ssem0)

            @pl.when(j + 2 < per_worker)
            def _():
                store_wait(buf0, ssem0)
                gather(j + 2, idx0, buf0, gsem0)

            gather_wait(idx1, buf1, gsem1)
            store(j + 1, buf1, ssem1)
            return carry
        lax.fori_loop(0, per_worker // 2, two_windows, 0)
        store_wait(buf0, ssem0)
        store_wait(buf1, ssem1)

    return run(ys, pos_t.reshape(k_n))


MOE_RING = 6


def _moe_kernel(st_ref, wg_ref, wu_ref, wd_ref, xs_ref, ys_ref, xbuf, ybuf, wgb, wub, wdb, semx, semy):
    e = pl.program_id(0)
    n_e = pl.num_programs(0)
    g0 = st_ref[e] // EXPERT_BLOCK
    nchunks = st_ref[e + 1] // EXPERT_BLOCK - g0
    ntot = st_ref[n_e] // EXPERT_BLOCK

    def rows(g):
        return pl.ds(pl.multiple_of(g * EXPERT_BLOCK, EXPERT_BLOCK), EXPERT_BLOCK)

    def x_copy(g):
        slot = g % MOE_RING
        return pltpu.make_async_copy(xs_ref.at[rows(g)], xbuf.at[slot], semx.at[slot])

    def y_copy(g):
        slot = g % MOE_RING
        return pltpu.make_async_copy(ybuf.at[slot], ys_ref.at[rows(g)], semy.at[slot])

    @pl.when(e == 0)
    def _():
        for g in range(MOE_RING - 1):
            @pl.when(g < ntot)
            def _(g=g):
                x_copy(g).start()

    wgb[...] = wg_ref[...].astype(BF16)
    wub[...] = wu_ref[...].astype(BF16)
    wdb[...] = wd_ref[...].astype(BF16)

    def take(g):
        @pl.when(g + (MOE_RING - 1) < ntot)
        def _():
            x_copy(g + (MOE_RING - 1)).start()

        x_copy(g).wait()

        @pl.when(g >= MOE_RING)
        def _():
            y_copy(g - MOE_RING).wait()

        return xbuf[g % MOE_RING]

    def expert_mlp(x_words):
        x = _unpack_rows(x_words).astype(BF16)
        mid = _silu(_dot(x, wgb[...])) * _dot(x, wub[...])
        return _pack_rows(_dot(mid.astype(BF16), wdb[...]))

    def put(g, y_words):
        ybuf[g % MOE_RING] = y_words
        y_copy(g).start()

    def pair(p, carry):
        ga = g0 + 2 * p
        xa = take(ga)
        xb = take(ga + 1)
        ya, yb = expert_mlp(xa), expert_mlp(xb)
        put(ga, ya)
        put(ga + 1, yb)
        return carry

    lax.fori_loop(0, nchunks // 2, pair, 0)

    @pl.when(nchunks % 2 == 1)
    def _():
        g = g0 + nchunks - 1
        put(g, expert_mlp(take(g)))

    @pl.when(e == n_e - 1)
    def _():
        for j in range(MOE_RING):
            g = ntot - MOE_RING + j

            @pl.when(g >= 0)
            def _(g=g):
                y_copy(g).wait()


def _moe(starts, xs, wg, wu, wd):
    n_rows, c = xs.shape
    n_exp, d, f = wg.shape
    grid_spec = pltpu.PrefetchScalarGridSpec(
        num_scalar_prefetch=1, grid=(n_exp,),
        in_specs=[pl.BlockSpec((None, d, f), lambda e, st: (e, 0, 0)),
                  pl.BlockSpec((None, d, f), lambda e, st: (e, 0, 0)),
                  pl.BlockSpec((None, f, d), lambda e, st: (e, 0, 0)),
                  pl.BlockSpec(memory_space=pl.ANY)],
        out_specs=pl.BlockSpec(memory_space=pl.ANY),
        scratch_shapes=[pltpu.VMEM((MOE_RING, EXPERT_BLOCK, c), I32), pltpu.VMEM((MOE_RING, EXPERT_BLOCK, c), I32),
                        pltpu.VMEM((d, f), BF16), pltpu.VMEM((d, f), BF16), pltpu.VMEM((f, d), BF16),
                        pltpu.SemaphoreType.DMA((MOE_RING,)), pltpu.SemaphoreType.DMA((MOE_RING,))])
    return pl.pallas_call(
        _moe_kernel, grid_spec=grid_spec, out_shape=jax.ShapeDtypeStruct((n_rows, c), I32),
        compiler_params=_cp(("arbitrary",)), name="moe",
    )(starts, wg, wu, wd, xs)


def _final_kernel(yg_ref, w8_ref, x1_ref, h2_ref, mod_ref, wsg_ref, wsu_ref, wsd_ref, nf_ref, y_ref, *, d):
    hb = _unpack_rows(h2_ref[...]).astype(BF16)
    shared = _dot((_silu(_dot(hb, wsg_ref[...])) * _dot(hb, wsu_ref[...])).astype(BF16), wsd_ref[...])
    w8 = w8_ref[...]
    routed = _unpack_rows(yg_ref[0]) * w8[:, 0:1]
    for k in range(1, TOP_K):
        routed = routed + _unpack_rows(yg_ref[k]) * w8[:, k:k + 1]
    g2 = mod_ref[...][:, (N_MOD - 1) * d:N_MOD * d]
    x2 = x1_ref[...] + g2 * (routed + shared)
    y_ref[...] = _rms(x2, nf_ref[...])


def _final(yg3, w8, x1, h2, mod, mod_spec, wts, nf, tf, n_tiles, tile0, out_shape, out_spec):
    d = x1.shape[1]
    rows = lambda i: (i, 0)
    in_specs = [pl.BlockSpec((TOP_K, tf, d // 2), lambda i: (0, tile0 + i, 0)),
                pl.BlockSpec((tf, LANE), lambda i: (tile0 + i, 0)),
                pl.BlockSpec((tf, d), rows), pl.BlockSpec((tf, d // 2), rows), mod_spec]
    in_specs += [_const_spec(w.shape) for w in wts]
    in_specs += [_const_spec(nf.shape)]
    return pl.pallas_call(
        functools.partial(_final_kernel, d=d), grid=(n_tiles,), in_specs=in_specs,
        out_specs=out_spec, out_shape=out_shape,
        compiler_params=_cp(("arbitrary",)), name="final",
    )(yg3, w8, x1, h2, mod, *wts, nf)


def kernel(x_prompt, x_sample, c_prompt, c_sample, state_pool, state_gla, w_ada, b_ada, norm1, w_in,
           pool_w, pool_scale, gla_w_alpha, gla_b_alpha, gla_norm, w_branch_a, w_branch_b, w_out, norm2,
           w_router, router_bias, w_exp_gate, w_exp_up, w_exp_down, w_sh_gate, w_sh_up, w_sh_down, norm_f):
    depth = w_in.shape[0]
    assert depth == 1, "single-layer trunk"
    bp, seq, d = x_prompt.shape
    bs, t_s, _ = x_sample.shape
    _, _, past, pw = state_pool.shape
    _, _, heads, dk, dv = state_gla.shape
    rank = gla_w_alpha.shape[1]
    n_exp = w_router.shape[2]
    assert past == POOL_HIST - 1 and pw == len(POOL_WINDOWS) * LANE and dv == LANE and rank <= LANE
    dm = _Dims(d, pw, heads, dk, dv)
    n_p, n_s = bp * seq, bs * t_s
    n_all = n_p + n_s

    off = [0]
    for sz in (pw, dm.hk, dm.hk, dm.gw, dm.gw, rank, d, d):
        off.append(off[-1] + sz)
    wi = w_in[0]
    win = (wi[:, :off[5]].astype(BF16), wi[:, off[6]:].astype(BF16),
           jnp.concatenate([wi[:, off[5]:off[6]], jnp.zeros((d, LANE - rank), F32)], axis=1).astype(BF16))
    assert off[5] + 2 * d + LANE == dm.zw
    wal = jnp.concatenate([gla_w_alpha[0], jnp.zeros((LANE - rank, dm.hk), F32)], axis=0).astype(BF16)
    bal = gla_b_alpha[0].reshape(1, dm.hk)
    wr = w_router[0]
    wrh = wr.astype(BF16)
    wrl = (wr - wrh.astype(F32)).astype(BF16)
    post_w = (pool_w[0].astype(BF16), pool_scale[0].reshape(1, pw), gla_norm[0].reshape(1, dm.gw),
              w_branch_a[0].astype(BF16), w_branch_b[0].astype(BF16), w_out[0].astype(BF16),
              norm2[0].reshape(1, d), wrh, wrl)
    n1 = norm1[0].reshape(1, d)

    mod = _ada(jnp.concatenate([c_prompt, c_sample], axis=0), w_ada[0], b_ada[0])
    mod_p = mod[:bp].reshape(bp, 1, N_MOD * d)
    mod_s = mod[bp:]

    x1_p, h2_p, lg_p, npool_p, ngla_p = _mix_prompt(
        dm, x_prompt.reshape(n_p, d), mod_p, (n1,) + win + (wal, bal) + post_w, bp, seq, n_exp)

    xs2d = x_sample.reshape(bs, t_s * d)
    z_s = _sample_in(dm, xs2d, mod_s, n1, win, bs, t_s)
    mixed_s, o_s, npool_s, ngla_s = _sample_state(
        dm, z_s, state_pool[0].reshape(bs, past * pw), state_gla[0].reshape(bs, dm.hk * dv), wal, bal, bs, t_s)
    x1_s, h2_s, lg_s = _sample_post(dm, xs2d, mod_s, z_s, mixed_s, o_s, post_w, bs, t_s, n_exp)

    tr = 512 if (n_p % 512 == 0 and n_s % 512 == 0) else LANE
    assert n_p % tr == 0 and n_s % tr == 0 and n_p % bs == 0
    idx8, w8, counts = _route(lg_p, lg_s, router_bias[0], tr)
    nblk = (n_all * TOP_K) // EXPERT_BLOCK + n_exp
    pos_t, bounds = _plan(idx8, counts, tr)
    starts = jnp.concatenate([bounds[0], bounds[1, n_exp - 1:]])

    xs = _dispatch(h2_p, h2_s, pos_t, nblk * EXPERT_BLOCK)
    ys = _moe(starts, xs, w_exp_gate[0], w_exp_up[0], w_exp_down[0])
    yg3 = _combine_gather(ys, pos_t).reshape(TOP_K, n_all, d // 2)

    sh_w = (w_sh_gate[0].astype(BF16), w_sh_up[0].astype(BF16), w_sh_down[0].astype(BF16))
    nf = norm_f.reshape(1, d)
    tf = 256 if seq % 256 == 0 else seq
    n_tp = seq // tf
    y_p = _final(
        yg3, w8, x1_p, h2_p, mod_p,
        pl.BlockSpec((None, 1, N_MOD * d), lambda i: (i // n_tp, 0, 0)), sh_w, nf, tf, n_p // tf, 0,
        jax.ShapeDtypeStruct((n_p, d), F32), pl.BlockSpec((tf, d), lambda i: (i, 0)))
    y_s = _final(
        yg3, w8, x1_s, h2_s, mod_s,
        _const_spec(mod_s.shape), sh_w, nf, bs, t_s, n_p // bs,
        jax.ShapeDtypeStruct((bs, t_s * d), F32), pl.BlockSpec((bs, d), lambda i: (0, i)))

    return (y_p.reshape(bp, seq, d), y_s.reshape(bs, t_s, d),
            npool_p.reshape(depth, bp, past, pw), ngla_p.reshape(depth, bp, heads, dk, dv),
            npool_s.reshape(depth, bs, past, pw), ngla_s.reshape(depth, bs, heads, dk, dv))
```

```python
import functools

import jax
import jax.numpy as jnp
from jax import lax
from jax.experimental import pallas as pl
from jax.experimental.pallas import tpu as pltpu
from jax.experimental.pallas import tpu_sc as plsc

F32 = jnp.float32
BF16 = jnp.bfloat16
I32 = jnp.int32

EPS = 1e-6
N_MOD = 6
POOL_WINDOWS = (2, 4, 8, 16)
POOL_HIST = 16
GLA_GATE_TEMP = 16.0
TOP_K = 8
ROUTED_SCALE = 2.5
PAST_LEN = 16384
EXPERT_BLOCK = 128
LANE = 128
GLA_FAST_MAX_DECAY = 40.0
VMEM_LIMIT = 56 * 1024 * 1024


def _cp(sem, vmem=VMEM_LIMIT):
    return pltpu.CompilerParams(dimension_semantics=sem, vmem_limit_bytes=vmem)


def _rms(x, g):
    return x * lax.rsqrt(jnp.mean(x * x, axis=-1, keepdims=True) + EPS) * g


def _silu(x):
    return x * jax.nn.sigmoid(x)


def _dot(a, b):
    return jnp.dot(a, b, preferred_element_type=F32)


def _split3(a):
    a0 = a.astype(BF16)
    r1 = a - a0.astype(F32)
    a1 = r1.astype(BF16)
    a2 = (r1 - a1.astype(F32)).astype(BF16)
    return a0, a1, a2


def _mod_parts(mod, d):
    return [mod[:, i * d:(i + 1) * d] for i in range(N_MOD)]


_HI16 = -65536


def _pack_rows(v):
    c = v.shape[1] // 2
    lo = lax.bitcast_convert_type(v[:, :c].astype(BF16).astype(F32), I32)
    hi = lax.bitcast_convert_type(v[:, c:].astype(BF16).astype(F32), I32)
    return (hi & _HI16) | lax.shift_right_logical(lo, 16)


def _unpack_rows(w):
    lo = lax.bitcast_convert_type(lax.shift_left(w, 16), F32)
    hi = lax.bitcast_convert_type(w & _HI16, F32)
    return jnp.concatenate([lo, hi], axis=1)


def _ada_kernel(c_ref, w_ref, b_ref, o_ref):
    a = _silu(c_ref[...]).astype(BF16)
    o_ref[...] = _dot(a, w_ref[...].astype(BF16)) + b_ref[...]


def _ada(c_all, w_ada, b_ada):
    n, d = c_all.shape
    cols = w_ada.shape[1]
    bc = d
    return pl.pallas_call(
        _ada_kernel,
        grid=(cols // bc,),
        in_specs=[pl.BlockSpec((n, d), lambda i: (0, 0)),
                  pl.BlockSpec((d, bc), lambda i: (0, i)),
                  pl.BlockSpec((1, bc), lambda i: (0, i))],
        out_specs=pl.BlockSpec((n, bc), lambda i: (0, i)),
        out_shape=jax.ShapeDtypeStruct((n, cols), F32),
        compiler_params=_cp(("arbitrary",)),
        name="ada",
    )(c_all, w_ada, b_ada.reshape(1, cols))


class _Dims:
    def __init__(self, d, pw, heads, dk, dv):
        self.d, self.pw, self.h, self.dk, self.dv = d, pw, heads, dk, dv
        self.hk, self.gw = heads * dk, heads * dv
        self.q0 = pw
        self.k0 = self.q0 + self.hk
        self.v0 = self.k0 + self.hk
        self.go0 = self.v0 + self.gw
        self.ga0 = self.go0 + self.gw
        self.gb0 = self.ga0 + d
        self.al0 = self.gb0 + d
        self.zw = self.al0 + LANE


def _log_decay(z_al, wal_ref, bal_ref):
    xal = _dot(z_al.astype(BF16), wal_ref[...]) + bal_ref[...]
    return jax.nn.log_sigmoid(xal) / GLA_GATE_TEMP


def _pool_project(mixed_groups, pw_ref, ps_ref):
    ys = [_dot(m.astype(BF16), pw_ref[g]) for g, m in enumerate(mixed_groups)]
    return jnp.concatenate(ys, axis=1) * ps_ref[...]


def _post_mix(dm, x, mod, ya, o, go, ga, gb, gn_ref, wa_ref, wb_ref, wo_ref, n2_ref, wrh_ref, wrl_ref):
    _, _, g1, sh2, sc2, _ = _mod_parts(mod, dm.d)
    parts = []
    for h in range(dm.h):
        oh = o[:, h * dm.dv:(h + 1) * dm.dv]
        parts.append(oh * lax.rsqrt(jnp.mean(oh * oh, axis=-1, keepdims=True) + EPS))
    yb = jnp.concatenate(parts, axis=1) * gn_ref[...] * _silu(go)
    m = (jax.nn.sigmoid(ga) * _dot(ya.astype(BF16), wa_ref[...])
         + jax.nn.sigmoid(gb) * _dot(yb.astype(BF16), wb_ref[...]))
    x1 = x + g1 * _dot(m.astype(BF16), wo_ref[...])
    h2 = _rms(x1, n2_ref[...]) * (1 + sc2) + sh2
    hi = h2.astype(BF16)
    lo = (h2 - hi.astype(F32)).astype(BF16)
    logits = _dot(hi, wrh_ref[...]) + (_dot(hi, wrl_ref[...]) + _dot(lo, wrh_ref[...]))
    return x1, _pack_rows(h2), logits


def _in_project(dm, x, mod, n1_ref, win_refs, z_ref):
    h = (_rms(x, n1_ref[...]) * (1 + mod[:, dm.d:2 * dm.d]) + mod[:, 0:dm.d]).astype(BF16)
    c0 = 0
    for w_ref in win_refs:
        z_ref[:, c0:c0 + w_ref.shape[1]] = _dot(h, w_ref[...])
        c0 += w_ref.shape[1]


def _mixp_kernel(x_ref, mod_ref, n1_ref, wia_ref, wig_ref, wil_ref, wal_ref, bal_ref, pw_ref, ps_ref, gn_ref,
                 wa_ref, wb_ref, wo_ref, n2_ref, wrh_ref, wrl_ref,
                 x1_ref, h2_ref, lg_ref, np_ref, ng_ref,
                 z_ref, uext_ref, s_ref, b_ref, oi_ref, *, dm, tt, n_t):
    j = pl.program_id(1)
    x = x_ref[...]
    mod = mod_ref[...]
    _in_project(dm, x, mod, n1_ref, (wia_ref, wig_ref, wil_ref), z_ref)

    @pl.when(j == 0)
    def _():
        uext_ref[0:POOL_HIST, :] = jnp.zeros((POOL_HIST, dm.pw), F32)
        s_ref[...] = jnp.zeros(s_ref.shape, F32)

    uext_ref[POOL_HIST:POOL_HIST + tt, :] = z_ref[:, 0:dm.pw]
    pos = j * tt + lax.broadcasted_iota(I32, (tt, 1), 0)
    gw = dm.pw // len(POOL_WINDOWS)
    mixed = []
    for g, w in enumerate(POOL_WINDOWS):
        c0 = g * gw
        cur = uext_ref[POOL_HIST:POOL_HIST + tt, c0:c0 + gw]
        acc = cur
        for i in range(1, w):
            acc = acc + uext_ref[POOL_HIST - i:POOL_HIST - i + tt, c0:c0 + gw]
        cnt = jnp.minimum(w, pos + 1).astype(F32)
        mixed.append(acc / cnt - cur)
    ya = _pool_project(mixed, pw_ref, ps_ref)

    @pl.when(j == n_t - 1)
    def _():
        np_ref[...] = uext_ref[tt + 1:tt + POOL_HIST, :]

    uext_ref[0:POOL_HIST, :] = uext_ref[tt:tt + POOL_HIST, :]

    la = _log_decay(z_ref[:, dm.al0:dm.al0 + LANE], wal_ref, bal_ref)
    r_i = lax.broadcasted_iota(I32, (tt, tt), 0)
    c_i = lax.broadcasted_iota(I32, (tt, tt), 1)
    causal = r_i >= c_i
    tri = jnp.where(causal, 1.0, 0.0).astype(BF16)
    a0, a1, a2 = _split3(la)
    bc = _dot(tri, a0) + (_dot(tri, a1) + _dot(tri, a2))
    b_ref[...] = bc
    q = z_ref[:, dm.q0:dm.q0 + dm.hk]
    k = z_ref[:, dm.k0:dm.k0 + dm.hk]
    v = z_ref[:, dm.v0:dm.v0 + dm.gw]
    scale = dm.dk ** -0.5
    qe = (q * scale) * jnp.exp(bc)
    blast = bc[tt - 1:tt, :]
    kd = k * jnp.exp(blast - bc)
    s_old = s_ref[...]
    s_bf = s_old.astype(BF16)
    vb = v.astype(BF16)
    lane = lax.broadcasted_iota(I32, (1, dm.hk), 1)
    head_masks = [(lane >= h * dm.dk) & (lane < (h + 1) * dm.dk) for h in range(dm.h)]
    o_inter = [_dot(jnp.where(head_masks[h], qe, 0.0).astype(BF16), s_bf) for h in range(dm.h)]

    fast = jnp.max(-blast) <= GLA_FAST_MAX_DECAY

    @pl.when(fast)
    def _():
        ke = (k * jnp.exp(-bc)).astype(BF16)
        for h in range(dm.h):
            qh = jnp.where(head_masks[h], qe, 0.0).astype(BF16)
            sc = lax.dot_general(qh, ke, (((1,), (1,)), ((), ())), preferred_element_type=F32)
            sc = jnp.where(causal, sc, 0.0).astype(BF16)
            oi_ref[:, h * dm.dv:(h + 1) * dm.dv] = _dot(sc, vb[:, h * dm.dv:(h + 1) * dm.dv])

    @pl.when(jnp.logical_not(fast))
    def _():
        rows = lax.broadcasted_iota(I32, (tt, 1), 0)
        for h in range(dm.h):
            def body(tb, carry, h=h):
                base = pl.multiple_of(tb * 8, 8)
                q8 = z_ref[pl.ds(base, 8), dm.q0:dm.q0 + dm.hk] * scale
                b8 = b_ref[pl.ds(base, 8), :]
                vh = z_ref[:, dm.v0 + h * dm.dv:dm.v0 + (h + 1) * dm.dv]
                out_rows = []
                for r in range(8):
                    keep = (rows <= base + r) & head_masks[h]
                    dec = jnp.exp(jnp.where(keep, b8[r:r + 1, :] - b_ref[...], -jnp.inf))
                    wgt = (q8[r:r + 1, :] * dec) * z_ref[:, dm.k0:dm.k0 + dm.hk]
                    s_col = jnp.sum(wgt, axis=-1, keepdims=True)
                    out_rows.append(jnp.sum(s_col * vh, axis=0, keepdims=True))
                oi_ref[pl.ds(base, 8), h * dm.dv:(h + 1) * dm.dv] = jnp.concatenate(out_rows, axis=0)
                return carry
            lax.fori_loop(0, tt // 8, body, 0)

    o = jnp.concatenate(o_inter, axis=1) + oi_ref[...]

    u_all = _dot(kd.T.astype(BF16), vb)
    dec_col = jnp.broadcast_to(jnp.exp(blast), (dm.hk, dm.hk)).T[:, 0:dm.dv]
    upd = jnp.concatenate(
        [u_all[h * dm.dk:(h + 1) * dm.dk, h * dm.dv:(h + 1) * dm.dv] for h in range(dm.h)], axis=0)
    s_new = dec_col * s_old + upd
    s_ref[...] = s_new

    @pl.when(j == n_t - 1)
    def _():
        ng_ref[...] = s_new

    go = z_ref[:, dm.go0:dm.go0 + dm.gw]
    ga = z_ref[:, dm.ga0:dm.ga0 + dm.d]
    gb = z_ref[:, dm.gb0:dm.gb0 + dm.d]
    x1, h2, logits = _post_mix(dm, x, mod, ya, o, go, ga, gb, gn_ref, wa_ref, wb_ref, wo_ref,
                               n2_ref, wrh_ref, wrl_ref)
    x1_ref[...] = x1
    h2_ref[...] = h2
    lg_ref[...] = logits


def _const_spec(shape):
    nd = len(shape)
    return pl.BlockSpec(shape, lambda *_: (0,) * nd, pipeline_mode=pl.Buffered(1))


def _mix_prompt(dm, x2d, mod3, wts, bsz, seq, n_exp):
    tt = 256 if seq % 256 == 0 else seq
    n_t = seq // tt
    n = bsz * seq
    d = dm.d
    kern = functools.partial(_mixp_kernel, dm=dm, tt=tt, n_t=n_t)
    row = lambda b, j: (b * n_t + j, 0)
    in_specs = [pl.BlockSpec((tt, d), row),
                pl.BlockSpec((None, 1, N_MOD * d), lambda b, j: (b, 0, 0))]
    in_specs += [_const_spec(w.shape) for w in wts]
    out_specs = [pl.BlockSpec((tt, d), row), pl.BlockSpec((tt, d // 2), row), pl.BlockSpec((tt, n_exp), row),
                 pl.BlockSpec((None, POOL_HIST - 1, dm.pw), lambda b, j: (b, 0, 0)),
                 pl.BlockSpec((None, dm.hk, dm.dv), lambda b, j: (b, 0, 0))]
    out_shape = [jax.ShapeDtypeStruct((n, d), F32), jax.ShapeDtypeStruct((n, d // 2), I32),
                 jax.ShapeDtypeStruct((n, n_exp), F32),
                 jax.ShapeDtypeStruct((bsz, POOL_HIST - 1, dm.pw), F32),
                 jax.ShapeDtypeStruct((bsz, dm.hk, dm.dv), F32)]
    scratch = [pltpu.VMEM((tt, dm.zw), F32), pltpu.VMEM((tt + POOL_HIST, dm.pw), F32),
               pltpu.VMEM((dm.hk, dm.dv), F32), pltpu.VMEM((tt, dm.hk), F32), pltpu.VMEM((tt, dm.gw), F32)]
    return pl.pallas_call(
        kern, grid=(bsz, n_t), in_specs=in_specs, out_specs=out_specs, out_shape=out_shape,
        scratch_shapes=scratch, compiler_params=_cp(("arbitrary", "arbitrary")), name="mixp",
    )(x2d, mod3, *wts)


def _sin_kernel(x_ref, mod_ref, n1_ref, wia_ref, wig_ref, wil_ref, z_ref, *, dm):
    _in_project(dm, x_ref[...], mod_ref[...], n1_ref, (wia_ref, wig_ref, wil_ref), z_ref)


def _sample_in(dm, xs2d, mod_s, n1, win, bs, t_s):
    d = dm.d
    return pl.pallas_call(
        functools.partial(_sin_kernel, dm=dm),
        grid=(t_s,),
        in_specs=[pl.BlockSpec((bs, d), lambda t: (0, t)), _const_spec(mod_s.shape), _const_spec(n1.shape)]
        + [_const_spec(w.shape) for w in win],
        out_specs=pl.BlockSpec((bs, dm.zw), lambda t: (t, 0)),
        out_shape=jax.ShapeDtypeStruct((t_s * bs, dm.zw), F32),
        compiler_params=_cp(("arbitrary",)), name="sample_in",
    )(xs2d, mod_s, n1, *win)


def _sst_kernel(z_ref, pool_ref, gs_ref, wal_ref, bal_ref,
                mixed_ref, o_ref, npool_ref, ngs_ref,
                qt_ref, kt_ref, dt_ref, vt_ref, ot_ref, *, dm, bs, t_s, ch, pos0):
    i = pl.program_id(0)
    n_i = pl.num_programs(0)
    scale = dm.dk ** -0.5
    past = POOL_HIST - 1

    @pl.when(i == 0)
    def _():
        for t in range(t_s):
            r0 = t * bs
            la = _log_decay(z_ref[r0:r0 + bs, dm.al0:dm.al0 + LANE], wal_ref, bal_ref)
            dt_ref[t] = jnp.exp(la).T
            qt_ref[t] = (z_ref[r0:r0 + bs, dm.q0:dm.q0 + dm.hk] * scale).T
            kt_ref[t] = z_ref[r0:r0 + bs, dm.k0:dm.k0 + dm.hk].T
            vt_ref[t] = z_ref[r0:r0 + bs, dm.v0:dm.v0 + dm.gw].T
        ot_ref[...] = jnp.zeros(ot_ref.shape, F32)

        def item(s, c0, c1):
            if s >= 0:
                return z_ref[s * bs:(s + 1) * bs, c0:c1]
            return pool_ref[:, (past + s) * dm.pw + c0:(past + s) * dm.pw + c1]

        gw = dm.pw // len(POOL_WINDOWS)
        for t in range(t_s):
            for g, w in enumerate(POOL_WINDOWS):
                c0, c1 = g * gw, (g + 1) * gw
                cur = item(t, c0, c1)
                acc = cur
                for r in range(1, w):
                    acc = acc + item(t - r, c0, c1)
                cnt = float(min(w, pos0 + t + 1))
                mixed_ref[t * bs:(t + 1) * bs, c0:c1] = acc / cnt - cur
        for r in range(past):
            npool_ref[:, r * dm.pw:(r + 1) * dm.pw] = item(r - past + t_s, 0, dm.pw)

    per_head = dm.dk // ch
    hoff = pl.multiple_of((i // per_head) * dm.dv, dm.dv)
    for p in range(ch):
        hd = i * ch + p
        st = gs_ref[:, p * dm.dv:(p + 1) * dm.dv].T
        for t in range(t_s):
            st = dt_ref[t, pl.ds(hd, 1), :] * st + kt_ref[t, pl.ds(hd, 1), :] * vt_ref[t, pl.ds(hoff, dm.dv), :]
            ot_ref[t, pl.ds(hoff, dm.dv), :] += qt_ref[t, pl.ds(hd, 1), :] * st
        ngs_ref[:, p * dm.dv:(p + 1) * dm.dv] = st.T

    @pl.when(i == n_i - 1)
    def _():
        for t in range(t_s):
            for h in range(dm.h):
                o_ref[t * bs:(t + 1) * bs, h * dm.dv:(h + 1) * dm.dv] = ot_ref[t, h * dm.dv:(h + 1) * dm.dv, :].T


def _sample_state(dm, z_s, pool2d, gs2d, wal, bal, bs, t_s):
    ch = 16
    n_i = dm.hk // ch
    past = POOL_HIST - 1
    kern = functools.partial(_sst_kernel, dm=dm, bs=bs, t_s=t_s, ch=ch, pos0=PAST_LEN)
    full = lambda shape: pl.BlockSpec(shape, lambda i: (0,) * len(shape))
    return pl.pallas_call(
        kern, grid=(n_i,),
        in_specs=[_const_spec(z_s.shape), _const_spec(pool2d.shape),
                  pl.BlockSpec((bs, ch * dm.dv), lambda i: (0, i)),
                  _const_spec(wal.shape), _const_spec(bal.shape)],
        out_specs=[full((t_s * bs, dm.pw)), full((t_s * bs, dm.gw)), full((bs, past * dm.pw)),
                   pl.BlockSpec((bs, ch * dm.dv), lambda i: (0, i))],
        out_shape=[jax.ShapeDtypeStruct((t_s * bs, dm.pw), F32), jax.ShapeDtypeStruct((t_s * bs, dm.gw), F32),
                   jax.ShapeDtypeStruct((bs, past * dm.pw), F32), jax.ShapeDtypeStruct(gs2d.shape, F32)],
        scratch_shapes=[pltpu.VMEM((t_s, dm.hk, bs), F32), pltpu.VMEM((t_s, dm.hk, bs), F32),
                        pltpu.VMEM((t_s, dm.hk, bs), F32), pltpu.VMEM((t_s, dm.gw, bs), F32),
                        pltpu.VMEM((t_s, dm.gw, bs), F32)],
        compiler_params=_cp(("arbitrary",)), name="sample_state",
    )(z_s, pool2d, gs2d, wal, bal)


def _spost_kernel(x_ref, mod_ref, z_ref, mixed_ref, o_ref, pw_ref, ps_ref, gn_ref,
                  wa_ref, wb_ref, wo_ref, n2_ref, wrh_ref, wrl_ref,
                  x1_ref, h2_ref, lg_ref, *, dm):
    gw = dm.pw // len(POOL_WINDOWS)
    ya = _pool_project([mixed_ref[:, g * gw:(g + 1) * gw] for g in range(len(POOL_WINDOWS))], pw_ref, ps_ref)
    x1, h2, logits = _post_mix(
        dm, x_ref[...], mod_ref[...], ya, o_ref[...],
        z_ref[:, dm.go0:dm.go0 + dm.gw], z_ref[:, dm.ga0:dm.ga0 + dm.d], z_ref[:, dm.gb0:dm.gb0 + dm.d],
        gn_ref, wa_ref, wb_ref, wo_ref, n2_ref, wrh_ref, wrl_ref)
    x1_ref[...] = x1
    h2_ref[...] = h2
    lg_ref[...] = logits


def _sample_post(dm, xs2d, mod_s, z_s, mixed_s, o_s, wts, bs, t_s, n_exp):
    d = dm.d
    rows = lambda t: (t, 0)
    in_specs = [pl.BlockSpec((bs, d), lambda t: (0, t)), _const_spec(mod_s.shape),
                pl.BlockSpec((bs, dm.zw), rows), pl.BlockSpec((bs, dm.pw), rows), pl.BlockSpec((bs, dm.gw), rows)]
    in_specs += [_const_spec(w.shape) for w in wts]
    return pl.pallas_call(
        functools.partial(_spost_kernel, dm=dm), grid=(t_s,), in_specs=in_specs,
        out_specs=[pl.BlockSpec((bs, d), rows), pl.BlockSpec((bs, d // 2), rows), pl.BlockSpec((bs, n_exp), rows)],
        out_shape=[jax.ShapeDtypeStruct((t_s * bs, d), F32), jax.ShapeDtypeStruct((t_s * bs, d // 2), I32),
                   jax.ShapeDtypeStruct((t_s * bs, n_exp), F32)],
        compiler_params=_cp(("arbitrary",)), name="sample_post",
    )(xs2d, mod_s, z_s, mixed_s, o_s, *wts)


def _route_kernel(lgp_ref, lgs_ref, bias_ref, idx_ref, w_ref, cnt_ref, *, n_exp, tiles_p):
    i = pl.program_id(0)
    s = jax.nn.sigmoid(jnp.where(i < tiles_p, lgp_ref[...], lgs_ref[...]))
    cur = s + bias_ref[...]
    tr = s.shape[0]
    lane = lax.broadcasted_iota(I32, (1, n_exp), 1).astype(F32)
    slot = lax.broadcasted_iota(I32, (1, LANE), 1)
    idx8 = jnp.zeros((tr, LANE), F32)
    w8 = jnp.zeros((tr, LANE), F32)
    wsum = jnp.zeros((tr, 1), F32)
    picked = jnp.zeros((tr, n_exp), F32)
    for k in range(TOP_K):
        m = jnp.max(cur, axis=-1, keepdims=True)
        am = jnp.min(jnp.where(cur == m, lane, float(n_exp)), axis=-1, keepdims=True)
        oh = lane == am
        sk = jnp.sum(jnp.where(oh, s, 0.0), axis=-1, keepdims=True)
        cur = jnp.where(oh, -jnp.inf, cur)
        idx8 = jnp.where(slot == k, am, idx8)
        w8 = jnp.where(slot == k, sk, w8)
        wsum = wsum + sk
        picked = picked + jnp.where(oh, 1.0, 0.0)
    idx_ref[...] = idx8.astype(I32)
    w_ref[...] = w8 / wsum * ROUTED_SCALE

    @pl.when(i == 0)
    def _():
        cnt_ref[...] = jnp.zeros(cnt_ref.shape, F32)

    cnt_ref[...] += jnp.sum(picked, axis=0, keepdims=True)


def _two_source_specs(block, tiles_p, tiles_s):
    return [pl.BlockSpec(block, lambda i: (jnp.minimum(i, tiles_p - 1), 0)),
            pl.BlockSpec(block, lambda i: (jnp.clip(i - tiles_p, 0, tiles_s - 1), 0))]


def _route(lg_p, lg_s, bias, tr):
    n_exp = lg_p.shape[1]
    tiles_p, tiles_s = lg_p.shape[0] // tr, lg_s.shape[0] // tr
    n = (tiles_p + tiles_s) * tr
    rows = lambda i: (i, 0)
    return pl.pallas_call(
        functools.partial(_route_kernel, n_exp=n_exp, tiles_p=tiles_p), grid=(tiles_p + tiles_s,),
        in_specs=_two_source_specs((tr, n_exp), tiles_p, tiles_s) + [_const_spec((1, n_exp))],
        out_specs=[pl.BlockSpec((tr, LANE), rows), pl.BlockSpec((tr, LANE), rows),
                   pl.BlockSpec((8, n_exp), lambda i: (0, 0))],
        out_shape=[jax.ShapeDtypeStruct((n, LANE), I32), jax.ShapeDtypeStruct((n, LANE), F32),
                   jax.ShapeDtypeStruct((8, n_exp), F32)],
        compiler_params=_cp(("arbitrary",)), name="route",
    )(lg_p, lg_s, bias.reshape(1, n_exp))


def _plan_kernel(idx_ref, cnt_ref, pos_ref, bounds_ref, start_ref, run_ref, *, n_exp):
    i = pl.program_id(0)
    lane = lax.broadcasted_iota(I32, (1, n_exp), 1)

    @pl.when(i == 0)
    def _():
        cnt = cnt_ref[...].astype(I32)
        padded = ((cnt + (EXPERT_BLOCK - 1)) // EXPERT_BLOCK) * EXPERT_BLOCK
        pe = padded
        s = 1
        while s < n_exp:
            pe = pe + jnp.where(lane >= s, pltpu.roll(pe, s, 1), 0)
            s *= 2
        start_ref[...] = (pe - padded).astype(F32)
        run_ref[...] = jnp.zeros(run_ref.shape, F32)
        row = lax.broadcasted_iota(I32, (8, n_exp), 0)
        bounds_ref[...] = jnp.where(row == 0, pe - padded, pe)

    idx = idx_ref[...]
    tr = idx.shape[0]
    onehots = [lane == idx[:, k:k + 1] for k in range(TOP_K)]
    member = onehots[0]
    for k in range(1, TOP_K):
        member = member | onehots[k]
    mb = jnp.where(member, 1.0, 0.0).astype(BF16)
    r_i = lax.broadcasted_iota(I32, (tr, tr), 0)
    c_i = lax.broadcasted_iota(I32, (tr, tr), 1)
    earlier = jnp.where(r_i > c_i, 1.0, 0.0).astype(BF16)
    dest = _dot(earlier, mb) + (start_ref[0:1, :] + run_ref[0:1, :])
    slot = lax.broadcasted_iota(I32, (1, LANE), 1)
    pos8 = jnp.zeros((tr, LANE), F32)
    for k in range(TOP_K):
        pk = jnp.sum(jnp.where(onehots[k], dest, 0.0), axis=-1, keepdims=True)
        pos8 = jnp.where(slot == k, pk, pos8)
    pos_ref[...] = pos8.T[0:TOP_K, :].astype(I32)
    run_ref[...] += jnp.sum(mb.astype(F32), axis=0, keepdims=True)


def _plan(idx8, counts, tr):
    n = idx8.shape[0]
    n_exp = counts.shape[1]
    rows = lambda i: (i, 0)
    return pl.pallas_call(
        functools.partial(_plan_kernel, n_exp=n_exp), grid=(n // tr,),
        in_specs=[pl.BlockSpec((tr, LANE), rows), _const_spec((8, n_exp))],
        out_specs=[pl.BlockSpec((TOP_K, tr), lambda i: (0, i)), pl.BlockSpec((8, n_exp), lambda i: (0, 0))],
        out_shape=[jax.ShapeDtypeStruct((TOP_K, n), I32), jax.ShapeDtypeStruct((8, n_exp), I32)],
        scratch_shapes=[pltpu.VMEM((8, n_exp), F32), pltpu.VMEM((8, n_exp), F32)],
        compiler_params=_cp(("arbitrary",)), name="plan",
    )(idx8, counts)


SC_CORES = 2
SC_SUBCORES = 16
SC_AXES = ("core", "subcore")
SC_WORKERS = SC_CORES * SC_SUBCORES
SC_WINDOW = 128


def _sc_mesh():
    return plsc.VectorSubcoreMesh(core_axis_name=SC_AXES[0], subcore_axis_name=SC_AXES[1],
                                  num_cores=SC_CORES, num_subcores=SC_SUBCORES)


def _sc_worker():
    return lax.axis_index(SC_AXES[1]) * SC_CORES + lax.axis_index(SC_AXES[0])


def _dispatch(h_p, h_s, pos_t, n_rows):
    n_p, c = h_p.shape
    n_s = h_s.shape[0]
    win_p, win_s = n_p // SC_WINDOW, n_s // SC_WINDOW
    assert n_p % (SC_WINDOW * SC_WORKERS) == 0 and n_s % SC_WINDOW == 0 and win_s <= SC_WORKERS

    @functools.partial(
        pl.kernel, out_type=jax.ShapeDtypeStruct((n_rows, c), h_p.dtype), mesh=_sc_mesh(), name="dispatch",
        scratch_types=[pltpu.VMEM((TOP_K, SC_WINDOW), I32), pltpu.VMEM((SC_WINDOW, c), h_p.dtype),
                       pltpu.SemaphoreType.DMA])
    def run(hp_hbm, hs_hbm, pos_hbm, xs_hbm, idx_v, rows_v, sem):
        wid = _sc_worker()

        def scatter_window(h_hbm, row0, tok0):
            pltpu.sync_copy(h_hbm.at[pl.ds(row0, SC_WINDOW)], rows_v)
            pltpu.sync_copy(pos_hbm.at[:, pl.ds(tok0, SC_WINDOW)], idx_v)
            copies = [pltpu.async_copy(rows_v, xs_hbm.at[idx_v.at[k]], sem) for k in range(TOP_K)]
            for cp in copies:
                cp.wait()

        def prompt_window(j, carry):
            row0 = pl.multiple_of((wid * (win_p // SC_WORKERS) + j) * SC_WINDOW, SC_WINDOW)
            scatter_window(hp_hbm, row0, row0)
            return carry
        lax.fori_loop(0, win_p // SC_WORKERS, prompt_window, 0)

        @pl.when(wid < win_s)
        def _():
            row0 = pl.multiple_of(wid * SC_WINDOW, SC_WINDOW)
            scatter_window(hs_hbm, row0, n_p + row0)

    return run(h_p, h_s, pos_t)


def _combine_gather(ys, pos_t):
    k_n = pos_t.shape[0] * pos_t.shape[1]
    c = ys.shape[1]
    win = SC_WINDOW // 2
    per_worker = k_n // (win * SC_WORKERS)
    assert k_n % (win * SC_WORKERS) == 0 and per_worker % 2 == 0
    dma = pltpu.SemaphoreType.DMA

    @functools.partial(
        pl.kernel, out_type=jax.ShapeDtypeStruct((k_n, c), ys.dtype), mesh=_sc_mesh(), name="combine_gather",
        scratch_types=[pltpu.VMEM((win,), I32), pltpu.VMEM((win,), I32),
                       pltpu.VMEM((win, c), ys.dtype), pltpu.VMEM((win, c), ys.dtype), dma, dma, dma, dma])
    def run(ys_hbm, idx_hbm, out_hbm, idx0, idx1, buf0, buf1, gsem0, gsem1, ssem0, ssem1):
        wid = _sc_worker()

        def rows(j):
            return pl.ds(pl.multiple_of((wid * per_worker + j) * win, win), win)

        def gather(j, idx_v, buf, sem):
            pltpu.sync_copy(idx_hbm.at[rows(j)], idx_v)
            pltpu.async_copy(ys_hbm.at[idx_v], buf, sem)

        def gather_wait(idx_v, buf, sem):
            pltpu.make_async_copy(ys_hbm.at[idx_v], buf, sem).wait()

        def store(j, buf, sem):
            pltpu.async_copy(buf, out_hbm.at[rows(j)], sem)

        def store_wait(buf, sem):
            pltpu.make_async_copy(buf, out_hbm.at[rows(0)], sem).wait()

        gather(0, idx0, buf0, gsem0)

        def two_windows(i, carry):
            j = 2 * i

            @pl.when(i > 0)
            def _():
                store_wait(buf1, ssem1)

            gather(j + 1, idx1, buf1, gsem1)
            gather_wait(idx0, buf0, gsem0)
            store(j, buf0, ssem0)

            @pl.when(j + 2 < per_worker)
            def _():
                store_wait(buf0, ssem0)
                gather(j + 2, idx0, buf0, gsem0)

            gather_wait(idx1, buf1, gsem1)
            store(j + 1, buf1, ssem1)
            return carry
        lax.fori_loop(0, per_worker // 2, two_windows, 0)
        store_wait(buf0, ssem0)
        store_wait(buf1, ssem1)

    return run(ys, pos_t.reshape(k_n))


MOE_RING = 6


MOE_WEIGHT_RING = 3


def _moe_kernel(st_ref, wg_ref, wu_ref, wd_ref, xs_ref, ys_ref, xbuf, ybuf, wgf, wuf, wdf, wgb, wub, wdb,
                semx, semy, semw):
    e = pl.program_id(0)
    n_e = pl.num_programs(0)

    def w_copies(ex):
        slot = ex % MOE_WEIGHT_RING
        return [pltpu.make_async_copy(src.at[ex], dst.at[slot], semw.at[slot, i])
                for i, (src, dst) in enumerate(((wg_ref, wgf), (wu_ref, wuf), (wd_ref, wdf)))]

    @pl.when(e == 0)
    def _():
        for ex in range(MOE_WEIGHT_RING - 1):
            @pl.when(ex < n_e)
            def _(ex=ex):
                for cp in w_copies(ex):
                    cp.start()

    @pl.when(e + (MOE_WEIGHT_RING - 1) < n_e)
    def _():
        for cp in w_copies(e + (MOE_WEIGHT_RING - 1)):
            cp.start()
    g0 = st_ref[e] // EXPERT_BLOCK
    nchunks = st_ref[e + 1] // EXPERT_BLOCK - g0
    ntot = st_ref[n_e] // EXPERT_BLOCK

    def rows(g):
        return pl.ds(pl.multiple_of(g * EXPERT_BLOCK, EXPERT_BLOCK), EXPERT_BLOCK)

    def x_copy(g):
        slot = g % MOE_RING
        return pltpu.make_async_copy(xs_ref.at[rows(g)], xbuf.at[slot], semx.at[slot])

    def y_copy(g):
        slot = g % MOE_RING
        return pltpu.make_async_copy(ybuf.at[slot], ys_ref.at[rows(g)], semy.at[slot])

    @pl.when(e == 0)
    def _():
        for g in range(MOE_RING - 1):
            @pl.when(g < ntot)
            def _(g=g):
                x_copy(g).start()

    for cp in w_copies(e):
        cp.wait()
    wslot = e % MOE_WEIGHT_RING
    wgb[...] = wgf[wslot].astype(BF16)
    wub[...] = wuf[wslot].astype(BF16)
    wdb[...] = wdf[wslot].astype(BF16)

    def take(g):
        @pl.when(g + (MOE_RING - 1) < ntot)
        def _():
            x_copy(g + (MOE_RING - 1)).start()

        x_copy(g).wait()

        @pl.when(g >= MOE_RING)
        def _():
            y_copy(g - MOE_RING).wait()

        return xbuf[g % MOE_RING]

    def expert_mlp(x_words):
        x = _unpack_rows(x_words).astype(BF16)
        mid = _silu(_dot(x, wgb[...])) * _dot(x, wub[...])
        return _pack_rows(_dot(mid.astype(BF16), wdb[...]))

    def put(g, y_words):
        ybuf[g % MOE_RING] = y_words
        y_copy(g).start()

    def pair(p, carry):
        ga = g0 + 2 * p
        xa = take(ga)
        xb = take(ga + 1)
        y = expert_mlp(jnp.concatenate([xa, xb], axis=0))
        put(ga, y[:EXPERT_BLOCK])
        put(ga + 1, y[EXPERT_BLOCK:])
        return carry

    lax.fori_loop(0, nchunks // 2, pair, 0)

    @pl.when(nchunks % 2 == 1)
    def _():
        g = g0 + nchunks - 1
        put(g, expert_mlp(take(g)))

    @pl.when(e == n_e - 1)
    def _():
        for j in range(MOE_RING):
            g = ntot - MOE_RING + j

            @pl.when(g >= 0)
            def _(g=g):
                y_copy(g).wait()


def _moe(starts, xs, wg, wu, wd):
    n_rows, c = xs.shape
    n_exp, d, f = wg.shape
    grid_spec = pltpu.PrefetchScalarGridSpec(
        num_scalar_prefetch=1, grid=(n_exp,),
        in_specs=[pl.BlockSpec(memory_space=pl.ANY)] * 4,
        out_specs=pl.BlockSpec(memory_space=pl.ANY),
        scratch_shapes=[pltpu.VMEM((MOE_RING, EXPERT_BLOCK, c), I32), pltpu.VMEM((MOE_RING, EXPERT_BLOCK, c), I32),
                        pltpu.VMEM((MOE_WEIGHT_RING, d, f), F32), pltpu.VMEM((MOE_WEIGHT_RING, d, f), F32),
                        pltpu.VMEM((MOE_WEIGHT_RING, f, d), F32),
                        pltpu.VMEM((d, f), BF16), pltpu.VMEM((d, f), BF16), pltpu.VMEM((f, d), BF16),
                        pltpu.SemaphoreType.DMA((MOE_RING,)), pltpu.SemaphoreType.DMA((MOE_RING,)),
                        pltpu.SemaphoreType.DMA((MOE_WEIGHT_RING, 3))])
    return pl.pallas_call(
        _moe_kernel, grid_spec=grid_spec, out_shape=jax.ShapeDtypeStruct((n_rows, c), I32),
        compiler_params=_cp(("arbitrary",)), name="moe",
    )(starts, wg, wu, wd, xs)


def _final_kernel(yg_ref, w8_ref, x1_ref, h2_ref, mod_ref, wsg_ref, wsu_ref, wsd_ref, nf_ref, y_ref, *, d):
    hb = _unpack_rows(h2_ref[...]).astype(BF16)
    shared = _dot((_silu(_dot(hb, wsg_ref[...])) * _dot(hb, wsu_ref[...])).astype(BF16), wsd_ref[...])
    w8 = w8_ref[...]
    routed = _unpack_rows(yg_ref[0]) * w8[:, 0:1]
    for k in range(1, TOP_K):
        routed = routed + _unpack_rows(yg_ref[k]) * w8[:, k:k + 1]
    g2 = mod_ref[...][:, (N_MOD - 1) * d:N_MOD * d]
    x2 = x1_ref[...] + g2 * (routed + shared)
    y_ref[...] = _rms(x2, nf_ref[...])


def _final(yg3, w8, x1, h2, mod, mod_spec, wts, nf, tf, n_tiles, tile0, out_shape, out_spec):
    d = x1.shape[1]
    rows = lambda i: (i, 0)
    in_specs = [pl.BlockSpec((TOP_K, tf, d // 2), lambda i: (0, tile0 + i, 0)),
                pl.BlockSpec((tf, LANE), lambda i: (tile0 + i, 0)),
                pl.BlockSpec((tf, d), rows), pl.BlockSpec((tf, d // 2), rows), mod_spec]
    in_specs += [_const_spec(w.shape) for w in wts]
    in_specs += [_const_spec(nf.shape)]
    return pl.pallas_call(
        functools.partial(_final_kernel, d=d), grid=(n_tiles,), in_specs=in_specs,
        out_specs=out_spec, out_shape=out_shape,
        compiler_params=_cp(("arbitrary",)), name="final",
    )(yg3, w8, x1, h2, mod, *wts, nf)


def kernel(x_prompt, x_sample, c_prompt, c_sample, state_pool, state_gla, w_ada, b_ada, norm1, w_in,
           pool_w, pool_scale, gla_w_alpha, gla_b_alpha, gla_norm, w_branch_a, w_branch_b, w_out, norm2,
           w_router, router_bias, w_exp_gate, w_exp_up, w_exp_down, w_sh_gate, w_sh_up, w_sh_down, norm_f):
    depth = w_in.shape[0]
    assert depth == 1, "single-layer trunk"
    bp, seq, d = x_prompt.shape
    bs, t_s, _ = x_sample.shape
    _, _, past, pw = state_pool.shape
    _, _, heads, dk, dv = state_gla.shape
    rank = gla_w_alpha.shape[1]
    n_exp = w_router.shape[2]
    assert past == POOL_HIST - 1 and pw == len(POOL_WINDOWS) * LANE and dv == LANE and rank <= LANE
    dm = _Dims(d, pw, heads, dk, dv)
    n_p, n_s = bp * seq, bs * t_s
    n_all = n_p + n_s

    off = [0]
    for sz in (pw, dm.hk, dm.hk, dm.gw, dm.gw, rank, d, d):
        off.append(off[-1] + sz)
    wi = w_in[0]
    win = (wi[:, :off[5]].astype(BF16), wi[:, off[6]:].astype(BF16),
           jnp.concatenate([wi[:, off[5]:off[6]], jnp.zeros((d, LANE - rank), F32)], axis=1).astype(BF16))
    assert off[5] + 2 * d + LANE == dm.zw
    wal = jnp.concatenate([gla_w_alpha[0], jnp.zeros((LANE - rank, dm.hk), F32)], axis=0).astype(BF16)
    bal = gla_b_alpha[0].reshape(1, dm.hk)
    wr = w_router[0]
    wrh = wr.astype(BF16)
    wrl = (wr - wrh.astype(F32)).astype(BF16)
    post_w = (pool_w[0].astype(BF16), pool_scale[0].reshape(1, pw), gla_norm[0].reshape(1, dm.gw),
              w_branch_a[0].astype(BF16), w_branch_b[0].astype(BF16), w_out[0].astype(BF16),
              norm2[0].reshape(1, d), wrh, wrl)
    n1 = norm1[0].reshape(1, d)

    mod = _ada(jnp.concatenate([c_prompt, c_sample], axis=0), w_ada[0], b_ada[0])
    mod_p = mod[:bp].reshape(bp, 1, N_MOD * d)
    mod_s = mod[bp:]

    x1_p, h2_p, lg_p, npool_p, ngla_p = _mix_prompt(
        dm, x_prompt.reshape(n_p, d), mod_p, (n1,) + win + (wal, bal) + post_w, bp, seq, n_exp)

    xs2d = x_sample.reshape(bs, t_s * d)
    z_s = _sample_in(dm, xs2d, mod_s, n1, win, bs, t_s)
    mixed_s, o_s, npool_s, ngla_s = _sample_state(
        dm, z_s, state_pool[0].reshape(bs, past * pw), state_gla[0].reshape(bs, dm.hk * dv), wal, bal, bs, t_s)
    x1_s, h2_s, lg_s = _sample_post(dm, xs2d, mod_s, z_s, mixed_s, o_s, post_w, bs, t_s, n_exp)

    tr = 512 if (n_p % 512 == 0 and n_s % 512 == 0) else LANE
    assert n_p % tr == 0 and n_s % tr == 0 and n_p % bs == 0
    idx8, w8, counts = _route(lg_p, lg_s, router_bias[0], tr)
    nblk = (n_all * TOP_K) // EXPERT_BLOCK + n_exp
    pos_t, bounds = _plan(idx8, counts, tr)
    starts = jnp.concatenate([bounds[0], bounds[1, n_exp - 1:]])

    xs = _dispatch(h2_p, h2_s, pos_t, nblk * EXPERT_BLOCK)
    ys = _moe(starts, xs, w_exp_gate[0], w_exp_up[0], w_exp_down[0])
    yg3 = _combine_gather(ys, pos_t).reshape(TOP_K, n_all, d // 2)

    sh_w = (w_sh_gate[0].astype(BF16), w_sh_up[0].astype(BF16), w_sh_down[0].astype(BF16))
    nf = norm_f.reshape(1, d)
    tf = 256 if seq % 256 == 0 else seq
    n_tp = seq // tf
    y_p = _final(
        yg3, w8, x1_p, h2_p, mod_p,
        pl.BlockSpec((None, 1, N_MOD * d), lambda i: (i // n_tp, 0, 0)), sh_w, nf, tf, n_p // tf, 0,
        jax.ShapeDtypeStruct((n_p, d), F32), pl.BlockSpec((tf, d), lambda i: (i, 0)))
    y_s = _final(
        yg3, w8, x1_s, h2_s, mod_s,
        _const_spec(mod_s.shape), sh_w, nf, bs, t_s, n_p // bs,
        jax.ShapeDtypeStruct((bs, t_s * d), F32), pl.BlockSpec((bs, d), lambda i: (0, i)))

    return (y_p.reshape(bp, seq, d), y_s.reshape(bs, t_s, d),
            npool_p.reshape(depth, bp, past, pw), ngla_p.reshape(depth, bp, heads, dk, dv),
            npool_s.reshape(depth, bs, past, pw), ngla_s.reshape(depth, bs, heads, dk, dv))
```

```python
import functools

import jax
import jax.numpy as jnp
from jax import lax
from jax.experimental import pallas as pl
from jax.experimental.pallas import tpu as pltpu
from jax.experimental.pallas import tpu_sc as plsc

F32 = jnp.float32
BF16 = jnp.bfloat16
I32 = jnp.int32

EPS = 1e-6
N_MOD = 6
POOL_WINDOWS = (2, 4, 8, 16)
POOL_HIST = 16
GLA_GATE_TEMP = 16.0
TOP_K = 8
ROUTED_SCALE = 2.5
PAST_LEN = 16384
EXPERT_BLOCK = 128
LANE = 128
GLA_FAST_MAX_DECAY = 40.0
VMEM_LIMIT = 56 * 1024 * 1024


def _cp(sem, vmem=VMEM_LIMIT):
    return pltpu.CompilerParams(dimension_semantics=sem, vmem_limit_bytes=vmem)


def _rms(x, g):
    return x * lax.rsqrt(jnp.mean(x * x, axis=-1, keepdims=True) + EPS) * g


def _silu(x):
    return x * jax.nn.sigmoid(x)


def _dot(a, b):
    return jnp.dot(a, b, preferred_element_type=F32)


def _split3(a):
    a0 = a.astype(BF16)
    r1 = a - a0.astype(F32)
    a1 = r1.astype(BF16)
    a2 = (r1 - a1.astype(F32)).astype(BF16)
    return a0, a1, a2


def _mod_parts(mod, d):
    return [mod[:, i * d:(i + 1) * d] for i in range(N_MOD)]


_HI16 = -65536


def _pack_rows(v):
    c = v.shape[1] // 2
    lo = lax.bitcast_convert_type(v[:, :c].astype(BF16).astype(F32), I32)
    hi = lax.bitcast_convert_type(v[:, c:].astype(BF16).astype(F32), I32)
    return (hi & _HI16) | lax.shift_right_logical(lo, 16)


def _unpack_rows(w):
    lo = lax.bitcast_convert_type(lax.shift_left(w, 16), F32)
    hi = lax.bitcast_convert_type(w & _HI16, F32)
    return jnp.concatenate([lo, hi], axis=1)


def _ada_kernel(c_ref, w_ref, b_ref, o_ref):
    a = _silu(c_ref[...]).astype(BF16)
    o_ref[...] = _dot(a, w_ref[...].astype(BF16)) + b_ref[...]


def _ada(c_all, w_ada, b_ada):
    n, d = c_all.shape
    cols = w_ada.shape[1]
    bc = d
    return pl.pallas_call(
        _ada_kernel,
        grid=(cols // bc,),
        in_specs=[pl.BlockSpec((n, d), lambda i: (0, 0)),
                  pl.BlockSpec((d, bc), lambda i: (0, i)),
                  pl.BlockSpec((1, bc), lambda i: (0, i))],
        out_specs=pl.BlockSpec((n, bc), lambda i: (0, i)),
        out_shape=jax.ShapeDtypeStruct((n, cols), F32),
        compiler_params=_cp(("arbitrary",)),
        name="ada",
    )(c_all, w_ada, b_ada.reshape(1, cols))


class _Dims:
    def __init__(self, d, pw, heads, dk, dv):
        self.d, self.pw, self.h, self.dk, self.dv = d, pw, heads, dk, dv
        self.hk, self.gw = heads * dk, heads * dv
        self.q0 = pw
        self.k0 = self.q0 + self.hk
        self.v0 = self.k0 + self.hk
        self.go0 = self.v0 + self.gw
        self.ga0 = self.go0 + self.gw
        self.gb0 = self.ga0 + d
        self.al0 = self.gb0 + d
        self.zw = self.al0 + LANE


def _log_decay(z_al, wal_ref, bal_ref):
    xal = _dot(z_al.astype(BF16), wal_ref[...]) + bal_ref[...]
    return jax.nn.log_sigmoid(xal) / GLA_GATE_TEMP


def _pool_project(mixed_groups, pw_ref, ps_ref):
    ys = [_dot(m.astype(BF16), pw_ref[g]) for g, m in enumerate(mixed_groups)]
    return jnp.concatenate(ys, axis=1) * ps_ref[...]


def _post_mix(dm, x, mod, ya, o, go, ga, gb, gn_ref, wa_ref, wb_ref, wo_ref, n2_ref, wrh_ref, wrl_ref):
    _, _, g1, sh2, sc2, _ = _mod_parts(mod, dm.d)
    parts = []
    for h in range(dm.h):
        oh = o[:, h * dm.dv:(h + 1) * dm.dv]
        parts.append(oh * lax.rsqrt(jnp.mean(oh * oh, axis=-1, keepdims=True) + EPS))
    yb = jnp.concatenate(parts, axis=1) * gn_ref[...] * _silu(go)
    m = (jax.nn.sigmoid(ga) * _dot(ya.astype(BF16), wa_ref[...])
         + jax.nn.sigmoid(gb) * _dot(yb.astype(BF16), wb_ref[...]))
    x1 = x + g1 * _dot(m.astype(BF16), wo_ref[...])
    h2 = _rms(x1, n2_ref[...]) * (1 + sc2) + sh2
    hi = h2.astype(BF16)
    lo = (h2 - hi.astype(F32)).astype(BF16)
    logits = _dot(hi, wrh_ref[...]) + (_dot(hi, wrl_ref[...]) + _dot(lo, wrh_ref[...]))
    return x1, _pack_rows(h2), logits


def _in_project(dm, x, mod, n1_ref, win_refs, z_ref):
    h = (_rms(x, n1_ref[...]) * (1 + mod[:, dm.d:2 * dm.d]) + mod[:, 0:dm.d]).astype(BF16)
    c0 = 0
    for w_ref in win_refs:
        z_ref[:, c0:c0 + w_ref.shape[1]] = _dot(h, w_ref[...])
        c0 += w_ref.shape[1]


def _mixp_kernel(x_ref, mod_ref, n1_ref, wia_ref, wig_ref, wil_ref, wal_ref, bal_ref, pw_ref, ps_ref, gn_ref,
                 wa_ref, wb_ref, wo_ref, n2_ref, wrh_ref, wrl_ref,
                 x1_ref, h2_ref, lg_ref, np_ref, ng_ref,
                 z_ref, uext_ref, s_ref, b_ref, oi_ref, *, dm, tt, n_t):
    j = pl.program_id(1)
    x = x_ref[...]
    mod = mod_ref[...]
    _in_project(dm, x, mod, n1_ref, (wia_ref, wig_ref, wil_ref), z_ref)

    @pl.when(j == 0)
    def _():
        uext_ref[0:POOL_HIST, :] = jnp.zeros((POOL_HIST, dm.pw), F32)
        s_ref[...] = jnp.zeros(s_ref.shape, F32)

    uext_ref[POOL_HIST:POOL_HIST + tt, :] = z_ref[:, 0:dm.pw]
    pos = j * tt + lax.broadcasted_iota(I32, (tt, 1), 0)
    gw = dm.pw // len(POOL_WINDOWS)
    mixed = []
    for g, w in enumerate(POOL_WINDOWS):
        c0 = g * gw
        cur = uext_ref[POOL_HIST:POOL_HIST + tt, c0:c0 + gw]
        acc = cur
        for i in range(1, w):
            acc = acc + uext_ref[POOL_HIST - i:POOL_HIST - i + tt, c0:c0 + gw]
        cnt = jnp.minimum(w, pos + 1).astype(F32)
        mixed.append(acc / cnt - cur)
    ya = _pool_project(mixed, pw_ref, ps_ref)

    @pl.when(j == n_t - 1)
    def _():
        np_ref[...] = uext_ref[tt + 1:tt + POOL_HIST, :]

    uext_ref[0:POOL_HIST, :] = uext_ref[tt:tt + POOL_HIST, :]

    la = _log_decay(z_ref[:, dm.al0:dm.al0 + LANE], wal_ref, bal_ref)
    r_i = lax.broadcasted_iota(I32, (tt, tt), 0)
    c_i = lax.broadcasted_iota(I32, (tt, tt), 1)
    causal = r_i >= c_i
    tri = jnp.where(causal, 1.0, 0.0).astype(BF16)
    a0, a1, a2 = _split3(la)
    bc = _dot(tri, a0) + (_dot(tri, a1) + _dot(tri, a2))
    b_ref[...] = bc
    q = z_ref[:, dm.q0:dm.q0 + dm.hk]
    k = z_ref[:, dm.k0:dm.k0 + dm.hk]
    v = z_ref[:, dm.v0:dm.v0 + dm.gw]
    scale = dm.dk ** -0.5
    qe = (q * scale) * jnp.exp(bc)
    blast = bc[tt - 1:tt, :]
    kd = k * jnp.exp(blast - bc)
    s_old = s_ref[...]
    s_bf = s_old.astype(BF16)
    vb = v.astype(BF16)
    lane = lax.broadcasted_iota(I32, (1, dm.hk), 1)
    head_masks = [(lane >= h * dm.dk) & (lane < (h + 1) * dm.dk) for h in range(dm.h)]
    o_inter = [_dot(jnp.where(head_masks[h], qe, 0.0).astype(BF16), s_bf) for h in range(dm.h)]

    fast = jnp.max(-blast) <= GLA_FAST_MAX_DECAY

    @pl.when(fast)
    def _():
        ke = (k * jnp.exp(-bc)).astype(BF16)
        for h in range(dm.h):
            qh = jnp.where(head_masks[h], qe, 0.0).astype(BF16)
            sc = lax.dot_general(qh, ke, (((1,), (1,)), ((), ())), preferred_element_type=F32)
            sc = jnp.where(causal, sc, 0.0).astype(BF16)
            oi_ref[:, h * dm.dv:(h + 1) * dm.dv] = _dot(sc, vb[:, h * dm.dv:(h + 1) * dm.dv])

    @pl.when(jnp.logical_not(fast))
    def _():
        rows = lax.broadcasted_iota(I32, (tt, 1), 0)
        for h in range(dm.h):
            def body(tb, carry, h=h):
                base = pl.multiple_of(tb * 8, 8)
                q8 = z_ref[pl.ds(base, 8), dm.q0:dm.q0 + dm.hk] * scale
                b8 = b_ref[pl.ds(base, 8), :]
                vh = z_ref[:, dm.v0 + h * dm.dv:dm.v0 + (h + 1) * dm.dv]
                out_rows = []
                for r in range(8):
                    keep = (rows <= base + r) & head_masks[h]
                    dec = jnp.exp(jnp.where(keep, b8[r:r + 1, :] - b_ref[...], -jnp.inf))
                    wgt = (q8[r:r + 1, :] * dec) * z_ref[:, dm.k0:dm.k0 + dm.hk]
                    s_col = jnp.sum(wgt, axis=-1, keepdims=True)
                    out_rows.append(jnp.sum(s_col * vh, axis=0, keepdims=True))
                oi_ref[pl.ds(base, 8), h * dm.dv:(h + 1) * dm.dv] = jnp.concatenate(out_rows, axis=0)
                return carry
            lax.fori_loop(0, tt // 8, body, 0)

    o = jnp.concatenate(o_inter, axis=1) + oi_ref[...]

    u_all = _dot(kd.T.astype(BF16), vb)
    dec_col = jnp.broadcast_to(jnp.exp(blast), (dm.hk, dm.hk)).T[:, 0:dm.dv]
    upd = jnp.concatenate(
        [u_all[h * dm.dk:(h + 1) * dm.dk, h * dm.dv:(h + 1) * dm.dv] for h in range(dm.h)], axis=0)
    s_new = dec_col * s_old + upd
    s_ref[...] = s_new

    @pl.when(j == n_t - 1)
    def _():
        ng_ref[...] = s_new

    go = z_ref[:, dm.go0:dm.go0 + dm.gw]
    ga = z_ref[:, dm.ga0:dm.ga0 + dm.d]
    gb = z_ref[:, dm.gb0:dm.gb0 + dm.d]
    x1, h2, logits = _post_mix(dm, x, mod, ya, o, go, ga, gb, gn_ref, wa_ref, wb_ref, wo_ref,
                               n2_ref, wrh_ref, wrl_ref)
    x1_ref[...] = x1
    h2_ref[...] = h2
    lg_ref[...] = logits


def _const_spec(shape):
    nd = len(shape)
    return pl.BlockSpec(shape, lambda *_: (0,) * nd, pipeline_mode=pl.Buffered(1))


def _mix_prompt(dm, x2d, mod3, wts, bsz, seq, n_exp):
    tt = 256 if seq % 256 == 0 else seq
    n_t = seq // tt
    n = bsz * seq
    d = dm.d
    kern = functools.partial(_mixp_kernel, dm=dm, tt=tt, n_t=n_t)
    row = lambda b, j: (b * n_t + j, 0)
    in_specs = [pl.BlockSpec((tt, d), row),
                pl.BlockSpec((None, 1, N_MOD * d), lambda b, j: (b, 0, 0))]
    in_specs += [_const_spec(w.shape) for w in wts]
    out_specs = [pl.BlockSpec((tt, d), row), pl.BlockSpec((tt, d // 2), row), pl.BlockSpec((tt, n_exp), row),
                 pl.BlockSpec((None, POOL_HIST - 1, dm.pw), lambda b, j: (b, 0, 0)),
                 pl.BlockSpec((None, dm.hk, dm.dv), lambda b, j: (b, 0, 0))]
    out_shape = [jax.ShapeDtypeStruct((n, d), F32), jax.ShapeDtypeStruct((n, d // 2), I32),
                 jax.ShapeDtypeStruct((n, n_exp), F32),
                 jax.ShapeDtypeStruct((bsz, POOL_HIST - 1, dm.pw), F32),
                 jax.ShapeDtypeStruct((bsz, dm.hk, dm.dv), F32)]
    scratch = [pltpu.VMEM((tt, dm.zw), F32), pltpu.VMEM((tt + POOL_HIST, dm.pw), F32),
               pltpu.VMEM((dm.hk, dm.dv), F32), pltpu.VMEM((tt, dm.hk), F32), pltpu.VMEM((tt, dm.gw), F32)]
    return pl.pallas_call(
        kern, grid=(bsz, n_t), in_specs=in_specs, out_specs=out_specs, out_shape=out_shape,
        scratch_shapes=scratch, compiler_params=_cp(("arbitrary", "arbitrary")), name="mixp",
    )(x2d, mod3, *wts)


def _sin_kernel(x_ref, mod_ref, n1_ref, wia_ref, wig_ref, wil_ref, z_ref, *, dm):
    _in_project(dm, x_ref[...], mod_ref[...], n1_ref, (wia_ref, wig_ref, wil_ref), z_ref)


def _sample_in(dm, xs2d, mod_s, n1, win, bs, t_s):
    d = dm.d
    return pl.pallas_call(
        functools.partial(_sin_kernel, dm=dm),
        grid=(t_s,),
        in_specs=[pl.BlockSpec((bs, d), lambda t: (0, t)), _const_spec(mod_s.shape), _const_spec(n1.shape)]
        + [_const_spec(w.shape) for w in win],
        out_specs=pl.BlockSpec((bs, dm.zw), lambda t: (t, 0)),
        out_shape=jax.ShapeDtypeStruct((t_s * bs, dm.zw), F32),
        compiler_params=_cp(("arbitrary",)), name="sample_in",
    )(xs2d, mod_s, n1, *win)


def _sst_kernel(z_ref, pool_ref, gs_ref, wal_ref, bal_ref,
                mixed_ref, o_ref, npool_ref, ngs_ref,
                qt_ref, kt_ref, dt_ref, vt_ref, ot_ref, *, dm, bs, t_s, ch, pos0):
    i = pl.program_id(0)
    n_i = pl.num_programs(0)
    scale = dm.dk ** -0.5
    past = POOL_HIST - 1

    @pl.when(i == 0)
    def _():
        for t in range(t_s):
            r0 = t * bs
            la = _log_decay(z_ref[r0:r0 + bs, dm.al0:dm.al0 + LANE], wal_ref, bal_ref)
            dt_ref[t] = jnp.exp(la).T
            qt_ref[t] = (z_ref[r0:r0 + bs, dm.q0:dm.q0 + dm.hk] * scale).T
            kt_ref[t] = z_ref[r0:r0 + bs, dm.k0:dm.k0 + dm.hk].T
            vt_ref[t] = z_ref[r0:r0 + bs, dm.v0:dm.v0 + dm.gw].T
        ot_ref[...] = jnp.zeros(ot_ref.shape, F32)

        def item(s, c0, c1):
            if s >= 0:
                return z_ref[s * bs:(s + 1) * bs, c0:c1]
            return pool_ref[:, (past + s) * dm.pw + c0:(past + s) * dm.pw + c1]

        gw = dm.pw // len(POOL_WINDOWS)
        for t in range(t_s):
            for g, w in enumerate(POOL_WINDOWS):
                c0, c1 = g * gw, (g + 1) * gw
                cur = item(t, c0, c1)
                acc = cur
                for r in range(1, w):
                    acc = acc + item(t - r, c0, c1)
                cnt = float(min(w, pos0 + t + 1))
                mixed_ref[t * bs:(t + 1) * bs, c0:c1] = acc / cnt - cur
        for r in range(past):
            npool_ref[:, r * dm.pw:(r + 1) * dm.pw] = item(r - past + t_s, 0, dm.pw)

    per_head = dm.dk // ch
    hoff = pl.multiple_of((i // per_head) * dm.dv, dm.dv)
    for p in range(ch):
        hd = i * ch + p
        st = gs_ref[:, p, :].T
        for t in range(t_s):
            st = dt_ref[t, pl.ds(hd, 1), :] * st + kt_ref[t, pl.ds(hd, 1), :] * vt_ref[t, pl.ds(hoff, dm.dv), :]
            ot_ref[t, pl.ds(hoff, dm.dv), :] += qt_ref[t, pl.ds(hd, 1), :] * st
        ngs_ref[:, p, :] = st.T

    @pl.when(i == n_i - 1)
    def _():
        for t in range(t_s):
            for h in range(dm.h):
                o_ref[t * bs:(t + 1) * bs, h * dm.dv:(h + 1) * dm.dv] = ot_ref[t, h * dm.dv:(h + 1) * dm.dv, :].T


def _sample_state(dm, z_s, pool2d, gs3, wal, bal, bs, t_s):
    ch = 16
    n_i = dm.hk // ch
    past = POOL_HIST - 1
    kern = functools.partial(_sst_kernel, dm=dm, bs=bs, t_s=t_s, ch=ch, pos0=PAST_LEN)
    full = lambda shape: pl.BlockSpec(shape, lambda i: (0,) * len(shape))
    return pl.pallas_call(
        kern, grid=(n_i,),
        in_specs=[_const_spec(z_s.shape), _const_spec(pool2d.shape),
                  pl.BlockSpec((bs, ch, dm.dv), lambda i: (0, i, 0)),
                  _const_spec(wal.shape), _const_spec(bal.shape)],
        out_specs=[full((t_s * bs, dm.pw)), full((t_s * bs, dm.gw)), full((bs, past * dm.pw)),
                   pl.BlockSpec((bs, ch, dm.dv), lambda i: (0, i, 0))],
        out_shape=[jax.ShapeDtypeStruct((t_s * bs, dm.pw), F32), jax.ShapeDtypeStruct((t_s * bs, dm.gw), F32),
                   jax.ShapeDtypeStruct((bs, past * dm.pw), F32), jax.ShapeDtypeStruct(gs3.shape, F32)],
        scratch_shapes=[pltpu.VMEM((t_s, dm.hk, bs), F32), pltpu.VMEM((t_s, dm.hk, bs), F32),
                        pltpu.VMEM((t_s, dm.hk, bs), F32), pltpu.VMEM((t_s, dm.gw, bs), F32),
                        pltpu.VMEM((t_s, dm.gw, bs), F32)],
        compiler_params=_cp(("arbitrary",)), name="sample_state",
    )(z_s, pool2d, gs3, wal, bal)


def _spost_kernel(x_ref, mod_ref, z_ref, mixed_ref, o_ref, pw_ref, ps_ref, gn_ref,
                  wa_ref, wb_ref, wo_ref, n2_ref, wrh_ref, wrl_ref,
                  x1_ref, h2_ref, lg_ref, *, dm):
    gw = dm.pw // len(POOL_WINDOWS)
    ya = _pool_project([mixed_ref[:, g * gw:(g + 1) * gw] for g in range(len(POOL_WINDOWS))], pw_ref, ps_ref)
    x1, h2, logits = _post_mix(
        dm, x_ref[...], mod_ref[...], ya, o_ref[...],
        z_ref[:, dm.go0:dm.go0 + dm.gw], z_ref[:, dm.ga0:dm.ga0 + dm.d], z_ref[:, dm.gb0:dm.gb0 + dm.d],
        gn_ref, wa_ref, wb_ref, wo_ref, n2_ref, wrh_ref, wrl_ref)
    x1_ref[...] = x1
    h2_ref[...] = h2
    lg_ref[...] = logits


def _sample_post(dm, xs2d, mod_s, z_s, mixed_s, o_s, wts, bs, t_s, n_exp):
    d = dm.d
    rows = lambda t: (t, 0)
    in_specs = [pl.BlockSpec((bs, d), lambda t: (0, t)), _const_spec(mod_s.shape),
                pl.BlockSpec((bs, dm.zw), rows), pl.BlockSpec((bs, dm.pw), rows), pl.BlockSpec((bs, dm.gw), rows)]
    in_specs += [_const_spec(w.shape) for w in wts]
    return pl.pallas_call(
        functools.partial(_spost_kernel, dm=dm), grid=(t_s,), in_specs=in_specs,
        out_specs=[pl.BlockSpec((bs, d), rows), pl.BlockSpec((bs, d // 2), rows), pl.BlockSpec((bs, n_exp), rows)],
        out_shape=[jax.ShapeDtypeStruct((t_s * bs, d), F32), jax.ShapeDtypeStruct((t_s * bs, d // 2), I32),
                   jax.ShapeDtypeStruct((t_s * bs, n_exp), F32)],
        compiler_params=_cp(("arbitrary",)), name="sample_post",
    )(xs2d, mod_s, z_s, mixed_s, o_s, *wts)


def _route_kernel(lgp_ref, lgs_ref, bias_ref, idx_ref, w_ref, cnt_ref, *, n_exp, tiles_p):
    i = pl.program_id(0)
    s = jax.nn.sigmoid(jnp.where(i < tiles_p, lgp_ref[...], lgs_ref[...]))
    cur = s + bias_ref[...]
    tr = s.shape[0]
    lane = lax.broadcasted_iota(I32, (1, n_exp), 1).astype(F32)
    slot = lax.broadcasted_iota(I32, (1, LANE), 1)
    idx8 = jnp.zeros((tr, LANE), F32)
    w8 = jnp.zeros((tr, LANE), F32)
    wsum = jnp.zeros((tr, 1), F32)
    picked = jnp.zeros((tr, n_exp), F32)
    for k in range(TOP_K):
        m = jnp.max(cur, axis=-1, keepdims=True)
        am = jnp.min(jnp.where(cur == m, lane, float(n_exp)), axis=-1, keepdims=True)
        oh = lane == am
        sk = jnp.sum(jnp.where(oh, s, 0.0), axis=-1, keepdims=True)
        cur = jnp.where(oh, -jnp.inf, cur)
        idx8 = jnp.where(slot == k, am, idx8)
        w8 = jnp.where(slot == k, sk, w8)
        wsum = wsum + sk
        picked = picked + jnp.where(oh, 1.0, 0.0)
    idx_ref[...] = idx8.astype(I32)
    w_ref[...] = w8 / wsum * ROUTED_SCALE

    @pl.when(i == 0)
    def _():
        cnt_ref[...] = jnp.zeros(cnt_ref.shape, F32)

    cnt_ref[...] += jnp.sum(picked, axis=0, keepdims=True)


def _two_source_specs(block, tiles_p, tiles_s):
    return [pl.BlockSpec(block, lambda i: (jnp.minimum(i, tiles_p - 1), 0)),
            pl.BlockSpec(block, lambda i: (jnp.clip(i - tiles_p, 0, tiles_s - 1), 0))]


def _route(lg_p, lg_s, bias, tr):
    n_exp = lg_p.shape[1]
    tiles_p, tiles_s = lg_p.shape[0] // tr, lg_s.shape[0] // tr
    n = (tiles_p + tiles_s) * tr
    rows = lambda i: (i, 0)
    return pl.pallas_call(
        functools.partial(_route_kernel, n_exp=n_exp, tiles_p=tiles_p), grid=(tiles_p + tiles_s,),
        in_specs=_two_source_specs((tr, n_exp), tiles_p, tiles_s) + [_const_spec((1, n_exp))],
        out_specs=[pl.BlockSpec((tr, LANE), rows), pl.BlockSpec((tr, LANE), rows),
                   pl.BlockSpec((8, n_exp), lambda i: (0, 0))],
        out_shape=[jax.ShapeDtypeStruct((n, LANE), I32), jax.ShapeDtypeStruct((n, LANE), F32),
                   jax.ShapeDtypeStruct((8, n_exp), F32)],
        compiler_params=_cp(("arbitrary",)), name="route",
    )(lg_p, lg_s, bias.reshape(1, n_exp))


def _plan_kernel(idx_ref, cnt_ref, pos_ref, bounds_ref, start_ref, run_ref, *, n_exp):
    i = pl.program_id(0)
    lane = lax.broadcasted_iota(I32, (1, n_exp), 1)

    @pl.when(i == 0)
    def _():
        cnt = cnt_ref[...].astype(I32)
        padded = ((cnt + (EXPERT_BLOCK - 1)) // EXPERT_BLOCK) * EXPERT_BLOCK
        pe = padded
        s = 1
        while s < n_exp:
            pe = pe + jnp.where(lane >= s, pltpu.roll(pe, s, 1), 0)
            s *= 2
        start_ref[...] = (pe - padded).astype(F32)
        run_ref[...] = jnp.zeros(run_ref.shape, F32)
        row = lax.broadcasted_iota(I32, (8, n_exp), 0)
        bounds_ref[...] = jnp.where(row == 0, pe - padded, pe)

    idx = idx_ref[...]
    tr = idx.shape[0]
    onehots = [lane == idx[:, k:k + 1] for k in range(TOP_K)]
    member = onehots[0]
    for k in range(1, TOP_K):
        member = member | onehots[k]
    mb = jnp.where(member, 1.0, 0.0).astype(BF16)
    r_i = lax.broadcasted_iota(I32, (tr, tr), 0)
    c_i = lax.broadcasted_iota(I32, (tr, tr), 1)
    earlier = jnp.where(r_i > c_i, 1.0, 0.0).astype(BF16)
    dest = _dot(earlier, mb) + (start_ref[0:1, :] + run_ref[0:1, :])
    slot = lax.broadcasted_iota(I32, (1, LANE), 1)
    pos8 = jnp.zeros((tr, LANE), F32)
    for k in range(TOP_K):
        pk = jnp.sum(jnp.where(onehots[k], dest, 0.0), axis=-1, keepdims=True)
        pos8 = jnp.where(slot == k, pk, pos8)
    pos_ref[...] = pos8.T[0:TOP_K, :].astype(I32)
    run_ref[...] += jnp.sum(mb.astype(F32), axis=0, keepdims=True)


def _plan(idx8, counts, tr):
    n = idx8.shape[0]
    n_exp = counts.shape[1]
    rows = lambda i: (i, 0)
    return pl.pallas_call(
        functools.partial(_plan_kernel, n_exp=n_exp), grid=(n // tr,),
        in_specs=[pl.BlockSpec((tr, LANE), rows), _const_spec((8, n_exp))],
        out_specs=[pl.BlockSpec((TOP_K, tr), lambda i: (0, i)), pl.BlockSpec((8, n_exp), lambda i: (0, 0))],
        out_shape=[jax.ShapeDtypeStruct((TOP_K, n), I32), jax.ShapeDtypeStruct((8, n_exp), I32)],
        scratch_shapes=[pltpu.VMEM((8, n_exp), F32), pltpu.VMEM((8, n_exp), F32)],
        compiler_params=_cp(("arbitrary",)), name="plan",
    )(idx8, counts)


SC_CORES = 2
SC_SUBCORES = 16
SC_AXES = ("core", "subcore")
SC_WORKERS = SC_CORES * SC_SUBCORES
SC_WINDOW = 128


def _sc_mesh():
    return plsc.VectorSubcoreMesh(core_axis_name=SC_AXES[0], subcore_axis_name=SC_AXES[1],
                                  num_cores=SC_CORES, num_subcores=SC_SUBCORES)


def _sc_worker():
    return lax.axis_index(SC_AXES[1]) * SC_CORES + lax.axis_index(SC_AXES[0])


def _dispatch(h_p, h_s, pos_t, n_rows):
    n_p, c = h_p.shape
    n_s = h_s.shape[0]
    win_p, win_s = n_p // SC_WINDOW, n_s // SC_WINDOW
    assert n_p % (SC_WINDOW * SC_WORKERS) == 0 and n_s % SC_WINDOW == 0 and win_s <= SC_WORKERS

    @functools.partial(
        pl.kernel, out_type=jax.ShapeDtypeStruct((n_rows, c), h_p.dtype), mesh=_sc_mesh(), name="dispatch",
        scratch_types=[pltpu.VMEM((TOP_K, SC_WINDOW), I32), pltpu.VMEM((SC_WINDOW, c), h_p.dtype),
                       pltpu.SemaphoreType.DMA])
    def run(hp_hbm, hs_hbm, pos_hbm, xs_hbm, idx_v, rows_v, sem):
        wid = _sc_worker()

        def scatter_window(h_hbm, row0, tok0):
            pltpu.sync_copy(h_hbm.at[pl.ds(row0, SC_WINDOW)], rows_v)
            pltpu.sync_copy(pos_hbm.at[:, pl.ds(tok0, SC_WINDOW)], idx_v)
            copies = [pltpu.async_copy(rows_v, xs_hbm.at[idx_v.at[k]], sem) for k in range(TOP_K)]
            for cp in copies:
                cp.wait()

        def prompt_window(j, carry):
            row0 = pl.multiple_of((wid * (win_p // SC_WORKERS) + j) * SC_WINDOW, SC_WINDOW)
            scatter_window(hp_hbm, row0, row0)
            return carry
        lax.fori_loop(0, win_p // SC_WORKERS, prompt_window, 0)

        @pl.when(wid < win_s)
        def _():
            row0 = pl.multiple_of(wid * SC_WINDOW, SC_WINDOW)
            scatter_window(hs_hbm, row0, n_p + row0)

    return run(h_p, h_s, pos_t)


def _combine_gather(ys, pos_t):
    k_n = pos_t.shape[0] * pos_t.shape[1]
    c = ys.shape[1]
    win = SC_WINDOW // 2
    per_worker = k_n // (win * SC_WORKERS)
    assert k_n % (win * SC_WORKERS) == 0 and per_worker % 2 == 0
    dma = pltpu.SemaphoreType.DMA

    @functools.partial(
        pl.kernel, out_type=jax.ShapeDtypeStruct((k_n, c), ys.dtype), mesh=_sc_mesh(), name="combine_gather",
        scratch_types=[pltpu.VMEM((win,), I32), pltpu.VMEM((win,), I32),
                       pltpu.VMEM((win, c), ys.dtype), pltpu.VMEM((win, c), ys.dtype), dma, dma, dma, dma])
    def run(ys_hbm, idx_hbm, out_hbm, idx0, idx1, buf0, buf1, gsem0, gsem1, ssem0, ssem1):
        wid = _sc_worker()

        def rows(j):
            return pl.ds(pl.multiple_of((wid * per_worker + j) * win, win), win)

        def gather(j, idx_v, buf, sem):
            pltpu.sync_copy(idx_hbm.at[rows(j)], idx_v)
            pltpu.async_copy(ys_hbm.at[idx_v], buf, sem)

        def gather_wait(idx_v, buf, sem):
            pltpu.make_async_copy(ys_hbm.at[idx_v], buf, sem).wait()

        def store(j, buf, sem):
            pltpu.async_copy(buf, out_hbm.at[rows(j)], sem)

        def store_wait(buf, sem):
            pltpu.make_async_copy(buf, out_hbm.at[rows(0)], sem).wait()

        gather(0, idx0, buf0, gsem0)

        def two_windows(i, carry):
            j = 2 * i

            @pl.when(i > 0)
            def _():
                store_wait(buf1, ssem1)

            gather(j + 1, idx1, buf1, gsem1)
            gather_wait(idx0, buf0, gsem0)
            store(j, buf0, ssem0)

            @pl.when(j + 2 < per_worker)
            def _():
                store_wait(buf0, ssem0)
                gather(j + 2, idx0, buf0, gsem0)

            gather_wait(idx1, buf1, gsem1)
            store(j + 1, buf1, ssem1)
            return carry
        lax.fori_loop(0, per_worker // 2, two_windows, 0)
        store_wait(buf0, ssem0)
        store_wait(buf1, ssem1)

    return run(ys, pos_t.reshape(k_n))


MOE_RING = 8


MOE_WEIGHT_RING = 4


def _moe_kernel(st_ref, wg_ref, wu_ref, wd_ref, xs_ref, ys_ref, xbuf, ybuf, wgf, wuf, wdf, wgb, wub, wdb,
                semx, semy, semw):
    e = pl.program_id(0)
    n_e = pl.num_programs(0)

    def w_copies(ex):
        slot = ex % MOE_WEIGHT_RING
        return [pltpu.make_async_copy(src.at[ex], dst.at[slot], semw.at[slot, i])
                for i, (src, dst) in enumerate(((wg_ref, wgf), (wu_ref, wuf), (wd_ref, wdf)))]

    @pl.when(e == 0)
    def _():
        for ex in range(MOE_WEIGHT_RING - 1):
            @pl.when(ex < n_e)
            def _(ex=ex):
                for cp in w_copies(ex):
                    cp.start()

    @pl.when(e + (MOE_WEIGHT_RING - 1) < n_e)
    def _():
        for cp in w_copies(e + (MOE_WEIGHT_RING - 1)):
            cp.start()
    g0 = st_ref[e] // EXPERT_BLOCK
    nchunks = st_ref[e + 1] // EXPERT_BLOCK - g0
    ntot = st_ref[n_e] // EXPERT_BLOCK

    def rows(g):
        return pl.ds(pl.multiple_of(g * EXPERT_BLOCK, EXPERT_BLOCK), EXPERT_BLOCK)

    def x_copy(g):
        slot = g % MOE_RING
        return pltpu.make_async_copy(xs_ref.at[rows(g)], xbuf.at[slot], semx.at[slot])

    def y_copy(g):
        slot = g % MOE_RING
        return pltpu.make_async_copy(ybuf.at[slot], ys_ref.at[rows(g)], semy.at[slot])

    @pl.when(e == 0)
    def _():
        for g in range(MOE_RING - 1):
            @pl.when(g < ntot)
            def _(g=g):
                x_copy(g).start()

    for cp in w_copies(e):
        cp.wait()
    wslot = e % MOE_WEIGHT_RING
    wgb[...] = wgf[wslot].astype(BF16)
    wub[...] = wuf[wslot].astype(BF16)
    wdb[...] = wdf[wslot].astype(BF16)

    def take(g):
        @pl.when(g + (MOE_RING - 1) < ntot)
        def _():
            x_copy(g + (MOE_RING - 1)).start()

        x_copy(g).wait()

        @pl.when(g >= MOE_RING)
        def _():
            y_copy(g - MOE_RING).wait()

        return xbuf[g % MOE_RING]

    def expert_mlp(x_words):
        x = _unpack_rows(x_words).astype(BF16)
        mid = _silu(_dot(x, wgb[...])) * _dot(x, wub[...])
        return _pack_rows(_dot(mid.astype(BF16), wdb[...]))

    def put(g, y_words):
        ybuf[g % MOE_RING] = y_words
        y_copy(g).start()

    def run_chunks(g, m):
        xs = [take(g + i) for i in range(m)]
        y = expert_mlp(xs[0] if m == 1 else jnp.concatenate(xs, axis=0))
        for i in range(m):
            put(g + i, y[i * EXPERT_BLOCK:(i + 1) * EXPERT_BLOCK])

    def quad(p, carry):
        run_chunks(g0 + 4 * p, 4)
        return carry

    lax.fori_loop(0, nchunks // 4, quad, 0)
    rest = nchunks % 4
    g_rest = g0 + nchunks - rest

    @pl.when(rest >= 2)
    def _():
        run_chunks(g_rest, 2)

    @pl.when(rest % 2 == 1)
    def _():
        run_chunks(g0 + nchunks - 1, 1)

    @pl.when(e == n_e - 1)
    def _():
        for j in range(MOE_RING):
            g = ntot - MOE_RING + j

            @pl.when(g >= 0)
            def _(g=g):
                y_copy(g).wait()


def _moe(starts, xs, wg, wu, wd):
    n_rows, c = xs.shape
    n_exp, d, f = wg.shape
    grid_spec = pltpu.PrefetchScalarGridSpec(
        num_scalar_prefetch=1, grid=(n_exp,),
        in_specs=[pl.BlockSpec(memory_space=pl.ANY)] * 4,
        out_specs=pl.BlockSpec(memory_space=pl.ANY),
        scratch_shapes=[pltpu.VMEM((MOE_RING, EXPERT_BLOCK, c), I32), pltpu.VMEM((MOE_RING, EXPERT_BLOCK, c), I32),
                        pltpu.VMEM((MOE_WEIGHT_RING, d, f), F32), pltpu.VMEM((MOE_WEIGHT_RING, d, f), F32),
                        pltpu.VMEM((MOE_WEIGHT_RING, f, d), F32),
                        pltpu.VMEM((d, f), BF16), pltpu.VMEM((d, f), BF16), pltpu.VMEM((f, d), BF16),
                        pltpu.SemaphoreType.DMA((MOE_RING,)), pltpu.SemaphoreType.DMA((MOE_RING,)),
                        pltpu.SemaphoreType.DMA((MOE_WEIGHT_RING, 3))])
    return pl.pallas_call(
        _moe_kernel, grid_spec=grid_spec, out_shape=jax.ShapeDtypeStruct((n_rows, c), I32),
        compiler_params=_cp(("arbitrary",)), name="moe",
    )(starts, wg, wu, wd, xs)


def _final_kernel(yg_ref, w8_ref, x1_ref, h2_ref, mod_ref, wsg_ref, wsu_ref, wsd_ref, nf_ref, y_ref, *, d):
    hb = _unpack_rows(h2_ref[...]).astype(BF16)
    shared = _dot((_silu(_dot(hb, wsg_ref[...])) * _dot(hb, wsu_ref[...])).astype(BF16), wsd_ref[...])
    w8 = w8_ref[...]
    routed = _unpack_rows(yg_ref[0]) * w8[:, 0:1]
    for k in range(1, TOP_K):
        routed = routed + _unpack_rows(yg_ref[k]) * w8[:, k:k + 1]
    g2 = mod_ref[...][:, (N_MOD - 1) * d:N_MOD * d]
    x2 = x1_ref[...] + g2 * (routed + shared)
    y_ref[...] = _rms(x2, nf_ref[...])


def _final(yg3, w8, x1, h2, mod, mod_spec, wts, nf, tf, n_tiles, tile0, out_shape, out_spec):
    d = x1.shape[1]
    rows = lambda i: (i, 0)
    in_specs = [pl.BlockSpec((TOP_K, tf, d // 2), lambda i: (0, tile0 + i, 0)),
                pl.BlockSpec((tf, LANE), lambda i: (tile0 + i, 0)),
                pl.BlockSpec((tf, d), rows), pl.BlockSpec((tf, d // 2), rows), mod_spec]
    in_specs += [_const_spec(w.shape) for w in wts]
    in_specs += [_const_spec(nf.shape)]
    return pl.pallas_call(
        functools.partial(_final_kernel, d=d), grid=(n_tiles,), in_specs=in_specs,
        out_specs=out_spec, out_shape=out_shape,
        compiler_params=_cp(("arbitrary",)), name="final",
    )(yg3, w8, x1, h2, mod, *wts, nf)


def kernel(x_prompt, x_sample, c_prompt, c_sample, state_pool, state_gla, w_ada, b_ada, norm1, w_in,
           pool_w, pool_scale, gla_w_alpha, gla_b_alpha, gla_norm, w_branch_a, w_branch_b, w_out, norm2,
           w_router, router_bias, w_exp_gate, w_exp_up, w_exp_down, w_sh_gate, w_sh_up, w_sh_down, norm_f):
    depth = w_in.shape[0]
    assert depth == 1, "single-layer trunk"
    bp, seq, d = x_prompt.shape
    bs, t_s, _ = x_sample.shape
    _, _, past, pw = state_pool.shape
    _, _, heads, dk, dv = state_gla.shape
    rank = gla_w_alpha.shape[1]
    n_exp = w_router.shape[2]
    assert past == POOL_HIST - 1 and pw == len(POOL_WINDOWS) * LANE and dv == LANE and rank <= LANE
    dm = _Dims(d, pw, heads, dk, dv)
    n_p, n_s = bp * seq, bs * t_s
    n_all = n_p + n_s

    off = [0]
    for sz in (pw, dm.hk, dm.hk, dm.gw, dm.gw, rank, d, d):
        off.append(off[-1] + sz)
    wi = w_in[0]
    win = (wi[:, :off[5]].astype(BF16), wi[:, off[6]:].astype(BF16),
           jnp.concatenate([wi[:, off[5]:off[6]], jnp.zeros((d, LANE - rank), F32)], axis=1).astype(BF16))
    assert off[5] + 2 * d + LANE == dm.zw
    wal = jnp.concatenate([gla_w_alpha[0], jnp.zeros((LANE - rank, dm.hk), F32)], axis=0).astype(BF16)
    bal = gla_b_alpha[0].reshape(1, dm.hk)
    wr = w_router[0]
    wrh = wr.astype(BF16)
    wrl = (wr - wrh.astype(F32)).astype(BF16)
    post_w = (pool_w[0].astype(BF16), pool_scale[0].reshape(1, pw), gla_norm[0].reshape(1, dm.gw),
              w_branch_a[0].astype(BF16), w_branch_b[0].astype(BF16), w_out[0].astype(BF16),
              norm2[0].reshape(1, d), wrh, wrl)
    n1 = norm1[0].reshape(1, d)

    mod = _ada(jnp.concatenate([c_prompt, c_sample], axis=0), w_ada[0], b_ada[0])
    mod_p = mod[:bp].reshape(bp, 1, N_MOD * d)
    mod_s = mod[bp:]

    x1_p, h2_p, lg_p, npool_p, ngla_p = _mix_prompt(
        dm, x_prompt.reshape(n_p, d), mod_p, (n1,) + win + (wal, bal) + post_w, bp, seq, n_exp)

    xs2d = x_sample.reshape(bs, t_s * d)
    z_s = _sample_in(dm, xs2d, mod_s, n1, win, bs, t_s)
    mixed_s, o_s, npool_s, ngla_s = _sample_state(
        dm, z_s, state_pool[0].reshape(bs, past * pw), state_gla[0].reshape(bs, dm.hk, dv), wal, bal, bs, t_s)
    x1_s, h2_s, lg_s = _sample_post(dm, xs2d, mod_s, z_s, mixed_s, o_s, post_w, bs, t_s, n_exp)

    tr = 512 if (n_p % 512 == 0 and n_s % 512 == 0) else LANE
    assert n_p % tr == 0 and n_s % tr == 0 and n_p % bs == 0
    idx8, w8, counts = _route(lg_p, lg_s, router_bias[0], tr)
    nblk = (n_all * TOP_K) // EXPERT_BLOCK + n_exp
    pos_t, bounds = _plan(idx8, counts, tr)
    starts = jnp.concatenate([bounds[0], bounds[1, n_exp - 1:]])

    xs = _dispatch(h2_p, h2_s, pos_t, nblk * EXPERT_BLOCK)
    ys = _moe(starts, xs, w_exp_gate[0], w_exp_up[0], w_exp_down[0])
    yg3 = _combine_gather(ys, pos_t).reshape(TOP_K, n_all, d // 2)

    sh_w = (w_sh_gate[0].astype(BF16), w_sh_up[0].astype(BF16), w_sh_down[0].astype(BF16))
    nf = norm_f.reshape(1, d)
    tf = 256 if seq % 256 == 0 else seq
    n_tp = seq // tf
    y_p = _final(
        yg3, w8, x1_p, h2_p, mod_p,
        pl.BlockSpec((None, 1, N_MOD * d), lambda i: (i // n_tp, 0, 0)), sh_w, nf, tf, n_p // tf, 0,
        jax.ShapeDtypeStruct((n_p, d), F32), pl.BlockSpec((tf, d), lambda i: (i, 0)))
    y_s = _final(
        yg3, w8, x1_s, h2_s, mod_s,
        _const_spec(mod_s.shape), sh_w, nf, bs, t_s, n_p // bs,
        jax.ShapeDtypeStruct((bs, t_s * d), F32), pl.BlockSpec((bs, d), lambda i: (0, i)))

    return (y_p.reshape(bp, seq, d), y_s.reshape(bs, t_s, d),
            npool_p.reshape(depth, bp, past, pw), ngla_p.reshape(depth, bp, heads, dk, dv),
            npool_s.reshape(depth, bs, past, pw), ngla_s.reshape(depth, bs, heads, dk, dv))
```

```python
import functools

import jax
import jax.numpy as jnp
from jax import lax
from jax.experimental import pallas as pl
from jax.experimental.pallas import tpu as pltpu
from jax.experimental.pallas import tpu_sc as plsc

F32 = jnp.float32
BF16 = jnp.bfloat16
I32 = jnp.int32

EPS = 1e-6
N_MOD = 6
POOL_WINDOWS = (2, 4, 8, 16)
POOL_HIST = 16
GLA_GATE_TEMP = 16.0
TOP_K = 8
ROUTED_SCALE = 2.5
PAST_LEN = 16384
EXPERT_BLOCK = 128
LANE = 128
GLA_FAST_MAX_DECAY = 40.0
VMEM_LIMIT = 56 * 1024 * 1024


def _cp(sem, vmem=VMEM_LIMIT):
    return pltpu.CompilerParams(dimension_semantics=sem, vmem_limit_bytes=vmem)


def _rms(x, g):
    return x * lax.rsqrt(jnp.mean(x * x, axis=-1, keepdims=True) + EPS) * g


def _silu(x):
    return x * jax.nn.sigmoid(x)


def _dot(a, b):
    return jnp.dot(a, b, preferred_element_type=F32)


def _split3(a):
    a0 = a.astype(BF16)
    r1 = a - a0.astype(F32)
    a1 = r1.astype(BF16)
    a2 = (r1 - a1.astype(F32)).astype(BF16)
    return a0, a1, a2


def _mod_parts(mod, d):
    return [mod[:, i * d:(i + 1) * d] for i in range(N_MOD)]


_HI16 = -65536


def _pack_rows(v):
    c = v.shape[1] // 2
    lo = lax.bitcast_convert_type(v[:, :c].astype(BF16).astype(F32), I32)
    hi = lax.bitcast_convert_type(v[:, c:].astype(BF16).astype(F32), I32)
    return (hi & _HI16) | lax.shift_right_logical(lo, 16)


def _unpack_rows(w):
    lo = lax.bitcast_convert_type(lax.shift_left(w, 16), F32)
    hi = lax.bitcast_convert_type(w & _HI16, F32)
    return jnp.concatenate([lo, hi], axis=1)


def _ada_kernel(c_ref, w_ref, b_ref, o_ref):
    a = _silu(c_ref[...]).astype(BF16)
    o_ref[...] = _dot(a, w_ref[...].astype(BF16)) + b_ref[...]


def _ada(c_all, w_ada, b_ada):
    n, d = c_all.shape
    cols = w_ada.shape[1]
    bc = d
    return pl.pallas_call(
        _ada_kernel,
        grid=(cols // bc,),
        in_specs=[pl.BlockSpec((n, d), lambda i: (0, 0)),
                  pl.BlockSpec((d, bc), lambda i: (0, i)),
                  pl.BlockSpec((1, bc), lambda i: (0, i))],
        out_specs=pl.BlockSpec((n, bc), lambda i: (0, i)),
        out_shape=jax.ShapeDtypeStruct((n, cols), F32),
        compiler_params=_cp(("arbitrary",)),
        name="ada",
    )(c_all, w_ada, b_ada.reshape(1, cols))


class _Dims:
    def __init__(self, d, pw, heads, dk, dv):
        self.d, self.pw, self.h, self.dk, self.dv = d, pw, heads, dk, dv
        self.hk, self.gw = heads * dk, heads * dv
        self.q0 = pw
        self.k0 = self.q0 + self.hk
        self.v0 = self.k0 + self.hk
        self.go0 = self.v0 + self.gw
        self.ga0 = self.go0 + self.gw
        self.gb0 = self.ga0 + d
        self.al0 = self.gb0 + d
        self.zw = self.al0 + LANE


def _log_decay(z_al, wal_ref, bal_ref):
    xal = _dot(z_al.astype(BF16), wal_ref[...]) + bal_ref[...]
    return jax.nn.log_sigmoid(xal) / GLA_GATE_TEMP


def _pool_project(mixed_groups, pw_ref, ps_ref):
    ys = [_dot(m.astype(BF16), pw_ref[g]) for g, m in enumerate(mixed_groups)]
    return jnp.concatenate(ys, axis=1) * ps_ref[...]


def _post_mix(dm, x, mod, ya, o, go, ga, gb, gn_ref, wa_ref, wb_ref, wo_ref, n2_ref, wrh_ref, wrl_ref):
    _, _, g1, sh2, sc2, _ = _mod_parts(mod, dm.d)
    parts = []
    for h in range(dm.h):
        oh = o[:, h * dm.dv:(h + 1) * dm.dv]
        parts.append(oh * lax.rsqrt(jnp.mean(oh * oh, axis=-1, keepdims=True) + EPS))
    yb = jnp.concatenate(parts, axis=1) * gn_ref[...] * _silu(go)
    m = (jax.nn.sigmoid(ga) * _dot(ya.astype(BF16), wa_ref[...])
         + jax.nn.sigmoid(gb) * _dot(yb.astype(BF16), wb_ref[...]))
    x1 = x + g1 * _dot(m.astype(BF16), wo_ref[...])
    h2 = _rms(x1, n2_ref[...]) * (1 + sc2) + sh2
    hi = h2.astype(BF16)
    lo = (h2 - hi.astype(F32)).astype(BF16)
    logits = _dot(hi, wrh_ref[...]) + (_dot(hi, wrl_ref[...]) + _dot(lo, wrh_ref[...]))
    return x1, _pack_rows(h2), logits


def _in_project(dm, x, mod, n1_ref, win_refs, z_ref):
    h = (_rms(x, n1_ref[...]) * (1 + mod[:, dm.d:2 * dm.d]) + mod[:, 0:dm.d]).astype(BF16)
    starts = [0]
    for w_ref in win_refs:
        starts.append(starts[-1] + w_ref.shape[1])
    for i in (0, 2, 1):
        z_ref[:, starts[i]:starts[i + 1]] = _dot(h, win_refs[i][...])


def _mixp_kernel(x_ref, mod_ref, n1_ref, wia_ref, wig_ref, wil_ref, wal_ref, bal_ref, pw_ref, ps_ref, gn_ref,
                 wa_ref, wb_ref, wo_ref, n2_ref, wrh_ref, wrl_ref,
                 x1_ref, h2_ref, lg_ref, np_ref, ng_ref,
                 z_ref, uext_ref, s_ref, b_ref, oi_ref, *, dm, tt, n_t):
    j = pl.program_id(1)

    @pl.when(j == 0)
    def _():
        uext_ref[0:POOL_HIST, :] = jnp.zeros((POOL_HIST, dm.pw), F32)
        s_ref[...] = jnp.zeros(s_ref.shape, F32)

    scale = dm.dk ** -0.5
    lane = lax.broadcasted_iota(I32, (1, dm.hk), 1)
    head_masks = [(lane >= h * dm.dk) & (lane < (h + 1) * dm.dk) for h in range(dm.h)]
    causal = lax.broadcasted_iota(I32, (tt, tt), 0) >= lax.broadcasted_iota(I32, (tt, tt), 1)

    def pool_branch():
        pos = j * tt + lax.broadcasted_iota(I32, (tt, 1), 0)
        gw = dm.pw // len(POOL_WINDOWS)
        mixed = []
        for g, w in enumerate(POOL_WINDOWS):
            c0 = g * gw
            cur = uext_ref[POOL_HIST:POOL_HIST + tt, c0:c0 + gw]
            acc = cur
            for i in range(1, w):
                acc = acc + uext_ref[POOL_HIST - i:POOL_HIST - i + tt, c0:c0 + gw]
            cnt = jnp.minimum(w, pos + 1).astype(F32)
            mixed.append(acc / cnt - cur)
        return _pool_project(mixed, pw_ref, ps_ref)

    def scaled_queries(bc):
        return (z_ref[:, dm.q0:dm.q0 + dm.hk] * scale) * jnp.exp(bc)

    def inter_chunk(qe):
        s_bf = s_ref[...].astype(BF16)
        return jnp.concatenate(
            [_dot(jnp.where(head_masks[h], qe, 0.0).astype(BF16), s_bf) for h in range(dm.h)], axis=1)

    def finish(ya, o):
        x1, h2, logits = _post_mix(
            dm, x_ref[...], mod_ref[...], ya, o, z_ref[:, dm.go0:dm.go0 + dm.gw],
            z_ref[:, dm.ga0:dm.ga0 + dm.d], z_ref[:, dm.gb0:dm.gb0 + dm.d],
            gn_ref, wa_ref, wb_ref, wo_ref, n2_ref, wrh_ref, wrl_ref)
        x1_ref[...] = x1
        h2_ref[...] = h2
        lg_ref[...] = logits

    _in_project(dm, x_ref[...], mod_ref[...], n1_ref, (wia_ref, wig_ref, wil_ref), z_ref)
    uext_ref[POOL_HIST:POOL_HIST + tt, :] = z_ref[:, 0:dm.pw]
    ya = pool_branch()
    la = _log_decay(z_ref[:, dm.al0:dm.al0 + LANE], wal_ref, bal_ref)
    tri = jnp.where(causal, 1.0, 0.0).astype(BF16)
    a0, a1, a2 = _split3(la)
    bc = _dot(tri, a0) + (_dot(tri, a1) + _dot(tri, a2))
    b_ref[...] = bc
    qe = scaled_queries(bc)
    k = z_ref[:, dm.k0:dm.k0 + dm.hk]
    vb = z_ref[:, dm.v0:dm.v0 + dm.gw].astype(BF16)
    blast = bc[tt - 1:tt, :]
    fast = jnp.max(-blast) <= GLA_FAST_MAX_DECAY
    ke = (k * jnp.exp(-bc)).astype(BF16)
    intra = []
    for h in range(dm.h):
        qh = jnp.where(head_masks[h], qe, 0.0).astype(BF16)
        sc = lax.dot_general(qh, ke, (((1,), (1,)), ((), ())), preferred_element_type=F32)
        sc = jnp.where(causal, sc, 0.0).astype(BF16)
        intra.append(_dot(sc, vb[:, h * dm.dv:(h + 1) * dm.dv]))
    finish(ya, inter_chunk(qe) + jnp.concatenate(intra, axis=1))

    @pl.when(jnp.logical_not(fast))
    def _():
        rows = lax.broadcasted_iota(I32, (tt, 1), 0)
        for h in range(dm.h):
            def body(tb, carry, h=h):
                base = pl.multiple_of(tb * 8, 8)
                q8 = z_ref[pl.ds(base, 8), dm.q0:dm.q0 + dm.hk] * scale
                b8 = b_ref[pl.ds(base, 8), :]
                vh = z_ref[:, dm.v0 + h * dm.dv:dm.v0 + (h + 1) * dm.dv]
                out_rows = []
                for r in range(8):
                    keep = (rows <= base + r) & head_masks[h]
                    dec = jnp.exp(jnp.where(keep, b8[r:r + 1, :] - b_ref[...], -jnp.inf))
                    wgt = (q8[r:r + 1, :] * dec) * z_ref[:, dm.k0:dm.k0 + dm.hk]
                    s_col = jnp.sum(wgt, axis=-1, keepdims=True)
                    out_rows.append(jnp.sum(s_col * vh, axis=0, keepdims=True))
                oi_ref[pl.ds(base, 8), h * dm.dv:(h + 1) * dm.dv] = jnp.concatenate(out_rows, axis=0)
                return carry
            lax.fori_loop(0, tt // 8, body, 0)
        finish(pool_branch(), inter_chunk(scaled_queries(b_ref[...])) + oi_ref[...])

    bc = b_ref[...]
    blast = bc[tt - 1:tt, :]
    kd = z_ref[:, dm.k0:dm.k0 + dm.hk] * jnp.exp(blast - bc)
    u_all = _dot(kd.T.astype(BF16), z_ref[:, dm.v0:dm.v0 + dm.gw].astype(BF16))
    dec_col = jnp.broadcast_to(jnp.exp(blast), (dm.hk, dm.hk)).T[:, 0:dm.dv]
    upd = jnp.concatenate(
        [u_all[h * dm.dk:(h + 1) * dm.dk, h * dm.dv:(h + 1) * dm.dv] for h in range(dm.h)], axis=0)
    s_ref[...] = dec_col * s_ref[...] + upd
    uext_ref[0:POOL_HIST, :] = uext_ref[tt:tt + POOL_HIST, :]

    @pl.when(j == n_t - 1)
    def _():
        np_ref[...] = uext_ref[tt + 1:tt + POOL_HIST, :]
        ng_ref[...] = s_ref[...]


def _const_spec(shape):
    nd = len(shape)
    return pl.BlockSpec(shape, lambda *_: (0,) * nd, pipeline_mode=pl.Buffered(1))


def _mix_prompt(dm, x2d, mod3, wts, bsz, seq, n_exp):
    tt = 256 if seq % 256 == 0 else seq
    n_t = seq // tt
    n = bsz * seq
    d = dm.d
    kern = functools.partial(_mixp_kernel, dm=dm, tt=tt, n_t=n_t)
    row = lambda b, j: (b * n_t + j, 0)
    in_specs = [pl.BlockSpec((tt, d), row),
                pl.BlockSpec((None, 1, N_MOD * d), lambda b, j: (b, 0, 0))]
    in_specs += [_const_spec(w.shape) for w in wts]
    out_specs = [pl.BlockSpec((tt, d), row), pl.BlockSpec((tt, d // 2), row), pl.BlockSpec((tt, n_exp), row),
                 pl.BlockSpec((None, POOL_HIST - 1, dm.pw), lambda b, j: (b, 0, 0)),
                 pl.BlockSpec((None, dm.hk, dm.dv), lambda b, j: (b, 0, 0))]
    out_shape = [jax.ShapeDtypeStruct((n, d), F32), jax.ShapeDtypeStruct((n, d // 2), I32),
                 jax.ShapeDtypeStruct((n, n_exp), F32),
                 jax.ShapeDtypeStruct((bsz, POOL_HIST - 1, dm.pw), F32),
                 jax.ShapeDtypeStruct((bsz, dm.hk, dm.dv), F32)]
    scratch = [pltpu.VMEM((tt, dm.zw), F32), pltpu.VMEM((tt + POOL_HIST, dm.pw), F32),
               pltpu.VMEM((dm.hk, dm.dv), F32), pltpu.VMEM((tt, dm.hk), F32), pltpu.VMEM((tt, dm.gw), F32)]
    return pl.pallas_call(
        kern, grid=(bsz, n_t), in_specs=in_specs, out_specs=out_specs, out_shape=out_shape,
        scratch_shapes=scratch, compiler_params=_cp(("arbitrary", "arbitrary")), name="mixp",
    )(x2d, mod3, *wts)


def _sin_kernel(x_ref, mod_ref, n1_ref, wia_ref, wig_ref, wil_ref, z_ref, *, dm):
    _in_project(dm, x_ref[...], mod_ref[...], n1_ref, (wia_ref, wig_ref, wil_ref), z_ref)


def _sample_in(dm, xs2d, mod_s, n1, win, bs, t_s):
    d = dm.d
    return pl.pallas_call(
        functools.partial(_sin_kernel, dm=dm),
        grid=(t_s,),
        in_specs=[pl.BlockSpec((bs, d), lambda t: (0, t)), _const_spec(mod_s.shape), _const_spec(n1.shape)]
        + [_const_spec(w.shape) for w in win],
        out_specs=pl.BlockSpec((bs, dm.zw), lambda t: (t, 0)),
        out_shape=jax.ShapeDtypeStruct((t_s * bs, dm.zw), F32),
        compiler_params=_cp(("arbitrary",)), name="sample_in",
    )(xs2d, mod_s, n1, *win)


def _sst_kernel(z_ref, pool_ref, gs_ref, wal_ref, bal_ref,
                mixed_ref, o_ref, npool_ref, ngs_ref,
                qt_ref, kt_ref, dt_ref, vt_ref, ot_ref, *, dm, bs, t_s, ch, pos0):
    i = pl.program_id(0)
    n_i = pl.num_programs(0)
    scale = dm.dk ** -0.5
    past = POOL_HIST - 1

    @pl.when(i == 0)
    def _():
        for t in range(t_s):
            r0 = t * bs
            la = _log_decay(z_ref[r0:r0 + bs, dm.al0:dm.al0 + LANE], wal_ref, bal_ref)
            dt_ref[t] = jnp.exp(la).T
            qt_ref[t] = (z_ref[r0:r0 + bs, dm.q0:dm.q0 + dm.hk] * scale).T
            kt_ref[t] = z_ref[r0:r0 + bs, dm.k0:dm.k0 + dm.hk].T
            vt_ref[t] = z_ref[r0:r0 + bs, dm.v0:dm.v0 + dm.gw].T
        ot_ref[...] = jnp.zeros(ot_ref.shape, F32)

        def item(s, c0, c1):
            if s >= 0:
                return z_ref[s * bs:(s + 1) * bs, c0:c1]
            return pool_ref[:, (past + s) * dm.pw + c0:(past + s) * dm.pw + c1]

        gw = dm.pw // len(POOL_WINDOWS)
        for t in range(t_s):
            for g, w in enumerate(POOL_WINDOWS):
                c0, c1 = g * gw, (g + 1) * gw
                cur = item(t, c0, c1)
                acc = cur
                for r in range(1, w):
                    acc = acc + item(t - r, c0, c1)
                cnt = float(min(w, pos0 + t + 1))
                mixed_ref[t * bs:(t + 1) * bs, c0:c1] = acc / cnt - cur
        for r in range(past):
            npool_ref[:, r * dm.pw:(r + 1) * dm.pw] = item(r - past + t_s, 0, dm.pw)

    per_head = dm.dk // ch
    hoff = pl.multiple_of((i // per_head) * dm.dv, dm.dv)
    for p in range(ch):
        hd = i * ch + p
        st = gs_ref[:, p, :].T
        for t in range(t_s):
            st = dt_ref[t, pl.ds(hd, 1), :] * st + kt_ref[t, pl.ds(hd, 1), :] * vt_ref[t, pl.ds(hoff, dm.dv), :]
            ot_ref[t, pl.ds(hoff, dm.dv), :] += qt_ref[t, pl.ds(hd, 1), :] * st
        ngs_ref[:, p, :] = st.T

    @pl.when(i == n_i - 1)
    def _():
        for t in range(t_s):
            for h in range(dm.h):
                o_ref[t * bs:(t + 1) * bs, h * dm.dv:(h + 1) * dm.dv] = ot_ref[t, h * dm.dv:(h + 1) * dm.dv, :].T


def _sample_state(dm, z_s, pool2d, gs3, wal, bal, bs, t_s):
    ch = 16
    n_i = dm.hk // ch
    past = POOL_HIST - 1
    kern = functools.partial(_sst_kernel, dm=dm, bs=bs, t_s=t_s, ch=ch, pos0=PAST_LEN)
    full = lambda shape: pl.BlockSpec(shape, lambda i: (0,) * len(shape))
    return pl.pallas_call(
        kern, grid=(n_i,),
        in_specs=[_const_spec(z_s.shape), _const_spec(pool2d.shape),
                  pl.BlockSpec((bs, ch, dm.dv), lambda i: (0, i, 0)),
                  _const_spec(wal.shape), _const_spec(bal.shape)],
        out_specs=[full((t_s * bs, dm.pw)), full((t_s * bs, dm.gw)), full((bs, past * dm.pw)),
                   pl.BlockSpec((bs, ch, dm.dv), lambda i: (0, i, 0))],
        out_shape=[jax.ShapeDtypeStruct((t_s * bs, dm.pw), F32), jax.ShapeDtypeStruct((t_s * bs, dm.gw), F32),
                   jax.ShapeDtypeStruct((bs, past * dm.pw), F32), jax.ShapeDtypeStruct(gs3.shape, F32)],
        scratch_shapes=[pltpu.VMEM((t_s, dm.hk, bs), F32), pltpu.VMEM((t_s, dm.hk, bs), F32),
                        pltpu.VMEM((t_s, dm.hk, bs), F32), pltpu.VMEM((t_s, dm.gw, bs), F32),
                        pltpu.VMEM((t_s, dm.gw, bs), F32)],
        compiler_params=_cp(("arbitrary",)), name="sample_state",
    )(z_s, pool2d, gs3, wal, bal)


def _spost_kernel(x_ref, mod_ref, z_ref, mixed_ref, o_ref, pw_ref, ps_ref, gn_ref,
                  wa_ref, wb_ref, wo_ref, n2_ref, wrh_ref, wrl_ref,
                  x1_ref, h2_ref, lg_ref, *, dm):
    gw = dm.pw // len(POOL_WINDOWS)
    ya = _pool_project([mixed_ref[:, g * gw:(g + 1) * gw] for g in range(len(POOL_WINDOWS))], pw_ref, ps_ref)
    x1, h2, logits = _post_mix(
        dm, x_ref[...], mod_ref[...], ya, o_ref[...],
        z_ref[:, dm.go0:dm.go0 + dm.gw], z_ref[:, dm.ga0:dm.ga0 + dm.d], z_ref[:, dm.gb0:dm.gb0 + dm.d],
        gn_ref, wa_ref, wb_ref, wo_ref, n2_ref, wrh_ref, wrl_ref)
    x1_ref[...] = x1
    h2_ref[...] = h2
    lg_ref[...] = logits


def _sample_post(dm, xs2d, mod_s, z_s, mixed_s, o_s, wts, bs, t_s, n_exp):
    d = dm.d
    rows = lambda t: (t, 0)
    in_specs = [pl.BlockSpec((bs, d), lambda t: (0, t)), _const_spec(mod_s.shape),
                pl.BlockSpec((bs, dm.zw), rows), pl.BlockSpec((bs, dm.pw), rows), pl.BlockSpec((bs, dm.gw), rows)]
    in_specs += [_const_spec(w.shape) for w in wts]
    return pl.pallas_call(
        functools.partial(_spost_kernel, dm=dm), grid=(t_s,), in_specs=in_specs,
        out_specs=[pl.BlockSpec((bs, d), rows), pl.BlockSpec((bs, d // 2), rows), pl.BlockSpec((bs, n_exp), rows)],
        out_shape=[jax.ShapeDtypeStruct((t_s * bs, d), F32), jax.ShapeDtypeStruct((t_s * bs, d // 2), I32),
                   jax.ShapeDtypeStruct((t_s * bs, n_exp), F32)],
        compiler_params=_cp(("arbitrary",)), name="sample_post",
    )(xs2d, mod_s, z_s, mixed_s, o_s, *wts)


def _route_kernel(lgp_ref, lgs_ref, bias_ref, idx_ref, w_ref, cnt_ref, *, n_exp, tiles_p):
    i = pl.program_id(0)
    s = jax.nn.sigmoid(jnp.where(i < tiles_p, lgp_ref[...], lgs_ref[...]))
    cur = s + bias_ref[...]
    tr = s.shape[0]
    lane = lax.broadcasted_iota(I32, (1, n_exp), 1).astype(F32)
    slot = lax.broadcasted_iota(I32, (1, LANE), 1)
    idx8 = jnp.zeros((tr, LANE), F32)
    w8 = jnp.zeros((tr, LANE), F32)
    wsum = jnp.zeros((tr, 1), F32)
    picked = jnp.zeros((tr, n_exp), F32)
    for k in range(TOP_K):
        m = jnp.max(cur, axis=-1, keepdims=True)
        am = jnp.min(jnp.where(cur == m, lane, float(n_exp)), axis=-1, keepdims=True)
        oh = lane == am
        sk = jnp.sum(jnp.where(oh, s, 0.0), axis=-1, keepdims=True)
        cur = jnp.where(oh, -jnp.inf, cur)
        idx8 = jnp.where(slot == k, am, idx8)
        w8 = jnp.where(slot == k, sk, w8)
        wsum = wsum + sk
        picked = picked + jnp.where(oh, 1.0, 0.0)
    idx_ref[...] = idx8.astype(I32)
    w_ref[...] = w8 / wsum * ROUTED_SCALE

    @pl.when(i == 0)
    def _():
        cnt_ref[...] = jnp.zeros(cnt_ref.shape, F32)

    cnt_ref[...] += jnp.sum(picked, axis=0, keepdims=True)


def _two_source_specs(block, tiles_p, tiles_s):
    return [pl.BlockSpec(block, lambda i: (jnp.minimum(i, tiles_p - 1), 0)),
            pl.BlockSpec(block, lambda i: (jnp.clip(i - tiles_p, 0, tiles_s - 1), 0))]


def _route(lg_p, lg_s, bias, tr):
    n_exp = lg_p.shape[1]
    tiles_p, tiles_s = lg_p.shape[0] // tr, lg_s.shape[0] // tr
    n = (tiles_p + tiles_s) * tr
    rows = lambda i: (i, 0)
    return pl.pallas_call(
        functools.partial(_route_kernel, n_exp=n_exp, tiles_p=tiles_p), grid=(tiles_p + tiles_s,),
        in_specs=_two_source_specs((tr, n_exp), tiles_p, tiles_s) + [_const_spec((1, n_exp))],
        out_specs=[pl.BlockSpec((tr, LANE), rows), pl.BlockSpec((tr, LANE), rows),
                   pl.BlockSpec((8, n_exp), lambda i: (0, 0))],
        out_shape=[jax.ShapeDtypeStruct((n, LANE), I32), jax.ShapeDtypeStruct((n, LANE), F32),
                   jax.ShapeDtypeStruct((8, n_exp), F32)],
        compiler_params=_cp(("arbitrary",)), name="route",
    )(lg_p, lg_s, bias.reshape(1, n_exp))


def _plan_kernel(idx_ref, cnt_ref, pos_ref, bounds_ref, start_ref, run_ref, *, n_exp):
    i = pl.program_id(0)
    lane = lax.broadcasted_iota(I32, (1, n_exp), 1)

    @pl.when(i == 0)
    def _():
        cnt = cnt_ref[...].astype(I32)
        padded = ((cnt + (EXPERT_BLOCK - 1)) // EXPERT_BLOCK) * EXPERT_BLOCK
        pe = padded
        s = 1
        while s < n_exp:
            pe = pe + jnp.where(lane >= s, pltpu.roll(pe, s, 1), 0)
            s *= 2
        start_ref[...] = (pe - padded).astype(F32)
        run_ref[...] = jnp.zeros(run_ref.shape, F32)
        row = lax.broadcasted_iota(I32, (8, n_exp), 0)
        bounds_ref[...] = jnp.where(row == 0, pe - padded, pe)

    idx = idx_ref[...]
    tr = idx.shape[0]
    onehots = [lane == idx[:, k:k + 1] for k in range(TOP_K)]
    member = onehots[0]
    for k in range(1, TOP_K):
        member = member | onehots[k]
    mb = jnp.where(member, 1.0, 0.0).astype(BF16)
    r_i = lax.broadcasted_iota(I32, (tr, tr), 0)
    c_i = lax.broadcasted_iota(I32, (tr, tr), 1)
    earlier = jnp.where(r_i > c_i, 1.0, 0.0).astype(BF16)
    dest = _dot(earlier, mb) + (start_ref[0:1, :] + run_ref[0:1, :])
    slot = lax.broadcasted_iota(I32, (1, LANE), 1)
    pos8 = jnp.zeros((tr, LANE), F32)
    for k in range(TOP_K):
        pk = jnp.sum(jnp.where(onehots[k], dest, 0.0), axis=-1, keepdims=True)
        pos8 = jnp.where(slot == k, pk, pos8)
    pos_ref[...] = pos8.T[0:TOP_K, :].astype(I32)
    run_ref[...] += jnp.sum(mb.astype(F32), axis=0, keepdims=True)


def _plan(idx8, counts, tr):
    n = idx8.shape[0]
    n_exp = counts.shape[1]
    rows = lambda i: (i, 0)
    return pl.pallas_call(
        functools.partial(_plan_kernel, n_exp=n_exp), grid=(n // tr,),
        in_specs=[pl.BlockSpec((tr, LANE), rows), _const_spec((8, n_exp))],
        out_specs=[pl.BlockSpec((TOP_K, tr), lambda i: (0, i)), pl.BlockSpec((8, n_exp), lambda i: (0, 0))],
        out_shape=[jax.ShapeDtypeStruct((TOP_K, n), I32), jax.ShapeDtypeStruct((8, n_exp), I32)],
        scratch_shapes=[pltpu.VMEM((8, n_exp), F32), pltpu.VMEM((8, n_exp), F32)],
        compiler_params=_cp(("arbitrary",)), name="plan",
    )(idx8, counts)


SC_CORES = 2
SC_SUBCORES = 16
SC_AXES = ("core", "subcore")
SC_WORKERS = SC_CORES * SC_SUBCORES
SC_WINDOW = 128


def _sc_mesh():
    return plsc.VectorSubcoreMesh(core_axis_name=SC_AXES[0], subcore_axis_name=SC_AXES[1],
                                  num_cores=SC_CORES, num_subcores=SC_SUBCORES)


def _sc_worker():
    return lax.axis_index(SC_AXES[1]) * SC_CORES + lax.axis_index(SC_AXES[0])


def _dispatch(h_p, h_s, pos_t, n_rows):
    n_p, c = h_p.shape
    n_s = h_s.shape[0]
    win_p, win_s = n_p // SC_WINDOW, n_s // SC_WINDOW
    assert n_p % (SC_WINDOW * SC_WORKERS) == 0 and n_s % SC_WINDOW == 0 and win_s <= SC_WORKERS

    @functools.partial(
        pl.kernel, out_type=jax.ShapeDtypeStruct((n_rows, c), h_p.dtype), mesh=_sc_mesh(), name="dispatch",
        scratch_types=[pltpu.VMEM((TOP_K, SC_WINDOW), I32), pltpu.VMEM((SC_WINDOW, c), h_p.dtype),
                       pltpu.SemaphoreType.DMA])
    def run(hp_hbm, hs_hbm, pos_hbm, xs_hbm, idx_v, rows_v, sem):
        wid = _sc_worker()

        def scatter_window(h_hbm, row0, tok0):
            pltpu.sync_copy(h_hbm.at[pl.ds(row0, SC_WINDOW)], rows_v)
            pltpu.sync_copy(pos_hbm.at[:, pl.ds(tok0, SC_WINDOW)], idx_v)
            copies = [pltpu.async_copy(rows_v, xs_hbm.at[idx_v.at[k]], sem) for k in range(TOP_K)]
            for cp in copies:
                cp.wait()

        def prompt_window(j, carry):
            row0 = pl.multiple_of((wid * (win_p // SC_WORKERS) + j) * SC_WINDOW, SC_WINDOW)
            scatter_window(hp_hbm, row0, row0)
            return carry
        lax.fori_loop(0, win_p // SC_WORKERS, prompt_window, 0)

        @pl.when(wid < win_s)
        def _():
            row0 = pl.multiple_of(wid * SC_WINDOW, SC_WINDOW)
            scatter_window(hs_hbm, row0, n_p + row0)

    return run(h_p, h_s, pos_t)


def _combine_gather(ys, pos_t):
    k_n = pos_t.shape[0] * pos_t.shape[1]
    c = ys.shape[1]
    win = SC_WINDOW // 2
    per_worker = k_n // (win * SC_WORKERS)
    assert k_n % (win * SC_WORKERS) == 0 and per_worker % 2 == 0
    dma = pltpu.SemaphoreType.DMA

    @functools.partial(
        pl.kernel, out_type=jax.ShapeDtypeStruct((k_n, c), ys.dtype), mesh=_sc_mesh(), name="combine_gather",
        scratch_types=[pltpu.VMEM((win,), I32), pltpu.VMEM((win,), I32),
                       pltpu.VMEM((win, c), ys.dtype), pltpu.VMEM((win, c), ys.dtype), dma, dma, dma, dma])
    def run(ys_hbm, idx_hbm, out_hbm, idx0, idx1, buf0, buf1, gsem0, gsem1, ssem0, ssem1):
        wid = _sc_worker()

        def rows(j):
            return pl.ds(pl.multiple_of((wid * per_worker + j) * win, win), win)

        def gather(j, idx_v, buf, sem):
            pltpu.sync_copy(idx_hbm.at[rows(j)], idx_v)
            pltpu.async_copy(ys_hbm.at[idx_v], buf, sem)

        def gather_wait(idx_v, buf, sem):
            pltpu.make_async_copy(ys_hbm.at[idx_v], buf, sem).wait()

        def store(j, buf, sem):
            pltpu.async_copy(buf, out_hbm.at[rows(j)], sem)

        def store_wait(buf, sem):
            pltpu.make_async_copy(buf, out_hbm.at[rows(0)], sem).wait()

        gather(0, idx0, buf0, gsem0)

        def two_windows(i, carry):
            j = 2 * i

            @pl.when(i > 0)
            def _():
                store_wait(buf1, ssem1)

            gather(j + 1, idx1, buf1, gsem1)
            gather_wait(idx0, buf0, gsem0)
            store(j, buf0, ssem0)

            @pl.when(j + 2 < per_worker)
            def _():
                store_wait(buf0, ssem0)
                gather(j + 2, idx0, buf0, gsem0)

            gather_wait(idx1, buf1, gsem1)
            store(j + 1, buf1, ssem1)
            return carry
        lax.fori_loop(0, per_worker // 2, two_windows, 0)
        store_wait(buf0, ssem0)
        store_wait(buf1, ssem1)

    return run(ys, pos_t.reshape(k_n))


MOE_RING = 8


MOE_WEIGHT_RING = 4


def _moe_kernel(st_ref, wg_ref, wu_ref, wd_ref, xs_ref, ys_ref, xbuf, ybuf, wgf, wuf, wdf, wgb, wub, wdb,
                semx, semy, semw):
    e = pl.program_id(0)
    n_e = pl.num_programs(0)

    def w_copies(ex):
        slot = ex % MOE_WEIGHT_RING
        return [pltpu.make_async_copy(src.at[ex], dst.at[slot], semw.at[slot, i])
                for i, (src, dst) in enumerate(((wg_ref, wgf), (wu_ref, wuf), (wd_ref, wdf)))]

    @pl.when(e == 0)
    def _():
        for ex in range(MOE_WEIGHT_RING - 1):
            @pl.when(ex < n_e)
            def _(ex=ex):
                for cp in w_copies(ex):
                    cp.start()

    @pl.when(e + (MOE_WEIGHT_RING - 1) < n_e)
    def _():
        for cp in w_copies(e + (MOE_WEIGHT_RING - 1)):
            cp.start()
    g0 = st_ref[e] // EXPERT_BLOCK
    nchunks = st_ref[e + 1] // EXPERT_BLOCK - g0
    ntot = st_ref[n_e] // EXPERT_BLOCK

    def rows(g):
        return pl.ds(pl.multiple_of(g * EXPERT_BLOCK, EXPERT_BLOCK), EXPERT_BLOCK)

    def x_copy(g):
        slot = g % MOE_RING
        return pltpu.make_async_copy(xs_ref.at[rows(g)], xbuf.at[slot], semx.at[slot])

    def y_copy(g):
        slot = g % MOE_RING
        return pltpu.make_async_copy(ybuf.at[slot], ys_ref.at[rows(g)], semy.at[slot])

    @pl.when(e == 0)
    def _():
        for g in range(MOE_RING - 1):
            @pl.when(g < ntot)
            def _(g=g):
                x_copy(g).start()

    for cp in w_copies(e):
        cp.wait()
    wslot = e % MOE_WEIGHT_RING
    wgb[...] = wgf[wslot].astype(BF16)
    wub[...] = wuf[wslot].astype(BF16)
    wdb[...] = wdf[wslot].astype(BF16)

    def take(g):
        @pl.when(g + (MOE_RING - 1) < ntot)
        def _():
            x_copy(g + (MOE_RING - 1)).start()

        x_copy(g).wait()

        @pl.when(g >= MOE_RING)
        def _():
            y_copy(g - MOE_RING).wait()

        return xbuf[g % MOE_RING]

    def expert_mlp(x_words):
        x = _unpack_rows(x_words).astype(BF16)
        mid = _silu(_dot(x, wgb[...])) * _dot(x, wub[...])
        return _pack_rows(_dot(mid.astype(BF16), wdb[...]))

    def put(g, y_words):
        ybuf[g % MOE_RING] = y_words
        y_copy(g).start()

    def run_chunks(g, m):
        xs = [take(g + i) for i in range(m)]
        y = expert_mlp(xs[0] if m == 1 else jnp.concatenate(xs, axis=0))
        for i in range(m):
            put(g + i, y[i * EXPERT_BLOCK:(i + 1) * EXPERT_BLOCK])

    def quad(p, carry):
        run_chunks(g0 + 4 * p, 4)
        return carry

    lax.fori_loop(0, nchunks // 4, quad, 0)
    rest = nchunks % 4
    g_rest = g0 + nchunks - rest

    @pl.when(rest >= 2)
    def _():
        run_chunks(g_rest, 2)

    @pl.when(rest % 2 == 1)
    def _():
        run_chunks(g0 + nchunks - 1, 1)

    @pl.when(e == n_e - 1)
    def _():
        for j in range(MOE_RING):
            g = ntot - MOE_RING + j

            @pl.when(g >= 0)
            def _(g=g):
                y_copy(g).wait()


def _moe(starts, xs, wg, wu, wd):
    n_rows, c = xs.shape
    n_exp, d, f = wg.shape
    grid_spec = pltpu.PrefetchScalarGridSpec(
        num_scalar_prefetch=1, grid=(n_exp,),
        in_specs=[pl.BlockSpec(memory_space=pl.ANY)] * 4,
        out_specs=pl.BlockSpec(memory_space=pl.ANY),
        scratch_shapes=[pltpu.VMEM((MOE_RING, EXPERT_BLOCK, c), I32), pltpu.VMEM((MOE_RING, EXPERT_BLOCK, c), I32),
                        pltpu.VMEM((MOE_WEIGHT_RING, d, f), F32), pltpu.VMEM((MOE_WEIGHT_RING, d, f), F32),
                        pltpu.VMEM((MOE_WEIGHT_RING, f, d), F32),
                        pltpu.VMEM((d, f), BF16), pltpu.VMEM((d, f), BF16), pltpu.VMEM((f, d), BF16),
                        pltpu.SemaphoreType.DMA((MOE_RING,)), pltpu.SemaphoreType.DMA((MOE_RING,)),
                        pltpu.SemaphoreType.DMA((MOE_WEIGHT_RING, 3))])
    return pl.pallas_call(
        _moe_kernel, grid_spec=grid_spec, out_shape=jax.ShapeDtypeStruct((n_rows, c), I32),
        compiler_params=_cp(("arbitrary",)), name="moe",
    )(starts, wg, wu, wd, xs)


def _final_kernel(yg_ref, w8_ref, x1_ref, h2_ref, mod_ref, wsg_ref, wsu_ref, wsd_ref, nf_ref, y_ref, *, d):
    hb = _unpack_rows(h2_ref[...]).astype(BF16)
    shared = _dot((_silu(_dot(hb, wsg_ref[...])) * _dot(hb, wsu_ref[...])).astype(BF16), wsd_ref[...])
    w8 = w8_ref[...]
    routed = _unpack_rows(yg_ref[0]) * w8[:, 0:1]
    for k in range(1, TOP_K):
        routed = routed + _unpack_rows(yg_ref[k]) * w8[:, k:k + 1]
    g2 = mod_ref[...][:, (N_MOD - 1) * d:N_MOD * d]
    x2 = x1_ref[...] + g2 * (routed + shared)
    y_ref[...] = _rms(x2, nf_ref[...])


def _final(yg3, w8, x1, h2, mod, mod_spec, wts, nf, tf, n_tiles, tile0, out_shape, out_spec):
    d = x1.shape[1]
    rows = lambda i: (i, 0)
    in_specs = [pl.BlockSpec((TOP_K, tf, d // 2), lambda i: (0, tile0 + i, 0)),
                pl.BlockSpec((tf, LANE), lambda i: (tile0 + i, 0)),
                pl.BlockSpec((tf, d), rows), pl.BlockSpec((tf, d // 2), rows), mod_spec]
    in_specs += [_const_spec(w.shape) for w in wts]
    in_specs += [_const_spec(nf.shape)]
    return pl.pallas_call(
        functools.partial(_final_kernel, d=d), grid=(n_tiles,), in_specs=in_specs,
        out_specs=out_spec, out_shape=out_shape,
        compiler_params=_cp(("arbitrary",)), name="final",
    )(yg3, w8, x1, h2, mod, *wts, nf)


def kernel(x_prompt, x_sample, c_prompt, c_sample, state_pool, state_gla, w_ada, b_ada, norm1, w_in,
           pool_w, pool_scale, gla_w_alpha, gla_b_alpha, gla_norm, w_branch_a, w_branch_b, w_out, norm2,
           w_router, router_bias, w_exp_gate, w_exp_up, w_exp_down, w_sh_gate, w_sh_up, w_sh_down, norm_f):
    depth = w_in.shape[0]
    assert depth == 1, "single-layer trunk"
    bp, seq, d = x_prompt.shape
    bs, t_s, _ = x_sample.shape
    _, _, past, pw = state_pool.shape
    _, _, heads, dk, dv = state_gla.shape
    rank = gla_w_alpha.shape[1]
    n_exp = w_router.shape[2]
    assert past == POOL_HIST - 1 and pw == len(POOL_WINDOWS) * LANE and dv == LANE and rank <= LANE
    dm = _Dims(d, pw, heads, dk, dv)
    n_p, n_s = bp * seq, bs * t_s
    n_all = n_p + n_s

    off = [0]
    for sz in (pw, dm.hk, dm.hk, dm.gw, dm.gw, rank, d, d):
        off.append(off[-1] + sz)
    wi = w_in[0]
    win = (wi[:, :off[5]].astype(BF16), wi[:, off[6]:].astype(BF16),
           jnp.concatenate([wi[:, off[5]:off[6]], jnp.zeros((d, LANE - rank), F32)], axis=1).astype(BF16))
    assert off[5] + 2 * d + LANE == dm.zw
    wal = jnp.concatenate([gla_w_alpha[0], jnp.zeros((LANE - rank, dm.hk), F32)], axis=0).astype(BF16)
    bal = gla_b_alpha[0].reshape(1, dm.hk)
    wr = w_router[0]
    wrh = wr.astype(BF16)
    wrl = (wr - wrh.astype(F32)).astype(BF16)
    post_w = (pool_w[0].astype(BF16), pool_scale[0].reshape(1, pw), gla_norm[0].reshape(1, dm.gw),
              w_branch_a[0].astype(BF16), w_branch_b[0].astype(BF16), w_out[0].astype(BF16),
              norm2[0].reshape(1, d), wrh, wrl)
    n1 = norm1[0].reshape(1, d)

    mod = _ada(jnp.concatenate([c_prompt, c_sample], axis=0), w_ada[0], b_ada[0])
    mod_p = mod[:bp].reshape(bp, 1, N_MOD * d)
    mod_s = mod[bp:]

    x1_p, h2_p, lg_p, npool_p, ngla_p = _mix_prompt(
        dm, x_prompt.reshape(n_p, d), mod_p, (n1,) + win + (wal, bal) + post_w, bp, seq, n_exp)

    xs2d = x_sample.reshape(bs, t_s * d)
    z_s = _sample_in(dm, xs2d, mod_s, n1, win, bs, t_s)
    mixed_s, o_s, npool_s, ngla_s = _sample_state(
        dm, z_s, state_pool[0].reshape(bs, past * pw), state_gla[0].reshape(bs, dm.hk, dv), wal, bal, bs, t_s)
    x1_s, h2_s, lg_s = _sample_post(dm, xs2d, mod_s, z_s, mixed_s, o_s, post_w, bs, t_s, n_exp)

    tr = 512 if (n_p % 512 == 0 and n_s % 512 == 0) else LANE
    assert n_p % tr == 0 and n_s % tr == 0 and n_p % bs == 0
    idx8, w8, counts = _route(lg_p, lg_s, router_bias[0], tr)
    nblk = (n_all * TOP_K) // EXPERT_BLOCK + n_exp
    pos_t, bounds = _plan(idx8, counts, tr)
    starts = jnp.concatenate([bounds[0], bounds[1, n_exp - 1:]])

    xs = _dispatch(h2_p, h2_s, pos_t, nblk * EXPERT_BLOCK)
    ys = _moe(starts, xs, w_exp_gate[0], w_exp_up[0], w_exp_down[0])
    yg3 = _combine_gather(ys, pos_t).reshape(TOP_K, n_all, d // 2)

    sh_w = (w_sh_gate[0].astype(BF16), w_sh_up[0].astype(BF16), w_sh_down[0].astype(BF16))
    nf = norm_f.reshape(1, d)
    tf = 256 if seq % 256 == 0 else seq
    n_tp = seq // tf
    y_p = _final(
        yg3, w8, x1_p, h2_p, mod_p,
        pl.BlockSpec((None, 1, N_MOD * d), lambda i: (i // n_tp, 0, 0)), sh_w, nf, tf, n_p // tf, 0,
        jax.ShapeDtypeStruct((n_p, d), F32), pl.BlockSpec((tf, d), lambda i: (i, 0)))
    y_s = _final(
        yg3, w8, x1_s, h2_s, mod_s,
        _const_spec(mod_s.shape), sh_w, nf, bs, t_s, n_p // bs,
        jax.ShapeDtypeStruct((bs, t_s * d), F32), pl.BlockSpec((bs, d), lambda i: (0, i)))

    return (y_p.reshape(bp, seq, d), y_s.reshape(bs, t_s, d),
            npool_p.reshape(depth, bp, past, pw), ngla_p.reshape(depth, bp, heads, dk, dv),
            npool_s.reshape(depth, bs, past, pw), ngla_s.reshape(depth, bs, heads, dk, dv))
```

```python
import functools

import jax
import jax.numpy as jnp
from jax import lax
from jax.experimental import pallas as pl
from jax.experimental.pallas import tpu as pltpu
from jax.experimental.pallas import tpu_sc as plsc

F32 = jnp.float32
BF16 = jnp.bfloat16
I32 = jnp.int32

EPS = 1e-6
N_MOD = 6
POOL_WINDOWS = (2, 4, 8, 16)
POOL_HIST = 16
GLA_GATE_TEMP = 16.0
TOP_K = 8
ROUTED_SCALE = 2.5
PAST_LEN = 16384
EXPERT_BLOCK = 128
LANE = 128
GLA_FAST_MAX_DECAY = 40.0
PROJ_PIECE = 512
VMEM_LIMIT = 56 * 1024 * 1024


def _cp(sem, vmem=VMEM_LIMIT):
    return pltpu.CompilerParams(dimension_semantics=sem, vmem_limit_bytes=vmem)


def _rms(x, g):
    return x * lax.rsqrt(jnp.mean(x * x, axis=-1, keepdims=True) + EPS) * g


def _silu(x):
    return x * jax.nn.sigmoid(x)


def _dot(a, b):
    return jnp.dot(a, b, preferred_element_type=F32)


def _split3(a):
    a0 = a.astype(BF16)
    r1 = a - a0.astype(F32)
    a1 = r1.astype(BF16)
    a2 = (r1 - a1.astype(F32)).astype(BF16)
    return a0, a1, a2


def _mod_parts(mod, d):
    return [mod[:, i * d:(i + 1) * d] for i in range(N_MOD)]


_HI16 = -65536


def _pack_rows(v):
    c = v.shape[1] // 2
    lo = lax.bitcast_convert_type(v[:, :c].astype(BF16).astype(F32), I32)
    hi = lax.bitcast_convert_type(v[:, c:].astype(BF16).astype(F32), I32)
    return (hi & _HI16) | lax.shift_right_logical(lo, 16)


def _unpack_rows(w):
    lo = lax.bitcast_convert_type(lax.shift_left(w, 16), F32)
    hi = lax.bitcast_convert_type(w & _HI16, F32)
    return jnp.concatenate([lo, hi], axis=1)


def _ada_kernel(c_ref, w_ref, b_ref, o_ref):
    a = _silu(c_ref[...]).astype(BF16)
    o_ref[...] = _dot(a, w_ref[...].astype(BF16)) + b_ref[...]


def _ada(c_all, w_ada, b_ada):
    n, d = c_all.shape
    cols = w_ada.shape[1]
    bc = d
    return pl.pallas_call(
        _ada_kernel,
        grid=(cols // bc,),
        in_specs=[pl.BlockSpec((n, d), lambda i: (0, 0)),
                  pl.BlockSpec((d, bc), lambda i: (0, i)),
                  pl.BlockSpec((1, bc), lambda i: (0, i))],
        out_specs=pl.BlockSpec((n, bc), lambda i: (0, i)),
        out_shape=jax.ShapeDtypeStruct((n, cols), F32),
        compiler_params=_cp(("arbitrary",)),
        name="ada",
    )(c_all, w_ada, b_ada.reshape(1, cols))


class _Dims:
    def __init__(self, d, pw, heads, dk, dv):
        self.d, self.pw, self.h, self.dk, self.dv = d, pw, heads, dk, dv
        self.hk, self.gw = heads * dk, heads * dv
        self.q0 = pw
        self.k0 = self.q0 + self.hk
        self.v0 = self.k0 + self.hk
        self.go0 = self.v0 + self.gw
        self.ga0 = self.go0 + self.gw
        self.gb0 = self.ga0 + d
        self.al0 = self.gb0 + d
        self.zw = self.al0 + LANE


def _log_decay(z_al, wal_ref, bal_ref):
    xal = _dot(z_al.astype(BF16), wal_ref[...]) + bal_ref[...]
    return jax.nn.log_sigmoid(xal) / GLA_GATE_TEMP


def _pool_project(mixed_groups, pw_ref, ps_ref):
    ys = [_dot(m.astype(BF16), pw_ref[g]) for g, m in enumerate(mixed_groups)]
    return jnp.concatenate(ys, axis=1) * ps_ref[...]


def _post_mix(dm, x, mod, ya, o, go, ga, gb, gn_ref, wa_ref, wb_ref, wo_ref, n2_ref, wrh_ref, wrl_ref,
              filler=lambda: None):
    _, _, g1, sh2, sc2, _ = _mod_parts(mod, dm.d)
    parts = []
    for h in range(dm.h):
        oh = o[:, h * dm.dv:(h + 1) * dm.dv]
        parts.append(oh * lax.rsqrt(jnp.mean(oh * oh, axis=-1, keepdims=True) + EPS))
    yb = jnp.concatenate(parts, axis=1) * gn_ref[...] * _silu(go)
    filler()
    m = (jax.nn.sigmoid(ga) * _dot(ya.astype(BF16), wa_ref[...])
         + jax.nn.sigmoid(gb) * _dot(yb.astype(BF16), wb_ref[...]))
    filler()
    x1 = x + g1 * _dot(m.astype(BF16), wo_ref[...])
    h2 = _rms(x1, n2_ref[...]) * (1 + sc2) + sh2
    hi = h2.astype(BF16)
    lo = (h2 - hi.astype(F32)).astype(BF16)
    filler()
    logits = _dot(hi, wrh_ref[...]) + (_dot(hi, wrl_ref[...]) + _dot(lo, wrh_ref[...]))
    filler()
    return x1, _pack_rows(h2), logits


def _in_project(dm, x, mod, n1_ref, win_refs, z_ref):
    h = (_rms(x, n1_ref[...]) * (1 + mod[:, dm.d:2 * dm.d]) + mod[:, 0:dm.d]).astype(BF16)
    starts = [0]
    for w_ref in win_refs:
        starts.append(starts[-1] + w_ref.shape[1])
    for i in (0, 2, 1):
        z_ref[:, starts[i]:starts[i + 1]] = _dot(h, win_refs[i][...])


def _mixp_kernel(x_ref, mod_ref, n1_ref, wia_ref, wig_ref, wil_ref, wal_ref, bal_ref, pw_ref, ps_ref, gn_ref,
                 wa_ref, wb_ref, wo_ref, n2_ref, wrh_ref, wrl_ref,
                 x1_ref, h2_ref, lg_ref, np_ref, ng_ref,
                 za_ref, zb_ref, xk_ref, mk_ref, hn_ref, uext_ref, s_ref, b_ref, oi_ref, *, dm, tt, n_t):
    n = pl.program_id(0)
    j = jnp.maximum(n - 1, 0) % n_t

    @pl.when(n == 0)
    def _():
        za_ref[...] = jnp.zeros(za_ref.shape, F32)
        zb_ref[...] = jnp.zeros(zb_ref.shape, F32)
        xk_ref[...] = jnp.zeros(xk_ref.shape, F32)
        mk_ref[...] = jnp.zeros(mk_ref.shape, F32)

    @pl.when(j == 0)
    def _():
        uext_ref[0:POOL_HIST, :] = jnp.zeros((POOL_HIST, dm.pw), F32)
        s_ref[...] = jnp.zeros(s_ref.shape, F32)

    @pl.when(n % 2 == 0)
    def _():
        _mixp_region(za_ref, zb_ref, x_ref, mod_ref, n1_ref, wia_ref, wig_ref, wil_ref, wal_ref, bal_ref, pw_ref,
                     ps_ref, gn_ref, wa_ref, wb_ref, wo_ref, n2_ref, wrh_ref, wrl_ref, x1_ref, h2_ref, lg_ref,
                     xk_ref, mk_ref, hn_ref, uext_ref, s_ref, b_ref, oi_ref, dm=dm, tt=tt, j=j)

    @pl.when(n % 2 == 1)
    def _():
        _mixp_region(zb_ref, za_ref, x_ref, mod_ref, n1_ref, wia_ref, wig_ref, wil_ref, wal_ref, bal_ref, pw_ref,
                     ps_ref, gn_ref, wa_ref, wb_ref, wo_ref, n2_ref, wrh_ref, wrl_ref, x1_ref, h2_ref, lg_ref,
                     xk_ref, mk_ref, hn_ref, uext_ref, s_ref, b_ref, oi_ref, dm=dm, tt=tt, j=j)

    xk_ref[...] = x_ref[...]
    mk_ref[...] = mod_ref[...]

    @pl.when(j == n_t - 1)
    def _():
        np_ref[...] = uext_ref[tt + 1:tt + POOL_HIST, :]
        ng_ref[...] = s_ref[...]


def _mixp_region(zp_ref, z_ref, x_ref, mod_ref, n1_ref, wia_ref, wig_ref, wil_ref, wal_ref, bal_ref, pw_ref,
                 ps_ref, gn_ref, wa_ref, wb_ref, wo_ref, n2_ref, wrh_ref, wrl_ref, x1_ref, h2_ref, lg_ref,
                 xk_ref, mk_ref, hn_ref, uext_ref, s_ref, b_ref, oi_ref, *, dm, tt, j):
    scale = dm.dk ** -0.5
    lane = lax.broadcasted_iota(I32, (1, dm.hk), 1)
    head_masks = [(lane >= h * dm.dk) & (lane < (h + 1) * dm.dk) for h in range(dm.h)]
    causal = lax.broadcasted_iota(I32, (tt, tt), 0) >= lax.broadcasted_iota(I32, (tt, tt), 1)

    def pool_branch():
        pos = j * tt + lax.broadcasted_iota(I32, (tt, 1), 0)
        gw = dm.pw // len(POOL_WINDOWS)
        mixed = []
        for g, w in enumerate(POOL_WINDOWS):
            c0 = g * gw
            cur = uext_ref[POOL_HIST:POOL_HIST + tt, c0:c0 + gw]
            acc = cur
            for i in range(1, w):
                acc = acc + uext_ref[POOL_HIST - i:POOL_HIST - i + tt, c0:c0 + gw]
            cnt = jnp.minimum(w, pos + 1).astype(F32)
            mixed.append(acc / cnt - cur)
        return _pool_project(mixed, pw_ref, ps_ref)

    def scaled_queries(bc):
        return (z_ref[:, dm.q0:dm.q0 + dm.hk] * scale) * jnp.exp(bc)

    def inter_chunk(qe):
        s_bf = s_ref[...].astype(BF16)
        return jnp.concatenate(
            [_dot(jnp.where(head_masks[h], qe, 0.0).astype(BF16), s_bf) for h in range(dm.h)], axis=1)

    def finish(ya, o, filler=lambda: None):
        x1, h2, logits = _post_mix(
            dm, xk_ref[...], mk_ref[...], ya, o, z_ref[:, dm.go0:dm.go0 + dm.gw],
            z_ref[:, dm.ga0:dm.ga0 + dm.d], z_ref[:, dm.gb0:dm.gb0 + dm.d],
            gn_ref, wa_ref, wb_ref, wo_ref, n2_ref, wrh_ref, wrl_ref, filler)
        x1_ref[...] = x1
        h2_ref[...] = h2
        lg_ref[...] = logits

    mod_in = mod_ref[...]
    hn_ref[...] = (_rms(x_ref[...], n1_ref[...]) * (1 + mod_in[:, dm.d:2 * dm.d]) + mod_in[:, 0:dm.d]).astype(BF16)
    pieces = iter([(w_ref, a, min(a + PROJ_PIECE, w_ref.shape[1]), c0)
                   for w_ref, c0 in ((wia_ref, 0), (wil_ref, dm.al0), (wig_ref, dm.ga0))
                   for a in range(0, w_ref.shape[1], PROJ_PIECE)])

    def project_piece():
        piece = next(pieces, None)
        if piece is not None:
            w_ref, a, b, c0 = piece
            zp_ref[:, c0 + a:c0 + b] = _dot(hn_ref[...], w_ref[:, a:b])
        return piece is not None

    project_piece()
    project_piece()
    uext_ref[POOL_HIST:POOL_HIST + tt, :] = z_ref[:, 0:dm.pw]
    ya = pool_branch()
    project_piece()
    la = _log_decay(z_ref[:, dm.al0:dm.al0 + LANE], wal_ref, bal_ref)
    tri = jnp.where(causal, 1.0, 0.0).astype(BF16)
    a0, a1, a2 = _split3(la)
    bc = _dot(tri, a0) + (_dot(tri, a1) + _dot(tri, a2))
    b_ref[...] = bc
    project_piece()
    qe = scaled_queries(bc)
    k = z_ref[:, dm.k0:dm.k0 + dm.hk]
    vb = z_ref[:, dm.v0:dm.v0 + dm.gw].astype(BF16)
    blast = bc[tt - 1:tt, :]
    fast = jnp.max(-blast) <= GLA_FAST_MAX_DECAY
    ke = (k * jnp.exp(-bc)).astype(BF16)
    intra = []
    for h in range(dm.h):
        qh = jnp.where(head_masks[h], qe, 0.0).astype(BF16)
        sc = lax.dot_general(qh, ke, (((1,), (1,)), ((), ())), preferred_element_type=F32)
        sc = jnp.where(causal, sc, 0.0).astype(BF16)
        intra.append(_dot(sc, vb[:, h * dm.dv:(h + 1) * dm.dv]))
    project_piece()
    finish(ya, inter_chunk(qe) + jnp.concatenate(intra, axis=1), project_piece)
    while project_piece():
        pass

    @pl.when(jnp.logical_not(fast))
    def _():
        rows = lax.broadcasted_iota(I32, (tt, 1), 0)
        for h in range(dm.h):
            def body(tb, carry, h=h):
                base = pl.multiple_of(tb * 8, 8)
                q8 = z_ref[pl.ds(base, 8), dm.q0:dm.q0 + dm.hk] * scale
                b8 = b_ref[pl.ds(base, 8), :]
                vh = z_ref[:, dm.v0 + h * dm.dv:dm.v0 + (h + 1) * dm.dv]
                out_rows = []
                for r in range(8):
                    keep = (rows <= base + r) & head_masks[h]
                    dec = jnp.exp(jnp.where(keep, b8[r:r + 1, :] - b_ref[...], -jnp.inf))
                    wgt = (q8[r:r + 1, :] * dec) * z_ref[:, dm.k0:dm.k0 + dm.hk]
                    s_col = jnp.sum(wgt, axis=-1, keepdims=True)
                    out_rows.append(jnp.sum(s_col * vh, axis=0, keepdims=True))
                oi_ref[pl.ds(base, 8), h * dm.dv:(h + 1) * dm.dv] = jnp.concatenate(out_rows, axis=0)
                return carry
            lax.fori_loop(0, tt // 8, body, 0)
        finish(pool_branch(), inter_chunk(scaled_queries(b_ref[...])) + oi_ref[...])

    bc = b_ref[...]
    blast = bc[tt - 1:tt, :]
    kd = z_ref[:, dm.k0:dm.k0 + dm.hk] * jnp.exp(blast - bc)
    u_all = _dot(kd.T.astype(BF16), z_ref[:, dm.v0:dm.v0 + dm.gw].astype(BF16))
    dec_col = jnp.broadcast_to(jnp.exp(blast), (dm.hk, dm.hk)).T[:, 0:dm.dv]
    upd = jnp.concatenate(
        [u_all[h * dm.dk:(h + 1) * dm.dk, h * dm.dv:(h + 1) * dm.dv] for h in range(dm.h)], axis=0)
    s_ref[...] = dec_col * s_ref[...] + upd
    uext_ref[0:POOL_HIST, :] = uext_ref[tt:tt + POOL_HIST, :]


def _const_spec(shape):
    nd = len(shape)
    return pl.BlockSpec(shape, lambda *_: (0,) * nd, pipeline_mode=pl.Buffered(1))


def _mix_prompt(dm, x2d, mod3, wts, bsz, seq, n_exp):
    tt = 256 if seq % 256 == 0 else seq
    n_t = seq // tt
    n = bsz * seq
    d = dm.d
    kern = functools.partial(_mixp_kernel, dm=dm, tt=tt, n_t=n_t)
    tiles = bsz * n_t
    proj = lambda i: jnp.minimum(i, tiles - 1)
    fin = lambda i: jnp.maximum(i - 1, 0)
    row = lambda i: (fin(i), 0)
    in_specs = [pl.BlockSpec((tt, d), lambda i: (proj(i), 0)),
                pl.BlockSpec((None, 1, N_MOD * d), lambda i: (proj(i) // n_t, 0, 0))]
    in_specs += [_const_spec(w.shape) for w in wts]
    out_specs = [pl.BlockSpec((tt, d), row), pl.BlockSpec((tt, d // 2), row), pl.BlockSpec((tt, n_exp), row),
                 pl.BlockSpec((None, POOL_HIST - 1, dm.pw), lambda i: (fin(i) // n_t, 0, 0)),
                 pl.BlockSpec((None, dm.hk, dm.dv), lambda i: (fin(i) // n_t, 0, 0))]
    out_shape = [jax.ShapeDtypeStruct((n, d), F32), jax.ShapeDtypeStruct((n, d // 2), I32),
                 jax.ShapeDtypeStruct((n, n_exp), F32),
                 jax.ShapeDtypeStruct((bsz, POOL_HIST - 1, dm.pw), F32),
                 jax.ShapeDtypeStruct((bsz, dm.hk, dm.dv), F32)]
    scratch = [pltpu.VMEM((tt, dm.zw), F32), pltpu.VMEM((tt, dm.zw), F32),
               pltpu.VMEM((tt, d), F32), pltpu.VMEM((1, N_MOD * d), F32), pltpu.VMEM((tt, d), BF16),
               pltpu.VMEM((tt + POOL_HIST, dm.pw), F32),
               pltpu.VMEM((dm.hk, dm.dv), F32), pltpu.VMEM((tt, dm.hk), F32), pltpu.VMEM((tt, dm.gw), F32)]
    return pl.pallas_call(
        kern, grid=(tiles + 1,), in_specs=in_specs, out_specs=out_specs, out_shape=out_shape,
        scratch_shapes=scratch, compiler_params=_cp(("arbitrary",)), name="mixp",
    )(x2d, mod3, *wts)


def _sin_kernel(x_ref, mod_ref, n1_ref, wia_ref, wig_ref, wil_ref, z_ref, *, dm):
    _in_project(dm, x_ref[...], mod_ref[...], n1_ref, (wia_ref, wig_ref, wil_ref), z_ref)


def _sample_in(dm, xs2d, mod_s, n1, win, bs, t_s):
    d = dm.d
    return pl.pallas_call(
        functools.partial(_sin_kernel, dm=dm),
        grid=(t_s,),
        in_specs=[pl.BlockSpec((bs, d), lambda t: (0, t)), _const_spec(mod_s.shape), _const_spec(n1.shape)]
        + [_const_spec(w.shape) for w in win],
        out_specs=pl.BlockSpec((bs, dm.zw), lambda t: (t, 0)),
        out_shape=jax.ShapeDtypeStruct((t_s * bs, dm.zw), F32),
        compiler_params=_cp(("arbitrary",)), name="sample_in",
    )(xs2d, mod_s, n1, *win)


def _sst_kernel(z_ref, pool_ref, gs_ref, wal_ref, bal_ref,
                mixed_ref, o_ref, npool_ref, ngs_ref,
                qt_ref, kt_ref, dt_ref, vt_ref, ot_ref, *, dm, bs, t_s, ch, pos0):
    i = pl.program_id(0)
    n_i = pl.num_programs(0)
    scale = dm.dk ** -0.5
    past = POOL_HIST - 1

    @pl.when(i == 0)
    def _():
        for t in range(t_s):
            r0 = t * bs
            la = _log_decay(z_ref[r0:r0 + bs, dm.al0:dm.al0 + LANE], wal_ref, bal_ref)
            dt_ref[t] = jnp.exp(la).T
            qt_ref[t] = (z_ref[r0:r0 + bs, dm.q0:dm.q0 + dm.hk] * scale).T
            kt_ref[t] = z_ref[r0:r0 + bs, dm.k0:dm.k0 + dm.hk].T
            vt_ref[t] = z_ref[r0:r0 + bs, dm.v0:dm.v0 + dm.gw].T
        ot_ref[...] = jnp.zeros(ot_ref.shape, F32)

        def item(s, c0, c1):
            if s >= 0:
                return z_ref[s * bs:(s + 1) * bs, c0:c1]
            return pool_ref[:, (past + s) * dm.pw + c0:(past + s) * dm.pw + c1]

        gw = dm.pw // len(POOL_WINDOWS)
        for t in range(t_s):
            for g, w in enumerate(POOL_WINDOWS):
                c0, c1 = g * gw, (g + 1) * gw
                cur = item(t, c0, c1)
                acc = cur
                for r in range(1, w):
                    acc = acc + item(t - r, c0, c1)
                cnt = float(min(w, pos0 + t + 1))
                mixed_ref[t * bs:(t + 1) * bs, c0:c1] = acc / cnt - cur
        for r in range(past):
            npool_ref[:, r * dm.pw:(r + 1) * dm.pw] = item(r - past + t_s, 0, dm.pw)

    per_head = dm.dk // ch
    hoff = pl.multiple_of((i // per_head) * dm.dv, dm.dv)
    for p in range(ch):
        hd = i * ch + p
        st = gs_ref[:, p, :].T
        for t in range(t_s):
            st = dt_ref[t, pl.ds(hd, 1), :] * st + kt_ref[t, pl.ds(hd, 1), :] * vt_ref[t, pl.ds(hoff, dm.dv), :]
            ot_ref[t, pl.ds(hoff, dm.dv), :] += qt_ref[t, pl.ds(hd, 1), :] * st
        ngs_ref[:, p, :] = st.T

    @pl.when(i == n_i - 1)
    def _():
        for t in range(t_s):
            for h in range(dm.h):
                o_ref[t * bs:(t + 1) * bs, h * dm.dv:(h + 1) * dm.dv] = ot_ref[t, h * dm.dv:(h + 1) * dm.dv, :].T


def _sample_state(dm, z_s, pool2d, gs3, wal, bal, bs, t_s):
    ch = 16
    n_i = dm.hk // ch
    past = POOL_HIST - 1
    kern = functools.partial(_sst_kernel, dm=dm, bs=bs, t_s=t_s, ch=ch, pos0=PAST_LEN)
    full = lambda shape: pl.BlockSpec(shape, lambda i: (0,) * len(shape))
    return pl.pallas_call(
        kern, grid=(n_i,),
        in_specs=[_const_spec(z_s.shape), _const_spec(pool2d.shape),
                  pl.BlockSpec((bs, ch, dm.dv), lambda i: (0, i, 0)),
                  _const_spec(wal.shape), _const_spec(bal.shape)],
        out_specs=[full((t_s * bs, dm.pw)), full((t_s * bs, dm.gw)), full((bs, past * dm.pw)),
                   pl.BlockSpec((bs, ch, dm.dv), lambda i: (0, i, 0))],
        out_shape=[jax.ShapeDtypeStruct((t_s * bs, dm.pw), F32), jax.ShapeDtypeStruct((t_s * bs, dm.gw), F32),
                   jax.ShapeDtypeStruct((bs, past * dm.pw), F32), jax.ShapeDtypeStruct(gs3.shape, F32)],
        scratch_shapes=[pltpu.VMEM((t_s, dm.hk, bs), F32), pltpu.VMEM((t_s, dm.hk, bs), F32),
                        pltpu.VMEM((t_s, dm.hk, bs), F32), pltpu.VMEM((t_s, dm.gw, bs), F32),
                        pltpu.VMEM((t_s, dm.gw, bs), F32)],
        compiler_params=_cp(("arbitrary",)), name="sample_state",
    )(z_s, pool2d, gs3, wal, bal)


def _spost_kernel(x_ref, mod_ref, z_ref, mixed_ref, o_ref, pw_ref, ps_ref, gn_ref,
                  wa_ref, wb_ref, wo_ref, n2_ref, wrh_ref, wrl_ref,
                  x1_ref, h2_ref, lg_ref, *, dm):
    gw = dm.pw // len(POOL_WINDOWS)
    ya = _pool_project([mixed_ref[:, g * gw:(g + 1) * gw] for g in range(len(POOL_WINDOWS))], pw_ref, ps_ref)
    x1, h2, logits = _post_mix(
        dm, x_ref[...], mod_ref[...], ya, o_ref[...],
        z_ref[:, dm.go0:dm.go0 + dm.gw], z_ref[:, dm.ga0:dm.ga0 + dm.d], z_ref[:, dm.gb0:dm.gb0 + dm.d],
        gn_ref, wa_ref, wb_ref, wo_ref, n2_ref, wrh_ref, wrl_ref)
    x1_ref[...] = x1
    h2_ref[...] = h2
    lg_ref[...] = logits


def _sample_post(dm, xs2d, mod_s, z_s, mixed_s, o_s, wts, bs, t_s, n_exp):
    d = dm.d
    rows = lambda t: (t, 0)
    in_specs = [pl.BlockSpec((bs, d), lambda t: (0, t)), _const_spec(mod_s.shape),
                pl.BlockSpec((bs, dm.zw), rows), pl.BlockSpec((bs, dm.pw), rows), pl.BlockSpec((bs, dm.gw), rows)]
    in_specs += [_const_spec(w.shape) for w in wts]
    return pl.pallas_call(
        functools.partial(_spost_kernel, dm=dm), grid=(t_s,), in_specs=in_specs,
        out_specs=[pl.BlockSpec((bs, d), rows), pl.BlockSpec((bs, d // 2), rows), pl.BlockSpec((bs, n_exp), rows)],
        out_shape=[jax.ShapeDtypeStruct((t_s * bs, d), F32), jax.ShapeDtypeStruct((t_s * bs, d // 2), I32),
                   jax.ShapeDtypeStruct((t_s * bs, n_exp), F32)],
        compiler_params=_cp(("arbitrary",)), name="sample_post",
    )(xs2d, mod_s, z_s, mixed_s, o_s, *wts)


def _route_kernel(lgp_ref, lgs_ref, bias_ref, idx_ref, w_ref, cnt_ref, *, n_exp, tiles_p):
    i = pl.program_id(0)
    s = jax.nn.sigmoid(jnp.where(i < tiles_p, lgp_ref[...], lgs_ref[...]))
    cur = s + bias_ref[...]
    tr = s.shape[0]
    lane = lax.broadcasted_iota(I32, (1, n_exp), 1).astype(F32)
    slot = lax.broadcasted_iota(I32, (1, LANE), 1)
    idx8 = jnp.zeros((tr, LANE), F32)
    w8 = jnp.zeros((tr, LANE), F32)
    wsum = jnp.zeros((tr, 1), F32)
    picked = jnp.zeros((tr, n_exp), F32)
    for k in range(TOP_K):
        m = jnp.max(cur, axis=-1, keepdims=True)
        am = jnp.min(jnp.where(cur == m, lane, float(n_exp)), axis=-1, keepdims=True)
        oh = lane == am
        sk = jnp.sum(jnp.where(oh, s, 0.0), axis=-1, keepdims=True)
        cur = jnp.where(oh, -jnp.inf, cur)
        idx8 = jnp.where(slot == k, am, idx8)
        w8 = jnp.where(slot == k, sk, w8)
        wsum = wsum + sk
        picked = picked + jnp.where(oh, 1.0, 0.0)
    idx_ref[...] = idx8.astype(I32)
    w_ref[...] = w8 / wsum * ROUTED_SCALE

    @pl.when(i == 0)
    def _():
        cnt_ref[...] = jnp.zeros(cnt_ref.shape, F32)

    cnt_ref[...] += jnp.sum(picked, axis=0, keepdims=True)


def _two_source_specs(block, tiles_p, tiles_s):
    return [pl.BlockSpec(block, lambda i: (jnp.minimum(i, tiles_p - 1), 0)),
            pl.BlockSpec(block, lambda i: (jnp.clip(i - tiles_p, 0, tiles_s - 1), 0))]


def _route(lg_p, lg_s, bias, tr):
    n_exp = lg_p.shape[1]
    tiles_p, tiles_s = lg_p.shape[0] // tr, lg_s.shape[0] // tr
    n = (tiles_p + tiles_s) * tr
    rows = lambda i: (i, 0)
    return pl.pallas_call(
        functools.partial(_route_kernel, n_exp=n_exp, tiles_p=tiles_p), grid=(tiles_p + tiles_s,),
        in_specs=_two_source_specs((tr, n_exp), tiles_p, tiles_s) + [_const_spec((1, n_exp))],
        out_specs=[pl.BlockSpec((tr, LANE), rows), pl.BlockSpec((tr, LANE), rows),
                   pl.BlockSpec((8, n_exp), lambda i: (0, 0))],
        out_shape=[jax.ShapeDtypeStruct((n, LANE), I32), jax.ShapeDtypeStruct((n, LANE), F32),
                   jax.ShapeDtypeStruct((8, n_exp), F32)],
        compiler_params=_cp(("arbitrary",)), name="route",
    )(lg_p, lg_s, bias.reshape(1, n_exp))


def _plan_kernel(idx_ref, cnt_ref, pos_ref, bounds_ref, start_ref, run_ref, *, n_exp):
    i = pl.program_id(0)
    lane = lax.broadcasted_iota(I32, (1, n_exp), 1)

    @pl.when(i == 0)
    def _():
        cnt = cnt_ref[...].astype(I32)
        padded = ((cnt + (EXPERT_BLOCK - 1)) // EXPERT_BLOCK) * EXPERT_BLOCK
        pe = padded
        s = 1
        while s < n_exp:
            pe = pe + jnp.where(lane >= s, pltpu.roll(pe, s, 1), 0)
            s *= 2
        start_ref[...] = (pe - padded).astype(F32)
        run_ref[...] = jnp.zeros(run_ref.shape, F32)
        row = lax.broadcasted_iota(I32, (8, n_exp), 0)
        bounds_ref[...] = jnp.where(row == 0, pe - padded, pe)

    idx = idx_ref[...]
    tr = idx.shape[0]
    onehots = [lane == idx[:, k:k + 1] for k in range(TOP_K)]
    member = onehots[0]
    for k in range(1, TOP_K):
        member = member | onehots[k]
    mb = jnp.where(member, 1.0, 0.0).astype(BF16)
    r_i = lax.broadcasted_iota(I32, (tr, tr), 0)
    c_i = lax.broadcasted_iota(I32, (tr, tr), 1)
    earlier = jnp.where(r_i > c_i, 1.0, 0.0).astype(BF16)
    dest = _dot(earlier, mb) + (start_ref[0:1, :] + run_ref[0:1, :])
    slot = lax.broadcasted_iota(I32, (1, LANE), 1)
    pos8 = jnp.zeros((tr, LANE), F32)
    for k in range(TOP_K):
        pk = jnp.sum(jnp.where(onehots[k], dest, 0.0), axis=-1, keepdims=True)
        pos8 = jnp.where(slot == k, pk, pos8)
    pos_ref[...] = pos8.T[0:TOP_K, :].astype(I32)
    run_ref[...] += jnp.sum(mb.astype(F32), axis=0, keepdims=True)


def _plan(idx8, counts, tr):
    n = idx8.shape[0]
    n_exp = counts.shape[1]
    rows = lambda i: (i, 0)
    return pl.pallas_call(
        functools.partial(_plan_kernel, n_exp=n_exp), grid=(n // tr,),
        in_specs=[pl.BlockSpec((tr, LANE), rows), _const_spec((8, n_exp))],
        out_specs=[pl.BlockSpec((TOP_K, tr), lambda i: (0, i)), pl.BlockSpec((8, n_exp), lambda i: (0, 0))],
        out_shape=[jax.ShapeDtypeStruct((TOP_K, n), I32), jax.ShapeDtypeStruct((8, n_exp), I32)],
        scratch_shapes=[pltpu.VMEM((8, n_exp), F32), pltpu.VMEM((8, n_exp), F32)],
        compiler_params=_cp(("arbitrary",)), name="plan",
    )(idx8, counts)


SC_CORES = 2
SC_SUBCORES = 16
SC_AXES = ("core", "subcore")
SC_WORKERS = SC_CORES * SC_SUBCORES
SC_WINDOW = 128


def _sc_mesh():
    return plsc.VectorSubcoreMesh(core_axis_name=SC_AXES[0], subcore_axis_name=SC_AXES[1],
                                  num_cores=SC_CORES, num_subcores=SC_SUBCORES)


def _sc_worker():
    return lax.axis_index(SC_AXES[1]) * SC_CORES + lax.axis_index(SC_AXES[0])


def _dispatch(h_p, h_s, pos_t, n_rows):
    n_p, c = h_p.shape
    n_s = h_s.shape[0]
    win_p, win_s = n_p // SC_WINDOW, n_s // SC_WINDOW
    assert n_p % (SC_WINDOW * SC_WORKERS) == 0 and n_s % SC_WINDOW == 0 and win_s <= SC_WORKERS

    @functools.partial(
        pl.kernel, out_type=jax.ShapeDtypeStruct((n_rows, c), h_p.dtype), mesh=_sc_mesh(), name="dispatch",
        scratch_types=[pltpu.VMEM((TOP_K, SC_WINDOW), I32), pltpu.VMEM((SC_WINDOW, c), h_p.dtype),
                       pltpu.SemaphoreType.DMA])
    def run(hp_hbm, hs_hbm, pos_hbm, xs_hbm, idx_v, rows_v, sem):
        wid = _sc_worker()

        def scatter_window(h_hbm, row0, tok0):
            pltpu.sync_copy(h_hbm.at[pl.ds(row0, SC_WINDOW)], rows_v)
            pltpu.sync_copy(pos_hbm.at[:, pl.ds(tok0, SC_WINDOW)], idx_v)
            copies = [pltpu.async_copy(rows_v, xs_hbm.at[idx_v.at[k]], sem) for k in range(TOP_K)]
            for cp in copies:
                cp.wait()

        def prompt_window(j, carry):
            row0 = pl.multiple_of((wid * (win_p // SC_WORKERS) + j) * SC_WINDOW, SC_WINDOW)
            scatter_window(hp_hbm, row0, row0)
            return carry
        lax.fori_loop(0, win_p // SC_WORKERS, prompt_window, 0)

        @pl.when(wid < win_s)
        def _():
            row0 = pl.multiple_of(wid * SC_WINDOW, SC_WINDOW)
            scatter_window(hs_hbm, row0, n_p + row0)

    return run(h_p, h_s, pos_t)


def _combine_gather(ys, pos_t):
    k_n = pos_t.shape[0] * pos_t.shape[1]
    c = ys.shape[1]
    win = SC_WINDOW // 2
    per_worker = k_n // (win * SC_WORKERS)
    assert k_n % (win * SC_WORKERS) == 0 and per_worker % 2 == 0
    dma = pltpu.SemaphoreType.DMA

    @functools.partial(
        pl.kernel, out_type=jax.ShapeDtypeStruct((k_n, c), ys.dtype), mesh=_sc_mesh(), name="combine_gather",
        scratch_types=[pltpu.VMEM((win,), I32), pltpu.VMEM((win,), I32),
                       pltpu.VMEM((win, c), ys.dtype), pltpu.VMEM((win, c), ys.dtype), dma, dma, dma, dma])
    def run(ys_hbm, idx_hbm, out_hbm, idx0, idx1, buf0, buf1, gsem0, gsem1, ssem0, ssem1):
        wid = _sc_worker()

        def rows(j):
            return pl.ds(pl.multiple_of((wid * per_worker + j) * win, win), win)

        def gather(j, idx_v, buf, sem):
            pltpu.sync_copy(idx_hbm.at[rows(j)], idx_v)
            pltpu.async_copy(ys_hbm.at[idx_v], buf, sem)

        def gather_wait(idx_v, buf, sem):
            pltpu.make_async_copy(ys_hbm.at[idx_v], buf, sem).wait()

        def store(j, buf, sem):
            pltpu.async_copy(buf, out_hbm.at[rows(j)], sem)

        def store_wait(buf, sem):
            pltpu.make_async_copy(buf, out_hbm.at[rows(0)], sem).wait()

        gather(0, idx0, buf0, gsem0)

        def two_windows(i, carry):
            j = 2 * i

            @pl.when(i > 0)
            def _():
                store_wait(buf1, ssem1)

            gather(j + 1, idx1, buf1, gsem1)
            gather_wait(idx0, buf0, gsem0)
            store(j, buf0, ssem0)

            @pl.when(j + 2 < per_worker)
            def _():
                store_wait(buf0, ssem0)
                gather(j + 2, idx0, buf0, gsem0)

            gather_wait(idx1, buf1, gsem1)
            store(j + 1, buf1, ssem1)
            return carry
        lax.fori_loop(0, per_worker // 2, two_windows, 0)
        store_wait(buf0, ssem0)
        store_wait(buf1, ssem1)

    return run(ys, pos_t.reshape(k_n))


MOE_RING = 8


MOE_WEIGHT_RING = 4


def _moe_kernel(st_ref, wg_ref, wu_ref, wd_ref, xs_ref, ys_ref, xbuf, ybuf, wgf, wuf, wdf, wgb, wub, wdb,
                semx, semy, semw):
    e = pl.program_id(0)
    n_e = pl.num_programs(0)

    def w_copies(ex):
        slot = ex % MOE_WEIGHT_RING
        return [pltpu.make_async_copy(src.at[ex], dst.at[slot], semw.at[slot, i])
                for i, (src, dst) in enumerate(((wg_ref, wgf), (wu_ref, wuf), (wd_ref, wdf)))]

    @pl.when(e == 0)
    def _():
        for ex in range(MOE_WEIGHT_RING - 1):
            @pl.when(ex < n_e)
            def _(ex=ex):
                for cp in w_copies(ex):
                    cp.start()

    @pl.when(e + (MOE_WEIGHT_RING - 1) < n_e)
    def _():
        for cp in w_copies(e + (MOE_WEIGHT_RING - 1)):
            cp.start()
    g0 = st_ref[e] // EXPERT_BLOCK
    nchunks = st_ref[e + 1] // EXPERT_BLOCK - g0
    ntot = st_ref[n_e] // EXPERT_BLOCK

    def rows(g):
        return pl.ds(pl.multiple_of(g * EXPERT_BLOCK, EXPERT_BLOCK), EXPERT_BLOCK)

    def x_copy(g):
        slot = g % MOE_RING
        return pltpu.make_async_copy(xs_ref.at[rows(g)], xbuf.at[slot], semx.at[slot])

    def y_copy(g):
        slot = g % MOE_RING
        return pltpu.make_async_copy(ybuf.at[slot], ys_ref.at[rows(g)], semy.at[slot])

    @pl.when(e == 0)
    def _():
        for g in range(MOE_RING - 1):
            @pl.when(g < ntot)
            def _(g=g):
                x_copy(g).start()

    for cp in w_copies(e):
        cp.wait()
    wslot = e % MOE_WEIGHT_RING
    wgb[...] = wgf[wslot].astype(BF16)
    wub[...] = wuf[wslot].astype(BF16)
    wdb[...] = wdf[wslot].astype(BF16)

    def take(g):
        @pl.when(g + (MOE_RING - 1) < ntot)
        def _():
            x_copy(g + (MOE_RING - 1)).start()

        x_copy(g).wait()

        @pl.when(g >= MOE_RING)
        def _():
            y_copy(g - MOE_RING).wait()

        return xbuf[g % MOE_RING]

    def expert_mlp(x_words):
        x = _unpack_rows(x_words).astype(BF16)
        mid = _silu(_dot(x, wgb[...])) * _dot(x, wub[...])
        return _pack_rows(_dot(mid.astype(BF16), wdb[...]))

    def put(g, y_words):
        ybuf[g % MOE_RING] = y_words
        y_copy(g).start()

    def run_chunks(g, m):
        xs = [take(g + i) for i in range(m)]
        y = expert_mlp(xs[0] if m == 1 else jnp.concatenate(xs, axis=0))
        for i in range(m):
            put(g + i, y[i * EXPERT_BLOCK:(i + 1) * EXPERT_BLOCK])

    def quad(p, carry):
        run_chunks(g0 + 4 * p, 4)
        return carry

    lax.fori_loop(0, nchunks // 4, quad, 0)
    rest = nchunks % 4
    g_rest = g0 + nchunks - rest

    @pl.when(rest >= 2)
    def _():
        run_chunks(g_rest, 2)

    @pl.when(rest % 2 == 1)
    def _():
        run_chunks(g0 + nchunks - 1, 1)

    @pl.when(e == n_e - 1)
    def _():
        for j in range(MOE_RING):
            g = ntot - MOE_RING + j

            @pl.when(g >= 0)
            def _(g=g):
                y_copy(g).wait()


def _moe(starts, xs, wg, wu, wd):
    n_rows, c = xs.shape
    n_exp, d, f = wg.shape
    grid_spec = pltpu.PrefetchScalarGridSpec(
        num_scalar_prefetch=1, grid=(n_exp,),
        in_specs=[pl.BlockSpec(memory_space=pl.ANY)] * 4,
        out_specs=pl.BlockSpec(memory_space=pl.ANY),
        scratch_shapes=[pltpu.VMEM((MOE_RING, EXPERT_BLOCK, c), I32), pltpu.VMEM((MOE_RING, EXPERT_BLOCK, c), I32),
                        pltpu.VMEM((MOE_WEIGHT_RING, d, f), F32), pltpu.VMEM((MOE_WEIGHT_RING, d, f), F32),
                        pltpu.VMEM((MOE_WEIGHT_RING, f, d), F32),
                        pltpu.VMEM((d, f), BF16), pltpu.VMEM((d, f), BF16), pltpu.VMEM((f, d), BF16),
                        pltpu.SemaphoreType.DMA((MOE_RING,)), pltpu.SemaphoreType.DMA((MOE_RING,)),
                        pltpu.SemaphoreType.DMA((MOE_WEIGHT_RING, 3))])
    return pl.pallas_call(
        _moe_kernel, grid_spec=grid_spec, out_shape=jax.ShapeDtypeStruct((n_rows, c), I32),
        compiler_params=_cp(("arbitrary",)), name="moe",
    )(starts, wg, wu, wd, xs)


def _final_kernel(yg_ref, w8_ref, x1_ref, h2_ref, mod_ref, wsg_ref, wsu_ref, wsd_ref, nf_ref, y_ref, *, d):
    hb = _unpack_rows(h2_ref[...]).astype(BF16)
    shared = _dot((_silu(_dot(hb, wsg_ref[...])) * _dot(hb, wsu_ref[...])).astype(BF16), wsd_ref[...])
    w8 = w8_ref[...]
    routed = _unpack_rows(yg_ref[0]) * w8[:, 0:1]
    for k in range(1, TOP_K):
        routed = routed + _unpack_rows(yg_ref[k]) * w8[:, k:k + 1]
    g2 = mod_ref[...][:, (N_MOD - 1) * d:N_MOD * d]
    x2 = x1_ref[...] + g2 * (routed + shared)
    y_ref[...] = _rms(x2, nf_ref[...])


def _final(yg3, w8, x1, h2, mod, mod_spec, wts, nf, tf, n_tiles, tile0, out_shape, out_spec):
    d = x1.shape[1]
    rows = lambda i: (i, 0)
    in_specs = [pl.BlockSpec((TOP_K, tf, d // 2), lambda i: (0, tile0 + i, 0)),
                pl.BlockSpec((tf, LANE), lambda i: (tile0 + i, 0)),
                pl.BlockSpec((tf, d), rows), pl.BlockSpec((tf, d // 2), rows), mod_spec]
    in_specs += [_const_spec(w.shape) for w in wts]
    in_specs += [_const_spec(nf.shape)]
    return pl.pallas_call(
        functools.partial(_final_kernel, d=d), grid=(n_tiles,), in_specs=in_specs,
        out_specs=out_spec, out_shape=out_shape,
        compiler_params=_cp(("arbitrary",)), name="final",
    )(yg3, w8, x1, h2, mod, *wts, nf)


def kernel(x_prompt, x_sample, c_prompt, c_sample, state_pool, state_gla, w_ada, b_ada, norm1, w_in,
           pool_w, pool_scale, gla_w_alpha, gla_b_alpha, gla_norm, w_branch_a, w_branch_b, w_out, norm2,
           w_router, router_bias, w_exp_gate, w_exp_up, w_exp_down, w_sh_gate, w_sh_up, w_sh_down, norm_f):
    depth = w_in.shape[0]
    assert depth == 1, "single-layer trunk"
    bp, seq, d = x_prompt.shape
    bs, t_s, _ = x_sample.shape
    _, _, past, pw = state_pool.shape
    _, _, heads, dk, dv = state_gla.shape
    rank = gla_w_alpha.shape[1]
    n_exp = w_router.shape[2]
    assert past == POOL_HIST - 1 and pw == len(POOL_WINDOWS) * LANE and dv == LANE and rank <= LANE
    dm = _Dims(d, pw, heads, dk, dv)
    n_p, n_s = bp * seq, bs * t_s
    n_all = n_p + n_s

    off = [0]
    for sz in (pw, dm.hk, dm.hk, dm.gw, dm.gw, rank, d, d):
        off.append(off[-1] + sz)
    wi = w_in[0]
    win = (wi[:, :off[5]].astype(BF16), wi[:, off[6]:].astype(BF16),
           jnp.concatenate([wi[:, off[5]:off[6]], jnp.zeros((d, LANE - rank), F32)], axis=1).astype(BF16))
    assert off[5] + 2 * d + LANE == dm.zw
    wal = jnp.concatenate([gla_w_alpha[0], jnp.zeros((LANE - rank, dm.hk), F32)], axis=0).astype(BF16)
    bal = gla_b_alpha[0].reshape(1, dm.hk)
    wr = w_router[0]
    wrh = wr.astype(BF16)
    wrl = (wr - wrh.astype(F32)).astype(BF16)
    post_w = (pool_w[0].astype(BF16), pool_scale[0].reshape(1, pw), gla_norm[0].reshape(1, dm.gw),
              w_branch_a[0].astype(BF16), w_branch_b[0].astype(BF16), w_out[0].astype(BF16),
              norm2[0].reshape(1, d), wrh, wrl)
    n1 = norm1[0].reshape(1, d)

    mod = _ada(jnp.concatenate([c_prompt, c_sample], axis=0), w_ada[0], b_ada[0])
    mod_p = mod[:bp].reshape(bp, 1, N_MOD * d)
    mod_s = mod[bp:]

    x1_p, h2_p, lg_p, npool_p, ngla_p = _mix_prompt(
        dm, x_prompt.reshape(n_p, d), mod_p, (n1,) + win + (wal, bal) + post_w, bp, seq, n_exp)

    xs2d = x_sample.reshape(bs, t_s * d)
    z_s = _sample_in(dm, xs2d, mod_s, n1, win, bs, t_s)
    mixed_s, o_s, npool_s, ngla_s = _sample_state(
        dm, z_s, state_pool[0].reshape(bs, past * pw), state_gla[0].reshape(bs, dm.hk, dv), wal, bal, bs, t_s)
    x1_s, h2_s, lg_s = _sample_post(dm, xs2d, mod_s, z_s, mixed_s, o_s, post_w, bs, t_s, n_exp)

    tr = 512 if (n_p % 512 == 0 and n_s % 512 == 0) else LANE
    assert n_p % tr == 0 and n_s % tr == 0 and n_p % bs == 0
    idx8, w8, counts = _route(lg_p, lg_s, router_bias[0], tr)
    nblk = (n_all * TOP_K) // EXPERT_BLOCK + n_exp
    pos_t, bounds = _plan(idx8, counts, tr)
    starts = jnp.concatenate([bounds[0], bounds[1, n_exp - 1:]])

    xs = _dispatch(h2_p, h2_s, pos_t, nblk * EXPERT_BLOCK)
    ys = _moe(starts, xs, w_exp_gate[0], w_exp_up[0], w_exp_down[0])
    yg3 = _combine_gather(ys, pos_t).reshape(TOP_K, n_all, d // 2)

    sh_w = (w_sh_gate[0].astype(BF16), w_sh_up[0].astype(BF16), w_sh_down[0].astype(BF16))
    nf = norm_f.reshape(1, d)
    tf = 256 if seq % 256 == 0 else seq
    n_tp = seq // tf
    y_p = _final(
        yg3, w8, x1_p, h2_p, mod_p,
        pl.BlockSpec((None, 1, N_MOD * d), lambda i: (i // n_tp, 0, 0)), sh_w, nf, tf, n_p // tf, 0,
        jax.ShapeDtypeStruct((n_p, d), F32), pl.BlockSpec((tf, d), lambda i: (i, 0)))
    y_s = _final(
        yg3, w8, x1_s, h2_s, mod_s,
        _const_spec(mod_s.shape), sh_w, nf, bs, t_s, n_p // bs,
        jax.ShapeDtypeStruct((bs, t_s * d), F32), pl.BlockSpec((bs, d), lambda i: (0, i)))

    return (y_p.reshape(bp, seq, d), y_s.reshape(bs, t_s, d),
            npool_p.reshape(depth, bp, past, pw), ngla_p.reshape(depth, bp, heads, dk, dv),
            npool_s.reshape(depth, bs, past, pw), ngla_s.reshape(depth, bs, heads, dk, dv))
```

```python
import functools

import jax
import jax.numpy as jnp
from jax import lax
from jax.experimental import pallas as pl
from jax.experimental.pallas import tpu as pltpu
from jax.experimental.pallas import tpu_sc as plsc

F32 = jnp.float32
BF16 = jnp.bfloat16
I32 = jnp.int32

EPS = 1e-6
N_MOD = 6
POOL_WINDOWS = (2, 4, 8, 16)
POOL_HIST = 16
GLA_GATE_TEMP = 16.0
TOP_K = 8
ROUTED_SCALE = 2.5
PAST_LEN = 16384
EXPERT_BLOCK = 128
LANE = 128
GLA_FAST_MAX_DECAY = 40.0
PROJ_PIECE = 512
VMEM_LIMIT = 56 * 1024 * 1024


def _cp(sem, vmem=VMEM_LIMIT):
    return pltpu.CompilerParams(dimension_semantics=sem, vmem_limit_bytes=vmem)


def _rms(x, g):
    return x * lax.rsqrt(jnp.mean(x * x, axis=-1, keepdims=True) + EPS) * g


def _silu(x):
    return x * jax.nn.sigmoid(x)


def _dot(a, b):
    return jnp.dot(a, b, preferred_element_type=F32)


def _split3(a):
    a0 = a.astype(BF16)
    r1 = a - a0.astype(F32)
    a1 = r1.astype(BF16)
    a2 = (r1 - a1.astype(F32)).astype(BF16)
    return a0, a1, a2


def _mod_parts(mod, d):
    return [mod[:, i * d:(i + 1) * d] for i in range(N_MOD)]


_HI16 = -65536


def _pack_rows(v):
    c = v.shape[1] // 2
    lo = lax.bitcast_convert_type(v[:, :c].astype(BF16).astype(F32), I32)
    hi = lax.bitcast_convert_type(v[:, c:].astype(BF16).astype(F32), I32)
    return (hi & _HI16) | lax.shift_right_logical(lo, 16)


def _unpack_rows(w):
    lo = lax.bitcast_convert_type(lax.shift_left(w, 16), F32)
    hi = lax.bitcast_convert_type(w & _HI16, F32)
    return jnp.concatenate([lo, hi], axis=1)


def _ada_kernel(c_ref, w_ref, b_ref, o_ref):
    a = _silu(c_ref[...]).astype(BF16)
    o_ref[...] = _dot(a, w_ref[...].astype(BF16)) + b_ref[...]


def _ada(c_all, w_ada, b_ada):
    n, d = c_all.shape
    cols = w_ada.shape[1]
    bc = d
    return pl.pallas_call(
        _ada_kernel,
        grid=(cols // bc,),
        in_specs=[pl.BlockSpec((n, d), lambda i: (0, 0)),
                  pl.BlockSpec((d, bc), lambda i: (0, i)),
                  pl.BlockSpec((1, bc), lambda i: (0, i))],
        out_specs=pl.BlockSpec((n, bc), lambda i: (0, i)),
        out_shape=jax.ShapeDtypeStruct((n, cols), F32),
        compiler_params=_cp(("arbitrary",)),
        name="ada",
    )(c_all, w_ada, b_ada.reshape(1, cols))


class _Dims:
    def __init__(self, d, pw, heads, dk, dv):
        self.d, self.pw, self.h, self.dk, self.dv = d, pw, heads, dk, dv
        self.hk, self.gw = heads * dk, heads * dv
        self.q0 = pw
        self.k0 = self.q0 + self.hk
        self.v0 = self.k0 + self.hk
        self.go0 = self.v0 + self.gw
        self.ga0 = self.go0 + self.gw
        self.gb0 = self.ga0 + d
        self.al0 = self.gb0 + d
        self.zw = self.al0 + LANE


def _log_decay(z_al, wal_ref, bal_ref):
    xal = _dot(z_al.astype(BF16), wal_ref[...]) + bal_ref[...]
    return jax.nn.log_sigmoid(xal) / GLA_GATE_TEMP


def _pool_project(mixed_groups, pw_ref, ps_ref):
    ys = [_dot(m.astype(BF16), pw_ref[g]) for g, m in enumerate(mixed_groups)]
    return jnp.concatenate(ys, axis=1) * ps_ref[...]


def _post_mix(dm, x, mod, ya, o, go, ga, gb, gn_ref, wa_ref, wb_ref, wo_ref, n2_ref, wrh_ref, wrl_ref,
              filler=lambda: None):
    _, _, g1, sh2, sc2, _ = _mod_parts(mod, dm.d)
    parts = []
    for h in range(dm.h):
        oh = o[:, h * dm.dv:(h + 1) * dm.dv]
        parts.append(oh * lax.rsqrt(jnp.mean(oh * oh, axis=-1, keepdims=True) + EPS))
    yb = jnp.concatenate(parts, axis=1) * gn_ref[...] * _silu(go)
    filler()
    m = (jax.nn.sigmoid(ga) * _dot(ya.astype(BF16), wa_ref[...])
         + jax.nn.sigmoid(gb) * _dot(yb.astype(BF16), wb_ref[...]))
    filler()
    x1 = x + g1 * _dot(m.astype(BF16), wo_ref[...])
    h2 = _rms(x1, n2_ref[...]) * (1 + sc2) + sh2
    hi = h2.astype(BF16)
    lo = (h2 - hi.astype(F32)).astype(BF16)
    filler()
    logits = _dot(hi, wrh_ref[...]) + (_dot(hi, wrl_ref[...]) + _dot(lo, wrh_ref[...]))
    filler()
    return x1, _pack_rows(h2), logits


def _in_project(dm, x, mod, n1_ref, win_refs, z_ref):
    h = (_rms(x, n1_ref[...]) * (1 + mod[:, dm.d:2 * dm.d]) + mod[:, 0:dm.d]).astype(BF16)
    starts = [0]
    for w_ref in win_refs:
        starts.append(starts[-1] + w_ref.shape[1])
    for i in (0, 2, 1):
        z_ref[:, starts[i]:starts[i + 1]] = _dot(h, win_refs[i][...])


def _mixp_kernel(x_ref, mod_ref, n1_ref, wia_ref, wig_ref, wil_ref, wal_ref, bal_ref, pw_ref, ps_ref, gn_ref,
                 wa_ref, wb_ref, wo_ref, n2_ref, wrh_ref, wrl_ref,
                 x1_ref, h2_ref, lg_ref, np_ref, ng_ref,
                 za_ref, zb_ref, xk_ref, mk_ref, hn_ref, uext_ref, s_ref, b_ref, oi_ref, *, dm, tt, n_t):
    n = pl.program_id(0)
    j = jnp.maximum(n - 1, 0) % n_t

    @pl.when(n == 0)
    def _():
        za_ref[...] = jnp.zeros(za_ref.shape, F32)
        zb_ref[...] = jnp.zeros(zb_ref.shape, F32)
        xk_ref[...] = jnp.zeros(xk_ref.shape, F32)
        mk_ref[...] = jnp.zeros(mk_ref.shape, F32)

    @pl.when(j == 0)
    def _():
        uext_ref[0:POOL_HIST, :] = jnp.zeros((POOL_HIST, dm.pw), F32)
        s_ref[...] = jnp.zeros(s_ref.shape, F32)

    @pl.when(n % 2 == 0)
    def _():
        _mixp_region(za_ref, zb_ref, x_ref, mod_ref, n1_ref, wia_ref, wig_ref, wil_ref, wal_ref, bal_ref, pw_ref,
                     ps_ref, gn_ref, wa_ref, wb_ref, wo_ref, n2_ref, wrh_ref, wrl_ref, x1_ref, h2_ref, lg_ref,
                     xk_ref, mk_ref, hn_ref, uext_ref, s_ref, b_ref, oi_ref, dm=dm, tt=tt, j=j)

    @pl.when(n % 2 == 1)
    def _():
        _mixp_region(zb_ref, za_ref, x_ref, mod_ref, n1_ref, wia_ref, wig_ref, wil_ref, wal_ref, bal_ref, pw_ref,
                     ps_ref, gn_ref, wa_ref, wb_ref, wo_ref, n2_ref, wrh_ref, wrl_ref, x1_ref, h2_ref, lg_ref,
                     xk_ref, mk_ref, hn_ref, uext_ref, s_ref, b_ref, oi_ref, dm=dm, tt=tt, j=j)

    xk_ref[...] = x_ref[...]
    mk_ref[...] = mod_ref[...]

    @pl.when(j == n_t - 1)
    def _():
        np_ref[...] = uext_ref[tt + 1:tt + POOL_HIST, :]
        ng_ref[...] = s_ref[...]


def _mixp_region(zp_ref, z_ref, x_ref, mod_ref, n1_ref, wia_ref, wig_ref, wil_ref, wal_ref, bal_ref, pw_ref,
                 ps_ref, gn_ref, wa_ref, wb_ref, wo_ref, n2_ref, wrh_ref, wrl_ref, x1_ref, h2_ref, lg_ref,
                 xk_ref, mk_ref, hn_ref, uext_ref, s_ref, b_ref, oi_ref, *, dm, tt, j):
    scale = dm.dk ** -0.5
    lane = lax.broadcasted_iota(I32, (1, dm.hk), 1)
    head_masks = [(lane >= h * dm.dk) & (lane < (h + 1) * dm.dk) for h in range(dm.h)]
    causal = lax.broadcasted_iota(I32, (tt, tt), 0) >= lax.broadcasted_iota(I32, (tt, tt), 1)

    def pool_branch():
        pos = j * tt + lax.broadcasted_iota(I32, (tt, 1), 0)
        gw = dm.pw // len(POOL_WINDOWS)
        mixed = []
        for g, w in enumerate(POOL_WINDOWS):
            c0 = g * gw
            cur = uext_ref[POOL_HIST:POOL_HIST + tt, c0:c0 + gw]
            acc = cur
            for i in range(1, w):
                acc = acc + uext_ref[POOL_HIST - i:POOL_HIST - i + tt, c0:c0 + gw]
            cnt = jnp.minimum(w, pos + 1).astype(F32)
            mixed.append(acc / cnt - cur)
        return _pool_project(mixed, pw_ref, ps_ref)

    def scaled_queries(bc):
        return (z_ref[:, dm.q0:dm.q0 + dm.hk] * scale) * jnp.exp(bc)

    def inter_chunk(qe):
        s_bf = s_ref[...].astype(BF16)
        return jnp.concatenate(
            [_dot(jnp.where(head_masks[h], qe, 0.0).astype(BF16), s_bf) for h in range(dm.h)], axis=1)

    def finish(ya, o, filler=lambda: None):
        x1, h2, logits = _post_mix(
            dm, xk_ref[...], mk_ref[...], ya, o, z_ref[:, dm.go0:dm.go0 + dm.gw],
            z_ref[:, dm.ga0:dm.ga0 + dm.d], z_ref[:, dm.gb0:dm.gb0 + dm.d],
            gn_ref, wa_ref, wb_ref, wo_ref, n2_ref, wrh_ref, wrl_ref, filler)
        x1_ref[...] = x1
        h2_ref[...] = h2
        lg_ref[...] = logits

    mod_in = mod_ref[...]
    hn_ref[...] = (_rms(x_ref[...], n1_ref[...]) * (1 + mod_in[:, dm.d:2 * dm.d]) + mod_in[:, 0:dm.d]).astype(BF16)
    pieces = iter([(w_ref, a, min(a + PROJ_PIECE, w_ref.shape[1]), c0)
                   for w_ref, c0 in ((wia_ref, 0), (wil_ref, dm.al0), (wig_ref, dm.ga0))
                   for a in range(0, w_ref.shape[1], PROJ_PIECE)])

    def project_piece():
        piece = next(pieces, None)
        if piece is not None:
            w_ref, a, b, c0 = piece
            zp_ref[:, c0 + a:c0 + b] = _dot(hn_ref[...], w_ref[:, a:b])
        return piece is not None

    project_piece()
    project_piece()
    uext_ref[POOL_HIST:POOL_HIST + tt, :] = z_ref[:, 0:dm.pw]
    ya = pool_branch()
    project_piece()
    la = _log_decay(z_ref[:, dm.al0:dm.al0 + LANE], wal_ref, bal_ref)
    tri = jnp.where(causal, 1.0, 0.0).astype(BF16)
    a0, a1, a2 = _split3(la)
    bc = _dot(tri, a0) + (_dot(tri, a1) + _dot(tri, a2))
    b_ref[...] = bc
    project_piece()
    qe = scaled_queries(bc)
    k = z_ref[:, dm.k0:dm.k0 + dm.hk]
    vb = z_ref[:, dm.v0:dm.v0 + dm.gw].astype(BF16)
    blast = bc[tt - 1:tt, :]
    fast = jnp.max(-blast) <= GLA_FAST_MAX_DECAY
    ke = (k * jnp.exp(-bc)).astype(BF16)
    intra = []
    for h in range(dm.h):
        qh = jnp.where(head_masks[h], qe, 0.0).astype(BF16)
        sc = lax.dot_general(qh, ke, (((1,), (1,)), ((), ())), preferred_element_type=F32)
        sc = jnp.where(causal, sc, 0.0).astype(BF16)
        intra.append(_dot(sc, vb[:, h * dm.dv:(h + 1) * dm.dv]))
    project_piece()
    finish(ya, inter_chunk(qe) + jnp.concatenate(intra, axis=1), project_piece)
    while project_piece():
        pass

    @pl.when(jnp.logical_not(fast))
    def _():
        rows = lax.broadcasted_iota(I32, (tt, 1), 0)
        for h in range(dm.h):
            def body(tb, carry, h=h):
                base = pl.multiple_of(tb * 8, 8)
                q8 = z_ref[pl.ds(base, 8), dm.q0:dm.q0 + dm.hk] * scale
                b8 = b_ref[pl.ds(base, 8), :]
                vh = z_ref[:, dm.v0 + h * dm.dv:dm.v0 + (h + 1) * dm.dv]
                out_rows = []
                for r in range(8):
                    keep = (rows <= base + r) & head_masks[h]
                    dec = jnp.exp(jnp.where(keep, b8[r:r + 1, :] - b_ref[...], -jnp.inf))
                    wgt = (q8[r:r + 1, :] * dec) * z_ref[:, dm.k0:dm.k0 + dm.hk]
                    s_col = jnp.sum(wgt, axis=-1, keepdims=True)
                    out_rows.append(jnp.sum(s_col * vh, axis=0, keepdims=True))
                oi_ref[pl.ds(base, 8), h * dm.dv:(h + 1) * dm.dv] = jnp.concatenate(out_rows, axis=0)
                return carry
            lax.fori_loop(0, tt // 8, body, 0)
        finish(pool_branch(), inter_chunk(scaled_queries(b_ref[...])) + oi_ref[...])

    bc = b_ref[...]
    blast = bc[tt - 1:tt, :]
    kd = z_ref[:, dm.k0:dm.k0 + dm.hk] * jnp.exp(blast - bc)
    u_all = _dot(kd.T.astype(BF16), z_ref[:, dm.v0:dm.v0 + dm.gw].astype(BF16))
    dec_col = jnp.broadcast_to(jnp.exp(blast), (dm.hk, dm.hk)).T[:, 0:dm.dv]
    upd = jnp.concatenate(
        [u_all[h * dm.dk:(h + 1) * dm.dk, h * dm.dv:(h + 1) * dm.dv] for h in range(dm.h)], axis=0)
    s_ref[...] = dec_col * s_ref[...] + upd
    uext_ref[0:POOL_HIST, :] = uext_ref[tt:tt + POOL_HIST, :]


def _const_spec(shape):
    nd = len(shape)
    return pl.BlockSpec(shape, lambda *_: (0,) * nd, pipeline_mode=pl.Buffered(1))


def _mix_prompt(dm, x2d, mod3, wts, bsz, seq, n_exp):
    tt = 256 if seq % 256 == 0 else seq
    n_t = seq // tt
    n = bsz * seq
    d = dm.d
    kern = functools.partial(_mixp_kernel, dm=dm, tt=tt, n_t=n_t)
    tiles = bsz * n_t
    proj = lambda i: jnp.minimum(i, tiles - 1)
    fin = lambda i: jnp.maximum(i - 1, 0)
    row = lambda i: (fin(i), 0)
    in_specs = [pl.BlockSpec((tt, d), lambda i: (proj(i), 0)),
                pl.BlockSpec((None, 1, N_MOD * d), lambda i: (proj(i) // n_t, 0, 0))]
    in_specs += [_const_spec(w.shape) for w in wts]
    out_specs = [pl.BlockSpec((tt, d), row), pl.BlockSpec((tt, d // 2), row), pl.BlockSpec((tt, n_exp), row),
                 pl.BlockSpec((None, POOL_HIST - 1, dm.pw), lambda i: (fin(i) // n_t, 0, 0)),
                 pl.BlockSpec((None, dm.hk, dm.dv), lambda i: (fin(i) // n_t, 0, 0))]
    out_shape = [jax.ShapeDtypeStruct((n, d), F32), jax.ShapeDtypeStruct((n, d // 2), I32),
                 jax.ShapeDtypeStruct((n, n_exp), F32),
                 jax.ShapeDtypeStruct((bsz, POOL_HIST - 1, dm.pw), F32),
                 jax.ShapeDtypeStruct((bsz, dm.hk, dm.dv), F32)]
    scratch = [pltpu.VMEM((tt, dm.zw), F32), pltpu.VMEM((tt, dm.zw), F32),
               pltpu.VMEM((tt, d), F32), pltpu.VMEM((1, N_MOD * d), F32), pltpu.VMEM((tt, d), BF16),
               pltpu.VMEM((tt + POOL_HIST, dm.pw), F32),
               pltpu.VMEM((dm.hk, dm.dv), F32), pltpu.VMEM((tt, dm.hk), F32), pltpu.VMEM((tt, dm.gw), F32)]
    return pl.pallas_call(
        kern, grid=(tiles + 1,), in_specs=in_specs, out_specs=out_specs, out_shape=out_shape,
        scratch_shapes=scratch, compiler_params=_cp(("arbitrary",)), name="mixp",
    )(x2d, mod3, *wts)


def _sin_kernel(x_ref, mod_ref, n1_ref, wia_ref, wig_ref, wil_ref, z_ref, *, dm):
    _in_project(dm, x_ref[...], mod_ref[...], n1_ref, (wia_ref, wig_ref, wil_ref), z_ref)


def _sample_in(dm, xs2d, mod_s, n1, win, bs, t_s):
    d = dm.d
    return pl.pallas_call(
        functools.partial(_sin_kernel, dm=dm),
        grid=(t_s,),
        in_specs=[pl.BlockSpec((bs, d), lambda t: (0, t)), _const_spec(mod_s.shape), _const_spec(n1.shape)]
        + [_const_spec(w.shape) for w in win],
        out_specs=pl.BlockSpec((bs, dm.zw), lambda t: (t, 0)),
        out_shape=jax.ShapeDtypeStruct((t_s * bs, dm.zw), F32),
        compiler_params=_cp(("arbitrary",)), name="sample_in",
    )(xs2d, mod_s, n1, *win)


def _sst_kernel(z_ref, pool_ref, gs_ref, wal_ref, bal_ref,
                mixed_ref, o_ref, npool_ref, ngs_ref,
                qt_ref, kt_ref, dt_ref, vt_ref, ot_ref, *, dm, bs, t_s, ch, pos0):
    i = pl.program_id(0)
    n_i = pl.num_programs(0)
    scale = dm.dk ** -0.5
    past = POOL_HIST - 1

    @pl.when(i == 0)
    def _():
        for t in range(t_s):
            r0 = t * bs
            la = _log_decay(z_ref[r0:r0 + bs, dm.al0:dm.al0 + LANE], wal_ref, bal_ref)
            dt_ref[t] = jnp.exp(la).T
            qt_ref[t] = (z_ref[r0:r0 + bs, dm.q0:dm.q0 + dm.hk] * scale).T
            kt_ref[t] = z_ref[r0:r0 + bs, dm.k0:dm.k0 + dm.hk].T
            vt_ref[t] = z_ref[r0:r0 + bs, dm.v0:dm.v0 + dm.gw].T
        ot_ref[...] = jnp.zeros(ot_ref.shape, F32)

        def item(s, c0, c1):
            if s >= 0:
                return z_ref[s * bs:(s + 1) * bs, c0:c1]
            return pool_ref[:, (past + s) * dm.pw + c0:(past + s) * dm.pw + c1]

        gw = dm.pw // len(POOL_WINDOWS)
        for t in range(t_s):
            for g, w in enumerate(POOL_WINDOWS):
                c0, c1 = g * gw, (g + 1) * gw
                cur = item(t, c0, c1)
                acc = cur
                for r in range(1, w):
                    acc = acc + item(t - r, c0, c1)
                cnt = float(min(w, pos0 + t + 1))
                mixed_ref[t * bs:(t + 1) * bs, c0:c1] = acc / cnt - cur
        for r in range(past):
            npool_ref[:, r * dm.pw:(r + 1) * dm.pw] = item(r - past + t_s, 0, dm.pw)

    per_head = dm.dk // ch
    hoff = pl.multiple_of((i // per_head) * dm.dv, dm.dv)
    for p in range(ch):
        hd = i * ch + p
        st = gs_ref[:, p, :].T
        for t in range(t_s):
            st = dt_ref[t, pl.ds(hd, 1), :] * st + kt_ref[t, pl.ds(hd, 1), :] * vt_ref[t, pl.ds(hoff, dm.dv), :]
            ot_ref[t, pl.ds(hoff, dm.dv), :] += qt_ref[t, pl.ds(hd, 1), :] * st
        ngs_ref[:, p, :] = st.T

    @pl.when(i == n_i - 1)
    def _():
        for t in range(t_s):
            for h in range(dm.h):
                o_ref[t * bs:(t + 1) * bs, h * dm.dv:(h + 1) * dm.dv] = ot_ref[t, h * dm.dv:(h + 1) * dm.dv, :].T


def _sample_state(dm, z_s, pool2d, gs3, wal, bal, bs, t_s):
    ch = 16
    n_i = dm.hk // ch
    past = POOL_HIST - 1
    kern = functools.partial(_sst_kernel, dm=dm, bs=bs, t_s=t_s, ch=ch, pos0=PAST_LEN)
    full = lambda shape: pl.BlockSpec(shape, lambda i: (0,) * len(shape))
    return pl.pallas_call(
        kern, grid=(n_i,),
        in_specs=[_const_spec(z_s.shape), _const_spec(pool2d.shape),
                  pl.BlockSpec((bs, ch, dm.dv), lambda i: (0, i, 0)),
                  _const_spec(wal.shape), _const_spec(bal.shape)],
        out_specs=[full((t_s * bs, dm.pw)), full((t_s * bs, dm.gw)), full((bs, past * dm.pw)),
                   pl.BlockSpec((bs, ch, dm.dv), lambda i: (0, i, 0))],
        out_shape=[jax.ShapeDtypeStruct((t_s * bs, dm.pw), F32), jax.ShapeDtypeStruct((t_s * bs, dm.gw), F32),
                   jax.ShapeDtypeStruct((bs, past * dm.pw), F32), jax.ShapeDtypeStruct(gs3.shape, F32)],
        scratch_shapes=[pltpu.VMEM((t_s, dm.hk, bs), F32), pltpu.VMEM((t_s, dm.hk, bs), F32),
                        pltpu.VMEM((t_s, dm.hk, bs), F32), pltpu.VMEM((t_s, dm.gw, bs), F32),
                        pltpu.VMEM((t_s, dm.gw, bs), F32)],
        compiler_params=_cp(("arbitrary",)), name="sample_state",
    )(z_s, pool2d, gs3, wal, bal)


def _spost_kernel(x_ref, mod_ref, z_ref, mixed_ref, o_ref, pw_ref, ps_ref, gn_ref,
                  wa_ref, wb_ref, wo_ref, n2_ref, wrh_ref, wrl_ref,
                  x1_ref, h2_ref, lg_ref, *, dm):
    gw = dm.pw // len(POOL_WINDOWS)
    ya = _pool_project([mixed_ref[:, g * gw:(g + 1) * gw] for g in range(len(POOL_WINDOWS))], pw_ref, ps_ref)
    x1, h2, logits = _post_mix(
        dm, x_ref[...], mod_ref[...], ya, o_ref[...],
        z_ref[:, dm.go0:dm.go0 + dm.gw], z_ref[:, dm.ga0:dm.ga0 + dm.d], z_ref[:, dm.gb0:dm.gb0 + dm.d],
        gn_ref, wa_ref, wb_ref, wo_ref, n2_ref, wrh_ref, wrl_ref)
    x1_ref[...] = x1
    h2_ref[...] = h2
    lg_ref[...] = logits


def _sample_post(dm, xs2d, mod_s, z_s, mixed_s, o_s, wts, bs, t_s, n_exp):
    d = dm.d
    rows = lambda t: (t, 0)
    in_specs = [pl.BlockSpec((bs, d), lambda t: (0, t)), _const_spec(mod_s.shape),
                pl.BlockSpec((bs, dm.zw), rows), pl.BlockSpec((bs, dm.pw), rows), pl.BlockSpec((bs, dm.gw), rows)]
    in_specs += [_const_spec(w.shape) for w in wts]
    return pl.pallas_call(
        functools.partial(_spost_kernel, dm=dm), grid=(t_s,), in_specs=in_specs,
        out_specs=[pl.BlockSpec((bs, d), rows), pl.BlockSpec((bs, d // 2), rows), pl.BlockSpec((bs, n_exp), rows)],
        out_shape=[jax.ShapeDtypeStruct((t_s * bs, d), F32), jax.ShapeDtypeStruct((t_s * bs, d // 2), I32),
                   jax.ShapeDtypeStruct((t_s * bs, n_exp), F32)],
        compiler_params=_cp(("arbitrary",)), name="sample_post",
    )(xs2d, mod_s, z_s, mixed_s, o_s, *wts)


def _route_kernel(lgp_ref, lgs_ref, bias_ref, idx_ref, w_ref, cnt_ref, *, n_exp, tiles_p):
    i = pl.program_id(0)
    s = jax.nn.sigmoid(jnp.where(i < tiles_p, lgp_ref[...], lgs_ref[...]))
    cur = s + bias_ref[...]
    tr = s.shape[0]
    lane = lax.broadcasted_iota(I32, (1, n_exp), 1).astype(F32)
    slot = lax.broadcasted_iota(I32, (1, LANE), 1)
    idx8 = jnp.zeros((tr, LANE), F32)
    w8 = jnp.zeros((tr, LANE), F32)
    wsum = jnp.zeros((tr, 1), F32)
    picked = jnp.zeros((tr, n_exp), F32)
    for k in range(TOP_K):
        m = jnp.max(cur, axis=-1, keepdims=True)
        am = jnp.min(jnp.where(cur == m, lane, float(n_exp)), axis=-1, keepdims=True)
        oh = lane == am
        sk = jnp.sum(jnp.where(oh, s, 0.0), axis=-1, keepdims=True)
        cur = jnp.where(oh, -jnp.inf, cur)
        idx8 = jnp.where(slot == k, am, idx8)
        w8 = jnp.where(slot == k, sk, w8)
        wsum = wsum + sk
        picked = picked + jnp.where(oh, 1.0, 0.0)
    idx_ref[...] = idx8.astype(I32)
    w_ref[...] = w8 / wsum * ROUTED_SCALE

    @pl.when(i == 0)
    def _():
        cnt_ref[...] = jnp.zeros(cnt_ref.shape, F32)

    cnt_ref[...] += jnp.sum(picked, axis=0, keepdims=True)


def _two_source_specs(block, tiles_p, tiles_s):
    return [pl.BlockSpec(block, lambda i: (jnp.minimum(i, tiles_p - 1), 0)),
            pl.BlockSpec(block, lambda i: (jnp.clip(i - tiles_p, 0, tiles_s - 1), 0))]


def _route(lg_p, lg_s, bias, tr):
    n_exp = lg_p.shape[1]
    tiles_p, tiles_s = lg_p.shape[0] // tr, lg_s.shape[0] // tr
    n = (tiles_p + tiles_s) * tr
    rows = lambda i: (i, 0)
    return pl.pallas_call(
        functools.partial(_route_kernel, n_exp=n_exp, tiles_p=tiles_p), grid=(tiles_p + tiles_s,),
        in_specs=_two_source_specs((tr, n_exp), tiles_p, tiles_s) + [_const_spec((1, n_exp))],
        out_specs=[pl.BlockSpec((tr, LANE), rows), pl.BlockSpec((tr, LANE), rows),
                   pl.BlockSpec((8, n_exp), lambda i: (0, 0))],
        out_shape=[jax.ShapeDtypeStruct((n, LANE), I32), jax.ShapeDtypeStruct((n, LANE), F32),
                   jax.ShapeDtypeStruct((8, n_exp), F32)],
        compiler_params=_cp(("arbitrary",)), name="route",
    )(lg_p, lg_s, bias.reshape(1, n_exp))


def _plan_kernel(idx_ref, cnt_ref, pos_ref, bounds_ref, start_ref, run_ref, *, n_exp):
    i = pl.program_id(0)
    lane = lax.broadcasted_iota(I32, (1, n_exp), 1)

    @pl.when(i == 0)
    def _():
        cnt = cnt_ref[...].astype(I32)
        padded = ((cnt + (EXPERT_BLOCK - 1)) // EXPERT_BLOCK) * EXPERT_BLOCK
        pe = padded
        s = 1
        while s < n_exp:
            pe = pe + jnp.where(lane >= s, pltpu.roll(pe, s, 1), 0)
            s *= 2
        start_ref[...] = (pe - padded).astype(F32)
        run_ref[...] = jnp.zeros(run_ref.shape, F32)
        row = lax.broadcasted_iota(I32, (8, n_exp), 0)
        bounds_ref[...] = jnp.where(row == 0, pe - padded, pe)

    idx = idx_ref[...]
    tr = idx.shape[0]
    onehots = [lane == idx[:, k:k + 1] for k in range(TOP_K)]
    member = onehots[0]
    for k in range(1, TOP_K):
        member = member | onehots[k]
    mb = jnp.where(member, 1.0, 0.0).astype(BF16)
    r_i = lax.broadcasted_iota(I32, (tr, tr), 0)
    c_i = lax.broadcasted_iota(I32, (tr, tr), 1)
    earlier = jnp.where(r_i > c_i, 1.0, 0.0).astype(BF16)
    dest = _dot(earlier, mb) + (start_ref[0:1, :] + run_ref[0:1, :])
    slot = lax.broadcasted_iota(I32, (1, LANE), 1)
    pos8 = jnp.zeros((tr, LANE), F32)
    for k in range(TOP_K):
        pk = jnp.sum(jnp.where(onehots[k], dest, 0.0), axis=-1, keepdims=True)
        pos8 = jnp.where(slot == k, pk, pos8)
    pos_ref[...] = pos8.T[0:TOP_K, :].astype(I32)
    run_ref[...] += jnp.sum(mb.astype(F32), axis=0, keepdims=True)


def _plan(idx8, counts, tr):
    n = idx8.shape[0]
    n_exp = counts.shape[1]
    rows = lambda i: (i, 0)
    return pl.pallas_call(
        functools.partial(_plan_kernel, n_exp=n_exp), grid=(n // tr,),
        in_specs=[pl.BlockSpec((tr, LANE), rows), _const_spec((8, n_exp))],
        out_specs=[pl.BlockSpec((TOP_K, tr), lambda i: (0, i)), pl.BlockSpec((8, n_exp), lambda i: (0, 0))],
        out_shape=[jax.ShapeDtypeStruct((TOP_K, n), I32), jax.ShapeDtypeStruct((8, n_exp), I32)],
        scratch_shapes=[pltpu.VMEM((8, n_exp), F32), pltpu.VMEM((8, n_exp), F32)],
        compiler_params=_cp(("arbitrary",)), name="plan",
    )(idx8, counts)


SC_CORES = 2
SC_SUBCORES = 16
SC_AXES = ("core", "subcore")
SC_WORKERS = SC_CORES * SC_SUBCORES
SC_WINDOW = 128


def _sc_mesh():
    return plsc.VectorSubcoreMesh(core_axis_name=SC_AXES[0], subcore_axis_name=SC_AXES[1],
                                  num_cores=SC_CORES, num_subcores=SC_SUBCORES)


def _sc_worker():
    return lax.axis_index(SC_AXES[1]) * SC_CORES + lax.axis_index(SC_AXES[0])


def _dispatch(h_p, h_s, pos_t, n_rows):
    n_p, c = h_p.shape
    n_s = h_s.shape[0]
    win_p, win_s = n_p // SC_WINDOW, n_s // SC_WINDOW
    assert n_p % (SC_WINDOW * SC_WORKERS) == 0 and n_s % SC_WINDOW == 0 and win_s <= SC_WORKERS

    @functools.partial(
        pl.kernel, out_type=jax.ShapeDtypeStruct((n_rows, c), h_p.dtype), mesh=_sc_mesh(), name="dispatch",
        scratch_types=[pltpu.VMEM((TOP_K, SC_WINDOW), I32), pltpu.VMEM((SC_WINDOW, c), h_p.dtype),
                       pltpu.SemaphoreType.DMA])
    def run(hp_hbm, hs_hbm, pos_hbm, xs_hbm, idx_v, rows_v, sem):
        wid = _sc_worker()

        def scatter_window(h_hbm, row0, tok0):
            pltpu.sync_copy(h_hbm.at[pl.ds(row0, SC_WINDOW)], rows_v)
            pltpu.sync_copy(pos_hbm.at[:, pl.ds(tok0, SC_WINDOW)], idx_v)
            copies = [pltpu.async_copy(rows_v, xs_hbm.at[idx_v.at[k]], sem) for k in range(TOP_K)]
            for cp in copies:
                cp.wait()

        def prompt_window(j, carry):
            row0 = pl.multiple_of((wid * (win_p // SC_WORKERS) + j) * SC_WINDOW, SC_WINDOW)
            scatter_window(hp_hbm, row0, row0)
            return carry
        lax.fori_loop(0, win_p // SC_WORKERS, prompt_window, 0)

        @pl.when(wid < win_s)
        def _():
            row0 = pl.multiple_of(wid * SC_WINDOW, SC_WINDOW)
            scatter_window(hs_hbm, row0, n_p + row0)

    return run(h_p, h_s, pos_t)


def _combine_gather(ys, pos_t):
    k_n = pos_t.shape[0] * pos_t.shape[1]
    c = ys.shape[1]
    win = SC_WINDOW // 2
    per_worker = k_n // (win * SC_WORKERS)
    assert k_n % (win * SC_WORKERS) == 0 and per_worker % 2 == 0
    dma = pltpu.SemaphoreType.DMA

    @functools.partial(
        pl.kernel, out_type=jax.ShapeDtypeStruct((k_n, c), ys.dtype), mesh=_sc_mesh(), name="combine_gather",
        scratch_types=[pltpu.VMEM((win,), I32), pltpu.VMEM((win,), I32),
                       pltpu.VMEM((win, c), ys.dtype), pltpu.VMEM((win, c), ys.dtype), dma, dma, dma, dma])
    def run(ys_hbm, idx_hbm, out_hbm, idx0, idx1, buf0, buf1, gsem0, gsem1, ssem0, ssem1):
        wid = _sc_worker()

        def rows(j):
            return pl.ds(pl.multiple_of((wid * per_worker + j) * win, win), win)

        def gather(j, idx_v, buf, sem):
            pltpu.sync_copy(idx_hbm.at[rows(j)], idx_v)
            pltpu.async_copy(ys_hbm.at[idx_v], buf, sem)

        def gather_wait(idx_v, buf, sem):
            pltpu.make_async_copy(ys_hbm.at[idx_v], buf, sem).wait()

        def store(j, buf, sem):
            pltpu.async_copy(buf, out_hbm.at[rows(j)], sem)

        def store_wait(buf, sem):
            pltpu.make_async_copy(buf, out_hbm.at[rows(0)], sem).wait()

        gather(0, idx0, buf0, gsem0)

        def two_windows(i, carry):
            j = 2 * i

            @pl.when(i > 0)
            def _():
                store_wait(buf1, ssem1)

            gather(j + 1, idx1, buf1, gsem1)
            gather_wait(idx0, buf0, gsem0)
            store(j, buf0, ssem0)

            @pl.when(j + 2 < per_worker)
            def _():
                store_wait(buf0, ssem0)
                gather(j + 2, idx0, buf0, gsem0)

            gather_wait(idx1, buf1, gsem1)
            store(j + 1, buf1, ssem1)
            return carry
        lax.fori_loop(0, per_worker // 2, two_windows, 0)
        store_wait(buf0, ssem0)
        store_wait(buf1, ssem1)

    return run(ys, pos_t.reshape(k_n))


MOE_RING = 8


MOE_WEIGHT_RING = 4


def _moe_kernel(st_ref, wg_ref, wu_ref, wd_ref, xs_ref, ys_ref, xbuf, ybuf, wgf, wuf, wdf, wgb, wub, wdb,
                semx, semy, semw):
    e = pl.program_id(0)
    n_e = pl.num_programs(0)

    def w_copies(ex):
        slot = ex % MOE_WEIGHT_RING
        return [pltpu.make_async_copy(src.at[ex], dst.at[slot], semw.at[slot, i])
                for i, (src, dst) in enumerate(((wg_ref, wgf), (wu_ref, wuf), (wd_ref, wdf)))]

    @pl.when(e == 0)
    def _():
        for ex in range(MOE_WEIGHT_RING - 1):
            @pl.when(ex < n_e)
            def _(ex=ex):
                for cp in w_copies(ex):
                    cp.start()

    @pl.when(e + (MOE_WEIGHT_RING - 1) < n_e)
    def _():
        for cp in w_copies(e + (MOE_WEIGHT_RING - 1)):
            cp.start()
    g0 = st_ref[e] // EXPERT_BLOCK
    nchunks = st_ref[e + 1] // EXPERT_BLOCK - g0
    ntot = st_ref[n_e] // EXPERT_BLOCK

    def rows(g):
        return pl.ds(pl.multiple_of(g * EXPERT_BLOCK, EXPERT_BLOCK), EXPERT_BLOCK)

    def x_copy(g):
        slot = g % MOE_RING
        return pltpu.make_async_copy(xs_ref.at[rows(g)], xbuf.at[slot], semx.at[slot])

    def y_copy(g):
        slot = g % MOE_RING
        return pltpu.make_async_copy(ybuf.at[slot], ys_ref.at[rows(g)], semy.at[slot])

    @pl.when(e == 0)
    def _():
        for g in range(MOE_RING - 1):
            @pl.when(g < ntot)
            def _(g=g):
                x_copy(g).start()

    for cp in w_copies(e):
        cp.wait()
    wslot = e % MOE_WEIGHT_RING
    wgb[...] = wgf[wslot].astype(BF16)
    wub[...] = wuf[wslot].astype(BF16)
    wdb[...] = wdf[wslot].astype(BF16)

    def take(g):
        @pl.when(g + (MOE_RING - 1) < ntot)
        def _():
            x_copy(g + (MOE_RING - 1)).start()

        x_copy(g).wait()

        @pl.when(g >= MOE_RING)
        def _():
            y_copy(g - MOE_RING).wait()

        return xbuf[g % MOE_RING]

    def expert_mlp(x_words):
        x = _unpack_rows(x_words).astype(BF16)
        mid = _silu(_dot(x, wgb[...])) * _dot(x, wub[...])
        return _pack_rows(_dot(mid.astype(BF16), wdb[...]))

    def put(g, y_words):
        ybuf[g % MOE_RING] = y_words
        y_copy(g).start()

    def run_chunks(g, m):
        xs = [take(g + i) for i in range(m)]
        y = expert_mlp(xs[0] if m == 1 else jnp.concatenate(xs, axis=0))
        for i in range(m):
            put(g + i, y[i * EXPERT_BLOCK:(i + 1) * EXPERT_BLOCK])

    def quad(p, carry):
        run_chunks(g0 + 4 * p, 4)
        return carry

    rest = nchunks % 4
    five = (rest == 1) & (nchunks >= 5)
    n_quads = nchunks // 4 - five.astype(I32)
    lax.fori_loop(0, n_quads, quad, 0)
    g_rest = g0 + 4 * n_quads

    @pl.when(five)
    def _():
        run_chunks(g_rest, 5)

    @pl.when(rest >= 2)
    def _():
        run_chunks(g_rest, 2)

    @pl.when((rest == 3) | (nchunks == 1))
    def _():
        run_chunks(g0 + nchunks - 1, 1)

    @pl.when(e == n_e - 1)
    def _():
        for j in range(MOE_RING):
            g = ntot - MOE_RING + j

            @pl.when(g >= 0)
            def _(g=g):
                y_copy(g).wait()


def _moe(starts, xs, wg, wu, wd):
    n_rows, c = xs.shape
    n_exp, d, f = wg.shape
    grid_spec = pltpu.PrefetchScalarGridSpec(
        num_scalar_prefetch=1, grid=(n_exp,),
        in_specs=[pl.BlockSpec(memory_space=pl.ANY)] * 4,
        out_specs=pl.BlockSpec(memory_space=pl.ANY),
        scratch_shapes=[pltpu.VMEM((MOE_RING, EXPERT_BLOCK, c), I32), pltpu.VMEM((MOE_RING, EXPERT_BLOCK, c), I32),
                        pltpu.VMEM((MOE_WEIGHT_RING, d, f), F32), pltpu.VMEM((MOE_WEIGHT_RING, d, f), F32),
                        pltpu.VMEM((MOE_WEIGHT_RING, f, d), F32),
                        pltpu.VMEM((d, f), BF16), pltpu.VMEM((d, f), BF16), pltpu.VMEM((f, d), BF16),
                        pltpu.SemaphoreType.DMA((MOE_RING,)), pltpu.SemaphoreType.DMA((MOE_RING,)),
                        pltpu.SemaphoreType.DMA((MOE_WEIGHT_RING, 3))])
    return pl.pallas_call(
        _moe_kernel, grid_spec=grid_spec, out_shape=jax.ShapeDtypeStruct((n_rows, c), I32),
        compiler_params=_cp(("arbitrary",)), name="moe",
    )(starts, wg, wu, wd, xs)


def _final_kernel(yg_ref, w8_ref, x1_ref, h2_ref, mod_ref, wsg_ref, wsu_ref, wsd_ref, nf_ref, y_ref, *, d):
    hb = _unpack_rows(h2_ref[...]).astype(BF16)
    shared = _dot((_silu(_dot(hb, wsg_ref[...])) * _dot(hb, wsu_ref[...])).astype(BF16), wsd_ref[...])
    w8 = w8_ref[...]
    routed = _unpack_rows(yg_ref[0]) * w8[:, 0:1]
    for k in range(1, TOP_K):
        routed = routed + _unpack_rows(yg_ref[k]) * w8[:, k:k + 1]
    g2 = mod_ref[...][:, (N_MOD - 1) * d:N_MOD * d]
    x2 = x1_ref[...] + g2 * (routed + shared)
    y_ref[...] = _rms(x2, nf_ref[...])


def _final(yg3, w8, x1, h2, mod, mod_spec, wts, nf, tf, n_tiles, tile0, out_shape, out_spec):
    d = x1.shape[1]
    rows = lambda i: (i, 0)
    in_specs = [pl.BlockSpec((TOP_K, tf, d // 2), lambda i: (0, tile0 + i, 0)),
                pl.BlockSpec((tf, LANE), lambda i: (tile0 + i, 0)),
                pl.BlockSpec((tf, d), rows), pl.BlockSpec((tf, d // 2), rows), mod_spec]
    in_specs += [_const_spec(w.shape) for w in wts]
    in_specs += [_const_spec(nf.shape)]
    return pl.pallas_call(
        functools.partial(_final_kernel, d=d), grid=(n_tiles,), in_specs=in_specs,
        out_specs=out_spec, out_shape=out_shape,
        compiler_params=_cp(("arbitrary",)), name="final",
    )(yg3, w8, x1, h2, mod, *wts, nf)


def kernel(x_prompt, x_sample, c_prompt, c_sample, state_pool, state_gla, w_ada, b_ada, norm1, w_in,
           pool_w, pool_scale, gla_w_alpha, gla_b_alpha, gla_norm, w_branch_a, w_branch_b, w_out, norm2,
           w_router, router_bias, w_exp_gate, w_exp_up, w_exp_down, w_sh_gate, w_sh_up, w_sh_down, norm_f):
    depth = w_in.shape[0]
    assert depth == 1, "single-layer trunk"
    bp, seq, d = x_prompt.shape
    bs, t_s, _ = x_sample.shape
    _, _, past, pw = state_pool.shape
    _, _, heads, dk, dv = state_gla.shape
    rank = gla_w_alpha.shape[1]
    n_exp = w_router.shape[2]
    assert past == POOL_HIST - 1 and pw == len(POOL_WINDOWS) * LANE and dv == LANE and rank <= LANE
    dm = _Dims(d, pw, heads, dk, dv)
    n_p, n_s = bp * seq, bs * t_s
    n_all = n_p + n_s

    off = [0]
    for sz in (pw, dm.hk, dm.hk, dm.gw, dm.gw, rank, d, d):
        off.append(off[-1] + sz)
    wi = w_in[0]
    win = (wi[:, :off[5]].astype(BF16), wi[:, off[6]:].astype(BF16),
           jnp.concatenate([wi[:, off[5]:off[6]], jnp.zeros((d, LANE - rank), F32)], axis=1).astype(BF16))
    assert off[5] + 2 * d + LANE == dm.zw
    wal = jnp.concatenate([gla_w_alpha[0], jnp.zeros((LANE - rank, dm.hk), F32)], axis=0).astype(BF16)
    bal = gla_b_alpha[0].reshape(1, dm.hk)
    wr = w_router[0]
    wrh = wr.astype(BF16)
    wrl = (wr - wrh.astype(F32)).astype(BF16)
    post_w = (pool_w[0].astype(BF16), pool_scale[0].reshape(1, pw), gla_norm[0].reshape(1, dm.gw),
              w_branch_a[0].astype(BF16), w_branch_b[0].astype(BF16), w_out[0].astype(BF16),
              norm2[0].reshape(1, d), wrh, wrl)
    n1 = norm1[0].reshape(1, d)

    mod = _ada(jnp.concatenate([c_prompt, c_sample], axis=0), w_ada[0], b_ada[0])
    mod_p = mod[:bp].reshape(bp, 1, N_MOD * d)
    mod_s = mod[bp:]

    x1_p, h2_p, lg_p, npool_p, ngla_p = _mix_prompt(
        dm, x_prompt.reshape(n_p, d), mod_p, (n1,) + win + (wal, bal) + post_w, bp, seq, n_exp)

    xs2d = x_sample.reshape(bs, t_s * d)
    z_s = _sample_in(dm, xs2d, mod_s, n1, win, bs, t_s)
    mixed_s, o_s, npool_s, ngla_s = _sample_state(
        dm, z_s, state_pool[0].reshape(bs, past * pw), state_gla[0].reshape(bs, dm.hk, dv), wal, bal, bs, t_s)
    x1_s, h2_s, lg_s = _sample_post(dm, xs2d, mod_s, z_s, mixed_s, o_s, post_w, bs, t_s, n_exp)

    tr = 512 if (n_p % 512 == 0 and n_s % 512 == 0) else LANE
    assert n_p % tr == 0 and n_s % tr == 0 and n_p % bs == 0
    idx8, w8, counts = _route(lg_p, lg_s, router_bias[0], tr)
    nblk = (n_all * TOP_K) // EXPERT_BLOCK + n_exp
    pos_t, bounds = _plan(idx8, counts, tr)
    starts = jnp.concatenate([bounds[0], bounds[1, n_exp - 1:]])

    xs = _dispatch(h2_p, h2_s, pos_t, nblk * EXPERT_BLOCK)
    ys = _moe(starts, xs, w_exp_gate[0], w_exp_up[0], w_exp_down[0])
    yg3 = _combine_gather(ys, pos_t).reshape(TOP_K, n_all, d // 2)

    sh_w = (w_sh_gate[0].astype(BF16), w_sh_up[0].astype(BF16), w_sh_down[0].astype(BF16))
    nf = norm_f.reshape(1, d)
    tf = 256 if seq % 256 == 0 else seq
    n_tp = seq // tf
    y_p = _final(
        yg3, w8, x1_p, h2_p, mod_p,
        pl.BlockSpec((None, 1, N_MOD * d), lambda i: (i // n_tp, 0, 0)), sh_w, nf, tf, n_p // tf, 0,
        jax.ShapeDtypeStruct((n_p, d), F32), pl.BlockSpec((tf, d), lambda i: (i, 0)))
    y_s = _final(
        yg3, w8, x1_s, h2_s, mod_s,
        _const_spec(mod_s.shape), sh_w, nf, bs, t_s, n_p // bs,
        jax.ShapeDtypeStruct((bs, t_s * d), F32), pl.BlockSpec((bs, d), lambda i: (0, i)))

    return (y_p.reshape(bp, seq, d), y_s.reshape(bs, t_s, d),
            npool_p.reshape(depth, bp, past, pw), ngla_p.reshape(depth, bp, heads, dk, dv),
            npool_s.reshape(depth, bs, past, pw), ngla_s.reshape(depth, bs, heads, dk, dv))
```

```python
import functools

import jax
import jax.numpy as jnp
from jax import lax
from jax.experimental import pallas as pl
from jax.experimental.pallas import tpu as pltpu
from jax.experimental.pallas import tpu_sc as plsc

F32 = jnp.float32
BF16 = jnp.bfloat16
I32 = jnp.int32

EPS = 1e-6
N_MOD = 6
POOL_WINDOWS = (2, 4, 8, 16)
POOL_HIST = 16
GLA_GATE_TEMP = 16.0
TOP_K = 8
ROUTED_SCALE = 2.5
PAST_LEN = 16384
EXPERT_BLOCK = 128
LANE = 128
GLA_FAST_MAX_DECAY = 40.0
PROJ_PIECE = 512
PROJ_LEAD = 2
VMEM_LIMIT = 56 * 1024 * 1024


def _cp(sem, vmem=VMEM_LIMIT):
    return pltpu.CompilerParams(dimension_semantics=sem, vmem_limit_bytes=vmem)


def _rms(x, g):
    return x * lax.rsqrt(jnp.mean(x * x, axis=-1, keepdims=True) + EPS) * g


def _silu(x):
    return x * jax.nn.sigmoid(x)


def _dot(a, b):
    return jnp.dot(a, b, preferred_element_type=F32)


def _split3(a):
    a0 = a.astype(BF16)
    r1 = a - a0.astype(F32)
    a1 = r1.astype(BF16)
    a2 = (r1 - a1.astype(F32)).astype(BF16)
    return a0, a1, a2


def _mod_parts(mod, d):
    return [mod[:, i * d:(i + 1) * d] for i in range(N_MOD)]


_HI16 = -65536


def _pack_rows(v):
    c = v.shape[1] // 2
    lo = lax.bitcast_convert_type(v[:, :c].astype(BF16).astype(F32), I32)
    hi = lax.bitcast_convert_type(v[:, c:].astype(BF16).astype(F32), I32)
    return (hi & _HI16) | lax.shift_right_logical(lo, 16)


def _unpack_rows(w):
    lo = lax.bitcast_convert_type(lax.shift_left(w, 16), F32)
    hi = lax.bitcast_convert_type(w & _HI16, F32)
    return jnp.concatenate([lo, hi], axis=1)


def _ada_kernel(c_ref, w_ref, b_ref, o_ref):
    a = _silu(c_ref[...]).astype(BF16)
    o_ref[...] = _dot(a, w_ref[...].astype(BF16)) + b_ref[...]


def _ada(c_all, w_ada, b_ada):
    n, d = c_all.shape
    cols = w_ada.shape[1]
    bc = d
    return pl.pallas_call(
        _ada_kernel,
        grid=(cols // bc,),
        in_specs=[pl.BlockSpec((n, d), lambda i: (0, 0)),
                  pl.BlockSpec((d, bc), lambda i: (0, i)),
                  pl.BlockSpec((1, bc), lambda i: (0, i))],
        out_specs=pl.BlockSpec((n, bc), lambda i: (0, i)),
        out_shape=jax.ShapeDtypeStruct((n, cols), F32),
        compiler_params=_cp(("arbitrary",)),
        name="ada",
    )(c_all, w_ada, b_ada.reshape(1, cols))


class _Dims:
    def __init__(self, d, pw, heads, dk, dv):
        self.d, self.pw, self.h, self.dk, self.dv = d, pw, heads, dk, dv
        self.hk, self.gw = heads * dk, heads * dv
        self.q0 = pw
        self.k0 = self.q0 + self.hk
        self.v0 = self.k0 + self.hk
        self.go0 = self.v0 + self.gw
        self.ga0 = self.go0 + self.gw
        self.gb0 = self.ga0 + d
        self.al0 = self.gb0 + d
        self.zw = self.al0 + LANE


def _log_decay(z_al, wal_ref, bal_ref):
    xal = _dot(z_al.astype(BF16), wal_ref[...]) + bal_ref[...]
    return jax.nn.log_sigmoid(xal) / GLA_GATE_TEMP


def _pool_project(mixed_groups, pw_ref, ps_ref):
    ys = [_dot(m.astype(BF16), pw_ref[g]) for g, m in enumerate(mixed_groups)]
    return jnp.concatenate(ys, axis=1) * ps_ref[...]


def _post_mix(dm, x, mod, ya, o, go, ga, gb, gn_ref, wa_ref, wb_ref, wo_ref, n2_ref, wrh_ref, wrl_ref,
              filler=lambda: None):
    _, _, g1, sh2, sc2, _ = _mod_parts(mod, dm.d)
    parts = []
    for h in range(dm.h):
        oh = o[:, h * dm.dv:(h + 1) * dm.dv]
        parts.append(oh * lax.rsqrt(jnp.mean(oh * oh, axis=-1, keepdims=True) + EPS))
    yb = jnp.concatenate(parts, axis=1) * gn_ref[...] * _silu(go)
    filler()
    m = (jax.nn.sigmoid(ga) * _dot(ya.astype(BF16), wa_ref[...])
         + jax.nn.sigmoid(gb) * _dot(yb.astype(BF16), wb_ref[...]))
    filler()
    x1 = x + g1 * _dot(m.astype(BF16), wo_ref[...])
    h2 = _rms(x1, n2_ref[...]) * (1 + sc2) + sh2
    hi = h2.astype(BF16)
    lo = (h2 - hi.astype(F32)).astype(BF16)
    filler()
    logits = _dot(hi, wrh_ref[...]) + (_dot(hi, wrl_ref[...]) + _dot(lo, wrh_ref[...]))
    filler()
    return x1, _pack_rows(h2), logits


def _in_project(dm, x, mod, n1_ref, win_refs, z_ref):
    h = (_rms(x, n1_ref[...]) * (1 + mod[:, dm.d:2 * dm.d]) + mod[:, 0:dm.d]).astype(BF16)
    starts = [0]
    for w_ref in win_refs:
        starts.append(starts[-1] + w_ref.shape[1])
    for i in (0, 2, 1):
        z_ref[:, starts[i]:starts[i + 1]] = _dot(h, win_refs[i][...])


def _mixp_kernel(x_ref, mod_ref, n1_ref, wia_ref, wig_ref, wil_ref, wal_ref, bal_ref, pw_ref, ps_ref, gn_ref,
                 wa_ref, wb_ref, wo_ref, n2_ref, wrh_ref, wrl_ref,
                 x1_ref, h2_ref, lg_ref, np_ref, ng_ref,
                 za_ref, zb_ref, xk_ref, mk_ref, hn_ref, uext_ref, s_ref, b_ref, oi_ref, *, dm, tt, n_t):
    n = pl.program_id(0)
    j = jnp.maximum(n - 1, 0) % n_t

    @pl.when(n == 0)
    def _():
        za_ref[...] = jnp.zeros(za_ref.shape, F32)
        zb_ref[...] = jnp.zeros(zb_ref.shape, F32)
        xk_ref[...] = jnp.zeros(xk_ref.shape, F32)
        mk_ref[...] = jnp.zeros(mk_ref.shape, F32)

    @pl.when(j == 0)
    def _():
        uext_ref[0:POOL_HIST, :] = jnp.zeros((POOL_HIST, dm.pw), F32)
        s_ref[...] = jnp.zeros(s_ref.shape, F32)

    @pl.when(n % 2 == 0)
    def _():
        _mixp_region(za_ref, zb_ref, x_ref, mod_ref, n1_ref, wia_ref, wig_ref, wil_ref, wal_ref, bal_ref, pw_ref,
                     ps_ref, gn_ref, wa_ref, wb_ref, wo_ref, n2_ref, wrh_ref, wrl_ref, x1_ref, h2_ref, lg_ref,
                     xk_ref, mk_ref, hn_ref, uext_ref, s_ref, b_ref, oi_ref, dm=dm, tt=tt, j=j)

    @pl.when(n % 2 == 1)
    def _():
        _mixp_region(zb_ref, za_ref, x_ref, mod_ref, n1_ref, wia_ref, wig_ref, wil_ref, wal_ref, bal_ref, pw_ref,
                     ps_ref, gn_ref, wa_ref, wb_ref, wo_ref, n2_ref, wrh_ref, wrl_ref, x1_ref, h2_ref, lg_ref,
                     xk_ref, mk_ref, hn_ref, uext_ref, s_ref, b_ref, oi_ref, dm=dm, tt=tt, j=j)

    xk_ref[...] = x_ref[...]
    mk_ref[...] = mod_ref[...]

    @pl.when(j == n_t - 1)
    def _():
        np_ref[...] = uext_ref[tt + 1:tt + POOL_HIST, :]
        ng_ref[...] = s_ref[...]


def _mixp_region(zp_ref, z_ref, x_ref, mod_ref, n1_ref, wia_ref, wig_ref, wil_ref, wal_ref, bal_ref, pw_ref,
                 ps_ref, gn_ref, wa_ref, wb_ref, wo_ref, n2_ref, wrh_ref, wrl_ref, x1_ref, h2_ref, lg_ref,
                 xk_ref, mk_ref, hn_ref, uext_ref, s_ref, b_ref, oi_ref, *, dm, tt, j):
    scale = dm.dk ** -0.5
    lane = lax.broadcasted_iota(I32, (1, dm.hk), 1)
    head_masks = [(lane >= h * dm.dk) & (lane < (h + 1) * dm.dk) for h in range(dm.h)]
    causal = lax.broadcasted_iota(I32, (tt, tt), 0) >= lax.broadcasted_iota(I32, (tt, tt), 1)

    def pool_branch():
        pos = j * tt + lax.broadcasted_iota(I32, (tt, 1), 0)
        gw = dm.pw // len(POOL_WINDOWS)
        mixed = []
        for g, w in enumerate(POOL_WINDOWS):
            c0 = g * gw
            cur = uext_ref[POOL_HIST:POOL_HIST + tt, c0:c0 + gw]
            acc = cur
            for i in range(1, w):
                acc = acc + uext_ref[POOL_HIST - i:POOL_HIST - i + tt, c0:c0 + gw]
            cnt = jnp.minimum(w, pos + 1).astype(F32)
            mixed.append(acc / cnt - cur)
        return _pool_project(mixed, pw_ref, ps_ref)

    def scaled_queries(bc):
        return (z_ref[:, dm.q0:dm.q0 + dm.hk] * scale) * jnp.exp(bc)

    def inter_chunk(qe):
        s_bf = s_ref[...].astype(BF16)
        return jnp.concatenate(
            [_dot(jnp.where(head_masks[h], qe, 0.0).astype(BF16), s_bf) for h in range(dm.h)], axis=1)

    def finish(ya, o, filler=lambda: None):
        x1, h2, logits = _post_mix(
            dm, xk_ref[...], mk_ref[...], ya, o, z_ref[:, dm.go0:dm.go0 + dm.gw],
            z_ref[:, dm.ga0:dm.ga0 + dm.d], z_ref[:, dm.gb0:dm.gb0 + dm.d],
            gn_ref, wa_ref, wb_ref, wo_ref, n2_ref, wrh_ref, wrl_ref, filler)
        x1_ref[...] = x1
        h2_ref[...] = h2
        lg_ref[...] = logits

    mod_in = mod_ref[...]
    hn_ref[...] = (_rms(x_ref[...], n1_ref[...]) * (1 + mod_in[:, dm.d:2 * dm.d]) + mod_in[:, 0:dm.d]).astype(BF16)
    pieces = iter([(w_ref, a, min(a + PROJ_PIECE, w_ref.shape[1]), c0)
                   for w_ref, c0 in ((wia_ref, 0), (wil_ref, dm.al0), (wig_ref, dm.ga0))
                   for a in range(0, w_ref.shape[1], PROJ_PIECE)])

    def project_piece():
        piece = next(pieces, None)
        if piece is not None:
            w_ref, a, b, c0 = piece
            zp_ref[:, c0 + a:c0 + b] = _dot(hn_ref[...], w_ref[:, a:b])
        return piece is not None

    for _ in range(PROJ_LEAD):
        project_piece()
    uext_ref[POOL_HIST:POOL_HIST + tt, :] = z_ref[:, 0:dm.pw]
    ya = pool_branch()
    project_piece()
    la = _log_decay(z_ref[:, dm.al0:dm.al0 + LANE], wal_ref, bal_ref)
    tri = jnp.where(causal, 1.0, 0.0).astype(BF16)
    a0, a1, a2 = _split3(la)
    bc = _dot(tri, a0) + (_dot(tri, a1) + _dot(tri, a2))
    b_ref[...] = bc
    project_piece()
    qe = scaled_queries(bc)
    k = z_ref[:, dm.k0:dm.k0 + dm.hk]
    vb = z_ref[:, dm.v0:dm.v0 + dm.gw].astype(BF16)
    blast = bc[tt - 1:tt, :]
    fast = jnp.max(-blast) <= GLA_FAST_MAX_DECAY
    ke = (k * jnp.exp(-bc)).astype(BF16)
    intra = []
    for h in range(dm.h):
        qh = jnp.where(head_masks[h], qe, 0.0).astype(BF16)
        sc = lax.dot_general(qh, ke, (((1,), (1,)), ((), ())), preferred_element_type=F32)
        sc = jnp.where(causal, sc, 0.0).astype(BF16)
        intra.append(_dot(sc, vb[:, h * dm.dv:(h + 1) * dm.dv]))
    project_piece()
    finish(ya, inter_chunk(qe) + jnp.concatenate(intra, axis=1), project_piece)
    while project_piece():
        pass

    @pl.when(jnp.logical_not(fast))
    def _():
        rows = lax.broadcasted_iota(I32, (tt, 1), 0)
        for h in range(dm.h):
            def body(tb, carry, h=h):
                base = pl.multiple_of(tb * 8, 8)
                q8 = z_ref[pl.ds(base, 8), dm.q0:dm.q0 + dm.hk] * scale
                b8 = b_ref[pl.ds(base, 8), :]
                vh = z_ref[:, dm.v0 + h * dm.dv:dm.v0 + (h + 1) * dm.dv]
                out_rows = []
                for r in range(8):
                    keep = (rows <= base + r) & head_masks[h]
                    dec = jnp.exp(jnp.where(keep, b8[r:r + 1, :] - b_ref[...], -jnp.inf))
                    wgt = (q8[r:r + 1, :] * dec) * z_ref[:, dm.k0:dm.k0 + dm.hk]
                    s_col = jnp.sum(wgt, axis=-1, keepdims=True)
                    out_rows.append(jnp.sum(s_col * vh, axis=0, keepdims=True))
                oi_ref[pl.ds(base, 8), h * dm.dv:(h + 1) * dm.dv] = jnp.concatenate(out_rows, axis=0)
                return carry
            lax.fori_loop(0, tt // 8, body, 0)
        finish(pool_branch(), inter_chunk(scaled_queries(b_ref[...])) + oi_ref[...])

    bc = b_ref[...]
    blast = bc[tt - 1:tt, :]
    kd = z_ref[:, dm.k0:dm.k0 + dm.hk] * jnp.exp(blast - bc)
    u_all = _dot(kd.T.astype(BF16), z_ref[:, dm.v0:dm.v0 + dm.gw].astype(BF16))
    dec_col = jnp.broadcast_to(jnp.exp(blast), (dm.hk, dm.hk)).T[:, 0:dm.dv]
    upd = jnp.concatenate(
        [u_all[h * dm.dk:(h + 1) * dm.dk, h * dm.dv:(h + 1) * dm.dv] for h in range(dm.h)], axis=0)
    s_ref[...] = dec_col * s_ref[...] + upd
    uext_ref[0:POOL_HIST, :] = uext_ref[tt:tt + POOL_HIST, :]


def _const_spec(shape):
    nd = len(shape)
    return pl.BlockSpec(shape, lambda *_: (0,) * nd, pipeline_mode=pl.Buffered(1))


def _mix_prompt(dm, x2d, mod3, wts, bsz, seq, n_exp):
    tt = 256 if seq % 256 == 0 else seq
    n_t = seq // tt
    n = bsz * seq
    d = dm.d
    kern = functools.partial(_mixp_kernel, dm=dm, tt=tt, n_t=n_t)
    tiles = bsz * n_t
    proj = lambda i: jnp.minimum(i, tiles - 1)
    fin = lambda i: jnp.maximum(i - 1, 0)
    row = lambda i: (fin(i), 0)
    in_specs = [pl.BlockSpec((tt, d), lambda i: (proj(i), 0)),
                pl.BlockSpec((None, 1, N_MOD * d), lambda i: (proj(i) // n_t, 0, 0))]
    in_specs += [_const_spec(w.shape) for w in wts]
    out_specs = [pl.BlockSpec((tt, d), row), pl.BlockSpec((tt, d // 2), row), pl.BlockSpec((tt, n_exp), row),
                 pl.BlockSpec((None, POOL_HIST - 1, dm.pw), lambda i: (fin(i) // n_t, 0, 0)),
                 pl.BlockSpec((None, dm.hk, dm.dv), lambda i: (fin(i) // n_t, 0, 0))]
    out_shape = [jax.ShapeDtypeStruct((n, d), F32), jax.ShapeDtypeStruct((n, d // 2), I32),
                 jax.ShapeDtypeStruct((n, n_exp), F32),
                 jax.ShapeDtypeStruct((bsz, POOL_HIST - 1, dm.pw), F32),
                 jax.ShapeDtypeStruct((bsz, dm.hk, dm.dv), F32)]
    scratch = [pltpu.VMEM((tt, dm.zw), F32), pltpu.VMEM((tt, dm.zw), F32),
               pltpu.VMEM((tt, d), F32), pltpu.VMEM((1, N_MOD * d), F32), pltpu.VMEM((tt, d), BF16),
               pltpu.VMEM((tt + POOL_HIST, dm.pw), F32),
               pltpu.VMEM((dm.hk, dm.dv), F32), pltpu.VMEM((tt, dm.hk), F32), pltpu.VMEM((tt, dm.gw), F32)]
    return pl.pallas_call(
        kern, grid=(tiles + 1,), in_specs=in_specs, out_specs=out_specs, out_shape=out_shape,
        scratch_shapes=scratch, compiler_params=_cp(("arbitrary",)), name="mixp",
    )(x2d, mod3, *wts)


def _sin_kernel(x_ref, mod_ref, n1_ref, wia_ref, wig_ref, wil_ref, z_ref, *, dm):
    _in_project(dm, x_ref[...], mod_ref[...], n1_ref, (wia_ref, wig_ref, wil_ref), z_ref)


def _sample_in(dm, xs2d, mod_s, n1, win, bs, t_s):
    d = dm.d
    return pl.pallas_call(
        functools.partial(_sin_kernel, dm=dm),
        grid=(t_s,),
        in_specs=[pl.BlockSpec((bs, d), lambda t: (0, t)), _const_spec(mod_s.shape), _const_spec(n1.shape)]
        + [_const_spec(w.shape) for w in win],
        out_specs=pl.BlockSpec((bs, dm.zw), lambda t: (t, 0)),
        out_shape=jax.ShapeDtypeStruct((t_s * bs, dm.zw), F32),
        compiler_params=_cp(("arbitrary",)), name="sample_in",
    )(xs2d, mod_s, n1, *win)


def _sst_kernel(z_ref, pool_ref, gs_ref, wal_ref, bal_ref,
                mixed_ref, o_ref, npool_ref, ngs_ref,
                qt_ref, kt_ref, dt_ref, vt_ref, ot_ref, *, dm, bs, t_s, ch, pos0):
    i = pl.program_id(0)
    n_i = pl.num_programs(0)
    scale = dm.dk ** -0.5
    past = POOL_HIST - 1

    @pl.when(i == 0)
    def _():
        for t in range(t_s):
            r0 = t * bs
            la = _log_decay(z_ref[r0:r0 + bs, dm.al0:dm.al0 + LANE], wal_ref, bal_ref)
            dt_ref[t] = jnp.exp(la).T
            qt_ref[t] = (z_ref[r0:r0 + bs, dm.q0:dm.q0 + dm.hk] * scale).T
            kt_ref[t] = z_ref[r0:r0 + bs, dm.k0:dm.k0 + dm.hk].T
            vt_ref[t] = z_ref[r0:r0 + bs, dm.v0:dm.v0 + dm.gw].T
        ot_ref[...] = jnp.zeros(ot_ref.shape, F32)

        def item(s, c0, c1):
            if s >= 0:
                return z_ref[s * bs:(s + 1) * bs, c0:c1]
            return pool_ref[:, (past + s) * dm.pw + c0:(past + s) * dm.pw + c1]

        gw = dm.pw // len(POOL_WINDOWS)
        for t in range(t_s):
            for g, w in enumerate(POOL_WINDOWS):
                c0, c1 = g * gw, (g + 1) * gw
                cur = item(t, c0, c1)
                acc = cur
                for r in range(1, w):
                    acc = acc + item(t - r, c0, c1)
                cnt = float(min(w, pos0 + t + 1))
                mixed_ref[t * bs:(t + 1) * bs, c0:c1] = acc / cnt - cur
        for r in range(past):
            npool_ref[:, r * dm.pw:(r + 1) * dm.pw] = item(r - past + t_s, 0, dm.pw)

    per_head = dm.dk // ch
    hoff = pl.multiple_of((i // per_head) * dm.dv, dm.dv)
    for p in range(ch):
        hd = i * ch + p
        st = gs_ref[:, p, :].T
        for t in range(t_s):
            st = dt_ref[t, pl.ds(hd, 1), :] * st + kt_ref[t, pl.ds(hd, 1), :] * vt_ref[t, pl.ds(hoff, dm.dv), :]
            ot_ref[t, pl.ds(hoff, dm.dv), :] += qt_ref[t, pl.ds(hd, 1), :] * st
        ngs_ref[:, p, :] = st.T

    @pl.when(i == n_i - 1)
    def _():
        for t in range(t_s):
            for h in range(dm.h):
                o_ref[t * bs:(t + 1) * bs, h * dm.dv:(h + 1) * dm.dv] = ot_ref[t, h * dm.dv:(h + 1) * dm.dv, :].T


def _sample_state(dm, z_s, pool2d, gs3, wal, bal, bs, t_s):
    ch = 16
    n_i = dm.hk // ch
    past = POOL_HIST - 1
    kern = functools.partial(_sst_kernel, dm=dm, bs=bs, t_s=t_s, ch=ch, pos0=PAST_LEN)
    full = lambda shape: pl.BlockSpec(shape, lambda i: (0,) * len(shape))
    return pl.pallas_call(
        kern, grid=(n_i,),
        in_specs=[_const_spec(z_s.shape), _const_spec(pool2d.shape),
                  pl.BlockSpec((bs, ch, dm.dv), lambda i: (0, i, 0)),
                  _const_spec(wal.shape), _const_spec(bal.shape)],
        out_specs=[full((t_s * bs, dm.pw)), full((t_s * bs, dm.gw)), full((bs, past * dm.pw)),
                   pl.BlockSpec((bs, ch, dm.dv), lambda i: (0, i, 0))],
        out_shape=[jax.ShapeDtypeStruct((t_s * bs, dm.pw), F32), jax.ShapeDtypeStruct((t_s * bs, dm.gw), F32),
                   jax.ShapeDtypeStruct((bs, past * dm.pw), F32), jax.ShapeDtypeStruct(gs3.shape, F32)],
        scratch_shapes=[pltpu.VMEM((t_s, dm.hk, bs), F32), pltpu.VMEM((t_s, dm.hk, bs), F32),
                        pltpu.VMEM((t_s, dm.hk, bs), F32), pltpu.VMEM((t_s, dm.gw, bs), F32),
                        pltpu.VMEM((t_s, dm.gw, bs), F32)],
        compiler_params=_cp(("arbitrary",)), name="sample_state",
    )(z_s, pool2d, gs3, wal, bal)


def _spost_kernel(x_ref, mod_ref, z_ref, mixed_ref, o_ref, pw_ref, ps_ref, gn_ref,
                  wa_ref, wb_ref, wo_ref, n2_ref, wrh_ref, wrl_ref,
                  x1_ref, h2_ref, lg_ref, *, dm):
    gw = dm.pw // len(POOL_WINDOWS)
    ya = _pool_project([mixed_ref[:, g * gw:(g + 1) * gw] for g in range(len(POOL_WINDOWS))], pw_ref, ps_ref)
    x1, h2, logits = _post_mix(
        dm, x_ref[...], mod_ref[...], ya, o_ref[...],
        z_ref[:, dm.go0:dm.go0 + dm.gw], z_ref[:, dm.ga0:dm.ga0 + dm.d], z_ref[:, dm.gb0:dm.gb0 + dm.d],
        gn_ref, wa_ref, wb_ref, wo_ref, n2_ref, wrh_ref, wrl_ref)
    x1_ref[...] = x1
    h2_ref[...] = h2
    lg_ref[...] = logits


def _sample_post(dm, xs2d, mod_s, z_s, mixed_s, o_s, wts, bs, t_s, n_exp):
    d = dm.d
    rows = lambda t: (t, 0)
    in_specs = [pl.BlockSpec((bs, d), lambda t: (0, t)), _const_spec(mod_s.shape),
                pl.BlockSpec((bs, dm.zw), rows), pl.BlockSpec((bs, dm.pw), rows), pl.BlockSpec((bs, dm.gw), rows)]
    in_specs += [_const_spec(w.shape) for w in wts]
    return pl.pallas_call(
        functools.partial(_spost_kernel, dm=dm), grid=(t_s,), in_specs=in_specs,
        out_specs=[pl.BlockSpec((bs, d), rows), pl.BlockSpec((bs, d // 2), rows), pl.BlockSpec((bs, n_exp), rows)],
        out_shape=[jax.ShapeDtypeStruct((t_s * bs, d), F32), jax.ShapeDtypeStruct((t_s * bs, d // 2), I32),
                   jax.ShapeDtypeStruct((t_s * bs, n_exp), F32)],
        compiler_params=_cp(("arbitrary",)), name="sample_post",
    )(xs2d, mod_s, z_s, mixed_s, o_s, *wts)


def _route_kernel(lgp_ref, lgs_ref, bias_ref, idx_ref, w_ref, cnt_ref, *, n_exp, tiles_p):
    i = pl.program_id(0)
    s = jax.nn.sigmoid(jnp.where(i < tiles_p, lgp_ref[...], lgs_ref[...]))
    cur = s + bias_ref[...]
    tr = s.shape[0]
    lane = lax.broadcasted_iota(I32, (1, n_exp), 1).astype(F32)
    slot = lax.broadcasted_iota(I32, (1, LANE), 1)
    idx8 = jnp.zeros((tr, LANE), F32)
    w8 = jnp.zeros((tr, LANE), F32)
    wsum = jnp.zeros((tr, 1), F32)
    picked = jnp.zeros((tr, n_exp), F32)
    for k in range(TOP_K):
        m = jnp.max(cur, axis=-1, keepdims=True)
        am = jnp.min(jnp.where(cur == m, lane, float(n_exp)), axis=-1, keepdims=True)
        oh = lane == am
        sk = jnp.sum(jnp.where(oh, s, 0.0), axis=-1, keepdims=True)
        cur = jnp.where(oh, -jnp.inf, cur)
        idx8 = jnp.where(slot == k, am, idx8)
        w8 = jnp.where(slot == k, sk, w8)
        wsum = wsum + sk
        picked = picked + jnp.where(oh, 1.0, 0.0)
    idx_ref[...] = idx8.astype(I32)
    w_ref[...] = w8 / wsum * ROUTED_SCALE

    @pl.when(i == 0)
    def _():
        cnt_ref[...] = jnp.zeros(cnt_ref.shape, F32)

    cnt_ref[...] += jnp.sum(picked, axis=0, keepdims=True)


def _two_source_specs(block, tiles_p, tiles_s):
    return [pl.BlockSpec(block, lambda i: (jnp.minimum(i, tiles_p - 1), 0)),
            pl.BlockSpec(block, lambda i: (jnp.clip(i - tiles_p, 0, tiles_s - 1), 0))]


def _route(lg_p, lg_s, bias, tr):
    n_exp = lg_p.shape[1]
    tiles_p, tiles_s = lg_p.shape[0] // tr, lg_s.shape[0] // tr
    n = (tiles_p + tiles_s) * tr
    rows = lambda i: (i, 0)
    return pl.pallas_call(
        functools.partial(_route_kernel, n_exp=n_exp, tiles_p=tiles_p), grid=(tiles_p + tiles_s,),
        in_specs=_two_source_specs((tr, n_exp), tiles_p, tiles_s) + [_const_spec((1, n_exp))],
        out_specs=[pl.BlockSpec((tr, LANE), rows), pl.BlockSpec((tr, LANE), rows),
                   pl.BlockSpec((8, n_exp), lambda i: (0, 0))],
        out_shape=[jax.ShapeDtypeStruct((n, LANE), I32), jax.ShapeDtypeStruct((n, LANE), F32),
                   jax.ShapeDtypeStruct((8, n_exp), F32)],
        compiler_params=_cp(("arbitrary",)), name="route",
    )(lg_p, lg_s, bias.reshape(1, n_exp))


def _plan_kernel(idx_ref, cnt_ref, pos_ref, bounds_ref, start_ref, run_ref, *, n_exp):
    i = pl.program_id(0)
    lane = lax.broadcasted_iota(I32, (1, n_exp), 1)

    @pl.when(i == 0)
    def _():
        cnt = cnt_ref[...].astype(I32)
        padded = ((cnt + (EXPERT_BLOCK - 1)) // EXPERT_BLOCK) * EXPERT_BLOCK
        pe = padded
        s = 1
        while s < n_exp:
            pe = pe + jnp.where(lane >= s, pltpu.roll(pe, s, 1), 0)
            s *= 2
        start_ref[...] = (pe - padded).astype(F32)
        run_ref[...] = jnp.zeros(run_ref.shape, F32)
        row = lax.broadcasted_iota(I32, (8, n_exp), 0)
        bounds_ref[...] = jnp.where(row == 0, pe - padded, pe)

    idx = idx_ref[...]
    tr = idx.shape[0]
    onehots = [lane == idx[:, k:k + 1] for k in range(TOP_K)]
    member = onehots[0]
    for k in range(1, TOP_K):
        member = member | onehots[k]
    mb = jnp.where(member, 1.0, 0.0).astype(BF16)
    r_i = lax.broadcasted_iota(I32, (tr, tr), 0)
    c_i = lax.broadcasted_iota(I32, (tr, tr), 1)
    earlier = jnp.where(r_i > c_i, 1.0, 0.0).astype(BF16)
    dest = _dot(earlier, mb) + (start_ref[0:1, :] + run_ref[0:1, :])
    slot = lax.broadcasted_iota(I32, (1, LANE), 1)
    pos8 = jnp.zeros((tr, LANE), F32)
    for k in range(TOP_K):
        pk = jnp.sum(jnp.where(onehots[k], dest, 0.0), axis=-1, keepdims=True)
        pos8 = jnp.where(slot == k, pk, pos8)
    pos_ref[...] = pos8.T[0:TOP_K, :].astype(I32)
    run_ref[...] += jnp.sum(mb.astype(F32), axis=0, keepdims=True)


def _plan(idx8, counts, tr):
    n = idx8.shape[0]
    n_exp = counts.shape[1]
    rows = lambda i: (i, 0)
    return pl.pallas_call(
        functools.partial(_plan_kernel, n_exp=n_exp), grid=(n // tr,),
        in_specs=[pl.BlockSpec((tr, LANE), rows), _const_spec((8, n_exp))],
        out_specs=[pl.BlockSpec((TOP_K, tr), lambda i: (0, i)), pl.BlockSpec((8, n_exp), lambda i: (0, 0))],
        out_shape=[jax.ShapeDtypeStruct((TOP_K, n), I32), jax.ShapeDtypeStruct((8, n_exp), I32)],
        scratch_shapes=[pltpu.VMEM((8, n_exp), F32), pltpu.VMEM((8, n_exp), F32)],
        compiler_params=_cp(("arbitrary",)), name="plan",
    )(idx8, counts)


SC_CORES = 2
SC_SUBCORES = 16
SC_AXES = ("core", "subcore")
SC_WORKERS = SC_CORES * SC_SUBCORES
COMBINE_SEGMENTS = 4
SC_WINDOW = 128


def _sc_mesh():
    return plsc.VectorSubcoreMesh(core_axis_name=SC_AXES[0], subcore_axis_name=SC_AXES[1],
                                  num_cores=SC_CORES, num_subcores=SC_SUBCORES)


def _sc_worker():
    return lax.axis_index(SC_AXES[1]) * SC_CORES + lax.axis_index(SC_AXES[0])


def _dispatch(h_p, h_s, pos_t, n_rows):
    n_p, c = h_p.shape
    n_s = h_s.shape[0]
    win_p, win_s = n_p // SC_WINDOW, n_s // SC_WINDOW
    assert n_p % (SC_WINDOW * SC_WORKERS) == 0 and n_s % SC_WINDOW == 0 and win_s <= SC_WORKERS

    @functools.partial(
        pl.kernel, out_type=jax.ShapeDtypeStruct((n_rows, c), h_p.dtype), mesh=_sc_mesh(), name="dispatch",
        scratch_types=[pltpu.VMEM((TOP_K, SC_WINDOW), I32), pltpu.VMEM((SC_WINDOW, c), h_p.dtype),
                       pltpu.SemaphoreType.DMA])
    def run(hp_hbm, hs_hbm, pos_hbm, xs_hbm, idx_v, rows_v, sem):
        wid = _sc_worker()

        def scatter_window(h_hbm, row0, tok0):
            pltpu.sync_copy(h_hbm.at[pl.ds(row0, SC_WINDOW)], rows_v)
            pltpu.sync_copy(pos_hbm.at[:, pl.ds(tok0, SC_WINDOW)], idx_v)
            copies = [pltpu.async_copy(rows_v, xs_hbm.at[idx_v.at[k]], sem) for k in range(TOP_K)]
            for cp in copies:
                cp.wait()

        def prompt_window(j, carry):
            row0 = pl.multiple_of((wid * (win_p // SC_WORKERS) + j) * SC_WINDOW, SC_WINDOW)
            scatter_window(hp_hbm, row0, row0)
            return carry
        lax.fori_loop(0, win_p // SC_WORKERS, prompt_window, 0)

        @pl.when(wid < win_s)
        def _():
            row0 = pl.multiple_of(wid * SC_WINDOW, SC_WINDOW)
            scatter_window(hs_hbm, row0, n_p + row0)

    return run(h_p, h_s, pos_t)


def _combine_gather(ys, pos_t):
    k_n = pos_t.shape[0] * pos_t.shape[1]
    c = ys.shape[1]
    win = SC_WINDOW // 2
    per_worker = k_n // (win * SC_WORKERS)
    assert k_n % (win * SC_WORKERS) == 0 and per_worker % 2 == 0
    dma = pltpu.SemaphoreType.DMA

    @functools.partial(
        pl.kernel, out_type=jax.ShapeDtypeStruct((k_n, c), ys.dtype), mesh=_sc_mesh(), name="combine_gather",
        scratch_types=[pltpu.VMEM((win,), I32), pltpu.VMEM((win,), I32),
                       pltpu.VMEM((win, c), ys.dtype), pltpu.VMEM((win, c), ys.dtype), dma, dma, dma, dma])
    def run(ys_hbm, idx_hbm, out_hbm, idx0, idx1, buf0, buf1, gsem0, gsem1, ssem0, ssem1):
        wid = _sc_worker()

        def rows(j):
            return pl.ds(pl.multiple_of((wid * per_worker + j) * win, win), win)

        def gather(j, idx_v, buf, sem):
            pltpu.sync_copy(idx_hbm.at[rows(j)], idx_v)
            pltpu.async_copy(ys_hbm.at[idx_v], buf, sem)

        def gather_wait(idx_v, buf, sem):
            pltpu.make_async_copy(ys_hbm.at[idx_v], buf, sem).wait()

        def store(j, buf, sem):
            pltpu.async_copy(buf, out_hbm.at[rows(j)], sem)

        def store_wait(buf, sem):
            pltpu.make_async_copy(buf, out_hbm.at[rows(0)], sem).wait()

        gather(0, idx0, buf0, gsem0)

        def two_windows(i, carry):
            j = 2 * i

            @pl.when(i > 0)
            def _():
                store_wait(buf1, ssem1)

            gather(j + 1, idx1, buf1, gsem1)
            gather_wait(idx0, buf0, gsem0)
            store(j, buf0, ssem0)

            @pl.when(j + 2 < per_worker)
            def _():
                store_wait(buf0, ssem0)
                gather(j + 2, idx0, buf0, gsem0)

            gather_wait(idx1, buf1, gsem1)
            store(j + 1, buf1, ssem1)
            return carry
        lax.fori_loop(0, per_worker // 2, two_windows, 0)
        store_wait(buf0, ssem0)
        store_wait(buf1, ssem1)

    return run(ys, pos_t.reshape(k_n))


MOE_RING = 8


MOE_WEIGHT_RING = 4


def _moe_kernel(st_ref, wg_ref, wu_ref, wd_ref, xs_ref, ys_ref, xbuf, ybuf, wgf, wuf, wdf, wgb, wub, wdb,
                semx, semy, semw):
    e = pl.program_id(0)
    n_e = pl.num_programs(0)

    def w_copies(ex):
        slot = ex % MOE_WEIGHT_RING
        return [pltpu.make_async_copy(src.at[ex], dst.at[slot], semw.at[slot, i])
                for i, (src, dst) in enumerate(((wg_ref, wgf), (wu_ref, wuf), (wd_ref, wdf)))]

    @pl.when(e == 0)
    def _():
        for ex in range(MOE_WEIGHT_RING - 1):
            @pl.when(ex < n_e)
            def _(ex=ex):
                for cp in w_copies(ex):
                    cp.start()

    @pl.when(e + (MOE_WEIGHT_RING - 1) < n_e)
    def _():
        for cp in w_copies(e + (MOE_WEIGHT_RING - 1)):
            cp.start()
    g0 = st_ref[e] // EXPERT_BLOCK
    nchunks = st_ref[e + 1] // EXPERT_BLOCK - g0
    ntot = st_ref[n_e] // EXPERT_BLOCK

    def rows(g):
        return pl.ds(pl.multiple_of(g * EXPERT_BLOCK, EXPERT_BLOCK), EXPERT_BLOCK)

    def x_copy(g):
        slot = g % MOE_RING
        return pltpu.make_async_copy(xs_ref.at[rows(g)], xbuf.at[slot], semx.at[slot])

    def y_copy(g):
        slot = g % MOE_RING
        return pltpu.make_async_copy(ybuf.at[slot], ys_ref.at[rows(g)], semy.at[slot])

    @pl.when(e == 0)
    def _():
        for g in range(MOE_RING - 1):
            @pl.when(g < ntot)
            def _(g=g):
                x_copy(g).start()

    for cp in w_copies(e):
        cp.wait()
    wslot = e % MOE_WEIGHT_RING
    wgb[...] = wgf[wslot].astype(BF16)
    wub[...] = wuf[wslot].astype(BF16)
    wdb[...] = wdf[wslot].astype(BF16)

    def take(g):
        @pl.when(g + (MOE_RING - 1) < ntot)
        def _():
            x_copy(g + (MOE_RING - 1)).start()

        x_copy(g).wait()

        @pl.when(g >= MOE_RING)
        def _():
            y_copy(g - MOE_RING).wait()

        return xbuf[g % MOE_RING]

    def expert_mlp(x_words):
        x = _unpack_rows(x_words).astype(BF16)
        mid = _silu(_dot(x, wgb[...])) * _dot(x, wub[...])
        return _pack_rows(_dot(mid.astype(BF16), wdb[...]))

    def put(g, y_words):
        ybuf[g % MOE_RING] = y_words
        y_copy(g).start()

    def run_chunks(g, m):
        xs = [take(g + i) for i in range(m)]
        y = expert_mlp(xs[0] if m == 1 else jnp.concatenate(xs, axis=0))
        for i in range(m):
            put(g + i, y[i * EXPERT_BLOCK:(i + 1) * EXPERT_BLOCK])

    def quad(p, carry):
        run_chunks(g0 + 4 * p, 4)
        return carry

    rest = nchunks % 4
    five = (rest == 1) & (nchunks >= 5)
    n_quads = nchunks // 4 - five.astype(I32)
    lax.fori_loop(0, n_quads, quad, 0)
    g_rest = g0 + 4 * n_quads

    @pl.when(five)
    def _():
        run_chunks(g_rest, 5)

    @pl.when(rest >= 2)
    def _():
        run_chunks(g_rest, 2)

    @pl.when((rest == 3) | (nchunks == 1))
    def _():
        run_chunks(g0 + nchunks - 1, 1)

    @pl.when(e == n_e - 1)
    def _():
        for j in range(MOE_RING):
            g = ntot - MOE_RING + j

            @pl.when(g >= 0)
            def _(g=g):
                y_copy(g).wait()


def _moe(starts, xs, wg, wu, wd):
    n_rows, c = xs.shape
    n_exp, d, f = wg.shape
    grid_spec = pltpu.PrefetchScalarGridSpec(
        num_scalar_prefetch=1, grid=(n_exp,),
        in_specs=[pl.BlockSpec(memory_space=pl.ANY)] * 4,
        out_specs=pl.BlockSpec(memory_space=pl.ANY),
        scratch_shapes=[pltpu.VMEM((MOE_RING, EXPERT_BLOCK, c), I32), pltpu.VMEM((MOE_RING, EXPERT_BLOCK, c), I32),
                        pltpu.VMEM((MOE_WEIGHT_RING, d, f), F32), pltpu.VMEM((MOE_WEIGHT_RING, d, f), F32),
                        pltpu.VMEM((MOE_WEIGHT_RING, f, d), F32),
                        pltpu.VMEM((d, f), BF16), pltpu.VMEM((d, f), BF16), pltpu.VMEM((f, d), BF16),
                        pltpu.SemaphoreType.DMA((MOE_RING,)), pltpu.SemaphoreType.DMA((MOE_RING,)),
                        pltpu.SemaphoreType.DMA((MOE_WEIGHT_RING, 3))])
    return pl.pallas_call(
        _moe_kernel, grid_spec=grid_spec, out_shape=jax.ShapeDtypeStruct((n_rows, c), I32),
        compiler_params=_cp(("arbitrary",)), name="moe",
    )(starts, wg, wu, wd, xs)


def _final_kernel(yg_ref, w8_ref, x1_ref, h2_ref, mod_ref, wsg_ref, wsu_ref, wsd_ref, nf_ref, *rest, d):
    y_ref = rest[-1]
    hb = _unpack_rows(h2_ref[...]).astype(BF16)
    shared = _dot((_silu(_dot(hb, wsg_ref[...])) * _dot(hb, wsu_ref[...])).astype(BF16), wsd_ref[...])
    w8 = w8_ref[...]
    routed = _unpack_rows(yg_ref[0]) * w8[:, 0:1]
    for k in range(1, TOP_K):
        routed = routed + _unpack_rows(yg_ref[k]) * w8[:, k:k + 1]
    g2 = mod_ref[...][:, (N_MOD - 1) * d:N_MOD * d]
    x2 = x1_ref[...] + g2 * (routed + shared)
    y_ref[...] = _rms(x2, nf_ref[...])


def _final(yg3, w8, x1, h2, mod, mod_spec, wts, nf, tf, n_tiles, yg_tile0, w_tile0, x_tile0, out_shape, out_spec,
           y_prev=None):
    d = x1.shape[1]
    rows = lambda i: (x_tile0 + i, 0)
    in_specs = [pl.BlockSpec((TOP_K, tf, d // 2), lambda i: (0, yg_tile0 + i, 0)),
                pl.BlockSpec((tf, LANE), lambda i: (w_tile0 + i, 0)),
                pl.BlockSpec((tf, d), rows), pl.BlockSpec((tf, d // 2), rows), mod_spec]
    in_specs += [_const_spec(w.shape) for w in wts]
    in_specs += [_const_spec(nf.shape)]
    args = [yg3, w8, x1, h2, mod, *wts, nf]
    aliases = {}
    if y_prev is not None:
        aliases = {len(args): 0}
        in_specs.append(pl.BlockSpec(memory_space=pl.ANY))
        args.append(y_prev)
    return pl.pallas_call(
        functools.partial(_final_kernel, d=d), grid=(n_tiles,), in_specs=in_specs,
        out_specs=out_spec, out_shape=out_shape, input_output_aliases=aliases,
        compiler_params=_cp(("arbitrary",)), name="final",
    )(*args)


def kernel(x_prompt, x_sample, c_prompt, c_sample, state_pool, state_gla, w_ada, b_ada, norm1, w_in,
           pool_w, pool_scale, gla_w_alpha, gla_b_alpha, gla_norm, w_branch_a, w_branch_b, w_out, norm2,
           w_router, router_bias, w_exp_gate, w_exp_up, w_exp_down, w_sh_gate, w_sh_up, w_sh_down, norm_f):
    depth = w_in.shape[0]
    assert depth == 1, "single-layer trunk"
    bp, seq, d = x_prompt.shape
    bs, t_s, _ = x_sample.shape
    _, _, past, pw = state_pool.shape
    _, _, heads, dk, dv = state_gla.shape
    rank = gla_w_alpha.shape[1]
    n_exp = w_router.shape[2]
    assert past == POOL_HIST - 1 and pw == len(POOL_WINDOWS) * LANE and dv == LANE and rank <= LANE
    dm = _Dims(d, pw, heads, dk, dv)
    n_p, n_s = bp * seq, bs * t_s
    n_all = n_p + n_s

    off = [0]
    for sz in (pw, dm.hk, dm.hk, dm.gw, dm.gw, rank, d, d):
        off.append(off[-1] + sz)
    wi = w_in[0]
    win = (wi[:, :off[5]].astype(BF16), wi[:, off[6]:].astype(BF16),
           jnp.concatenate([wi[:, off[5]:off[6]], jnp.zeros((d, LANE - rank), F32)], axis=1).astype(BF16))
    assert off[5] + 2 * d + LANE == dm.zw
    wal = jnp.concatenate([gla_w_alpha[0], jnp.zeros((LANE - rank, dm.hk), F32)], axis=0).astype(BF16)
    bal = gla_b_alpha[0].reshape(1, dm.hk)
    wr = w_router[0]
    wrh = wr.astype(BF16)
    wrl = (wr - wrh.astype(F32)).astype(BF16)
    post_w = (pool_w[0].astype(BF16), pool_scale[0].reshape(1, pw), gla_norm[0].reshape(1, dm.gw),
              w_branch_a[0].astype(BF16), w_branch_b[0].astype(BF16), w_out[0].astype(BF16),
              norm2[0].reshape(1, d), wrh, wrl)
    n1 = norm1[0].reshape(1, d)

    mod = _ada(jnp.concatenate([c_prompt, c_sample], axis=0), w_ada[0], b_ada[0])
    mod_p = mod[:bp].reshape(bp, 1, N_MOD * d)
    mod_s = mod[bp:]

    x1_p, h2_p, lg_p, npool_p, ngla_p = _mix_prompt(
        dm, x_prompt.reshape(n_p, d), mod_p, (n1,) + win + (wal, bal) + post_w, bp, seq, n_exp)

    xs2d = x_sample.reshape(bs, t_s * d)
    z_s = _sample_in(dm, xs2d, mod_s, n1, win, bs, t_s)
    mixed_s, o_s, npool_s, ngla_s = _sample_state(
        dm, z_s, state_pool[0].reshape(bs, past * pw), state_gla[0].reshape(bs, dm.hk, dv), wal, bal, bs, t_s)
    x1_s, h2_s, lg_s = _sample_post(dm, xs2d, mod_s, z_s, mixed_s, o_s, post_w, bs, t_s, n_exp)

    tr = 512 if (n_p % 512 == 0 and n_s % 512 == 0) else LANE
    assert n_p % tr == 0 and n_s % tr == 0 and n_p % bs == 0
    idx8, w8, counts = _route(lg_p, lg_s, router_bias[0], tr)
    nblk = (n_all * TOP_K) // EXPERT_BLOCK + n_exp
    pos_t, bounds = _plan(idx8, counts, tr)
    starts = jnp.concatenate([bounds[0], bounds[1, n_exp - 1:]])

    xs = _dispatch(h2_p, h2_s, pos_t, nblk * EXPERT_BLOCK)
    ys = _moe(starts, xs, w_exp_gate[0], w_exp_up[0], w_exp_down[0])

    sh_w = (w_sh_gate[0].astype(BF16), w_sh_up[0].astype(BF16), w_sh_down[0].astype(BF16))
    nf = norm_f.reshape(1, d)
    tf = 256 if seq % 256 == 0 else seq
    n_tp = seq // tf
    n_seg = COMBINE_SEGMENTS if (n_p // tf) % COMBINE_SEGMENTS == 0 else 1
    seg = n_p // n_seg
    y_p = None
    for s in range(n_seg):
        t0, t1 = s * seg, ((s + 1) * seg if s + 1 < n_seg else n_all)
        yg3 = _combine_gather(ys, pos_t[:, t0:t1]).reshape(TOP_K, t1 - t0, d // 2)
        y_p = _final(
            yg3, w8, x1_p, h2_p, mod_p,
            pl.BlockSpec((None, 1, N_MOD * d), lambda i, s=s: ((s * (seg // tf) + i) // n_tp, 0, 0)), sh_w, nf,
            tf, seg // tf, 0, t0 // tf, t0 // tf,
            jax.ShapeDtypeStruct((n_p, d), F32), pl.BlockSpec((tf, d), lambda i, s=s: (s * (seg // tf) + i, 0)),
            y_prev=y_p)
    y_s = _final(
        yg3, w8, x1_s, h2_s, mod_s,
        _const_spec(mod_s.shape), sh_w, nf, bs, t_s, seg // bs, n_p // bs, 0,
        jax.ShapeDtypeStruct((bs, t_s * d), F32), pl.BlockSpec((bs, d), lambda i: (0, i)))

    return (y_p.reshape(bp, seq, d), y_s.reshape(bs, t_s, d),
            npool_p.reshape(depth, bp, past, pw), ngla_p.reshape(depth, bp, heads, dk, dv),
            npool_s.reshape(depth, bs, past, pw), ngla_s.reshape(depth, bs, heads, dk, dv))
```

```python
import functools

import jax
import jax.numpy as jnp
from jax import lax
from jax.experimental import pallas as pl
from jax.experimental.pallas import tpu as pltpu
from jax.experimental.pallas import tpu_sc as plsc

F32 = jnp.float32
BF16 = jnp.bfloat16
I32 = jnp.int32

EPS = 1e-6
N_MOD = 6
POOL_WINDOWS = (2, 4, 8, 16)
POOL_HIST = 16
GLA_GATE_TEMP = 16.0
TOP_K = 8
ROUTED_SCALE = 2.5
PAST_LEN = 16384
EXPERT_BLOCK = 128
LANE = 128
GLA_FAST_MAX_DECAY = 40.0
PROJ_PIECE = 512
PROJ_LEAD = 2
VMEM_LIMIT = 56 * 1024 * 1024


def _cp(sem, vmem=VMEM_LIMIT):
    return pltpu.CompilerParams(dimension_semantics=sem, vmem_limit_bytes=vmem)


def _rms(x, g):
    return x * lax.rsqrt(jnp.mean(x * x, axis=-1, keepdims=True) + EPS) * g


def _silu(x):
    return x * jax.nn.sigmoid(x)


def _dot(a, b):
    return jnp.dot(a, b, preferred_element_type=F32)


def _split3(a):
    a0 = a.astype(BF16)
    r1 = a - a0.astype(F32)
    a1 = r1.astype(BF16)
    a2 = (r1 - a1.astype(F32)).astype(BF16)
    return a0, a1, a2


def _mod_parts(mod, d):
    return [mod[:, i * d:(i + 1) * d] for i in range(N_MOD)]


_HI16 = -65536


def _pack_rows(v):
    c = v.shape[1] // 2
    lo = lax.bitcast_convert_type(v[:, :c].astype(BF16).astype(F32), I32)
    hi = lax.bitcast_convert_type(v[:, c:].astype(BF16).astype(F32), I32)
    return (hi & _HI16) | lax.shift_right_logical(lo, 16)


def _unpack_rows(w):
    lo = lax.bitcast_convert_type(lax.shift_left(w, 16), F32)
    hi = lax.bitcast_convert_type(w & _HI16, F32)
    return jnp.concatenate([lo, hi], axis=1)


def _ada_kernel(c_ref, w_ref, b_ref, o_ref):
    a = _silu(c_ref[...]).astype(BF16)
    o_ref[...] = _dot(a, w_ref[...].astype(BF16)) + b_ref[...]


def _ada(c_all, w_ada, b_ada):
    n, d = c_all.shape
    cols = w_ada.shape[1]
    bc = d
    return pl.pallas_call(
        _ada_kernel,
        grid=(cols // bc,),
        in_specs=[pl.BlockSpec((n, d), lambda i: (0, 0)),
                  pl.BlockSpec((d, bc), lambda i: (0, i)),
                  pl.BlockSpec((1, bc), lambda i: (0, i))],
        out_specs=pl.BlockSpec((n, bc), lambda i: (0, i)),
        out_shape=jax.ShapeDtypeStruct((n, cols), F32),
        compiler_params=_cp(("arbitrary",)),
        name="ada",
    )(c_all, w_ada, b_ada.reshape(1, cols))


class _Dims:
    def __init__(self, d, pw, heads, dk, dv):
        self.d, self.pw, self.h, self.dk, self.dv = d, pw, heads, dk, dv
        self.hk, self.gw = heads * dk, heads * dv
        self.q0 = pw
        self.k0 = self.q0 + self.hk
        self.v0 = self.k0 + self.hk
        self.go0 = self.v0 + self.gw
        self.ga0 = self.go0 + self.gw
        self.gb0 = self.ga0 + d
        self.al0 = self.gb0 + d
        self.zw = self.al0 + LANE


def _log_decay(z_al, wal_ref, bal_ref):
    xal = _dot(z_al.astype(BF16), wal_ref[...]) + bal_ref[...]
    return jax.nn.log_sigmoid(xal) / GLA_GATE_TEMP


def _pool_project(mixed_groups, pw_ref, ps_ref):
    ys = [_dot(m.astype(BF16), pw_ref[g]) for g, m in enumerate(mixed_groups)]
    return jnp.concatenate(ys, axis=1) * ps_ref[...]


def _post_mix(dm, x, mod, ya, o, go, ga, gb, gn_ref, wa_ref, wb_ref, wo_ref, n2_ref, wrh_ref, wrl_ref,
              filler=lambda: None):
    _, _, g1, sh2, sc2, _ = _mod_parts(mod, dm.d)
    parts = []
    for h in range(dm.h):
        oh = o[:, h * dm.dv:(h + 1) * dm.dv]
        parts.append(oh * lax.rsqrt(jnp.mean(oh * oh, axis=-1, keepdims=True) + EPS))
    yb = jnp.concatenate(parts, axis=1) * gn_ref[...] * _silu(go)
    filler()
    m = (jax.nn.sigmoid(ga) * _dot(ya.astype(BF16), wa_ref[...])
         + jax.nn.sigmoid(gb) * _dot(yb.astype(BF16), wb_ref[...]))
    filler()
    x1 = x + g1 * _dot(m.astype(BF16), wo_ref[...])
    h2 = _rms(x1, n2_ref[...]) * (1 + sc2) + sh2
    hi = h2.astype(BF16)
    lo = (h2 - hi.astype(F32)).astype(BF16)
    filler()
    logits = _dot(hi, wrh_ref[...]) + (_dot(hi, wrl_ref[...]) + _dot(lo, wrh_ref[...]))
    filler()
    return x1, _pack_rows(h2), logits


def _in_project(dm, x, mod, n1_ref, win_refs, z_ref):
    h = (_rms(x, n1_ref[...]) * (1 + mod[:, dm.d:2 * dm.d]) + mod[:, 0:dm.d]).astype(BF16)
    starts = [0]
    for w_ref in win_refs:
        starts.append(starts[-1] + w_ref.shape[1])
    for i in (0, 2, 1):
        z_ref[:, starts[i]:starts[i + 1]] = _dot(h, win_refs[i][...])


def _mixp_kernel(x_ref, mod_ref, n1_ref, wia_ref, wig_ref, wil_ref, wal_ref, bal_ref, pw_ref, ps_ref, gn_ref,
                 wa_ref, wb_ref, wo_ref, n2_ref, wrh_ref, wrl_ref,
                 x1_ref, h2_ref, lg_ref, np_ref, ng_ref,
                 za_ref, zb_ref, xk_ref, mk_ref, hn_ref, uext_ref, s_ref, b_ref, oi_ref, *, dm, tt, n_t):
    n = pl.program_id(0)
    j = jnp.maximum(n - 1, 0) % n_t

    @pl.when(n == 0)
    def _():
        za_ref[...] = jnp.zeros(za_ref.shape, F32)
        zb_ref[...] = jnp.zeros(zb_ref.shape, F32)
        xk_ref[...] = jnp.zeros(xk_ref.shape, F32)
        mk_ref[...] = jnp.zeros(mk_ref.shape, F32)

    @pl.when(j == 0)
    def _():
        uext_ref[0:POOL_HIST, :] = jnp.zeros((POOL_HIST, dm.pw), F32)
        s_ref[...] = jnp.zeros(s_ref.shape, F32)

    @pl.when(n % 2 == 0)
    def _():
        _mixp_region(za_ref, zb_ref, x_ref, mod_ref, n1_ref, wia_ref, wig_ref, wil_ref, wal_ref, bal_ref, pw_ref,
                     ps_ref, gn_ref, wa_ref, wb_ref, wo_ref, n2_ref, wrh_ref, wrl_ref, x1_ref, h2_ref, lg_ref,
                     xk_ref, mk_ref, hn_ref, uext_ref, s_ref, b_ref, oi_ref, dm=dm, tt=tt, j=j)

    @pl.when(n % 2 == 1)
    def _():
        _mixp_region(zb_ref, za_ref, x_ref, mod_ref, n1_ref, wia_ref, wig_ref, wil_ref, wal_ref, bal_ref, pw_ref,
                     ps_ref, gn_ref, wa_ref, wb_ref, wo_ref, n2_ref, wrh_ref, wrl_ref, x1_ref, h2_ref, lg_ref,
                     xk_ref, mk_ref, hn_ref, uext_ref, s_ref, b_ref, oi_ref, dm=dm, tt=tt, j=j)

    xk_ref[...] = x_ref[...]
    mk_ref[...] = mod_ref[...]

    @pl.when(j == n_t - 1)
    def _():
        np_ref[...] = uext_ref[tt + 1:tt + POOL_HIST, :]
        ng_ref[...] = s_ref[...]


def _mixp_region(zp_ref, z_ref, x_ref, mod_ref, n1_ref, wia_ref, wig_ref, wil_ref, wal_ref, bal_ref, pw_ref,
                 ps_ref, gn_ref, wa_ref, wb_ref, wo_ref, n2_ref, wrh_ref, wrl_ref, x1_ref, h2_ref, lg_ref,
                 xk_ref, mk_ref, hn_ref, uext_ref, s_ref, b_ref, oi_ref, *, dm, tt, j):
    scale = dm.dk ** -0.5
    lane = lax.broadcasted_iota(I32, (1, dm.hk), 1)
    head_masks = [(lane >= h * dm.dk) & (lane < (h + 1) * dm.dk) for h in range(dm.h)]
    causal = lax.broadcasted_iota(I32, (tt, tt), 0) >= lax.broadcasted_iota(I32, (tt, tt), 1)

    def pool_branch():
        pos = j * tt + lax.broadcasted_iota(I32, (tt, 1), 0)
        gw = dm.pw // len(POOL_WINDOWS)
        mixed = []
        for g, w in enumerate(POOL_WINDOWS):
            c0 = g * gw
            cur = uext_ref[POOL_HIST:POOL_HIST + tt, c0:c0 + gw]
            acc = cur
            for i in range(1, w):
                acc = acc + uext_ref[POOL_HIST - i:POOL_HIST - i + tt, c0:c0 + gw]
            cnt = jnp.minimum(w, pos + 1).astype(F32)
            mixed.append(acc / cnt - cur)
        return _pool_project(mixed, pw_ref, ps_ref)

    def scaled_queries(bc):
        return (z_ref[:, dm.q0:dm.q0 + dm.hk] * scale) * jnp.exp(bc)

    def inter_chunk(qe):
        s_bf = s_ref[...].astype(BF16)
        return jnp.concatenate(
            [_dot(jnp.where(head_masks[h], qe, 0.0).astype(BF16), s_bf) for h in range(dm.h)], axis=1)

    def finish(ya, o, filler=lambda: None):
        x1, h2, logits = _post_mix(
            dm, xk_ref[...], mk_ref[...], ya, o, z_ref[:, dm.go0:dm.go0 + dm.gw],
            z_ref[:, dm.ga0:dm.ga0 + dm.d], z_ref[:, dm.gb0:dm.gb0 + dm.d],
            gn_ref, wa_ref, wb_ref, wo_ref, n2_ref, wrh_ref, wrl_ref, filler)
        x1_ref[...] = x1
        h2_ref[...] = h2
        lg_ref[...] = logits

    mod_in = mod_ref[...]
    hn_ref[...] = (_rms(x_ref[...], n1_ref[...]) * (1 + mod_in[:, dm.d:2 * dm.d]) + mod_in[:, 0:dm.d]).astype(BF16)
    pieces = iter([(w_ref, a, min(a + PROJ_PIECE, w_ref.shape[1]), c0)
                   for w_ref, c0 in ((wia_ref, 0), (wil_ref, dm.al0), (wig_ref, dm.ga0))
                   for a in range(0, w_ref.shape[1], PROJ_PIECE)])

    def project_piece():
        piece = next(pieces, None)
        if piece is not None:
            w_ref, a, b, c0 = piece
            zp_ref[:, c0 + a:c0 + b] = _dot(hn_ref[...], w_ref[:, a:b])
        return piece is not None

    for _ in range(PROJ_LEAD):
        project_piece()
    uext_ref[POOL_HIST:POOL_HIST + tt, :] = z_ref[:, 0:dm.pw]
    ya = pool_branch()
    project_piece()
    la = _log_decay(z_ref[:, dm.al0:dm.al0 + LANE], wal_ref, bal_ref)
    tri = jnp.where(causal, 1.0, 0.0).astype(BF16)
    a0, a1, a2 = _split3(la)
    bc = _dot(tri, a0) + (_dot(tri, a1) + _dot(tri, a2))
    b_ref[...] = bc
    project_piece()
    qe = scaled_queries(bc)
    k = z_ref[:, dm.k0:dm.k0 + dm.hk]
    vb = z_ref[:, dm.v0:dm.v0 + dm.gw].astype(BF16)
    blast = bc[tt - 1:tt, :]
    fast = jnp.max(-blast) <= GLA_FAST_MAX_DECAY
    ke = (k * jnp.exp(-bc)).astype(BF16)
    intra = []
    for h in range(dm.h):
        qh = jnp.where(head_masks[h], qe, 0.0).astype(BF16)
        sc = lax.dot_general(qh, ke, (((1,), (1,)), ((), ())), preferred_element_type=F32)
        sc = jnp.where(causal, sc, 0.0).astype(BF16)
        intra.append(_dot(sc, vb[:, h * dm.dv:(h + 1) * dm.dv]))
    project_piece()
    finish(ya, inter_chunk(qe) + jnp.concatenate(intra, axis=1), project_piece)
    while project_piece():
        pass

    @pl.when(jnp.logical_not(fast))
    def _():
        rows = lax.broadcasted_iota(I32, (tt, 1), 0)
        for h in range(dm.h):
            def body(tb, carry, h=h):
                base = pl.multiple_of(tb * 8, 8)
                q8 = z_ref[pl.ds(base, 8), dm.q0:dm.q0 + dm.hk] * scale
                b8 = b_ref[pl.ds(base, 8), :]
                vh = z_ref[:, dm.v0 + h * dm.dv:dm.v0 + (h + 1) * dm.dv]
                out_rows = []
                for r in range(8):
                    keep = (rows <= base + r) & head_masks[h]
                    dec = jnp.exp(jnp.where(keep, b8[r:r + 1, :] - b_ref[...], -jnp.inf))
                    wgt = (q8[r:r + 1, :] * dec) * z_ref[:, dm.k0:dm.k0 + dm.hk]
                    s_col = jnp.sum(wgt, axis=-1, keepdims=True)
                    out_rows.append(jnp.sum(s_col * vh, axis=0, keepdims=True))
                oi_ref[pl.ds(base, 8), h * dm.dv:(h + 1) * dm.dv] = jnp.concatenate(out_rows, axis=0)
                return carry
            lax.fori_loop(0, tt // 8, body, 0)
        finish(pool_branch(), inter_chunk(scaled_queries(b_ref[...])) + oi_ref[...])

    bc = b_ref[...]
    blast = bc[tt - 1:tt, :]
    kd = z_ref[:, dm.k0:dm.k0 + dm.hk] * jnp.exp(blast - bc)
    u_all = _dot(kd.T.astype(BF16), z_ref[:, dm.v0:dm.v0 + dm.gw].astype(BF16))
    dec_col = jnp.broadcast_to(jnp.exp(blast), (dm.hk, dm.hk)).T[:, 0:dm.dv]
    upd = jnp.concatenate(
        [u_all[h * dm.dk:(h + 1) * dm.dk, h * dm.dv:(h + 1) * dm.dv] for h in range(dm.h)], axis=0)
    s_ref[...] = dec_col * s_ref[...] + upd
    uext_ref[0:POOL_HIST, :] = uext_ref[tt:tt + POOL_HIST, :]


def _const_spec(shape):
    nd = len(shape)
    return pl.BlockSpec(shape, lambda *_: (0,) * nd, pipeline_mode=pl.Buffered(1))


def _mix_prompt(dm, x2d, mod3, wts, bsz, seq, n_exp):
    tt = 256 if seq % 256 == 0 else seq
    n_t = seq // tt
    n = bsz * seq
    d = dm.d
    kern = functools.partial(_mixp_kernel, dm=dm, tt=tt, n_t=n_t)
    tiles = bsz * n_t
    proj = lambda i: jnp.minimum(i, tiles - 1)
    fin = lambda i: jnp.maximum(i - 1, 0)
    row = lambda i: (fin(i), 0)
    in_specs = [pl.BlockSpec((tt, d), lambda i: (proj(i), 0)),
                pl.BlockSpec((None, 1, N_MOD * d), lambda i: (proj(i) // n_t, 0, 0))]
    in_specs += [_const_spec(w.shape) for w in wts]
    out_specs = [pl.BlockSpec((tt, d), row), pl.BlockSpec((tt, d // 2), row), pl.BlockSpec((tt, n_exp), row),
                 pl.BlockSpec((None, POOL_HIST - 1, dm.pw), lambda i: (fin(i) // n_t, 0, 0)),
                 pl.BlockSpec((None, dm.hk, dm.dv), lambda i: (fin(i) // n_t, 0, 0))]
    out_shape = [jax.ShapeDtypeStruct((n, d), F32), jax.ShapeDtypeStruct((n, d // 2), I32),
                 jax.ShapeDtypeStruct((n, n_exp), F32),
                 jax.ShapeDtypeStruct((bsz, POOL_HIST - 1, dm.pw), F32),
                 jax.ShapeDtypeStruct((bsz, dm.hk, dm.dv), F32)]
    scratch = [pltpu.VMEM((tt, dm.zw), F32), pltpu.VMEM((tt, dm.zw), F32),
               pltpu.VMEM((tt, d), F32), pltpu.VMEM((1, N_MOD * d), F32), pltpu.VMEM((tt, d), BF16),
               pltpu.VMEM((tt + POOL_HIST, dm.pw), F32),
               pltpu.VMEM((dm.hk, dm.dv), F32), pltpu.VMEM((tt, dm.hk), F32), pltpu.VMEM((tt, dm.gw), F32)]
    return pl.pallas_call(
        kern, grid=(tiles + 1,), in_specs=in_specs, out_specs=out_specs, out_shape=out_shape,
        scratch_shapes=scratch, compiler_params=_cp(("arbitrary",)), name="mixp",
    )(x2d, mod3, *wts)


def _sin_kernel(x_ref, mod_ref, n1_ref, wia_ref, wig_ref, wil_ref, z_ref, *, dm):
    _in_project(dm, x_ref[...], mod_ref[...], n1_ref, (wia_ref, wig_ref, wil_ref), z_ref)


def _sample_in(dm, xs2d, mod_s, n1, win, bs, t_s):
    d = dm.d
    return pl.pallas_call(
        functools.partial(_sin_kernel, dm=dm),
        grid=(t_s,),
        in_specs=[pl.BlockSpec((bs, d), lambda t: (0, t)), _const_spec(mod_s.shape), _const_spec(n1.shape)]
        + [_const_spec(w.shape) for w in win],
        out_specs=pl.BlockSpec((bs, dm.zw), lambda t: (t, 0)),
        out_shape=jax.ShapeDtypeStruct((t_s * bs, dm.zw), F32),
        compiler_params=_cp(("arbitrary",)), name="sample_in",
    )(xs2d, mod_s, n1, *win)


def _sst_kernel(z_ref, pool_ref, gs_ref, wal_ref, bal_ref,
                mixed_ref, o_ref, npool_ref, ngs_ref,
                qt_ref, kt_ref, dt_ref, vt_ref, ot_ref, *, dm, bs, t_s, ch, pos0):
    i = pl.program_id(0)
    n_i = pl.num_programs(0)
    scale = dm.dk ** -0.5
    past = POOL_HIST - 1

    @pl.when(i == 0)
    def _():
        for t in range(t_s):
            r0 = t * bs
            la = _log_decay(z_ref[r0:r0 + bs, dm.al0:dm.al0 + LANE], wal_ref, bal_ref)
            dt_ref[t] = jnp.exp(la).T
            qt_ref[t] = (z_ref[r0:r0 + bs, dm.q0:dm.q0 + dm.hk] * scale).T
            kt_ref[t] = z_ref[r0:r0 + bs, dm.k0:dm.k0 + dm.hk].T
            vt_ref[t] = z_ref[r0:r0 + bs, dm.v0:dm.v0 + dm.gw].T
        ot_ref[...] = jnp.zeros(ot_ref.shape, F32)

        def item(s, c0, c1):
            if s >= 0:
                return z_ref[s * bs:(s + 1) * bs, c0:c1]
            return pool_ref[:, past + s, c0:c1]

        gw = dm.pw // len(POOL_WINDOWS)
        for t in range(t_s):
            for g, w in enumerate(POOL_WINDOWS):
                c0, c1 = g * gw, (g + 1) * gw
                cur = item(t, c0, c1)
                acc = cur
                for r in range(1, w):
                    acc = acc + item(t - r, c0, c1)
                cnt = float(min(w, pos0 + t + 1))
                mixed_ref[t * bs:(t + 1) * bs, c0:c1] = acc / cnt - cur
        for r in range(past):
            npool_ref[:, r, :] = item(r - past + t_s, 0, dm.pw)

    per_head = dm.dk // ch
    hoff = pl.multiple_of((i // per_head) * dm.dv, dm.dv)
    for p in range(ch):
        hd = i * ch + p
        st = gs_ref[:, p, :].T
        for t in range(t_s):
            st = dt_ref[t, pl.ds(hd, 1), :] * st + kt_ref[t, pl.ds(hd, 1), :] * vt_ref[t, pl.ds(hoff, dm.dv), :]
            ot_ref[t, pl.ds(hoff, dm.dv), :] += qt_ref[t, pl.ds(hd, 1), :] * st
        ngs_ref[:, p, :] = st.T

    @pl.when(i == n_i - 1)
    def _():
        for t in range(t_s):
            for h in range(dm.h):
                o_ref[t * bs:(t + 1) * bs, h * dm.dv:(h + 1) * dm.dv] = ot_ref[t, h * dm.dv:(h + 1) * dm.dv, :].T


def _sample_state(dm, z_s, pool3, gs3, wal, bal, bs, t_s):
    ch = 16
    n_i = dm.hk // ch
    past = POOL_HIST - 1
    kern = functools.partial(_sst_kernel, dm=dm, bs=bs, t_s=t_s, ch=ch, pos0=PAST_LEN)
    full = lambda shape: pl.BlockSpec(shape, lambda i: (0,) * len(shape))
    return pl.pallas_call(
        kern, grid=(n_i,),
        in_specs=[_const_spec(z_s.shape), _const_spec(pool3.shape),
                  pl.BlockSpec((bs, ch, dm.dv), lambda i: (0, i, 0)),
                  _const_spec(wal.shape), _const_spec(bal.shape)],
        out_specs=[full((t_s * bs, dm.pw)), full((t_s * bs, dm.gw)), full(pool3.shape),
                   pl.BlockSpec((bs, ch, dm.dv), lambda i: (0, i, 0))],
        out_shape=[jax.ShapeDtypeStruct((t_s * bs, dm.pw), F32), jax.ShapeDtypeStruct((t_s * bs, dm.gw), F32),
                   jax.ShapeDtypeStruct(pool3.shape, F32), jax.ShapeDtypeStruct(gs3.shape, F32)],
        scratch_shapes=[pltpu.VMEM((t_s, dm.hk, bs), F32), pltpu.VMEM((t_s, dm.hk, bs), F32),
                        pltpu.VMEM((t_s, dm.hk, bs), F32), pltpu.VMEM((t_s, dm.gw, bs), F32),
                        pltpu.VMEM((t_s, dm.gw, bs), F32)],
        compiler_params=_cp(("arbitrary",)), name="sample_state",
    )(z_s, pool3, gs3, wal, bal)


def _spost_kernel(x_ref, mod_ref, z_ref, mixed_ref, o_ref, pw_ref, ps_ref, gn_ref,
                  wa_ref, wb_ref, wo_ref, n2_ref, wrh_ref, wrl_ref,
                  x1_ref, h2_ref, lg_ref, *, dm):
    gw = dm.pw // len(POOL_WINDOWS)
    ya = _pool_project([mixed_ref[:, g * gw:(g + 1) * gw] for g in range(len(POOL_WINDOWS))], pw_ref, ps_ref)
    x1, h2, logits = _post_mix(
        dm, x_ref[...], mod_ref[...], ya, o_ref[...],
        z_ref[:, dm.go0:dm.go0 + dm.gw], z_ref[:, dm.ga0:dm.ga0 + dm.d], z_ref[:, dm.gb0:dm.gb0 + dm.d],
        gn_ref, wa_ref, wb_ref, wo_ref, n2_ref, wrh_ref, wrl_ref)
    x1_ref[...] = x1
    h2_ref[...] = h2
    lg_ref[...] = logits


def _sample_post(dm, xs2d, mod_s, z_s, mixed_s, o_s, wts, bs, t_s, n_exp):
    d = dm.d
    rows = lambda t: (t, 0)
    in_specs = [pl.BlockSpec((bs, d), lambda t: (0, t)), _const_spec(mod_s.shape),
                pl.BlockSpec((bs, dm.zw), rows), pl.BlockSpec((bs, dm.pw), rows), pl.BlockSpec((bs, dm.gw), rows)]
    in_specs += [_const_spec(w.shape) for w in wts]
    return pl.pallas_call(
        functools.partial(_spost_kernel, dm=dm), grid=(t_s,), in_specs=in_specs,
        out_specs=[pl.BlockSpec((bs, d), rows), pl.BlockSpec((bs, d // 2), rows), pl.BlockSpec((bs, n_exp), rows)],
        out_shape=[jax.ShapeDtypeStruct((t_s * bs, d), F32), jax.ShapeDtypeStruct((t_s * bs, d // 2), I32),
                   jax.ShapeDtypeStruct((t_s * bs, n_exp), F32)],
        compiler_params=_cp(("arbitrary",)), name="sample_post",
    )(xs2d, mod_s, z_s, mixed_s, o_s, *wts)


def _route_kernel(lgp_ref, lgs_ref, bias_ref, idx_ref, w_ref, cnt_ref, *, n_exp, tiles_p):
    i = pl.program_id(0)
    s = jax.nn.sigmoid(jnp.where(i < tiles_p, lgp_ref[...], lgs_ref[...]))
    cur = s + bias_ref[...]
    tr = s.shape[0]
    lane = lax.broadcasted_iota(I32, (1, n_exp), 1).astype(F32)
    slot = lax.broadcasted_iota(I32, (1, LANE), 1)
    idx8 = jnp.zeros((tr, LANE), F32)
    w8 = jnp.zeros((tr, LANE), F32)
    wsum = jnp.zeros((tr, 1), F32)
    picked = jnp.zeros((tr, n_exp), F32)
    for k in range(TOP_K):
        m = jnp.max(cur, axis=-1, keepdims=True)
        am = jnp.min(jnp.where(cur == m, lane, float(n_exp)), axis=-1, keepdims=True)
        oh = lane == am
        sk = jnp.sum(jnp.where(oh, s, 0.0), axis=-1, keepdims=True)
        cur = jnp.where(oh, -jnp.inf, cur)
        idx8 = jnp.where(slot == k, am, idx8)
        w8 = jnp.where(slot == k, sk, w8)
        wsum = wsum + sk
        picked = picked + jnp.where(oh, 1.0, 0.0)
    idx_ref[...] = idx8.astype(I32)
    w_ref[...] = w8 / wsum * ROUTED_SCALE

    @pl.when(i == 0)
    def _():
        cnt_ref[...] = jnp.zeros(cnt_ref.shape, F32)

    cnt_ref[...] += jnp.sum(picked, axis=0, keepdims=True)


def _two_source_specs(block, tiles_p, tiles_s):
    return [pl.BlockSpec(block, lambda i: (jnp.minimum(i, tiles_p - 1), 0)),
            pl.BlockSpec(block, lambda i: (jnp.clip(i - tiles_p, 0, tiles_s - 1), 0))]


def _route(lg_p, lg_s, bias, tr):
    n_exp = lg_p.shape[1]
    tiles_p, tiles_s = lg_p.shape[0] // tr, lg_s.shape[0] // tr
    n = (tiles_p + tiles_s) * tr
    rows = lambda i: (i, 0)
    return pl.pallas_call(
        functools.partial(_route_kernel, n_exp=n_exp, tiles_p=tiles_p), grid=(tiles_p + tiles_s,),
        in_specs=_two_source_specs((tr, n_exp), tiles_p, tiles_s) + [_const_spec((1, n_exp))],
        out_specs=[pl.BlockSpec((tr, LANE), rows), pl.BlockSpec((tr, LANE), rows),
                   pl.BlockSpec((8, n_exp), lambda i: (0, 0))],
        out_shape=[jax.ShapeDtypeStruct((n, LANE), I32), jax.ShapeDtypeStruct((n, LANE), F32),
                   jax.ShapeDtypeStruct((8, n_exp), F32)],
        compiler_params=_cp(("arbitrary",)), name="route",
    )(lg_p, lg_s, bias.reshape(1, n_exp))


def _plan_kernel(idx_ref, cnt_ref, pos_ref, bounds_ref, start_ref, run_ref, *, n_exp):
    i = pl.program_id(0)
    lane = lax.broadcasted_iota(I32, (1, n_exp), 1)

    @pl.when(i == 0)
    def _():
        cnt = cnt_ref[...].astype(I32)
        padded = ((cnt + (EXPERT_BLOCK - 1)) // EXPERT_BLOCK) * EXPERT_BLOCK
        pe = padded
        s = 1
        while s < n_exp:
            pe = pe + jnp.where(lane >= s, pltpu.roll(pe, s, 1), 0)
            s *= 2
        start_ref[...] = (pe - padded).astype(F32)
        run_ref[...] = jnp.zeros(run_ref.shape, F32)
        row = lax.broadcasted_iota(I32, (8, n_exp), 0)
        bounds_ref[...] = jnp.where(row == 0, pe - padded, pe)

    idx = idx_ref[...]
    tr = idx.shape[0]
    onehots = [lane == idx[:, k:k + 1] for k in range(TOP_K)]
    member = onehots[0]
    for k in range(1, TOP_K):
        member = member | onehots[k]
    mb = jnp.where(member, 1.0, 0.0).astype(BF16)
    r_i = lax.broadcasted_iota(I32, (tr, tr), 0)
    c_i = lax.broadcasted_iota(I32, (tr, tr), 1)
    earlier = jnp.where(r_i > c_i, 1.0, 0.0).astype(BF16)
    dest = _dot(earlier, mb) + (start_ref[0:1, :] + run_ref[0:1, :])
    slot = lax.broadcasted_iota(I32, (1, LANE), 1)
    pos8 = jnp.zeros((tr, LANE), F32)
    for k in range(TOP_K):
        pk = jnp.sum(jnp.where(onehots[k], dest, 0.0), axis=-1, keepdims=True)
        pos8 = jnp.where(slot == k, pk, pos8)
    pos_ref[...] = pos8.T[0:TOP_K, :].astype(I32)
    run_ref[...] += jnp.sum(mb.astype(F32), axis=0, keepdims=True)


def _plan(idx8, counts, tr):
    n = idx8.shape[0]
    n_exp = counts.shape[1]
    rows = lambda i: (i, 0)
    return pl.pallas_call(
        functools.partial(_plan_kernel, n_exp=n_exp), grid=(n // tr,),
        in_specs=[pl.BlockSpec((tr, LANE), rows), _const_spec((8, n_exp))],
        out_specs=[pl.BlockSpec((TOP_K, tr), lambda i: (0, i)), pl.BlockSpec((8, n_exp), lambda i: (0, 0))],
        out_shape=[jax.ShapeDtypeStruct((TOP_K, n), I32), jax.ShapeDtypeStruct((8, n_exp), I32)],
        scratch_shapes=[pltpu.VMEM((8, n_exp), F32), pltpu.VMEM((8, n_exp), F32)],
        compiler_params=_cp(("arbitrary",)), name="plan",
    )(idx8, counts)


SC_CORES = 2
SC_SUBCORES = 16
SC_AXES = ("core", "subcore")
SC_WORKERS = SC_CORES * SC_SUBCORES
SC_WINDOW = 128


def _sc_mesh():
    return plsc.VectorSubcoreMesh(core_axis_name=SC_AXES[0], subcore_axis_name=SC_AXES[1],
                                  num_cores=SC_CORES, num_subcores=SC_SUBCORES)


def _sc_worker():
    return lax.axis_index(SC_AXES[1]) * SC_CORES + lax.axis_index(SC_AXES[0])


def _dispatch(h_p, h_s, pos_t, n_rows):
    n_p, c = h_p.shape
    n_s = h_s.shape[0]
    win_p, win_s = n_p // SC_WINDOW, n_s // SC_WINDOW
    assert n_p % (SC_WINDOW * SC_WORKERS) == 0 and n_s % SC_WINDOW == 0 and win_s <= SC_WORKERS

    @functools.partial(
        pl.kernel, out_type=jax.ShapeDtypeStruct((n_rows, c), h_p.dtype), mesh=_sc_mesh(), name="dispatch",
        scratch_types=[pltpu.VMEM((TOP_K, SC_WINDOW), I32), pltpu.VMEM((SC_WINDOW, c), h_p.dtype),
                       pltpu.SemaphoreType.DMA])
    def run(hp_hbm, hs_hbm, pos_hbm, xs_hbm, idx_v, rows_v, sem):
        wid = _sc_worker()

        def scatter_window(h_hbm, row0, tok0):
            pltpu.sync_copy(h_hbm.at[pl.ds(row0, SC_WINDOW)], rows_v)
            pltpu.sync_copy(pos_hbm.at[:, pl.ds(tok0, SC_WINDOW)], idx_v)
            copies = [pltpu.async_copy(rows_v, xs_hbm.at[idx_v.at[k]], sem) for k in range(TOP_K)]
            for cp in copies:
                cp.wait()

        def prompt_window(j, carry):
            row0 = pl.multiple_of((wid * (win_p // SC_WORKERS) + j) * SC_WINDOW, SC_WINDOW)
            scatter_window(hp_hbm, row0, row0)
            return carry
        lax.fori_loop(0, win_p // SC_WORKERS, prompt_window, 0)

        @pl.when(wid < win_s)
        def _():
            row0 = pl.multiple_of(wid * SC_WINDOW, SC_WINDOW)
            scatter_window(hs_hbm, row0, n_p + row0)

    return run(h_p, h_s, pos_t)


def _combine_gather(ys, pos_t):
    k_n = pos_t.shape[0] * pos_t.shape[1]
    c = ys.shape[1]
    win = SC_WINDOW // 2
    per_worker = k_n // (win * SC_WORKERS)
    assert k_n % (win * SC_WORKERS) == 0 and per_worker % 2 == 0
    dma = pltpu.SemaphoreType.DMA

    @functools.partial(
        pl.kernel, out_type=jax.ShapeDtypeStruct((k_n, c), ys.dtype), mesh=_sc_mesh(), name="combine_gather",
        scratch_types=[pltpu.VMEM((win,), I32), pltpu.VMEM((win,), I32),
                       pltpu.VMEM((win, c), ys.dtype), pltpu.VMEM((win, c), ys.dtype), dma, dma, dma, dma])
    def run(ys_hbm, idx_hbm, out_hbm, idx0, idx1, buf0, buf1, gsem0, gsem1, ssem0, ssem1):
        wid = _sc_worker()

        def rows(j):
            return pl.ds(pl.multiple_of((wid * per_worker + j) * win, win), win)

        def gather(j, idx_v, buf, sem):
            pltpu.sync_copy(idx_hbm.at[rows(j)], idx_v)
            pltpu.async_copy(ys_hbm.at[idx_v], buf, sem)

        def gather_wait(idx_v, buf, sem):
            pltpu.make_async_copy(ys_hbm.at[idx_v], buf, sem).wait()

        def store(j, buf, sem):
            pltpu.async_copy(buf, out_hbm.at[rows(j)], sem)

        def store_wait(buf, sem):
            pltpu.make_async_copy(buf, out_hbm.at[rows(0)], sem).wait()

        gather(0, idx0, buf0, gsem0)

        def two_windows(i, carry):
            j = 2 * i

            @pl.when(i > 0)
            def _():
                store_wait(buf1, ssem1)

            gather(j + 1, idx1, buf1, gsem1)
            gather_wait(idx0, buf0, gsem0)
            store(j, buf0, ssem0)

            @pl.when(j + 2 < per_worker)
            def _():
                store_wait(buf0, ssem0)
                gather(j + 2, idx0, buf0, gsem0)

            gather_wait(idx1, buf1, gsem1)
            store(j + 1, buf1, ssem1)
            return carry
        lax.fori_loop(0, per_worker // 2, two_windows, 0)
        store_wait(buf0, ssem0)
        store_wait(buf1, ssem1)

    return run(ys, pos_t.reshape(k_n))


MOE_RING = 8


MOE_WEIGHT_RING = 4


def _moe_kernel(st_ref, wg_ref, wu_ref, wd_ref, xs_ref, ys_ref, xbuf, ybuf, wgf, wuf, wdf, wgb, wub, wdb,
                semx, semy, semw):
    e = pl.program_id(0)
    n_e = pl.num_programs(0)

    def w_copies(ex):
        slot = ex % MOE_WEIGHT_RING
        return [pltpu.make_async_copy(src.at[ex], dst.at[slot], semw.at[slot, i])
                for i, (src, dst) in enumerate(((wg_ref, wgf), (wu_ref, wuf), (wd_ref, wdf)))]

    @pl.when(e == 0)
    def _():
        for ex in range(MOE_WEIGHT_RING - 1):
            @pl.when(ex < n_e)
            def _(ex=ex):
                for cp in w_copies(ex):
                    cp.start()

    @pl.when(e + (MOE_WEIGHT_RING - 1) < n_e)
    def _():
        for cp in w_copies(e + (MOE_WEIGHT_RING - 1)):
            cp.start()
    g0 = st_ref[e] // EXPERT_BLOCK
    nchunks = st_ref[e + 1] // EXPERT_BLOCK - g0
    ntot = st_ref[n_e] // EXPERT_BLOCK

    def rows(g):
        return pl.ds(pl.multiple_of(g * EXPERT_BLOCK, EXPERT_BLOCK), EXPERT_BLOCK)

    def x_copy(g):
        slot = g % MOE_RING
        return pltpu.make_async_copy(xs_ref.at[rows(g)], xbuf.at[slot], semx.at[slot])

    def y_copy(g):
        slot = g % MOE_RING
        return pltpu.make_async_copy(ybuf.at[slot], ys_ref.at[rows(g)], semy.at[slot])

    @pl.when(e == 0)
    def _():
        for g in range(MOE_RING - 1):
            @pl.when(g < ntot)
            def _(g=g):
                x_copy(g).start()

    for cp in w_copies(e):
        cp.wait()
    wslot = e % MOE_WEIGHT_RING
    wgb[...] = wgf[wslot].astype(BF16)
    wub[...] = wuf[wslot].astype(BF16)
    wdb[...] = wdf[wslot].astype(BF16)

    def take(g):
        @pl.when(g + (MOE_RING - 1) < ntot)
        def _():
            x_copy(g + (MOE_RING - 1)).start()

        x_copy(g).wait()

        @pl.when(g >= MOE_RING)
        def _():
            y_copy(g - MOE_RING).wait()

        return xbuf[g % MOE_RING]

    def expert_mlp(x_words):
        x = _unpack_rows(x_words).astype(BF16)
        mid = _silu(_dot(x, wgb[...])) * _dot(x, wub[...])
        return _pack_rows(_dot(mid.astype(BF16), wdb[...]))

    def put(g, y_words):
        ybuf[g % MOE_RING] = y_words
        y_copy(g).start()

    def run_chunks(g, m):
        xs = [take(g + i) for i in range(m)]
        y = expert_mlp(xs[0] if m == 1 else jnp.concatenate(xs, axis=0))
        for i in range(m):
            put(g + i, y[i * EXPERT_BLOCK:(i + 1) * EXPERT_BLOCK])

    def quad(p, carry):
        run_chunks(g0 + 4 * p, 4)
        return carry

    rest = nchunks % 4
    five = (rest == 1) & (nchunks >= 5)
    n_quads = nchunks // 4 - five.astype(I32)
    lax.fori_loop(0, n_quads, quad, 0)
    g_rest = g0 + 4 * n_quads

    @pl.when(five)
    def _():
        run_chunks(g_rest, 5)

    @pl.when(rest >= 2)
    def _():
        run_chunks(g_rest, 2)

    @pl.when((rest == 3) | (nchunks == 1))
    def _():
        run_chunks(g0 + nchunks - 1, 1)

    @pl.when(e == n_e - 1)
    def _():
        for j in range(MOE_RING):
            g = ntot - MOE_RING + j

            @pl.when(g >= 0)
            def _(g=g):
                y_copy(g).wait()


def _moe(starts, xs, wg, wu, wd):
    n_rows, c = xs.shape
    n_exp, d, f = wg.shape
    grid_spec = pltpu.PrefetchScalarGridSpec(
        num_scalar_prefetch=1, grid=(n_exp,),
        in_specs=[pl.BlockSpec(memory_space=pl.ANY)] * 4,
        out_specs=pl.BlockSpec(memory_space=pl.ANY),
        scratch_shapes=[pltpu.VMEM((MOE_RING, EXPERT_BLOCK, c), I32), pltpu.VMEM((MOE_RING, EXPERT_BLOCK, c), I32),
                        pltpu.VMEM((MOE_WEIGHT_RING, d, f), F32), pltpu.VMEM((MOE_WEIGHT_RING, d, f), F32),
                        pltpu.VMEM((MOE_WEIGHT_RING, f, d), F32),
                        pltpu.VMEM((d, f), BF16), pltpu.VMEM((d, f), BF16), pltpu.VMEM((f, d), BF16),
                        pltpu.SemaphoreType.DMA((MOE_RING,)), pltpu.SemaphoreType.DMA((MOE_RING,)),
                        pltpu.SemaphoreType.DMA((MOE_WEIGHT_RING, 3))])
    return pl.pallas_call(
        _moe_kernel, grid_spec=grid_spec, out_shape=jax.ShapeDtypeStruct((n_rows, c), I32),
        compiler_params=_cp(("arbitrary",)), name="moe",
    )(starts, wg, wu, wd, xs)


def _final_kernel(yg_ref, w8_ref, x1_ref, h2_ref, mod_ref, wsg_ref, wsu_ref, wsd_ref, nf_ref, y_ref, *, d):
    hb = _unpack_rows(h2_ref[...]).astype(BF16)
    shared = _dot((_silu(_dot(hb, wsg_ref[...])) * _dot(hb, wsu_ref[...])).astype(BF16), wsd_ref[...])
    w8 = w8_ref[...]
    routed = _unpack_rows(yg_ref[0]) * w8[:, 0:1]
    for k in range(1, TOP_K):
        routed = routed + _unpack_rows(yg_ref[k]) * w8[:, k:k + 1]
    g2 = mod_ref[...][:, (N_MOD - 1) * d:N_MOD * d]
    x2 = x1_ref[...] + g2 * (routed + shared)
    y_ref[...] = _rms(x2, nf_ref[...])


def _final(yg3, w8, x1, h2, mod, mod_spec, wts, nf, tf, n_tiles, tile0, out_shape, out_spec):
    d = x1.shape[1]
    rows = lambda i: (i, 0)
    in_specs = [pl.BlockSpec((TOP_K, tf, d // 2), lambda i: (0, tile0 + i, 0)),
                pl.BlockSpec((tf, LANE), lambda i: (tile0 + i, 0)),
                pl.BlockSpec((tf, d), rows), pl.BlockSpec((tf, d // 2), rows), mod_spec]
    in_specs += [_const_spec(w.shape) for w in wts]
    in_specs += [_const_spec(nf.shape)]
    return pl.pallas_call(
        functools.partial(_final_kernel, d=d), grid=(n_tiles,), in_specs=in_specs,
        out_specs=out_spec, out_shape=out_shape,
        compiler_params=_cp(("arbitrary",)), name="final",
    )(yg3, w8, x1, h2, mod, *wts, nf)


def _prep_in_kernel(w_ref, wia_ref, wig_ref, wil_ref, *, c_alr, c_ga, rank):
    wia_ref[...] = w_ref[:, 0:c_alr].astype(BF16)
    wig_ref[...] = w_ref[:, c_ga:].astype(BF16)
    lane = lax.broadcasted_iota(I32, (1, LANE), 1)
    wil_ref[...] = jnp.where(lane < rank, w_ref[:, c_alr:c_alr + LANE], 0.0).astype(BF16)


def _prep_in(wi, c_alr, c_ga, rank):
    d, width = wi.shape
    rb = 256 if d % 256 == 0 else d
    rows = lambda i: (i, 0)
    return pl.pallas_call(
        functools.partial(_prep_in_kernel, c_alr=c_alr, c_ga=c_ga, rank=rank), grid=(d // rb,),
        in_specs=[pl.BlockSpec((rb, width), rows)],
        out_specs=[pl.BlockSpec((rb, c_alr), rows), pl.BlockSpec((rb, width - c_ga), rows),
                   pl.BlockSpec((rb, LANE), rows)],
        out_shape=[jax.ShapeDtypeStruct((d, c_alr), BF16), jax.ShapeDtypeStruct((d, width - c_ga), BF16),
                   jax.ShapeDtypeStruct((d, LANE), BF16)],
        compiler_params=_cp(("arbitrary",)), name="prep_in",
    )(wi)


def _prep_small_kernel(*refs, n_cast, rank):
    ins, outs = refs[:n_cast + 2], refs[n_cast + 2:]
    for src, dst in zip(ins[:n_cast], outs[:n_cast]):
        dst[...] = src[...].astype(BF16)
    wr = ins[n_cast][...]
    hi = wr.astype(BF16)
    outs[n_cast][...] = hi
    outs[n_cast + 1][...] = (wr - hi.astype(F32)).astype(BF16)
    wal_out = outs[n_cast + 2]
    wal_out[...] = jnp.zeros(wal_out.shape, BF16)
    wal_out[0:rank, :] = ins[n_cast + 1][...].astype(BF16)


def _prep_small(casts, w_router, w_alpha):
    rank, hk = w_alpha.shape
    out_shape = [jax.ShapeDtypeStruct(w.shape, BF16) for w in casts]
    out_shape += [jax.ShapeDtypeStruct(w_router.shape, BF16)] * 2 + [jax.ShapeDtypeStruct((LANE, hk), BF16)]
    return pl.pallas_call(
        functools.partial(_prep_small_kernel, n_cast=len(casts), rank=rank), out_shape=out_shape,
        compiler_params=pltpu.CompilerParams(vmem_limit_bytes=VMEM_LIMIT), name="prep_small",
    )(*casts, w_router, w_alpha)


def kernel(x_prompt, x_sample, c_prompt, c_sample, state_pool, state_gla, w_ada, b_ada, norm1, w_in,
           pool_w, pool_scale, gla_w_alpha, gla_b_alpha, gla_norm, w_branch_a, w_branch_b, w_out, norm2,
           w_router, router_bias, w_exp_gate, w_exp_up, w_exp_down, w_sh_gate, w_sh_up, w_sh_down, norm_f):
    depth = w_in.shape[0]
    assert depth == 1, "single-layer trunk"
    bp, seq, d = x_prompt.shape
    bs, t_s, _ = x_sample.shape
    _, _, past, pw = state_pool.shape
    _, _, heads, dk, dv = state_gla.shape
    rank = gla_w_alpha.shape[1]
    n_exp = w_router.shape[2]
    assert past == POOL_HIST - 1 and pw == len(POOL_WINDOWS) * LANE and dv == LANE and rank <= LANE
    dm = _Dims(d, pw, heads, dk, dv)
    n_p, n_s = bp * seq, bs * t_s
    n_all = n_p + n_s

    off = [0]
    for sz in (pw, dm.hk, dm.hk, dm.gw, dm.gw, rank, d, d):
        off.append(off[-1] + sz)
    assert off[5] + 2 * d + LANE == dm.zw and off[5] % LANE == 0
    win = tuple(_prep_in(w_in[0], off[5], off[6], rank))
    pwb, wab, wbb, wob, wsg, wsu, wsd, wrh, wrl, wal = _prep_small(
        (pool_w[0], w_branch_a[0], w_branch_b[0], w_out[0], w_sh_gate[0], w_sh_up[0], w_sh_down[0]),
        w_router[0], gla_w_alpha[0])
    bal = gla_b_alpha[0].reshape(1, dm.hk)
    post_w = (pwb, pool_scale[0].reshape(1, pw), gla_norm[0].reshape(1, dm.gw), wab, wbb, wob,
              norm2[0].reshape(1, d), wrh, wrl)
    n1 = norm1[0].reshape(1, d)

    mod = _ada(jnp.concatenate([c_prompt, c_sample], axis=0), w_ada[0], b_ada[0])
    mod_p = mod[:bp].reshape(bp, 1, N_MOD * d)
    mod_s = mod[bp:]

    x1_p, h2_p, lg_p, npool_p, ngla_p = _mix_prompt(
        dm, x_prompt.reshape(n_p, d), mod_p, (n1,) + win + (wal, bal) + post_w, bp, seq, n_exp)

    xs2d = x_sample.reshape(bs, t_s * d)
    z_s = _sample_in(dm, xs2d, mod_s, n1, win, bs, t_s)
    mixed_s, o_s, npool_s, ngla_s = _sample_state(
        dm, z_s, state_pool[0], state_gla[0].reshape(bs, dm.hk, dv), wal, bal, bs, t_s)
    x1_s, h2_s, lg_s = _sample_post(dm, xs2d, mod_s, z_s, mixed_s, o_s, post_w, bs, t_s, n_exp)

    tr = 512 if (n_p % 512 == 0 and n_s % 512 == 0) else LANE
    assert n_p % tr == 0 and n_s % tr == 0 and n_p % bs == 0
    idx8, w8, counts = _route(lg_p, lg_s, router_bias[0], tr)
    nblk = (n_all * TOP_K) // EXPERT_BLOCK + n_exp
    pos_t, bounds = _plan(idx8, counts, tr)
    starts = jnp.concatenate([bounds[0], bounds[1, n_exp - 1:]])

    xs = _dispatch(h2_p, h2_s, pos_t, nblk * EXPERT_BLOCK)
    ys = _moe(starts, xs, w_exp_gate[0], w_exp_up[0], w_exp_down[0])
    yg3 = _combine_gather(ys, pos_t).reshape(TOP_K, n_all, d // 2)

    sh_w = (wsg, wsu, wsd)
    nf = norm_f.reshape(1, d)
    tf = 256 if seq % 256 == 0 else seq
    n_tp = seq // tf
    y_p = _final(
        yg3, w8, x1_p, h2_p, mod_p,
        pl.BlockSpec((None, 1, N_MOD * d), lambda i: (i // n_tp, 0, 0)), sh_w, nf, tf, n_p // tf, 0,
        jax.ShapeDtypeStruct((n_p, d), F32), pl.BlockSpec((tf, d), lambda i: (i, 0)))
    y_s = _final(
        yg3, w8, x1_s, h2_s, mod_s,
        _const_spec(mod_s.shape), sh_w, nf, bs, t_s, n_p // bs,
        jax.ShapeDtypeStruct((bs, t_s * d), F32), pl.BlockSpec((bs, d), lambda i: (0, i)))

    return (y_p.reshape(bp, seq, d), y_s.reshape(bs, t_s, d),
            npool_p.reshape(depth, bp, past, pw), ngla_p.reshape(depth, bp, heads, dk, dv),
            npool_s.reshape(depth, bs, past, pw), ngla_s.reshape(depth, bs, heads, dk, dv))
```

```python
import functools

import jax
import jax.numpy as jnp
from jax import lax
from jax.experimental import pallas as pl
from jax.experimental.pallas import tpu as pltpu
from jax.experimental.pallas import tpu_sc as plsc

F32 = jnp.float32
BF16 = jnp.bfloat16
I32 = jnp.int32

EPS = 1e-6
N_MOD = 6
POOL_WINDOWS = (2, 4, 8, 16)
POOL_HIST = 16
GLA_GATE_TEMP = 16.0
TOP_K = 8
ROUTED_SCALE = 2.5
PAST_LEN = 16384
EXPERT_BLOCK = 128
LANE = 128
GLA_FAST_MAX_DECAY = 40.0
PROJ_PIECE = 512
PROJ_LEAD = 2
VMEM_LIMIT = 56 * 1024 * 1024


def _cp(sem, vmem=VMEM_LIMIT):
    return pltpu.CompilerParams(dimension_semantics=sem, vmem_limit_bytes=vmem)


def _rms(x, g):
    return x * lax.rsqrt(jnp.mean(x * x, axis=-1, keepdims=True) + EPS) * g


def _silu(x):
    return x * jax.nn.sigmoid(x)


def _dot(a, b):
    return jnp.dot(a, b, preferred_element_type=F32)


def _split3(a):
    a0 = a.astype(BF16)
    r1 = a - a0.astype(F32)
    a1 = r1.astype(BF16)
    a2 = (r1 - a1.astype(F32)).astype(BF16)
    return a0, a1, a2


def _mod_parts(mod, d):
    return [mod[:, i * d:(i + 1) * d] for i in range(N_MOD)]


_HI16 = -65536


def _pack_rows(v):
    c = v.shape[1] // 2
    lo = lax.bitcast_convert_type(v[:, :c].astype(BF16).astype(F32), I32)
    hi = lax.bitcast_convert_type(v[:, c:].astype(BF16).astype(F32), I32)
    return (hi & _HI16) | lax.shift_right_logical(lo, 16)


def _unpack_rows(w):
    lo = lax.bitcast_convert_type(lax.shift_left(w, 16), F32)
    hi = lax.bitcast_convert_type(w & _HI16, F32)
    return jnp.concatenate([lo, hi], axis=1)


def _ada_kernel(c_ref, w_ref, b_ref, o_ref):
    a = _silu(c_ref[...]).astype(BF16)
    o_ref[...] = _dot(a, w_ref[...].astype(BF16)) + b_ref[...]


def _ada(c_all, w_ada, b_ada):
    n, d = c_all.shape
    cols = w_ada.shape[1]
    bc = d
    return pl.pallas_call(
        _ada_kernel,
        grid=(cols // bc,),
        in_specs=[pl.BlockSpec((n, d), lambda i: (0, 0)),
                  pl.BlockSpec((d, bc), lambda i: (0, i)),
                  pl.BlockSpec((1, bc), lambda i: (0, i))],
        out_specs=pl.BlockSpec((n, bc), lambda i: (0, i)),
        out_shape=jax.ShapeDtypeStruct((n, cols), F32),
        compiler_params=_cp(("arbitrary",)),
        name="ada",
    )(c_all, w_ada, b_ada.reshape(1, cols))


class _Dims:
    def __init__(self, d, pw, heads, dk, dv):
        self.d, self.pw, self.h, self.dk, self.dv = d, pw, heads, dk, dv
        self.hk, self.gw = heads * dk, heads * dv
        self.q0 = pw
        self.k0 = self.q0 + self.hk
        self.v0 = self.k0 + self.hk
        self.go0 = self.v0 + self.gw
        self.ga0 = self.go0 + self.gw
        self.gb0 = self.ga0 + d
        self.al0 = self.gb0 + d
        self.zw = self.al0 + LANE


def _log_decay(z_al, wal_ref, bal_ref):
    xal = _dot(z_al.astype(BF16), wal_ref[...]) + bal_ref[...]
    return jax.nn.log_sigmoid(xal) / GLA_GATE_TEMP


def _pool_project(mixed_groups, pw_ref, ps_ref):
    ys = [_dot(m.astype(BF16), pw_ref[g]) for g, m in enumerate(mixed_groups)]
    return jnp.concatenate(ys, axis=1) * ps_ref[...]


def _post_mix(dm, x, mod, ya, o, go, ga, gb, gn_ref, wa_ref, wb_ref, wo_ref, n2_ref, wrh_ref, wrl_ref,
              filler=lambda: None):
    _, _, g1, sh2, sc2, _ = _mod_parts(mod, dm.d)
    parts = []
    for h in range(dm.h):
        oh = o[:, h * dm.dv:(h + 1) * dm.dv]
        parts.append(oh * lax.rsqrt(jnp.mean(oh * oh, axis=-1, keepdims=True) + EPS))
    yb = jnp.concatenate(parts, axis=1) * gn_ref[...] * _silu(go)
    filler()
    m = (jax.nn.sigmoid(ga) * _dot(ya.astype(BF16), wa_ref[...])
         + jax.nn.sigmoid(gb) * _dot(yb.astype(BF16), wb_ref[...]))
    filler()
    x1 = x + g1 * _dot(m.astype(BF16), wo_ref[...])
    h2 = _rms(x1, n2_ref[...]) * (1 + sc2) + sh2
    hi = h2.astype(BF16)
    lo = (h2 - hi.astype(F32)).astype(BF16)
    filler()
    logits = _dot(hi, wrh_ref[...]) + (_dot(hi, wrl_ref[...]) + _dot(lo, wrh_ref[...]))
    filler()
    return x1, _pack_rows(h2), logits


def _in_project(dm, x, mod, n1_ref, win_refs, z_ref):
    h = (_rms(x, n1_ref[...]) * (1 + mod[:, dm.d:2 * dm.d]) + mod[:, 0:dm.d]).astype(BF16)
    starts = [0]
    for w_ref in win_refs:
        starts.append(starts[-1] + w_ref.shape[1])
    for i in (0, 2, 1):
        z_ref[:, starts[i]:starts[i + 1]] = _dot(h, win_refs[i][...])


def _mixp_kernel(x_ref, mod_ref, n1_ref, wia_ref, wig_ref, wil_ref, wal_ref, bal_ref, pw_ref, ps_ref, gn_ref,
                 wa_ref, wb_ref, wo_ref, n2_ref, wrh_ref, wrl_ref,
                 x1_ref, h2_ref, lg_ref, np_ref, ng_ref,
                 za_ref, zb_ref, xk_ref, mk_ref, hn_ref, uext_ref, s_ref, b_ref, oi_ref, *, dm, tt, n_t):
    n = pl.program_id(0)
    j = jnp.maximum(n - 1, 0) % n_t

    @pl.when(n == 0)
    def _():
        za_ref[...] = jnp.zeros(za_ref.shape, F32)
        zb_ref[...] = jnp.zeros(zb_ref.shape, F32)
        xk_ref[...] = jnp.zeros(xk_ref.shape, F32)
        mk_ref[...] = jnp.zeros(mk_ref.shape, F32)

    @pl.when(j == 0)
    def _():
        uext_ref[0:POOL_HIST, :] = jnp.zeros((POOL_HIST, dm.pw), F32)
        s_ref[...] = jnp.zeros(s_ref.shape, F32)

    @pl.when(n % 2 == 0)
    def _():
        _mixp_region(za_ref, zb_ref, x_ref, mod_ref, n1_ref, wia_ref, wig_ref, wil_ref, wal_ref, bal_ref, pw_ref,
                     ps_ref, gn_ref, wa_ref, wb_ref, wo_ref, n2_ref, wrh_ref, wrl_ref, x1_ref, h2_ref, lg_ref,
                     xk_ref, mk_ref, hn_ref, uext_ref, s_ref, b_ref, oi_ref, dm=dm, tt=tt, j=j)

    @pl.when(n % 2 == 1)
    def _():
        _mixp_region(zb_ref, za_ref, x_ref, mod_ref, n1_ref, wia_ref, wig_ref, wil_ref, wal_ref, bal_ref, pw_ref,
                     ps_ref, gn_ref, wa_ref, wb_ref, wo_ref, n2_ref, wrh_ref, wrl_ref, x1_ref, h2_ref, lg_ref,
                     xk_ref, mk_ref, hn_ref, uext_ref, s_ref, b_ref, oi_ref, dm=dm, tt=tt, j=j)

    xk_ref[...] = x_ref[...]
    mk_ref[...] = mod_ref[...]

    @pl.when(j == n_t - 1)
    def _():
        np_ref[...] = uext_ref[tt + 1:tt + POOL_HIST, :]
        ng_ref[...] = s_ref[...]


def _mixp_region(zp_ref, z_ref, x_ref, mod_ref, n1_ref, wia_ref, wig_ref, wil_ref, wal_ref, bal_ref, pw_ref,
                 ps_ref, gn_ref, wa_ref, wb_ref, wo_ref, n2_ref, wrh_ref, wrl_ref, x1_ref, h2_ref, lg_ref,
                 xk_ref, mk_ref, hn_ref, uext_ref, s_ref, b_ref, oi_ref, *, dm, tt, j):
    scale = dm.dk ** -0.5
    lane = lax.broadcasted_iota(I32, (1, dm.hk), 1)
    head_masks = [(lane >= h * dm.dk) & (lane < (h + 1) * dm.dk) for h in range(dm.h)]
    causal = lax.broadcasted_iota(I32, (tt, tt), 0) >= lax.broadcasted_iota(I32, (tt, tt), 1)

    def pool_branch():
        pos = j * tt + lax.broadcasted_iota(I32, (tt, 1), 0)
        gw = dm.pw // len(POOL_WINDOWS)
        mixed = []
        for g, w in enumerate(POOL_WINDOWS):
            c0 = g * gw
            cur = uext_ref[POOL_HIST:POOL_HIST + tt, c0:c0 + gw]
            acc = cur
            for i in range(1, w):
                acc = acc + uext_ref[POOL_HIST - i:POOL_HIST - i + tt, c0:c0 + gw]
            cnt = jnp.minimum(w, pos + 1).astype(F32)
            mixed.append(acc / cnt - cur)
        return _pool_project(mixed, pw_ref, ps_ref)

    def scaled_queries(bc):
        return (z_ref[:, dm.q0:dm.q0 + dm.hk] * scale) * jnp.exp(bc)

    def inter_chunk(qe):
        s_bf = s_ref[...].astype(BF16)
        return jnp.concatenate(
            [_dot(jnp.where(head_masks[h], qe, 0.0).astype(BF16), s_bf) for h in range(dm.h)], axis=1)

    def finish(ya, o, filler=lambda: None):
        x1, h2, logits = _post_mix(
            dm, xk_ref[...], mk_ref[...], ya, o, z_ref[:, dm.go0:dm.go0 + dm.gw],
            z_ref[:, dm.ga0:dm.ga0 + dm.d], z_ref[:, dm.gb0:dm.gb0 + dm.d],
            gn_ref, wa_ref, wb_ref, wo_ref, n2_ref, wrh_ref, wrl_ref, filler)
        x1_ref[...] = x1
        h2_ref[...] = h2
        lg_ref[...] = logits

    mod_in = mod_ref[...]
    hn_ref[...] = (_rms(x_ref[...], n1_ref[...]) * (1 + mod_in[:, dm.d:2 * dm.d]) + mod_in[:, 0:dm.d]).astype(BF16)
    pieces = iter([(w_ref, a, min(a + PROJ_PIECE, w_ref.shape[1]), c0)
                   for w_ref, c0 in ((wia_ref, 0), (wil_ref, dm.al0), (wig_ref, dm.ga0))
                   for a in range(0, w_ref.shape[1], PROJ_PIECE)])

    def project_piece():
        piece = next(pieces, None)
        if piece is not None:
            w_ref, a, b, c0 = piece
            zp_ref[:, c0 + a:c0 + b] = _dot(hn_ref[...], w_ref[:, a:b])
        return piece is not None

    for _ in range(PROJ_LEAD):
        project_piece()
    uext_ref[POOL_HIST:POOL_HIST + tt, :] = z_ref[:, 0:dm.pw]
    ya = pool_branch()
    project_piece()
    la = _log_decay(z_ref[:, dm.al0:dm.al0 + LANE], wal_ref, bal_ref)
    tri = jnp.where(causal, 1.0, 0.0).astype(BF16)
    a0, a1, a2 = _split3(la)
    bc = _dot(tri, a0) + (_dot(tri, a1) + _dot(tri, a2))
    b_ref[...] = bc
    project_piece()
    qe = scaled_queries(bc)
    k = z_ref[:, dm.k0:dm.k0 + dm.hk]
    vb = z_ref[:, dm.v0:dm.v0 + dm.gw].astype(BF16)
    blast = bc[tt - 1:tt, :]
    fast = jnp.max(-blast) <= GLA_FAST_MAX_DECAY
    ke = (k * jnp.exp(-bc)).astype(BF16)
    intra = []
    for h in range(dm.h):
        qh = jnp.where(head_masks[h], qe, 0.0).astype(BF16)
        sc = lax.dot_general(qh, ke, (((1,), (1,)), ((), ())), preferred_element_type=F32)
        sc = jnp.where(causal, sc, 0.0).astype(BF16)
        intra.append(_dot(sc, vb[:, h * dm.dv:(h + 1) * dm.dv]))
    project_piece()
    finish(ya, inter_chunk(qe) + jnp.concatenate(intra, axis=1), project_piece)
    while project_piece():
        pass

    @pl.when(jnp.logical_not(fast))
    def _():
        rows = lax.broadcasted_iota(I32, (tt, 1), 0)
        for h in range(dm.h):
            def body(tb, carry, h=h):
                base = pl.multiple_of(tb * 8, 8)
                q8 = z_ref[pl.ds(base, 8), dm.q0:dm.q0 + dm.hk] * scale
                b8 = b_ref[pl.ds(base, 8), :]
                vh = z_ref[:, dm.v0 + h * dm.dv:dm.v0 + (h + 1) * dm.dv]
                out_rows = []
                for r in range(8):
                    keep = (rows <= base + r) & head_masks[h]
                    dec = jnp.exp(jnp.where(keep, b8[r:r + 1, :] - b_ref[...], -jnp.inf))
                    wgt = (q8[r:r + 1, :] * dec) * z_ref[:, dm.k0:dm.k0 + dm.hk]
                    s_col = jnp.sum(wgt, axis=-1, keepdims=True)
                    out_rows.append(jnp.sum(s_col * vh, axis=0, keepdims=True))
                oi_ref[pl.ds(base, 8), h * dm.dv:(h + 1) * dm.dv] = jnp.concatenate(out_rows, axis=0)
                return carry
            lax.fori_loop(0, tt // 8, body, 0)
        finish(pool_branch(), inter_chunk(scaled_queries(b_ref[...])) + oi_ref[...])

    bc = b_ref[...]
    blast = bc[tt - 1:tt, :]
    kd = z_ref[:, dm.k0:dm.k0 + dm.hk] * jnp.exp(blast - bc)
    u_all = _dot(kd.T.astype(BF16), z_ref[:, dm.v0:dm.v0 + dm.gw].astype(BF16))
    dec_col = jnp.broadcast_to(jnp.exp(blast), (dm.hk, dm.hk)).T[:, 0:dm.dv]
    upd = jnp.concatenate(
        [u_all[h * dm.dk:(h + 1) * dm.dk, h * dm.dv:(h + 1) * dm.dv] for h in range(dm.h)], axis=0)
    s_ref[...] = dec_col * s_ref[...] + upd
    uext_ref[0:POOL_HIST, :] = uext_ref[tt:tt + POOL_HIST, :]


def _const_spec(shape):
    nd = len(shape)
    return pl.BlockSpec(shape, lambda *_: (0,) * nd, pipeline_mode=pl.Buffered(1))


def _mix_prompt(dm, x2d, mod3, wts, bsz, seq, n_exp):
    tt = 256 if seq % 256 == 0 else seq
    n_t = seq // tt
    n = bsz * seq
    d = dm.d
    kern = functools.partial(_mixp_kernel, dm=dm, tt=tt, n_t=n_t)
    tiles = bsz * n_t
    proj = lambda i: jnp.minimum(i, tiles - 1)
    fin = lambda i: jnp.maximum(i - 1, 0)
    row = lambda i: (fin(i), 0)
    in_specs = [pl.BlockSpec((tt, d), lambda i: (proj(i), 0)),
                pl.BlockSpec((None, 1, N_MOD * d), lambda i: (proj(i) // n_t, 0, 0))]
    in_specs += [_const_spec(w.shape) for w in wts]
    out_specs = [pl.BlockSpec((tt, d), row), pl.BlockSpec((tt, d // 2), row), pl.BlockSpec((tt, n_exp), row),
                 pl.BlockSpec((None, POOL_HIST - 1, dm.pw), lambda i: (fin(i) // n_t, 0, 0)),
                 pl.BlockSpec((None, dm.hk, dm.dv), lambda i: (fin(i) // n_t, 0, 0))]
    out_shape = [jax.ShapeDtypeStruct((n, d), F32), jax.ShapeDtypeStruct((n, d // 2), I32),
                 jax.ShapeDtypeStruct((n, n_exp), F32),
                 jax.ShapeDtypeStruct((bsz, POOL_HIST - 1, dm.pw), F32),
                 jax.ShapeDtypeStruct((bsz, dm.hk, dm.dv), F32)]
    scratch = [pltpu.VMEM((tt, dm.zw), F32), pltpu.VMEM((tt, dm.zw), F32),
               pltpu.VMEM((tt, d), F32), pltpu.VMEM((1, N_MOD * d), F32), pltpu.VMEM((tt, d), BF16),
               pltpu.VMEM((tt + POOL_HIST, dm.pw), F32),
               pltpu.VMEM((dm.hk, dm.dv), F32), pltpu.VMEM((tt, dm.hk), F32), pltpu.VMEM((tt, dm.gw), F32)]
    return pl.pallas_call(
        kern, grid=(tiles + 1,), in_specs=in_specs, out_specs=out_specs, out_shape=out_shape,
        scratch_shapes=scratch, compiler_params=_cp(("arbitrary",)), name="mixp",
    )(x2d, mod3, *wts)


def _sin_kernel(x_ref, mod_ref, n1_ref, wia_ref, wig_ref, wil_ref, z_ref, *, dm):
    _in_project(dm, x_ref[...], mod_ref[...], n1_ref, (wia_ref, wig_ref, wil_ref), z_ref)


def _sample_in(dm, xs2d, mod_s, n1, win, bs, t_s):
    d = dm.d
    return pl.pallas_call(
        functools.partial(_sin_kernel, dm=dm),
        grid=(t_s,),
        in_specs=[pl.BlockSpec((bs, d), lambda t: (0, t)), _const_spec(mod_s.shape), _const_spec(n1.shape)]
        + [_const_spec(w.shape) for w in win],
        out_specs=pl.BlockSpec((bs, dm.zw), lambda t: (t, 0)),
        out_shape=jax.ShapeDtypeStruct((t_s * bs, dm.zw), F32),
        compiler_params=_cp(("arbitrary",)), name="sample_in",
    )(xs2d, mod_s, n1, *win)


def _sst_kernel(z_ref, pool_ref, gs_ref, wal_ref, bal_ref,
                mixed_ref, o_ref, npool_ref, ngs_ref,
                qt_ref, kt_ref, dt_ref, vt_ref, ot_ref, *, dm, bs, t_s, ch, pos0):
    i = pl.program_id(0)
    n_i = pl.num_programs(0)
    scale = dm.dk ** -0.5
    past = POOL_HIST - 1

    @pl.when(i == 0)
    def _():
        for t in range(t_s):
            r0 = t * bs
            la = _log_decay(z_ref[r0:r0 + bs, dm.al0:dm.al0 + LANE], wal_ref, bal_ref)
            dt_ref[t] = jnp.exp(la).T
            qt_ref[t] = (z_ref[r0:r0 + bs, dm.q0:dm.q0 + dm.hk] * scale).T
            kt_ref[t] = z_ref[r0:r0 + bs, dm.k0:dm.k0 + dm.hk].T
            vt_ref[t] = z_ref[r0:r0 + bs, dm.v0:dm.v0 + dm.gw].T
        ot_ref[...] = jnp.zeros(ot_ref.shape, F32)

        def item(s, c0, c1):
            if s >= 0:
                return z_ref[s * bs:(s + 1) * bs, c0:c1]
            return pool_ref[:, (past + s) * dm.pw + c0:(past + s) * dm.pw + c1]

        gw = dm.pw // len(POOL_WINDOWS)
        for t in range(t_s):
            for g, w in enumerate(POOL_WINDOWS):
                c0, c1 = g * gw, (g + 1) * gw
                cur = item(t, c0, c1)
                acc = cur
                for r in range(1, w):
                    acc = acc + item(t - r, c0, c1)
                cnt = float(min(w, pos0 + t + 1))
                mixed_ref[t * bs:(t + 1) * bs, c0:c1] = acc / cnt - cur
        for r in range(past):
            npool_ref[:, r * dm.pw:(r + 1) * dm.pw] = item(r - past + t_s, 0, dm.pw)

    per_head = dm.dk // ch
    hoff = pl.multiple_of((i // per_head) * dm.dv, dm.dv)
    for p in range(ch):
        hd = i * ch + p
        st = gs_ref[:, p, :].T
        for t in range(t_s):
            st = dt_ref[t, pl.ds(hd, 1), :] * st + kt_ref[t, pl.ds(hd, 1), :] * vt_ref[t, pl.ds(hoff, dm.dv), :]
            ot_ref[t, pl.ds(hoff, dm.dv), :] += qt_ref[t, pl.ds(hd, 1), :] * st
        ngs_ref[:, p, :] = st.T

    @pl.when(i == n_i - 1)
    def _():
        for t in range(t_s):
            for h in range(dm.h):
                o_ref[t * bs:(t + 1) * bs, h * dm.dv:(h + 1) * dm.dv] = ot_ref[t, h * dm.dv:(h + 1) * dm.dv, :].T


def _sample_state(dm, z_s, pool2d, gs3, wal, bal, bs, t_s):
    ch = 16
    n_i = dm.hk // ch
    past = POOL_HIST - 1
    kern = functools.partial(_sst_kernel, dm=dm, bs=bs, t_s=t_s, ch=ch, pos0=PAST_LEN)
    full = lambda shape: pl.BlockSpec(shape, lambda i: (0,) * len(shape))
    return pl.pallas_call(
        kern, grid=(n_i,),
        in_specs=[_const_spec(z_s.shape), _const_spec(pool2d.shape),
                  pl.BlockSpec((bs, ch, dm.dv), lambda i: (0, i, 0)),
                  _const_spec(wal.shape), _const_spec(bal.shape)],
        out_specs=[full((t_s * bs, dm.pw)), full((t_s * bs, dm.gw)), full((bs, past * dm.pw)),
                   pl.BlockSpec((bs, ch, dm.dv), lambda i: (0, i, 0))],
        out_shape=[jax.ShapeDtypeStruct((t_s * bs, dm.pw), F32), jax.ShapeDtypeStruct((t_s * bs, dm.gw), F32),
                   jax.ShapeDtypeStruct((bs, past * dm.pw), F32), jax.ShapeDtypeStruct(gs3.shape, F32)],
        scratch_shapes=[pltpu.VMEM((t_s, dm.hk, bs), F32), pltpu.VMEM((t_s, dm.hk, bs), F32),
                        pltpu.VMEM((t_s, dm.hk, bs), F32), pltpu.VMEM((t_s, dm.gw, bs), F32),
                        pltpu.VMEM((t_s, dm.gw, bs), F32)],
        compiler_params=_cp(("arbitrary",)), name="sample_state",
    )(z_s, pool2d, gs3, wal, bal)


def _spost_kernel(x_ref, mod_ref, z_ref, mixed_ref, o_ref, pw_ref, ps_ref, gn_ref,
                  wa_ref, wb_ref, wo_ref, n2_ref, wrh_ref, wrl_ref,
                  x1_ref, h2_ref, lg_ref, *, dm):
    gw = dm.pw // len(POOL_WINDOWS)
    ya = _pool_project([mixed_ref[:, g * gw:(g + 1) * gw] for g in range(len(POOL_WINDOWS))], pw_ref, ps_ref)
    x1, h2, logits = _post_mix(
        dm, x_ref[...], mod_ref[...], ya, o_ref[...],
        z_ref[:, dm.go0:dm.go0 + dm.gw], z_ref[:, dm.ga0:dm.ga0 + dm.d], z_ref[:, dm.gb0:dm.gb0 + dm.d],
        gn_ref, wa_ref, wb_ref, wo_ref, n2_ref, wrh_ref, wrl_ref)
    x1_ref[...] = x1
    h2_ref[...] = h2
    lg_ref[...] = logits


def _sample_post(dm, xs2d, mod_s, z_s, mixed_s, o_s, wts, bs, t_s, n_exp):
    d = dm.d
    rows = lambda t: (t, 0)
    in_specs = [pl.BlockSpec((bs, d), lambda t: (0, t)), _const_spec(mod_s.shape),
                pl.BlockSpec((bs, dm.zw), rows), pl.BlockSpec((bs, dm.pw), rows), pl.BlockSpec((bs, dm.gw), rows)]
    in_specs += [_const_spec(w.shape) for w in wts]
    return pl.pallas_call(
        functools.partial(_spost_kernel, dm=dm), grid=(t_s,), in_specs=in_specs,
        out_specs=[pl.BlockSpec((bs, d), rows), pl.BlockSpec((bs, d // 2), rows), pl.BlockSpec((bs, n_exp), rows)],
        out_shape=[jax.ShapeDtypeStruct((t_s * bs, d), F32), jax.ShapeDtypeStruct((t_s * bs, d // 2), I32),
                   jax.ShapeDtypeStruct((t_s * bs, n_exp), F32)],
        compiler_params=_cp(("arbitrary",)), name="sample_post",
    )(xs2d, mod_s, z_s, mixed_s, o_s, *wts)


def _route_kernel(lgp_ref, lgs_ref, bias_ref, idx_ref, w_ref, cnt_ref, *, n_exp, tiles_p):
    i = pl.program_id(0)
    s = jax.nn.sigmoid(jnp.where(i < tiles_p, lgp_ref[...], lgs_ref[...]))
    cur = s + bias_ref[...]
    tr = s.shape[0]
    lane = lax.broadcasted_iota(I32, (1, n_exp), 1).astype(F32)
    slot = lax.broadcasted_iota(I32, (1, LANE), 1)
    idx8 = jnp.zeros((tr, LANE), F32)
    w8 = jnp.zeros((tr, LANE), F32)
    wsum = jnp.zeros((tr, 1), F32)
    picked = jnp.zeros((tr, n_exp), F32)
    for k in range(TOP_K):
        m = jnp.max(cur, axis=-1, keepdims=True)
        am = jnp.min(jnp.where(cur == m, lane, float(n_exp)), axis=-1, keepdims=True)
        oh = lane == am
        sk = jnp.sum(jnp.where(oh, s, 0.0), axis=-1, keepdims=True)
        cur = jnp.where(oh, -jnp.inf, cur)
        idx8 = jnp.where(slot == k, am, idx8)
        w8 = jnp.where(slot == k, sk, w8)
        wsum = wsum + sk
        picked = picked + jnp.where(oh, 1.0, 0.0)
    idx_ref[...] = idx8.astype(I32)
    w_ref[...] = w8 / wsum * ROUTED_SCALE

    @pl.when(i == 0)
    def _():
        cnt_ref[...] = jnp.zeros(cnt_ref.shape, F32)

    cnt_ref[...] += jnp.sum(picked, axis=0, keepdims=True)


def _two_source_specs(block, tiles_p, tiles_s):
    return [pl.BlockSpec(block, lambda i: (jnp.minimum(i, tiles_p - 1), 0)),
            pl.BlockSpec(block, lambda i: (jnp.clip(i - tiles_p, 0, tiles_s - 1), 0))]


def _route(lg_p, lg_s, bias, tr):
    n_exp = lg_p.shape[1]
    tiles_p, tiles_s = lg_p.shape[0] // tr, lg_s.shape[0] // tr
    n = (tiles_p + tiles_s) * tr
    rows = lambda i: (i, 0)
    return pl.pallas_call(
        functools.partial(_route_kernel, n_exp=n_exp, tiles_p=tiles_p), grid=(tiles_p + tiles_s,),
        in_specs=_two_source_specs((tr, n_exp), tiles_p, tiles_s) + [_const_spec((1, n_exp))],
        out_specs=[pl.BlockSpec((tr, LANE), rows), pl.BlockSpec((tr, LANE), rows),
                   pl.BlockSpec((8, n_exp), lambda i: (0, 0))],
        out_shape=[jax.ShapeDtypeStruct((n, LANE), I32), jax.ShapeDtypeStruct((n, LANE), F32),
                   jax.ShapeDtypeStruct((8, n_exp), F32)],
        compiler_params=_cp(("arbitrary",)), name="route",
    )(lg_p, lg_s, bias.reshape(1, n_exp))


def _plan_kernel(idx_ref, cnt_ref, pos_ref, bounds_ref, start_ref, run_ref, *, n_exp):
    i = pl.program_id(0)
    lane = lax.broadcasted_iota(I32, (1, n_exp), 1)

    @pl.when(i == 0)
    def _():
        cnt = cnt_ref[...].astype(I32)
        padded = ((cnt + (EXPERT_BLOCK - 1)) // EXPERT_BLOCK) * EXPERT_BLOCK
        pe = padded
        s = 1
        while s < n_exp:
            pe = pe + jnp.where(lane >= s, pltpu.roll(pe, s, 1), 0)
            s *= 2
        start_ref[...] = (pe - padded).astype(F32)
        run_ref[...] = jnp.zeros(run_ref.shape, F32)
        row = lax.broadcasted_iota(I32, (8, n_exp), 0)
        bounds_ref[...] = jnp.where(row == 0, pe - padded, pe)

    idx = idx_ref[...]
    tr = idx.shape[0]
    onehots = [lane == idx[:, k:k + 1] for k in range(TOP_K)]
    member = onehots[0]
    for k in range(1, TOP_K):
        member = member | onehots[k]
    mb = jnp.where(member, 1.0, 0.0).astype(BF16)
    r_i = lax.broadcasted_iota(I32, (tr, tr), 0)
    c_i = lax.broadcasted_iota(I32, (tr, tr), 1)
    earlier = jnp.where(r_i > c_i, 1.0, 0.0).astype(BF16)
    dest = _dot(earlier, mb) + (start_ref[0:1, :] + run_ref[0:1, :])
    slot = lax.broadcasted_iota(I32, (1, LANE), 1)
    pos8 = jnp.zeros((tr, LANE), F32)
    for k in range(TOP_K):
        pk = jnp.sum(jnp.where(onehots[k], dest, 0.0), axis=-1, keepdims=True)
        pos8 = jnp.where(slot == k, pk, pos8)
    pos_ref[...] = pos8.T[0:TOP_K, :].astype(I32)
    run_ref[...] += jnp.sum(mb.astype(F32), axis=0, keepdims=True)


def _plan(idx8, counts, tr):
    n = idx8.shape[0]
    n_exp = counts.shape[1]
    rows = lambda i: (i, 0)
    return pl.pallas_call(
        functools.partial(_plan_kernel, n_exp=n_exp), grid=(n // tr,),
        in_specs=[pl.BlockSpec((tr, LANE), rows), _const_spec((8, n_exp))],
        out_specs=[pl.BlockSpec((TOP_K, tr), lambda i: (0, i)), pl.BlockSpec((8, n_exp), lambda i: (0, 0))],
        out_shape=[jax.ShapeDtypeStruct((TOP_K, n), I32), jax.ShapeDtypeStruct((8, n_exp), I32)],
        scratch_shapes=[pltpu.VMEM((8, n_exp), F32), pltpu.VMEM((8, n_exp), F32)],
        compiler_params=_cp(("arbitrary",)), name="plan",
    )(idx8, counts)


SC_CORES = 2
SC_SUBCORES = 16
SC_AXES = ("core", "subcore")
SC_WORKERS = SC_CORES * SC_SUBCORES
SC_WINDOW = 128


def _sc_mesh():
    return plsc.VectorSubcoreMesh(core_axis_name=SC_AXES[0], subcore_axis_name=SC_AXES[1],
                                  num_cores=SC_CORES, num_subcores=SC_SUBCORES)


def _sc_worker():
    return lax.axis_index(SC_AXES[1]) * SC_CORES + lax.axis_index(SC_AXES[0])


def _dispatch(h_p, h_s, pos_t, n_rows):
    n_p, c = h_p.shape
    n_s = h_s.shape[0]
    win_p, win_s = n_p // SC_WINDOW, n_s // SC_WINDOW
    assert n_p % (SC_WINDOW * SC_WORKERS) == 0 and n_s % SC_WINDOW == 0 and win_s <= SC_WORKERS

    @functools.partial(
        pl.kernel, out_type=jax.ShapeDtypeStruct((n_rows, c), h_p.dtype), mesh=_sc_mesh(), name="dispatch",
        scratch_types=[pltpu.VMEM((TOP_K, SC_WINDOW), I32), pltpu.VMEM((SC_WINDOW, c), h_p.dtype),
                       pltpu.SemaphoreType.DMA])
    def run(hp_hbm, hs_hbm, pos_hbm, xs_hbm, idx_v, rows_v, sem):
        wid = _sc_worker()

        def scatter_window(h_hbm, row0, tok0):
            pltpu.sync_copy(h_hbm.at[pl.ds(row0, SC_WINDOW)], rows_v)
            pltpu.sync_copy(pos_hbm.at[:, pl.ds(tok0, SC_WINDOW)], idx_v)
            copies = [pltpu.async_copy(rows_v, xs_hbm.at[idx_v.at[k]], sem) for k in range(TOP_K)]
            for cp in copies:
                cp.wait()

        def prompt_window(j, carry):
            row0 = pl.multiple_of((wid * (win_p // SC_WORKERS) + j) * SC_WINDOW, SC_WINDOW)
            scatter_window(hp_hbm, row0, row0)
            return carry
        lax.fori_loop(0, win_p // SC_WORKERS, prompt_window, 0)

        @pl.when(wid < win_s)
        def _():
            row0 = pl.multiple_of(wid * SC_WINDOW, SC_WINDOW)
            scatter_window(hs_hbm, row0, n_p + row0)

    return run(h_p, h_s, pos_t)


def _combine_gather(ys, pos_t):
    k_n = pos_t.shape[0] * pos_t.shape[1]
    c = ys.shape[1]
    win = SC_WINDOW // 2
    per_worker = k_n // (win * SC_WORKERS)
    assert k_n % (win * SC_WORKERS) == 0 and per_worker % 2 == 0
    dma = pltpu.SemaphoreType.DMA

    @functools.partial(
        pl.kernel, out_type=jax.ShapeDtypeStruct((k_n, c), ys.dtype), mesh=_sc_mesh(), name="combine_gather",
        scratch_types=[pltpu.VMEM((win,), I32), pltpu.VMEM((win,), I32),
                       pltpu.VMEM((win, c), ys.dtype), pltpu.VMEM((win, c), ys.dtype), dma, dma, dma, dma])
    def run(ys_hbm, idx_hbm, out_hbm, idx0, idx1, buf0, buf1, gsem0, gsem1, ssem0, ssem1):
        wid = _sc_worker()

        def rows(j):
            return pl.ds(pl.multiple_of((wid * per_worker + j) * win, win), win)

        def gather(j, idx_v, buf, sem):
            pltpu.sync_copy(idx_hbm.at[rows(j)], idx_v)
            pltpu.async_copy(ys_hbm.at[idx_v], buf, sem)

        def gather_wait(idx_v, buf, sem):
            pltpu.make_async_copy(ys_hbm.at[idx_v], buf, sem).wait()

        def store(j, buf, sem):
            pltpu.async_copy(buf, out_hbm.at[rows(j)], sem)

        def store_wait(buf, sem):
            pltpu.make_async_copy(buf, out_hbm.at[rows(0)], sem).wait()

        gather(0, idx0, buf0, gsem0)

        def two_windows(i, carry):
            j = 2 * i

            @pl.when(i > 0)
            def _():
                store_wait(buf1, ssem1)

            gather(j + 1, idx1, buf1, gsem1)
            gather_wait(idx0, buf0, gsem0)
            store(j, buf0, ssem0)

            @pl.when(j + 2 < per_worker)
            def _():
                store_wait(buf0, ssem0)
                gather(j + 2, idx0, buf0, gsem0)

            gather_wait(idx1, buf1, gsem1)
            store(j + 1, buf1, ssem1)
            return carry
        lax.fori_loop(0, per_worker // 2, two_windows, 0)
        store_wait(buf0, ssem0)
        store_wait(buf1, ssem1)

    return run(ys, pos_t.reshape(k_n))


MOE_RING = 12
MOE_WEIGHT_RING = 6


def _moe_kernel(st_ref, wg_ref, wu_ref, wd_ref, xs_ref, ys_ref, xbuf, ybuf, wgf, wuf, wdf, wgb, wub, wdb,
                semx, semy, semw):
    e = pl.program_id(0)
    n_e = pl.num_programs(0)

    def w_copies(ex):
        slot = ex % MOE_WEIGHT_RING
        return [pltpu.make_async_copy(src.at[ex], dst.at[slot], semw.at[slot, i])
                for i, (src, dst) in enumerate(((wg_ref, wgf), (wu_ref, wuf), (wd_ref, wdf)))]

    @pl.when(e == 0)
    def _():
        for ex in range(MOE_WEIGHT_RING - 1):
            @pl.when(ex < n_e)
            def _(ex=ex):
                for cp in w_copies(ex):
                    cp.start()

    @pl.when(e + (MOE_WEIGHT_RING - 1) < n_e)
    def _():
        for cp in w_copies(e + (MOE_WEIGHT_RING - 1)):
            cp.start()
    g0 = st_ref[e] // EXPERT_BLOCK
    nchunks = st_ref[e + 1] // EXPERT_BLOCK - g0
    ntot = st_ref[n_e] // EXPERT_BLOCK

    def rows(g):
        return pl.ds(pl.multiple_of(g * EXPERT_BLOCK, EXPERT_BLOCK), EXPERT_BLOCK)

    def x_copy(g):
        slot = g % MOE_RING
        return pltpu.make_async_copy(xs_ref.at[rows(g)], xbuf.at[slot], semx.at[slot])

    def y_copy(g):
        slot = g % MOE_RING
        return pltpu.make_async_copy(ybuf.at[slot], ys_ref.at[rows(g)], semy.at[slot])

    @pl.when(e == 0)
    def _():
        for g in range(MOE_RING - 1):
            @pl.when(g < ntot)
            def _(g=g):
                x_copy(g).start()

    for cp in w_copies(e):
        cp.wait()
    wslot = e % MOE_WEIGHT_RING
    wgb[...] = wgf[wslot].astype(BF16)
    wub[...] = wuf[wslot].astype(BF16)
    wdb[...] = wdf[wslot].astype(BF16)

    def take(g):
        @pl.when(g + (MOE_RING - 1) < ntot)
        def _():
            x_copy(g + (MOE_RING - 1)).start()

        x_copy(g).wait()

        @pl.when(g >= MOE_RING)
        def _():
            y_copy(g - MOE_RING).wait()

        return xbuf[g % MOE_RING]

    def expert_mlp(x_words):
        x = _unpack_rows(x_words).astype(BF16)
        mid = _silu(_dot(x, wgb[...])) * _dot(x, wub[...])
        return _pack_rows(_dot(mid.astype(BF16), wdb[...]))

    def put(g, y_words):
        ybuf[g % MOE_RING] = y_words
        y_copy(g).start()

    def run_chunks(g, m):
        xs = [take(g + i) for i in range(m)]
        y = expert_mlp(xs[0] if m == 1 else jnp.concatenate(xs, axis=0))
        for i in range(m):
            put(g + i, y[i * EXPERT_BLOCK:(i + 1) * EXPERT_BLOCK])

    def quad(p, carry):
        run_chunks(g0 + 4 * p, 4)
        return carry

    rest = nchunks % 4
    five = (rest == 1) & (nchunks >= 5)
    n_quads = nchunks // 4 - five.astype(I32)
    lax.fori_loop(0, n_quads, quad, 0)
    g_rest = g0 + 4 * n_quads

    @pl.when(five)
    def _():
        run_chunks(g_rest, 5)

    @pl.when(rest >= 2)
    def _():
        run_chunks(g_rest, 2)

    @pl.when((rest == 3) | (nchunks == 1))
    def _():
        run_chunks(g0 + nchunks - 1, 1)

    @pl.when(e == n_e - 1)
    def _():
        for j in range(MOE_RING):
            g = ntot - MOE_RING + j

            @pl.when(g >= 0)
            def _(g=g):
                y_copy(g).wait()


def _moe(starts, xs, wg, wu, wd):
    n_rows, c = xs.shape
    n_exp, d, f = wg.shape
    grid_spec = pltpu.PrefetchScalarGridSpec(
        num_scalar_prefetch=1, grid=(n_exp,),
        in_specs=[pl.BlockSpec(memory_space=pl.ANY)] * 4,
        out_specs=pl.BlockSpec(memory_space=pl.ANY),
        scratch_shapes=[pltpu.VMEM((MOE_RING, EXPERT_BLOCK, c), I32), pltpu.VMEM((MOE_RING, EXPERT_BLOCK, c), I32),
                        pltpu.VMEM((MOE_WEIGHT_RING, d, f), F32), pltpu.VMEM((MOE_WEIGHT_RING, d, f), F32),
                        pltpu.VMEM((MOE_WEIGHT_RING, f, d), F32),
                        pltpu.VMEM((d, f), BF16), pltpu.VMEM((d, f), BF16), pltpu.VMEM((f, d), BF16),
                        pltpu.SemaphoreType.DMA((MOE_RING,)), pltpu.SemaphoreType.DMA((MOE_RING,)),
                        pltpu.SemaphoreType.DMA((MOE_WEIGHT_RING, 3))])
    return pl.pallas_call(
        _moe_kernel, grid_spec=grid_spec, out_shape=jax.ShapeDtypeStruct((n_rows, c), I32),
        compiler_params=_cp(("arbitrary",)), name="moe",
    )(starts, wg, wu, wd, xs)


def _final_kernel(yg_ref, w8_ref, x1_ref, h2_ref, mod_ref, wsg_ref, wsu_ref, wsd_ref, nf_ref, y_ref, *, d):
    hb = _unpack_rows(h2_ref[...]).astype(BF16)
    shared = _dot((_silu(_dot(hb, wsg_ref[...])) * _dot(hb, wsu_ref[...])).astype(BF16), wsd_ref[...])
    w8 = w8_ref[...]
    routed = _unpack_rows(yg_ref[0]) * w8[:, 0:1]
    for k in range(1, TOP_K):
        routed = routed + _unpack_rows(yg_ref[k]) * w8[:, k:k + 1]
    g2 = mod_ref[...][:, (N_MOD - 1) * d:N_MOD * d]
    x2 = x1_ref[...] + g2 * (routed + shared)
    y_ref[...] = _rms(x2, nf_ref[...])


def _final(yg3, w8, x1, h2, mod, mod_spec, wts, nf, tf, n_tiles, tile0, out_shape, out_spec):
    d = x1.shape[1]
    rows = lambda i: (i, 0)
    in_specs = [pl.BlockSpec((TOP_K, tf, d // 2), lambda i: (0, tile0 + i, 0)),
                pl.BlockSpec((tf, LANE), lambda i: (tile0 + i, 0)),
                pl.BlockSpec((tf, d), rows), pl.BlockSpec((tf, d // 2), rows), mod_spec]
    in_specs += [_const_spec(w.shape) for w in wts]
    in_specs += [_const_spec(nf.shape)]
    return pl.pallas_call(
        functools.partial(_final_kernel, d=d), grid=(n_tiles,), in_specs=in_specs,
        out_specs=out_spec, out_shape=out_shape,
        compiler_params=_cp(("arbitrary",)), name="final",
    )(yg3, w8, x1, h2, mod, *wts, nf)


def kernel(x_prompt, x_sample, c_prompt, c_sample, state_pool, state_gla, w_ada, b_ada, norm1, w_in,
           pool_w, pool_scale, gla_w_alpha, gla_b_alpha, gla_norm, w_branch_a, w_branch_b, w_out, norm2,
           w_router, router_bias, w_exp_gate, w_exp_up, w_exp_down, w_sh_gate, w_sh_up, w_sh_down, norm_f):
    depth = w_in.shape[0]
    assert depth == 1, "single-layer trunk"
    bp, seq, d = x_prompt.shape
    bs, t_s, _ = x_sample.shape
    _, _, past, pw = state_pool.shape
    _, _, heads, dk, dv = state_gla.shape
    rank = gla_w_alpha.shape[1]
    n_exp = w_router.shape[2]
    assert past == POOL_HIST - 1 and pw == len(POOL_WINDOWS) * LANE and dv == LANE and rank <= LANE
    dm = _Dims(d, pw, heads, dk, dv)
    n_p, n_s = bp * seq, bs * t_s
    n_all = n_p + n_s

    off = [0]
    for sz in (pw, dm.hk, dm.hk, dm.gw, dm.gw, rank, d, d):
        off.append(off[-1] + sz)
    wi = w_in[0]
    win = (wi[:, :off[5]].astype(BF16), wi[:, off[6]:].astype(BF16),
           jnp.concatenate([wi[:, off[5]:off[6]], jnp.zeros((d, LANE - rank), F32)], axis=1).astype(BF16))
    assert off[5] + 2 * d + LANE == dm.zw
    wal = jnp.concatenate([gla_w_alpha[0], jnp.zeros((LANE - rank, dm.hk), F32)], axis=0).astype(BF16)
    bal = gla_b_alpha[0].reshape(1, dm.hk)
    wr = w_router[0]
    wrh = wr.astype(BF16)
    wrl = (wr - wrh.astype(F32)).astype(BF16)
    post_w = (pool_w[0].astype(BF16), pool_scale[0].reshape(1, pw), gla_norm[0].reshape(1, dm.gw),
              w_branch_a[0].astype(BF16), w_branch_b[0].astype(BF16), w_out[0].astype(BF16),
              norm2[0].reshape(1, d), wrh, wrl)
    n1 = norm1[0].reshape(1, d)

    mod = _ada(jnp.concatenate([c_prompt, c_sample], axis=0), w_ada[0], b_ada[0])
    mod_p = mod[:bp].reshape(bp, 1, N_MOD * d)
    mod_s = mod[bp:]

    x1_p, h2_p, lg_p, npool_p, ngla_p = _mix_prompt(
        dm, x_prompt.reshape(n_p, d), mod_p, (n1,) + win + (wal, bal) + post_w, bp, seq, n_exp)

    xs2d = x_sample.reshape(bs, t_s * d)
    z_s = _sample_in(dm, xs2d, mod_s, n1, win, bs, t_s)
    mixed_s, o_s, npool_s, ngla_s = _sample_state(
        dm, z_s, state_pool[0].reshape(bs, past * pw), state_gla[0].reshape(bs, dm.hk, dv), wal, bal, bs, t_s)
    x1_s, h2_s, lg_s = _sample_post(dm, xs2d, mod_s, z_s, mixed_s, o_s, post_w, bs, t_s, n_exp)

    tr = 512 if (n_p % 512 == 0 and n_s % 512 == 0) else LANE
    assert n_p % tr == 0 and n_s % tr == 0 and n_p % bs == 0
    idx8, w8, counts = _route(lg_p, lg_s, router_bias[0], tr)
    nblk = (n_all * TOP_K) // EXPERT_BLOCK + n_exp
    pos_t, bounds = _plan(idx8, counts, tr)
    starts = jnp.concatenate([bounds[0], bounds[1, n_exp - 1:]])

    xs = _dispatch(h2_p, h2_s, pos_t, nblk * EXPERT_BLOCK)
    ys = _moe(starts, xs, w_exp_gate[0], w_exp_up[0], w_exp_down[0])
    yg3 = _combine_gather(ys, pos_t).reshape(TOP_K, n_all, d // 2)

    sh_w = (w_sh_gate[0].astype(BF16), w_sh_up[0].astype(BF16), w_sh_down[0].astype(BF16))
    nf = norm_f.reshape(1, d)
    tf = 512 if seq % 512 == 0 else seq
    n_tp = seq // tf
    y_p = _final(
        yg3, w8, x1_p, h2_p, mod_p,
        pl.BlockSpec((None, 1, N_MOD * d), lambda i: (i // n_tp, 0, 0)), sh_w, nf, tf, n_p // tf, 0,
        jax.ShapeDtypeStruct((n_p, d), F32), pl.BlockSpec((tf, d), lambda i: (i, 0)))
    y_s = _final(
        yg3, w8, x1_s, h2_s, mod_s,
        _const_spec(mod_s.shape), sh_w, nf, bs, t_s, n_p // bs,
        jax.ShapeDtypeStruct((bs, t_s * d), F32), pl.BlockSpec((bs, d), lambda i: (0, i)))

    return (y_p.reshape(bp, seq, d), y_s.reshape(bs, t_s, d),
            npool_p.reshape(depth, bp, past, pw), ngla_p.reshape(depth, bp, heads, dk, dv),
            npool_s.reshape(depth, bs, past, pw), ngla_s.reshape(depth, bs, heads, dk, dv))
```

```python
import functools

import jax
import jax.numpy as jnp
from jax import lax
from jax.experimental import pallas as pl
from jax.experimental.pallas import tpu as pltpu
from jax.experimental.pallas import tpu_sc as plsc

F32 = jnp.float32
BF16 = jnp.bfloat16
I32 = jnp.int32

EPS = 1e-6
N_MOD = 6
POOL_WINDOWS = (2, 4, 8, 16)
POOL_HIST = 16
GLA_GATE_TEMP = 16.0
TOP_K = 8
ROUTED_SCALE = 2.5
PAST_LEN = 16384
EXPERT_BLOCK = 128
LANE = 128
GLA_FAST_MAX_DECAY = 40.0
PROJ_PIECE = 512
PROJ_LEAD = 2
VMEM_LIMIT = 56 * 1024 * 1024


def _cp(sem, vmem=VMEM_LIMIT):
    return pltpu.CompilerParams(dimension_semantics=sem, vmem_limit_bytes=vmem)


def _rms(x, g):
    return x * lax.rsqrt(jnp.mean(x * x, axis=-1, keepdims=True) + EPS) * g


def _silu(x):
    return x * jax.nn.sigmoid(x)


def _dot(a, b):
    return jnp.dot(a, b, preferred_element_type=F32)


def _split3(a):
    a0 = a.astype(BF16)
    r1 = a - a0.astype(F32)
    a1 = r1.astype(BF16)
    a2 = (r1 - a1.astype(F32)).astype(BF16)
    return a0, a1, a2


def _mod_parts(mod, d):
    return [mod[:, i * d:(i + 1) * d] for i in range(N_MOD)]


_HI16 = -65536


def _pack_rows(v):
    c = v.shape[1] // 2
    lo = lax.bitcast_convert_type(v[:, :c].astype(BF16).astype(F32), I32)
    hi = lax.bitcast_convert_type(v[:, c:].astype(BF16).astype(F32), I32)
    return (hi & _HI16) | lax.shift_right_logical(lo, 16)


def _unpack_rows(w):
    lo = lax.bitcast_convert_type(lax.shift_left(w, 16), F32)
    hi = lax.bitcast_convert_type(w & _HI16, F32)
    return jnp.concatenate([lo, hi], axis=1)


def _ada_kernel(c_ref, w_ref, b_ref, o_ref):
    a = _silu(c_ref[...]).astype(BF16)
    o_ref[...] = _dot(a, w_ref[...].astype(BF16)) + b_ref[...]


def _ada(c_all, w_ada, b_ada):
    n, d = c_all.shape
    cols = w_ada.shape[1]
    bc = d
    return pl.pallas_call(
        _ada_kernel,
        grid=(cols // bc,),
        in_specs=[pl.BlockSpec((n, d), lambda i: (0, 0)),
                  pl.BlockSpec((d, bc), lambda i: (0, i)),
                  pl.BlockSpec((1, bc), lambda i: (0, i))],
        out_specs=pl.BlockSpec((n, bc), lambda i: (0, i)),
        out_shape=jax.ShapeDtypeStruct((n, cols), F32),
        compiler_params=_cp(("arbitrary",)),
        name="ada",
    )(c_all, w_ada, b_ada.reshape(1, cols))


class _Dims:
    def __init__(self, d, pw, heads, dk, dv):
        self.d, self.pw, self.h, self.dk, self.dv = d, pw, heads, dk, dv
        self.hk, self.gw = heads * dk, heads * dv
        self.q0 = pw
        self.k0 = self.q0 + self.hk
        self.v0 = self.k0 + self.hk
        self.go0 = self.v0 + self.gw
        self.ga0 = self.go0 + self.gw
        self.gb0 = self.ga0 + d
        self.al0 = self.gb0 + d
        self.zw = self.al0 + LANE


def _log_decay(z_al, wal_ref, bal_ref):
    xal = _dot(z_al.astype(BF16), wal_ref[...]) + bal_ref[...]
    return jax.nn.log_sigmoid(xal) / GLA_GATE_TEMP


def _pool_project(mixed_groups, pw_ref, ps_ref):
    ys = [_dot(m.astype(BF16), pw_ref[g]) for g, m in enumerate(mixed_groups)]
    return jnp.concatenate(ys, axis=1) * ps_ref[...]


def _post_mix(dm, x, mod, ya, o, go, ga, gb, gn_ref, wa_ref, wb_ref, wo_ref, n2_ref, wrh_ref, wrl_ref,
              filler=lambda: None):
    _, _, g1, sh2, sc2, _ = _mod_parts(mod, dm.d)
    parts = []
    for h in range(dm.h):
        oh = o[:, h * dm.dv:(h + 1) * dm.dv]
        parts.append(oh * lax.rsqrt(jnp.mean(oh * oh, axis=-1, keepdims=True) + EPS))
    yb = jnp.concatenate(parts, axis=1) * gn_ref[...] * _silu(go)
    filler()
    m = (jax.nn.sigmoid(ga) * _dot(ya.astype(BF16), wa_ref[...])
         + jax.nn.sigmoid(gb) * _dot(yb.astype(BF16), wb_ref[...]))
    filler()
    x1 = x + g1 * _dot(m.astype(BF16), wo_ref[...])
    h2 = _rms(x1, n2_ref[...]) * (1 + sc2) + sh2
    hi = h2.astype(BF16)
    lo = (h2 - hi.astype(F32)).astype(BF16)
    filler()
    logits = _dot(hi, wrh_ref[...]) + (_dot(hi, wrl_ref[...]) + _dot(lo, wrh_ref[...]))
    filler()
    return x1, _pack_rows(h2), logits


def _in_project(dm, x, mod, n1_ref, win_refs, z_ref):
    h = (_rms(x, n1_ref[...]) * (1 + mod[:, dm.d:2 * dm.d]) + mod[:, 0:dm.d]).astype(BF16)
    starts = [0]
    for w_ref in win_refs:
        starts.append(starts[-1] + w_ref.shape[1])
    for i in (0, 2, 1):
        z_ref[:, starts[i]:starts[i + 1]] = _dot(h, win_refs[i][...])


def _mixp_kernel(x_ref, mod_ref, n1_ref, wia_ref, wig_ref, wil_ref, wal_ref, bal_ref, pw_ref, ps_ref, gn_ref,
                 wa_ref, wb_ref, wo_ref, n2_ref, wrh_ref, wrl_ref,
                 x1_ref, h2_ref, lg_ref, np_ref, ng_ref,
                 za_ref, zb_ref, xk_ref, mk_ref, hn_ref, uext_ref, s_ref, b_ref, oi_ref, *, dm, tt, n_t):
    n = pl.program_id(0)
    j = jnp.maximum(n - 1, 0) % n_t

    @pl.when(n == 0)
    def _():
        za_ref[...] = jnp.zeros(za_ref.shape, F32)
        zb_ref[...] = jnp.zeros(zb_ref.shape, F32)
        xk_ref[...] = jnp.zeros(xk_ref.shape, F32)
        mk_ref[...] = jnp.zeros(mk_ref.shape, F32)

    @pl.when(j == 0)
    def _():
        uext_ref[0:POOL_HIST, :] = jnp.zeros((POOL_HIST, dm.pw), F32)
        s_ref[...] = jnp.zeros(s_ref.shape, F32)

    @pl.when(n % 2 == 0)
    def _():
        _mixp_region(za_ref, zb_ref, x_ref, mod_ref, n1_ref, wia_ref, wig_ref, wil_ref, wal_ref, bal_ref, pw_ref,
                     ps_ref, gn_ref, wa_ref, wb_ref, wo_ref, n2_ref, wrh_ref, wrl_ref, x1_ref, h2_ref, lg_ref,
                     xk_ref, mk_ref, hn_ref, uext_ref, s_ref, b_ref, oi_ref, dm=dm, tt=tt, j=j)

    @pl.when(n % 2 == 1)
    def _():
        _mixp_region(zb_ref, za_ref, x_ref, mod_ref, n1_ref, wia_ref, wig_ref, wil_ref, wal_ref, bal_ref, pw_ref,
                     ps_ref, gn_ref, wa_ref, wb_ref, wo_ref, n2_ref, wrh_ref, wrl_ref, x1_ref, h2_ref, lg_ref,
                     xk_ref, mk_ref, hn_ref, uext_ref, s_ref, b_ref, oi_ref, dm=dm, tt=tt, j=j)

    xk_ref[...] = x_ref[...]
    mk_ref[...] = mod_ref[...]

    @pl.when(j == n_t - 1)
    def _():
        np_ref[...] = uext_ref[tt + 1:tt + POOL_HIST, :]
        ng_ref[...] = s_ref[...]


def _mixp_region(zp_ref, z_ref, x_ref, mod_ref, n1_ref, wia_ref, wig_ref, wil_ref, wal_ref, bal_ref, pw_ref,
                 ps_ref, gn_ref, wa_ref, wb_ref, wo_ref, n2_ref, wrh_ref, wrl_ref, x1_ref, h2_ref, lg_ref,
                 xk_ref, mk_ref, hn_ref, uext_ref, s_ref, b_ref, oi_ref, *, dm, tt, j):
    scale = dm.dk ** -0.5
    lane = lax.broadcasted_iota(I32, (1, dm.hk), 1)
    head_masks = [(lane >= h * dm.dk) & (lane < (h + 1) * dm.dk) for h in range(dm.h)]
    causal = lax.broadcasted_iota(I32, (tt, tt), 0) >= lax.broadcasted_iota(I32, (tt, tt), 1)

    def pool_branch():
        pos = j * tt + lax.broadcasted_iota(I32, (tt, 1), 0)
        gw = dm.pw // len(POOL_WINDOWS)
        mixed = []
        for g, w in enumerate(POOL_WINDOWS):
            c0 = g * gw
            cur = uext_ref[POOL_HIST:POOL_HIST + tt, c0:c0 + gw]
            acc = cur
            for i in range(1, w):
                acc = acc + uext_ref[POOL_HIST - i:POOL_HIST - i + tt, c0:c0 + gw]
            cnt = jnp.minimum(w, pos + 1).astype(F32)
            mixed.append(acc / cnt - cur)
        return _pool_project(mixed, pw_ref, ps_ref)

    def scaled_queries(bc):
        return (z_ref[:, dm.q0:dm.q0 + dm.hk] * scale) * jnp.exp(bc)

    def inter_chunk(qe):
        s_bf = s_ref[...].astype(BF16)
        return jnp.concatenate(
            [_dot(jnp.where(head_masks[h], qe, 0.0).astype(BF16), s_bf) for h in range(dm.h)], axis=1)

    def finish(ya, o, filler=lambda: None):
        x1, h2, logits = _post_mix(
            dm, xk_ref[...], mk_ref[...], ya, o, z_ref[:, dm.go0:dm.go0 + dm.gw],
            z_ref[:, dm.ga0:dm.ga0 + dm.d], z_ref[:, dm.gb0:dm.gb0 + dm.d],
            gn_ref, wa_ref, wb_ref, wo_ref, n2_ref, wrh_ref, wrl_ref, filler)
        x1_ref[...] = x1
        h2_ref[...] = h2
        lg_ref[...] = logits

    mod_in = mod_ref[...]
    hn_ref[...] = (_rms(x_ref[...], n1_ref[...]) * (1 + mod_in[:, dm.d:2 * dm.d]) + mod_in[:, 0:dm.d]).astype(BF16)
    pieces = iter([(w_ref, a, min(a + PROJ_PIECE, w_ref.shape[1]), c0)
                   for w_ref, c0 in ((wia_ref, 0), (wil_ref, dm.al0), (wig_ref, dm.ga0))
                   for a in range(0, w_ref.shape[1], PROJ_PIECE)])

    def project_piece():
        piece = next(pieces, None)
        if piece is not None:
            w_ref, a, b, c0 = piece
            zp_ref[:, c0 + a:c0 + b] = _dot(hn_ref[...], w_ref[:, a:b])
        return piece is not None

    for _ in range(PROJ_LEAD):
        project_piece()
    uext_ref[POOL_HIST:POOL_HIST + tt, :] = z_ref[:, 0:dm.pw]
    ya = pool_branch()
    project_piece()
    la = _log_decay(z_ref[:, dm.al0:dm.al0 + LANE], wal_ref, bal_ref)
    tri = jnp.where(causal, 1.0, 0.0).astype(BF16)
    a0, a1, a2 = _split3(la)
    bc = _dot(tri, a0) + (_dot(tri, a1) + _dot(tri, a2))
    b_ref[...] = bc
    project_piece()
    qe = scaled_queries(bc)
    k = z_ref[:, dm.k0:dm.k0 + dm.hk]
    vb = z_ref[:, dm.v0:dm.v0 + dm.gw].astype(BF16)
    blast = bc[tt - 1:tt, :]
    fast = jnp.max(-blast) <= GLA_FAST_MAX_DECAY
    ke = (k * jnp.exp(-bc)).astype(BF16)
    intra = []
    for h in range(dm.h):
        qh = jnp.where(head_masks[h], qe, 0.0).astype(BF16)
        sc = lax.dot_general(qh, ke, (((1,), (1,)), ((), ())), preferred_element_type=F32)
        sc = jnp.where(causal, sc, 0.0).astype(BF16)
        intra.append(_dot(sc, vb[:, h * dm.dv:(h + 1) * dm.dv]))
    project_piece()
    finish(ya, inter_chunk(qe) + jnp.concatenate(intra, axis=1), project_piece)
    while project_piece():
        pass

    @pl.when(jnp.logical_not(fast))
    def _():
        rows = lax.broadcasted_iota(I32, (tt, 1), 0)
        for h in range(dm.h):
            def body(tb, carry, h=h):
                base = pl.multiple_of(tb * 8, 8)
                q8 = z_ref[pl.ds(base, 8), dm.q0:dm.q0 + dm.hk] * scale
                b8 = b_ref[pl.ds(base, 8), :]
                vh = z_ref[:, dm.v0 + h * dm.dv:dm.v0 + (h + 1) * dm.dv]
                out_rows = []
                for r in range(8):
                    keep = (rows <= base + r) & head_masks[h]
                    dec = jnp.exp(jnp.where(keep, b8[r:r + 1, :] - b_ref[...], -jnp.inf))
                    wgt = (q8[r:r + 1, :] * dec) * z_ref[:, dm.k0:dm.k0 + dm.hk]
                    s_col = jnp.sum(wgt, axis=-1, keepdims=True)
                    out_rows.append(jnp.sum(s_col * vh, axis=0, keepdims=True))
                oi_ref[pl.ds(base, 8), h * dm.dv:(h + 1) * dm.dv] = jnp.concatenate(out_rows, axis=0)
                return carry
            lax.fori_loop(0, tt // 8, body, 0)
        finish(pool_branch(), inter_chunk(scaled_queries(b_ref[...])) + oi_ref[...])

    bc = b_ref[...]
    blast = bc[tt - 1:tt, :]
    kd = z_ref[:, dm.k0:dm.k0 + dm.hk] * jnp.exp(blast - bc)
    u_all = _dot(kd.T.astype(BF16), z_ref[:, dm.v0:dm.v0 + dm.gw].astype(BF16))
    dec_col = jnp.broadcast_to(jnp.exp(blast), (dm.hk, dm.hk)).T[:, 0:dm.dv]
    upd = jnp.concatenate(
        [u_all[h * dm.dk:(h + 1) * dm.dk, h * dm.dv:(h + 1) * dm.dv] for h in range(dm.h)], axis=0)
    s_ref[...] = dec_col * s_ref[...] + upd
    uext_ref[0:POOL_HIST, :] = uext_ref[tt:tt + POOL_HIST, :]


def _const_spec(shape):
    nd = len(shape)
    return pl.BlockSpec(shape, lambda *_: (0,) * nd, pipeline_mode=pl.Buffered(1))


def _mix_prompt(dm, x2d, mod3, wts, bsz, seq, n_exp):
    tt = 256 if seq % 256 == 0 else seq
    n_t = seq // tt
    n = bsz * seq
    d = dm.d
    kern = functools.partial(_mixp_kernel, dm=dm, tt=tt, n_t=n_t)
    tiles = bsz * n_t
    proj = lambda i: jnp.minimum(i, tiles - 1)
    fin = lambda i: jnp.maximum(i - 1, 0)
    row = lambda i: (fin(i), 0)
    in_specs = [pl.BlockSpec((tt, d), lambda i: (proj(i), 0)),
                pl.BlockSpec((None, 1, N_MOD * d), lambda i: (proj(i) // n_t, 0, 0))]
    in_specs += [_const_spec(w.shape) for w in wts]
    out_specs = [pl.BlockSpec((tt, d), row), pl.BlockSpec((tt, d // 2), row), pl.BlockSpec((tt, n_exp), row),
                 pl.BlockSpec((None, POOL_HIST - 1, dm.pw), lambda i: (fin(i) // n_t, 0, 0)),
                 pl.BlockSpec((None, dm.hk, dm.dv), lambda i: (fin(i) // n_t, 0, 0))]
    out_shape = [jax.ShapeDtypeStruct((n, d), F32), jax.ShapeDtypeStruct((n, d // 2), I32),
                 jax.ShapeDtypeStruct((n, n_exp), F32),
                 jax.ShapeDtypeStruct((bsz, POOL_HIST - 1, dm.pw), F32),
                 jax.ShapeDtypeStruct((bsz, dm.hk, dm.dv), F32)]
    scratch = [pltpu.VMEM((tt, dm.zw), F32), pltpu.VMEM((tt, dm.zw), F32),
               pltpu.VMEM((tt, d), F32), pltpu.VMEM((1, N_MOD * d), F32), pltpu.VMEM((tt, d), BF16),
               pltpu.VMEM((tt + POOL_HIST, dm.pw), F32),
               pltpu.VMEM((dm.hk, dm.dv), F32), pltpu.VMEM((tt, dm.hk), F32), pltpu.VMEM((tt, dm.gw), F32)]
    return pl.pallas_call(
        kern, grid=(tiles + 1,), in_specs=in_specs, out_specs=out_specs, out_shape=out_shape,
        scratch_shapes=scratch, compiler_params=_cp(("arbitrary",)), name="mixp",
    )(x2d, mod3, *wts)


def _sin_kernel(x_ref, mod_ref, n1_ref, wia_ref, wig_ref, wil_ref, z_ref, *, dm):
    _in_project(dm, x_ref[...], mod_ref[...], n1_ref, (wia_ref, wig_ref, wil_ref), z_ref)


def _sample_in(dm, xs2d, mod_s, n1, win, bs, t_s):
    d = dm.d
    return pl.pallas_call(
        functools.partial(_sin_kernel, dm=dm),
        grid=(t_s,),
        in_specs=[pl.BlockSpec((bs, d), lambda t: (0, t)), _const_spec(mod_s.shape), _const_spec(n1.shape)]
        + [_const_spec(w.shape) for w in win],
        out_specs=pl.BlockSpec((bs, dm.zw), lambda t: (t, 0)),
        out_shape=jax.ShapeDtypeStruct((t_s * bs, dm.zw), F32),
        compiler_params=_cp(("arbitrary",)), name="sample_in",
    )(xs2d, mod_s, n1, *win)


def _sst_kernel(z_ref, pool_ref, gs_ref, wal_ref, bal_ref,
                mixed_ref, o_ref, npool_ref, ngs_ref,
                qt_ref, kt_ref, dt_ref, vt_ref, ot_ref, *, dm, bs, t_s, ch, pos0):
    i = pl.program_id(0)
    n_i = pl.num_programs(0)
    scale = dm.dk ** -0.5
    past = POOL_HIST - 1

    @pl.when(i == 0)
    def _():
        for t in range(t_s):
            r0 = t * bs
            la = _log_decay(z_ref[r0:r0 + bs, dm.al0:dm.al0 + LANE], wal_ref, bal_ref)
            dt_ref[t] = jnp.exp(la).T
            qt_ref[t] = (z_ref[r0:r0 + bs, dm.q0:dm.q0 + dm.hk] * scale).T
            kt_ref[t] = z_ref[r0:r0 + bs, dm.k0:dm.k0 + dm.hk].T
            vt_ref[t] = z_ref[r0:r0 + bs, dm.v0:dm.v0 + dm.gw].T
        ot_ref[...] = jnp.zeros(ot_ref.shape, F32)

        def item(s, c0, c1):
            if s >= 0:
                return z_ref[s * bs:(s + 1) * bs, c0:c1]
            return pool_ref[:, (past + s) * dm.pw + c0:(past + s) * dm.pw + c1]

        gw = dm.pw // len(POOL_WINDOWS)
        for t in range(t_s):
            for g, w in enumerate(POOL_WINDOWS):
                c0, c1 = g * gw, (g + 1) * gw
                cur = item(t, c0, c1)
                acc = cur
                for r in range(1, w):
                    acc = acc + item(t - r, c0, c1)
                cnt = float(min(w, pos0 + t + 1))
                mixed_ref[t * bs:(t + 1) * bs, c0:c1] = acc / cnt - cur
        for r in range(past):
            npool_ref[:, r * dm.pw:(r + 1) * dm.pw] = item(r - past + t_s, 0, dm.pw)

    per_head = dm.dk // ch
    hoff = pl.multiple_of((i // per_head) * dm.dv, dm.dv)
    for p in range(ch):
        hd = i * ch + p
        st = gs_ref[:, p, :].T
        for t in range(t_s):
            st = dt_ref[t, pl.ds(hd, 1), :] * st + kt_ref[t, pl.ds(hd, 1), :] * vt_ref[t, pl.ds(hoff, dm.dv), :]
            ot_ref[t, pl.ds(hoff, dm.dv), :] += qt_ref[t, pl.ds(hd, 1), :] * st
        ngs_ref[:, p, :] = st.T

    @pl.when(i == n_i - 1)
    def _():
        for t in range(t_s):
            for h in range(dm.h):
                o_ref[t * bs:(t + 1) * bs, h * dm.dv:(h + 1) * dm.dv] = ot_ref[t, h * dm.dv:(h + 1) * dm.dv, :].T


def _sample_state(dm, z_s, pool2d, gs3, wal, bal, bs, t_s):
    ch = 16
    n_i = dm.hk // ch
    past = POOL_HIST - 1
    kern = functools.partial(_sst_kernel, dm=dm, bs=bs, t_s=t_s, ch=ch, pos0=PAST_LEN)
    full = lambda shape: pl.BlockSpec(shape, lambda i: (0,) * len(shape))
    return pl.pallas_call(
        kern, grid=(n_i,),
        in_specs=[_const_spec(z_s.shape), _const_spec(pool2d.shape),
                  pl.BlockSpec((bs, ch, dm.dv), lambda i: (0, i, 0)),
                  _const_spec(wal.shape), _const_spec(bal.shape)],
        out_specs=[full((t_s * bs, dm.pw)), full((t_s * bs, dm.gw)), full((bs, past * dm.pw)),
                   pl.BlockSpec((bs, ch, dm.dv), lambda i: (0, i, 0))],
        out_shape=[jax.ShapeDtypeStruct((t_s * bs, dm.pw), F32), jax.ShapeDtypeStruct((t_s * bs, dm.gw), F32),
                   jax.ShapeDtypeStruct((bs, past * dm.pw), F32), jax.ShapeDtypeStruct(gs3.shape, F32)],
        scratch_shapes=[pltpu.VMEM((t_s, dm.hk, bs), F32), pltpu.VMEM((t_s, dm.hk, bs), F32),
                        pltpu.VMEM((t_s, dm.hk, bs), F32), pltpu.VMEM((t_s, dm.gw, bs), F32),
                        pltpu.VMEM((t_s, dm.gw, bs), F32)],
        compiler_params=_cp(("arbitrary",)), name="sample_state",
    )(z_s, pool2d, gs3, wal, bal)


def _spost_kernel(x_ref, mod_ref, z_ref, mixed_ref, o_ref, pw_ref, ps_ref, gn_ref,
                  wa_ref, wb_ref, wo_ref, n2_ref, wrh_ref, wrl_ref,
                  x1_ref, h2_ref, lg_ref, *, dm):
    gw = dm.pw // len(POOL_WINDOWS)
    ya = _pool_project([mixed_ref[:, g * gw:(g + 1) * gw] for g in range(len(POOL_WINDOWS))], pw_ref, ps_ref)
    x1, h2, logits = _post_mix(
        dm, x_ref[...], mod_ref[...], ya, o_ref[...],
        z_ref[:, dm.go0:dm.go0 + dm.gw], z_ref[:, dm.ga0:dm.ga0 + dm.d], z_ref[:, dm.gb0:dm.gb0 + dm.d],
        gn_ref, wa_ref, wb_ref, wo_ref, n2_ref, wrh_ref, wrl_ref)
    x1_ref[...] = x1
    h2_ref[...] = h2
    lg_ref[...] = logits


def _sample_post(dm, xs2d, mod_s, z_s, mixed_s, o_s, wts, bs, t_s, n_exp):
    d = dm.d
    rows = lambda t: (t, 0)
    in_specs = [pl.BlockSpec((bs, d), lambda t: (0, t)), _const_spec(mod_s.shape),
                pl.BlockSpec((bs, dm.zw), rows), pl.BlockSpec((bs, dm.pw), rows), pl.BlockSpec((bs, dm.gw), rows)]
    in_specs += [_const_spec(w.shape) for w in wts]
    return pl.pallas_call(
        functools.partial(_spost_kernel, dm=dm), grid=(t_s,), in_specs=in_specs,
        out_specs=[pl.BlockSpec((bs, d), rows), pl.BlockSpec((bs, d // 2), rows), pl.BlockSpec((bs, n_exp), rows)],
        out_shape=[jax.ShapeDtypeStruct((t_s * bs, d), F32), jax.ShapeDtypeStruct((t_s * bs, d // 2), I32),
                   jax.ShapeDtypeStruct((t_s * bs, n_exp), F32)],
        compiler_params=_cp(("arbitrary",)), name="sample_post",
    )(xs2d, mod_s, z_s, mixed_s, o_s, *wts)


def _route_kernel(lgp_ref, lgs_ref, bias_ref, idx_ref, w_ref, cnt_ref, *, n_exp, tiles_p):
    i = pl.program_id(0)
    s = jax.nn.sigmoid(jnp.where(i < tiles_p, lgp_ref[...], lgs_ref[...]))
    cur = s + bias_ref[...]
    tr = s.shape[0]
    lane = lax.broadcasted_iota(I32, (1, n_exp), 1).astype(F32)
    slot = lax.broadcasted_iota(I32, (1, LANE), 1)
    idx8 = jnp.zeros((tr, LANE), F32)
    w8 = jnp.zeros((tr, LANE), F32)
    wsum = jnp.zeros((tr, 1), F32)
    picked = jnp.zeros((tr, n_exp), F32)
    for k in range(TOP_K):
        m = jnp.max(cur, axis=-1, keepdims=True)
        am = jnp.min(jnp.where(cur == m, lane, float(n_exp)), axis=-1, keepdims=True)
        oh = lane == am
        sk = jnp.sum(jnp.where(oh, s, 0.0), axis=-1, keepdims=True)
        cur = jnp.where(oh, -jnp.inf, cur)
        idx8 = jnp.where(slot == k, am, idx8)
        w8 = jnp.where(slot == k, sk, w8)
        wsum = wsum + sk
        picked = picked + jnp.where(oh, 1.0, 0.0)
    idx_ref[...] = idx8.astype(I32)
    w_ref[...] = w8 / wsum * ROUTED_SCALE

    @pl.when(i == 0)
    def _():
        cnt_ref[...] = jnp.zeros(cnt_ref.shape, F32)

    cnt_ref[...] += jnp.sum(picked, axis=0, keepdims=True)


def _two_source_specs(block, tiles_p, tiles_s):
    return [pl.BlockSpec(block, lambda i: (jnp.minimum(i, tiles_p - 1), 0)),
            pl.BlockSpec(block, lambda i: (jnp.clip(i - tiles_p, 0, tiles_s - 1), 0))]


def _route(lg_p, lg_s, bias, tr):
    n_exp = lg_p.shape[1]
    tiles_p, tiles_s = lg_p.shape[0] // tr, lg_s.shape[0] // tr
    n = (tiles_p + tiles_s) * tr
    rows = lambda i: (i, 0)
    return pl.pallas_call(
        functools.partial(_route_kernel, n_exp=n_exp, tiles_p=tiles_p), grid=(tiles_p + tiles_s,),
        in_specs=_two_source_specs((tr, n_exp), tiles_p, tiles_s) + [_const_spec((1, n_exp))],
        out_specs=[pl.BlockSpec((tr, LANE), rows), pl.BlockSpec((tr, LANE), rows),
                   pl.BlockSpec((8, n_exp), lambda i: (0, 0))],
        out_shape=[jax.ShapeDtypeStruct((n, LANE), I32), jax.ShapeDtypeStruct((n, LANE), F32),
                   jax.ShapeDtypeStruct((8, n_exp), F32)],
        compiler_params=_cp(("arbitrary",)), name="route",
    )(lg_p, lg_s, bias.reshape(1, n_exp))


def _plan_kernel(idx_ref, cnt_ref, pos_ref, bounds_ref, start_ref, run_ref, *, n_exp):
    i = pl.program_id(0)
    lane = lax.broadcasted_iota(I32, (1, n_exp), 1)

    @pl.when(i == 0)
    def _():
        cnt = cnt_ref[...].astype(I32)
        padded = ((cnt + (EXPERT_BLOCK - 1)) // EXPERT_BLOCK) * EXPERT_BLOCK
        pe = padded
        s = 1
        while s < n_exp:
            pe = pe + jnp.where(lane >= s, pltpu.roll(pe, s, 1), 0)
            s *= 2
        start_ref[...] = (pe - padded).astype(F32)
        run_ref[...] = jnp.zeros(run_ref.shape, F32)
        row = lax.broadcasted_iota(I32, (8, n_exp), 0)
        bounds_ref[...] = jnp.where(row == 0, pe - padded, pe)

    idx = idx_ref[...]
    tr = idx.shape[0]
    onehots = [lane == idx[:, k:k + 1] for k in range(TOP_K)]
    member = onehots[0]
    for k in range(1, TOP_K):
        member = member | onehots[k]
    mb = jnp.where(member, 1.0, 0.0).astype(BF16)
    r_i = lax.broadcasted_iota(I32, (tr, tr), 0)
    c_i = lax.broadcasted_iota(I32, (tr, tr), 1)
    earlier = jnp.where(r_i > c_i, 1.0, 0.0).astype(BF16)
    dest = _dot(earlier, mb) + (start_ref[0:1, :] + run_ref[0:1, :])
    slot = lax.broadcasted_iota(I32, (1, LANE), 1)
    pos8 = jnp.zeros((tr, LANE), F32)
    for k in range(TOP_K):
        pk = jnp.sum(jnp.where(onehots[k], dest, 0.0), axis=-1, keepdims=True)
        pos8 = jnp.where(slot == k, pk, pos8)
    pos_ref[...] = pos8.T[0:TOP_K, :].astype(I32)
    run_ref[...] += jnp.sum(mb.astype(F32), axis=0, keepdims=True)


def _plan(idx8, counts, tr):
    n = idx8.shape[0]
    n_exp = counts.shape[1]
    rows = lambda i: (i, 0)
    return pl.pallas_call(
        functools.partial(_plan_kernel, n_exp=n_exp), grid=(n // tr,),
        in_specs=[pl.BlockSpec((tr, LANE), rows), _const_spec((8, n_exp))],
        out_specs=[pl.BlockSpec((TOP_K, tr), lambda i: (0, i)), pl.BlockSpec((8, n_exp), lambda i: (0, 0))],
        out_shape=[jax.ShapeDtypeStruct((TOP_K, n), I32), jax.ShapeDtypeStruct((8, n_exp), I32)],
        scratch_shapes=[pltpu.VMEM((8, n_exp), F32), pltpu.VMEM((8, n_exp), F32)],
        compiler_params=_cp(("arbitrary",)), name="plan",
    )(idx8, counts)


SC_CORES = 2
SC_SUBCORES = 16
SC_AXES = ("core", "subcore")
SC_WORKERS = SC_CORES * SC_SUBCORES
SC_WINDOW = 128


def _sc_mesh():
    return plsc.VectorSubcoreMesh(core_axis_name=SC_AXES[0], subcore_axis_name=SC_AXES[1],
                                  num_cores=SC_CORES, num_subcores=SC_SUBCORES)


def _sc_worker():
    return lax.axis_index(SC_AXES[1]) * SC_CORES + lax.axis_index(SC_AXES[0])


def _dispatch(h_p, h_s, pos_t, n_rows):
    n_p, c = h_p.shape
    n_s = h_s.shape[0]
    win_p, win_s = n_p // SC_WINDOW, n_s // SC_WINDOW
    assert n_p % (SC_WINDOW * SC_WORKERS) == 0 and n_s % SC_WINDOW == 0 and win_s <= SC_WORKERS

    @functools.partial(
        pl.kernel, out_type=jax.ShapeDtypeStruct((n_rows, c), h_p.dtype), mesh=_sc_mesh(), name="dispatch",
        scratch_types=[pltpu.VMEM((TOP_K, SC_WINDOW), I32), pltpu.VMEM((SC_WINDOW, c), h_p.dtype),
                       pltpu.SemaphoreType.DMA])
    def run(hp_hbm, hs_hbm, pos_hbm, xs_hbm, idx_v, rows_v, sem):
        wid = _sc_worker()

        def scatter_window(h_hbm, row0, tok0):
            pltpu.sync_copy(h_hbm.at[pl.ds(row0, SC_WINDOW)], rows_v)
            pltpu.sync_copy(pos_hbm.at[:, pl.ds(tok0, SC_WINDOW)], idx_v)
            copies = [pltpu.async_copy(rows_v, xs_hbm.at[idx_v.at[k]], sem) for k in range(TOP_K)]
            for cp in copies:
                cp.wait()

        def prompt_window(j, carry):
            row0 = pl.multiple_of((wid * (win_p // SC_WORKERS) + j) * SC_WINDOW, SC_WINDOW)
            scatter_window(hp_hbm, row0, row0)
            return carry
        lax.fori_loop(0, win_p // SC_WORKERS, prompt_window, 0)

        @pl.when(wid < win_s)
        def _():
            row0 = pl.multiple_of(wid * SC_WINDOW, SC_WINDOW)
            scatter_window(hs_hbm, row0, n_p + row0)

    return run(h_p, h_s, pos_t)


def _combine_gather(ys, pos_t):
    k_n = pos_t.shape[0] * pos_t.shape[1]
    c = ys.shape[1]
    win = SC_WINDOW // 2
    per_worker = k_n // (win * SC_WORKERS)
    assert k_n % (win * SC_WORKERS) == 0 and per_worker % 2 == 0
    dma = pltpu.SemaphoreType.DMA

    @functools.partial(
        pl.kernel, out_type=jax.ShapeDtypeStruct((k_n, c), ys.dtype), mesh=_sc_mesh(), name="combine_gather",
        scratch_types=[pltpu.VMEM((win,), I32), pltpu.VMEM((win,), I32),
                       pltpu.VMEM((win, c), ys.dtype), pltpu.VMEM((win, c), ys.dtype), dma, dma, dma, dma])
    def run(ys_hbm, idx_hbm, out_hbm, idx0, idx1, buf0, buf1, gsem0, gsem1, ssem0, ssem1):
        wid = _sc_worker()

        def rows(j):
            return pl.ds(pl.multiple_of((wid * per_worker + j) * win, win), win)

        def gather(j, idx_v, buf, sem):
            pltpu.sync_copy(idx_hbm.at[rows(j)], idx_v)
            pltpu.async_copy(ys_hbm.at[idx_v], buf, sem)

        def gather_wait(idx_v, buf, sem):
            pltpu.make_async_copy(ys_hbm.at[idx_v], buf, sem).wait()

        def store(j, buf, sem):
            pltpu.async_copy(buf, out_hbm.at[rows(j)], sem)

        def store_wait(buf, sem):
            pltpu.make_async_copy(buf, out_hbm.at[rows(0)], sem).wait()

        gather(0, idx0, buf0, gsem0)

        def two_windows(i, carry):
            j = 2 * i

            @pl.when(i > 0)
            def _():
                store_wait(buf1, ssem1)

            gather(j + 1, idx1, buf1, gsem1)
            gather_wait(idx0, buf0, gsem0)
            store(j, buf0, ssem0)

            @pl.when(j + 2 < per_worker)
            def _():
                store_wait(buf0, ssem0)
                gather(j + 2, idx0, buf0, gsem0)

            gather_wait(idx1, buf1, gsem1)
            store(j + 1, buf1, ssem1)
            return carry
        lax.fori_loop(0, per_worker // 2, two_windows, 0)
        store_wait(buf0, ssem0)
        store_wait(buf1, ssem1)

    return run(ys, pos_t.reshape(k_n))


MOE_RING = 16
MOE_WEIGHT_RING = 8


def _moe_kernel(st_ref, wg_ref, wu_ref, wd_ref, xs_ref, ys_ref, xbuf, ybuf, wgf, wuf, wdf, wgb, wub, wdb,
                semx, semy, semw):
    e = pl.program_id(0)
    n_e = pl.num_programs(0)

    def w_copies(ex):
        slot = ex % MOE_WEIGHT_RING
        return [pltpu.make_async_copy(src.at[ex], dst.at[slot], semw.at[slot, i])
                for i, (src, dst) in enumerate(((wg_ref, wgf), (wu_ref, wuf), (wd_ref, wdf)))]

    @pl.when(e == 0)
    def _():
        for ex in range(MOE_WEIGHT_RING - 1):
            @pl.when(ex < n_e)
            def _(ex=ex):
                for cp in w_copies(ex):
                    cp.start()

    @pl.when(e + (MOE_WEIGHT_RING - 1) < n_e)
    def _():
        for cp in w_copies(e + (MOE_WEIGHT_RING - 1)):
            cp.start()
    g0 = st_ref[e] // EXPERT_BLOCK
    nchunks = st_ref[e + 1] // EXPERT_BLOCK - g0
    ntot = st_ref[n_e] // EXPERT_BLOCK

    def rows(g):
        return pl.ds(pl.multiple_of(g * EXPERT_BLOCK, EXPERT_BLOCK), EXPERT_BLOCK)

    def x_copy(g):
        slot = g % MOE_RING
        return pltpu.make_async_copy(xs_ref.at[rows(g)], xbuf.at[slot], semx.at[slot])

    def y_copy(g):
        slot = g % MOE_RING
        return pltpu.make_async_copy(ybuf.at[slot], ys_ref.at[rows(g)], semy.at[slot])

    @pl.when(e == 0)
    def _():
        for g in range(MOE_RING - 1):
            @pl.when(g < ntot)
            def _(g=g):
                x_copy(g).start()

    for cp in w_copies(e):
        cp.wait()
    wslot = e % MOE_WEIGHT_RING
    wgb[...] = wgf[wslot].astype(BF16)
    wub[...] = wuf[wslot].astype(BF16)
    wdb[...] = wdf[wslot].astype(BF16)

    def take(g):
        @pl.when(g + (MOE_RING - 1) < ntot)
        def _():
            x_copy(g + (MOE_RING - 1)).start()

        x_copy(g).wait()

        @pl.when(g >= MOE_RING)
        def _():
            y_copy(g - MOE_RING).wait()

        return xbuf[g % MOE_RING]

    def expert_mlp(x_words):
        x = _unpack_rows(x_words).astype(BF16)
        mid = _silu(_dot(x, wgb[...])) * _dot(x, wub[...])
        return _pack_rows(_dot(mid.astype(BF16), wdb[...]))

    def put(g, y_words):
        ybuf[g % MOE_RING] = y_words
        y_copy(g).start()

    def run_chunks(g, m):
        xs = [take(g + i) for i in range(m)]
        y = expert_mlp(xs[0] if m == 1 else jnp.concatenate(xs, axis=0))
        for i in range(m):
            put(g + i, y[i * EXPERT_BLOCK:(i + 1) * EXPERT_BLOCK])

    def quad(p, carry):
        run_chunks(g0 + 4 * p, 4)
        return carry

    rest = nchunks % 4
    five = (rest == 1) & (nchunks >= 5)
    n_quads = nchunks // 4 - five.astype(I32)
    lax.fori_loop(0, n_quads, quad, 0)
    g_rest = g0 + 4 * n_quads

    @pl.when(five)
    def _():
        run_chunks(g_rest, 5)

    @pl.when(rest >= 2)
    def _():
        run_chunks(g_rest, 2)

    @pl.when((rest == 3) | (nchunks == 1))
    def _():
        run_chunks(g0 + nchunks - 1, 1)

    @pl.when(e == n_e - 1)
    def _():
        for j in range(MOE_RING):
            g = ntot - MOE_RING + j

            @pl.when(g >= 0)
            def _(g=g):
                y_copy(g).wait()


def _moe(starts, xs, wg, wu, wd):
    n_rows, c = xs.shape
    n_exp, d, f = wg.shape
    grid_spec = pltpu.PrefetchScalarGridSpec(
        num_scalar_prefetch=1, grid=(n_exp,),
        in_specs=[pl.BlockSpec(memory_space=pl.ANY)] * 4,
        out_specs=pl.BlockSpec(memory_space=pl.ANY),
        scratch_shapes=[pltpu.VMEM((MOE_RING, EXPERT_BLOCK, c), I32), pltpu.VMEM((MOE_RING, EXPERT_BLOCK, c), I32),
                        pltpu.VMEM((MOE_WEIGHT_RING, d, f), F32), pltpu.VMEM((MOE_WEIGHT_RING, d, f), F32),
                        pltpu.VMEM((MOE_WEIGHT_RING, f, d), F32),
                        pltpu.VMEM((d, f), BF16), pltpu.VMEM((d, f), BF16), pltpu.VMEM((f, d), BF16),
                        pltpu.SemaphoreType.DMA((MOE_RING,)), pltpu.SemaphoreType.DMA((MOE_RING,)),
                        pltpu.SemaphoreType.DMA((MOE_WEIGHT_RING, 3))])
    return pl.pallas_call(
        _moe_kernel, grid_spec=grid_spec, out_shape=jax.ShapeDtypeStruct((n_rows, c), I32),
        compiler_params=_cp(("arbitrary",)), name="moe",
    )(starts, wg, wu, wd, xs)


def _final_kernel(yg_ref, w8_ref, x1_ref, h2_ref, mod_ref, wsg_ref, wsu_ref, wsd_ref, nf_ref, y_ref, *, d):
    hb = _unpack_rows(h2_ref[...]).astype(BF16)
    shared = _dot((_silu(_dot(hb, wsg_ref[...])) * _dot(hb, wsu_ref[...])).astype(BF16), wsd_ref[...])
    w8 = w8_ref[...]
    routed = _unpack_rows(yg_ref[0]) * w8[:, 0:1]
    for k in range(1, TOP_K):
        routed = routed + _unpack_rows(yg_ref[k]) * w8[:, k:k + 1]
    g2 = mod_ref[...][:, (N_MOD - 1) * d:N_MOD * d]
    x2 = x1_ref[...] + g2 * (routed + shared)
    y_ref[...] = _rms(x2, nf_ref[...])


def _final(yg3, w8, x1, h2, mod, mod_spec, wts, nf, tf, n_tiles, tile0, out_shape, out_spec):
    d = x1.shape[1]
    rows = lambda i: (i, 0)
    in_specs = [pl.BlockSpec((TOP_K, tf, d // 2), lambda i: (0, tile0 + i, 0)),
                pl.BlockSpec((tf, LANE), lambda i: (tile0 + i, 0)),
                pl.BlockSpec((tf, d), rows), pl.BlockSpec((tf, d // 2), rows), mod_spec]
    in_specs += [_const_spec(w.shape) for w in wts]
    in_specs += [_const_spec(nf.shape)]
    return pl.pallas_call(
        functools.partial(_final_kernel, d=d), grid=(n_tiles,), in_specs=in_specs,
        out_specs=out_spec, out_shape=out_shape,
        compiler_params=_cp(("arbitrary",)), name="final",
    )(yg3, w8, x1, h2, mod, *wts, nf)


def kernel(x_prompt, x_sample, c_prompt, c_sample, state_pool, state_gla, w_ada, b_ada, norm1, w_in,
           pool_w, pool_scale, gla_w_alpha, gla_b_alpha, gla_norm, w_branch_a, w_branch_b, w_out, norm2,
           w_router, router_bias, w_exp_gate, w_exp_up, w_exp_down, w_sh_gate, w_sh_up, w_sh_down, norm_f):
    depth = w_in.shape[0]
    assert depth == 1, "single-layer trunk"
    bp, seq, d = x_prompt.shape
    bs, t_s, _ = x_sample.shape
    _, _, past, pw = state_pool.shape
    _, _, heads, dk, dv = state_gla.shape
    rank = gla_w_alpha.shape[1]
    n_exp = w_router.shape[2]
    assert past == POOL_HIST - 1 and pw == len(POOL_WINDOWS) * LANE and dv == LANE and rank <= LANE
    dm = _Dims(d, pw, heads, dk, dv)
    n_p, n_s = bp * seq, bs * t_s
    n_all = n_p + n_s

    off = [0]
    for sz in (pw, dm.hk, dm.hk, dm.gw, dm.gw, rank, d, d):
        off.append(off[-1] + sz)
    wi = w_in[0]
    win = (wi[:, :off[5]].astype(BF16), wi[:, off[6]:].astype(BF16),
           jnp.concatenate([wi[:, off[5]:off[6]], jnp.zeros((d, LANE - rank), F32)], axis=1).astype(BF16))
    assert off[5] + 2 * d + LANE == dm.zw
    wal = jnp.concatenate([gla_w_alpha[0], jnp.zeros((LANE - rank, dm.hk), F32)], axis=0).astype(BF16)
    bal = gla_b_alpha[0].reshape(1, dm.hk)
    wr = w_router[0]
    wrh = wr.astype(BF16)
    wrl = (wr - wrh.astype(F32)).astype(BF16)
    post_w = (pool_w[0].astype(BF16), pool_scale[0].reshape(1, pw), gla_norm[0].reshape(1, dm.gw),
              w_branch_a[0].astype(BF16), w_branch_b[0].astype(BF16), w_out[0].astype(BF16),
              norm2[0].reshape(1, d), wrh, wrl)
    n1 = norm1[0].reshape(1, d)

    mod = _ada(jnp.concatenate([c_prompt, c_sample], axis=0), w_ada[0], b_ada[0])
    mod_p = mod[:bp].reshape(bp, 1, N_MOD * d)
    mod_s = mod[bp:]

    x1_p, h2_p, lg_p, npool_p, ngla_p = _mix_prompt(
        dm, x_prompt.reshape(n_p, d), mod_p, (n1,) + win + (wal, bal) + post_w, bp, seq, n_exp)

    xs2d = x_sample.reshape(bs, t_s * d)
    z_s = _sample_in(dm, xs2d, mod_s, n1, win, bs, t_s)
    mixed_s, o_s, npool_s, ngla_s = _sample_state(
        dm, z_s, state_pool[0].reshape(bs, past * pw), state_gla[0].reshape(bs, dm.hk, dv), wal, bal, bs, t_s)
    x1_s, h2_s, lg_s = _sample_post(dm, xs2d, mod_s, z_s, mixed_s, o_s, post_w, bs, t_s, n_exp)

    tr = 512 if (n_p % 512 == 0 and n_s % 512 == 0) else LANE
    assert n_p % tr == 0 and n_s % tr == 0 and n_p % bs == 0
    idx8, w8, counts = _route(lg_p, lg_s, router_bias[0], tr)
    nblk = (n_all * TOP_K) // EXPERT_BLOCK + n_exp
    pos_t, bounds = _plan(idx8, counts, tr)
    starts = jnp.concatenate([bounds[0], bounds[1, n_exp - 1:]])

    xs = _dispatch(h2_p, h2_s, pos_t, nblk * EXPERT_BLOCK)
    ys = _moe(starts, xs, w_exp_gate[0], w_exp_up[0], w_exp_down[0])
    yg3 = _combine_gather(ys, pos_t).reshape(TOP_K, n_all, d // 2)

    sh_w = (w_sh_gate[0].astype(BF16), w_sh_up[0].astype(BF16), w_sh_down[0].astype(BF16))
    nf = norm_f.reshape(1, d)
    tf = 512 if seq % 512 == 0 else seq
    n_tp = seq // tf
    y_p = _final(
        yg3, w8, x1_p, h2_p, mod_p,
        pl.BlockSpec((None, 1, N_MOD * d), lambda i: (i // n_tp, 0, 0)), sh_w, nf, tf, n_p // tf, 0,
        jax.ShapeDtypeStruct((n_p, d), F32), pl.BlockSpec((tf, d), lambda i: (i, 0)))
    y_s = _final(
        yg3, w8, x1_s, h2_s, mod_s,
        _const_spec(mod_s.shape), sh_w, nf, bs, t_s, n_p // bs,
        jax.ShapeDtypeStruct((bs, t_s * d), F32), pl.BlockSpec((bs, d), lambda i: (0, i)))

    return (y_p.reshape(bp, seq, d), y_s.reshape(bs, t_s, d),
            npool_p.reshape(depth, bp, past, pw), ngla_p.reshape(depth, bp, heads, dk, dv),
            npool_s.reshape(depth, bs, past, pw), ngla_s.reshape(depth, bs, heads, dk, dv))
```

```python
import functools

import jax
import jax.numpy as jnp
from jax import lax
from jax.experimental import pallas as pl
from jax.experimental.pallas import tpu as pltpu
from jax.experimental.pallas import tpu_sc as plsc

F32 = jnp.float32
BF16 = jnp.bfloat16
I32 = jnp.int32

EPS = 1e-6
N_MOD = 6
POOL_WINDOWS = (2, 4, 8, 16)
POOL_HIST = 16
GLA_GATE_TEMP = 16.0
TOP_K = 8
ROUTED_SCALE = 2.5
PAST_LEN = 16384
EXPERT_BLOCK = 128
LANE = 128
GLA_FAST_MAX_DECAY = 40.0
PROJ_PIECE = 512
PROJ_LEAD = 2
VMEM_LIMIT = 56 * 1024 * 1024


def _cp(sem, vmem=VMEM_LIMIT):
    return pltpu.CompilerParams(dimension_semantics=sem, vmem_limit_bytes=vmem)


def _rms(x, g):
    return x * lax.rsqrt(jnp.mean(x * x, axis=-1, keepdims=True) + EPS) * g


def _silu(x):
    return x * jax.nn.sigmoid(x)


def _dot(a, b):
    return jnp.dot(a, b, preferred_element_type=F32)


def _split3(a):
    a0 = a.astype(BF16)
    r1 = a - a0.astype(F32)
    a1 = r1.astype(BF16)
    a2 = (r1 - a1.astype(F32)).astype(BF16)
    return a0, a1, a2


def _mod_parts(mod, d):
    return [mod[:, i * d:(i + 1) * d] for i in range(N_MOD)]


_HI16 = -65536


def _pack_rows(v):
    c = v.shape[1] // 2
    lo = lax.bitcast_convert_type(v[:, :c].astype(BF16).astype(F32), I32)
    hi = lax.bitcast_convert_type(v[:, c:].astype(BF16).astype(F32), I32)
    return (hi & _HI16) | lax.shift_right_logical(lo, 16)


def _unpack_rows(w):
    lo = lax.bitcast_convert_type(lax.shift_left(w, 16), F32)
    hi = lax.bitcast_convert_type(w & _HI16, F32)
    return jnp.concatenate([lo, hi], axis=1)


def _ada_kernel(c_ref, w_ref, b_ref, o_ref):
    a = _silu(c_ref[...]).astype(BF16)
    o_ref[...] = _dot(a, w_ref[...].astype(BF16)) + b_ref[...]


def _ada(c_all, w_ada, b_ada):
    n, d = c_all.shape
    cols = w_ada.shape[1]
    bc = d
    return pl.pallas_call(
        _ada_kernel,
        grid=(cols // bc,),
        in_specs=[pl.BlockSpec((n, d), lambda i: (0, 0)),
                  pl.BlockSpec((d, bc), lambda i: (0, i)),
                  pl.BlockSpec((1, bc), lambda i: (0, i))],
        out_specs=pl.BlockSpec((n, bc), lambda i: (0, i)),
        out_shape=jax.ShapeDtypeStruct((n, cols), F32),
        compiler_params=_cp(("arbitrary",)),
        name="ada",
    )(c_all, w_ada, b_ada.reshape(1, cols))


class _Dims:
    def __init__(self, d, pw, heads, dk, dv):
        self.d, self.pw, self.h, self.dk, self.dv = d, pw, heads, dk, dv
        self.hk, self.gw = heads * dk, heads * dv
        self.q0 = pw
        self.k0 = self.q0 + self.hk
        self.v0 = self.k0 + self.hk
        self.go0 = self.v0 + self.gw
        self.ga0 = self.go0 + self.gw
        self.gb0 = self.ga0 + d
        self.al0 = self.gb0 + d
        self.zw = self.al0 + LANE


def _log_decay(z_al, wal_ref, bal_ref):
    xal = _dot(z_al.astype(BF16), wal_ref[...]) + bal_ref[...]
    return jax.nn.log_sigmoid(xal) / GLA_GATE_TEMP


def _pool_project(mixed_groups, pw_ref, ps_ref):
    ys = [_dot(m.astype(BF16), pw_ref[g]) for g, m in enumerate(mixed_groups)]
    return jnp.concatenate(ys, axis=1) * ps_ref[...]


def _post_mix(dm, x, mod, ya, o, go, ga, gb, gn_ref, wa_ref, wb_ref, wo_ref, n2_ref, wrh_ref, wrl_ref,
              filler=lambda: None):
    _, _, g1, sh2, sc2, _ = _mod_parts(mod, dm.d)
    parts = []
    for h in range(dm.h):
        oh = o[:, h * dm.dv:(h + 1) * dm.dv]
        parts.append(oh * lax.rsqrt(jnp.mean(oh * oh, axis=-1, keepdims=True) + EPS))
    yb = jnp.concatenate(parts, axis=1) * gn_ref[...] * _silu(go)
    filler()
    m = (jax.nn.sigmoid(ga) * _dot(ya.astype(BF16), wa_ref[...])
         + jax.nn.sigmoid(gb) * _dot(yb.astype(BF16), wb_ref[...]))
    filler()
    x1 = x + g1 * _dot(m.astype(BF16), wo_ref[...])
    h2 = _rms(x1, n2_ref[...]) * (1 + sc2) + sh2
    hi = h2.astype(BF16)
    lo = (h2 - hi.astype(F32)).astype(BF16)
    filler()
    logits = _dot(hi, wrh_ref[...]) + (_dot(hi, wrl_ref[...]) + _dot(lo, wrh_ref[...]))
    filler()
    return x1, _pack_rows(h2), logits


def _in_project(dm, x, mod, n1_ref, win_refs, z_ref):
    h = (_rms(x, n1_ref[...]) * (1 + mod[:, dm.d:2 * dm.d]) + mod[:, 0:dm.d]).astype(BF16)
    starts = [0]
    for w_ref in win_refs:
        starts.append(starts[-1] + w_ref.shape[1])
    for i in (0, 2, 1):
        z_ref[:, starts[i]:starts[i + 1]] = _dot(h, win_refs[i][...])


def _mixp_kernel(x_ref, mod_ref, n1_ref, wia_ref, wig_ref, wil_ref, wal_ref, bal_ref, pw_ref, ps_ref, gn_ref,
                 wa_ref, wb_ref, wo_ref, n2_ref, wrh_ref, wrl_ref,
                 x1_ref, h2_ref, lg_ref, np_ref, ng_ref,
                 za_ref, zb_ref, xk_ref, mk_ref, hn_ref, uext_ref, s_ref, b_ref, oi_ref, *, dm, tt, n_t):
    n = pl.program_id(0)
    j = jnp.maximum(n - 1, 0) % n_t

    @pl.when(n == 0)
    def _():
        za_ref[...] = jnp.zeros(za_ref.shape, F32)
        zb_ref[...] = jnp.zeros(zb_ref.shape, F32)
        xk_ref[...] = jnp.zeros(xk_ref.shape, F32)
        mk_ref[...] = jnp.zeros(mk_ref.shape, F32)

    @pl.when(j == 0)
    def _():
        uext_ref[0:POOL_HIST, :] = jnp.zeros((POOL_HIST, dm.pw), F32)
        s_ref[...] = jnp.zeros(s_ref.shape, F32)

    @pl.when(n % 2 == 0)
    def _():
        _mixp_region(za_ref, zb_ref, x_ref, mod_ref, n1_ref, wia_ref, wig_ref, wil_ref, wal_ref, bal_ref, pw_ref,
                     ps_ref, gn_ref, wa_ref, wb_ref, wo_ref, n2_ref, wrh_ref, wrl_ref, x1_ref, h2_ref, lg_ref,
                     xk_ref, mk_ref, hn_ref, uext_ref, s_ref, b_ref, oi_ref, dm=dm, tt=tt, j=j)

    @pl.when(n % 2 == 1)
    def _():
        _mixp_region(zb_ref, za_ref, x_ref, mod_ref, n1_ref, wia_ref, wig_ref, wil_ref, wal_ref, bal_ref, pw_ref,
                     ps_ref, gn_ref, wa_ref, wb_ref, wo_ref, n2_ref, wrh_ref, wrl_ref, x1_ref, h2_ref, lg_ref,
                     xk_ref, mk_ref, hn_ref, uext_ref, s_ref, b_ref, oi_ref, dm=dm, tt=tt, j=j)

    xk_ref[...] = x_ref[...]
    mk_ref[...] = mod_ref[...]

    @pl.when(j == n_t - 1)
    def _():
        np_ref[...] = uext_ref[tt + 1:tt + POOL_HIST, :]
        ng_ref[...] = s_ref[...]


def _mixp_region(zp_ref, z_ref, x_ref, mod_ref, n1_ref, wia_ref, wig_ref, wil_ref, wal_ref, bal_ref, pw_ref,
                 ps_ref, gn_ref, wa_ref, wb_ref, wo_ref, n2_ref, wrh_ref, wrl_ref, x1_ref, h2_ref, lg_ref,
                 xk_ref, mk_ref, hn_ref, uext_ref, s_ref, b_ref, oi_ref, *, dm, tt, j):
    scale = dm.dk ** -0.5
    lane = lax.broadcasted_iota(I32, (1, dm.hk), 1)
    head_masks = [(lane >= h * dm.dk) & (lane < (h + 1) * dm.dk) for h in range(dm.h)]
    causal = lax.broadcasted_iota(I32, (tt, tt), 0) >= lax.broadcasted_iota(I32, (tt, tt), 1)

    def pool_branch():
        pos = j * tt + lax.broadcasted_iota(I32, (tt, 1), 0)
        gw = dm.pw // len(POOL_WINDOWS)
        mixed = []
        for g, w in enumerate(POOL_WINDOWS):
            c0 = g * gw
            cur = uext_ref[POOL_HIST:POOL_HIST + tt, c0:c0 + gw]
            acc = cur
            for i in range(1, w):
                acc = acc + uext_ref[POOL_HIST - i:POOL_HIST - i + tt, c0:c0 + gw]
            cnt = jnp.minimum(w, pos + 1).astype(F32)
            mixed.append(acc / cnt - cur)
        return _pool_project(mixed, pw_ref, ps_ref)

    def scaled_queries(bc):
        return (z_ref[:, dm.q0:dm.q0 + dm.hk] * scale) * jnp.exp(bc)

    def inter_chunk(qe):
        s_bf = s_ref[...].astype(BF16)
        return jnp.concatenate(
            [_dot(jnp.where(head_masks[h], qe, 0.0).astype(BF16), s_bf) for h in range(dm.h)], axis=1)

    def finish(ya, o, filler=lambda: None):
        x1, h2, logits = _post_mix(
            dm, xk_ref[...], mk_ref[...], ya, o, z_ref[:, dm.go0:dm.go0 + dm.gw],
            z_ref[:, dm.ga0:dm.ga0 + dm.d], z_ref[:, dm.gb0:dm.gb0 + dm.d],
            gn_ref, wa_ref, wb_ref, wo_ref, n2_ref, wrh_ref, wrl_ref, filler)
        x1_ref[...] = x1
        h2_ref[...] = h2
        lg_ref[...] = logits

    mod_in = mod_ref[...]
    hn_ref[...] = (_rms(x_ref[...], n1_ref[...]) * (1 + mod_in[:, dm.d:2 * dm.d]) + mod_in[:, 0:dm.d]).astype(BF16)
    pieces = iter([(w_ref, a, min(a + PROJ_PIECE, w_ref.shape[1]), c0)
                   for w_ref, c0 in ((wia_ref, 0), (wil_ref, dm.al0), (wig_ref, dm.ga0))
                   for a in range(0, w_ref.shape[1], PROJ_PIECE)])

    def project_piece():
        piece = next(pieces, None)
        if piece is not None:
            w_ref, a, b, c0 = piece
            zp_ref[:, c0 + a:c0 + b] = _dot(hn_ref[...], w_ref[:, a:b])
        return piece is not None

    for _ in range(PROJ_LEAD):
        project_piece()
    uext_ref[POOL_HIST:POOL_HIST + tt, :] = z_ref[:, 0:dm.pw]
    ya = pool_branch()
    project_piece()
    la = _log_decay(z_ref[:, dm.al0:dm.al0 + LANE], wal_ref, bal_ref)
    tri = jnp.where(causal, 1.0, 0.0).astype(BF16)
    a0, a1, a2 = _split3(la)
    bc = _dot(tri, a0) + (_dot(tri, a1) + _dot(tri, a2))
    b_ref[...] = bc
    project_piece()
    qe = scaled_queries(bc)
    k = z_ref[:, dm.k0:dm.k0 + dm.hk]
    vb = z_ref[:, dm.v0:dm.v0 + dm.gw].astype(BF16)
    blast = bc[tt - 1:tt, :]
    fast = jnp.max(-blast) <= GLA_FAST_MAX_DECAY
    ke = (k * jnp.exp(-bc)).astype(BF16)
    intra = []
    for h in range(dm.h):
        qh = jnp.where(head_masks[h], qe, 0.0).astype(BF16)
        sc = lax.dot_general(qh, ke, (((1,), (1,)), ((), ())), preferred_element_type=F32)
        sc = jnp.where(causal, sc, 0.0).astype(BF16)
        intra.append(_dot(sc, vb[:, h * dm.dv:(h + 1) * dm.dv]))
    project_piece()
    finish(ya, inter_chunk(qe) + jnp.concatenate(intra, axis=1), project_piece)
    while project_piece():
        pass

    @pl.when(jnp.logical_not(fast))
    def _():
        rows = lax.broadcasted_iota(I32, (tt, 1), 0)
        for h in range(dm.h):
            def body(tb, carry, h=h):
                base = pl.multiple_of(tb * 8, 8)
                q8 = z_ref[pl.ds(base, 8), dm.q0:dm.q0 + dm.hk] * scale
                b8 = b_ref[pl.ds(base, 8), :]
                vh = z_ref[:, dm.v0 + h * dm.dv:dm.v0 + (h + 1) * dm.dv]
                out_rows = []
                for r in range(8):
                    keep = (rows <= base + r) & head_masks[h]
                    dec = jnp.exp(jnp.where(keep, b8[r:r + 1, :] - b_ref[...], -jnp.inf))
                    wgt = (q8[r:r + 1, :] * dec) * z_ref[:, dm.k0:dm.k0 + dm.hk]
                    s_col = jnp.sum(wgt, axis=-1, keepdims=True)
                    out_rows.append(jnp.sum(s_col * vh, axis=0, keepdims=True))
                oi_ref[pl.ds(base, 8), h * dm.dv:(h + 1) * dm.dv] = jnp.concatenate(out_rows, axis=0)
                return carry
            lax.fori_loop(0, tt // 8, body, 0)
        finish(pool_branch(), inter_chunk(scaled_queries(b_ref[...])) + oi_ref[...])

    bc = b_ref[...]
    blast = bc[tt - 1:tt, :]
    kd = z_ref[:, dm.k0:dm.k0 + dm.hk] * jnp.exp(blast - bc)
    u_all = _dot(kd.T.astype(BF16), z_ref[:, dm.v0:dm.v0 + dm.gw].astype(BF16))
    dec_col = jnp.broadcast_to(jnp.exp(blast), (dm.hk, dm.hk)).T[:, 0:dm.dv]
    upd = jnp.concatenate(
        [u_all[h * dm.dk:(h + 1) * dm.dk, h * dm.dv:(h + 1) * dm.dv] for h in range(dm.h)], axis=0)
    s_ref[...] = dec_col * s_ref[...] + upd
    uext_ref[0:POOL_HIST, :] = uext_ref[tt:tt + POOL_HIST, :]


def _const_spec(shape):
    nd = len(shape)
    return pl.BlockSpec(shape, lambda *_: (0,) * nd, pipeline_mode=pl.Buffered(1))


def _mix_prompt(dm, x2d, mod3, wts, bsz, seq, n_exp):
    tt = 256 if seq % 256 == 0 else seq
    n_t = seq // tt
    n = bsz * seq
    d = dm.d
    kern = functools.partial(_mixp_kernel, dm=dm, tt=tt, n_t=n_t)
    tiles = bsz * n_t
    proj = lambda i: jnp.minimum(i, tiles - 1)
    fin = lambda i: jnp.maximum(i - 1, 0)
    row = lambda i: (fin(i), 0)
    in_specs = [pl.BlockSpec((tt, d), lambda i: (proj(i), 0)),
                pl.BlockSpec((None, 1, N_MOD * d), lambda i: (proj(i) // n_t, 0, 0))]
    in_specs += [_const_spec(w.shape) for w in wts]
    out_specs = [pl.BlockSpec((tt, d), row), pl.BlockSpec((tt, d // 2), row), pl.BlockSpec((tt, n_exp), row),
                 pl.BlockSpec((None, POOL_HIST - 1, dm.pw), lambda i: (fin(i) // n_t, 0, 0)),
                 pl.BlockSpec((None, dm.hk, dm.dv), lambda i: (fin(i) // n_t, 0, 0))]
    out_shape = [jax.ShapeDtypeStruct((n, d), F32), jax.ShapeDtypeStruct((n, d // 2), I32),
                 jax.ShapeDtypeStruct((n, n_exp), F32),
                 jax.ShapeDtypeStruct((bsz, POOL_HIST - 1, dm.pw), F32),
                 jax.ShapeDtypeStruct((bsz, dm.hk, dm.dv), F32)]
    scratch = [pltpu.VMEM((tt, dm.zw), F32), pltpu.VMEM((tt, dm.zw), F32),
               pltpu.VMEM((tt, d), F32), pltpu.VMEM((1, N_MOD * d), F32), pltpu.VMEM((tt, d), BF16),
               pltpu.VMEM((tt + POOL_HIST, dm.pw), F32),
               pltpu.VMEM((dm.hk, dm.dv), F32), pltpu.VMEM((tt, dm.hk), F32), pltpu.VMEM((tt, dm.gw), F32)]
    return pl.pallas_call(
        kern, grid=(tiles + 1,), in_specs=in_specs, out_specs=out_specs, out_shape=out_shape,
        scratch_shapes=scratch, compiler_params=_cp(("arbitrary",)), name="mixp",
    )(x2d, mod3, *wts)


def _sin_kernel(x_ref, mod_ref, n1_ref, wia_ref, wig_ref, wil_ref, z_ref, *, dm):
    _in_project(dm, x_ref[...], mod_ref[...], n1_ref, (wia_ref, wig_ref, wil_ref), z_ref)


def _sample_in(dm, xs2d, mod_s, n1, win, bs, t_s):
    d = dm.d
    return pl.pallas_call(
        functools.partial(_sin_kernel, dm=dm),
        grid=(t_s,),
        in_specs=[pl.BlockSpec((bs, d), lambda t: (0, t)), _const_spec(mod_s.shape), _const_spec(n1.shape)]
        + [_const_spec(w.shape) for w in win],
        out_specs=pl.BlockSpec((bs, dm.zw), lambda t: (t, 0)),
        out_shape=jax.ShapeDtypeStruct((t_s * bs, dm.zw), F32),
        compiler_params=_cp(("arbitrary",)), name="sample_in",
    )(xs2d, mod_s, n1, *win)


def _sst_kernel(z_ref, pool_ref, gs_ref, wal_ref, bal_ref,
                mixed_ref, o_ref, npool_ref, ngs_ref,
                qt_ref, kt_ref, dt_ref, vt_ref, ot_ref, *, dm, bs, t_s, ch, pos0):
    i = pl.program_id(0)
    n_i = pl.num_programs(0)
    scale = dm.dk ** -0.5
    past = POOL_HIST - 1

    @pl.when(i == 0)
    def _():
        for t in range(t_s):
            r0 = t * bs
            la = _log_decay(z_ref[r0:r0 + bs, dm.al0:dm.al0 + LANE], wal_ref, bal_ref)
            dt_ref[t] = jnp.exp(la).T
            qt_ref[t] = (z_ref[r0:r0 + bs, dm.q0:dm.q0 + dm.hk] * scale).T
            kt_ref[t] = z_ref[r0:r0 + bs, dm.k0:dm.k0 + dm.hk].T
            vt_ref[t] = z_ref[r0:r0 + bs, dm.v0:dm.v0 + dm.gw].T
        ot_ref[...] = jnp.zeros(ot_ref.shape, F32)

        def item(s, c0, c1):
            if s >= 0:
                return z_ref[s * bs:(s + 1) * bs, c0:c1]
            return pool_ref[:, (past + s) * dm.pw + c0:(past + s) * dm.pw + c1]

        gw = dm.pw // len(POOL_WINDOWS)
        for t in range(t_s):
            for g, w in enumerate(POOL_WINDOWS):
                c0, c1 = g * gw, (g + 1) * gw
                cur = item(t, c0, c1)
                acc = cur
                for r in range(1, w):
                    acc = acc + item(t - r, c0, c1)
                cnt = float(min(w, pos0 + t + 1))
                mixed_ref[t * bs:(t + 1) * bs, c0:c1] = acc / cnt - cur
        for r in range(past):
            npool_ref[:, r * dm.pw:(r + 1) * dm.pw] = item(r - past + t_s, 0, dm.pw)

    per_head = dm.dk // ch
    hoff = pl.multiple_of((i // per_head) * dm.dv, dm.dv)
    for p in range(ch):
        hd = i * ch + p
        st = gs_ref[:, p, :].T
        for t in range(t_s):
            st = dt_ref[t, pl.ds(hd, 1), :] * st + kt_ref[t, pl.ds(hd, 1), :] * vt_ref[t, pl.ds(hoff, dm.dv), :]
            ot_ref[t, pl.ds(hoff, dm.dv), :] += qt_ref[t, pl.ds(hd, 1), :] * st
        ngs_ref[:, p, :] = st.T

    @pl.when(i == n_i - 1)
    def _():
        for t in range(t_s):
            for h in range(dm.h):
                o_ref[t * bs:(t + 1) * bs, h * dm.dv:(h + 1) * dm.dv] = ot_ref[t, h * dm.dv:(h + 1) * dm.dv, :].T


def _sample_state(dm, z_s, pool2d, gs3, wal, bal, bs, t_s):
    ch = 16
    n_i = dm.hk // ch
    past = POOL_HIST - 1
    kern = functools.partial(_sst_kernel, dm=dm, bs=bs, t_s=t_s, ch=ch, pos0=PAST_LEN)
    full = lambda shape: pl.BlockSpec(shape, lambda i: (0,) * len(shape))
    return pl.pallas_call(
        kern, grid=(n_i,),
        in_specs=[_const_spec(z_s.shape), _const_spec(pool2d.shape),
                  pl.BlockSpec((bs, ch, dm.dv), lambda i: (0, i, 0)),
                  _const_spec(wal.shape), _const_spec(bal.shape)],
        out_specs=[full((t_s * bs, dm.pw)), full((t_s * bs, dm.gw)), full((bs, past * dm.pw)),
                   pl.BlockSpec((bs, ch, dm.dv), lambda i: (0, i, 0))],
        out_shape=[jax.ShapeDtypeStruct((t_s * bs, dm.pw), F32), jax.ShapeDtypeStruct((t_s * bs, dm.gw), F32),
                   jax.ShapeDtypeStruct((bs, past * dm.pw), F32), jax.ShapeDtypeStruct(gs3.shape, F32)],
        scratch_shapes=[pltpu.VMEM((t_s, dm.hk, bs), F32), pltpu.VMEM((t_s, dm.hk, bs), F32),
                        pltpu.VMEM((t_s, dm.hk, bs), F32), pltpu.VMEM((t_s, dm.gw, bs), F32),
                        pltpu.VMEM((t_s, dm.gw, bs), F32)],
        compiler_params=_cp(("arbitrary",)), name="sample_state",
    )(z_s, pool2d, gs3, wal, bal)


def _spost_kernel(x_ref, mod_ref, z_ref, mixed_ref, o_ref, pw_ref, ps_ref, gn_ref,
                  wa_ref, wb_ref, wo_ref, n2_ref, wrh_ref, wrl_ref,
                  x1_ref, h2_ref, lg_ref, *, dm):
    gw = dm.pw // len(POOL_WINDOWS)
    ya = _pool_project([mixed_ref[:, g * gw:(g + 1) * gw] for g in range(len(POOL_WINDOWS))], pw_ref, ps_ref)
    x1, h2, logits = _post_mix(
        dm, x_ref[...], mod_ref[...], ya, o_ref[...],
        z_ref[:, dm.go0:dm.go0 + dm.gw], z_ref[:, dm.ga0:dm.ga0 + dm.d], z_ref[:, dm.gb0:dm.gb0 + dm.d],
        gn_ref, wa_ref, wb_ref, wo_ref, n2_ref, wrh_ref, wrl_ref)
    x1_ref[...] = x1
    h2_ref[...] = h2
    lg_ref[...] = logits


def _sample_post(dm, xs2d, mod_s, z_s, mixed_s, o_s, wts, bs, t_s, n_exp):
    d = dm.d
    rows = lambda t: (t, 0)
    in_specs = [pl.BlockSpec((bs, d), lambda t: (0, t)), _const_spec(mod_s.shape),
                pl.BlockSpec((bs, dm.zw), rows), pl.BlockSpec((bs, dm.pw), rows), pl.BlockSpec((bs, dm.gw), rows)]
    in_specs += [_const_spec(w.shape) for w in wts]
    return pl.pallas_call(
        functools.partial(_spost_kernel, dm=dm), grid=(t_s,), in_specs=in_specs,
        out_specs=[pl.BlockSpec((bs, d), rows), pl.BlockSpec((bs, d // 2), rows), pl.BlockSpec((bs, n_exp), rows)],
        out_shape=[jax.ShapeDtypeStruct((t_s * bs, d), F32), jax.ShapeDtypeStruct((t_s * bs, d // 2), I32),
                   jax.ShapeDtypeStruct((t_s * bs, n_exp), F32)],
        compiler_params=_cp(("arbitrary",)), name="sample_post",
    )(xs2d, mod_s, z_s, mixed_s, o_s, *wts)


def _route_kernel(lgp_ref, lgs_ref, bias_ref, idx_ref, w_ref, cnt_ref, *, n_exp, tiles_p):
    i = pl.program_id(0)
    s = jax.nn.sigmoid(jnp.where(i < tiles_p, lgp_ref[...], lgs_ref[...]))
    cur = s + bias_ref[...]
    tr = s.shape[0]
    lane = lax.broadcasted_iota(I32, (1, n_exp), 1).astype(F32)
    slot = lax.broadcasted_iota(I32, (1, LANE), 1)
    idx8 = jnp.zeros((tr, LANE), F32)
    w8 = jnp.zeros((tr, LANE), F32)
    wsum = jnp.zeros((tr, 1), F32)
    picked = jnp.zeros((tr, n_exp), F32)
    for k in range(TOP_K):
        m = jnp.max(cur, axis=-1, keepdims=True)
        am = jnp.min(jnp.where(cur == m, lane, float(n_exp)), axis=-1, keepdims=True)
        oh = lane == am
        sk = jnp.sum(jnp.where(oh, s, 0.0), axis=-1, keepdims=True)
        cur = jnp.where(oh, -jnp.inf, cur)
        idx8 = jnp.where(slot == k, am, idx8)
        w8 = jnp.where(slot == k, sk, w8)
        wsum = wsum + sk
        picked = picked + jnp.where(oh, 1.0, 0.0)
    idx_ref[...] = idx8.T[0:TOP_K, :].astype(I32)
    w_ref[...] = w8 / wsum * ROUTED_SCALE

    @pl.when(i == 0)
    def _():
        cnt_ref[...] = jnp.zeros(cnt_ref.shape, F32)

    cnt_ref[...] += jnp.sum(picked, axis=0, keepdims=True)


def _two_source_specs(block, tiles_p, tiles_s):
    return [pl.BlockSpec(block, lambda i: (jnp.minimum(i, tiles_p - 1), 0)),
            pl.BlockSpec(block, lambda i: (jnp.clip(i - tiles_p, 0, tiles_s - 1), 0))]


def _route(lg_p, lg_s, bias, tr):
    n_exp = lg_p.shape[1]
    tiles_p, tiles_s = lg_p.shape[0] // tr, lg_s.shape[0] // tr
    n = (tiles_p + tiles_s) * tr
    rows = lambda i: (i, 0)
    return pl.pallas_call(
        functools.partial(_route_kernel, n_exp=n_exp, tiles_p=tiles_p), grid=(tiles_p + tiles_s,),
        in_specs=_two_source_specs((tr, n_exp), tiles_p, tiles_s) + [_const_spec((1, n_exp))],
        out_specs=[pl.BlockSpec((TOP_K, tr), lambda i: (0, i)), pl.BlockSpec((tr, LANE), rows),
                   pl.BlockSpec((8, n_exp), lambda i: (0, 0))],
        out_shape=[jax.ShapeDtypeStruct((TOP_K, n), I32), jax.ShapeDtypeStruct((n, LANE), F32),
                   jax.ShapeDtypeStruct((8, n_exp), F32)],
        compiler_params=_cp(("arbitrary",)), name="route",
    )(lg_p, lg_s, bias.reshape(1, n_exp))


def _plan_kernel(idx_ref, cnt_ref, pos_ref, bounds_ref, base_ref, dest_ref, *, n_exp):
    i = pl.program_id(0)
    tr = idx_ref.shape[1]

    @pl.when(i == 0)
    def _():
        lane = lax.broadcasted_iota(I32, (1, n_exp), 1)
        cnt = cnt_ref[...].astype(I32)
        padded = ((cnt + (EXPERT_BLOCK - 1)) // EXPERT_BLOCK) * EXPERT_BLOCK
        pe = padded
        s = 1
        while s < n_exp:
            pe = pe + jnp.where(lane >= s, pltpu.roll(pe, s, 1), 0)
            s *= 2
        row = lax.broadcasted_iota(I32, (8, n_exp), 0)
        bounds_ref[...] = jnp.where(row == 0, pe - padded, pe)
        first = (pe - padded).astype(F32)[0:1, :]
        base_ref[...] = jnp.broadcast_to(first, (LANE, n_exp)).T

    eidx = lax.broadcasted_iota(I32, (n_exp, 1), 0)
    idx = idx_ref[...]
    member = eidx == idx[0:1, :]
    for k in range(1, TOP_K):
        member = member | (eidx == idx[k:k + 1, :])
    mb = jnp.where(member, 1.0, 0.0).astype(BF16)
    before = jnp.where(lax.broadcasted_iota(I32, (tr, tr), 0) < lax.broadcasted_iota(I32, (tr, tr), 1),
                       1.0, 0.0).astype(BF16)
    dest_ref[...] = _dot(mb, before) + base_ref[:, 0:1]
    for b in range(tr // LANE):
        cols = slice(b * LANE, (b + 1) * LANE)
        dest = dest_ref[:, cols]
        rows = [jnp.sum(jnp.where(eidx == idx[k:k + 1, cols], dest, 0.0), axis=0, keepdims=True)
                for k in range(TOP_K)]
        pos_ref[:, cols] = jnp.concatenate(rows, axis=0).astype(I32)
    base_ref[...] += jnp.sum(mb.astype(F32), axis=1, keepdims=True)


def _plan(idx_t, counts, tr):
    n = idx_t.shape[1]
    n_exp = counts.shape[1]
    cols = lambda i: (0, i)
    return pl.pallas_call(
        functools.partial(_plan_kernel, n_exp=n_exp), grid=(n // tr,),
        in_specs=[pl.BlockSpec((TOP_K, tr), cols), _const_spec((8, n_exp))],
        out_specs=[pl.BlockSpec((TOP_K, tr), cols), pl.BlockSpec((8, n_exp), lambda i: (0, 0))],
        out_shape=[jax.ShapeDtypeStruct((TOP_K, n), I32), jax.ShapeDtypeStruct((8, n_exp), I32)],
        scratch_shapes=[pltpu.VMEM((n_exp, LANE), F32), pltpu.VMEM((n_exp, tr), F32)],
        compiler_params=_cp(("arbitrary",)), name="plan",
    )(idx_t, counts)


SC_CORES = 2
SC_SUBCORES = 16
SC_AXES = ("core", "subcore")
SC_WORKERS = SC_CORES * SC_SUBCORES
SC_WINDOW = 128


def _sc_mesh():
    return plsc.VectorSubcoreMesh(core_axis_name=SC_AXES[0], subcore_axis_name=SC_AXES[1],
                                  num_cores=SC_CORES, num_subcores=SC_SUBCORES)


def _sc_worker():
    return lax.axis_index(SC_AXES[1]) * SC_CORES + lax.axis_index(SC_AXES[0])


def _dispatch(h_p, h_s, pos_t, n_rows):
    n_p, c = h_p.shape
    n_s = h_s.shape[0]
    win_p, win_s = n_p // SC_WINDOW, n_s // SC_WINDOW
    assert n_p % (SC_WINDOW * SC_WORKERS) == 0 and n_s % SC_WINDOW == 0 and win_s <= SC_WORKERS

    @functools.partial(
        pl.kernel, out_type=jax.ShapeDtypeStruct((n_rows, c), h_p.dtype), mesh=_sc_mesh(), name="dispatch",
        scratch_types=[pltpu.VMEM((TOP_K, SC_WINDOW), I32), pltpu.VMEM((SC_WINDOW, c), h_p.dtype),
                       pltpu.SemaphoreType.DMA])
    def run(hp_hbm, hs_hbm, pos_hbm, xs_hbm, idx_v, rows_v, sem):
        wid = _sc_worker()

        def scatter_window(h_hbm, row0, tok0):
            pltpu.sync_copy(h_hbm.at[pl.ds(row0, SC_WINDOW)], rows_v)
            pltpu.sync_copy(pos_hbm.at[:, pl.ds(tok0, SC_WINDOW)], idx_v)
            copies = [pltpu.async_copy(rows_v, xs_hbm.at[idx_v.at[k]], sem) for k in range(TOP_K)]
            for cp in copies:
                cp.wait()

        def prompt_window(j, carry):
            row0 = pl.multiple_of((wid * (win_p // SC_WORKERS) + j) * SC_WINDOW, SC_WINDOW)
            scatter_window(hp_hbm, row0, row0)
            return carry
        lax.fori_loop(0, win_p // SC_WORKERS, prompt_window, 0)

        @pl.when(wid < win_s)
        def _():
            row0 = pl.multiple_of(wid * SC_WINDOW, SC_WINDOW)
            scatter_window(hs_hbm, row0, n_p + row0)

    return run(h_p, h_s, pos_t)


def _combine_gather(ys, pos_t):
    k_n = pos_t.shape[0] * pos_t.shape[1]
    c = ys.shape[1]
    win = SC_WINDOW // 2
    per_worker = k_n // (win * SC_WORKERS)
    assert k_n % (win * SC_WORKERS) == 0 and per_worker % 2 == 0
    dma = pltpu.SemaphoreType.DMA

    @functools.partial(
        pl.kernel, out_type=jax.ShapeDtypeStruct((k_n, c), ys.dtype), mesh=_sc_mesh(), name="combine_gather",
        scratch_types=[pltpu.VMEM((win,), I32), pltpu.VMEM((win,), I32),
                       pltpu.VMEM((win, c), ys.dtype), pltpu.VMEM((win, c), ys.dtype), dma, dma, dma, dma])
    def run(ys_hbm, idx_hbm, out_hbm, idx0, idx1, buf0, buf1, gsem0, gsem1, ssem0, ssem1):
        wid = _sc_worker()

        def rows(j):
            return pl.ds(pl.multiple_of((wid * per_worker + j) * win, win), win)

        def gather(j, idx_v, buf, sem):
            pltpu.sync_copy(idx_hbm.at[rows(j)], idx_v)
            pltpu.async_copy(ys_hbm.at[idx_v], buf, sem)

        def gather_wait(idx_v, buf, sem):
            pltpu.make_async_copy(ys_hbm.at[idx_v], buf, sem).wait()

        def store(j, buf, sem):
            pltpu.async_copy(buf, out_hbm.at[rows(j)], sem)

        def store_wait(buf, sem):
            pltpu.make_async_copy(buf, out_hbm.at[rows(0)], sem).wait()

        gather(0, idx0, buf0, gsem0)

        def two_windows(i, carry):
            j = 2 * i

            @pl.when(i > 0)
            def _():
                store_wait(buf1, ssem1)

            gather(j + 1, idx1, buf1, gsem1)
            gather_wait(idx0, buf0, gsem0)
            store(j, buf0, ssem0)

            @pl.when(j + 2 < per_worker)
            def _():
                store_wait(buf0, ssem0)
                gather(j + 2, idx0, buf0, gsem0)

            gather_wait(idx1, buf1, gsem1)
            store(j + 1, buf1, ssem1)
            return carry
        lax.fori_loop(0, per_worker // 2, two_windows, 0)
        store_wait(buf0, ssem0)
        store_wait(buf1, ssem1)

    return run(ys, pos_t.reshape(k_n))


MOE_RING = 16
MOE_WEIGHT_RING = 8


def _moe_kernel(st_ref, wg_ref, wu_ref, wd_ref, xs_ref, ys_ref, xbuf, ybuf, wgf, wuf, wdf, wgb, wub, wdb,
                semx, semy, semw):
    e = pl.program_id(0)
    n_e = pl.num_programs(0)

    def w_copies(ex):
        slot = ex % MOE_WEIGHT_RING
        return [pltpu.make_async_copy(src.at[ex], dst.at[slot], semw.at[slot, i])
                for i, (src, dst) in enumerate(((wg_ref, wgf), (wu_ref, wuf), (wd_ref, wdf)))]

    @pl.when(e == 0)
    def _():
        for ex in range(MOE_WEIGHT_RING - 1):
            @pl.when(ex < n_e)
            def _(ex=ex):
                for cp in w_copies(ex):
                    cp.start()

    @pl.when(e + (MOE_WEIGHT_RING - 1) < n_e)
    def _():
        for cp in w_copies(e + (MOE_WEIGHT_RING - 1)):
            cp.start()
    g0 = st_ref[e] // EXPERT_BLOCK
    nchunks = st_ref[e + 1] // EXPERT_BLOCK - g0
    ntot = st_ref[n_e] // EXPERT_BLOCK

    def rows(g):
        return pl.ds(pl.multiple_of(g * EXPERT_BLOCK, EXPERT_BLOCK), EXPERT_BLOCK)

    def x_copy(g):
        slot = g % MOE_RING
        return pltpu.make_async_copy(xs_ref.at[rows(g)], xbuf.at[slot], semx.at[slot])

    def y_copy(g):
        slot = g % MOE_RING
        return pltpu.make_async_copy(ybuf.at[slot], ys_ref.at[rows(g)], semy.at[slot])

    @pl.when(e == 0)
    def _():
        for g in range(MOE_RING - 1):
            @pl.when(g < ntot)
            def _(g=g):
                x_copy(g).start()

    for cp in w_copies(e):
        cp.wait()
    wslot = e % MOE_WEIGHT_RING
    wgb[...] = wgf[wslot].astype(BF16)
    wub[...] = wuf[wslot].astype(BF16)
    wdb[...] = wdf[wslot].astype(BF16)

    def take(g):
        @pl.when(g + (MOE_RING - 1) < ntot)
        def _():
            x_copy(g + (MOE_RING - 1)).start()

        x_copy(g).wait()

        @pl.when(g >= MOE_RING)
        def _():
            y_copy(g - MOE_RING).wait()

        return xbuf[g % MOE_RING]

    def expert_mlp(x_words):
        x = _unpack_rows(x_words).astype(BF16)
        mid = _silu(_dot(x, wgb[...])) * _dot(x, wub[...])
        return _pack_rows(_dot(mid.astype(BF16), wdb[...]))

    def put(g, y_words):
        ybuf[g % MOE_RING] = y_words
        y_copy(g).start()

    def run_chunks(g, m):
        xs = [take(g + i) for i in range(m)]
        y = expert_mlp(xs[0] if m == 1 else jnp.concatenate(xs, axis=0))
        for i in range(m):
            put(g + i, y[i * EXPERT_BLOCK:(i + 1) * EXPERT_BLOCK])

    def quad(p, carry):
        run_chunks(g0 + 4 * p, 4)
        return carry

    rest = nchunks % 4
    five = (rest == 1) & (nchunks >= 5)
    n_quads = nchunks // 4 - five.astype(I32)
    lax.fori_loop(0, n_quads, quad, 0)
    g_rest = g0 + 4 * n_quads

    @pl.when(five)
    def _():
        run_chunks(g_rest, 5)

    @pl.when(rest >= 2)
    def _():
        run_chunks(g_rest, 2)

    @pl.when((rest == 3) | (nchunks == 1))
    def _():
        run_chunks(g0 + nchunks - 1, 1)

    @pl.when(e == n_e - 1)
    def _():
        for j in range(MOE_RING):
            g = ntot - MOE_RING + j

            @pl.when(g >= 0)
            def _(g=g):
                y_copy(g).wait()


def _moe(starts, xs, wg, wu, wd):
    n_rows, c = xs.shape
    n_exp, d, f = wg.shape
    grid_spec = pltpu.PrefetchScalarGridSpec(
        num_scalar_prefetch=1, grid=(n_exp,),
        in_specs=[pl.BlockSpec(memory_space=pl.ANY)] * 4,
        out_specs=pl.BlockSpec(memory_space=pl.ANY),
        scratch_shapes=[pltpu.VMEM((MOE_RING, EXPERT_BLOCK, c), I32), pltpu.VMEM((MOE_RING, EXPERT_BLOCK, c), I32),
                        pltpu.VMEM((MOE_WEIGHT_RING, d, f), F32), pltpu.VMEM((MOE_WEIGHT_RING, d, f), F32),
                        pltpu.VMEM((MOE_WEIGHT_RING, f, d), F32),
                        pltpu.VMEM((d, f), BF16), pltpu.VMEM((d, f), BF16), pltpu.VMEM((f, d), BF16),
                        pltpu.SemaphoreType.DMA((MOE_RING,)), pltpu.SemaphoreType.DMA((MOE_RING,)),
                        pltpu.SemaphoreType.DMA((MOE_WEIGHT_RING, 3))])
    return pl.pallas_call(
        _moe_kernel, grid_spec=grid_spec, out_shape=jax.ShapeDtypeStruct((n_rows, c), I32),
        compiler_params=_cp(("arbitrary",)), name="moe",
    )(starts, wg, wu, wd, xs)


def _final_kernel(yg_ref, w8_ref, x1_ref, h2_ref, mod_ref, wsg_ref, wsu_ref, wsd_ref, nf_ref, y_ref, *, d):
    hb = _unpack_rows(h2_ref[...]).astype(BF16)
    shared = _dot((_silu(_dot(hb, wsg_ref[...])) * _dot(hb, wsu_ref[...])).astype(BF16), wsd_ref[...])
    w8 = w8_ref[...]
    routed = _unpack_rows(yg_ref[0]) * w8[:, 0:1]
    for k in range(1, TOP_K):
        routed = routed + _unpack_rows(yg_ref[k]) * w8[:, k:k + 1]
    g2 = mod_ref[...][:, (N_MOD - 1) * d:N_MOD * d]
    x2 = x1_ref[...] + g2 * (routed + shared)
    y_ref[...] = _rms(x2, nf_ref[...])


def _final(yg3, w8, x1, h2, mod, mod_spec, wts, nf, tf, n_tiles, tile0, out_shape, out_spec):
    d = x1.shape[1]
    rows = lambda i: (i, 0)
    in_specs = [pl.BlockSpec((TOP_K, tf, d // 2), lambda i: (0, tile0 + i, 0)),
                pl.BlockSpec((tf, LANE), lambda i: (tile0 + i, 0)),
                pl.BlockSpec((tf, d), rows), pl.BlockSpec((tf, d // 2), rows), mod_spec]
    in_specs += [_const_spec(w.shape) for w in wts]
    in_specs += [_const_spec(nf.shape)]
    return pl.pallas_call(
        functools.partial(_final_kernel, d=d), grid=(n_tiles,), in_specs=in_specs,
        out_specs=out_spec, out_shape=out_shape,
        compiler_params=_cp(("arbitrary",)), name="final",
    )(yg3, w8, x1, h2, mod, *wts, nf)


def kernel(x_prompt, x_sample, c_prompt, c_sample, state_pool, state_gla, w_ada, b_ada, norm1, w_in,
           pool_w, pool_scale, gla_w_alpha, gla_b_alpha, gla_norm, w_branch_a, w_branch_b, w_out, norm2,
           w_router, router_bias, w_exp_gate, w_exp_up, w_exp_down, w_sh_gate, w_sh_up, w_sh_down, norm_f):
    depth = w_in.shape[0]
    assert depth == 1, "single-layer trunk"
    bp, seq, d = x_prompt.shape
    bs, t_s, _ = x_sample.shape
    _, _, past, pw = state_pool.shape
    _, _, heads, dk, dv = state_gla.shape
    rank = gla_w_alpha.shape[1]
    n_exp = w_router.shape[2]
    assert past == POOL_HIST - 1 and pw == len(POOL_WINDOWS) * LANE and dv == LANE and rank <= LANE
    dm = _Dims(d, pw, heads, dk, dv)
    n_p, n_s = bp * seq, bs * t_s
    n_all = n_p + n_s

    off = [0]
    for sz in (pw, dm.hk, dm.hk, dm.gw, dm.gw, rank, d, d):
        off.append(off[-1] + sz)
    wi = w_in[0]
    win = (wi[:, :off[5]].astype(BF16), wi[:, off[6]:].astype(BF16),
           jnp.concatenate([wi[:, off[5]:off[6]], jnp.zeros((d, LANE - rank), F32)], axis=1).astype(BF16))
    assert off[5] + 2 * d + LANE == dm.zw
    wal = jnp.concatenate([gla_w_alpha[0], jnp.zeros((LANE - rank, dm.hk), F32)], axis=0).astype(BF16)
    bal = gla_b_alpha[0].reshape(1, dm.hk)
    wr = w_router[0]
    wrh = wr.astype(BF16)
    wrl = (wr - wrh.astype(F32)).astype(BF16)
    post_w = (pool_w[0].astype(BF16), pool_scale[0].reshape(1, pw), gla_norm[0].reshape(1, dm.gw),
              w_branch_a[0].astype(BF16), w_branch_b[0].astype(BF16), w_out[0].astype(BF16),
              norm2[0].reshape(1, d), wrh, wrl)
    n1 = norm1[0].reshape(1, d)

    mod = _ada(jnp.concatenate([c_prompt, c_sample], axis=0), w_ada[0], b_ada[0])
    mod_p = mod[:bp].reshape(bp, 1, N_MOD * d)
    mod_s = mod[bp:]

    x1_p, h2_p, lg_p, npool_p, ngla_p = _mix_prompt(
        dm, x_prompt.reshape(n_p, d), mod_p, (n1,) + win + (wal, bal) + post_w, bp, seq, n_exp)

    xs2d = x_sample.reshape(bs, t_s * d)
    z_s = _sample_in(dm, xs2d, mod_s, n1, win, bs, t_s)
    mixed_s, o_s, npool_s, ngla_s = _sample_state(
        dm, z_s, state_pool[0].reshape(bs, past * pw), state_gla[0].reshape(bs, dm.hk, dv), wal, bal, bs, t_s)
    x1_s, h2_s, lg_s = _sample_post(dm, xs2d, mod_s, z_s, mixed_s, o_s, post_w, bs, t_s, n_exp)

    tr = 512 if (n_p % 512 == 0 and n_s % 512 == 0) else LANE
    assert n_p % tr == 0 and n_s % tr == 0 and n_p % bs == 0
    idx8, w8, counts = _route(lg_p, lg_s, router_bias[0], tr)
    nblk = (n_all * TOP_K) // EXPERT_BLOCK + n_exp
    pos_t, bounds = _plan(idx8, counts, tr)
    starts = jnp.concatenate([bounds[0], bounds[1, n_exp - 1:]])

    xs = _dispatch(h2_p, h2_s, pos_t, nblk * EXPERT_BLOCK)
    ys = _moe(starts, xs, w_exp_gate[0], w_exp_up[0], w_exp_down[0])
    yg3 = _combine_gather(ys, pos_t).reshape(TOP_K, n_all, d // 2)

    sh_w = (w_sh_gate[0].astype(BF16), w_sh_up[0].astype(BF16), w_sh_down[0].astype(BF16))
    nf = norm_f.reshape(1, d)
    tf = 512 if seq % 512 == 0 else seq
    n_tp = seq // tf
    y_p = _final(
        yg3, w8, x1_p, h2_p, mod_p,
        pl.BlockSpec((None, 1, N_MOD * d), lambda i: (i // n_tp, 0, 0)), sh_w, nf, tf, n_p // tf, 0,
        jax.ShapeDtypeStruct((n_p, d), F32), pl.BlockSpec((tf, d), lambda i: (i, 0)))
    y_s = _final(
        yg3, w8, x1_s, h2_s, mod_s,
        _const_spec(mod_s.shape), sh_w, nf, bs, t_s, n_p // bs,
        jax.ShapeDtypeStruct((bs, t_s * d), F32), pl.BlockSpec((bs, d), lambda i: (0, i)))

    return (y_p.reshape(bp, seq, d), y_s.reshape(bs, t_s, d),
            npool_p.reshape(depth, bp, past, pw), ngla_p.reshape(depth, bp, heads, dk, dv),
            npool_s.reshape(depth, bs, past, pw), ngla_s.reshape(depth, bs, heads, dk, dv))
```

```python
import functools

import jax
import jax.numpy as jnp
from jax import lax
from jax.experimental import pallas as pl
from jax.experimental.pallas import tpu as pltpu
from jax.experimental.pallas import tpu_sc as plsc

F32 = jnp.float32
BF16 = jnp.bfloat16
I32 = jnp.int32

EPS = 1e-6
N_MOD = 6
POOL_WINDOWS = (2, 4, 8, 16)
POOL_HIST = 16
GLA_GATE_TEMP = 16.0
TOP_K = 8
ROUTED_SCALE = 2.5
PAST_LEN = 16384
EXPERT_BLOCK = 128
LANE = 128
GLA_FAST_MAX_DECAY = 40.0
PROJ_PIECE = 512
PROJ_LEAD = 2
VMEM_LIMIT = 56 * 1024 * 1024


def _cp(sem, vmem=VMEM_LIMIT):
    return pltpu.CompilerParams(dimension_semantics=sem, vmem_limit_bytes=vmem)


def _rms(x, g):
    return x * lax.rsqrt(jnp.mean(x * x, axis=-1, keepdims=True) + EPS) * g


def _silu(x):
    return x * jax.nn.sigmoid(x)


def _dot(a, b):
    return jnp.dot(a, b, preferred_element_type=F32)


def _split3(a):
    a0 = a.astype(BF16)
    r1 = a - a0.astype(F32)
    a1 = r1.astype(BF16)
    a2 = (r1 - a1.astype(F32)).astype(BF16)
    return a0, a1, a2


def _mod_parts(mod, d):
    return [mod[:, i * d:(i + 1) * d] for i in range(N_MOD)]


_HI16 = -65536


def _pack_rows(v):
    c = v.shape[1] // 2
    lo = lax.bitcast_convert_type(v[:, :c].astype(BF16).astype(F32), I32)
    hi = lax.bitcast_convert_type(v[:, c:].astype(BF16).astype(F32), I32)
    return (hi & _HI16) | lax.shift_right_logical(lo, 16)


def _unpack_rows(w):
    lo = lax.bitcast_convert_type(lax.shift_left(w, 16), F32)
    hi = lax.bitcast_convert_type(w & _HI16, F32)
    return jnp.concatenate([lo, hi], axis=1)


def _ada_kernel(c_ref, w_ref, b_ref, o_ref):
    a = _silu(c_ref[...]).astype(BF16)
    o_ref[...] = _dot(a, w_ref[...].astype(BF16)) + b_ref[...]


def _ada(c_all, w_ada, b_ada):
    n, d = c_all.shape
    cols = w_ada.shape[1]
    bc = d
    return pl.pallas_call(
        _ada_kernel,
        grid=(cols // bc,),
        in_specs=[pl.BlockSpec((n, d), lambda i: (0, 0)),
                  pl.BlockSpec((d, bc), lambda i: (0, i)),
                  pl.BlockSpec((1, bc), lambda i: (0, i))],
        out_specs=pl.BlockSpec((n, bc), lambda i: (0, i)),
        out_shape=jax.ShapeDtypeStruct((n, cols), F32),
        compiler_params=_cp(("arbitrary",)),
        name="ada",
    )(c_all, w_ada, b_ada.reshape(1, cols))


class _Dims:
    def __init__(self, d, pw, heads, dk, dv):
        self.d, self.pw, self.h, self.dk, self.dv = d, pw, heads, dk, dv
        self.hk, self.gw = heads * dk, heads * dv
        self.q0 = pw
        self.k0 = self.q0 + self.hk
        self.v0 = self.k0 + self.hk
        self.go0 = self.v0 + self.gw
        self.ga0 = self.go0 + self.gw
        self.gb0 = self.ga0 + d
        self.al0 = self.gb0 + d
        self.zw = self.al0 + LANE


def _log_decay(z_al, wal_ref, bal_ref):
    xal = _dot(z_al.astype(BF16), wal_ref[...]) + bal_ref[...]
    return jax.nn.log_sigmoid(xal) / GLA_GATE_TEMP


def _pool_project(mixed_groups, pw_ref, ps_ref):
    ys = [_dot(m.astype(BF16), pw_ref[g]) for g, m in enumerate(mixed_groups)]
    return jnp.concatenate(ys, axis=1) * ps_ref[...]


def _post_mix(dm, x, mod, ya, o, go, ga, gb, gn_ref, wa_ref, wb_ref, wo_ref, n2_ref, wrh_ref, wrl_ref,
              filler=lambda: None):
    _, _, g1, sh2, sc2, _ = _mod_parts(mod, dm.d)
    parts = []
    for h in range(dm.h):
        oh = o[:, h * dm.dv:(h + 1) * dm.dv]
        parts.append(oh * lax.rsqrt(jnp.mean(oh * oh, axis=-1, keepdims=True) + EPS))
    yb = jnp.concatenate(parts, axis=1) * gn_ref[...] * _silu(go)
    filler()
    m = (jax.nn.sigmoid(ga) * _dot(ya.astype(BF16), wa_ref[...])
         + jax.nn.sigmoid(gb) * _dot(yb.astype(BF16), wb_ref[...]))
    filler()
    x1 = x + g1 * _dot(m.astype(BF16), wo_ref[...])
    h2 = _rms(x1, n2_ref[...]) * (1 + sc2) + sh2
    hi = h2.astype(BF16)
    lo = (h2 - hi.astype(F32)).astype(BF16)
    filler()
    logits = _dot(hi, wrh_ref[...]) + (_dot(hi, wrl_ref[...]) + _dot(lo, wrh_ref[...]))
    filler()
    return x1, _pack_rows(h2), logits


def _topk_rows(logits, bias):
    tr, n_exp = logits.shape
    s = jax.nn.sigmoid(logits)
    cur = s + bias
    lane = lax.broadcasted_iota(I32, (1, n_exp), 1).astype(F32)
    slot = lax.broadcasted_iota(I32, (1, LANE), 1)
    idx8 = jnp.zeros((tr, LANE), F32)
    w8 = jnp.zeros((tr, LANE), F32)
    wsum = jnp.zeros((tr, 1), F32)
    picked = jnp.zeros((tr, n_exp), F32)
    for k in range(TOP_K):
        m = jnp.max(cur, axis=-1, keepdims=True)
        am = jnp.min(jnp.where(cur == m, lane, float(n_exp)), axis=-1, keepdims=True)
        oh = lane == am
        sk = jnp.sum(jnp.where(oh, s, 0.0), axis=-1, keepdims=True)
        cur = jnp.where(oh, -jnp.inf, cur)
        idx8 = jnp.where(slot == k, am, idx8)
        w8 = jnp.where(slot == k, sk, w8)
        wsum = wsum + sk
        picked = picked + jnp.where(oh, 1.0, 0.0)
    return idx8.T[0:TOP_K, :].astype(I32), w8 / wsum * ROUTED_SCALE, picked.astype(BF16)


def _count_members(picked):
    return _dot(jnp.ones((8, picked.shape[0]), BF16), picked)


def _in_project(dm, x, mod, n1_ref, win_refs, z_ref):
    h = (_rms(x, n1_ref[...]) * (1 + mod[:, dm.d:2 * dm.d]) + mod[:, 0:dm.d]).astype(BF16)
    starts = [0]
    for w_ref in win_refs:
        starts.append(starts[-1] + w_ref.shape[1])
    for i in (0, 2, 1):
        z_ref[:, starts[i]:starts[i + 1]] = _dot(h, win_refs[i][...])


def _mixp_kernel(x_ref, mod_ref, n1_ref, wia_ref, wig_ref, wil_ref, wal_ref, bal_ref, pw_ref, ps_ref, gn_ref,
                 wa_ref, wb_ref, wo_ref, n2_ref, wrh_ref, wrl_ref, rb_ref,
                 x1_ref, h2_ref, idx_ref, w8_ref, cnt_ref, np_ref, ng_ref,
                 za_ref, zb_ref, xk_ref, mk_ref, hn_ref, lg_ref, uext_ref, s_ref, b_ref, oi_ref,
                 *, dm, tt, n_t, tiles):
    n = pl.program_id(0)
    j = jnp.clip(n - 1, 0, tiles - 1) % n_t

    @pl.when(n == 0)
    def _():
        za_ref[...] = jnp.zeros(za_ref.shape, F32)
        zb_ref[...] = jnp.zeros(zb_ref.shape, F32)
        xk_ref[...] = jnp.zeros(xk_ref.shape, F32)
        mk_ref[...] = jnp.zeros(mk_ref.shape, F32)
        lg_ref[...] = jnp.zeros(lg_ref.shape, F32)
        cnt_ref[...] = jnp.zeros(cnt_ref.shape, F32)

    def route_previous():
        idx_ref[...], w8_ref[...], picked = _topk_rows(lg_ref[...], rb_ref[...])
        return picked

    def count_routed(picked):
        cnt_ref[...] += jnp.where(n >= 2, _count_members(picked), 0.0)

    @pl.when((j == 0) & (n <= tiles))
    def _():
        uext_ref[0:POOL_HIST, :] = jnp.zeros((POOL_HIST, dm.pw), F32)
        s_ref[...] = jnp.zeros(s_ref.shape, F32)

    @pl.when((n % 2 == 0) & (n <= tiles))
    def _():
        _mixp_region(za_ref, zb_ref, x_ref, mod_ref, n1_ref, wia_ref, wig_ref, wil_ref, wal_ref, bal_ref, pw_ref,
                     ps_ref, gn_ref, wa_ref, wb_ref, wo_ref, n2_ref, wrh_ref, wrl_ref, x1_ref, h2_ref, lg_ref,
                     xk_ref, mk_ref, hn_ref, uext_ref, s_ref, b_ref, oi_ref, dm=dm, tt=tt, j=j,
                     route_previous=route_previous, count_routed=count_routed)

    @pl.when((n % 2 == 1) & (n <= tiles))
    def _():
        _mixp_region(zb_ref, za_ref, x_ref, mod_ref, n1_ref, wia_ref, wig_ref, wil_ref, wal_ref, bal_ref, pw_ref,
                     ps_ref, gn_ref, wa_ref, wb_ref, wo_ref, n2_ref, wrh_ref, wrl_ref, x1_ref, h2_ref, lg_ref,
                     xk_ref, mk_ref, hn_ref, uext_ref, s_ref, b_ref, oi_ref, dm=dm, tt=tt, j=j,
                     route_previous=route_previous, count_routed=count_routed)

    @pl.when(n > tiles)
    def _():
        count_routed(route_previous())

    xk_ref[...] = x_ref[...]
    mk_ref[...] = mod_ref[...]

    @pl.when((j == n_t - 1) & (n <= tiles))
    def _():
        np_ref[...] = uext_ref[tt + 1:tt + POOL_HIST, :]
        ng_ref[...] = s_ref[...]


def _mixp_region(zp_ref, z_ref, x_ref, mod_ref, n1_ref, wia_ref, wig_ref, wil_ref, wal_ref, bal_ref, pw_ref,
                 ps_ref, gn_ref, wa_ref, wb_ref, wo_ref, n2_ref, wrh_ref, wrl_ref, x1_ref, h2_ref, lg_ref,
                 xk_ref, mk_ref, hn_ref, uext_ref, s_ref, b_ref, oi_ref, *, dm, tt, j, route_previous, count_routed):
    picked = route_previous()
    scale = dm.dk ** -0.5
    lane = lax.broadcasted_iota(I32, (1, dm.hk), 1)
    head_masks = [(lane >= h * dm.dk) & (lane < (h + 1) * dm.dk) for h in range(dm.h)]
    causal = lax.broadcasted_iota(I32, (tt, tt), 0) >= lax.broadcasted_iota(I32, (tt, tt), 1)

    def pool_branch():
        pos = j * tt + lax.broadcasted_iota(I32, (tt, 1), 0)
        gw = dm.pw // len(POOL_WINDOWS)
        mixed = []
        for g, w in enumerate(POOL_WINDOWS):
            c0 = g * gw
            cur = uext_ref[POOL_HIST:POOL_HIST + tt, c0:c0 + gw]
            acc = cur
            for i in range(1, w):
                acc = acc + uext_ref[POOL_HIST - i:POOL_HIST - i + tt, c0:c0 + gw]
            cnt = jnp.minimum(w, pos + 1).astype(F32)
            mixed.append(acc / cnt - cur)
        return _pool_project(mixed, pw_ref, ps_ref)

    def scaled_queries(bc):
        return (z_ref[:, dm.q0:dm.q0 + dm.hk] * scale) * jnp.exp(bc)

    def inter_chunk(qe):
        s_bf = s_ref[...].astype(BF16)
        return jnp.concatenate(
            [_dot(jnp.where(head_masks[h], qe, 0.0).astype(BF16), s_bf) for h in range(dm.h)], axis=1)

    def finish(ya, o, filler=lambda: None):
        x1, h2, logits = _post_mix(
            dm, xk_ref[...], mk_ref[...], ya, o, z_ref[:, dm.go0:dm.go0 + dm.gw],
            z_ref[:, dm.ga0:dm.ga0 + dm.d], z_ref[:, dm.gb0:dm.gb0 + dm.d],
            gn_ref, wa_ref, wb_ref, wo_ref, n2_ref, wrh_ref, wrl_ref, filler)
        x1_ref[...] = x1
        h2_ref[...] = h2
        lg_ref[...] = logits

    mod_in = mod_ref[...]
    hn_ref[...] = (_rms(x_ref[...], n1_ref[...]) * (1 + mod_in[:, dm.d:2 * dm.d]) + mod_in[:, 0:dm.d]).astype(BF16)
    pieces = iter([(w_ref, a, min(a + PROJ_PIECE, w_ref.shape[1]), c0)
                   for w_ref, c0 in ((wia_ref, 0), (wil_ref, dm.al0), (wig_ref, dm.ga0))
                   for a in range(0, w_ref.shape[1], PROJ_PIECE)])

    def project_piece():
        piece = next(pieces, None)
        if piece is not None:
            w_ref, a, b, c0 = piece
            zp_ref[:, c0 + a:c0 + b] = _dot(hn_ref[...], w_ref[:, a:b])
        return piece is not None

    for _ in range(PROJ_LEAD):
        project_piece()
    uext_ref[POOL_HIST:POOL_HIST + tt, :] = z_ref[:, 0:dm.pw]
    ya = pool_branch()
    project_piece()
    la = _log_decay(z_ref[:, dm.al0:dm.al0 + LANE], wal_ref, bal_ref)
    tri = jnp.where(causal, 1.0, 0.0).astype(BF16)
    a0, a1, a2 = _split3(la)
    bc = _dot(tri, a0) + (_dot(tri, a1) + _dot(tri, a2))
    b_ref[...] = bc
    project_piece()
    qe = scaled_queries(bc)
    k = z_ref[:, dm.k0:dm.k0 + dm.hk]
    vb = z_ref[:, dm.v0:dm.v0 + dm.gw].astype(BF16)
    blast = bc[tt - 1:tt, :]
    fast = jnp.max(-blast) <= GLA_FAST_MAX_DECAY
    ke = (k * jnp.exp(-bc)).astype(BF16)
    intra = []
    for h in range(dm.h):
        qh = jnp.where(head_masks[h], qe, 0.0).astype(BF16)
        sc = lax.dot_general(qh, ke, (((1,), (1,)), ((), ())), preferred_element_type=F32)
        sc = jnp.where(causal, sc, 0.0).astype(BF16)
        intra.append(_dot(sc, vb[:, h * dm.dv:(h + 1) * dm.dv]))
    count_routed(picked)
    project_piece()
    finish(ya, inter_chunk(qe) + jnp.concatenate(intra, axis=1), project_piece)
    while project_piece():
        pass

    @pl.when(jnp.logical_not(fast))
    def _():
        rows = lax.broadcasted_iota(I32, (tt, 1), 0)
        for h in range(dm.h):
            def body(tb, carry, h=h):
                base = pl.multiple_of(tb * 8, 8)
                q8 = z_ref[pl.ds(base, 8), dm.q0:dm.q0 + dm.hk] * scale
                b8 = b_ref[pl.ds(base, 8), :]
                vh = z_ref[:, dm.v0 + h * dm.dv:dm.v0 + (h + 1) * dm.dv]
                out_rows = []
                for r in range(8):
                    keep = (rows <= base + r) & head_masks[h]
                    dec = jnp.exp(jnp.where(keep, b8[r:r + 1, :] - b_ref[...], -jnp.inf))
                    wgt = (q8[r:r + 1, :] * dec) * z_ref[:, dm.k0:dm.k0 + dm.hk]
                    s_col = jnp.sum(wgt, axis=-1, keepdims=True)
                    out_rows.append(jnp.sum(s_col * vh, axis=0, keepdims=True))
                oi_ref[pl.ds(base, 8), h * dm.dv:(h + 1) * dm.dv] = jnp.concatenate(out_rows, axis=0)
                return carry
            lax.fori_loop(0, tt // 8, body, 0)
        finish(pool_branch(), inter_chunk(scaled_queries(b_ref[...])) + oi_ref[...])

    bc = b_ref[...]
    blast = bc[tt - 1:tt, :]
    kd = z_ref[:, dm.k0:dm.k0 + dm.hk] * jnp.exp(blast - bc)
    u_all = _dot(kd.T.astype(BF16), z_ref[:, dm.v0:dm.v0 + dm.gw].astype(BF16))
    dec_col = jnp.broadcast_to(jnp.exp(blast), (dm.hk, dm.hk)).T[:, 0:dm.dv]
    upd = jnp.concatenate(
        [u_all[h * dm.dk:(h + 1) * dm.dk, h * dm.dv:(h + 1) * dm.dv] for h in range(dm.h)], axis=0)
    s_ref[...] = dec_col * s_ref[...] + upd
    uext_ref[0:POOL_HIST, :] = uext_ref[tt:tt + POOL_HIST, :]


def _const_spec(shape):
    nd = len(shape)
    return pl.BlockSpec(shape, lambda *_: (0,) * nd, pipeline_mode=pl.Buffered(1))


def _mix_prompt(dm, x2d, mod3, wts, bsz, seq, n_exp):
    tt = 256 if seq % 256 == 0 else seq
    n_t = seq // tt
    n = bsz * seq
    d = dm.d
    tiles = bsz * n_t
    kern = functools.partial(_mixp_kernel, dm=dm, tt=tt, n_t=n_t, tiles=tiles)
    proj = lambda i: jnp.minimum(i, tiles - 1)
    fin = lambda i: jnp.clip(i - 1, 0, tiles - 1)
    routed = lambda i: jnp.clip(i - 2, 0, tiles - 1)
    row = lambda i: (fin(i), 0)
    in_specs = [pl.BlockSpec((tt, d), lambda i: (proj(i), 0)),
                pl.BlockSpec((None, 1, N_MOD * d), lambda i: (proj(i) // n_t, 0, 0))]
    in_specs += [_const_spec(w.shape) for w in wts]
    out_specs = [pl.BlockSpec((tt, d), row), pl.BlockSpec((tt, d // 2), row),
                 pl.BlockSpec((TOP_K, tt), lambda i: (0, routed(i))), pl.BlockSpec((tt, LANE), lambda i: (routed(i), 0)),
                 pl.BlockSpec((8, n_exp), lambda i: (0, 0)),
                 pl.BlockSpec((None, POOL_HIST - 1, dm.pw), lambda i: (fin(i) // n_t, 0, 0)),
                 pl.BlockSpec((None, dm.hk, dm.dv), lambda i: (fin(i) // n_t, 0, 0))]
    out_shape = [jax.ShapeDtypeStruct((n, d), F32), jax.ShapeDtypeStruct((n, d // 2), I32),
                 jax.ShapeDtypeStruct((TOP_K, n), I32), jax.ShapeDtypeStruct((n, LANE), F32),
                 jax.ShapeDtypeStruct((8, n_exp), F32),
                 jax.ShapeDtypeStruct((bsz, POOL_HIST - 1, dm.pw), F32),
                 jax.ShapeDtypeStruct((bsz, dm.hk, dm.dv), F32)]
    scratch = [pltpu.VMEM((tt, dm.zw), F32), pltpu.VMEM((tt, dm.zw), F32),
               pltpu.VMEM((tt, d), F32), pltpu.VMEM((1, N_MOD * d), F32), pltpu.VMEM((tt, d), BF16),
               pltpu.VMEM((tt, n_exp), F32), pltpu.VMEM((tt + POOL_HIST, dm.pw), F32),
               pltpu.VMEM((dm.hk, dm.dv), F32), pltpu.VMEM((tt, dm.hk), F32), pltpu.VMEM((tt, dm.gw), F32)]
    return pl.pallas_call(
        kern, grid=(tiles + 2,), in_specs=in_specs, out_specs=out_specs, out_shape=out_shape,
        scratch_shapes=scratch, compiler_params=_cp(("arbitrary",)), name="mixp",
    )(x2d, mod3, *wts)


def _sin_kernel(x_ref, mod_ref, n1_ref, wia_ref, wig_ref, wil_ref, z_ref, *, dm):
    _in_project(dm, x_ref[...], mod_ref[...], n1_ref, (wia_ref, wig_ref, wil_ref), z_ref)


def _sample_in(dm, xs2d, mod_s, n1, win, bs, t_s):
    d = dm.d
    return pl.pallas_call(
        functools.partial(_sin_kernel, dm=dm),
        grid=(t_s,),
        in_specs=[pl.BlockSpec((bs, d), lambda t: (0, t)), _const_spec(mod_s.shape), _const_spec(n1.shape)]
        + [_const_spec(w.shape) for w in win],
        out_specs=pl.BlockSpec((bs, dm.zw), lambda t: (t, 0)),
        out_shape=jax.ShapeDtypeStruct((t_s * bs, dm.zw), F32),
        compiler_params=_cp(("arbitrary",)), name="sample_in",
    )(xs2d, mod_s, n1, *win)


def _sst_kernel(z_ref, pool_ref, gs_ref, wal_ref, bal_ref,
                mixed_ref, o_ref, npool_ref, ngs_ref,
                qt_ref, kt_ref, dt_ref, vt_ref, ot_ref, *, dm, bs, t_s, ch, pos0):
    i = pl.program_id(0)
    n_i = pl.num_programs(0)
    scale = dm.dk ** -0.5
    past = POOL_HIST - 1

    @pl.when(i == 0)
    def _():
        for t in range(t_s):
            r0 = t * bs
            la = _log_decay(z_ref[r0:r0 + bs, dm.al0:dm.al0 + LANE], wal_ref, bal_ref)
            dt_ref[t] = jnp.exp(la).T
            qt_ref[t] = (z_ref[r0:r0 + bs, dm.q0:dm.q0 + dm.hk] * scale).T
            kt_ref[t] = z_ref[r0:r0 + bs, dm.k0:dm.k0 + dm.hk].T
            vt_ref[t] = z_ref[r0:r0 + bs, dm.v0:dm.v0 + dm.gw].T
        ot_ref[...] = jnp.zeros(ot_ref.shape, F32)

        def item(s, c0, c1):
            if s >= 0:
                return z_ref[s * bs:(s + 1) * bs, c0:c1]
            return pool_ref[:, (past + s) * dm.pw + c0:(past + s) * dm.pw + c1]

        gw = dm.pw // len(POOL_WINDOWS)
        for t in range(t_s):
            for g, w in enumerate(POOL_WINDOWS):
                c0, c1 = g * gw, (g + 1) * gw
                cur = item(t, c0, c1)
                acc = cur
                for r in range(1, w):
                    acc = acc + item(t - r, c0, c1)
                cnt = float(min(w, pos0 + t + 1))
                mixed_ref[t * bs:(t + 1) * bs, c0:c1] = acc / cnt - cur
        for r in range(past):
            npool_ref[:, r * dm.pw:(r + 1) * dm.pw] = item(r - past + t_s, 0, dm.pw)

    per_head = dm.dk // ch
    hoff = pl.multiple_of((i // per_head) * dm.dv, dm.dv)
    for p in range(ch):
        hd = i * ch + p
        st = gs_ref[:, p, :].T
        for t in range(t_s):
            st = dt_ref[t, pl.ds(hd, 1), :] * st + kt_ref[t, pl.ds(hd, 1), :] * vt_ref[t, pl.ds(hoff, dm.dv), :]
            ot_ref[t, pl.ds(hoff, dm.dv), :] += qt_ref[t, pl.ds(hd, 1), :] * st
        ngs_ref[:, p, :] = st.T

    @pl.when(i == n_i - 1)
    def _():
        for t in range(t_s):
            for h in range(dm.h):
                o_ref[t * bs:(t + 1) * bs, h * dm.dv:(h + 1) * dm.dv] = ot_ref[t, h * dm.dv:(h + 1) * dm.dv, :].T


def _sample_state(dm, z_s, pool2d, gs3, wal, bal, bs, t_s):
    ch = 16
    n_i = dm.hk // ch
    past = POOL_HIST - 1
    kern = functools.partial(_sst_kernel, dm=dm, bs=bs, t_s=t_s, ch=ch, pos0=PAST_LEN)
    full = lambda shape: pl.BlockSpec(shape, lambda i: (0,) * len(shape))
    return pl.pallas_call(
        kern, grid=(n_i,),
        in_specs=[_const_spec(z_s.shape), _const_spec(pool2d.shape),
                  pl.BlockSpec((bs, ch, dm.dv), lambda i: (0, i, 0)),
                  _const_spec(wal.shape), _const_spec(bal.shape)],
        out_specs=[full((t_s * bs, dm.pw)), full((t_s * bs, dm.gw)), full((bs, past * dm.pw)),
                   pl.BlockSpec((bs, ch, dm.dv), lambda i: (0, i, 0))],
        out_shape=[jax.ShapeDtypeStruct((t_s * bs, dm.pw), F32), jax.ShapeDtypeStruct((t_s * bs, dm.gw), F32),
                   jax.ShapeDtypeStruct((bs, past * dm.pw), F32), jax.ShapeDtypeStruct(gs3.shape, F32)],
        scratch_shapes=[pltpu.VMEM((t_s, dm.hk, bs), F32), pltpu.VMEM((t_s, dm.hk, bs), F32),
                        pltpu.VMEM((t_s, dm.hk, bs), F32), pltpu.VMEM((t_s, dm.gw, bs), F32),
                        pltpu.VMEM((t_s, dm.gw, bs), F32)],
        compiler_params=_cp(("arbitrary",)), name="sample_state",
    )(z_s, pool2d, gs3, wal, bal)


def _spost_kernel(x_ref, mod_ref, z_ref, mixed_ref, o_ref, pw_ref, ps_ref, gn_ref,
                  wa_ref, wb_ref, wo_ref, n2_ref, wrh_ref, wrl_ref,
                  x1_ref, h2_ref, lg_ref, *, dm):
    gw = dm.pw // len(POOL_WINDOWS)
    ya = _pool_project([mixed_ref[:, g * gw:(g + 1) * gw] for g in range(len(POOL_WINDOWS))], pw_ref, ps_ref)
    x1, h2, logits = _post_mix(
        dm, x_ref[...], mod_ref[...], ya, o_ref[...],
        z_ref[:, dm.go0:dm.go0 + dm.gw], z_ref[:, dm.ga0:dm.ga0 + dm.d], z_ref[:, dm.gb0:dm.gb0 + dm.d],
        gn_ref, wa_ref, wb_ref, wo_ref, n2_ref, wrh_ref, wrl_ref)
    x1_ref[...] = x1
    h2_ref[...] = h2
    lg_ref[...] = logits


def _sample_post(dm, xs2d, mod_s, z_s, mixed_s, o_s, wts, bs, t_s, n_exp):
    d = dm.d
    rows = lambda t: (t, 0)
    in_specs = [pl.BlockSpec((bs, d), lambda t: (0, t)), _const_spec(mod_s.shape),
                pl.BlockSpec((bs, dm.zw), rows), pl.BlockSpec((bs, dm.pw), rows), pl.BlockSpec((bs, dm.gw), rows)]
    in_specs += [_const_spec(w.shape) for w in wts]
    return pl.pallas_call(
        functools.partial(_spost_kernel, dm=dm), grid=(t_s,), in_specs=in_specs,
        out_specs=[pl.BlockSpec((bs, d), rows), pl.BlockSpec((bs, d // 2), rows), pl.BlockSpec((bs, n_exp), rows)],
        out_shape=[jax.ShapeDtypeStruct((t_s * bs, d), F32), jax.ShapeDtypeStruct((t_s * bs, d // 2), I32),
                   jax.ShapeDtypeStruct((t_s * bs, n_exp), F32)],
        compiler_params=_cp(("arbitrary",)), name="sample_post",
    )(xs2d, mod_s, z_s, mixed_s, o_s, *wts)


def _route_kernel(lg_ref, bias_ref, idx_ref, w_ref, cnt_ref):
    i = pl.program_id(0)
    idx_ref[...], w_ref[...], picked = _topk_rows(lg_ref[...], bias_ref[...])

    @pl.when(i == 0)
    def _():
        cnt_ref[...] = jnp.zeros(cnt_ref.shape, F32)

    cnt_ref[...] += _count_members(picked)


def _route(logits, bias, tr):
    n, n_exp = logits.shape
    rows = lambda i: (i, 0)
    return pl.pallas_call(
        _route_kernel, grid=(n // tr,),
        in_specs=[pl.BlockSpec((tr, n_exp), rows), _const_spec((1, n_exp))],
        out_specs=[pl.BlockSpec((TOP_K, tr), lambda i: (0, i)), pl.BlockSpec((tr, LANE), rows),
                   pl.BlockSpec((8, n_exp), lambda i: (0, 0))],
        out_shape=[jax.ShapeDtypeStruct((TOP_K, n), I32), jax.ShapeDtypeStruct((n, LANE), F32),
                   jax.ShapeDtypeStruct((8, n_exp), F32)],
        compiler_params=_cp(("arbitrary",)), name="route",
    )(logits, bias)


def _plan_kernel(idx_ref, cnt_ref, pos_ref, bounds_ref, base_ref, dest_ref, *, n_exp):
    i = pl.program_id(0)
    tr = idx_ref.shape[1]

    @pl.when(i == 0)
    def _():
        lane = lax.broadcasted_iota(I32, (1, n_exp), 1)
        cnt = cnt_ref[...].astype(I32)
        padded = ((cnt + (EXPERT_BLOCK - 1)) // EXPERT_BLOCK) * EXPERT_BLOCK
        pe = padded
        s = 1
        while s < n_exp:
            pe = pe + jnp.where(lane >= s, pltpu.roll(pe, s, 1), 0)
            s *= 2
        row = lax.broadcasted_iota(I32, (8, n_exp), 0)
        bounds_ref[...] = jnp.where(row == 0, pe - padded, pe)
        first = (pe - padded).astype(F32)[0:1, :]
        base_ref[...] = jnp.broadcast_to(first, (LANE, n_exp)).T

    eidx = lax.broadcasted_iota(I32, (n_exp, 1), 0)
    idx = idx_ref[...]
    member = eidx == idx[0:1, :]
    for k in range(1, TOP_K):
        member = member | (eidx == idx[k:k + 1, :])
    mb = jnp.where(member, 1.0, 0.0).astype(BF16)
    before = jnp.where(lax.broadcasted_iota(I32, (tr, tr), 0) < lax.broadcasted_iota(I32, (tr, tr), 1),
                       1.0, 0.0).astype(BF16)
    dest_ref[...] = _dot(mb, before) + base_ref[:, 0:1]
    for b in range(tr // LANE):
        cols = slice(b * LANE, (b + 1) * LANE)
        dest = dest_ref[:, cols]
        rows = [jnp.sum(jnp.where(eidx == idx[k:k + 1, cols], dest, 0.0), axis=0, keepdims=True)
                for k in range(TOP_K)]
        pos_ref[:, cols] = jnp.concatenate(rows, axis=0).astype(I32)
    base_ref[...] += jnp.sum(mb.astype(F32), axis=1, keepdims=True)


def _plan(idx_t, counts, tr):
    n = idx_t.shape[1]
    n_exp = counts.shape[1]
    cols = lambda i: (0, i)
    return pl.pallas_call(
        functools.partial(_plan_kernel, n_exp=n_exp), grid=(n // tr,),
        in_specs=[pl.BlockSpec((TOP_K, tr), cols), _const_spec((8, n_exp))],
        out_specs=[pl.BlockSpec((TOP_K, tr), cols), pl.BlockSpec((8, n_exp), lambda i: (0, 0))],
        out_shape=[jax.ShapeDtypeStruct((TOP_K, n), I32), jax.ShapeDtypeStruct((8, n_exp), I32)],
        scratch_shapes=[pltpu.VMEM((n_exp, LANE), F32), pltpu.VMEM((n_exp, tr), F32)],
        compiler_params=_cp(("arbitrary",)), name="plan",
    )(idx_t, counts)


SC_CORES = 2
SC_SUBCORES = 16
SC_AXES = ("core", "subcore")
SC_WORKERS = SC_CORES * SC_SUBCORES
SC_WINDOW = 128


def _sc_mesh():
    return plsc.VectorSubcoreMesh(core_axis_name=SC_AXES[0], subcore_axis_name=SC_AXES[1],
                                  num_cores=SC_CORES, num_subcores=SC_SUBCORES)


def _sc_worker():
    return lax.axis_index(SC_AXES[1]) * SC_CORES + lax.axis_index(SC_AXES[0])


def _dispatch(h_p, h_s, pos_t, n_rows):
    n_p, c = h_p.shape
    n_s = h_s.shape[0]
    win_p, win_s = n_p // SC_WINDOW, n_s // SC_WINDOW
    assert n_p % (SC_WINDOW * SC_WORKERS) == 0 and n_s % SC_WINDOW == 0 and win_s <= SC_WORKERS

    @functools.partial(
        pl.kernel, out_type=jax.ShapeDtypeStruct((n_rows, c), h_p.dtype), mesh=_sc_mesh(), name="dispatch",
        scratch_types=[pltpu.VMEM((TOP_K, SC_WINDOW), I32), pltpu.VMEM((SC_WINDOW, c), h_p.dtype),
                       pltpu.SemaphoreType.DMA])
    def run(hp_hbm, hs_hbm, pos_hbm, xs_hbm, idx_v, rows_v, sem):
        wid = _sc_worker()

        def scatter_window(h_hbm, row0, tok0):
            pltpu.sync_copy(h_hbm.at[pl.ds(row0, SC_WINDOW)], rows_v)
            pltpu.sync_copy(pos_hbm.at[:, pl.ds(tok0, SC_WINDOW)], idx_v)
            copies = [pltpu.async_copy(rows_v, xs_hbm.at[idx_v.at[k]], sem) for k in range(TOP_K)]
            for cp in copies:
                cp.wait()

        def prompt_window(j, carry):
            row0 = pl.multiple_of((wid * (win_p // SC_WORKERS) + j) * SC_WINDOW, SC_WINDOW)
            scatter_window(hp_hbm, row0, row0)
            return carry
        lax.fori_loop(0, win_p // SC_WORKERS, prompt_window, 0)

        @pl.when(wid < win_s)
        def _():
            row0 = pl.multiple_of(wid * SC_WINDOW, SC_WINDOW)
            scatter_window(hs_hbm, row0, n_p + row0)

    return run(h_p, h_s, pos_t)


def _combine_gather(ys, pos_t):
    k_n = pos_t.shape[0] * pos_t.shape[1]
    c = ys.shape[1]
    win = SC_WINDOW // 2
    per_worker = k_n // (win * SC_WORKERS)
    assert k_n % (win * SC_WORKERS) == 0 and per_worker % 2 == 0
    dma = pltpu.SemaphoreType.DMA

    @functools.partial(
        pl.kernel, out_type=jax.ShapeDtypeStruct((k_n, c), ys.dtype), mesh=_sc_mesh(), name="combine_gather",
        scratch_types=[pltpu.VMEM((win,), I32), pltpu.VMEM((win,), I32),
                       pltpu.VMEM((win, c), ys.dtype), pltpu.VMEM((win, c), ys.dtype), dma, dma, dma, dma])
    def run(ys_hbm, idx_hbm, out_hbm, idx0, idx1, buf0, buf1, gsem0, gsem1, ssem0, ssem1):
        wid = _sc_worker()

        def rows(j):
            return pl.ds(pl.multiple_of((wid * per_worker + j) * win, win), win)

        def gather(j, idx_v, buf, sem):
            pltpu.sync_copy(idx_hbm.at[rows(j)], idx_v)
            pltpu.async_copy(ys_hbm.at[idx_v], buf, sem)

        def gather_wait(idx_v, buf, sem):
            pltpu.make_async_copy(ys_hbm.at[idx_v], buf, sem).wait()

        def store(j, buf, sem):
            pltpu.async_copy(buf, out_hbm.at[rows(j)], sem)

        def store_wait(buf, sem):
            pltpu.make_async_copy(buf, out_hbm.at[rows(0)], sem).wait()

        gather(0, idx0, buf0, gsem0)

        def two_windows(i, carry):
            j = 2 * i

            @pl.when(i > 0)
            def _():
                store_wait(buf1, ssem1)

            gather(j + 1, idx1, buf1, gsem1)
            gather_wait(idx0, buf0, gsem0)
            store(j, buf0, ssem0)

            @pl.when(j + 2 < per_worker)
            def _():
                store_wait(buf0, ssem0)
                gather(j + 2, idx0, buf0, gsem0)

            gather_wait(idx1, buf1, gsem1)
            store(j + 1, buf1, ssem1)
            return carry
        lax.fori_loop(0, per_worker // 2, two_windows, 0)
        store_wait(buf0, ssem0)
        store_wait(buf1, ssem1)

    return run(ys, pos_t.reshape(k_n))


MOE_RING = 16
MOE_WEIGHT_RING = 8


def _moe_kernel(st_ref, wg_ref, wu_ref, wd_ref, xs_ref, ys_ref, xbuf, ybuf, wgf, wuf, wdf, wgb, wub, wdb,
                semx, semy, semw):
    e = pl.program_id(0)
    n_e = pl.num_programs(0)

    def w_copies(ex):
        slot = ex % MOE_WEIGHT_RING
        return [pltpu.make_async_copy(src.at[ex], dst.at[slot], semw.at[slot, i])
                for i, (src, dst) in enumerate(((wg_ref, wgf), (wu_ref, wuf), (wd_ref, wdf)))]

    @pl.when(e == 0)
    def _():
        for ex in range(MOE_WEIGHT_RING - 1):
            @pl.when(ex < n_e)
            def _(ex=ex):
                for cp in w_copies(ex):
                    cp.start()

    @pl.when(e + (MOE_WEIGHT_RING - 1) < n_e)
    def _():
        for cp in w_copies(e + (MOE_WEIGHT_RING - 1)):
            cp.start()
    g0 = st_ref[e] // EXPERT_BLOCK
    nchunks = st_ref[e + 1] // EXPERT_BLOCK - g0
    ntot = st_ref[n_e] // EXPERT_BLOCK

    def rows(g):
        return pl.ds(pl.multiple_of(g * EXPERT_BLOCK, EXPERT_BLOCK), EXPERT_BLOCK)

    def x_copy(g):
        slot = g % MOE_RING
        return pltpu.make_async_copy(xs_ref.at[rows(g)], xbuf.at[slot], semx.at[slot])

    def y_copy(g):
        slot = g % MOE_RING
        return pltpu.make_async_copy(ybuf.at[slot], ys_ref.at[rows(g)], semy.at[slot])

    @pl.when(e == 0)
    def _():
        for g in range(MOE_RING - 1):
            @pl.when(g < ntot)
            def _(g=g):
                x_copy(g).start()

    for cp in w_copies(e):
        cp.wait()
    wslot = e % MOE_WEIGHT_RING
    wgb[...] = wgf[wslot].astype(BF16)
    wub[...] = wuf[wslot].astype(BF16)
    wdb[...] = wdf[wslot].astype(BF16)

    def take(g):
        @pl.when(g + (MOE_RING - 1) < ntot)
        def _():
            x_copy(g + (MOE_RING - 1)).start()

        x_copy(g).wait()

        @pl.when(g >= MOE_RING)
        def _():
            y_copy(g - MOE_RING).wait()

        return xbuf[g % MOE_RING]

    def expert_mlp(x_words):
        x = _unpack_rows(x_words).astype(BF16)
        mid = _silu(_dot(x, wgb[...])) * _dot(x, wub[...])
        return _pack_rows(_dot(mid.astype(BF16), wdb[...]))

    def put(g, y_words):
        ybuf[g % MOE_RING] = y_words
        y_copy(g).start()

    def run_chunks(g, m):
        xs = [take(g + i) for i in range(m)]
        y = expert_mlp(xs[0] if m == 1 else jnp.concatenate(xs, axis=0))
        for i in range(m):
            put(g + i, y[i * EXPERT_BLOCK:(i + 1) * EXPERT_BLOCK])

    def quad(p, carry):
        run_chunks(g0 + 4 * p, 4)
        return carry

    rest = nchunks % 4
    five = (rest == 1) & (nchunks >= 5)
    n_quads = nchunks // 4 - five.astype(I32)
    lax.fori_loop(0, n_quads, quad, 0)
    g_rest = g0 + 4 * n_quads

    @pl.when(five)
    def _():
        run_chunks(g_rest, 5)

    @pl.when(rest >= 2)
    def _():
        run_chunks(g_rest, 2)

    @pl.when((rest == 3) | (nchunks == 1))
    def _():
        run_chunks(g0 + nchunks - 1, 1)

    @pl.when(e == n_e - 1)
    def _():
        for j in range(MOE_RING):
            g = ntot - MOE_RING + j

            @pl.when(g >= 0)
            def _(g=g):
                y_copy(g).wait()


def _moe(starts, xs, wg, wu, wd):
    n_rows, c = xs.shape
    n_exp, d, f = wg.shape
    grid_spec = pltpu.PrefetchScalarGridSpec(
        num_scalar_prefetch=1, grid=(n_exp,),
        in_specs=[pl.BlockSpec(memory_space=pl.ANY)] * 4,
        out_specs=pl.BlockSpec(memory_space=pl.ANY),
        scratch_shapes=[pltpu.VMEM((MOE_RING, EXPERT_BLOCK, c), I32), pltpu.VMEM((MOE_RING, EXPERT_BLOCK, c), I32),
                        pltpu.VMEM((MOE_WEIGHT_RING, d, f), F32), pltpu.VMEM((MOE_WEIGHT_RING, d, f), F32),
                        pltpu.VMEM((MOE_WEIGHT_RING, f, d), F32),
                        pltpu.VMEM((d, f), BF16), pltpu.VMEM((d, f), BF16), pltpu.VMEM((f, d), BF16),
                        pltpu.SemaphoreType.DMA((MOE_RING,)), pltpu.SemaphoreType.DMA((MOE_RING,)),
                        pltpu.SemaphoreType.DMA((MOE_WEIGHT_RING, 3))])
    return pl.pallas_call(
        _moe_kernel, grid_spec=grid_spec, out_shape=jax.ShapeDtypeStruct((n_rows, c), I32),
        compiler_params=_cp(("arbitrary",)), name="moe",
    )(starts, wg, wu, wd, xs)


def _final_kernel(yg_ref, w8_ref, x1_ref, h2_ref, mod_ref, wsg_ref, wsu_ref, wsd_ref, nf_ref, y_ref, *, d):
    hb = _unpack_rows(h2_ref[...]).astype(BF16)
    shared = _dot((_silu(_dot(hb, wsg_ref[...])) * _dot(hb, wsu_ref[...])).astype(BF16), wsd_ref[...])
    w8 = w8_ref[...]
    routed = _unpack_rows(yg_ref[0]) * w8[:, 0:1]
    for k in range(1, TOP_K):
        routed = routed + _unpack_rows(yg_ref[k]) * w8[:, k:k + 1]
    g2 = mod_ref[...][:, (N_MOD - 1) * d:N_MOD * d]
    x2 = x1_ref[...] + g2 * (routed + shared)
    y_ref[...] = _rms(x2, nf_ref[...])


def _final(yg3, w8, x1, h2, mod, mod_spec, wts, nf, tf, n_tiles, tile0, out_shape, out_spec):
    d = x1.shape[1]
    rows = lambda i: (i, 0)
    in_specs = [pl.BlockSpec((TOP_K, tf, d // 2), lambda i: (0, tile0 + i, 0)),
                pl.BlockSpec((tf, LANE), rows),
                pl.BlockSpec((tf, d), rows), pl.BlockSpec((tf, d // 2), rows), mod_spec]
    in_specs += [_const_spec(w.shape) for w in wts]
    in_specs += [_const_spec(nf.shape)]
    return pl.pallas_call(
        functools.partial(_final_kernel, d=d), grid=(n_tiles,), in_specs=in_specs,
        out_specs=out_spec, out_shape=out_shape,
        compiler_params=_cp(("arbitrary",)), name="final",
    )(yg3, w8, x1, h2, mod, *wts, nf)


def kernel(x_prompt, x_sample, c_prompt, c_sample, state_pool, state_gla, w_ada, b_ada, norm1, w_in,
           pool_w, pool_scale, gla_w_alpha, gla_b_alpha, gla_norm, w_branch_a, w_branch_b, w_out, norm2,
           w_router, router_bias, w_exp_gate, w_exp_up, w_exp_down, w_sh_gate, w_sh_up, w_sh_down, norm_f):
    depth = w_in.shape[0]
    assert depth == 1, "single-layer trunk"
    bp, seq, d = x_prompt.shape
    bs, t_s, _ = x_sample.shape
    _, _, past, pw = state_pool.shape
    _, _, heads, dk, dv = state_gla.shape
    rank = gla_w_alpha.shape[1]
    n_exp = w_router.shape[2]
    assert past == POOL_HIST - 1 and pw == len(POOL_WINDOWS) * LANE and dv == LANE and rank <= LANE
    dm = _Dims(d, pw, heads, dk, dv)
    n_p, n_s = bp * seq, bs * t_s
    n_all = n_p + n_s

    off = [0]
    for sz in (pw, dm.hk, dm.hk, dm.gw, dm.gw, rank, d, d):
        off.append(off[-1] + sz)
    wi = w_in[0]
    win = (wi[:, :off[5]].astype(BF16), wi[:, off[6]:].astype(BF16),
           jnp.concatenate([wi[:, off[5]:off[6]], jnp.zeros((d, LANE - rank), F32)], axis=1).astype(BF16))
    assert off[5] + 2 * d + LANE == dm.zw
    wal = jnp.concatenate([gla_w_alpha[0], jnp.zeros((LANE - rank, dm.hk), F32)], axis=0).astype(BF16)
    bal = gla_b_alpha[0].reshape(1, dm.hk)
    wr = w_router[0]
    wrh = wr.astype(BF16)
    wrl = (wr - wrh.astype(F32)).astype(BF16)
    post_w = (pool_w[0].astype(BF16), pool_scale[0].reshape(1, pw), gla_norm[0].reshape(1, dm.gw),
              w_branch_a[0].astype(BF16), w_branch_b[0].astype(BF16), w_out[0].astype(BF16),
              norm2[0].reshape(1, d), wrh, wrl)
    n1 = norm1[0].reshape(1, d)

    mod = _ada(jnp.concatenate([c_prompt, c_sample], axis=0), w_ada[0], b_ada[0])
    mod_p = mod[:bp].reshape(bp, 1, N_MOD * d)
    mod_s = mod[bp:]

    rbias = router_bias[0].reshape(1, n_exp)
    x1_p, h2_p, idx_p, w8_p, cnt_p, npool_p, ngla_p = _mix_prompt(
        dm, x_prompt.reshape(n_p, d), mod_p, (n1,) + win + (wal, bal) + post_w + (rbias,), bp, seq, n_exp)

    xs2d = x_sample.reshape(bs, t_s * d)
    z_s = _sample_in(dm, xs2d, mod_s, n1, win, bs, t_s)
    mixed_s, o_s, npool_s, ngla_s = _sample_state(
        dm, z_s, state_pool[0].reshape(bs, past * pw), state_gla[0].reshape(bs, dm.hk, dv), wal, bal, bs, t_s)
    x1_s, h2_s, lg_s = _sample_post(dm, xs2d, mod_s, z_s, mixed_s, o_s, post_w, bs, t_s, n_exp)

    tr = 512 if (n_p % 512 == 0 and n_s % 512 == 0) else LANE
    assert n_p % tr == 0 and n_s % tr == 0 and n_p % bs == 0
    idx_s, w8_s, cnt_s = _route(lg_s, rbias, tr)
    nblk = (n_all * TOP_K) // EXPERT_BLOCK + n_exp
    pos_t, bounds = _plan(jnp.concatenate([idx_p, idx_s], axis=1), cnt_p + cnt_s, tr)
    starts = jnp.concatenate([bounds[0], bounds[1, n_exp - 1:]])

    xs = _dispatch(h2_p, h2_s, pos_t, nblk * EXPERT_BLOCK)
    ys = _moe(starts, xs, w_exp_gate[0], w_exp_up[0], w_exp_down[0])
    yg3 = _combine_gather(ys, pos_t).reshape(TOP_K, n_all, d // 2)

    sh_w = (w_sh_gate[0].astype(BF16), w_sh_up[0].astype(BF16), w_sh_down[0].astype(BF16))
    nf = norm_f.reshape(1, d)
    tf = 512 if seq % 512 == 0 else seq
    n_tp = seq // tf
    y_p = _final(
        yg3, w8_p, x1_p, h2_p, mod_p,
        pl.BlockSpec((None, 1, N_MOD * d), lambda i: (i // n_tp, 0, 0)), sh_w, nf, tf, n_p // tf, 0,
        jax.ShapeDtypeStruct((n_p, d), F32), pl.BlockSpec((tf, d), lambda i: (i, 0)))
    y_s = _final(
        yg3, w8_s, x1_s, h2_s, mod_s,
        _const_spec(mod_s.shape), sh_w, nf, bs, t_s, n_p // bs,
        jax.ShapeDtypeStruct((bs, t_s * d), F32), pl.BlockSpec((bs, d), lambda i: (0, i)))

    return (y_p.reshape(bp, seq, d), y_s.reshape(bs, t_s, d),
            npool_p.reshape(depth, bp, past, pw), ngla_p.reshape(depth, bp, heads, dk, dv),
            npool_s.reshape(depth, bs, past, pw), ngla_s.reshape(depth, bs, heads, dk, dv))
```

```python
import functools

import jax
import jax.numpy as jnp
from jax import lax
from jax.experimental import pallas as pl
from jax.experimental.pallas import tpu as pltpu
from jax.experimental.pallas import tpu_sc as plsc

F32 = jnp.float32
BF16 = jnp.bfloat16
I32 = jnp.int32

EPS = 1e-6
N_MOD = 6
POOL_WINDOWS = (2, 4, 8, 16)
POOL_HIST = 16
GLA_GATE_TEMP = 16.0
TOP_K = 8
ROUTED_SCALE = 2.5
PAST_LEN = 16384
EXPERT_BLOCK = 128
LANE = 128
GLA_FAST_MAX_DECAY = 40.0
PROJ_PIECE = 512
PROJ_LEAD = 2
VMEM_LIMIT = 56 * 1024 * 1024


def _cp(sem, vmem=VMEM_LIMIT):
    return pltpu.CompilerParams(dimension_semantics=sem, vmem_limit_bytes=vmem)


def _rms(x, g):
    return x * lax.rsqrt(jnp.mean(x * x, axis=-1, keepdims=True) + EPS) * g


def _silu(x):
    return x * jax.nn.sigmoid(x)


def _dot(a, b):
    return jnp.dot(a, b, preferred_element_type=F32)


def _split3(a):
    a0 = a.astype(BF16)
    r1 = a - a0.astype(F32)
    a1 = r1.astype(BF16)
    a2 = (r1 - a1.astype(F32)).astype(BF16)
    return a0, a1, a2


def _mod_parts(mod, d):
    return [mod[:, i * d:(i + 1) * d] for i in range(N_MOD)]


_HI16 = -65536


def _pack_rows(v):
    c = v.shape[1] // 2
    lo = lax.bitcast_convert_type(v[:, :c].astype(BF16).astype(F32), I32)
    hi = lax.bitcast_convert_type(v[:, c:].astype(BF16).astype(F32), I32)
    return (hi & _HI16) | lax.shift_right_logical(lo, 16)


def _unpack_rows(w):
    lo = lax.bitcast_convert_type(lax.shift_left(w, 16), F32)
    hi = lax.bitcast_convert_type(w & _HI16, F32)
    return jnp.concatenate([lo, hi], axis=1)


def _ada_kernel(c_ref, w_ref, b_ref, o_ref):
    a = _silu(c_ref[...]).astype(BF16)
    o_ref[...] = _dot(a, w_ref[...].astype(BF16)) + b_ref[...]


def _ada(c_all, w_ada, b_ada):
    n, d = c_all.shape
    cols = w_ada.shape[1]
    bc = d
    return pl.pallas_call(
        _ada_kernel,
        grid=(cols // bc,),
        in_specs=[pl.BlockSpec((n, d), lambda i: (0, 0)),
                  pl.BlockSpec((d, bc), lambda i: (0, i)),
                  pl.BlockSpec((1, bc), lambda i: (0, i))],
        out_specs=pl.BlockSpec((n, bc), lambda i: (0, i)),
        out_shape=jax.ShapeDtypeStruct((n, cols), F32),
        compiler_params=_cp(("arbitrary",)),
        name="ada",
    )(c_all, w_ada, b_ada.reshape(1, cols))


class _Dims:
    def __init__(self, d, pw, heads, dk, dv):
        self.d, self.pw, self.h, self.dk, self.dv = d, pw, heads, dk, dv
        self.hk, self.gw = heads * dk, heads * dv
        self.q0 = pw
        self.k0 = self.q0 + self.hk
        self.v0 = self.k0 + self.hk
        self.go0 = self.v0 + self.gw
        self.ga0 = self.go0 + self.gw
        self.gb0 = self.ga0 + d
        self.al0 = self.gb0 + d
        self.zw = self.al0 + LANE


def _log_decay(z_al, wal_ref, bal_ref):
    xal = _dot(z_al.astype(BF16), wal_ref[...]) + bal_ref[...]
    return jax.nn.log_sigmoid(xal) / GLA_GATE_TEMP


def _pool_project(mixed_groups, pw_ref, ps_ref):
    ys = [_dot(jnp.concatenate(mixed_groups[2 * p:2 * p + 2], axis=1).astype(BF16), pw_ref[p])
          for p in range(len(mixed_groups) // 2)]
    return jnp.concatenate(ys, axis=1) * ps_ref[...]


def _post_mix(dm, x, mod, ya, o, go, ga, gb, gn_ref, wa_ref, wb_ref, wo_ref, n2_ref, wrh_ref, wrl_ref,
              filler=lambda: None):
    _, _, g1, sh2, sc2, _ = _mod_parts(mod, dm.d)
    parts = []
    for h in range(dm.h):
        oh = o[:, h * dm.dv:(h + 1) * dm.dv]
        parts.append(oh * lax.rsqrt(jnp.mean(oh * oh, axis=-1, keepdims=True) + EPS))
    yb = jnp.concatenate(parts, axis=1) * gn_ref[...] * _silu(go)
    filler()
    m = (jax.nn.sigmoid(ga) * _dot(ya.astype(BF16), wa_ref[...])
         + jax.nn.sigmoid(gb) * _dot(yb.astype(BF16), wb_ref[...]))
    filler()
    x1 = x + g1 * _dot(m.astype(BF16), wo_ref[...])
    h2 = _rms(x1, n2_ref[...]) * (1 + sc2) + sh2
    hi = h2.astype(BF16)
    lo = (h2 - hi.astype(F32)).astype(BF16)
    filler()
    logits = _dot(hi, wrh_ref[...]) + (_dot(hi, wrl_ref[...]) + _dot(lo, wrh_ref[...]))
    filler()
    return x1, _pack_rows(h2), logits


def _topk_rows(logits, bias):
    tr, n_exp = logits.shape
    s = jax.nn.sigmoid(logits)
    cur = s + bias
    lane = lax.broadcasted_iota(I32, (1, n_exp), 1).astype(F32)
    slot = lax.broadcasted_iota(I32, (1, LANE), 1)
    idx8 = jnp.zeros((tr, LANE), F32)
    w8 = jnp.zeros((tr, LANE), F32)
    wsum = jnp.zeros((tr, 1), F32)
    picked = jnp.zeros((tr, n_exp), F32)
    for k in range(TOP_K):
        m = jnp.max(cur, axis=-1, keepdims=True)
        am = jnp.min(jnp.where(cur == m, lane, float(n_exp)), axis=-1, keepdims=True)
        oh = lane == am
        sk = jnp.sum(jnp.where(oh, s, 0.0), axis=-1, keepdims=True)
        cur = jnp.where(oh, -jnp.inf, cur)
        idx8 = jnp.where(slot == k, am, idx8)
        w8 = jnp.where(slot == k, sk, w8)
        wsum = wsum + sk
        picked = picked + jnp.where(oh, 1.0, 0.0)
    return idx8.T[0:TOP_K, :].astype(I32), w8 / wsum * ROUTED_SCALE, picked.astype(BF16)


def _count_members(picked):
    return _dot(jnp.ones((8, picked.shape[0]), BF16), picked)


def _in_project(dm, x, mod, n1_ref, win_refs, z_ref):
    h = (_rms(x, n1_ref[...]) * (1 + mod[:, dm.d:2 * dm.d]) + mod[:, 0:dm.d]).astype(BF16)
    starts = [0]
    for w_ref in win_refs:
        starts.append(starts[-1] + w_ref.shape[1])
    for i in (0, 2, 1):
        z_ref[:, starts[i]:starts[i + 1]] = _dot(h, win_refs[i][...])


def _mixp_kernel(x_ref, mod_ref, n1_ref, wia_ref, wig_ref, wil_ref, wal_ref, bal_ref, pw_ref, ps_ref, gn_ref,
                 wa_ref, wb_ref, wo_ref, n2_ref, wrh_ref, wrl_ref, rb_ref,
                 x1_ref, h2_ref, idx_ref, w8_ref, cnt_ref, np_ref, ng_ref,
                 za_ref, zb_ref, xk_ref, mk_ref, hn_ref, lg_ref, uext_ref, s_ref, b_ref, oi_ref,
                 *, dm, tt, n_t, tiles):
    n = pl.program_id(0)
    j = jnp.clip(n - 1, 0, tiles - 1) % n_t

    @pl.when(n == 0)
    def _():
        za_ref[...] = jnp.zeros(za_ref.shape, F32)
        zb_ref[...] = jnp.zeros(zb_ref.shape, F32)
        xk_ref[...] = jnp.zeros(xk_ref.shape, F32)
        mk_ref[...] = jnp.zeros(mk_ref.shape, F32)
        lg_ref[...] = jnp.zeros(lg_ref.shape, F32)
        cnt_ref[...] = jnp.zeros(cnt_ref.shape, F32)

    def route_previous():
        idx_ref[...], w8_ref[...], picked = _topk_rows(lg_ref[...], rb_ref[...])
        return picked

    def count_routed(picked):
        cnt_ref[...] += jnp.where(n >= 2, _count_members(picked), 0.0)

    @pl.when((j == 0) & (n <= tiles))
    def _():
        uext_ref[0:POOL_HIST, :] = jnp.zeros((POOL_HIST, dm.pw), F32)
        s_ref[...] = jnp.zeros(s_ref.shape, F32)

    @pl.when((n % 2 == 0) & (n <= tiles))
    def _():
        _mixp_region(za_ref, zb_ref, x_ref, mod_ref, n1_ref, wia_ref, wig_ref, wil_ref, wal_ref, bal_ref, pw_ref,
                     ps_ref, gn_ref, wa_ref, wb_ref, wo_ref, n2_ref, wrh_ref, wrl_ref, x1_ref, h2_ref, lg_ref,
                     xk_ref, mk_ref, hn_ref, uext_ref, s_ref, b_ref, oi_ref, dm=dm, tt=tt, j=j,
                     route_previous=route_previous, count_routed=count_routed)

    @pl.when((n % 2 == 1) & (n <= tiles))
    def _():
        _mixp_region(zb_ref, za_ref, x_ref, mod_ref, n1_ref, wia_ref, wig_ref, wil_ref, wal_ref, bal_ref, pw_ref,
                     ps_ref, gn_ref, wa_ref, wb_ref, wo_ref, n2_ref, wrh_ref, wrl_ref, x1_ref, h2_ref, lg_ref,
                     xk_ref, mk_ref, hn_ref, uext_ref, s_ref, b_ref, oi_ref, dm=dm, tt=tt, j=j,
                     route_previous=route_previous, count_routed=count_routed)

    @pl.when(n > tiles)
    def _():
        count_routed(route_previous())

    xk_ref[...] = x_ref[...]
    mk_ref[...] = mod_ref[...]

    @pl.when((j == n_t - 1) & (n <= tiles))
    def _():
        np_ref[...] = uext_ref[tt + 1:tt + POOL_HIST, :]
        ng_ref[...] = s_ref[...]


def _mixp_region(zp_ref, z_ref, x_ref, mod_ref, n1_ref, wia_ref, wig_ref, wil_ref, wal_ref, bal_ref, pw_ref,
                 ps_ref, gn_ref, wa_ref, wb_ref, wo_ref, n2_ref, wrh_ref, wrl_ref, x1_ref, h2_ref, lg_ref,
                 xk_ref, mk_ref, hn_ref, uext_ref, s_ref, b_ref, oi_ref, *, dm, tt, j, route_previous, count_routed):
    picked = route_previous()
    scale = dm.dk ** -0.5
    lane = lax.broadcasted_iota(I32, (1, dm.hk), 1)
    head_masks = [(lane >= h * dm.dk) & (lane < (h + 1) * dm.dk) for h in range(dm.h)]
    causal = lax.broadcasted_iota(I32, (tt, tt), 0) >= lax.broadcasted_iota(I32, (tt, tt), 1)

    def pool_branch():
        pos = j * tt + lax.broadcasted_iota(I32, (tt, 1), 0)
        gw = dm.pw // len(POOL_WINDOWS)
        mixed = []
        for g, w in enumerate(POOL_WINDOWS):
            c0 = g * gw
            cur = uext_ref[POOL_HIST:POOL_HIST + tt, c0:c0 + gw]
            acc = cur
            for i in range(1, w):
                acc = acc + uext_ref[POOL_HIST - i:POOL_HIST - i + tt, c0:c0 + gw]
            cnt = jnp.minimum(w, pos + 1).astype(F32)
            mixed.append(acc / cnt - cur)
        return _pool_project(mixed, pw_ref, ps_ref)

    def scaled_queries(bc):
        return (z_ref[:, dm.q0:dm.q0 + dm.hk] * scale) * jnp.exp(bc)

    def inter_chunk(qe):
        s_rows = lax.broadcasted_iota(I32, (dm.hk, dm.gw), 0) // dm.dk
        s_cols = lax.broadcasted_iota(I32, (dm.hk, dm.gw), 1) // dm.dv
        s_all = jnp.where(s_rows == s_cols, jnp.concatenate([s_ref[...]] * dm.h, axis=1), 0.0)
        return _dot(qe.astype(BF16), s_all.astype(BF16))

    def finish(ya, o, filler=lambda: None):
        x1, h2, logits = _post_mix(
            dm, xk_ref[...], mk_ref[...], ya, o, z_ref[:, dm.go0:dm.go0 + dm.gw],
            z_ref[:, dm.ga0:dm.ga0 + dm.d], z_ref[:, dm.gb0:dm.gb0 + dm.d],
            gn_ref, wa_ref, wb_ref, wo_ref, n2_ref, wrh_ref, wrl_ref, filler)
        x1_ref[...] = x1
        h2_ref[...] = h2
        lg_ref[...] = logits

    mod_in = mod_ref[...]
    hn_ref[...] = (_rms(x_ref[...], n1_ref[...]) * (1 + mod_in[:, dm.d:2 * dm.d]) + mod_in[:, 0:dm.d]).astype(BF16)
    pieces = iter([(w_ref, a, min(a + PROJ_PIECE, w_ref.shape[1]), c0)
                   for w_ref, c0 in ((wia_ref, 0), (wil_ref, dm.al0), (wig_ref, dm.ga0))
                   for a in range(0, w_ref.shape[1], PROJ_PIECE)])

    def project_piece():
        piece = next(pieces, None)
        if piece is not None:
            w_ref, a, b, c0 = piece
            zp_ref[:, c0 + a:c0 + b] = _dot(hn_ref[...], w_ref[:, a:b])
        return piece is not None

    for _ in range(PROJ_LEAD):
        project_piece()
    uext_ref[POOL_HIST:POOL_HIST + tt, :] = z_ref[:, 0:dm.pw]
    ya = pool_branch()
    project_piece()
    la = _log_decay(z_ref[:, dm.al0:dm.al0 + LANE], wal_ref, bal_ref)
    tri = jnp.where(causal, 1.0, 0.0).astype(BF16)
    a0, a1, a2 = _split3(la)
    bc = _dot(tri, a0) + (_dot(tri, a1) + _dot(tri, a2))
    b_ref[...] = bc
    project_piece()
    qe = scaled_queries(bc)
    k = z_ref[:, dm.k0:dm.k0 + dm.hk]
    vb = z_ref[:, dm.v0:dm.v0 + dm.gw].astype(BF16)
    blast = bc[tt - 1:tt, :]
    fast = jnp.max(-blast) <= GLA_FAST_MAX_DECAY
    ke = (k * jnp.exp(-bc)).astype(BF16)
    intra = []
    for h in range(dm.h):
        qh = jnp.where(head_masks[h], qe, 0.0).astype(BF16)
        sc = lax.dot_general(qh, ke, (((1,), (1,)), ((), ())), preferred_element_type=F32)
        sc = jnp.where(causal, sc, 0.0).astype(BF16)
        intra.append(_dot(sc, vb[:, h * dm.dv:(h + 1) * dm.dv]))
    count_routed(picked)
    project_piece()
    finish(ya, inter_chunk(qe) + jnp.concatenate(intra, axis=1), project_piece)
    while project_piece():
        pass

    @pl.when(jnp.logical_not(fast))
    def _():
        rows = lax.broadcasted_iota(I32, (tt, 1), 0)
        for h in range(dm.h):
            def body(tb, carry, h=h):
                base = pl.multiple_of(tb * 8, 8)
                q8 = z_ref[pl.ds(base, 8), dm.q0:dm.q0 + dm.hk] * scale
                b8 = b_ref[pl.ds(base, 8), :]
                vh = z_ref[:, dm.v0 + h * dm.dv:dm.v0 + (h + 1) * dm.dv]
                out_rows = []
                for r in range(8):
                    keep = (rows <= base + r) & head_masks[h]
                    dec = jnp.exp(jnp.where(keep, b8[r:r + 1, :] - b_ref[...], -jnp.inf))
                    wgt = (q8[r:r + 1, :] * dec) * z_ref[:, dm.k0:dm.k0 + dm.hk]
                    s_col = jnp.sum(wgt, axis=-1, keepdims=True)
                    out_rows.append(jnp.sum(s_col * vh, axis=0, keepdims=True))
                oi_ref[pl.ds(base, 8), h * dm.dv:(h + 1) * dm.dv] = jnp.concatenate(out_rows, axis=0)
                return carry
            lax.fori_loop(0, tt // 8, body, 0)
        finish(pool_branch(), inter_chunk(scaled_queries(b_ref[...])) + oi_ref[...])

    bc = b_ref[...]
    blast = bc[tt - 1:tt, :]
    kd = z_ref[:, dm.k0:dm.k0 + dm.hk] * jnp.exp(blast - bc)
    u_all = _dot(kd.T.astype(BF16), z_ref[:, dm.v0:dm.v0 + dm.gw].astype(BF16))
    dec_col = jnp.broadcast_to(jnp.exp(blast), (dm.hk, dm.hk)).T[:, 0:dm.dv]
    upd = jnp.concatenate(
        [u_all[h * dm.dk:(h + 1) * dm.dk, h * dm.dv:(h + 1) * dm.dv] for h in range(dm.h)], axis=0)
    s_ref[...] = dec_col * s_ref[...] + upd
    uext_ref[0:POOL_HIST, :] = uext_ref[tt:tt + POOL_HIST, :]


def _const_spec(shape):
    nd = len(shape)
    return pl.BlockSpec(shape, lambda *_: (0,) * nd, pipeline_mode=pl.Buffered(1))


def _mix_prompt(dm, x2d, mod3, wts, bsz, seq, n_exp):
    tt = 256 if seq % 256 == 0 else seq
    n_t = seq // tt
    n = bsz * seq
    d = dm.d
    tiles = bsz * n_t
    kern = functools.partial(_mixp_kernel, dm=dm, tt=tt, n_t=n_t, tiles=tiles)
    proj = lambda i: jnp.minimum(i, tiles - 1)
    fin = lambda i: jnp.clip(i - 1, 0, tiles - 1)
    routed = lambda i: jnp.clip(i - 2, 0, tiles - 1)
    row = lambda i: (fin(i), 0)
    in_specs = [pl.BlockSpec((tt, d), lambda i: (proj(i), 0)),
                pl.BlockSpec((None, 1, N_MOD * d), lambda i: (proj(i) // n_t, 0, 0))]
    in_specs += [_const_spec(w.shape) for w in wts]
    out_specs = [pl.BlockSpec((tt, d), row), pl.BlockSpec((tt, d // 2), row),
                 pl.BlockSpec((TOP_K, tt), lambda i: (0, routed(i))), pl.BlockSpec((tt, LANE), lambda i: (routed(i), 0)),
                 pl.BlockSpec((8, n_exp), lambda i: (0, 0)),
                 pl.BlockSpec((None, POOL_HIST - 1, dm.pw), lambda i: (fin(i) // n_t, 0, 0)),
                 pl.BlockSpec((None, dm.hk, dm.dv), lambda i: (fin(i) // n_t, 0, 0))]
    out_shape = [jax.ShapeDtypeStruct((n, d), F32), jax.ShapeDtypeStruct((n, d // 2), I32),
                 jax.ShapeDtypeStruct((TOP_K, n), I32), jax.ShapeDtypeStruct((n, LANE), F32),
                 jax.ShapeDtypeStruct((8, n_exp), F32),
                 jax.ShapeDtypeStruct((bsz, POOL_HIST - 1, dm.pw), F32),
                 jax.ShapeDtypeStruct((bsz, dm.hk, dm.dv), F32)]
    scratch = [pltpu.VMEM((tt, dm.zw), F32), pltpu.VMEM((tt, dm.zw), F32),
               pltpu.VMEM((tt, d), F32), pltpu.VMEM((1, N_MOD * d), F32), pltpu.VMEM((tt, d), BF16),
               pltpu.VMEM((tt, n_exp), F32), pltpu.VMEM((tt + POOL_HIST, dm.pw), F32),
               pltpu.VMEM((dm.hk, dm.dv), F32), pltpu.VMEM((tt, dm.hk), F32), pltpu.VMEM((tt, dm.gw), F32)]
    return pl.pallas_call(
        kern, grid=(tiles + 2,), in_specs=in_specs, out_specs=out_specs, out_shape=out_shape,
        scratch_shapes=scratch, compiler_params=_cp(("arbitrary",)), name="mixp",
    )(x2d, mod3, *wts)


def _sin_kernel(x_ref, mod_ref, n1_ref, wia_ref, wig_ref, wil_ref, z_ref, *, dm):
    _in_project(dm, x_ref[...], mod_ref[...], n1_ref, (wia_ref, wig_ref, wil_ref), z_ref)


def _sample_in(dm, xs2d, mod_s, n1, win, bs, t_s):
    d = dm.d
    return pl.pallas_call(
        functools.partial(_sin_kernel, dm=dm),
        grid=(t_s,),
        in_specs=[pl.BlockSpec((bs, d), lambda t: (0, t)), _const_spec(mod_s.shape), _const_spec(n1.shape)]
        + [_const_spec(w.shape) for w in win],
        out_specs=pl.BlockSpec((bs, dm.zw), lambda t: (t, 0)),
        out_shape=jax.ShapeDtypeStruct((t_s * bs, dm.zw), F32),
        compiler_params=_cp(("arbitrary",)), name="sample_in",
    )(xs2d, mod_s, n1, *win)


def _sst_kernel(z_ref, pool_ref, gs_ref, wal_ref, bal_ref,
                mixed_ref, o_ref, npool_ref, ngs_ref,
                qt_ref, kt_ref, dt_ref, vt_ref, ot_ref, *, dm, bs, t_s, ch, pos0):
    i = pl.program_id(0)
    n_i = pl.num_programs(0)
    scale = dm.dk ** -0.5
    past = POOL_HIST - 1

    @pl.when(i == 0)
    def _():
        for t in range(t_s):
            r0 = t * bs
            la = _log_decay(z_ref[r0:r0 + bs, dm.al0:dm.al0 + LANE], wal_ref, bal_ref)
            dt_ref[t] = jnp.exp(la).T
            qt_ref[t] = (z_ref[r0:r0 + bs, dm.q0:dm.q0 + dm.hk] * scale).T
            kt_ref[t] = z_ref[r0:r0 + bs, dm.k0:dm.k0 + dm.hk].T
            vt_ref[t] = z_ref[r0:r0 + bs, dm.v0:dm.v0 + dm.gw].T
        ot_ref[...] = jnp.zeros(ot_ref.shape, F32)

        def item(s, c0, c1):
            if s >= 0:
                return z_ref[s * bs:(s + 1) * bs, c0:c1]
            return pool_ref[:, (past + s) * dm.pw + c0:(past + s) * dm.pw + c1]

        gw = dm.pw // len(POOL_WINDOWS)
        for t in range(t_s):
            for g, w in enumerate(POOL_WINDOWS):
                c0, c1 = g * gw, (g + 1) * gw
                cur = item(t, c0, c1)
                acc = cur
                for r in range(1, w):
                    acc = acc + item(t - r, c0, c1)
                cnt = float(min(w, pos0 + t + 1))
                mixed_ref[t * bs:(t + 1) * bs, c0:c1] = acc / cnt - cur
        for r in range(past):
            npool_ref[:, r * dm.pw:(r + 1) * dm.pw] = item(r - past + t_s, 0, dm.pw)

    per_head = dm.dk // ch
    hoff = pl.multiple_of((i // per_head) * dm.dv, dm.dv)
    for p in range(ch):
        hd = i * ch + p
        st = gs_ref[:, p, :].T
        for t in range(t_s):
            st = dt_ref[t, pl.ds(hd, 1), :] * st + kt_ref[t, pl.ds(hd, 1), :] * vt_ref[t, pl.ds(hoff, dm.dv), :]
            ot_ref[t, pl.ds(hoff, dm.dv), :] += qt_ref[t, pl.ds(hd, 1), :] * st
        ngs_ref[:, p, :] = st.T

    @pl.when(i == n_i - 1)
    def _():
        for t in range(t_s):
            for h in range(dm.h):
                o_ref[t * bs:(t + 1) * bs, h * dm.dv:(h + 1) * dm.dv] = ot_ref[t, h * dm.dv:(h + 1) * dm.dv, :].T


def _sample_state(dm, z_s, pool2d, gs3, wal, bal, bs, t_s):
    ch = 16
    n_i = dm.hk // ch
    past = POOL_HIST - 1
    kern = functools.partial(_sst_kernel, dm=dm, bs=bs, t_s=t_s, ch=ch, pos0=PAST_LEN)
    full = lambda shape: pl.BlockSpec(shape, lambda i: (0,) * len(shape))
    return pl.pallas_call(
        kern, grid=(n_i,),
        in_specs=[_const_spec(z_s.shape), _const_spec(pool2d.shape),
                  pl.BlockSpec((bs, ch, dm.dv), lambda i: (0, i, 0)),
                  _const_spec(wal.shape), _const_spec(bal.shape)],
        out_specs=[full((t_s * bs, dm.pw)), full((t_s * bs, dm.gw)), full((bs, past * dm.pw)),
                   pl.BlockSpec((bs, ch, dm.dv), lambda i: (0, i, 0))],
        out_shape=[jax.ShapeDtypeStruct((t_s * bs, dm.pw), F32), jax.ShapeDtypeStruct((t_s * bs, dm.gw), F32),
                   jax.ShapeDtypeStruct((bs, past * dm.pw), F32), jax.ShapeDtypeStruct(gs3.shape, F32)],
        scratch_shapes=[pltpu.VMEM((t_s, dm.hk, bs), F32), pltpu.VMEM((t_s, dm.hk, bs), F32),
                        pltpu.VMEM((t_s, dm.hk, bs), F32), pltpu.VMEM((t_s, dm.gw, bs), F32),
                        pltpu.VMEM((t_s, dm.gw, bs), F32)],
        compiler_params=_cp(("arbitrary",)), name="sample_state",
    )(z_s, pool2d, gs3, wal, bal)


def _spost_kernel(x_ref, mod_ref, z_ref, mixed_ref, o_ref, pw_ref, ps_ref, gn_ref,
                  wa_ref, wb_ref, wo_ref, n2_ref, wrh_ref, wrl_ref,
                  x1_ref, h2_ref, lg_ref, *, dm):
    gw = dm.pw // len(POOL_WINDOWS)
    ya = _pool_project([mixed_ref[:, g * gw:(g + 1) * gw] for g in range(len(POOL_WINDOWS))], pw_ref, ps_ref)
    x1, h2, logits = _post_mix(
        dm, x_ref[...], mod_ref[...], ya, o_ref[...],
        z_ref[:, dm.go0:dm.go0 + dm.gw], z_ref[:, dm.ga0:dm.ga0 + dm.d], z_ref[:, dm.gb0:dm.gb0 + dm.d],
        gn_ref, wa_ref, wb_ref, wo_ref, n2_ref, wrh_ref, wrl_ref)
    x1_ref[...] = x1
    h2_ref[...] = h2
    lg_ref[...] = logits


def _sample_post(dm, xs2d, mod_s, z_s, mixed_s, o_s, wts, bs, t_s, n_exp):
    d = dm.d
    rows = lambda t: (t, 0)
    in_specs = [pl.BlockSpec((bs, d), lambda t: (0, t)), _const_spec(mod_s.shape),
                pl.BlockSpec((bs, dm.zw), rows), pl.BlockSpec((bs, dm.pw), rows), pl.BlockSpec((bs, dm.gw), rows)]
    in_specs += [_const_spec(w.shape) for w in wts]
    return pl.pallas_call(
        functools.partial(_spost_kernel, dm=dm), grid=(t_s,), in_specs=in_specs,
        out_specs=[pl.BlockSpec((bs, d), rows), pl.BlockSpec((bs, d // 2), rows), pl.BlockSpec((bs, n_exp), rows)],
        out_shape=[jax.ShapeDtypeStruct((t_s * bs, d), F32), jax.ShapeDtypeStruct((t_s * bs, d // 2), I32),
                   jax.ShapeDtypeStruct((t_s * bs, n_exp), F32)],
        compiler_params=_cp(("arbitrary",)), name="sample_post",
    )(xs2d, mod_s, z_s, mixed_s, o_s, *wts)


def _route_kernel(lg_ref, bias_ref, idx_ref, w_ref, cnt_ref):
    i = pl.program_id(0)
    idx_ref[...], w_ref[...], picked = _topk_rows(lg_ref[...], bias_ref[...])

    @pl.when(i == 0)
    def _():
        cnt_ref[...] = jnp.zeros(cnt_ref.shape, F32)

    cnt_ref[...] += _count_members(picked)


def _route(logits, bias, tr):
    n, n_exp = logits.shape
    rows = lambda i: (i, 0)
    return pl.pallas_call(
        _route_kernel, grid=(n // tr,),
        in_specs=[pl.BlockSpec((tr, n_exp), rows), _const_spec((1, n_exp))],
        out_specs=[pl.BlockSpec((TOP_K, tr), lambda i: (0, i)), pl.BlockSpec((tr, LANE), rows),
                   pl.BlockSpec((8, n_exp), lambda i: (0, 0))],
        out_shape=[jax.ShapeDtypeStruct((TOP_K, n), I32), jax.ShapeDtypeStruct((n, LANE), F32),
                   jax.ShapeDtypeStruct((8, n_exp), F32)],
        compiler_params=_cp(("arbitrary",)), name="route",
    )(logits, bias)


def _plan_kernel(idx_ref, cnt_ref, pos_ref, bounds_ref, base_ref, dest_ref, *, n_exp):
    i = pl.program_id(0)
    tr = idx_ref.shape[1]

    @pl.when(i == 0)
    def _():
        lane = lax.broadcasted_iota(I32, (1, n_exp), 1)
        cnt = cnt_ref[...].astype(I32)
        padded = ((cnt + (EXPERT_BLOCK - 1)) // EXPERT_BLOCK) * EXPERT_BLOCK
        pe = padded
        s = 1
        while s < n_exp:
            pe = pe + jnp.where(lane >= s, pltpu.roll(pe, s, 1), 0)
            s *= 2
        row = lax.broadcasted_iota(I32, (8, n_exp), 0)
        bounds_ref[...] = jnp.where(row == 0, pe - padded, pe)
        first = (pe - padded).astype(F32)[0:1, :]
        base_ref[...] = jnp.broadcast_to(first, (LANE, n_exp)).T

    eidx = lax.broadcasted_iota(I32, (n_exp, 1), 0)
    idx = idx_ref[...]
    member = eidx == idx[0:1, :]
    for k in range(1, TOP_K):
        member = member | (eidx == idx[k:k + 1, :])
    mb = jnp.where(member, 1.0, 0.0).astype(BF16)
    before = jnp.where(lax.broadcasted_iota(I32, (tr, tr), 0) < lax.broadcasted_iota(I32, (tr, tr), 1),
                       1.0, 0.0).astype(BF16)
    dest_ref[...] = _dot(mb, before) + base_ref[:, 0:1]
    for b in range(tr // LANE):
        cols = slice(b * LANE, (b + 1) * LANE)
        dest = dest_ref[:, cols]
        rows = [jnp.sum(jnp.where(eidx == idx[k:k + 1, cols], dest, 0.0), axis=0, keepdims=True)
                for k in range(TOP_K)]
        pos_ref[:, cols] = jnp.concatenate(rows, axis=0).astype(I32)
    base_ref[...] += jnp.sum(mb.astype(F32), axis=1, keepdims=True)


def _plan(idx_t, counts, tr):
    n = idx_t.shape[1]
    n_exp = counts.shape[1]
    cols = lambda i: (0, i)
    return pl.pallas_call(
        functools.partial(_plan_kernel, n_exp=n_exp), grid=(n // tr,),
        in_specs=[pl.BlockSpec((TOP_K, tr), cols), _const_spec((8, n_exp))],
        out_specs=[pl.BlockSpec((TOP_K, tr), cols), pl.BlockSpec((8, n_exp), lambda i: (0, 0))],
        out_shape=[jax.ShapeDtypeStruct((TOP_K, n), I32), jax.ShapeDtypeStruct((8, n_exp), I32)],
        scratch_shapes=[pltpu.VMEM((n_exp, LANE), F32), pltpu.VMEM((n_exp, tr), F32)],
        compiler_params=_cp(("arbitrary",)), name="plan",
    )(idx_t, counts)


SC_CORES = 2
SC_SUBCORES = 16
SC_AXES = ("core", "subcore")
SC_WORKERS = SC_CORES * SC_SUBCORES
SC_WINDOW = 128


def _sc_mesh():
    return plsc.VectorSubcoreMesh(core_axis_name=SC_AXES[0], subcore_axis_name=SC_AXES[1],
                                  num_cores=SC_CORES, num_subcores=SC_SUBCORES)


def _sc_worker():
    return lax.axis_index(SC_AXES[1]) * SC_CORES + lax.axis_index(SC_AXES[0])


def _dispatch(h_p, h_s, pos_t, n_rows):
    n_p, c = h_p.shape
    n_s = h_s.shape[0]
    win_p, win_s = n_p // SC_WINDOW, n_s // SC_WINDOW
    assert n_p % (SC_WINDOW * SC_WORKERS) == 0 and n_s % SC_WINDOW == 0 and win_s <= SC_WORKERS

    @functools.partial(
        pl.kernel, out_type=jax.ShapeDtypeStruct((n_rows, c), h_p.dtype), mesh=_sc_mesh(), name="dispatch",
        scratch_types=[pltpu.VMEM((TOP_K, SC_WINDOW), I32), pltpu.VMEM((SC_WINDOW, c), h_p.dtype),
                       pltpu.SemaphoreType.DMA])
    def run(hp_hbm, hs_hbm, pos_hbm, xs_hbm, idx_v, rows_v, sem):
        wid = _sc_worker()

        def scatter_window(h_hbm, row0, tok0):
            pltpu.sync_copy(h_hbm.at[pl.ds(row0, SC_WINDOW)], rows_v)
            pltpu.sync_copy(pos_hbm.at[:, pl.ds(tok0, SC_WINDOW)], idx_v)
            copies = [pltpu.async_copy(rows_v, xs_hbm.at[idx_v.at[k]], sem) for k in range(TOP_K)]
            for cp in copies:
                cp.wait()

        def prompt_window(j, carry):
            row0 = pl.multiple_of((wid * (win_p // SC_WORKERS) + j) * SC_WINDOW, SC_WINDOW)
            scatter_window(hp_hbm, row0, row0)
            return carry
        lax.fori_loop(0, win_p // SC_WORKERS, prompt_window, 0)

        @pl.when(wid < win_s)
        def _():
            row0 = pl.multiple_of(wid * SC_WINDOW, SC_WINDOW)
            scatter_window(hs_hbm, row0, n_p + row0)

    return run(h_p, h_s, pos_t)


def _combine_gather(ys, pos_t):
    k_n = pos_t.shape[0] * pos_t.shape[1]
    c = ys.shape[1]
    win = SC_WINDOW // 2
    per_worker = k_n // (win * SC_WORKERS)
    assert k_n % (win * SC_WORKERS) == 0 and per_worker % 2 == 0
    dma = pltpu.SemaphoreType.DMA

    @functools.partial(
        pl.kernel, out_type=jax.ShapeDtypeStruct((k_n, c), ys.dtype), mesh=_sc_mesh(), name="combine_gather",
        scratch_types=[pltpu.VMEM((win,), I32), pltpu.VMEM((win,), I32),
                       pltpu.VMEM((win, c), ys.dtype), pltpu.VMEM((win, c), ys.dtype), dma, dma, dma, dma])
    def run(ys_hbm, idx_hbm, out_hbm, idx0, idx1, buf0, buf1, gsem0, gsem1, ssem0, ssem1):
        wid = _sc_worker()

        def rows(j):
            return pl.ds(pl.multiple_of((wid * per_worker + j) * win, win), win)

        def gather(j, idx_v, buf, sem):
            pltpu.sync_copy(idx_hbm.at[rows(j)], idx_v)
            pltpu.async_copy(ys_hbm.at[idx_v], buf, sem)

        def gather_wait(idx_v, buf, sem):
            pltpu.make_async_copy(ys_hbm.at[idx_v], buf, sem).wait()

        def store(j, buf, sem):
            pltpu.async_copy(buf, out_hbm.at[rows(j)], sem)

        def store_wait(buf, sem):
            pltpu.make_async_copy(buf, out_hbm.at[rows(0)], sem).wait()

        gather(0, idx0, buf0, gsem0)

        def two_windows(i, carry):
            j = 2 * i

            @pl.when(i > 0)
            def _():
                store_wait(buf1, ssem1)

            gather(j + 1, idx1, buf1, gsem1)
            gather_wait(idx0, buf0, gsem0)
            store(j, buf0, ssem0)

            @pl.when(j + 2 < per_worker)
            def _():
                store_wait(buf0, ssem0)
                gather(j + 2, idx0, buf0, gsem0)

            gather_wait(idx1, buf1, gsem1)
            store(j + 1, buf1, ssem1)
            return carry
        lax.fori_loop(0, per_worker // 2, two_windows, 0)
        store_wait(buf0, ssem0)
        store_wait(buf1, ssem1)

    return run(ys, pos_t.reshape(k_n))


MOE_RING = 24
MOE_WEIGHT_RING = 12


def _moe_kernel(st_ref, wg_ref, wu_ref, wd_ref, xs_ref, ys_ref, xbuf, ybuf, wgf, wuf, wdf, wgb, wub, wdb,
                semx, semy, semw):
    e = pl.program_id(0)
    n_e = pl.num_programs(0)

    def w_copies(ex):
        slot = ex % MOE_WEIGHT_RING
        return [pltpu.make_async_copy(src.at[ex], dst.at[slot], semw.at[slot, i])
                for i, (src, dst) in enumerate(((wg_ref, wgf), (wu_ref, wuf), (wd_ref, wdf)))]

    @pl.when(e == 0)
    def _():
        for ex in range(MOE_WEIGHT_RING - 1):
            @pl.when(ex < n_e)
            def _(ex=ex):
                for cp in w_copies(ex):
                    cp.start()

    @pl.when(e + (MOE_WEIGHT_RING - 1) < n_e)
    def _():
        for cp in w_copies(e + (MOE_WEIGHT_RING - 1)):
            cp.start()
    g0 = st_ref[e] // EXPERT_BLOCK
    nchunks = st_ref[e + 1] // EXPERT_BLOCK - g0
    ntot = st_ref[n_e] // EXPERT_BLOCK

    def rows(g):
        return pl.ds(pl.multiple_of(g * EXPERT_BLOCK, EXPERT_BLOCK), EXPERT_BLOCK)

    def x_copy(g):
        slot = g % MOE_RING
        return pltpu.make_async_copy(xs_ref.at[rows(g)], xbuf.at[slot], semx.at[slot])

    def y_copy(g):
        slot = g % MOE_RING
        return pltpu.make_async_copy(ybuf.at[slot], ys_ref.at[rows(g)], semy.at[slot])

    @pl.when(e == 0)
    def _():
        for g in range(MOE_RING - 1):
            @pl.when(g < ntot)
            def _(g=g):
                x_copy(g).start()

    for cp in w_copies(e):
        cp.wait()
    wslot = e % MOE_WEIGHT_RING
    wgb[...] = wgf[wslot].astype(BF16)
    wub[...] = wuf[wslot].astype(BF16)
    wdb[...] = wdf[wslot].astype(BF16)

    def take(g):
        @pl.when(g + (MOE_RING - 1) < ntot)
        def _():
            x_copy(g + (MOE_RING - 1)).start()

        x_copy(g).wait()

        @pl.when(g >= MOE_RING)
        def _():
            y_copy(g - MOE_RING).wait()

        return xbuf[g % MOE_RING]

    def expert_mlp(x_words):
        x = _unpack_rows(x_words).astype(BF16)
        mid = _silu(_dot(x, wgb[...])) * _dot(x, wub[...])
        return _pack_rows(_dot(mid.astype(BF16), wdb[...]))

    def put(g, y_words):
        ybuf[g % MOE_RING] = y_words
        y_copy(g).start()

    def run_chunks(g, m):
        xs = [take(g + i) for i in range(m)]
        y = expert_mlp(xs[0] if m == 1 else jnp.concatenate(xs, axis=0))
        for i in range(m):
            put(g + i, y[i * EXPERT_BLOCK:(i + 1) * EXPERT_BLOCK])

    def quad(p, carry):
        run_chunks(g0 + 4 * p, 4)
        return carry

    rest = nchunks % 4
    five = (rest == 1) & (nchunks >= 5)
    n_quads = nchunks // 4 - five.astype(I32)
    lax.fori_loop(0, n_quads, quad, 0)
    g_rest = g0 + 4 * n_quads

    @pl.when(five)
    def _():
        run_chunks(g_rest, 5)

    @pl.when(rest >= 2)
    def _():
        run_chunks(g_rest, 2)

    @pl.when((rest == 3) | (nchunks == 1))
    def _():
        run_chunks(g0 + nchunks - 1, 1)

    @pl.when(e == n_e - 1)
    def _():
        for j in range(MOE_RING):
            g = ntot - MOE_RING + j

            @pl.when(g >= 0)
            def _(g=g):
                y_copy(g).wait()


def _moe(starts, xs, wg, wu, wd):
    n_rows, c = xs.shape
    n_exp, d, f = wg.shape
    grid_spec = pltpu.PrefetchScalarGridSpec(
        num_scalar_prefetch=1, grid=(n_exp,),
        in_specs=[pl.BlockSpec(memory_space=pl.ANY)] * 4,
        out_specs=pl.BlockSpec(memory_space=pl.ANY),
        scratch_shapes=[pltpu.VMEM((MOE_RING, EXPERT_BLOCK, c), I32), pltpu.VMEM((MOE_RING, EXPERT_BLOCK, c), I32),
                        pltpu.VMEM((MOE_WEIGHT_RING, d, f), F32), pltpu.VMEM((MOE_WEIGHT_RING, d, f), F32),
                        pltpu.VMEM((MOE_WEIGHT_RING, f, d), F32),
                        pltpu.VMEM((d, f), BF16), pltpu.VMEM((d, f), BF16), pltpu.VMEM((f, d), BF16),
                        pltpu.SemaphoreType.DMA((MOE_RING,)), pltpu.SemaphoreType.DMA((MOE_RING,)),
                        pltpu.SemaphoreType.DMA((MOE_WEIGHT_RING, 3))])
    return pl.pallas_call(
        _moe_kernel, grid_spec=grid_spec, out_shape=jax.ShapeDtypeStruct((n_rows, c), I32),
        compiler_params=_cp(("arbitrary",)), name="moe",
    )(starts, wg, wu, wd, xs)


def _final_kernel(yg_ref, w8_ref, x1_ref, h2_ref, mod_ref, wsg_ref, wsu_ref, wsd_ref, nf_ref, y_ref, *, d):
    hb = _unpack_rows(h2_ref[...]).astype(BF16)
    shared = _dot((_silu(_dot(hb, wsg_ref[...])) * _dot(hb, wsu_ref[...])).astype(BF16), wsd_ref[...])
    w8 = w8_ref[...]
    routed = _unpack_rows(yg_ref[0]) * w8[:, 0:1]
    for k in range(1, TOP_K):
        routed = routed + _unpack_rows(yg_ref[k]) * w8[:, k:k + 1]
    g2 = mod_ref[...][:, (N_MOD - 1) * d:N_MOD * d]
    x2 = x1_ref[...] + g2 * (routed + shared)
    y_ref[...] = _rms(x2, nf_ref[...])


def _final(yg3, w8, x1, h2, mod, mod_spec, wts, nf, tf, n_tiles, tile0, out_shape, out_spec):
    d = x1.shape[1]
    rows = lambda i: (i, 0)
    in_specs = [pl.BlockSpec((TOP_K, tf, d // 2), lambda i: (0, tile0 + i, 0)),
                pl.BlockSpec((tf, LANE), rows),
                pl.BlockSpec((tf, d), rows), pl.BlockSpec((tf, d // 2), rows), mod_spec]
    in_specs += [_const_spec(w.shape) for w in wts]
    in_specs += [_const_spec(nf.shape)]
    return pl.pallas_call(
        functools.partial(_final_kernel, d=d), grid=(n_tiles,), in_specs=in_specs,
        out_specs=out_spec, out_shape=out_shape,
        compiler_params=_cp(("arbitrary",)), name="final",
    )(yg3, w8, x1, h2, mod, *wts, nf)


def kernel(x_prompt, x_sample, c_prompt, c_sample, state_pool, state_gla, w_ada, b_ada, norm1, w_in,
           pool_w, pool_scale, gla_w_alpha, gla_b_alpha, gla_norm, w_branch_a, w_branch_b, w_out, norm2,
           w_router, router_bias, w_exp_gate, w_exp_up, w_exp_down, w_sh_gate, w_sh_up, w_sh_down, norm_f):
    depth = w_in.shape[0]
    assert depth == 1, "single-layer trunk"
    bp, seq, d = x_prompt.shape
    bs, t_s, _ = x_sample.shape
    _, _, past, pw = state_pool.shape
    _, _, heads, dk, dv = state_gla.shape
    rank = gla_w_alpha.shape[1]
    n_exp = w_router.shape[2]
    assert past == POOL_HIST - 1 and pw == len(POOL_WINDOWS) * LANE and dv == LANE and rank <= LANE
    dm = _Dims(d, pw, heads, dk, dv)
    n_p, n_s = bp * seq, bs * t_s
    n_all = n_p + n_s

    off = [0]
    for sz in (pw, dm.hk, dm.hk, dm.gw, dm.gw, rank, d, d):
        off.append(off[-1] + sz)
    wi = w_in[0]
    win = (wi[:, :off[5]].astype(BF16), wi[:, off[6]:].astype(BF16),
           jnp.concatenate([wi[:, off[5]:off[6]], jnp.zeros((d, LANE - rank), F32)], axis=1).astype(BF16))
    assert off[5] + 2 * d + LANE == dm.zw
    wal = jnp.concatenate([gla_w_alpha[0], jnp.zeros((LANE - rank, dm.hk), F32)], axis=0).astype(BF16)
    bal = gla_b_alpha[0].reshape(1, dm.hk)
    wr = w_router[0]
    wrh = wr.astype(BF16)
    wrl = (wr - wrh.astype(F32)).astype(BF16)
    pwg = pool_w[0]
    zero = jnp.zeros_like(pwg[0])
    pw_pairs = jnp.stack([jnp.block([[pwg[g], zero], [zero, pwg[g + 1]]]) for g in range(0, len(POOL_WINDOWS), 2)])
    post_w = (pw_pairs.astype(BF16), pool_scale[0].reshape(1, pw), gla_norm[0].reshape(1, dm.gw),
              w_branch_a[0].astype(BF16), w_branch_b[0].astype(BF16), w_out[0].astype(BF16),
              norm2[0].reshape(1, d), wrh, wrl)
    n1 = norm1[0].reshape(1, d)

    mod = _ada(jnp.concatenate([c_prompt, c_sample], axis=0), w_ada[0], b_ada[0])
    mod_p = mod[:bp].reshape(bp, 1, N_MOD * d)
    mod_s = mod[bp:]

    rbias = router_bias[0].reshape(1, n_exp)
    x1_p, h2_p, idx_p, w8_p, cnt_p, npool_p, ngla_p = _mix_prompt(
        dm, x_prompt.reshape(n_p, d), mod_p, (n1,) + win + (wal, bal) + post_w + (rbias,), bp, seq, n_exp)

    xs2d = x_sample.reshape(bs, t_s * d)
    z_s = _sample_in(dm, xs2d, mod_s, n1, win, bs, t_s)
    mixed_s, o_s, npool_s, ngla_s = _sample_state(
        dm, z_s, state_pool[0].reshape(bs, past * pw), state_gla[0].reshape(bs, dm.hk, dv), wal, bal, bs, t_s)
    x1_s, h2_s, lg_s = _sample_post(dm, xs2d, mod_s, z_s, mixed_s, o_s, post_w, bs, t_s, n_exp)

    tr = 512 if (n_p % 512 == 0 and n_s % 512 == 0) else LANE
    assert n_p % tr == 0 and n_s % tr == 0 and n_p % bs == 0
    idx_s, w8_s, cnt_s = _route(lg_s, rbias, tr)
    nblk = (n_all * TOP_K) // EXPERT_BLOCK + n_exp
    pos_t, bounds = _plan(jnp.concatenate([idx_p, idx_s], axis=1), cnt_p + cnt_s, tr)
    starts = jnp.concatenate([bounds[0], bounds[1, n_exp - 1:]])

    xs = _dispatch(h2_p, h2_s, pos_t, nblk * EXPERT_BLOCK)
    ys = _moe(starts, xs, w_exp_gate[0], w_exp_up[0], w_exp_down[0])
    yg3 = _combine_gather(ys, pos_t).reshape(TOP_K, n_all, d // 2)

    sh_w = (w_sh_gate[0].astype(BF16), w_sh_up[0].astype(BF16), w_sh_down[0].astype(BF16))
    nf = norm_f.reshape(1, d)
    tf = 512 if seq % 512 == 0 else seq
    n_tp = seq // tf
    y_p = _final(
        yg3, w8_p, x1_p, h2_p, mod_p,
        pl.BlockSpec((None, 1, N_MOD * d), lambda i: (i // n_tp, 0, 0)), sh_w, nf, tf, n_p // tf, 0,
        jax.ShapeDtypeStruct((n_p, d), F32), pl.BlockSpec((tf, d), lambda i: (i, 0)))
    y_s = _final(
        yg3, w8_s, x1_s, h2_s, mod_s,
        _const_spec(mod_s.shape), sh_w, nf, bs, t_s, n_p // bs,
        jax.ShapeDtypeStruct((bs, t_s * d), F32), pl.BlockSpec((bs, d), lambda i: (0, i)))

    return (y_p.reshape(bp, seq, d), y_s.reshape(bs, t_s, d),
            npool_p.reshape(depth, bp, past, pw), ngla_p.reshape(depth, bp, heads, dk, dv),
            npool_s.reshape(depth, bs, past, pw), ngla_s.reshape(depth, bs, heads, dk, dv))
```

```python
import functools

import jax
import jax.numpy as jnp
from jax import lax
from jax.experimental import pallas as pl
from jax.experimental.pallas import tpu as pltpu
from jax.experimental.pallas import tpu_sc as plsc

F32 = jnp.float32
BF16 = jnp.bfloat16
I32 = jnp.int32

EPS = 1e-6
N_MOD = 6
POOL_WINDOWS = (2, 4, 8, 16)
POOL_HIST = 16
GLA_GATE_TEMP = 16.0
TOP_K = 8
ROUTED_SCALE = 2.5
PAST_LEN = 16384
EXPERT_BLOCK = 128
LANE = 128
GLA_FAST_MAX_DECAY = 40.0
PROJ_PIECE = 512
PROJ_LEAD = 2
VMEM_LIMIT = 56 * 1024 * 1024


def _cp(sem, vmem=VMEM_LIMIT):
    return pltpu.CompilerParams(dimension_semantics=sem, vmem_limit_bytes=vmem)


def _rms(x, g):
    return x * lax.rsqrt(jnp.mean(x * x, axis=-1, keepdims=True) + EPS) * g


def _silu(x):
    return x * jax.nn.sigmoid(x)


def _dot(a, b):
    return jnp.dot(a, b, preferred_element_type=F32)


def _split3(a):
    a0 = a.astype(BF16)
    r1 = a - a0.astype(F32)
    a1 = r1.astype(BF16)
    a2 = (r1 - a1.astype(F32)).astype(BF16)
    return a0, a1, a2


def _mod_parts(mod, d):
    return [mod[:, i * d:(i + 1) * d] for i in range(N_MOD)]


_HI16 = -65536


def _pack_rows(v):
    c = v.shape[1] // 2
    lo = lax.bitcast_convert_type(v[:, :c].astype(BF16).astype(F32), I32)
    hi = lax.bitcast_convert_type(v[:, c:].astype(BF16).astype(F32), I32)
    return (hi & _HI16) | lax.shift_right_logical(lo, 16)


def _unpack_rows(w):
    lo = lax.bitcast_convert_type(lax.shift_left(w, 16), F32)
    hi = lax.bitcast_convert_type(w & _HI16, F32)
    return jnp.concatenate([lo, hi], axis=1)


def _ada_kernel(c_ref, w_ref, b_ref, o_ref):
    a = _silu(c_ref[...]).astype(BF16)
    o_ref[...] = _dot(a, w_ref[...].astype(BF16)) + b_ref[...]


def _ada(c_all, w_ada, b_ada):
    n, d = c_all.shape
    cols = w_ada.shape[1]
    bc = d
    return pl.pallas_call(
        _ada_kernel,
        grid=(cols // bc,),
        in_specs=[pl.BlockSpec((n, d), lambda i: (0, 0)),
                  pl.BlockSpec((d, bc), lambda i: (0, i)),
                  pl.BlockSpec((1, bc), lambda i: (0, i))],
        out_specs=pl.BlockSpec((n, bc), lambda i: (0, i)),
        out_shape=jax.ShapeDtypeStruct((n, cols), F32),
        compiler_params=_cp(("arbitrary",)),
        name="ada",
    )(c_all, w_ada, b_ada.reshape(1, cols))


class _Dims:
    def __init__(self, d, pw, heads, dk, dv):
        self.d, self.pw, self.h, self.dk, self.dv = d, pw, heads, dk, dv
        self.hk, self.gw = heads * dk, heads * dv
        self.q0 = pw
        self.k0 = self.q0 + self.hk
        self.v0 = self.k0 + self.hk
        self.go0 = self.v0 + self.gw
        self.ga0 = self.go0 + self.gw
        self.gb0 = self.ga0 + d
        self.al0 = self.gb0 + d
        self.zw = self.al0 + LANE


def _log_decay(z_al, wal_ref, bal_ref):
    xal = _dot(z_al.astype(BF16), wal_ref[...]) + bal_ref[...]
    return jax.nn.log_sigmoid(xal) / GLA_GATE_TEMP


def _pool_project(mixed_groups, pw_ref, ps_ref):
    ys = [_dot(jnp.concatenate(mixed_groups[2 * p:2 * p + 2], axis=1).astype(BF16), pw_ref[p])
          for p in range(len(mixed_groups) // 2)]
    return jnp.concatenate(ys, axis=1) * ps_ref[...]


def _post_mix(dm, x, mod, ya, o, go, ga, gb, gn_ref, wa_ref, wb_ref, wo_ref, n2_ref, wrh_ref, wrl_ref,
              filler=lambda: None):
    _, _, g1, sh2, sc2, _ = _mod_parts(mod, dm.d)
    parts = []
    for h in range(dm.h):
        oh = o[:, h * dm.dv:(h + 1) * dm.dv]
        parts.append(oh * lax.rsqrt(jnp.mean(oh * oh, axis=-1, keepdims=True) + EPS))
    yb = jnp.concatenate(parts, axis=1) * gn_ref[...] * _silu(go)
    filler()
    m = (jax.nn.sigmoid(ga) * _dot(ya.astype(BF16), wa_ref[...])
         + jax.nn.sigmoid(gb) * _dot(yb.astype(BF16), wb_ref[...]))
    filler()
    x1 = x + g1 * _dot(m.astype(BF16), wo_ref[...])
    h2 = _rms(x1, n2_ref[...]) * (1 + sc2) + sh2
    hi = h2.astype(BF16)
    lo = (h2 - hi.astype(F32)).astype(BF16)
    filler()
    logits = _dot(hi, wrh_ref[...]) + (_dot(hi, wrl_ref[...]) + _dot(lo, wrh_ref[...]))
    filler()
    return x1, _pack_rows(h2), logits


def _topk_rows(logits, bias):
    tr, n_exp = logits.shape
    s = jax.nn.sigmoid(logits)
    cur = s + bias
    lane = lax.broadcasted_iota(I32, (1, n_exp), 1).astype(F32)
    slot = lax.broadcasted_iota(I32, (1, LANE), 1)
    idx8 = jnp.zeros((tr, LANE), F32)
    w8 = jnp.zeros((tr, LANE), F32)
    wsum = jnp.zeros((tr, 1), F32)
    cur0 = cur
    for k in range(TOP_K):
        m = jnp.max(cur, axis=-1, keepdims=True)
        am = jnp.min(jnp.where(cur == m, lane, float(n_exp)), axis=-1, keepdims=True)
        oh = lane == am
        sk = jnp.sum(jnp.where(oh, s, 0.0), axis=-1, keepdims=True)
        cur = jnp.where(oh, -jnp.inf, cur)
        idx8 = jnp.where(slot == k, am, idx8)
        w8 = jnp.where(slot == k, sk, w8)
        wsum = wsum + sk
    picked = jnp.where((cur == -jnp.inf) & (cur0 > -jnp.inf), 1.0, 0.0)
    return idx8.T[0:TOP_K, :].astype(I32), w8 / wsum * ROUTED_SCALE, picked.astype(BF16)


def _count_members(picked):
    return _dot(jnp.ones((8, picked.shape[0]), BF16), picked)


def _in_project(dm, x, mod, n1_ref, win_refs, z_ref):
    h = (_rms(x, n1_ref[...]) * (1 + mod[:, dm.d:2 * dm.d]) + mod[:, 0:dm.d]).astype(BF16)
    starts = [0]
    for w_ref in win_refs:
        starts.append(starts[-1] + w_ref.shape[1])
    for i in (0, 2, 1):
        z_ref[:, starts[i]:starts[i + 1]] = _dot(h, win_refs[i][...])


def _mixp_kernel(x_ref, mod_ref, n1_ref, wia_ref, wig_ref, wil_ref, wal_ref, bal_ref, pw_ref, ps_ref, gn_ref,
                 wa_ref, wb_ref, wo_ref, n2_ref, wrh_ref, wrl_ref, rb_ref,
                 x1_ref, h2_ref, idx_ref, w8_ref, cnt_ref, np_ref, ng_ref,
                 za_ref, zb_ref, xk_ref, mk_ref, hn_ref, lg_ref, uext_ref, s_ref, b_ref, oi_ref,
                 *, dm, tt, n_t, tiles):
    n = pl.program_id(0)
    j = jnp.clip(n - 1, 0, tiles - 1) % n_t

    @pl.when(n == 0)
    def _():
        za_ref[...] = jnp.zeros(za_ref.shape, F32)
        zb_ref[...] = jnp.zeros(zb_ref.shape, F32)
        xk_ref[...] = jnp.zeros(xk_ref.shape, F32)
        mk_ref[...] = jnp.zeros(mk_ref.shape, F32)
        lg_ref[...] = jnp.zeros(lg_ref.shape, F32)
        cnt_ref[...] = jnp.zeros(cnt_ref.shape, F32)

    def route_previous():
        idx_ref[...], w8_ref[...], picked = _topk_rows(lg_ref[...], rb_ref[...])
        return picked

    def count_routed(picked):
        cnt_ref[...] += jnp.where(n >= 2, _count_members(picked), 0.0)

    @pl.when((j == 0) & (n <= tiles))
    def _():
        uext_ref[0:POOL_HIST, :] = jnp.zeros((POOL_HIST, dm.pw), F32)
        s_ref[...] = jnp.zeros(s_ref.shape, F32)

    @pl.when((n % 2 == 0) & (n <= tiles))
    def _():
        _mixp_region(za_ref, zb_ref, x_ref, mod_ref, n1_ref, wia_ref, wig_ref, wil_ref, wal_ref, bal_ref, pw_ref,
                     ps_ref, gn_ref, wa_ref, wb_ref, wo_ref, n2_ref, wrh_ref, wrl_ref, x1_ref, h2_ref, lg_ref,
                     xk_ref, mk_ref, hn_ref, uext_ref, s_ref, b_ref, oi_ref, dm=dm, tt=tt, j=j,
                     route_previous=route_previous, count_routed=count_routed)

    @pl.when((n % 2 == 1) & (n <= tiles))
    def _():
        _mixp_region(zb_ref, za_ref, x_ref, mod_ref, n1_ref, wia_ref, wig_ref, wil_ref, wal_ref, bal_ref, pw_ref,
                     ps_ref, gn_ref, wa_ref, wb_ref, wo_ref, n2_ref, wrh_ref, wrl_ref, x1_ref, h2_ref, lg_ref,
                     xk_ref, mk_ref, hn_ref, uext_ref, s_ref, b_ref, oi_ref, dm=dm, tt=tt, j=j,
                     route_previous=route_previous, count_routed=count_routed)

    @pl.when(n > tiles)
    def _():
        count_routed(route_previous())

    xk_ref[...] = x_ref[...]
    mk_ref[...] = mod_ref[...]

    @pl.when((j == n_t - 1) & (n <= tiles))
    def _():
        np_ref[...] = uext_ref[tt + 1:tt + POOL_HIST, :]
        ng_ref[...] = s_ref[...]


def _mixp_region(zp_ref, z_ref, x_ref, mod_ref, n1_ref, wia_ref, wig_ref, wil_ref, wal_ref, bal_ref, pw_ref,
                 ps_ref, gn_ref, wa_ref, wb_ref, wo_ref, n2_ref, wrh_ref, wrl_ref, x1_ref, h2_ref, lg_ref,
                 xk_ref, mk_ref, hn_ref, uext_ref, s_ref, b_ref, oi_ref, *, dm, tt, j, route_previous, count_routed):
    picked = route_previous()
    scale = dm.dk ** -0.5
    lane = lax.broadcasted_iota(I32, (1, dm.hk), 1)
    head_masks = [(lane >= h * dm.dk) & (lane < (h + 1) * dm.dk) for h in range(dm.h)]
    causal = lax.broadcasted_iota(I32, (tt, tt), 0) >= lax.broadcasted_iota(I32, (tt, tt), 1)

    def pool_branch():
        pos = j * tt + lax.broadcasted_iota(I32, (tt, 1), 0)
        gw = dm.pw // len(POOL_WINDOWS)
        mixed = []
        for g, w in enumerate(POOL_WINDOWS):
            c0 = g * gw
            cur = uext_ref[POOL_HIST:POOL_HIST + tt, c0:c0 + gw]
            acc = cur
            for i in range(1, w):
                acc = acc + uext_ref[POOL_HIST - i:POOL_HIST - i + tt, c0:c0 + gw]
            cnt = jnp.minimum(w, pos + 1).astype(F32)
            mixed.append(acc / cnt - cur)
        return _pool_project(mixed, pw_ref, ps_ref)

    def scaled_queries(bc):
        return (z_ref[:, dm.q0:dm.q0 + dm.hk] * scale) * jnp.exp(bc)

    def inter_chunk(qe):
        s_rows = lax.broadcasted_iota(I32, (dm.hk, dm.gw), 0) // dm.dk
        s_cols = lax.broadcasted_iota(I32, (dm.hk, dm.gw), 1) // dm.dv
        s_all = jnp.where(s_rows == s_cols, jnp.concatenate([s_ref[...]] * dm.h, axis=1), 0.0)
        return _dot(qe.astype(BF16), s_all.astype(BF16))

    def finish(ya, o, filler=lambda: None):
        x1, h2, logits = _post_mix(
            dm, xk_ref[...], mk_ref[...], ya, o, z_ref[:, dm.go0:dm.go0 + dm.gw],
            z_ref[:, dm.ga0:dm.ga0 + dm.d], z_ref[:, dm.gb0:dm.gb0 + dm.d],
            gn_ref, wa_ref, wb_ref, wo_ref, n2_ref, wrh_ref, wrl_ref, filler)
        x1_ref[...] = x1
        h2_ref[...] = h2
        lg_ref[...] = logits

    mod_in = mod_ref[...]
    hn_ref[...] = (_rms(x_ref[...], n1_ref[...]) * (1 + mod_in[:, dm.d:2 * dm.d]) + mod_in[:, 0:dm.d]).astype(BF16)
    pieces = iter([(w_ref, a, min(a + PROJ_PIECE, w_ref.shape[1]), c0)
                   for w_ref, c0 in ((wia_ref, 0), (wil_ref, dm.al0), (wig_ref, dm.ga0))
                   for a in range(0, w_ref.shape[1], PROJ_PIECE)])

    def project_piece():
        piece = next(pieces, None)
        if piece is not None:
            w_ref, a, b, c0 = piece
            zp_ref[:, c0 + a:c0 + b] = _dot(hn_ref[...], w_ref[:, a:b])
        return piece is not None

    for _ in range(PROJ_LEAD):
        project_piece()
    uext_ref[POOL_HIST:POOL_HIST + tt, :] = z_ref[:, 0:dm.pw]
    ya = pool_branch()
    project_piece()
    la = _log_decay(z_ref[:, dm.al0:dm.al0 + LANE], wal_ref, bal_ref)
    tri = jnp.where(causal, 1.0, 0.0).astype(BF16)
    a0, a1, a2 = _split3(la)
    bc = _dot(tri, a0) + (_dot(tri, a1) + _dot(tri, a2))
    b_ref[...] = bc
    project_piece()
    qe = scaled_queries(bc)
    k = z_ref[:, dm.k0:dm.k0 + dm.hk]
    vb = z_ref[:, dm.v0:dm.v0 + dm.gw].astype(BF16)
    blast = bc[tt - 1:tt, :]
    fast = jnp.max(-blast) <= GLA_FAST_MAX_DECAY
    ke = (k * jnp.exp(-bc)).astype(BF16)
    intra = []
    for h in range(dm.h):
        qh = jnp.where(head_masks[h], qe, 0.0).astype(BF16)
        sc = lax.dot_general(qh, ke, (((1,), (1,)), ((), ())), preferred_element_type=F32)
        sc = jnp.where(causal, sc, 0.0).astype(BF16)
        intra.append(_dot(sc, vb[:, h * dm.dv:(h + 1) * dm.dv]))
    count_routed(picked)
    project_piece()
    finish(ya, inter_chunk(qe) + jnp.concatenate(intra, axis=1), project_piece)
    while project_piece():
        pass

    @pl.when(jnp.logical_not(fast))
    def _():
        rows = lax.broadcasted_iota(I32, (tt, 1), 0)
        for h in range(dm.h):
            def body(tb, carry, h=h):
                base = pl.multiple_of(tb * 8, 8)
                q8 = z_ref[pl.ds(base, 8), dm.q0:dm.q0 + dm.hk] * scale
                b8 = b_ref[pl.ds(base, 8), :]
                vh = z_ref[:, dm.v0 + h * dm.dv:dm.v0 + (h + 1) * dm.dv]
                out_rows = []
                for r in range(8):
                    keep = (rows <= base + r) & head_masks[h]
                    dec = jnp.exp(jnp.where(keep, b8[r:r + 1, :] - b_ref[...], -jnp.inf))
                    wgt = (q8[r:r + 1, :] * dec) * z_ref[:, dm.k0:dm.k0 + dm.hk]
                    s_col = jnp.sum(wgt, axis=-1, keepdims=True)
                    out_rows.append(jnp.sum(s_col * vh, axis=0, keepdims=True))
                oi_ref[pl.ds(base, 8), h * dm.dv:(h + 1) * dm.dv] = jnp.concatenate(out_rows, axis=0)
                return carry
            lax.fori_loop(0, tt // 8, body, 0)
        finish(pool_branch(), inter_chunk(scaled_queries(b_ref[...])) + oi_ref[...])

    bc = b_ref[...]
    blast = bc[tt - 1:tt, :]
    kd = z_ref[:, dm.k0:dm.k0 + dm.hk] * jnp.exp(blast - bc)
    u_all = _dot(kd.T.astype(BF16), z_ref[:, dm.v0:dm.v0 + dm.gw].astype(BF16))
    dec_col = jnp.broadcast_to(jnp.exp(blast), (dm.hk, dm.hk)).T[:, 0:dm.dv]
    upd = jnp.concatenate(
        [u_all[h * dm.dk:(h + 1) * dm.dk, h * dm.dv:(h + 1) * dm.dv] for h in range(dm.h)], axis=0)
    s_ref[...] = dec_col * s_ref[...] + upd
    uext_ref[0:POOL_HIST, :] = uext_ref[tt:tt + POOL_HIST, :]


def _const_spec(shape):
    nd = len(shape)
    return pl.BlockSpec(shape, lambda *_: (0,) * nd, pipeline_mode=pl.Buffered(1))


def _mix_prompt(dm, x2d, mod3, wts, bsz, seq, n_exp):
    tt = 256 if seq % 256 == 0 else seq
    n_t = seq // tt
    n = bsz * seq
    d = dm.d
    tiles = bsz * n_t
    kern = functools.partial(_mixp_kernel, dm=dm, tt=tt, n_t=n_t, tiles=tiles)
    proj = lambda i: jnp.minimum(i, tiles - 1)
    fin = lambda i: jnp.clip(i - 1, 0, tiles - 1)
    routed = lambda i: jnp.clip(i - 2, 0, tiles - 1)
    row = lambda i: (fin(i), 0)
    in_specs = [pl.BlockSpec((tt, d), lambda i: (proj(i), 0)),
                pl.BlockSpec((None, 1, N_MOD * d), lambda i: (proj(i) // n_t, 0, 0))]
    in_specs += [_const_spec(w.shape) for w in wts]
    out_specs = [pl.BlockSpec((tt, d), row), pl.BlockSpec((tt, d // 2), row),
                 pl.BlockSpec((TOP_K, tt), lambda i: (0, routed(i))), pl.BlockSpec((tt, LANE), lambda i: (routed(i), 0)),
                 pl.BlockSpec((8, n_exp), lambda i: (0, 0)),
                 pl.BlockSpec((None, POOL_HIST - 1, dm.pw), lambda i: (fin(i) // n_t, 0, 0)),
                 pl.BlockSpec((None, dm.hk, dm.dv), lambda i: (fin(i) // n_t, 0, 0))]
    out_shape = [jax.ShapeDtypeStruct((n, d), F32), jax.ShapeDtypeStruct((n, d // 2), I32),
                 jax.ShapeDtypeStruct((TOP_K, n), I32), jax.ShapeDtypeStruct((n, LANE), F32),
                 jax.ShapeDtypeStruct((8, n_exp), F32),
                 jax.ShapeDtypeStruct((bsz, POOL_HIST - 1, dm.pw), F32),
                 jax.ShapeDtypeStruct((bsz, dm.hk, dm.dv), F32)]
    scratch = [pltpu.VMEM((tt, dm.zw), F32), pltpu.VMEM((tt, dm.zw), F32),
               pltpu.VMEM((tt, d), F32), pltpu.VMEM((1, N_MOD * d), F32), pltpu.VMEM((tt, d), BF16),
               pltpu.VMEM((tt, n_exp), F32), pltpu.VMEM((tt + POOL_HIST, dm.pw), F32),
               pltpu.VMEM((dm.hk, dm.dv), F32), pltpu.VMEM((tt, dm.hk), F32), pltpu.VMEM((tt, dm.gw), F32)]
    return pl.pallas_call(
        kern, grid=(tiles + 2,), in_specs=in_specs, out_specs=out_specs, out_shape=out_shape,
        scratch_shapes=scratch, compiler_params=_cp(("arbitrary",)), name="mixp",
    )(x2d, mod3, *wts)


def _sin_kernel(x_ref, mod_ref, n1_ref, wia_ref, wig_ref, wil_ref, z_ref, *, dm):
    _in_project(dm, x_ref[...], mod_ref[...], n1_ref, (wia_ref, wig_ref, wil_ref), z_ref)


def _sample_in(dm, xs2d, mod_s, n1, win, bs, t_s):
    d = dm.d
    return pl.pallas_call(
        functools.partial(_sin_kernel, dm=dm),
        grid=(t_s,),
        in_specs=[pl.BlockSpec((bs, d), lambda t: (0, t)), _const_spec(mod_s.shape), _const_spec(n1.shape)]
        + [_const_spec(w.shape) for w in win],
        out_specs=pl.BlockSpec((bs, dm.zw), lambda t: (t, 0)),
        out_shape=jax.ShapeDtypeStruct((t_s * bs, dm.zw), F32),
        compiler_params=_cp(("arbitrary",)), name="sample_in",
    )(xs2d, mod_s, n1, *win)


def _sst_kernel(z_ref, pool_ref, gs_ref, wal_ref, bal_ref,
                mixed_ref, o_ref, npool_ref, ngs_ref,
                qt_ref, kt_ref, dt_ref, vt_ref, ot_ref, *, dm, bs, t_s, ch, pos0):
    i = pl.program_id(0)
    n_i = pl.num_programs(0)
    scale = dm.dk ** -0.5
    past = POOL_HIST - 1

    @pl.when(i == 0)
    def _():
        for t in range(t_s):
            r0 = t * bs
            la = _log_decay(z_ref[r0:r0 + bs, dm.al0:dm.al0 + LANE], wal_ref, bal_ref)
            dt_ref[t] = jnp.exp(la).T
            qt_ref[t] = (z_ref[r0:r0 + bs, dm.q0:dm.q0 + dm.hk] * scale).T
            kt_ref[t] = z_ref[r0:r0 + bs, dm.k0:dm.k0 + dm.hk].T
            vt_ref[t] = z_ref[r0:r0 + bs, dm.v0:dm.v0 + dm.gw].T
        ot_ref[...] = jnp.zeros(ot_ref.shape, F32)

        def item(s, c0, c1):
            if s >= 0:
                return z_ref[s * bs:(s + 1) * bs, c0:c1]
            return pool_ref[:, (past + s) * dm.pw + c0:(past + s) * dm.pw + c1]

        gw = dm.pw // len(POOL_WINDOWS)
        for t in range(t_s):
            for g, w in enumerate(POOL_WINDOWS):
                c0, c1 = g * gw, (g + 1) * gw
                cur = item(t, c0, c1)
                acc = cur
                for r in range(1, w):
                    acc = acc + item(t - r, c0, c1)
                cnt = float(min(w, pos0 + t + 1))
                mixed_ref[t * bs:(t + 1) * bs, c0:c1] = acc / cnt - cur
        for r in range(past):
            npool_ref[:, r * dm.pw:(r + 1) * dm.pw] = item(r - past + t_s, 0, dm.pw)

    per_head = dm.dk // ch
    hoff = pl.multiple_of((i // per_head) * dm.dv, dm.dv)
    for p in range(ch):
        hd = i * ch + p
        st = gs_ref[:, p, :].T
        for t in range(t_s):
            st = dt_ref[t, pl.ds(hd, 1), :] * st + kt_ref[t, pl.ds(hd, 1), :] * vt_ref[t, pl.ds(hoff, dm.dv), :]
            ot_ref[t, pl.ds(hoff, dm.dv), :] += qt_ref[t, pl.ds(hd, 1), :] * st
        ngs_ref[:, p, :] = st.T

    @pl.when(i == n_i - 1)
    def _():
        for t in range(t_s):
            for h in range(dm.h):
                o_ref[t * bs:(t + 1) * bs, h * dm.dv:(h + 1) * dm.dv] = ot_ref[t, h * dm.dv:(h + 1) * dm.dv, :].T


def _sample_state(dm, z_s, pool2d, gs3, wal, bal, bs, t_s):
    ch = 32
    n_i = dm.hk // ch
    past = POOL_HIST - 1
    kern = functools.partial(_sst_kernel, dm=dm, bs=bs, t_s=t_s, ch=ch, pos0=PAST_LEN)
    full = lambda shape: pl.BlockSpec(shape, lambda i: (0,) * len(shape))
    return pl.pallas_call(
        kern, grid=(n_i,),
        in_specs=[_const_spec(z_s.shape), _const_spec(pool2d.shape),
                  pl.BlockSpec((bs, ch, dm.dv), lambda i: (0, i, 0)),
                  _const_spec(wal.shape), _const_spec(bal.shape)],
        out_specs=[full((t_s * bs, dm.pw)), full((t_s * bs, dm.gw)), full((bs, past * dm.pw)),
                   pl.BlockSpec((bs, ch, dm.dv), lambda i: (0, i, 0))],
        out_shape=[jax.ShapeDtypeStruct((t_s * bs, dm.pw), F32), jax.ShapeDtypeStruct((t_s * bs, dm.gw), F32),
                   jax.ShapeDtypeStruct((bs, past * dm.pw), F32), jax.ShapeDtypeStruct(gs3.shape, F32)],
        scratch_shapes=[pltpu.VMEM((t_s, dm.hk, bs), F32), pltpu.VMEM((t_s, dm.hk, bs), F32),
                        pltpu.VMEM((t_s, dm.hk, bs), F32), pltpu.VMEM((t_s, dm.gw, bs), F32),
                        pltpu.VMEM((t_s, dm.gw, bs), F32)],
        compiler_params=_cp(("arbitrary",)), name="sample_state",
    )(z_s, pool2d, gs3, wal, bal)


def _spost_kernel(x_ref, mod_ref, z_ref, mixed_ref, o_ref, pw_ref, ps_ref, gn_ref,
                  wa_ref, wb_ref, wo_ref, n2_ref, wrh_ref, wrl_ref,
                  x1_ref, h2_ref, lg_ref, *, dm):
    gw = dm.pw // len(POOL_WINDOWS)
    ya = _pool_project([mixed_ref[:, g * gw:(g + 1) * gw] for g in range(len(POOL_WINDOWS))], pw_ref, ps_ref)
    x1, h2, logits = _post_mix(
        dm, x_ref[...], mod_ref[...], ya, o_ref[...],
        z_ref[:, dm.go0:dm.go0 + dm.gw], z_ref[:, dm.ga0:dm.ga0 + dm.d], z_ref[:, dm.gb0:dm.gb0 + dm.d],
        gn_ref, wa_ref, wb_ref, wo_ref, n2_ref, wrh_ref, wrl_ref)
    x1_ref[...] = x1
    h2_ref[...] = h2
    lg_ref[...] = logits


def _sample_post(dm, xs2d, mod_s, z_s, mixed_s, o_s, wts, bs, t_s, n_exp):
    d = dm.d
    rows = lambda t: (t, 0)
    in_specs = [pl.BlockSpec((bs, d), lambda t: (0, t)), _const_spec(mod_s.shape),
                pl.BlockSpec((bs, dm.zw), rows), pl.BlockSpec((bs, dm.pw), rows), pl.BlockSpec((bs, dm.gw), rows)]
    in_specs += [_const_spec(w.shape) for w in wts]
    return pl.pallas_call(
        functools.partial(_spost_kernel, dm=dm), grid=(t_s,), in_specs=in_specs,
        out_specs=[pl.BlockSpec((bs, d), rows), pl.BlockSpec((bs, d // 2), rows), pl.BlockSpec((bs, n_exp), rows)],
        out_shape=[jax.ShapeDtypeStruct((t_s * bs, d), F32), jax.ShapeDtypeStruct((t_s * bs, d // 2), I32),
                   jax.ShapeDtypeStruct((t_s * bs, n_exp), F32)],
        compiler_params=_cp(("arbitrary",)), name="sample_post",
    )(xs2d, mod_s, z_s, mixed_s, o_s, *wts)


def _route_kernel(lg_ref, bias_ref, idx_ref, w_ref, cnt_ref):
    i = pl.program_id(0)
    idx_ref[...], w_ref[...], picked = _topk_rows(lg_ref[...], bias_ref[...])

    @pl.when(i == 0)
    def _():
        cnt_ref[...] = jnp.zeros(cnt_ref.shape, F32)

    cnt_ref[...] += _count_members(picked)


def _route(logits, bias, tr):
    n, n_exp = logits.shape
    rows = lambda i: (i, 0)
    return pl.pallas_call(
        _route_kernel, grid=(n // tr,),
        in_specs=[pl.BlockSpec((tr, n_exp), rows), _const_spec((1, n_exp))],
        out_specs=[pl.BlockSpec((TOP_K, tr), lambda i: (0, i)), pl.BlockSpec((tr, LANE), rows),
                   pl.BlockSpec((8, n_exp), lambda i: (0, 0))],
        out_shape=[jax.ShapeDtypeStruct((TOP_K, n), I32), jax.ShapeDtypeStruct((n, LANE), F32),
                   jax.ShapeDtypeStruct((8, n_exp), F32)],
        compiler_params=_cp(("arbitrary",)), name="route",
    )(logits, bias)


def _plan_kernel(idx_ref, cnt_ref, pos_ref, bounds_ref, base_ref, dest_ref, *, n_exp):
    i = pl.program_id(0)
    tr = idx_ref.shape[1]

    @pl.when(i == 0)
    def _():
        lane = lax.broadcasted_iota(I32, (1, n_exp), 1)
        cnt = cnt_ref[...].astype(I32)
        padded = ((cnt + (EXPERT_BLOCK - 1)) // EXPERT_BLOCK) * EXPERT_BLOCK
        pe = padded
        s = 1
        while s < n_exp:
            pe = pe + jnp.where(lane >= s, pltpu.roll(pe, s, 1), 0)
            s *= 2
        row = lax.broadcasted_iota(I32, (8, n_exp), 0)
        bounds_ref[...] = jnp.where(row == 0, pe - padded, pe)
        first = (pe - padded).astype(F32)[0:1, :]
        base_ref[...] = jnp.broadcast_to(first, (LANE, n_exp)).T

    eidx = lax.broadcasted_iota(I32, (n_exp, 1), 0)
    idx = idx_ref[...]
    member = eidx == idx[0:1, :]
    for k in range(1, TOP_K):
        member = member | (eidx == idx[k:k + 1, :])
    mb = jnp.where(member, 1.0, 0.0).astype(BF16)
    before = jnp.where(lax.broadcasted_iota(I32, (tr, tr), 0) < lax.broadcasted_iota(I32, (tr, tr), 1),
                       1.0, 0.0).astype(BF16)
    dest_ref[...] = _dot(mb, before) + base_ref[:, 0:1]
    for b in range(tr // LANE):
        cols = slice(b * LANE, (b + 1) * LANE)
        dest = dest_ref[:, cols]
        rows = [jnp.sum(jnp.where(eidx == idx[k:k + 1, cols], dest, 0.0), axis=0, keepdims=True)
                for k in range(TOP_K)]
        pos_ref[:, cols] = jnp.concatenate(rows, axis=0).astype(I32)
    base_ref[...] += jnp.sum(mb.astype(F32), axis=1, keepdims=True)


def _plan(idx_t, counts, tr):
    n = idx_t.shape[1]
    n_exp = counts.shape[1]
    cols = lambda i: (0, i)
    return pl.pallas_call(
        functools.partial(_plan_kernel, n_exp=n_exp), grid=(n // tr,),
        in_specs=[pl.BlockSpec((TOP_K, tr), cols), _const_spec((8, n_exp))],
        out_specs=[pl.BlockSpec((TOP_K, tr), cols), pl.BlockSpec((8, n_exp), lambda i: (0, 0))],
        out_shape=[jax.ShapeDtypeStruct((TOP_K, n), I32), jax.ShapeDtypeStruct((8, n_exp), I32)],
        scratch_shapes=[pltpu.VMEM((n_exp, LANE), F32), pltpu.VMEM((n_exp, tr), F32)],
        compiler_params=_cp(("arbitrary",)), name="plan",
    )(idx_t, counts)


SC_CORES = 2
SC_SUBCORES = 16
SC_AXES = ("core", "subcore")
SC_WORKERS = SC_CORES * SC_SUBCORES
SC_WINDOW = 128


def _sc_mesh():
    return plsc.VectorSubcoreMesh(core_axis_name=SC_AXES[0], subcore_axis_name=SC_AXES[1],
                                  num_cores=SC_CORES, num_subcores=SC_SUBCORES)


def _sc_worker():
    return lax.axis_index(SC_AXES[1]) * SC_CORES + lax.axis_index(SC_AXES[0])


def _dispatch(h_p, h_s, pos_t, n_rows):
    n_p, c = h_p.shape
    n_s = h_s.shape[0]
    win_p, win_s = n_p // SC_WINDOW, n_s // SC_WINDOW
    assert n_p % (SC_WINDOW * SC_WORKERS) == 0 and n_s % SC_WINDOW == 0 and win_s <= SC_WORKERS

    @functools.partial(
        pl.kernel, out_type=jax.ShapeDtypeStruct((n_rows, c), h_p.dtype), mesh=_sc_mesh(), name="dispatch",
        scratch_types=[pltpu.VMEM((TOP_K, SC_WINDOW), I32), pltpu.VMEM((SC_WINDOW, c), h_p.dtype),
                       pltpu.SemaphoreType.DMA])
    def run(hp_hbm, hs_hbm, pos_hbm, xs_hbm, idx_v, rows_v, sem):
        wid = _sc_worker()

        def scatter_window(h_hbm, row0, tok0):
            pltpu.sync_copy(h_hbm.at[pl.ds(row0, SC_WINDOW)], rows_v)
            pltpu.sync_copy(pos_hbm.at[:, pl.ds(tok0, SC_WINDOW)], idx_v)
            copies = [pltpu.async_copy(rows_v, xs_hbm.at[idx_v.at[k]], sem) for k in range(TOP_K)]
            for cp in copies:
                cp.wait()

        def prompt_window(j, carry):
            row0 = pl.multiple_of((wid * (win_p // SC_WORKERS) + j) * SC_WINDOW, SC_WINDOW)
            scatter_window(hp_hbm, row0, row0)
            return carry
        lax.fori_loop(0, win_p // SC_WORKERS, prompt_window, 0)

        @pl.when(wid < win_s)
        def _():
            row0 = pl.multiple_of(wid * SC_WINDOW, SC_WINDOW)
            scatter_window(hs_hbm, row0, n_p + row0)

    return run(h_p, h_s, pos_t)


def _combine_gather(ys, pos_t):
    k_n = pos_t.shape[0] * pos_t.shape[1]
    c = ys.shape[1]
    win = SC_WINDOW // 2
    per_worker = k_n // (win * SC_WORKERS)
    assert k_n % (win * SC_WORKERS) == 0 and per_worker % 2 == 0
    dma = pltpu.SemaphoreType.DMA

    @functools.partial(
        pl.kernel, out_type=jax.ShapeDtypeStruct((k_n, c), ys.dtype), mesh=_sc_mesh(), name="combine_gather",
        scratch_types=[pltpu.VMEM((win,), I32), pltpu.VMEM((win,), I32),
                       pltpu.VMEM((win, c), ys.dtype), pltpu.VMEM((win, c), ys.dtype), dma, dma, dma, dma])
    def run(ys_hbm, idx_hbm, out_hbm, idx0, idx1, buf0, buf1, gsem0, gsem1, ssem0, ssem1):
        wid = _sc_worker()

        def rows(j):
            return pl.ds(pl.multiple_of((wid * per_worker + j) * win, win), win)

        def gather(j, idx_v, buf, sem):
            pltpu.sync_copy(idx_hbm.at[rows(j)], idx_v)
            pltpu.async_copy(ys_hbm.at[idx_v], buf, sem)

        def gather_wait(idx_v, buf, sem):
            pltpu.make_async_copy(ys_hbm.at[idx_v], buf, sem).wait()

        def store(j, buf, sem):
            pltpu.async_copy(buf, out_hbm.at[rows(j)], sem)

        def store_wait(buf, sem):
            pltpu.make_async_copy(buf, out_hbm.at[rows(0)], sem).wait()

        gather(0, idx0, buf0, gsem0)

        def two_windows(i, carry):
            j = 2 * i

            @pl.when(i > 0)
            def _():
                store_wait(buf1, ssem1)

            gather(j + 1, idx1, buf1, gsem1)
            gather_wait(idx0, buf0, gsem0)
            store(j, buf0, ssem0)

            @pl.when(j + 2 < per_worker)
            def _():
                store_wait(buf0, ssem0)
                gather(j + 2, idx0, buf0, gsem0)

            gather_wait(idx1, buf1, gsem1)
            store(j + 1, buf1, ssem1)
            return carry
        lax.fori_loop(0, per_worker // 2, two_windows, 0)
        store_wait(buf0, ssem0)
        store_wait(buf1, ssem1)

    return run(ys, pos_t.reshape(k_n))


MOE_RING = 16
MOE_WEIGHT_RING = 8


def _moe_kernel(st_ref, wg_ref, wu_ref, wd_ref, xs_ref, ys_ref, xbuf, ybuf, wgf, wuf, wdf, wgb, wub, wdb,
                semx, semy, semw):
    e = pl.program_id(0)
    n_e = pl.num_programs(0)

    def w_copies(ex):
        slot = ex % MOE_WEIGHT_RING
        return [pltpu.make_async_copy(src.at[ex], dst.at[slot], semw.at[slot, i])
                for i, (src, dst) in enumerate(((wg_ref, wgf), (wu_ref, wuf), (wd_ref, wdf)))]

    @pl.when(e == 0)
    def _():
        for ex in range(MOE_WEIGHT_RING - 1):
            @pl.when(ex < n_e)
            def _(ex=ex):
                for cp in w_copies(ex):
                    cp.start()

    @pl.when(e + (MOE_WEIGHT_RING - 1) < n_e)
    def _():
        for cp in w_copies(e + (MOE_WEIGHT_RING - 1)):
            cp.start()
    g0 = st_ref[e] // EXPERT_BLOCK
    nchunks = st_ref[e + 1] // EXPERT_BLOCK - g0
    ntot = st_ref[n_e] // EXPERT_BLOCK

    def rows(g):
        return pl.ds(pl.multiple_of(g * EXPERT_BLOCK, EXPERT_BLOCK), EXPERT_BLOCK)

    def x_copy(g):
        slot = g % MOE_RING
        return pltpu.make_async_copy(xs_ref.at[rows(g)], xbuf.at[slot], semx.at[slot])

    def y_copy(g):
        slot = g % MOE_RING
        return pltpu.make_async_copy(ybuf.at[slot], ys_ref.at[rows(g)], semy.at[slot])

    @pl.when(e == 0)
    def _():
        for g in range(MOE_RING - 1):
            @pl.when(g < ntot)
            def _(g=g):
                x_copy(g).start()

    for cp in w_copies(e):
        cp.wait()
    wslot = e % MOE_WEIGHT_RING
    wgb[...] = wgf[wslot].astype(BF16)
    wub[...] = wuf[wslot].astype(BF16)
    wdb[...] = wdf[wslot].astype(BF16)

    def take(g):
        @pl.when(g + (MOE_RING - 1) < ntot)
        def _():
            x_copy(g + (MOE_RING - 1)).start()

        x_copy(g).wait()

        @pl.when(g >= MOE_RING)
        def _():
            y_copy(g - MOE_RING).wait()

        return xbuf[g % MOE_RING]

    def expert_mlp(x_words):
        x = _unpack_rows(x_words).astype(BF16)
        mid = _silu(_dot(x, wgb[...])) * _dot(x, wub[...])
        return _pack_rows(_dot(mid.astype(BF16), wdb[...]))

    def put(g, y_words):
        ybuf[g % MOE_RING] = y_words
        y_copy(g).start()

    def run_chunks(g, m):
        xs = [take(g + i) for i in range(m)]
        y = expert_mlp(xs[0] if m == 1 else jnp.concatenate(xs, axis=0))
        for i in range(m):
            put(g + i, y[i * EXPERT_BLOCK:(i + 1) * EXPERT_BLOCK])

    def quad(p, carry):
        run_chunks(g0 + 4 * p, 4)
        return carry

    rest = nchunks % 4
    five = (rest == 1) & (nchunks >= 5)
    n_quads = nchunks // 4 - five.astype(I32)
    lax.fori_loop(0, n_quads, quad, 0)
    g_rest = g0 + 4 * n_quads

    @pl.when(five)
    def _():
        run_chunks(g_rest, 5)

    @pl.when(rest >= 2)
    def _():
        run_chunks(g_rest, 2)

    @pl.when((rest == 3) | (nchunks == 1))
    def _():
        run_chunks(g0 + nchunks - 1, 1)

    @pl.when(e == n_e - 1)
    def _():
        for j in range(MOE_RING):
            g = ntot - MOE_RING + j

            @pl.when(g >= 0)
            def _(g=g):
                y_copy(g).wait()


def _moe(starts, xs, wg, wu, wd):
    n_rows, c = xs.shape
    n_exp, d, f = wg.shape
    grid_spec = pltpu.PrefetchScalarGridSpec(
        num_scalar_prefetch=1, grid=(n_exp,),
        in_specs=[pl.BlockSpec(memory_space=pl.ANY)] * 4,
        out_specs=pl.BlockSpec(memory_space=pl.ANY),
        scratch_shapes=[pltpu.VMEM((MOE_RING, EXPERT_BLOCK, c), I32), pltpu.VMEM((MOE_RING, EXPERT_BLOCK, c), I32),
                        pltpu.VMEM((MOE_WEIGHT_RING, d, f), F32), pltpu.VMEM((MOE_WEIGHT_RING, d, f), F32),
                        pltpu.VMEM((MOE_WEIGHT_RING, f, d), F32),
                        pltpu.VMEM((d, f), BF16), pltpu.VMEM((d, f), BF16), pltpu.VMEM((f, d), BF16),
                        pltpu.SemaphoreType.DMA((MOE_RING,)), pltpu.SemaphoreType.DMA((MOE_RING,)),
                        pltpu.SemaphoreType.DMA((MOE_WEIGHT_RING, 3))])
    return pl.pallas_call(
        _moe_kernel, grid_spec=grid_spec, out_shape=jax.ShapeDtypeStruct((n_rows, c), I32),
        compiler_params=_cp(("arbitrary",)), name="moe",
    )(starts, wg, wu, wd, xs)


def _final_kernel(yg_ref, w8_ref, x1_ref, h2_ref, mod_ref, wsg_ref, wsu_ref, wsd_ref, nf_ref, y_ref, *, d):
    hb = _unpack_rows(h2_ref[...]).astype(BF16)
    shared = _dot((_silu(_dot(hb, wsg_ref[...])) * _dot(hb, wsu_ref[...])).astype(BF16), wsd_ref[...])
    w8 = w8_ref[...]
    routed = _unpack_rows(yg_ref[0]) * w8[:, 0:1]
    for k in range(1, TOP_K):
        routed = routed + _unpack_rows(yg_ref[k]) * w8[:, k:k + 1]
    g2 = mod_ref[...][:, (N_MOD - 1) * d:N_MOD * d]
    x2 = x1_ref[...] + g2 * (routed + shared)
    y_ref[...] = _rms(x2, nf_ref[...])


def _final(yg3, w8, x1, h2, mod, mod_spec, wts, nf, tf, n_tiles, tile0, out_shape, out_spec):
    d = x1.shape[1]
    rows = lambda i: (i, 0)
    in_specs = [pl.BlockSpec((TOP_K, tf, d // 2), lambda i: (0, tile0 + i, 0)),
                pl.BlockSpec((tf, LANE), rows),
                pl.BlockSpec((tf, d), rows), pl.BlockSpec((tf, d // 2), rows), mod_spec]
    in_specs += [_const_spec(w.shape) for w in wts]
    in_specs += [_const_spec(nf.shape)]
    return pl.pallas_call(
        functools.partial(_final_kernel, d=d), grid=(n_tiles,), in_specs=in_specs,
        out_specs=out_spec, out_shape=out_shape,
        compiler_params=_cp(("arbitrary",)), name="final",
    )(yg3, w8, x1, h2, mod, *wts, nf)


def kernel(x_prompt, x_sample, c_prompt, c_sample, state_pool, state_gla, w_ada, b_ada, norm1, w_in,
           pool_w, pool_scale, gla_w_alpha, gla_b_alpha, gla_norm, w_branch_a, w_branch_b, w_out, norm2,
           w_router, router_bias, w_exp_gate, w_exp_up, w_exp_down, w_sh_gate, w_sh_up, w_sh_down, norm_f):
    depth = w_in.shape[0]
    assert depth == 1, "single-layer trunk"
    bp, seq, d = x_prompt.shape
    bs, t_s, _ = x_sample.shape
    _, _, past, pw = state_pool.shape
    _, _, heads, dk, dv = state_gla.shape
    rank = gla_w_alpha.shape[1]
    n_exp = w_router.shape[2]
    assert past == POOL_HIST - 1 and pw == len(POOL_WINDOWS) * LANE and dv == LANE and rank <= LANE
    dm = _Dims(d, pw, heads, dk, dv)
    n_p, n_s = bp * seq, bs * t_s
    n_all = n_p + n_s

    off = [0]
    for sz in (pw, dm.hk, dm.hk, dm.gw, dm.gw, rank, d, d):
        off.append(off[-1] + sz)
    wi = w_in[0]
    win = (wi[:, :off[5]].astype(BF16), wi[:, off[6]:].astype(BF16),
           jnp.concatenate([wi[:, off[5]:off[6]], jnp.zeros((d, LANE - rank), F32)], axis=1).astype(BF16))
    assert off[5] + 2 * d + LANE == dm.zw
    wal = jnp.concatenate([gla_w_alpha[0], jnp.zeros((LANE - rank, dm.hk), F32)], axis=0).astype(BF16)
    bal = gla_b_alpha[0].reshape(1, dm.hk)
    wr = w_router[0]
    wrh = wr.astype(BF16)
    wrl = (wr - wrh.astype(F32)).astype(BF16)
    pwg = pool_w[0]
    zero = jnp.zeros_like(pwg[0])
    pw_pairs = jnp.stack([jnp.block([[pwg[g], zero], [zero, pwg[g + 1]]]) for g in range(0, len(POOL_WINDOWS), 2)])
    post_w = (pw_pairs.astype(BF16), pool_scale[0].reshape(1, pw), gla_norm[0].reshape(1, dm.gw),
              w_branch_a[0].astype(BF16), w_branch_b[0].astype(BF16), w_out[0].astype(BF16),
              norm2[0].reshape(1, d), wrh, wrl)
    n1 = norm1[0].reshape(1, d)

    mod = _ada(jnp.concatenate([c_prompt, c_sample], axis=0), w_ada[0], b_ada[0])
    mod_p = mod[:bp].reshape(bp, 1, N_MOD * d)
    mod_s = mod[bp:]

    rbias = router_bias[0].reshape(1, n_exp)
    x1_p, h2_p, idx_p, w8_p, cnt_p, npool_p, ngla_p = _mix_prompt(
        dm, x_prompt.reshape(n_p, d), mod_p, (n1,) + win + (wal, bal) + post_w + (rbias,), bp, seq, n_exp)

    xs2d = x_sample.reshape(bs, t_s * d)
    z_s = _sample_in(dm, xs2d, mod_s, n1, win, bs, t_s)
    mixed_s, o_s, npool_s, ngla_s = _sample_state(
        dm, z_s, state_pool[0].reshape(bs, past * pw), state_gla[0].reshape(bs, dm.hk, dv), wal, bal, bs, t_s)
    x1_s, h2_s, lg_s = _sample_post(dm, xs2d, mod_s, z_s, mixed_s, o_s, post_w, bs, t_s, n_exp)

    tr = 512 if (n_p % 512 == 0 and n_s % 512 == 0) else LANE
    assert n_p % tr == 0 and n_s % tr == 0 and n_p % bs == 0
    idx_s, w8_s, cnt_s = _route(lg_s, rbias, tr)
    nblk = (n_all * TOP_K) // EXPERT_BLOCK + n_exp
    pos_t, bounds = _plan(jnp.concatenate([idx_p, idx_s], axis=1), cnt_p + cnt_s, tr)
    starts = jnp.concatenate([bounds[0], bounds[1, n_exp - 1:]])

    xs = _dispatch(h2_p, h2_s, pos_t, nblk * EXPERT_BLOCK)
    ys = _moe(starts, xs, w_exp_gate[0], w_exp_up[0], w_exp_down[0])
    yg3 = _combine_gather(ys, pos_t).reshape(TOP_K, n_all, d // 2)

    sh_w = (w_sh_gate[0].astype(BF16), w_sh_up[0].astype(BF16), w_sh_down[0].astype(BF16))
    nf = norm_f.reshape(1, d)
    tf = 512 if seq % 512 == 0 else seq
    n_tp = seq // tf
    y_p = _final(
        yg3, w8_p, x1_p, h2_p, mod_p,
        pl.BlockSpec((None, 1, N_MOD * d), lambda i: (i // n_tp, 0, 0)), sh_w, nf, tf, n_p // tf, 0,
        jax.ShapeDtypeStruct((n_p, d), F32), pl.BlockSpec((tf, d), lambda i: (i, 0)))
    y_s = _final(
        yg3, w8_s, x1_s, h2_s, mod_s,
        _const_spec(mod_s.shape), sh_w, nf, bs, t_s, n_p // bs,
        jax.ShapeDtypeStruct((bs, t_s * d), F32), pl.BlockSpec((bs, d), lambda i: (0, i)))

    return (y_p.reshape(bp, seq, d), y_s.reshape(bs, t_s, d),
            npool_p.reshape(depth, bp, past, pw), ngla_p.reshape(depth, bp, heads, dk, dv),
            npool_s.reshape(depth, bs, past, pw), ngla_s.reshape(depth, bs, heads, dk, dv))
```

```python
import functools

import jax
import jax.numpy as jnp
from jax import lax
from jax.experimental import pallas as pl
from jax.experimental.pallas import tpu as pltpu
from jax.experimental.pallas import tpu_sc as plsc

F32 = jnp.float32
BF16 = jnp.bfloat16
I32 = jnp.int32

EPS = 1e-6
N_MOD = 6
POOL_WINDOWS = (2, 4, 8, 16)
POOL_HIST = 16
GLA_GATE_TEMP = 16.0
TOP_K = 8
ROUTED_SCALE = 2.5
PAST_LEN = 16384
EXPERT_BLOCK = 128
LANE = 128
GLA_FAST_MAX_DECAY = 40.0
PROJ_PIECE = 512
PROJ_LEAD = 2
VMEM_LIMIT = 56 * 1024 * 1024


def _cp(sem, vmem=VMEM_LIMIT):
    return pltpu.CompilerParams(dimension_semantics=sem, vmem_limit_bytes=vmem)


def _rms(x, g):
    return x * lax.rsqrt(jnp.mean(x * x, axis=-1, keepdims=True) + EPS) * g


def _silu(x):
    return x * jax.nn.sigmoid(x)


def _dot(a, b):
    return jnp.dot(a, b, preferred_element_type=F32)


def _split3(a):
    a0 = a.astype(BF16)
    r1 = a - a0.astype(F32)
    a1 = r1.astype(BF16)
    a2 = (r1 - a1.astype(F32)).astype(BF16)
    return a0, a1, a2


def _mod_parts(mod, d):
    return [mod[:, i * d:(i + 1) * d] for i in range(N_MOD)]


_HI16 = -65536


def _pack_rows(v):
    c = v.shape[1] // 2
    lo = lax.bitcast_convert_type(v[:, :c].astype(BF16).astype(F32), I32)
    hi = lax.bitcast_convert_type(v[:, c:].astype(BF16).astype(F32), I32)
    return (hi & _HI16) | lax.shift_right_logical(lo, 16)


def _unpack_rows(w):
    lo = lax.bitcast_convert_type(lax.shift_left(w, 16), F32)
    hi = lax.bitcast_convert_type(w & _HI16, F32)
    return jnp.concatenate([lo, hi], axis=1)


def _ada_kernel(cp_ref, cs_ref, w_ref, b_ref, op_ref, os_ref):
    w = w_ref[...].astype(BF16)
    for c_ref, o_ref in ((cp_ref, op_ref), (cs_ref, os_ref)):
        o_ref[...] = _dot(_silu(c_ref[...]).astype(BF16), w) + b_ref[...]


def _ada(c_p, c_s, w_ada, b_ada):
    d, cols = w_ada.shape
    bc = d
    whole = lambda c: pl.BlockSpec(c.shape, lambda i: (0, 0))
    col_block = lambda rows: pl.BlockSpec((rows, bc), lambda i: (0, i))
    return pl.pallas_call(
        _ada_kernel,
        grid=(cols // bc,),
        in_specs=[whole(c_p), whole(c_s), col_block(d), col_block(1)],
        out_specs=[col_block(c_p.shape[0]), col_block(c_s.shape[0])],
        out_shape=[jax.ShapeDtypeStruct((c_p.shape[0], cols), F32), jax.ShapeDtypeStruct((c_s.shape[0], cols), F32)],
        compiler_params=_cp(("arbitrary",)),
        name="ada",
    )(c_p, c_s, w_ada, b_ada.reshape(1, cols))


class _Dims:
    def __init__(self, d, pw, heads, dk, dv):
        self.d, self.pw, self.h, self.dk, self.dv = d, pw, heads, dk, dv
        self.hk, self.gw = heads * dk, heads * dv
        self.q0 = pw
        self.k0 = self.q0 + self.hk
        self.v0 = self.k0 + self.hk
        self.go0 = self.v0 + self.gw
        self.ga0 = self.go0 + self.gw
        self.gb0 = self.ga0 + d
        self.al0 = self.gb0 + d
        self.zw = self.al0 + LANE


def _log_decay(z_al, wal_ref, bal_ref):
    xal = _dot(z_al.astype(BF16), wal_ref[...]) + bal_ref[...]
    return jax.nn.log_sigmoid(xal) / GLA_GATE_TEMP


def _pool_project(mixed_groups, pw_ref, ps_ref):
    ys = [_dot(jnp.concatenate(mixed_groups[2 * p:2 * p + 2], axis=1).astype(BF16), pw_ref[p])
          for p in range(len(mixed_groups) // 2)]
    return jnp.concatenate(ys, axis=1) * ps_ref[...]


def _post_mix(dm, x, mod, ya, o, go, ga, gb, gn_ref, wa_ref, wb_ref, wo_ref, n2_ref, wrh_ref, wrl_ref,
              filler=lambda: None):
    _, _, g1, sh2, sc2, _ = _mod_parts(mod, dm.d)
    parts = []
    for h in range(dm.h):
        oh = o[:, h * dm.dv:(h + 1) * dm.dv]
        parts.append(oh * lax.rsqrt(jnp.mean(oh * oh, axis=-1, keepdims=True) + EPS))
    yb = jnp.concatenate(parts, axis=1) * gn_ref[...] * _silu(go)
    filler()
    m = (jax.nn.sigmoid(ga) * _dot(ya.astype(BF16), wa_ref[...])
         + jax.nn.sigmoid(gb) * _dot(yb.astype(BF16), wb_ref[...]))
    filler()
    x1 = x + g1 * _dot(m.astype(BF16), wo_ref[...])
    h2 = _rms(x1, n2_ref[...]) * (1 + sc2) + sh2
    hi = h2.astype(BF16)
    lo = (h2 - hi.astype(F32)).astype(BF16)
    filler()
    logits = _dot(hi, wrh_ref[...]) + (_dot(hi, wrl_ref[...]) + _dot(lo, wrh_ref[...]))
    filler()
    return x1, _pack_rows(h2), logits


def _topk_rows(logits, bias):
    tr, n_exp = logits.shape
    s = jax.nn.sigmoid(logits)
    cur = s + bias
    lane = lax.broadcasted_iota(I32, (1, n_exp), 1).astype(F32)
    slot = lax.broadcasted_iota(I32, (1, LANE), 1)
    idx8 = jnp.zeros((tr, LANE), F32)
    w8 = jnp.zeros((tr, LANE), F32)
    wsum = jnp.zeros((tr, 1), F32)
    cur0 = cur
    for k in range(TOP_K):
        m = jnp.max(cur, axis=-1, keepdims=True)
        am = jnp.min(jnp.where(cur == m, lane, float(n_exp)), axis=-1, keepdims=True)
        oh = lane == am
        sk = jnp.sum(jnp.where(oh, s, 0.0), axis=-1, keepdims=True)
        cur = jnp.where(oh, -jnp.inf, cur)
        idx8 = jnp.where(slot == k, am, idx8)
        w8 = jnp.where(slot == k, sk, w8)
        wsum = wsum + sk
    picked = jnp.where((cur == -jnp.inf) & (cur0 > -jnp.inf), 1.0, 0.0)
    return idx8.T[0:TOP_K, :].astype(I32), w8 / wsum * ROUTED_SCALE, picked.astype(BF16)


def _count_members(picked):
    return _dot(jnp.ones((8, picked.shape[0]), BF16), picked)


def _in_project(dm, x, mod, n1_ref, win_refs, z_ref):
    h = (_rms(x, n1_ref[...]) * (1 + mod[:, dm.d:2 * dm.d]) + mod[:, 0:dm.d]).astype(BF16)
    starts = [0]
    for w_ref in win_refs:
        starts.append(starts[-1] + w_ref.shape[1])
    for i in (0, 2, 1):
        z_ref[:, starts[i]:starts[i + 1]] = _dot(h, win_refs[i][...])


def _mixp_kernel(x_ref, mod_ref, n1_ref, wia_ref, wig_ref, wil_ref, wal_ref, bal_ref, pw_ref, ps_ref, gn_ref,
                 wa_ref, wb_ref, wo_ref, n2_ref, wrh_ref, wrl_ref, rb_ref,
                 x1_ref, h2_ref, idx_ref, w8_ref, cnt_ref, np_ref, ng_ref,
                 za_ref, zb_ref, xk_ref, mk_ref, hn_ref, lg_ref, uext_ref, s_ref, b_ref, oi_ref,
                 *, dm, tt, n_t, tiles):
    n = pl.program_id(0)
    j = jnp.clip(n - 1, 0, tiles - 1) % n_t

    @pl.when(n == 0)
    def _():
        za_ref[...] = jnp.zeros(za_ref.shape, F32)
        zb_ref[...] = jnp.zeros(zb_ref.shape, F32)
        xk_ref[...] = jnp.zeros(xk_ref.shape, F32)
        mk_ref[...] = jnp.zeros(mk_ref.shape, F32)
        lg_ref[...] = jnp.zeros(lg_ref.shape, F32)
        cnt_ref[...] = jnp.zeros(cnt_ref.shape, F32)

    def route_previous():
        idx_ref[...], w8_ref[...], picked = _topk_rows(lg_ref[...], rb_ref[...])
        return picked

    def count_routed(picked):
        cnt_ref[...] += jnp.where(n >= 2, _count_members(picked), 0.0)

    @pl.when((j == 0) & (n <= tiles))
    def _():
        uext_ref[0:POOL_HIST, :] = jnp.zeros((POOL_HIST, dm.pw), F32)
        s_ref[...] = jnp.zeros(s_ref.shape, F32)

    @pl.when((n % 2 == 0) & (n <= tiles))
    def _():
        _mixp_region(za_ref, zb_ref, x_ref, mod_ref, n1_ref, wia_ref, wig_ref, wil_ref, wal_ref, bal_ref, pw_ref,
                     ps_ref, gn_ref, wa_ref, wb_ref, wo_ref, n2_ref, wrh_ref, wrl_ref, x1_ref, h2_ref, lg_ref,
                     xk_ref, mk_ref, hn_ref, uext_ref, s_ref, b_ref, oi_ref, dm=dm, tt=tt, j=j,
                     route_previous=route_previous, count_routed=count_routed)

    @pl.when((n % 2 == 1) & (n <= tiles))
    def _():
        _mixp_region(zb_ref, za_ref, x_ref, mod_ref, n1_ref, wia_ref, wig_ref, wil_ref, wal_ref, bal_ref, pw_ref,
                     ps_ref, gn_ref, wa_ref, wb_ref, wo_ref, n2_ref, wrh_ref, wrl_ref, x1_ref, h2_ref, lg_ref,
                     xk_ref, mk_ref, hn_ref, uext_ref, s_ref, b_ref, oi_ref, dm=dm, tt=tt, j=j,
                     route_previous=route_previous, count_routed=count_routed)

    @pl.when(n > tiles)
    def _():
        count_routed(route_previous())

    xk_ref[...] = x_ref[...]
    mk_ref[...] = mod_ref[...]

    @pl.when((j == n_t - 1) & (n <= tiles))
    def _():
        np_ref[...] = uext_ref[tt + 1:tt + POOL_HIST, :]
        ng_ref[...] = s_ref[...]


def _mixp_region(zp_ref, z_ref, x_ref, mod_ref, n1_ref, wia_ref, wig_ref, wil_ref, wal_ref, bal_ref, pw_ref,
                 ps_ref, gn_ref, wa_ref, wb_ref, wo_ref, n2_ref, wrh_ref, wrl_ref, x1_ref, h2_ref, lg_ref,
                 xk_ref, mk_ref, hn_ref, uext_ref, s_ref, b_ref, oi_ref, *, dm, tt, j, route_previous, count_routed):
    picked = route_previous()
    scale = dm.dk ** -0.5
    lane = lax.broadcasted_iota(I32, (1, dm.hk), 1)
    head_masks = [(lane >= h * dm.dk) & (lane < (h + 1) * dm.dk) for h in range(dm.h)]
    causal = lax.broadcasted_iota(I32, (tt, tt), 0) >= lax.broadcasted_iota(I32, (tt, tt), 1)

    def pool_branch():
        pos = j * tt + lax.broadcasted_iota(I32, (tt, 1), 0)
        gw = dm.pw // len(POOL_WINDOWS)
        mixed = []
        for g, w in enumerate(POOL_WINDOWS):
            c0 = g * gw
            cur = uext_ref[POOL_HIST:POOL_HIST + tt, c0:c0 + gw]
            acc = cur
            for i in range(1, w):
                acc = acc + uext_ref[POOL_HIST - i:POOL_HIST - i + tt, c0:c0 + gw]
            cnt = jnp.minimum(w, pos + 1).astype(F32)
            mixed.append(acc / cnt - cur)
        return _pool_project(mixed, pw_ref, ps_ref)

    def scaled_queries(bc):
        return (z_ref[:, dm.q0:dm.q0 + dm.hk] * scale) * jnp.exp(bc)

    def inter_chunk(qe):
        s_rows = lax.broadcasted_iota(I32, (dm.hk, dm.gw), 0) // dm.dk
        s_cols = lax.broadcasted_iota(I32, (dm.hk, dm.gw), 1) // dm.dv
        s_all = jnp.where(s_rows == s_cols, jnp.concatenate([s_ref[...]] * dm.h, axis=1), 0.0)
        return _dot(qe.astype(BF16), s_all.astype(BF16))

    def finish(ya, o, filler=lambda: None):
        x1, h2, logits = _post_mix(
            dm, xk_ref[...], mk_ref[...], ya, o, z_ref[:, dm.go0:dm.go0 + dm.gw],
            z_ref[:, dm.ga0:dm.ga0 + dm.d], z_ref[:, dm.gb0:dm.gb0 + dm.d],
            gn_ref, wa_ref, wb_ref, wo_ref, n2_ref, wrh_ref, wrl_ref, filler)
        x1_ref[...] = x1
        h2_ref[...] = h2
        lg_ref[...] = logits

    mod_in = mod_ref[...]
    hn_ref[...] = (_rms(x_ref[...], n1_ref[...]) * (1 + mod_in[:, dm.d:2 * dm.d]) + mod_in[:, 0:dm.d]).astype(BF16)
    pieces = iter([(w_ref, a, min(a + PROJ_PIECE, w_ref.shape[1]), c0)
                   for w_ref, c0 in ((wia_ref, 0), (wil_ref, dm.al0), (wig_ref, dm.ga0))
                   for a in range(0, w_ref.shape[1], PROJ_PIECE)])

    def project_piece():
        piece = next(pieces, None)
        if piece is not None:
            w_ref, a, b, c0 = piece
            zp_ref[:, c0 + a:c0 + b] = _dot(hn_ref[...], w_ref[:, a:b])
        return piece is not None

    for _ in range(PROJ_LEAD):
        project_piece()
    uext_ref[POOL_HIST:POOL_HIST + tt, :] = z_ref[:, 0:dm.pw]
    ya = pool_branch()
    project_piece()
    la = _log_decay(z_ref[:, dm.al0:dm.al0 + LANE], wal_ref, bal_ref)
    tri = jnp.where(causal, 1.0, 0.0).astype(BF16)
    a0, a1, a2 = _split3(la)
    bc = _dot(tri, a0) + (_dot(tri, a1) + _dot(tri, a2))
    b_ref[...] = bc
    project_piece()
    qe = scaled_queries(bc)
    k = z_ref[:, dm.k0:dm.k0 + dm.hk]
    vb = z_ref[:, dm.v0:dm.v0 + dm.gw].astype(BF16)
    blast = bc[tt - 1:tt, :]
    fast = jnp.max(-blast) <= GLA_FAST_MAX_DECAY
    ke = (k * jnp.exp(-bc)).astype(BF16)
    intra = []
    for h in range(dm.h):
        qh = jnp.where(head_masks[h], qe, 0.0).astype(BF16)
        sc = lax.dot_general(qh, ke, (((1,), (1,)), ((), ())), preferred_element_type=F32)
        sc = jnp.where(causal, sc, 0.0).astype(BF16)
        intra.append(_dot(sc, vb[:, h * dm.dv:(h + 1) * dm.dv]))
    count_routed(picked)
    project_piece()
    finish(ya, inter_chunk(qe) + jnp.concatenate(intra, axis=1), project_piece)
    while project_piece():
        pass

    @pl.when(jnp.logical_not(fast))
    def _():
        rows = lax.broadcasted_iota(I32, (tt, 1), 0)
        for h in range(dm.h):
            def body(tb, carry, h=h):
                base = pl.multiple_of(tb * 8, 8)
                q8 = z_ref[pl.ds(base, 8), dm.q0:dm.q0 + dm.hk] * scale
                b8 = b_ref[pl.ds(base, 8), :]
                vh = z_ref[:, dm.v0 + h * dm.dv:dm.v0 + (h + 1) * dm.dv]
                out_rows = []
                for r in range(8):
                    keep = (rows <= base + r) & head_masks[h]
                    dec = jnp.exp(jnp.where(keep, b8[r:r + 1, :] - b_ref[...], -jnp.inf))
                    wgt = (q8[r:r + 1, :] * dec) * z_ref[:, dm.k0:dm.k0 + dm.hk]
                    s_col = jnp.sum(wgt, axis=-1, keepdims=True)
                    out_rows.append(jnp.sum(s_col * vh, axis=0, keepdims=True))
                oi_ref[pl.ds(base, 8), h * dm.dv:(h + 1) * dm.dv] = jnp.concatenate(out_rows, axis=0)
                return carry
            lax.fori_loop(0, tt // 8, body, 0)
        finish(pool_branch(), inter_chunk(scaled_queries(b_ref[...])) + oi_ref[...])

    bc = b_ref[...]
    blast = bc[tt - 1:tt, :]
    kd = z_ref[:, dm.k0:dm.k0 + dm.hk] * jnp.exp(blast - bc)
    u_all = _dot(kd.T.astype(BF16), z_ref[:, dm.v0:dm.v0 + dm.gw].astype(BF16))
    dec_col = jnp.broadcast_to(jnp.exp(blast), (dm.hk, dm.hk)).T[:, 0:dm.dv]
    upd = jnp.concatenate(
        [u_all[h * dm.dk:(h + 1) * dm.dk, h * dm.dv:(h + 1) * dm.dv] for h in range(dm.h)], axis=0)
    s_ref[...] = dec_col * s_ref[...] + upd
    uext_ref[0:POOL_HIST, :] = uext_ref[tt:tt + POOL_HIST, :]


def _const_spec(shape):
    nd = len(shape)
    return pl.BlockSpec(shape, lambda *_: (0,) * nd, pipeline_mode=pl.Buffered(1))


def _mix_prompt(dm, x2d, mod3, wts, bsz, seq, n_exp):
    tt = 256 if seq % 256 == 0 else seq
    n_t = seq // tt
    n = bsz * seq
    d = dm.d
    tiles = bsz * n_t
    kern = functools.partial(_mixp_kernel, dm=dm, tt=tt, n_t=n_t, tiles=tiles)
    proj = lambda i: jnp.minimum(i, tiles - 1)
    fin = lambda i: jnp.clip(i - 1, 0, tiles - 1)
    routed = lambda i: jnp.clip(i - 2, 0, tiles - 1)
    row = lambda i: (fin(i), 0)
    in_specs = [pl.BlockSpec((tt, d), lambda i: (proj(i), 0)),
                pl.BlockSpec((None, 1, N_MOD * d), lambda i: (proj(i) // n_t, 0, 0))]
    in_specs += [_const_spec(w.shape) for w in wts]
    out_specs = [pl.BlockSpec((tt, d), row), pl.BlockSpec((tt, d // 2), row),
                 pl.BlockSpec((TOP_K, tt), lambda i: (0, routed(i))), pl.BlockSpec((tt, LANE), lambda i: (routed(i), 0)),
                 pl.BlockSpec((8, n_exp), lambda i: (0, 0)),
                 pl.BlockSpec((None, POOL_HIST - 1, dm.pw), lambda i: (fin(i) // n_t, 0, 0)),
                 pl.BlockSpec((None, dm.hk, dm.dv), lambda i: (fin(i) // n_t, 0, 0))]
    out_shape = [jax.ShapeDtypeStruct((n, d), F32), jax.ShapeDtypeStruct((n, d // 2), I32),
                 jax.ShapeDtypeStruct((TOP_K, n), I32), jax.ShapeDtypeStruct((n, LANE), F32),
                 jax.ShapeDtypeStruct((8, n_exp), F32),
                 jax.ShapeDtypeStruct((bsz, POOL_HIST - 1, dm.pw), F32),
                 jax.ShapeDtypeStruct((bsz, dm.hk, dm.dv), F32)]
    scratch = [pltpu.VMEM((tt, dm.zw), F32), pltpu.VMEM((tt, dm.zw), F32),
               pltpu.VMEM((tt, d), F32), pltpu.VMEM((1, N_MOD * d), F32), pltpu.VMEM((tt, d), BF16),
               pltpu.VMEM((tt, n_exp), F32), pltpu.VMEM((tt + POOL_HIST, dm.pw), F32),
               pltpu.VMEM((dm.hk, dm.dv), F32), pltpu.VMEM((tt, dm.hk), F32), pltpu.VMEM((tt, dm.gw), F32)]
    return pl.pallas_call(
        kern, grid=(tiles + 2,), in_specs=in_specs, out_specs=out_specs, out_shape=out_shape,
        scratch_shapes=scratch, compiler_params=_cp(("arbitrary",)), name="mixp",
    )(x2d, mod3, *wts)


def _sin_kernel(x_ref, mod_ref, n1_ref, wia_ref, wig_ref, wil_ref, z_ref, *, dm):
    _in_project(dm, x_ref[...], mod_ref[...], n1_ref, (wia_ref, wig_ref, wil_ref), z_ref)


def _sample_in(dm, xs2d, mod_s, n1, win, bs, t_s):
    d = dm.d
    return pl.pallas_call(
        functools.partial(_sin_kernel, dm=dm),
        grid=(t_s,),
        in_specs=[pl.BlockSpec((bs, d), lambda t: (0, t)), _const_spec(mod_s.shape), _const_spec(n1.shape)]
        + [_const_spec(w.shape) for w in win],
        out_specs=pl.BlockSpec((bs, dm.zw), lambda t: (t, 0)),
        out_shape=jax.ShapeDtypeStruct((t_s * bs, dm.zw), F32),
        compiler_params=_cp(("arbitrary",)), name="sample_in",
    )(xs2d, mod_s, n1, *win)


def _sst_kernel(z_ref, pool_ref, gs_ref, wal_ref, bal_ref,
                mixed_ref, o_ref, npool_ref, ngs_ref,
                qt_ref, kt_ref, dt_ref, vt_ref, ot_ref, *, dm, bs, t_s, ch, pos0):
    i = pl.program_id(0)
    n_i = pl.num_programs(0)
    scale = dm.dk ** -0.5
    past = POOL_HIST - 1

    @pl.when(i == 0)
    def _():
        for t in range(t_s):
            r0 = t * bs
            la = _log_decay(z_ref[r0:r0 + bs, dm.al0:dm.al0 + LANE], wal_ref, bal_ref)
            dt_ref[t] = jnp.exp(la).T
            qt_ref[t] = (z_ref[r0:r0 + bs, dm.q0:dm.q0 + dm.hk] * scale).T
            kt_ref[t] = z_ref[r0:r0 + bs, dm.k0:dm.k0 + dm.hk].T
            vt_ref[t] = z_ref[r0:r0 + bs, dm.v0:dm.v0 + dm.gw].T
        ot_ref[...] = jnp.zeros(ot_ref.shape, F32)

        def item(s, c0, c1):
            if s >= 0:
                return z_ref[s * bs:(s + 1) * bs, c0:c1]
            return pool_ref[past + s, :, c0:c1]

        gw = dm.pw // len(POOL_WINDOWS)
        for t in range(t_s):
            for g, w in enumerate(POOL_WINDOWS):
                c0, c1 = g * gw, (g + 1) * gw
                cur = item(t, c0, c1)
                acc = cur
                for r in range(1, w):
                    acc = acc + item(t - r, c0, c1)
                cnt = float(min(w, pos0 + t + 1))
                mixed_ref[t * bs:(t + 1) * bs, c0:c1] = acc / cnt - cur
        for r in range(past):
            npool_ref[r] = item(r - past + t_s, 0, dm.pw)

    per_head = dm.dk // ch
    hoff = pl.multiple_of((i // per_head) * dm.dv, dm.dv)
    for p in range(ch):
        hd = i * ch + p
        st = gs_ref[:, p, :].T
        for t in range(t_s):
            st = dt_ref[t, pl.ds(hd, 1), :] * st + kt_ref[t, pl.ds(hd, 1), :] * vt_ref[t, pl.ds(hoff, dm.dv), :]
            ot_ref[t, pl.ds(hoff, dm.dv), :] += qt_ref[t, pl.ds(hd, 1), :] * st
        ngs_ref[:, p, :] = st.T

    @pl.when(i == n_i - 1)
    def _():
        for t in range(t_s):
            for h in range(dm.h):
                o_ref[t * bs:(t + 1) * bs, h * dm.dv:(h + 1) * dm.dv] = ot_ref[t, h * dm.dv:(h + 1) * dm.dv, :].T


def _sample_state(dm, z_s, pool3, gs3, wal, bal, bs, t_s):
    ch = 16
    n_i = dm.hk // ch
    past = POOL_HIST - 1
    kern = functools.partial(_sst_kernel, dm=dm, bs=bs, t_s=t_s, ch=ch, pos0=PAST_LEN)
    full = lambda shape: pl.BlockSpec(shape, lambda i: (0,) * len(shape))
    return pl.pallas_call(
        kern, grid=(n_i,),
        in_specs=[_const_spec(z_s.shape), _const_spec(pool3.shape),
                  pl.BlockSpec((bs, ch, dm.dv), lambda i: (0, i, 0)),
                  _const_spec(wal.shape), _const_spec(bal.shape)],
        out_specs=[full((t_s * bs, dm.pw)), full((t_s * bs, dm.gw)), full((past, bs, dm.pw)),
                   pl.BlockSpec((bs, ch, dm.dv), lambda i: (0, i, 0))],
        out_shape=[jax.ShapeDtypeStruct((t_s * bs, dm.pw), F32), jax.ShapeDtypeStruct((t_s * bs, dm.gw), F32),
                   jax.ShapeDtypeStruct((past, bs, dm.pw), F32), jax.ShapeDtypeStruct(gs3.shape, F32)],
        scratch_shapes=[pltpu.VMEM((t_s, dm.hk, bs), F32), pltpu.VMEM((t_s, dm.hk, bs), F32),
                        pltpu.VMEM((t_s, dm.hk, bs), F32), pltpu.VMEM((t_s, dm.gw, bs), F32),
                        pltpu.VMEM((t_s, dm.gw, bs), F32)],
        compiler_params=_cp(("arbitrary",)), name="sample_state",
    )(z_s, pool3, gs3, wal, bal)


def _spost_kernel(x_ref, mod_ref, z_ref, mixed_ref, o_ref, pw_ref, ps_ref, gn_ref,
                  wa_ref, wb_ref, wo_ref, n2_ref, wrh_ref, wrl_ref,
                  x1_ref, h2_ref, lg_ref, *, dm):
    gw = dm.pw // len(POOL_WINDOWS)
    ya = _pool_project([mixed_ref[:, g * gw:(g + 1) * gw] for g in range(len(POOL_WINDOWS))], pw_ref, ps_ref)
    x1, h2, logits = _post_mix(
        dm, x_ref[...], mod_ref[...], ya, o_ref[...],
        z_ref[:, dm.go0:dm.go0 + dm.gw], z_ref[:, dm.ga0:dm.ga0 + dm.d], z_ref[:, dm.gb0:dm.gb0 + dm.d],
        gn_ref, wa_ref, wb_ref, wo_ref, n2_ref, wrh_ref, wrl_ref)
    x1_ref[...] = x1
    h2_ref[...] = h2
    lg_ref[...] = logits


def _sample_post(dm, xs2d, mod_s, z_s, mixed_s, o_s, wts, bs, t_s, n_exp):
    d = dm.d
    rows = lambda t: (t, 0)
    in_specs = [pl.BlockSpec((bs, d), lambda t: (0, t)), _const_spec(mod_s.shape),
                pl.BlockSpec((bs, dm.zw), rows), pl.BlockSpec((bs, dm.pw), rows), pl.BlockSpec((bs, dm.gw), rows)]
    in_specs += [_const_spec(w.shape) for w in wts]
    return pl.pallas_call(
        functools.partial(_spost_kernel, dm=dm), grid=(t_s,), in_specs=in_specs,
        out_specs=[pl.BlockSpec((bs, d), rows), pl.BlockSpec((bs, d // 2), rows), pl.BlockSpec((bs, n_exp), rows)],
        out_shape=[jax.ShapeDtypeStruct((t_s * bs, d), F32), jax.ShapeDtypeStruct((t_s * bs, d // 2), I32),
                   jax.ShapeDtypeStruct((t_s * bs, n_exp), F32)],
        compiler_params=_cp(("arbitrary",)), name="sample_post",
    )(xs2d, mod_s, z_s, mixed_s, o_s, *wts)


def _route_kernel(lg_ref, bias_ref, idx_ref, w_ref, cnt_ref):
    i = pl.program_id(0)
    idx_ref[...], w_ref[...], picked = _topk_rows(lg_ref[...], bias_ref[...])

    @pl.when(i == 0)
    def _():
        cnt_ref[...] = jnp.zeros(cnt_ref.shape, F32)

    cnt_ref[...] += _count_members(picked)


def _route(logits, bias, tr):
    n, n_exp = logits.shape
    rows = lambda i: (i, 0)
    return pl.pallas_call(
        _route_kernel, grid=(n // tr,),
        in_specs=[pl.BlockSpec((tr, n_exp), rows), _const_spec((1, n_exp))],
        out_specs=[pl.BlockSpec((TOP_K, tr), lambda i: (0, i)), pl.BlockSpec((tr, LANE), rows),
                   pl.BlockSpec((8, n_exp), lambda i: (0, 0))],
        out_shape=[jax.ShapeDtypeStruct((TOP_K, n), I32), jax.ShapeDtypeStruct((n, LANE), F32),
                   jax.ShapeDtypeStruct((8, n_exp), F32)],
        compiler_params=_cp(("arbitrary",)), name="route",
    )(logits, bias)


def _plan_kernel(idx_ref, cnt_ref, pos_ref, bounds_ref, base_ref, dest_ref, *, n_exp):
    i = pl.program_id(0)
    tr = idx_ref.shape[1]

    @pl.when(i == 0)
    def _():
        lane = lax.broadcasted_iota(I32, (1, n_exp), 1)
        cnt = cnt_ref[...].astype(I32)
        padded = ((cnt + (EXPERT_BLOCK - 1)) // EXPERT_BLOCK) * EXPERT_BLOCK
        pe = padded
        s = 1
        while s < n_exp:
            pe = pe + jnp.where(lane >= s, pltpu.roll(pe, s, 1), 0)
            s *= 2
        row = lax.broadcasted_iota(I32, (8, n_exp), 0)
        bounds_ref[...] = jnp.where(row == 0, pe - padded, pe)
        first = (pe - padded).astype(F32)[0:1, :]
        base_ref[...] = jnp.broadcast_to(first, (LANE, n_exp)).T

    eidx = lax.broadcasted_iota(I32, (n_exp, 1), 0)
    idx = idx_ref[...]
    member = eidx == idx[0:1, :]
    for k in range(1, TOP_K):
        member = member | (eidx == idx[k:k + 1, :])
    mb = jnp.where(member, 1.0, 0.0).astype(BF16)
    before = jnp.where(lax.broadcasted_iota(I32, (tr, tr), 0) < lax.broadcasted_iota(I32, (tr, tr), 1),
                       1.0, 0.0).astype(BF16)
    dest_ref[...] = _dot(mb, before) + base_ref[:, 0:1]
    for b in range(tr // LANE):
        cols = slice(b * LANE, (b + 1) * LANE)
        dest = dest_ref[:, cols]
        rows = [jnp.sum(jnp.where(eidx == idx[k:k + 1, cols], dest, 0.0), axis=0, keepdims=True)
                for k in range(TOP_K)]
        pos_ref[:, cols] = jnp.concatenate(rows, axis=0).astype(I32)
    base_ref[...] += jnp.sum(mb.astype(F32), axis=1, keepdims=True)


def _plan(idx_t, counts, tr):
    n = idx_t.shape[1]
    n_exp = counts.shape[1]
    cols = lambda i: (0, i)
    return pl.pallas_call(
        functools.partial(_plan_kernel, n_exp=n_exp), grid=(n // tr,),
        in_specs=[pl.BlockSpec((TOP_K, tr), cols), _const_spec((8, n_exp))],
        out_specs=[pl.BlockSpec((TOP_K, tr), cols), pl.BlockSpec((8, n_exp), lambda i: (0, 0))],
        out_shape=[jax.ShapeDtypeStruct((TOP_K, n), I32), jax.ShapeDtypeStruct((8, n_exp), I32)],
        scratch_shapes=[pltpu.VMEM((n_exp, LANE), F32), pltpu.VMEM((n_exp, tr), F32)],
        compiler_params=_cp(("arbitrary",)), name="plan",
    )(idx_t, counts)


SC_CORES = 2
SC_SUBCORES = 16
SC_AXES = ("core", "subcore")
SC_WORKERS = SC_CORES * SC_SUBCORES
SC_WINDOW = 128


def _sc_mesh():
    return plsc.VectorSubcoreMesh(core_axis_name=SC_AXES[0], subcore_axis_name=SC_AXES[1],
                                  num_cores=SC_CORES, num_subcores=SC_SUBCORES)


def _sc_worker():
    return lax.axis_index(SC_AXES[1]) * SC_CORES + lax.axis_index(SC_AXES[0])


def _dispatch(h_p, h_s, pos_t, n_rows):
    n_p, c = h_p.shape
    n_s = h_s.shape[0]
    win_p, win_s = n_p // SC_WINDOW, n_s // SC_WINDOW
    assert n_p % (SC_WINDOW * SC_WORKERS) == 0 and n_s % SC_WINDOW == 0 and win_s <= SC_WORKERS

    @functools.partial(
        pl.kernel, out_type=jax.ShapeDtypeStruct((n_rows, c), h_p.dtype), mesh=_sc_mesh(), name="dispatch",
        scratch_types=[pltpu.VMEM((TOP_K, SC_WINDOW), I32), pltpu.VMEM((SC_WINDOW, c), h_p.dtype),
                       pltpu.SemaphoreType.DMA])
    def run(hp_hbm, hs_hbm, pos_hbm, xs_hbm, idx_v, rows_v, sem):
        wid = _sc_worker()

        def scatter_window(h_hbm, row0, tok0):
            pltpu.sync_copy(h_hbm.at[pl.ds(row0, SC_WINDOW)], rows_v)
            pltpu.sync_copy(pos_hbm.at[:, pl.ds(tok0, SC_WINDOW)], idx_v)
            copies = [pltpu.async_copy(rows_v, xs_hbm.at[idx_v.at[k]], sem) for k in range(TOP_K)]
            for cp in copies:
                cp.wait()

        def prompt_window(j, carry):
            row0 = pl.multiple_of((wid * (win_p // SC_WORKERS) + j) * SC_WINDOW, SC_WINDOW)
            scatter_window(hp_hbm, row0, row0)
            return carry
        lax.fori_loop(0, win_p // SC_WORKERS, prompt_window, 0)

        @pl.when(wid < win_s)
        def _():
            row0 = pl.multiple_of(wid * SC_WINDOW, SC_WINDOW)
            scatter_window(hs_hbm, row0, n_p + row0)

    return run(h_p, h_s, pos_t)


def _combine_gather(ys, pos_t):
    k_n = pos_t.shape[0] * pos_t.shape[1]
    c = ys.shape[1]
    win = SC_WINDOW // 2
    per_worker = k_n // (win * SC_WORKERS)
    assert k_n % (win * SC_WORKERS) == 0 and per_worker % 2 == 0
    dma = pltpu.SemaphoreType.DMA

    @functools.partial(
        pl.kernel, out_type=jax.ShapeDtypeStruct((k_n, c), ys.dtype), mesh=_sc_mesh(), name="combine_gather",
        scratch_types=[pltpu.VMEM((win,), I32), pltpu.VMEM((win,), I32),
                       pltpu.VMEM((win, c), ys.dtype), pltpu.VMEM((win, c), ys.dtype), dma, dma, dma, dma])
    def run(ys_hbm, idx_hbm, out_hbm, idx0, idx1, buf0, buf1, gsem0, gsem1, ssem0, ssem1):
        wid = _sc_worker()

        def rows(j):
            return pl.ds(pl.multiple_of((wid * per_worker + j) * win, win), win)

        def gather(j, idx_v, buf, sem):
            pltpu.sync_copy(idx_hbm.at[rows(j)], idx_v)
            pltpu.async_copy(ys_hbm.at[idx_v], buf, sem)

        def gather_wait(idx_v, buf, sem):
            pltpu.make_async_copy(ys_hbm.at[idx_v], buf, sem).wait()

        def store(j, buf, sem):
            pltpu.async_copy(buf, out_hbm.at[rows(j)], sem)

        def store_wait(buf, sem):
            pltpu.make_async_copy(buf, out_hbm.at[rows(0)], sem).wait()

        gather(0, idx0, buf0, gsem0)

        def two_windows(i, carry):
            j = 2 * i

            @pl.when(i > 0)
            def _():
                store_wait(buf1, ssem1)

            gather(j + 1, idx1, buf1, gsem1)
            gather_wait(idx0, buf0, gsem0)
            store(j, buf0, ssem0)

            @pl.when(j + 2 < per_worker)
            def _():
                store_wait(buf0, ssem0)
                gather(j + 2, idx0, buf0, gsem0)

            gather_wait(idx1, buf1, gsem1)
            store(j + 1, buf1, ssem1)
            return carry
        lax.fori_loop(0, per_worker // 2, two_windows, 0)
        store_wait(buf0, ssem0)
        store_wait(buf1, ssem1)

    return run(ys, pos_t.reshape(k_n))


MOE_RING = 16
MOE_WEIGHT_RING = 8


def _moe_kernel(st_ref, wg_ref, wu_ref, wd_ref, xs_ref, ys_ref, xbuf, ybuf, wgf, wuf, wdf, wgb, wub, wdb,
                semx, semy, semw):
    e = pl.program_id(0)
    n_e = pl.num_programs(0)

    def w_copies(ex):
        slot = ex % MOE_WEIGHT_RING
        return [pltpu.make_async_copy(src.at[ex], dst.at[slot], semw.at[slot, i])
                for i, (src, dst) in enumerate(((wg_ref, wgf), (wu_ref, wuf), (wd_ref, wdf)))]

    @pl.when(e == 0)
    def _():
        for ex in range(MOE_WEIGHT_RING - 1):
            @pl.when(ex < n_e)
            def _(ex=ex):
                for cp in w_copies(ex):
                    cp.start()

    @pl.when(e + (MOE_WEIGHT_RING - 1) < n_e)
    def _():
        for cp in w_copies(e + (MOE_WEIGHT_RING - 1)):
            cp.start()
    g0 = st_ref[e] // EXPERT_BLOCK
    nchunks = st_ref[e + 1] // EXPERT_BLOCK - g0
    ntot = st_ref[n_e] // EXPERT_BLOCK

    def rows(g):
        return pl.ds(pl.multiple_of(g * EXPERT_BLOCK, EXPERT_BLOCK), EXPERT_BLOCK)

    def x_copy(g):
        slot = g % MOE_RING
        return pltpu.make_async_copy(xs_ref.at[rows(g)], xbuf.at[slot], semx.at[slot])

    def y_copy(g):
        slot = g % MOE_RING
        return pltpu.make_async_copy(ybuf.at[slot], ys_ref.at[rows(g)], semy.at[slot])

    @pl.when(e == 0)
    def _():
        for g in range(MOE_RING - 1):
            @pl.when(g < ntot)
            def _(g=g):
                x_copy(g).start()

    for cp in w_copies(e):
        cp.wait()
    wslot = e % MOE_WEIGHT_RING
    wgb[...] = wgf[wslot].astype(BF16)
    wub[...] = wuf[wslot].astype(BF16)
    wdb[...] = wdf[wslot].astype(BF16)

    def take(g):
        @pl.when(g + (MOE_RING - 1) < ntot)
        def _():
            x_copy(g + (MOE_RING - 1)).start()

        x_copy(g).wait()

        @pl.when(g >= MOE_RING)
        def _():
            y_copy(g - MOE_RING).wait()

        return xbuf[g % MOE_RING]

    def expert_mlp(x_words):
        x = _unpack_rows(x_words).astype(BF16)
        mid = _silu(_dot(x, wgb[...])) * _dot(x, wub[...])
        return _pack_rows(_dot(mid.astype(BF16), wdb[...]))

    def put(g, y_words):
        ybuf[g % MOE_RING] = y_words
        y_copy(g).start()

    def run_chunks(g, m):
        xs = [take(g + i) for i in range(m)]
        y = expert_mlp(xs[0] if m == 1 else jnp.concatenate(xs, axis=0))
        for i in range(m):
            put(g + i, y[i * EXPERT_BLOCK:(i + 1) * EXPERT_BLOCK])

    def quad(p, carry):
        run_chunks(g0 + 4 * p, 4)
        return carry

    rest = nchunks % 4
    five = (rest == 1) & (nchunks >= 5)
    n_quads = nchunks // 4 - five.astype(I32)
    lax.fori_loop(0, n_quads, quad, 0)
    g_rest = g0 + 4 * n_quads

    @pl.when(five)
    def _():
        run_chunks(g_rest, 5)

    @pl.when(rest >= 2)
    def _():
        run_chunks(g_rest, 2)

    @pl.when((rest == 3) | (nchunks == 1))
    def _():
        run_chunks(g0 + nchunks - 1, 1)

    @pl.when(e == n_e - 1)
    def _():
        for j in range(MOE_RING):
            g = ntot - MOE_RING + j

            @pl.when(g >= 0)
            def _(g=g):
                y_copy(g).wait()


def _moe(starts, xs, wg, wu, wd):
    n_rows, c = xs.shape
    n_exp, d, f = wg.shape
    grid_spec = pltpu.PrefetchScalarGridSpec(
        num_scalar_prefetch=1, grid=(n_exp,),
        in_specs=[pl.BlockSpec(memory_space=pl.ANY)] * 4,
        out_specs=pl.BlockSpec(memory_space=pl.ANY),
        scratch_shapes=[pltpu.VMEM((MOE_RING, EXPERT_BLOCK, c), I32), pltpu.VMEM((MOE_RING, EXPERT_BLOCK, c), I32),
                        pltpu.VMEM((MOE_WEIGHT_RING, d, f), F32), pltpu.VMEM((MOE_WEIGHT_RING, d, f), F32),
                        pltpu.VMEM((MOE_WEIGHT_RING, f, d), F32),
                        pltpu.VMEM((d, f), BF16), pltpu.VMEM((d, f), BF16), pltpu.VMEM((f, d), BF16),
                        pltpu.SemaphoreType.DMA((MOE_RING,)), pltpu.SemaphoreType.DMA((MOE_RING,)),
                        pltpu.SemaphoreType.DMA((MOE_WEIGHT_RING, 3))])
    return pl.pallas_call(
        _moe_kernel, grid_spec=grid_spec, out_shape=jax.ShapeDtypeStruct((n_rows, c), I32),
        compiler_params=_cp(("arbitrary",)), name="moe",
    )(starts, wg, wu, wd, xs)


def _final_kernel(yg_ref, w8_ref, x1_ref, h2_ref, mod_ref, wsg_ref, wsu_ref, wsd_ref, nf_ref, y_ref, *, d):
    hb = _unpack_rows(h2_ref[...]).astype(BF16)
    shared = _dot((_silu(_dot(hb, wsg_ref[...])) * _dot(hb, wsu_ref[...])).astype(BF16), wsd_ref[...])
    w8 = w8_ref[...]
    routed = _unpack_rows(yg_ref[0]) * w8[:, 0:1]
    for k in range(1, TOP_K):
        routed = routed + _unpack_rows(yg_ref[k]) * w8[:, k:k + 1]
    g2 = mod_ref[...][:, (N_MOD - 1) * d:N_MOD * d]
    x2 = x1_ref[...] + g2 * (routed + shared)
    y_ref[...] = _rms(x2, nf_ref[...])


def _final(yg3, w8, x1, h2, mod, mod_spec, wts, nf, tf, n_tiles, tile0, out_shape, out_spec):
    d = x1.shape[1]
    rows = lambda i: (i, 0)
    in_specs = [pl.BlockSpec((TOP_K, tf, d // 2), lambda i: (0, tile0 + i, 0)),
                pl.BlockSpec((tf, LANE), rows),
                pl.BlockSpec((tf, d), rows), pl.BlockSpec((tf, d // 2), rows), mod_spec]
    in_specs += [_const_spec(w.shape) for w in wts]
    in_specs += [_const_spec(nf.shape)]
    return pl.pallas_call(
        functools.partial(_final_kernel, d=d), grid=(n_tiles,), in_specs=in_specs,
        out_specs=out_spec, out_shape=out_shape,
        compiler_params=_cp(("arbitrary",)), name="final",
    )(yg3, w8, x1, h2, mod, *wts, nf)


def kernel(x_prompt, x_sample, c_prompt, c_sample, state_pool, state_gla, w_ada, b_ada, norm1, w_in,
           pool_w, pool_scale, gla_w_alpha, gla_b_alpha, gla_norm, w_branch_a, w_branch_b, w_out, norm2,
           w_router, router_bias, w_exp_gate, w_exp_up, w_exp_down, w_sh_gate, w_sh_up, w_sh_down, norm_f):
    depth = w_in.shape[0]
    assert depth == 1, "single-layer trunk"
    bp, seq, d = x_prompt.shape
    bs, t_s, _ = x_sample.shape
    _, _, past, pw = state_pool.shape
    _, _, heads, dk, dv = state_gla.shape
    rank = gla_w_alpha.shape[1]
    n_exp = w_router.shape[2]
    assert past == POOL_HIST - 1 and pw == len(POOL_WINDOWS) * LANE and dv == LANE and rank <= LANE
    dm = _Dims(d, pw, heads, dk, dv)
    n_p, n_s = bp * seq, bs * t_s
    n_all = n_p + n_s

    off = [0]
    for sz in (pw, dm.hk, dm.hk, dm.gw, dm.gw, rank, d, d):
        off.append(off[-1] + sz)
    wi = w_in[0]
    win = (wi[:, :off[5]].astype(BF16), wi[:, off[6]:].astype(BF16),
           jnp.concatenate([wi[:, off[5]:off[6]], jnp.zeros((d, LANE - rank), F32)], axis=1).astype(BF16))
    assert off[5] + 2 * d + LANE == dm.zw
    wal = jnp.concatenate([gla_w_alpha[0], jnp.zeros((LANE - rank, dm.hk), F32)], axis=0).astype(BF16)
    bal = gla_b_alpha[0].reshape(1, dm.hk)
    wr = w_router[0]
    wrh = wr.astype(BF16)
    wrl = (wr - wrh.astype(F32)).astype(BF16)
    pwg = pool_w[0]
    zero = jnp.zeros_like(pwg[0])
    pw_pairs = jnp.stack([jnp.block([[pwg[g], zero], [zero, pwg[g + 1]]]) for g in range(0, len(POOL_WINDOWS), 2)])
    post_w = (pw_pairs.astype(BF16), pool_scale[0].reshape(1, pw), gla_norm[0].reshape(1, dm.gw),
              w_branch_a[0].astype(BF16), w_branch_b[0].astype(BF16), w_out[0].astype(BF16),
              norm2[0].reshape(1, d), wrh, wrl)
    n1 = norm1[0].reshape(1, d)

    mod_p, mod_s = _ada(c_prompt, c_sample, w_ada[0], b_ada[0])
    mod_p = mod_p.reshape(bp, 1, N_MOD * d)

    rbias = router_bias[0].reshape(1, n_exp)
    x1_p, h2_p, idx_p, w8_p, cnt_p, npool_p, ngla_p = _mix_prompt(
        dm, x_prompt.reshape(n_p, d), mod_p, (n1,) + win + (wal, bal) + post_w + (rbias,), bp, seq, n_exp)

    xs2d = x_sample.reshape(bs, t_s * d)
    z_s = _sample_in(dm, xs2d, mod_s, n1, win, bs, t_s)
    mixed_s, o_s, npool_s, ngla_s = _sample_state(
        dm, z_s, jnp.swapaxes(state_pool[0], 0, 1), state_gla[0].reshape(bs, dm.hk, dv), wal, bal, bs, t_s)
    x1_s, h2_s, lg_s = _sample_post(dm, xs2d, mod_s, z_s, mixed_s, o_s, post_w, bs, t_s, n_exp)

    tr = 512 if (n_p % 512 == 0 and n_s % 512 == 0) else LANE
    assert n_p % tr == 0 and n_s % tr == 0 and n_p % bs == 0
    idx_s, w8_s, cnt_s = _route(lg_s, rbias, tr)
    nblk = (n_all * TOP_K) // EXPERT_BLOCK + n_exp
    pos_t, bounds = _plan(jnp.concatenate([idx_p, idx_s], axis=1), cnt_p + cnt_s, tr)
    starts = jnp.concatenate([bounds[0], bounds[1, n_exp - 1:]])

    xs = _dispatch(h2_p, h2_s, pos_t, nblk * EXPERT_BLOCK)
    ys = _moe(starts, xs, w_exp_gate[0], w_exp_up[0], w_exp_down[0])
    yg3 = _combine_gather(ys, pos_t).reshape(TOP_K, n_all, d // 2)

    sh_w = (w_sh_gate[0].astype(BF16), w_sh_up[0].astype(BF16), w_sh_down[0].astype(BF16))
    nf = norm_f.reshape(1, d)
    tf = 512 if seq % 512 == 0 else seq
    n_tp = seq // tf
    y_p = _final(
        yg3, w8_p, x1_p, h2_p, mod_p,
        pl.BlockSpec((None, 1, N_MOD * d), lambda i: (i // n_tp, 0, 0)), sh_w, nf, tf, n_p // tf, 0,
        jax.ShapeDtypeStruct((n_p, d), F32), pl.BlockSpec((tf, d), lambda i: (i, 0)))
    y_s = _final(
        yg3, w8_s, x1_s, h2_s, mod_s,
        _const_spec(mod_s.shape), sh_w, nf, bs, t_s, n_p // bs,
        jax.ShapeDtypeStruct((bs, t_s * d), F32), pl.BlockSpec((bs, d), lambda i: (0, i)))

    return (y_p.reshape(bp, seq, d), y_s.reshape(bs, t_s, d),
            npool_p.reshape(depth, bp, past, pw), ngla_p.reshape(depth, bp, heads, dk, dv),
            jnp.swapaxes(npool_s, 0, 1).reshape(depth, bs, past, pw), ngla_s.reshape(depth, bs, heads, dk, dv))
```

```python
import functools

import jax
import jax.numpy as jnp
from jax import lax
from jax.experimental import pallas as pl
from jax.experimental.pallas import tpu as pltpu
from jax.experimental.pallas import tpu_sc as plsc

F32 = jnp.float32
BF16 = jnp.bfloat16
I32 = jnp.int32

EPS = 1e-6
N_MOD = 6
POOL_WINDOWS = (2, 4, 8, 16)
POOL_HIST = 16
GLA_GATE_TEMP = 16.0
TOP_K = 8
ROUTED_SCALE = 2.5
PAST_LEN = 16384
EXPERT_BLOCK = 128
LANE = 128
GLA_FAST_MAX_DECAY = 40.0
PROJ_PIECE = 512
PROJ_LEAD = 2
VMEM_LIMIT = 56 * 1024 * 1024


def _cp(sem, vmem=VMEM_LIMIT):
    return pltpu.CompilerParams(dimension_semantics=sem, vmem_limit_bytes=vmem)


def _rms(x, g):
    return x * lax.rsqrt(jnp.mean(x * x, axis=-1, keepdims=True) + EPS) * g


def _silu(x):
    return x * jax.nn.sigmoid(x)


def _dot(a, b):
    return jnp.dot(a, b, preferred_element_type=F32)


def _split3(a):
    a0 = a.astype(BF16)
    r1 = a - a0.astype(F32)
    a1 = r1.astype(BF16)
    a2 = (r1 - a1.astype(F32)).astype(BF16)
    return a0, a1, a2


def _mod_parts(mod, d):
    return [mod[:, i * d:(i + 1) * d] for i in range(N_MOD)]


_HI16 = -65536


def _pack_rows(v):
    c = v.shape[1] // 2
    lo = lax.bitcast_convert_type(v[:, :c].astype(BF16).astype(F32), I32)
    hi = lax.bitcast_convert_type(v[:, c:].astype(BF16).astype(F32), I32)
    return (hi & _HI16) | lax.shift_right_logical(lo, 16)


def _unpack_rows(w):
    lo = lax.bitcast_convert_type(lax.shift_left(w, 16), F32)
    hi = lax.bitcast_convert_type(w & _HI16, F32)
    return jnp.concatenate([lo, hi], axis=1)


def _ada_kernel(cp_ref, cs_ref, w_ref, b_ref, op_ref, os_ref):
    w = w_ref[...].astype(BF16)
    for c_ref, o_ref in ((cp_ref, op_ref), (cs_ref, os_ref)):
        o_ref[...] = _dot(_silu(c_ref[...]).astype(BF16), w) + b_ref[...]


def _ada(c_p, c_s, w_ada, b_ada):
    d, cols = w_ada.shape
    bc = d
    whole = lambda c: pl.BlockSpec(c.shape, lambda i: (0, 0))
    col_block = lambda rows: pl.BlockSpec((rows, bc), lambda i: (0, i))
    return pl.pallas_call(
        _ada_kernel,
        grid=(cols // bc,),
        in_specs=[whole(c_p), whole(c_s), col_block(d), col_block(1)],
        out_specs=[col_block(c_p.shape[0]), col_block(c_s.shape[0])],
        out_shape=[jax.ShapeDtypeStruct((c_p.shape[0], cols), F32), jax.ShapeDtypeStruct((c_s.shape[0], cols), F32)],
        compiler_params=_cp(("arbitrary",)),
        name="ada",
    )(c_p, c_s, w_ada, b_ada.reshape(1, cols))


class _Dims:
    def __init__(self, d, pw, heads, dk, dv):
        self.d, self.pw, self.h, self.dk, self.dv = d, pw, heads, dk, dv
        self.hk, self.gw = heads * dk, heads * dv
        self.q0 = pw
        self.k0 = self.q0 + self.hk
        self.v0 = self.k0 + self.hk
        self.go0 = self.v0 + self.gw
        self.ga0 = self.go0 + self.gw
        self.gb0 = self.ga0 + d
        self.al0 = self.gb0 + d
        self.zw = self.al0 + LANE


def _log_decay(z_al, wal_ref, bal_ref):
    xal = _dot(z_al.astype(BF16), wal_ref[...]) + bal_ref[...]
    return jax.nn.log_sigmoid(xal) / GLA_GATE_TEMP


def _pool_project(mixed_groups, pw_ref, ps_ref):
    ys = [_dot(jnp.concatenate(mixed_groups[2 * p:2 * p + 2], axis=1).astype(BF16), pw_ref[p])
          for p in range(len(mixed_groups) // 2)]
    return jnp.concatenate(ys, axis=1) * ps_ref[...]


def _post_mix(dm, x, mod, ya, o, go, ga, gb, gn_ref, wa_ref, wb_ref, wo_ref, n2_ref, wrh_ref, wrl_ref,
              filler=lambda: None):
    _, _, g1, sh2, sc2, _ = _mod_parts(mod, dm.d)
    parts = []
    for h in range(dm.h):
        oh = o[:, h * dm.dv:(h + 1) * dm.dv]
        parts.append(oh * lax.rsqrt(jnp.mean(oh * oh, axis=-1, keepdims=True) + EPS))
    yb = jnp.concatenate(parts, axis=1) * gn_ref[...] * _silu(go)
    filler()
    m = (jax.nn.sigmoid(ga) * _dot(ya.astype(BF16), wa_ref[...])
         + jax.nn.sigmoid(gb) * _dot(yb.astype(BF16), wb_ref[...]))
    filler()
    x1 = x + g1 * _dot(m.astype(BF16), wo_ref[...])
    h2 = _rms(x1, n2_ref[...]) * (1 + sc2) + sh2
    hi = h2.astype(BF16)
    lo = (h2 - hi.astype(F32)).astype(BF16)
    filler()
    logits = _dot(hi, wrh_ref[...]) + (_dot(hi, wrl_ref[...]) + _dot(lo, wrh_ref[...]))
    filler()
    return x1, _pack_rows(h2), logits


def _topk_rows(logits, bias):
    tr, n_exp = logits.shape
    s = jax.nn.sigmoid(logits)
    cur = s + bias
    lane = lax.broadcasted_iota(I32, (1, n_exp), 1).astype(F32)
    slot = lax.broadcasted_iota(I32, (1, LANE), 1)
    idx8 = jnp.zeros((tr, LANE), F32)
    w8 = jnp.zeros((tr, LANE), F32)
    wsum = jnp.zeros((tr, 1), F32)
    cur0 = cur
    for k in range(TOP_K):
        m = jnp.max(cur, axis=-1, keepdims=True)
        am = jnp.min(jnp.where(cur == m, lane, float(n_exp)), axis=-1, keepdims=True)
        oh = lane == am
        sk = jnp.sum(jnp.where(oh, s, 0.0), axis=-1, keepdims=True)
        cur = jnp.where(oh, -jnp.inf, cur)
        idx8 = jnp.where(slot == k, am, idx8)
        w8 = jnp.where(slot == k, sk, w8)
        wsum = wsum + sk
    picked = jnp.where((cur == -jnp.inf) & (cur0 > -jnp.inf), 1.0, 0.0)
    return idx8.T[0:TOP_K, :].astype(I32), w8 / wsum * ROUTED_SCALE, picked.astype(BF16)


def _count_members(picked):
    return _dot(jnp.ones((8, picked.shape[0]), BF16), picked)


def _in_project(dm, x, mod, n1_ref, win_refs, z_ref):
    h = (_rms(x, n1_ref[...]) * (1 + mod[:, dm.d:2 * dm.d]) + mod[:, 0:dm.d]).astype(BF16)
    starts = [0]
    for w_ref in win_refs:
        starts.append(starts[-1] + w_ref.shape[1])
    for i in (0, 2, 1):
        z_ref[:, starts[i]:starts[i + 1]] = _dot(h, win_refs[i][...])


def _mixp_kernel(x_ref, mod_ref, n1_ref, wia_ref, wig_ref, wil_ref, wal_ref, bal_ref, pw_ref, ps_ref, gn_ref,
                 wa_ref, wb_ref, wo_ref, n2_ref, wrh_ref, wrl_ref, rb_ref,
                 x1_ref, h2_ref, idx_ref, w8_ref, cnt_ref, np_ref, ng_ref,
                 za_ref, zb_ref, xk_ref, mk_ref, hn_ref, lg_ref, uext_ref, s_ref, b_ref, oi_ref,
                 *, dm, tt, n_t, tiles):
    n = pl.program_id(0)
    j = jnp.clip(n - 1, 0, tiles - 1) % n_t

    @pl.when(n == 0)
    def _():
        za_ref[...] = jnp.zeros(za_ref.shape, F32)
        zb_ref[...] = jnp.zeros(zb_ref.shape, F32)
        xk_ref[...] = jnp.zeros(xk_ref.shape, F32)
        mk_ref[...] = jnp.zeros(mk_ref.shape, F32)
        lg_ref[...] = jnp.zeros(lg_ref.shape, F32)
        cnt_ref[...] = jnp.zeros(cnt_ref.shape, F32)

    def route_previous():
        idx_ref[...], w8_ref[...], picked = _topk_rows(lg_ref[...], rb_ref[...])
        return picked

    def count_routed(picked):
        cnt_ref[...] += jnp.where(n >= 2, _count_members(picked), 0.0)

    @pl.when((j == 0) & (n <= tiles))
    def _():
        uext_ref[0:POOL_HIST, :] = jnp.zeros((POOL_HIST, dm.pw), F32)
        s_ref[...] = jnp.zeros(s_ref.shape, F32)

    @pl.when((n % 2 == 0) & (n <= tiles))
    def _():
        _mixp_region(za_ref, zb_ref, x_ref, mod_ref, n1_ref, wia_ref, wig_ref, wil_ref, wal_ref, bal_ref, pw_ref,
                     ps_ref, gn_ref, wa_ref, wb_ref, wo_ref, n2_ref, wrh_ref, wrl_ref, x1_ref, h2_ref, lg_ref,
                     xk_ref, mk_ref, hn_ref, uext_ref, s_ref, b_ref, oi_ref, dm=dm, tt=tt, j=j,
                     route_previous=route_previous, count_routed=count_routed)

    @pl.when((n % 2 == 1) & (n <= tiles))
    def _():
        _mixp_region(zb_ref, za_ref, x_ref, mod_ref, n1_ref, wia_ref, wig_ref, wil_ref, wal_ref, bal_ref, pw_ref,
                     ps_ref, gn_ref, wa_ref, wb_ref, wo_ref, n2_ref, wrh_ref, wrl_ref, x1_ref, h2_ref, lg_ref,
                     xk_ref, mk_ref, hn_ref, uext_ref, s_ref, b_ref, oi_ref, dm=dm, tt=tt, j=j,
                     route_previous=route_previous, count_routed=count_routed)

    @pl.when(n > tiles)
    def _():
        count_routed(route_previous())

    xk_ref[...] = x_ref[...]
    mk_ref[...] = mod_ref[...]

    @pl.when((j == n_t - 1) & (n <= tiles))
    def _():
        np_ref[...] = uext_ref[tt + 1:tt + POOL_HIST, :]
        ng_ref[...] = s_ref[...]


def _mixp_region(zp_ref, z_ref, x_ref, mod_ref, n1_ref, wia_ref, wig_ref, wil_ref, wal_ref, bal_ref, pw_ref,
                 ps_ref, gn_ref, wa_ref, wb_ref, wo_ref, n2_ref, wrh_ref, wrl_ref, x1_ref, h2_ref, lg_ref,
                 xk_ref, mk_ref, hn_ref, uext_ref, s_ref, b_ref, oi_ref, *, dm, tt, j, route_previous, count_routed):
    picked = route_previous()
    scale = dm.dk ** -0.5
    lane = lax.broadcasted_iota(I32, (1, dm.hk), 1)
    head_masks = [(lane >= h * dm.dk) & (lane < (h + 1) * dm.dk) for h in range(dm.h)]
    causal = lax.broadcasted_iota(I32, (tt, tt), 0) >= lax.broadcasted_iota(I32, (tt, tt), 1)

    def pool_branch():
        pos = j * tt + lax.broadcasted_iota(I32, (tt, 1), 0)
        gw = dm.pw // len(POOL_WINDOWS)
        mixed = []
        for g, w in enumerate(POOL_WINDOWS):
            c0 = g * gw
            cur = uext_ref[POOL_HIST:POOL_HIST + tt, c0:c0 + gw]
            acc = cur
            for i in range(1, w):
                acc = acc + uext_ref[POOL_HIST - i:POOL_HIST - i + tt, c0:c0 + gw]
            cnt = jnp.minimum(w, pos + 1).astype(F32)
            mixed.append(acc / cnt - cur)
        return _pool_project(mixed, pw_ref, ps_ref)

    def scaled_queries(bc):
        return (z_ref[:, dm.q0:dm.q0 + dm.hk] * scale) * jnp.exp(bc)

    def inter_chunk(qe):
        s_rows = lax.broadcasted_iota(I32, (dm.hk, dm.gw), 0) // dm.dk
        s_cols = lax.broadcasted_iota(I32, (dm.hk, dm.gw), 1) // dm.dv
        s_all = jnp.where(s_rows == s_cols, jnp.concatenate([s_ref[...]] * dm.h, axis=1), 0.0)
        return _dot(qe.astype(BF16), s_all.astype(BF16))

    def finish(ya, o, filler=lambda: None):
        x1, h2, logits = _post_mix(
            dm, xk_ref[...], mk_ref[...], ya, o, z_ref[:, dm.go0:dm.go0 + dm.gw],
            z_ref[:, dm.ga0:dm.ga0 + dm.d], z_ref[:, dm.gb0:dm.gb0 + dm.d],
            gn_ref, wa_ref, wb_ref, wo_ref, n2_ref, wrh_ref, wrl_ref, filler)
        x1_ref[...] = x1
        h2_ref[...] = h2
        lg_ref[...] = logits

    mod_in = mod_ref[...]
    hn_ref[...] = (_rms(x_ref[...], n1_ref[...]) * (1 + mod_in[:, dm.d:2 * dm.d]) + mod_in[:, 0:dm.d]).astype(BF16)
    pieces = iter([(w_ref, a, min(a + PROJ_PIECE, w_ref.shape[1]), c0)
                   for w_ref, c0 in ((wia_ref, 0), (wil_ref, dm.al0), (wig_ref, dm.ga0))
                   for a in range(0, w_ref.shape[1], PROJ_PIECE)])

    def project_piece():
        piece = next(pieces, None)
        if piece is not None:
            w_ref, a, b, c0 = piece
            zp_ref[:, c0 + a:c0 + b] = _dot(hn_ref[...], w_ref[:, a:b])
        return piece is not None

    for _ in range(PROJ_LEAD):
        project_piece()
    uext_ref[POOL_HIST:POOL_HIST + tt, :] = z_ref[:, 0:dm.pw]
    ya = pool_branch()
    project_piece()
    la = _log_decay(z_ref[:, dm.al0:dm.al0 + LANE], wal_ref, bal_ref)
    tri = jnp.where(causal, 1.0, 0.0).astype(BF16)
    a0, a1, a2 = _split3(la)
    bc = _dot(tri, a0) + (_dot(tri, a1) + _dot(tri, a2))
    b_ref[...] = bc
    project_piece()
    qe = scaled_queries(bc)
    k = z_ref[:, dm.k0:dm.k0 + dm.hk]
    vb = z_ref[:, dm.v0:dm.v0 + dm.gw].astype(BF16)
    blast = bc[tt - 1:tt, :]
    fast = jnp.max(-blast) <= GLA_FAST_MAX_DECAY
    ke = (k * jnp.exp(-bc)).astype(BF16)
    intra = []
    for h in range(dm.h):
        qh = jnp.where(head_masks[h], qe, 0.0).astype(BF16)
        sc = lax.dot_general(qh, ke, (((1,), (1,)), ((), ())), preferred_element_type=F32)
        sc = jnp.where(causal, sc, 0.0).astype(BF16)
        intra.append(_dot(sc, vb[:, h * dm.dv:(h + 1) * dm.dv]))
    project_piece()
    filler_calls = iter(range(4))

    def filler():
        if next(filler_calls) == 1:
            count_routed(picked)
        project_piece()

    finish(ya, inter_chunk(qe) + jnp.concatenate(intra, axis=1), filler)
    while project_piece():
        pass

    @pl.when(jnp.logical_not(fast))
    def _():
        rows = lax.broadcasted_iota(I32, (tt, 1), 0)
        for h in range(dm.h):
            def body(tb, carry, h=h):
                base = pl.multiple_of(tb * 8, 8)
                q8 = z_ref[pl.ds(base, 8), dm.q0:dm.q0 + dm.hk] * scale
                b8 = b_ref[pl.ds(base, 8), :]
                vh = z_ref[:, dm.v0 + h * dm.dv:dm.v0 + (h + 1) * dm.dv]
                out_rows = []
                for r in range(8):
                    keep = (rows <= base + r) & head_masks[h]
                    dec = jnp.exp(jnp.where(keep, b8[r:r + 1, :] - b_ref[...], -jnp.inf))
                    wgt = (q8[r:r + 1, :] * dec) * z_ref[:, dm.k0:dm.k0 + dm.hk]
                    s_col = jnp.sum(wgt, axis=-1, keepdims=True)
                    out_rows.append(jnp.sum(s_col * vh, axis=0, keepdims=True))
                oi_ref[pl.ds(base, 8), h * dm.dv:(h + 1) * dm.dv] = jnp.concatenate(out_rows, axis=0)
                return carry
            lax.fori_loop(0, tt // 8, body, 0)
        finish(pool_branch(), inter_chunk(scaled_queries(b_ref[...])) + oi_ref[...])

    bc = b_ref[...]
    blast = bc[tt - 1:tt, :]
    kd = z_ref[:, dm.k0:dm.k0 + dm.hk] * jnp.exp(blast - bc)
    u_all = _dot(kd.T.astype(BF16), z_ref[:, dm.v0:dm.v0 + dm.gw].astype(BF16))
    dec_col = jnp.broadcast_to(jnp.exp(blast), (dm.hk, dm.hk)).T[:, 0:dm.dv]
    upd = jnp.concatenate(
        [u_all[h * dm.dk:(h + 1) * dm.dk, h * dm.dv:(h + 1) * dm.dv] for h in range(dm.h)], axis=0)
    s_ref[...] = dec_col * s_ref[...] + upd
    uext_ref[0:POOL_HIST, :] = uext_ref[tt:tt + POOL_HIST, :]


def _const_spec(shape):
    nd = len(shape)
    return pl.BlockSpec(shape, lambda *_: (0,) * nd, pipeline_mode=pl.Buffered(1))


def _mix_prompt(dm, x2d, mod3, wts, bsz, seq, n_exp):
    tt = 256 if seq % 256 == 0 else seq
    n_t = seq // tt
    n = bsz * seq
    d = dm.d
    tiles = bsz * n_t
    kern = functools.partial(_mixp_kernel, dm=dm, tt=tt, n_t=n_t, tiles=tiles)
    proj = lambda i: jnp.minimum(i, tiles - 1)
    fin = lambda i: jnp.clip(i - 1, 0, tiles - 1)
    routed = lambda i: jnp.clip(i - 2, 0, tiles - 1)
    row = lambda i: (fin(i), 0)
    in_specs = [pl.BlockSpec((tt, d), lambda i: (proj(i), 0)),
                pl.BlockSpec((None, 1, N_MOD * d), lambda i: (proj(i) // n_t, 0, 0))]
    in_specs += [_const_spec(w.shape) for w in wts]
    out_specs = [pl.BlockSpec((tt, d), row), pl.BlockSpec((tt, d // 2), row),
                 pl.BlockSpec((TOP_K, tt), lambda i: (0, routed(i))), pl.BlockSpec((tt, LANE), lambda i: (routed(i), 0)),
                 pl.BlockSpec((8, n_exp), lambda i: (0, 0)),
                 pl.BlockSpec((None, POOL_HIST - 1, dm.pw), lambda i: (fin(i) // n_t, 0, 0)),
                 pl.BlockSpec((None, dm.hk, dm.dv), lambda i: (fin(i) // n_t, 0, 0))]
    out_shape = [jax.ShapeDtypeStruct((n, d), F32), jax.ShapeDtypeStruct((n, d // 2), I32),
                 jax.ShapeDtypeStruct((TOP_K, n), I32), jax.ShapeDtypeStruct((n, LANE), F32),
                 jax.ShapeDtypeStruct((8, n_exp), F32),
                 jax.ShapeDtypeStruct((bsz, POOL_HIST - 1, dm.pw), F32),
                 jax.ShapeDtypeStruct((bsz, dm.hk, dm.dv), F32)]
    scratch = [pltpu.VMEM((tt, dm.zw), F32), pltpu.VMEM((tt, dm.zw), F32),
               pltpu.VMEM((tt, d), F32), pltpu.VMEM((1, N_MOD * d), F32), pltpu.VMEM((tt, d), BF16),
               pltpu.VMEM((tt, n_exp), F32), pltpu.VMEM((tt + POOL_HIST, dm.pw), F32),
               pltpu.VMEM((dm.hk, dm.dv), F32), pltpu.VMEM((tt, dm.hk), F32), pltpu.VMEM((tt, dm.gw), F32)]
    return pl.pallas_call(
        kern, grid=(tiles + 2,), in_specs=in_specs, out_specs=out_specs, out_shape=out_shape,
        scratch_shapes=scratch, compiler_params=_cp(("arbitrary",)), name="mixp",
    )(x2d, mod3, *wts)


def _sin_kernel(x_ref, mod_ref, n1_ref, wia_ref, wig_ref, wil_ref, z_ref, *, dm):
    _in_project(dm, x_ref[...], mod_ref[...], n1_ref, (wia_ref, wig_ref, wil_ref), z_ref)


def _sample_in(dm, xs2d, mod_s, n1, win, bs, t_s):
    d = dm.d
    return pl.pallas_call(
        functools.partial(_sin_kernel, dm=dm),
        grid=(t_s,),
        in_specs=[pl.BlockSpec((bs, d), lambda t: (0, t)), _const_spec(mod_s.shape), _const_spec(n1.shape)]
        + [_const_spec(w.shape) for w in win],
        out_specs=pl.BlockSpec((bs, dm.zw), lambda t: (t, 0)),
        out_shape=jax.ShapeDtypeStruct((t_s * bs, dm.zw), F32),
        compiler_params=_cp(("arbitrary",)), name="sample_in",
    )(xs2d, mod_s, n1, *win)


def _sst_kernel(z_ref, pool_ref, gs_ref, wal_ref, bal_ref,
                mixed_ref, o_ref, npool_ref, ngs_ref,
                qt_ref, kt_ref, dt_ref, vt_ref, ot_ref, *, dm, bs, t_s, ch, pos0):
    i = pl.program_id(0)
    n_i = pl.num_programs(0)
    scale = dm.dk ** -0.5
    past = POOL_HIST - 1

    @pl.when(i == 0)
    def _():
        for t in range(t_s):
            r0 = t * bs
            la = _log_decay(z_ref[r0:r0 + bs, dm.al0:dm.al0 + LANE], wal_ref, bal_ref)
            dt_ref[t] = jnp.exp(la).T
            qt_ref[t] = (z_ref[r0:r0 + bs, dm.q0:dm.q0 + dm.hk] * scale).T
            kt_ref[t] = z_ref[r0:r0 + bs, dm.k0:dm.k0 + dm.hk].T
            vt_ref[t] = z_ref[r0:r0 + bs, dm.v0:dm.v0 + dm.gw].T
        ot_ref[...] = jnp.zeros(ot_ref.shape, F32)

        def item(s, c0, c1):
            if s >= 0:
                return z_ref[s * bs:(s + 1) * bs, c0:c1]
            return pool_ref[past + s, :, c0:c1]

        gw = dm.pw // len(POOL_WINDOWS)
        for t in range(t_s):
            for g, w in enumerate(POOL_WINDOWS):
                c0, c1 = g * gw, (g + 1) * gw
                cur = item(t, c0, c1)
                acc = cur
                for r in range(1, w):
                    acc = acc + item(t - r, c0, c1)
                cnt = float(min(w, pos0 + t + 1))
                mixed_ref[t * bs:(t + 1) * bs, c0:c1] = acc / cnt - cur
        for r in range(past):
            npool_ref[r] = item(r - past + t_s, 0, dm.pw)

    per_head = dm.dk // ch
    hoff = pl.multiple_of((i // per_head) * dm.dv, dm.dv)
    for p in range(ch):
        hd = i * ch + p
        st = gs_ref[:, p, :].T
        for t in range(t_s):
            st = dt_ref[t, pl.ds(hd, 1), :] * st + kt_ref[t, pl.ds(hd, 1), :] * vt_ref[t, pl.ds(hoff, dm.dv), :]
            ot_ref[t, pl.ds(hoff, dm.dv), :] += qt_ref[t, pl.ds(hd, 1), :] * st
        ngs_ref[:, p, :] = st.T

    @pl.when(i == n_i - 1)
    def _():
        for t in range(t_s):
            for h in range(dm.h):
                o_ref[t * bs:(t + 1) * bs, h * dm.dv:(h + 1) * dm.dv] = ot_ref[t, h * dm.dv:(h + 1) * dm.dv, :].T


def _sample_state(dm, z_s, pool3, gs3, wal, bal, bs, t_s):
    ch = 16
    n_i = dm.hk // ch
    past = POOL_HIST - 1
    kern = functools.partial(_sst_kernel, dm=dm, bs=bs, t_s=t_s, ch=ch, pos0=PAST_LEN)
    full = lambda shape: pl.BlockSpec(shape, lambda i: (0,) * len(shape))
    return pl.pallas_call(
        kern, grid=(n_i,),
        in_specs=[_const_spec(z_s.shape), _const_spec(pool3.shape),
                  pl.BlockSpec((bs, ch, dm.dv), lambda i: (0, i, 0)),
                  _const_spec(wal.shape), _const_spec(bal.shape)],
        out_specs=[full((t_s * bs, dm.pw)), full((t_s * bs, dm.gw)), full((past, bs, dm.pw)),
                   pl.BlockSpec((bs, ch, dm.dv), lambda i: (0, i, 0))],
        out_shape=[jax.ShapeDtypeStruct((t_s * bs, dm.pw), F32), jax.ShapeDtypeStruct((t_s * bs, dm.gw), F32),
                   jax.ShapeDtypeStruct((past, bs, dm.pw), F32), jax.ShapeDtypeStruct(gs3.shape, F32)],
        scratch_shapes=[pltpu.VMEM((t_s, dm.hk, bs), F32), pltpu.VMEM((t_s, dm.hk, bs), F32),
                        pltpu.VMEM((t_s, dm.hk, bs), F32), pltpu.VMEM((t_s, dm.gw, bs), F32),
                        pltpu.VMEM((t_s, dm.gw, bs), F32)],
        compiler_params=_cp(("arbitrary",)), name="sample_state",
    )(z_s, pool3, gs3, wal, bal)


def _spost_kernel(x_ref, mod_ref, z_ref, mixed_ref, o_ref, pw_ref, ps_ref, gn_ref,
                  wa_ref, wb_ref, wo_ref, n2_ref, wrh_ref, wrl_ref,
                  x1_ref, h2_ref, lg_ref, *, dm):
    gw = dm.pw // len(POOL_WINDOWS)
    ya = _pool_project([mixed_ref[:, g * gw:(g + 1) * gw] for g in range(len(POOL_WINDOWS))], pw_ref, ps_ref)
    x1, h2, logits = _post_mix(
        dm, x_ref[...], mod_ref[...], ya, o_ref[...],
        z_ref[:, dm.go0:dm.go0 + dm.gw], z_ref[:, dm.ga0:dm.ga0 + dm.d], z_ref[:, dm.gb0:dm.gb0 + dm.d],
        gn_ref, wa_ref, wb_ref, wo_ref, n2_ref, wrh_ref, wrl_ref)
    x1_ref[...] = x1
    h2_ref[...] = h2
    lg_ref[...] = logits


def _sample_post(dm, xs2d, mod_s, z_s, mixed_s, o_s, wts, bs, t_s, n_exp):
    d = dm.d
    rows = lambda t: (t, 0)
    in_specs = [pl.BlockSpec((bs, d), lambda t: (0, t)), _const_spec(mod_s.shape),
                pl.BlockSpec((bs, dm.zw), rows), pl.BlockSpec((bs, dm.pw), rows), pl.BlockSpec((bs, dm.gw), rows)]
    in_specs += [_const_spec(w.shape) for w in wts]
    return pl.pallas_call(
        functools.partial(_spost_kernel, dm=dm), grid=(t_s,), in_specs=in_specs,
        out_specs=[pl.BlockSpec((bs, d), rows), pl.BlockSpec((bs, d // 2), rows), pl.BlockSpec((bs, n_exp), rows)],
        out_shape=[jax.ShapeDtypeStruct((t_s * bs, d), F32), jax.ShapeDtypeStruct((t_s * bs, d // 2), I32),
                   jax.ShapeDtypeStruct((t_s * bs, n_exp), F32)],
        compiler_params=_cp(("arbitrary",)), name="sample_post",
    )(xs2d, mod_s, z_s, mixed_s, o_s, *wts)


def _route_kernel(lg_ref, bias_ref, idx_ref, w_ref, cnt_ref):
    i = pl.program_id(0)
    idx_ref[...], w_ref[...], picked = _topk_rows(lg_ref[...], bias_ref[...])

    @pl.when(i == 0)
    def _():
        cnt_ref[...] = jnp.zeros(cnt_ref.shape, F32)

    cnt_ref[...] += _count_members(picked)


def _route(logits, bias, tr):
    n, n_exp = logits.shape
    rows = lambda i: (i, 0)
    return pl.pallas_call(
        _route_kernel, grid=(n // tr,),
        in_specs=[pl.BlockSpec((tr, n_exp), rows), _const_spec((1, n_exp))],
        out_specs=[pl.BlockSpec((TOP_K, tr), lambda i: (0, i)), pl.BlockSpec((tr, LANE), rows),
                   pl.BlockSpec((8, n_exp), lambda i: (0, 0))],
        out_shape=[jax.ShapeDtypeStruct((TOP_K, n), I32), jax.ShapeDtypeStruct((n, LANE), F32),
                   jax.ShapeDtypeStruct((8, n_exp), F32)],
        compiler_params=_cp(("arbitrary",)), name="route",
    )(logits, bias)


def _plan_kernel(idx_ref, cnt_ref, pos_ref, bounds_ref, base_ref, dest_ref, *, n_exp):
    i = pl.program_id(0)
    tr = idx_ref.shape[1]

    @pl.when(i == 0)
    def _():
        lane = lax.broadcasted_iota(I32, (1, n_exp), 1)
        cnt = cnt_ref[...].astype(I32)
        padded = ((cnt + (EXPERT_BLOCK - 1)) // EXPERT_BLOCK) * EXPERT_BLOCK
        pe = padded
        s = 1
        while s < n_exp:
            pe = pe + jnp.where(lane >= s, pltpu.roll(pe, s, 1), 0)
            s *= 2
        row = lax.broadcasted_iota(I32, (8, n_exp), 0)
        bounds_ref[...] = jnp.where(row == 0, pe - padded, pe)
        first = (pe - padded).astype(F32)[0:1, :]
        base_ref[...] = jnp.broadcast_to(first, (LANE, n_exp)).T

    eidx = lax.broadcasted_iota(I32, (n_exp, 1), 0)
    idx = idx_ref[...]
    member = eidx == idx[0:1, :]
    for k in range(1, TOP_K):
        member = member | (eidx == idx[k:k + 1, :])
    mb = jnp.where(member, 1.0, 0.0).astype(BF16)
    before = jnp.where(lax.broadcasted_iota(I32, (tr, tr), 0) < lax.broadcasted_iota(I32, (tr, tr), 1),
                       1.0, 0.0).astype(BF16)
    dest_ref[...] = _dot(mb, before) + base_ref[:, 0:1]
    for b in range(tr // LANE):
        cols = slice(b * LANE, (b + 1) * LANE)
        dest = dest_ref[:, cols]
        rows = [jnp.sum(jnp.where(eidx == idx[k:k + 1, cols], dest, 0.0), axis=0, keepdims=True)
                for k in range(TOP_K)]
        pos_ref[:, cols] = jnp.concatenate(rows, axis=0).astype(I32)
    base_ref[...] += jnp.sum(mb.astype(F32), axis=1, keepdims=True)


def _plan(idx_t, counts, tr):
    n = idx_t.shape[1]
    n_exp = counts.shape[1]
    cols = lambda i: (0, i)
    return pl.pallas_call(
        functools.partial(_plan_kernel, n_exp=n_exp), grid=(n // tr,),
        in_specs=[pl.BlockSpec((TOP_K, tr), cols), _const_spec((8, n_exp))],
        out_specs=[pl.BlockSpec((TOP_K, tr), cols), pl.BlockSpec((8, n_exp), lambda i: (0, 0))],
        out_shape=[jax.ShapeDtypeStruct((TOP_K, n), I32), jax.ShapeDtypeStruct((8, n_exp), I32)],
        scratch_shapes=[pltpu.VMEM((n_exp, LANE), F32), pltpu.VMEM((n_exp, tr), F32)],
        compiler_params=_cp(("arbitrary",)), name="plan",
    )(idx_t, counts)


SC_CORES = 2
SC_SUBCORES = 16
SC_AXES = ("core", "subcore")
SC_WORKERS = SC_CORES * SC_SUBCORES
SC_WINDOW = 128


def _sc_mesh():
    return plsc.VectorSubcoreMesh(core_axis_name=SC_AXES[0], subcore_axis_name=SC_AXES[1],
                                  num_cores=SC_CORES, num_subcores=SC_SUBCORES)


def _sc_worker():
    return lax.axis_index(SC_AXES[1]) * SC_CORES + lax.axis_index(SC_AXES[0])


def _dispatch(h_p, h_s, pos_t, n_rows):
    n_p, c = h_p.shape
    n_s = h_s.shape[0]
    win_p, win_s = n_p // SC_WINDOW, n_s // SC_WINDOW
    assert n_p % (SC_WINDOW * SC_WORKERS) == 0 and n_s % SC_WINDOW == 0 and win_s <= SC_WORKERS

    @functools.partial(
        pl.kernel, out_type=jax.ShapeDtypeStruct((n_rows, c), h_p.dtype), mesh=_sc_mesh(), name="dispatch",
        scratch_types=[pltpu.VMEM((TOP_K, SC_WINDOW), I32), pltpu.VMEM((SC_WINDOW, c), h_p.dtype),
                       pltpu.SemaphoreType.DMA])
    def run(hp_hbm, hs_hbm, pos_hbm, xs_hbm, idx_v, rows_v, sem):
        wid = _sc_worker()

        def scatter_window(h_hbm, row0, tok0):
            pltpu.sync_copy(h_hbm.at[pl.ds(row0, SC_WINDOW)], rows_v)
            pltpu.sync_copy(pos_hbm.at[:, pl.ds(tok0, SC_WINDOW)], idx_v)
            copies = [pltpu.async_copy(rows_v, xs_hbm.at[idx_v.at[k]], sem) for k in range(TOP_K)]
            for cp in copies:
                cp.wait()

        def prompt_window(j, carry):
            row0 = pl.multiple_of((wid * (win_p // SC_WORKERS) + j) * SC_WINDOW, SC_WINDOW)
            scatter_window(hp_hbm, row0, row0)
            return carry
        lax.fori_loop(0, win_p // SC_WORKERS, prompt_window, 0)

        @pl.when(wid < win_s)
        def _():
            row0 = pl.multiple_of(wid * SC_WINDOW, SC_WINDOW)
            scatter_window(hs_hbm, row0, n_p + row0)

    return run(h_p, h_s, pos_t)


def _combine_gather(ys, pos_t):
    k_n = pos_t.shape[0] * pos_t.shape[1]
    c = ys.shape[1]
    win = SC_WINDOW // 2
    per_worker = k_n // (win * SC_WORKERS)
    assert k_n % (win * SC_WORKERS) == 0 and per_worker % 2 == 0
    dma = pltpu.SemaphoreType.DMA

    @functools.partial(
        pl.kernel, out_type=jax.ShapeDtypeStruct((k_n, c), ys.dtype), mesh=_sc_mesh(), name="combine_gather",
        scratch_types=[pltpu.VMEM((win,), I32), pltpu.VMEM((win,), I32),
                       pltpu.VMEM((win, c), ys.dtype), pltpu.VMEM((win, c), ys.dtype), dma, dma, dma, dma])
    def run(ys_hbm, idx_hbm, out_hbm, idx0, idx1, buf0, buf1, gsem0, gsem1, ssem0, ssem1):
        wid = _sc_worker()

        def rows(j):
            return pl.ds(pl.multiple_of((wid * per_worker + j) * win, win), win)

        def gather(j, idx_v, buf, sem):
            pltpu.sync_copy(idx_hbm.at[rows(j)], idx_v)
            pltpu.async_copy(ys_hbm.at[idx_v], buf, sem)

        def gather_wait(idx_v, buf, sem):
            pltpu.make_async_copy(ys_hbm.at[idx_v], buf, sem).wait()

        def store(j, buf, sem):
            pltpu.async_copy(buf, out_hbm.at[rows(j)], sem)

        def store_wait(buf, sem):
            pltpu.make_async_copy(buf, out_hbm.at[rows(0)], sem).wait()

        gather(0, idx0, buf0, gsem0)

        def two_windows(i, carry):
            j = 2 * i

            @pl.when(i > 0)
            def _():
                store_wait(buf1, ssem1)

            gather(j + 1, idx1, buf1, gsem1)
            gather_wait(idx0, buf0, gsem0)
            store(j, buf0, ssem0)

            @pl.when(j + 2 < per_worker)
            def _():
                store_wait(buf0, ssem0)
                gather(j + 2, idx0, buf0, gsem0)

            gather_wait(idx1, buf1, gsem1)
            store(j + 1, buf1, ssem1)
            return carry
        lax.fori_loop(0, per_worker // 2, two_windows, 0)
        store_wait(buf0, ssem0)
        store_wait(buf1, ssem1)

    return run(ys, pos_t.reshape(k_n))


MOE_RING = 16
MOE_WEIGHT_RING = 8


def _moe_kernel(st_ref, wg_ref, wu_ref, wd_ref, xs_ref, ys_ref, xbuf, ybuf, wgf, wuf, wdf, wgb, wub, wdb,
                semx, semy, semw):
    e = pl.program_id(0)
    n_e = pl.num_programs(0)

    def w_copies(ex):
        slot = ex % MOE_WEIGHT_RING
        return [pltpu.make_async_copy(src.at[ex], dst.at[slot], semw.at[slot, i])
                for i, (src, dst) in enumerate(((wg_ref, wgf), (wu_ref, wuf), (wd_ref, wdf)))]

    @pl.when(e == 0)
    def _():
        for ex in range(MOE_WEIGHT_RING - 1):
            @pl.when(ex < n_e)
            def _(ex=ex):
                for cp in w_copies(ex):
                    cp.start()

    @pl.when(e + (MOE_WEIGHT_RING - 1) < n_e)
    def _():
        for cp in w_copies(e + (MOE_WEIGHT_RING - 1)):
            cp.start()
    g0 = st_ref[e] // EXPERT_BLOCK
    nchunks = st_ref[e + 1] // EXPERT_BLOCK - g0
    ntot = st_ref[n_e] // EXPERT_BLOCK

    def rows(g):
        return pl.ds(pl.multiple_of(g * EXPERT_BLOCK, EXPERT_BLOCK), EXPERT_BLOCK)

    def x_copy(g):
        slot = g % MOE_RING
        return pltpu.make_async_copy(xs_ref.at[rows(g)], xbuf.at[slot], semx.at[slot])

    def y_copy(g):
        slot = g % MOE_RING
        return pltpu.make_async_copy(ybuf.at[slot], ys_ref.at[rows(g)], semy.at[slot])

    @pl.when(e == 0)
    def _():
        for g in range(MOE_RING - 1):
            @pl.when(g < ntot)
            def _(g=g):
                x_copy(g).start()

    for cp in w_copies(e):
        cp.wait()
    wslot = e % MOE_WEIGHT_RING
    wgb[...] = wgf[wslot].astype(BF16)
    wub[...] = wuf[wslot].astype(BF16)
    wdb[...] = wdf[wslot].astype(BF16)

    def take(g):
        @pl.when(g + (MOE_RING - 1) < ntot)
        def _():
            x_copy(g + (MOE_RING - 1)).start()

        x_copy(g).wait()

        @pl.when(g >= MOE_RING)
        def _():
            y_copy(g - MOE_RING).wait()

        return xbuf[g % MOE_RING]

    def expert_mlp(x_words):
        x = _unpack_rows(x_words).astype(BF16)
        mid = _silu(_dot(x, wgb[...])) * _dot(x, wub[...])
        return _pack_rows(_dot(mid.astype(BF16), wdb[...]))

    def put(g, y_words):
        ybuf[g % MOE_RING] = y_words
        y_copy(g).start()

    def run_chunks(g, m):
        xs = [take(g + i) for i in range(m)]
        y = expert_mlp(xs[0] if m == 1 else jnp.concatenate(xs, axis=0))
        for i in range(m):
            put(g + i, y[i * EXPERT_BLOCK:(i + 1) * EXPERT_BLOCK])

    def quad(p, carry):
        run_chunks(g0 + 4 * p, 4)
        return carry

    rest = nchunks % 4
    five = (rest == 1) & (nchunks >= 5)
    n_quads = nchunks // 4 - five.astype(I32)
    lax.fori_loop(0, n_quads, quad, 0)
    g_rest = g0 + 4 * n_quads

    @pl.when(five)
    def _():
        run_chunks(g_rest, 5)

    @pl.when(rest >= 2)
    def _():
        run_chunks(g_rest, 2)

    @pl.when((rest == 3) | (nchunks == 1))
    def _():
        run_chunks(g0 + nchunks - 1, 1)

    @pl.when(e == n_e - 1)
    def _():
        for j in range(MOE_RING):
            g = ntot - MOE_RING + j

            @pl.when(g >= 0)
            def _(g=g):
                y_copy(g).wait()


def _moe(starts, xs, wg, wu, wd):
    n_rows, c = xs.shape
    n_exp, d, f = wg.shape
    grid_spec = pltpu.PrefetchScalarGridSpec(
        num_scalar_prefetch=1, grid=(n_exp,),
        in_specs=[pl.BlockSpec(memory_space=pl.ANY)] * 4,
        out_specs=pl.BlockSpec(memory_space=pl.ANY),
        scratch_shapes=[pltpu.VMEM((MOE_RING, EXPERT_BLOCK, c), I32), pltpu.VMEM((MOE_RING, EXPERT_BLOCK, c), I32),
                        pltpu.VMEM((MOE_WEIGHT_RING, d, f), F32), pltpu.VMEM((MOE_WEIGHT_RING, d, f), F32),
                        pltpu.VMEM((MOE_WEIGHT_RING, f, d), F32),
                        pltpu.VMEM((d, f), BF16), pltpu.VMEM((d, f), BF16), pltpu.VMEM((f, d), BF16),
                        pltpu.SemaphoreType.DMA((MOE_RING,)), pltpu.SemaphoreType.DMA((MOE_RING,)),
                        pltpu.SemaphoreType.DMA((MOE_WEIGHT_RING, 3))])
    return pl.pallas_call(
        _moe_kernel, grid_spec=grid_spec, out_shape=jax.ShapeDtypeStruct((n_rows, c), I32),
        compiler_params=_cp(("arbitrary",)), name="moe",
    )(starts, wg, wu, wd, xs)


def _final_kernel(yg_ref, w8_ref, x1_ref, h2_ref, mod_ref, wsg_ref, wsu_ref, wsd_ref, nf_ref, y_ref, *, d):
    hb = _unpack_rows(h2_ref[...]).astype(BF16)
    shared = _dot((_silu(_dot(hb, wsg_ref[...])) * _dot(hb, wsu_ref[...])).astype(BF16), wsd_ref[...])
    w8 = w8_ref[...]
    routed = _unpack_rows(yg_ref[0]) * w8[:, 0:1]
    for k in range(1, TOP_K):
        routed = routed + _unpack_rows(yg_ref[k]) * w8[:, k:k + 1]
    g2 = mod_ref[...][:, (N_MOD - 1) * d:N_MOD * d]
    x2 = x1_ref[...] + g2 * (routed + shared)
    y_ref[...] = _rms(x2, nf_ref[...])


def _final(yg3, w8, x1, h2, mod, mod_spec, wts, nf, tf, n_tiles, tile0, out_shape, out_spec):
    d = x1.shape[1]
    rows = lambda i: (i, 0)
    in_specs = [pl.BlockSpec((TOP_K, tf, d // 2), lambda i: (0, tile0 + i, 0)),
                pl.BlockSpec((tf, LANE), rows),
                pl.BlockSpec((tf, d), rows), pl.BlockSpec((tf, d // 2), rows), mod_spec]
    in_specs += [_const_spec(w.shape) for w in wts]
    in_specs += [_const_spec(nf.shape)]
    return pl.pallas_call(
        functools.partial(_final_kernel, d=d), grid=(n_tiles,), in_specs=in_specs,
        out_specs=out_spec, out_shape=out_shape,
        compiler_params=_cp(("arbitrary",)), name="final",
    )(yg3, w8, x1, h2, mod, *wts, nf)


def kernel(x_prompt, x_sample, c_prompt, c_sample, state_pool, state_gla, w_ada, b_ada, norm1, w_in,
           pool_w, pool_scale, gla_w_alpha, gla_b_alpha, gla_norm, w_branch_a, w_branch_b, w_out, norm2,
           w_router, router_bias, w_exp_gate, w_exp_up, w_exp_down, w_sh_gate, w_sh_up, w_sh_down, norm_f):
    depth = w_in.shape[0]
    assert depth == 1, "single-layer trunk"
    bp, seq, d = x_prompt.shape
    bs, t_s, _ = x_sample.shape
    _, _, past, pw = state_pool.shape
    _, _, heads, dk, dv = state_gla.shape
    rank = gla_w_alpha.shape[1]
    n_exp = w_router.shape[2]
    assert past == POOL_HIST - 1 and pw == len(POOL_WINDOWS) * LANE and dv == LANE and rank <= LANE
    dm = _Dims(d, pw, heads, dk, dv)
    n_p, n_s = bp * seq, bs * t_s
    n_all = n_p + n_s

    off = [0]
    for sz in (pw, dm.hk, dm.hk, dm.gw, dm.gw, rank, d, d):
        off.append(off[-1] + sz)
    wi = w_in[0]
    win = (wi[:, :off[5]].astype(BF16), wi[:, off[6]:].astype(BF16),
           jnp.concatenate([wi[:, off[5]:off[6]], jnp.zeros((d, LANE - rank), F32)], axis=1).astype(BF16))
    assert off[5] + 2 * d + LANE == dm.zw
    wal = jnp.concatenate([gla_w_alpha[0], jnp.zeros((LANE - rank, dm.hk), F32)], axis=0).astype(BF16)
    bal = gla_b_alpha[0].reshape(1, dm.hk)
    wr = w_router[0]
    wrh = wr.astype(BF16)
    wrl = (wr - wrh.astype(F32)).astype(BF16)
    pwg = pool_w[0]
    zero = jnp.zeros_like(pwg[0])
    pw_pairs = jnp.stack([jnp.block([[pwg[g], zero], [zero, pwg[g + 1]]]) for g in range(0, len(POOL_WINDOWS), 2)])
    post_w = (pw_pairs.astype(BF16), pool_scale[0].reshape(1, pw), gla_norm[0].reshape(1, dm.gw),
              w_branch_a[0].astype(BF16), w_branch_b[0].astype(BF16), w_out[0].astype(BF16),
              norm2[0].reshape(1, d), wrh, wrl)
    n1 = norm1[0].reshape(1, d)

    mod_p, mod_s = _ada(c_prompt, c_sample, w_ada[0], b_ada[0])
    mod_p = mod_p.reshape(bp, 1, N_MOD * d)

    rbias = router_bias[0].reshape(1, n_exp)
    x1_p, h2_p, idx_p, w8_p, cnt_p, npool_p, ngla_p = _mix_prompt(
        dm, x_prompt.reshape(n_p, d), mod_p, (n1,) + win + (wal, bal) + post_w + (rbias,), bp, seq, n_exp)

    xs2d = x_sample.reshape(bs, t_s * d)
    z_s = _sample_in(dm, xs2d, mod_s, n1, win, bs, t_s)
    mixed_s, o_s, npool_s, ngla_s = _sample_state(
        dm, z_s, jnp.swapaxes(state_pool[0], 0, 1), state_gla[0].reshape(bs, dm.hk, dv), wal, bal, bs, t_s)
    x1_s, h2_s, lg_s = _sample_post(dm, xs2d, mod_s, z_s, mixed_s, o_s, post_w, bs, t_s, n_exp)

    tr = 512 if (n_p % 512 == 0 and n_s % 512 == 0) else LANE
    assert n_p % tr == 0 and n_s % tr == 0 and n_p % bs == 0
    idx_s, w8_s, cnt_s = _route(lg_s, rbias, tr)
    nblk = (n_all * TOP_K) // EXPERT_BLOCK + n_exp
    pos_t, bounds = _plan(jnp.concatenate([idx_p, idx_s], axis=1), cnt_p + cnt_s, tr)
    starts = jnp.concatenate([bounds[0], bounds[1, n_exp - 1:]])

    xs = _dispatch(h2_p, h2_s, pos_t, nblk * EXPERT_BLOCK)
    ys = _moe(starts, xs, w_exp_gate[0], w_exp_up[0], w_exp_down[0])
    yg3 = _combine_gather(ys, pos_t).reshape(TOP_K, n_all, d // 2)

    sh_w = (w_sh_gate[0].astype(BF16), w_sh_up[0].astype(BF16), w_sh_down[0].astype(BF16))
    nf = norm_f.reshape(1, d)
    tf = 512 if seq % 512 == 0 else seq
    n_tp = seq // tf
    y_p = _final(
        yg3, w8_p, x1_p, h2_p, mod_p,
        pl.BlockSpec((None, 1, N_MOD * d), lambda i: (i // n_tp, 0, 0)), sh_w, nf, tf, n_p // tf, 0,
        jax.ShapeDtypeStruct((n_p, d), F32), pl.BlockSpec((tf, d), lambda i: (i, 0)))
    y_s = _final(
        yg3, w8_s, x1_s, h2_s, mod_s,
        _const_spec(mod_s.shape), sh_w, nf, bs, t_s, n_p // bs,
        jax.ShapeDtypeStruct((bs, t_s * d), F32), pl.BlockSpec((bs, d), lambda i: (0, i)))

    return (y_p.reshape(bp, seq, d), y_s.reshape(bs, t_s, d),
            npool_p.reshape(depth, bp, past, pw), ngla_p.reshape(depth, bp, heads, dk, dv),
            jnp.swapaxes(npool_s, 0, 1).reshape(depth, bs, past, pw), ngla_s.reshape(depth, bs, heads, dk, dv))
```

```python
import functools

import jax
import jax.numpy as jnp
from jax import lax
from jax.experimental import pallas as pl
from jax.experimental.pallas import tpu as pltpu
from jax.experimental.pallas import tpu_sc as plsc

F32 = jnp.float32
BF16 = jnp.bfloat16
I32 = jnp.int32

EPS = 1e-6
N_MOD = 6
POOL_WINDOWS = (2, 4, 8, 16)
POOL_HIST = 16
GLA_GATE_TEMP = 16.0
TOP_K = 8
ROUTED_SCALE = 2.5
PAST_LEN = 16384
EXPERT_BLOCK = 128
LANE = 128
GLA_FAST_MAX_DECAY = 40.0
PROJ_PIECE = 512
PROJ_LEAD = 2
VMEM_LIMIT = 56 * 1024 * 1024


def _cp(sem, vmem=VMEM_LIMIT):
    return pltpu.CompilerParams(dimension_semantics=sem, vmem_limit_bytes=vmem)


def _rms(x, g):
    return x * lax.rsqrt(jnp.mean(x * x, axis=-1, keepdims=True) + EPS) * g


def _silu(x):
    return x * jax.nn.sigmoid(x)


def _dot(a, b):
    return jnp.dot(a, b, preferred_element_type=F32)


def _split3(a):
    a0 = a.astype(BF16)
    r1 = a - a0.astype(F32)
    a1 = r1.astype(BF16)
    a2 = (r1 - a1.astype(F32)).astype(BF16)
    return a0, a1, a2


def _mod_parts(mod, d):
    return [mod[:, i * d:(i + 1) * d] for i in range(N_MOD)]


_HI16 = -65536


def _pack_rows(v):
    c = v.shape[1] // 2
    lo = lax.bitcast_convert_type(v[:, :c].astype(BF16).astype(F32), I32)
    hi = lax.bitcast_convert_type(v[:, c:].astype(BF16).astype(F32), I32)
    return (hi & _HI16) | lax.shift_right_logical(lo, 16)


def _unpack_rows(w):
    lo = lax.bitcast_convert_type(lax.shift_left(w, 16), F32)
    hi = lax.bitcast_convert_type(w & _HI16, F32)
    return jnp.concatenate([lo, hi], axis=1)


def _ada_kernel(cp_ref, cs_ref, w_ref, b_ref, op_ref, os_ref):
    w = w_ref[...].astype(BF16)
    for c_ref, o_ref in ((cp_ref, op_ref), (cs_ref, os_ref)):
        o_ref[...] = _dot(_silu(c_ref[...]).astype(BF16), w) + b_ref[...]


def _ada(c_p, c_s, w_ada, b_ada):
    d, cols = w_ada.shape
    bc = d
    whole = lambda c: pl.BlockSpec(c.shape, lambda i: (0, 0))
    col_block = lambda rows: pl.BlockSpec((rows, bc), lambda i: (0, i))
    return pl.pallas_call(
        _ada_kernel,
        grid=(cols // bc,),
        in_specs=[whole(c_p), whole(c_s), col_block(d), col_block(1)],
        out_specs=[col_block(c_p.shape[0]), col_block(c_s.shape[0])],
        out_shape=[jax.ShapeDtypeStruct((c_p.shape[0], cols), F32), jax.ShapeDtypeStruct((c_s.shape[0], cols), F32)],
        compiler_params=_cp(("arbitrary",)),
        name="ada",
    )(c_p, c_s, w_ada, b_ada.reshape(1, cols))


class _Dims:
    def __init__(self, d, pw, heads, dk, dv):
        self.d, self.pw, self.h, self.dk, self.dv = d, pw, heads, dk, dv
        self.hk, self.gw = heads * dk, heads * dv
        self.q0 = pw
        self.k0 = self.q0 + self.hk
        self.v0 = self.k0 + self.hk
        self.go0 = self.v0 + self.gw
        self.ga0 = self.go0 + self.gw
        self.gb0 = self.ga0 + d
        self.al0 = self.gb0 + d
        self.zw = self.al0 + LANE


def _log_decay(z_al, wal_ref, bal_ref):
    xal = _dot(z_al.astype(BF16), wal_ref[...]) + bal_ref[...]
    return jax.nn.log_sigmoid(xal) / GLA_GATE_TEMP


def _pool_project(mixed_groups, pw_ref, ps_ref):
    ys = [_dot(jnp.concatenate(mixed_groups[2 * p:2 * p + 2], axis=1).astype(BF16), pw_ref[p])
          for p in range(len(mixed_groups) // 2)]
    return jnp.concatenate(ys, axis=1) * ps_ref[...]


def _post_mix(dm, x, mod, ya, o, go, ga, gb, gn_ref, wa_ref, wb_ref, wo_ref, n2_ref, wrh_ref, wrl_ref,
              filler=lambda: None):
    _, _, g1, sh2, sc2, _ = _mod_parts(mod, dm.d)
    parts = []
    for h in range(dm.h):
        oh = o[:, h * dm.dv:(h + 1) * dm.dv]
        parts.append(oh * lax.rsqrt(jnp.mean(oh * oh, axis=-1, keepdims=True) + EPS))
    yb = jnp.concatenate(parts, axis=1) * gn_ref[...] * _silu(go)
    filler()
    m = (jax.nn.sigmoid(ga) * _dot(ya.astype(BF16), wa_ref[...])
         + jax.nn.sigmoid(gb) * _dot(yb.astype(BF16), wb_ref[...]))
    filler()
    x1 = x + g1 * _dot(m.astype(BF16), wo_ref[...])
    h2 = _rms(x1, n2_ref[...]) * (1 + sc2) + sh2
    hi = h2.astype(BF16)
    lo = (h2 - hi.astype(F32)).astype(BF16)
    filler()
    logits = _dot(hi, wrh_ref[...]) + (_dot(hi, wrl_ref[...]) + _dot(lo, wrh_ref[...]))
    filler()
    return x1, _pack_rows(h2), logits


def _topk_rows(logits, bias):
    tr, n_exp = logits.shape
    s = jax.nn.sigmoid(logits)
    cur = s + bias
    lane = lax.broadcasted_iota(I32, (1, n_exp), 1).astype(F32)
    slot = lax.broadcasted_iota(I32, (1, LANE), 1)
    idx8 = jnp.zeros((tr, LANE), F32)
    w8 = jnp.zeros((tr, LANE), F32)
    wsum = jnp.zeros((tr, 1), F32)
    cur0 = cur
    for k in range(TOP_K):
        m = jnp.max(cur, axis=-1, keepdims=True)
        am = jnp.min(jnp.where(cur == m, lane, float(n_exp)), axis=-1, keepdims=True)
        oh = lane == am
        sk = jnp.sum(jnp.where(oh, s, 0.0), axis=-1, keepdims=True)
        cur = jnp.where(oh, -jnp.inf, cur)
        idx8 = jnp.where(slot == k, am, idx8)
        w8 = jnp.where(slot == k, sk, w8)
        wsum = wsum + sk
    picked = jnp.where((cur == -jnp.inf) & (cur0 > -jnp.inf), 1.0, 0.0)
    return idx8.T[0:TOP_K, :].astype(I32), w8 / wsum * ROUTED_SCALE, picked.astype(BF16)


def _count_members(picked):
    return _dot(jnp.ones((8, picked.shape[0]), BF16), picked)


def _in_project(dm, x, mod, n1_ref, win_refs, z_ref):
    h = (_rms(x, n1_ref[...]) * (1 + mod[:, dm.d:2 * dm.d]) + mod[:, 0:dm.d]).astype(BF16)
    starts = [0]
    for w_ref in win_refs:
        starts.append(starts[-1] + w_ref.shape[1])
    for i in (0, 2, 1):
        z_ref[:, starts[i]:starts[i + 1]] = _dot(h, win_refs[i][...])


def _mixp_kernel(x_ref, mod_ref, n1_ref, wia_ref, wig_ref, wil_ref, wal_ref, bal_ref, pw_ref, ps_ref, gn_ref,
                 wa_ref, wb_ref, wo_ref, n2_ref, wrh_ref, wrl_ref, rb_ref,
                 x1_ref, h2_ref, idx_ref, w8_ref, cnt_ref, np_ref, ng_ref,
                 za_ref, zb_ref, xk_ref, mk_ref, hn_ref, lg_ref, uext_ref, s_ref, b_ref, oi_ref,
                 *, dm, tt, n_t, tiles):
    n = pl.program_id(0)
    j = jnp.clip(n - 1, 0, tiles - 1) % n_t

    @pl.when(n == 0)
    def _():
        za_ref[...] = jnp.zeros(za_ref.shape, F32)
        zb_ref[...] = jnp.zeros(zb_ref.shape, F32)
        xk_ref[...] = jnp.zeros(xk_ref.shape, F32)
        mk_ref[...] = jnp.zeros(mk_ref.shape, F32)
        lg_ref[...] = jnp.zeros(lg_ref.shape, F32)
        cnt_ref[...] = jnp.zeros(cnt_ref.shape, F32)

    def route_previous():
        rows = LANE if tt % LANE == 0 else tt
        picked = []
        for r0 in range(0, tt, rows):
            idx_ref[:, r0:r0 + rows], w8_ref[r0:r0 + rows, :], pk = _topk_rows(lg_ref[r0:r0 + rows, :], rb_ref[...])
            picked.append(pk)
        return picked

    def count_routed(picked):
        cnt_ref[...] += jnp.where(n >= 2, _count_members(picked), 0.0)

    @pl.when((j == 0) & (n <= tiles))
    def _():
        uext_ref[0:POOL_HIST, :] = jnp.zeros((POOL_HIST, dm.pw), F32)
        s_ref[...] = jnp.zeros(s_ref.shape, F32)

    @pl.when((n % 2 == 0) & (n <= tiles))
    def _():
        _mixp_region(za_ref, zb_ref, x_ref, mod_ref, n1_ref, wia_ref, wig_ref, wil_ref, wal_ref, bal_ref, pw_ref,
                     ps_ref, gn_ref, wa_ref, wb_ref, wo_ref, n2_ref, wrh_ref, wrl_ref, x1_ref, h2_ref, lg_ref,
                     xk_ref, mk_ref, hn_ref, uext_ref, s_ref, b_ref, oi_ref, dm=dm, tt=tt, j=j,
                     route_previous=route_previous, count_routed=count_routed)

    @pl.when((n % 2 == 1) & (n <= tiles))
    def _():
        _mixp_region(zb_ref, za_ref, x_ref, mod_ref, n1_ref, wia_ref, wig_ref, wil_ref, wal_ref, bal_ref, pw_ref,
                     ps_ref, gn_ref, wa_ref, wb_ref, wo_ref, n2_ref, wrh_ref, wrl_ref, x1_ref, h2_ref, lg_ref,
                     xk_ref, mk_ref, hn_ref, uext_ref, s_ref, b_ref, oi_ref, dm=dm, tt=tt, j=j,
                     route_previous=route_previous, count_routed=count_routed)

    @pl.when(n > tiles)
    def _():
        for pk in route_previous():
            count_routed(pk)

    xk_ref[...] = x_ref[...]
    mk_ref[...] = mod_ref[...]

    @pl.when((j == n_t - 1) & (n <= tiles))
    def _():
        np_ref[...] = uext_ref[tt + 1:tt + POOL_HIST, :]
        ng_ref[...] = s_ref[...]


def _mixp_region(zp_ref, z_ref, x_ref, mod_ref, n1_ref, wia_ref, wig_ref, wil_ref, wal_ref, bal_ref, pw_ref,
                 ps_ref, gn_ref, wa_ref, wb_ref, wo_ref, n2_ref, wrh_ref, wrl_ref, x1_ref, h2_ref, lg_ref,
                 xk_ref, mk_ref, hn_ref, uext_ref, s_ref, b_ref, oi_ref, *, dm, tt, j, route_previous, count_routed):
    picked = route_previous()
    scale = dm.dk ** -0.5
    lane = lax.broadcasted_iota(I32, (1, dm.hk), 1)
    head_masks = [(lane >= h * dm.dk) & (lane < (h + 1) * dm.dk) for h in range(dm.h)]
    causal = lax.broadcasted_iota(I32, (tt, tt), 0) >= lax.broadcasted_iota(I32, (tt, tt), 1)

    def pool_branch():
        pos = j * tt + lax.broadcasted_iota(I32, (tt, 1), 0)
        gw = dm.pw // len(POOL_WINDOWS)
        mixed = []
        for g, w in enumerate(POOL_WINDOWS):
            c0 = g * gw
            cur = uext_ref[POOL_HIST:POOL_HIST + tt, c0:c0 + gw]
            acc = cur
            for i in range(1, w):
                acc = acc + uext_ref[POOL_HIST - i:POOL_HIST - i + tt, c0:c0 + gw]
            cnt = jnp.minimum(w, pos + 1).astype(F32)
            mixed.append(acc / cnt - cur)
        return _pool_project(mixed, pw_ref, ps_ref)

    def scaled_queries(bc):
        return (z_ref[:, dm.q0:dm.q0 + dm.hk] * scale) * jnp.exp(bc)

    def inter_chunk(qe):
        s_rows = lax.broadcasted_iota(I32, (dm.hk, dm.gw), 0) // dm.dk
        s_cols = lax.broadcasted_iota(I32, (dm.hk, dm.gw), 1) // dm.dv
        s_all = jnp.where(s_rows == s_cols, jnp.concatenate([s_ref[...]] * dm.h, axis=1), 0.0)
        return _dot(qe.astype(BF16), s_all.astype(BF16))

    def finish(ya, o, filler=lambda: None):
        x1, h2, logits = _post_mix(
            dm, xk_ref[...], mk_ref[...], ya, o, z_ref[:, dm.go0:dm.go0 + dm.gw],
            z_ref[:, dm.ga0:dm.ga0 + dm.d], z_ref[:, dm.gb0:dm.gb0 + dm.d],
            gn_ref, wa_ref, wb_ref, wo_ref, n2_ref, wrh_ref, wrl_ref, filler)
        x1_ref[...] = x1
        h2_ref[...] = h2
        lg_ref[...] = logits

    mod_in = mod_ref[...]
    hn_ref[...] = (_rms(x_ref[...], n1_ref[...]) * (1 + mod_in[:, dm.d:2 * dm.d]) + mod_in[:, 0:dm.d]).astype(BF16)
    pieces = iter([(w_ref, a, min(a + PROJ_PIECE, w_ref.shape[1]), c0)
                   for w_ref, c0 in ((wia_ref, 0), (wil_ref, dm.al0), (wig_ref, dm.ga0))
                   for a in range(0, w_ref.shape[1], PROJ_PIECE)])

    def project_piece():
        piece = next(pieces, None)
        if piece is not None:
            w_ref, a, b, c0 = piece
            zp_ref[:, c0 + a:c0 + b] = _dot(hn_ref[...], w_ref[:, a:b])
        return piece is not None

    for _ in range(PROJ_LEAD):
        project_piece()
    uext_ref[POOL_HIST:POOL_HIST + tt, :] = z_ref[:, 0:dm.pw]
    ya = pool_branch()
    project_piece()
    la = _log_decay(z_ref[:, dm.al0:dm.al0 + LANE], wal_ref, bal_ref)
    tri = jnp.where(causal, 1.0, 0.0).astype(BF16)
    a0, a1, a2 = _split3(la)
    bc = _dot(tri, a0) + (_dot(tri, a1) + _dot(tri, a2))
    b_ref[...] = bc
    project_piece()
    qe = scaled_queries(bc)
    k = z_ref[:, dm.k0:dm.k0 + dm.hk]
    vb = z_ref[:, dm.v0:dm.v0 + dm.gw].astype(BF16)
    blast = bc[tt - 1:tt, :]
    fast = jnp.max(-blast) <= GLA_FAST_MAX_DECAY
    ke = (k * jnp.exp(-bc)).astype(BF16)
    intra = []
    for h in range(dm.h):
        qh = jnp.where(head_masks[h], qe, 0.0).astype(BF16)
        sc = lax.dot_general(qh, ke, (((1,), (1,)), ((), ())), preferred_element_type=F32)
        sc = jnp.where(causal, sc, 0.0).astype(BF16)
        intra.append(_dot(sc, vb[:, h * dm.dv:(h + 1) * dm.dv]))
    project_piece()
    filler_calls = iter(range(4))

    def filler():
        call = next(filler_calls)
        for g, pk in enumerate(picked):
            if call == min(1 + g, 2):
                count_routed(pk)
        project_piece()

    finish(ya, inter_chunk(qe) + jnp.concatenate(intra, axis=1), filler)
    while project_piece():
        pass

    @pl.when(jnp.logical_not(fast))
    def _():
        rows = lax.broadcasted_iota(I32, (tt, 1), 0)
        for h in range(dm.h):
            def body(tb, carry, h=h):
                base = pl.multiple_of(tb * 8, 8)
                q8 = z_ref[pl.ds(base, 8), dm.q0:dm.q0 + dm.hk] * scale
                b8 = b_ref[pl.ds(base, 8), :]
                vh = z_ref[:, dm.v0 + h * dm.dv:dm.v0 + (h + 1) * dm.dv]
                out_rows = []
                for r in range(8):
                    keep = (rows <= base + r) & head_masks[h]
                    dec = jnp.exp(jnp.where(keep, b8[r:r + 1, :] - b_ref[...], -jnp.inf))
                    wgt = (q8[r:r + 1, :] * dec) * z_ref[:, dm.k0:dm.k0 + dm.hk]
                    s_col = jnp.sum(wgt, axis=-1, keepdims=True)
                    out_rows.append(jnp.sum(s_col * vh, axis=0, keepdims=True))
                oi_ref[pl.ds(base, 8), h * dm.dv:(h + 1) * dm.dv] = jnp.concatenate(out_rows, axis=0)
                return carry
            lax.fori_loop(0, tt // 8, body, 0)
        finish(pool_branch(), inter_chunk(scaled_queries(b_ref[...])) + oi_ref[...])

    bc = b_ref[...]
    blast = bc[tt - 1:tt, :]
    kd = z_ref[:, dm.k0:dm.k0 + dm.hk] * jnp.exp(blast - bc)
    u_all = _dot(kd.T.astype(BF16), z_ref[:, dm.v0:dm.v0 + dm.gw].astype(BF16))
    dec_col = jnp.broadcast_to(jnp.exp(blast), (dm.hk, dm.hk)).T[:, 0:dm.dv]
    upd = jnp.concatenate(
        [u_all[h * dm.dk:(h + 1) * dm.dk, h * dm.dv:(h + 1) * dm.dv] for h in range(dm.h)], axis=0)
    s_ref[...] = dec_col * s_ref[...] + upd
    uext_ref[0:POOL_HIST, :] = uext_ref[tt:tt + POOL_HIST, :]


def _const_spec(shape):
    nd = len(shape)
    return pl.BlockSpec(shape, lambda *_: (0,) * nd, pipeline_mode=pl.Buffered(1))


def _mix_prompt(dm, x2d, mod3, wts, bsz, seq, n_exp):
    tt = 256 if seq % 256 == 0 else seq
    n_t = seq // tt
    n = bsz * seq
    d = dm.d
    tiles = bsz * n_t
    kern = functools.partial(_mixp_kernel, dm=dm, tt=tt, n_t=n_t, tiles=tiles)
    proj = lambda i: jnp.minimum(i, tiles - 1)
    fin = lambda i: jnp.clip(i - 1, 0, tiles - 1)
    routed = lambda i: jnp.clip(i - 2, 0, tiles - 1)
    row = lambda i: (fin(i), 0)
    in_specs = [pl.BlockSpec((tt, d), lambda i: (proj(i), 0)),
                pl.BlockSpec((None, 1, N_MOD * d), lambda i: (proj(i) // n_t, 0, 0))]
    in_specs += [_const_spec(w.shape) for w in wts]
    out_specs = [pl.BlockSpec((tt, d), row), pl.BlockSpec((tt, d // 2), row),
                 pl.BlockSpec((TOP_K, tt), lambda i: (0, routed(i))), pl.BlockSpec((tt, LANE), lambda i: (routed(i), 0)),
                 pl.BlockSpec((8, n_exp), lambda i: (0, 0)),
                 pl.BlockSpec((None, POOL_HIST - 1, dm.pw), lambda i: (fin(i) // n_t, 0, 0)),
                 pl.BlockSpec((None, dm.hk, dm.dv), lambda i: (fin(i) // n_t, 0, 0))]
    out_shape = [jax.ShapeDtypeStruct((n, d), F32), jax.ShapeDtypeStruct((n, d // 2), I32),
                 jax.ShapeDtypeStruct((TOP_K, n), I32), jax.ShapeDtypeStruct((n, LANE), F32),
                 jax.ShapeDtypeStruct((8, n_exp), F32),
                 jax.ShapeDtypeStruct((bsz, POOL_HIST - 1, dm.pw), F32),
                 jax.ShapeDtypeStruct((bsz, dm.hk, dm.dv), F32)]
    scratch = [pltpu.VMEM((tt, dm.zw), F32), pltpu.VMEM((tt, dm.zw), F32),
               pltpu.VMEM((tt, d), F32), pltpu.VMEM((1, N_MOD * d), F32), pltpu.VMEM((tt, d), BF16),
               pltpu.VMEM((tt, n_exp), F32), pltpu.VMEM((tt + POOL_HIST, dm.pw), F32),
               pltpu.VMEM((dm.hk, dm.dv), F32), pltpu.VMEM((tt, dm.hk), F32), pltpu.VMEM((tt, dm.gw), F32)]
    return pl.pallas_call(
        kern, grid=(tiles + 2,), in_specs=in_specs, out_specs=out_specs, out_shape=out_shape,
        scratch_shapes=scratch, compiler_params=_cp(("arbitrary",)), name="mixp",
    )(x2d, mod3, *wts)


def _sin_kernel(x_ref, mod_ref, n1_ref, wia_ref, wig_ref, wil_ref, z_ref, *, dm):
    _in_project(dm, x_ref[...], mod_ref[...], n1_ref, (wia_ref, wig_ref, wil_ref), z_ref)


def _sample_in(dm, xs2d, mod_s, n1, win, bs, t_s):
    d = dm.d
    return pl.pallas_call(
        functools.partial(_sin_kernel, dm=dm),
        grid=(t_s,),
        in_specs=[pl.BlockSpec((bs, d), lambda t: (0, t)), _const_spec(mod_s.shape), _const_spec(n1.shape)]
        + [_const_spec(w.shape) for w in win],
        out_specs=pl.BlockSpec((bs, dm.zw), lambda t: (t, 0)),
        out_shape=jax.ShapeDtypeStruct((t_s * bs, dm.zw), F32),
        compiler_params=_cp(("arbitrary",)), name="sample_in",
    )(xs2d, mod_s, n1, *win)


def _sst_kernel(z_ref, pool_ref, gs_ref, wal_ref, bal_ref,
                mixed_ref, o_ref, npool_ref, ngs_ref,
                qt_ref, kt_ref, dt_ref, vt_ref, ot_ref, *, dm, bs, t_s, ch, pos0):
    i = pl.program_id(0)
    n_i = pl.num_programs(0)
    scale = dm.dk ** -0.5
    past = POOL_HIST - 1

    @pl.when(i == 0)
    def _():
        for t in range(t_s):
            r0 = t * bs
            la = _log_decay(z_ref[r0:r0 + bs, dm.al0:dm.al0 + LANE], wal_ref, bal_ref)
            dt_ref[t] = jnp.exp(la).T
            qt_ref[t] = (z_ref[r0:r0 + bs, dm.q0:dm.q0 + dm.hk] * scale).T
            kt_ref[t] = z_ref[r0:r0 + bs, dm.k0:dm.k0 + dm.hk].T
            vt_ref[t] = z_ref[r0:r0 + bs, dm.v0:dm.v0 + dm.gw].T
        ot_ref[...] = jnp.zeros(ot_ref.shape, F32)

        def item(s, c0, c1):
            if s >= 0:
                return z_ref[s * bs:(s + 1) * bs, c0:c1]
            return pool_ref[past + s, :, c0:c1]

        gw = dm.pw // len(POOL_WINDOWS)
        for t in range(t_s):
            for g, w in enumerate(POOL_WINDOWS):
                c0, c1 = g * gw, (g + 1) * gw
                cur = item(t, c0, c1)
                acc = cur
                for r in range(1, w):
                    acc = acc + item(t - r, c0, c1)
                cnt = float(min(w, pos0 + t + 1))
                mixed_ref[t * bs:(t + 1) * bs, c0:c1] = acc / cnt - cur
        for r in range(past):
            npool_ref[r] = item(r - past + t_s, 0, dm.pw)

    per_head = dm.dk // ch
    hoff = pl.multiple_of((i // per_head) * dm.dv, dm.dv)
    for p in range(ch):
        hd = i * ch + p
        st = gs_ref[:, p, :].T
        for t in range(t_s):
            st = dt_ref[t, pl.ds(hd, 1), :] * st + kt_ref[t, pl.ds(hd, 1), :] * vt_ref[t, pl.ds(hoff, dm.dv), :]
            ot_ref[t, pl.ds(hoff, dm.dv), :] += qt_ref[t, pl.ds(hd, 1), :] * st
        ngs_ref[:, p, :] = st.T

    @pl.when(i == n_i - 1)
    def _():
        for t in range(t_s):
            for h in range(dm.h):
                o_ref[t * bs:(t + 1) * bs, h * dm.dv:(h + 1) * dm.dv] = ot_ref[t, h * dm.dv:(h + 1) * dm.dv, :].T


def _sample_state(dm, z_s, pool3, gs3, wal, bal, bs, t_s):
    ch = 16
    n_i = dm.hk // ch
    past = POOL_HIST - 1
    kern = functools.partial(_sst_kernel, dm=dm, bs=bs, t_s=t_s, ch=ch, pos0=PAST_LEN)
    full = lambda shape: pl.BlockSpec(shape, lambda i: (0,) * len(shape))
    return pl.pallas_call(
        kern, grid=(n_i,),
        in_specs=[_const_spec(z_s.shape), _const_spec(pool3.shape),
                  pl.BlockSpec((bs, ch, dm.dv), lambda i: (0, i, 0)),
                  _const_spec(wal.shape), _const_spec(bal.shape)],
        out_specs=[full((t_s * bs, dm.pw)), full((t_s * bs, dm.gw)), full((past, bs, dm.pw)),
                   pl.BlockSpec((bs, ch, dm.dv), lambda i: (0, i, 0))],
        out_shape=[jax.ShapeDtypeStruct((t_s * bs, dm.pw), F32), jax.ShapeDtypeStruct((t_s * bs, dm.gw), F32),
                   jax.ShapeDtypeStruct((past, bs, dm.pw), F32), jax.ShapeDtypeStruct(gs3.shape, F32)],
        scratch_shapes=[pltpu.VMEM((t_s, dm.hk, bs), F32), pltpu.VMEM((t_s, dm.hk, bs), F32),
                        pltpu.VMEM((t_s, dm.hk, bs), F32), pltpu.VMEM((t_s, dm.gw, bs), F32),
                        pltpu.VMEM((t_s, dm.gw, bs), F32)],
        compiler_params=_cp(("arbitrary",)), name="sample_state",
    )(z_s, pool3, gs3, wal, bal)


def _spost_kernel(x_ref, mod_ref, z_ref, mixed_ref, o_ref, pw_ref, ps_ref, gn_ref,
                  wa_ref, wb_ref, wo_ref, n2_ref, wrh_ref, wrl_ref,
                  x1_ref, h2_ref, lg_ref, *, dm):
    gw = dm.pw // len(POOL_WINDOWS)
    ya = _pool_project([mixed_ref[:, g * gw:(g + 1) * gw] for g in range(len(POOL_WINDOWS))], pw_ref, ps_ref)
    x1, h2, logits = _post_mix(
        dm, x_ref[...], mod_ref[...], ya, o_ref[...],
        z_ref[:, dm.go0:dm.go0 + dm.gw], z_ref[:, dm.ga0:dm.ga0 + dm.d], z_ref[:, dm.gb0:dm.gb0 + dm.d],
        gn_ref, wa_ref, wb_ref, wo_ref, n2_ref, wrh_ref, wrl_ref)
    x1_ref[...] = x1
    h2_ref[...] = h2
    lg_ref[...] = logits


def _sample_post(dm, xs2d, mod_s, z_s, mixed_s, o_s, wts, bs, t_s, n_exp):
    d = dm.d
    rows = lambda t: (t, 0)
    in_specs = [pl.BlockSpec((bs, d), lambda t: (0, t)), _const_spec(mod_s.shape),
                pl.BlockSpec((bs, dm.zw), rows), pl.BlockSpec((bs, dm.pw), rows), pl.BlockSpec((bs, dm.gw), rows)]
    in_specs += [_const_spec(w.shape) for w in wts]
    return pl.pallas_call(
        functools.partial(_spost_kernel, dm=dm), grid=(t_s,), in_specs=in_specs,
        out_specs=[pl.BlockSpec((bs, d), rows), pl.BlockSpec((bs, d // 2), rows), pl.BlockSpec((bs, n_exp), rows)],
        out_shape=[jax.ShapeDtypeStruct((t_s * bs, d), F32), jax.ShapeDtypeStruct((t_s * bs, d // 2), I32),
                   jax.ShapeDtypeStruct((t_s * bs, n_exp), F32)],
        compiler_params=_cp(("arbitrary",)), name="sample_post",
    )(xs2d, mod_s, z_s, mixed_s, o_s, *wts)


def _route_kernel(lg_ref, bias_ref, idx_ref, w_ref, cnt_ref):
    i = pl.program_id(0)
    idx_ref[...], w_ref[...], picked = _topk_rows(lg_ref[...], bias_ref[...])

    @pl.when(i == 0)
    def _():
        cnt_ref[...] = jnp.zeros(cnt_ref.shape, F32)

    cnt_ref[...] += _count_members(picked)


def _route(logits, bias, tr):
    n, n_exp = logits.shape
    rows = lambda i: (i, 0)
    return pl.pallas_call(
        _route_kernel, grid=(n // tr,),
        in_specs=[pl.BlockSpec((tr, n_exp), rows), _const_spec((1, n_exp))],
        out_specs=[pl.BlockSpec((TOP_K, tr), lambda i: (0, i)), pl.BlockSpec((tr, LANE), rows),
                   pl.BlockSpec((8, n_exp), lambda i: (0, 0))],
        out_shape=[jax.ShapeDtypeStruct((TOP_K, n), I32), jax.ShapeDtypeStruct((n, LANE), F32),
                   jax.ShapeDtypeStruct((8, n_exp), F32)],
        compiler_params=_cp(("arbitrary",)), name="route",
    )(logits, bias)


def _plan_kernel(idx_ref, cnt_ref, pos_ref, bounds_ref, base_ref, dest_ref, *, n_exp):
    i = pl.program_id(0)
    tr = idx_ref.shape[1]

    @pl.when(i == 0)
    def _():
        lane = lax.broadcasted_iota(I32, (1, n_exp), 1)
        cnt = cnt_ref[...].astype(I32)
        padded = ((cnt + (EXPERT_BLOCK - 1)) // EXPERT_BLOCK) * EXPERT_BLOCK
        pe = padded
        s = 1
        while s < n_exp:
            pe = pe + jnp.where(lane >= s, pltpu.roll(pe, s, 1), 0)
            s *= 2
        row = lax.broadcasted_iota(I32, (8, n_exp), 0)
        bounds_ref[...] = jnp.where(row == 0, pe - padded, pe)
        first = (pe - padded).astype(F32)[0:1, :]
        base_ref[...] = jnp.broadcast_to(first, (LANE, n_exp)).T

    eidx = lax.broadcasted_iota(I32, (n_exp, 1), 0)
    idx = idx_ref[...]
    member = eidx == idx[0:1, :]
    for k in range(1, TOP_K):
        member = member | (eidx == idx[k:k + 1, :])
    mb = jnp.where(member, 1.0, 0.0).astype(BF16)
    before = jnp.where(lax.broadcasted_iota(I32, (tr, tr), 0) < lax.broadcasted_iota(I32, (tr, tr), 1),
                       1.0, 0.0).astype(BF16)
    dest_ref[...] = _dot(mb, before) + base_ref[:, 0:1]
    for b in range(tr // LANE):
        cols = slice(b * LANE, (b + 1) * LANE)
        dest = dest_ref[:, cols]
        rows = [jnp.sum(jnp.where(eidx == idx[k:k + 1, cols], dest, 0.0), axis=0, keepdims=True)
                for k in range(TOP_K)]
        pos_ref[:, cols] = jnp.concatenate(rows, axis=0).astype(I32)
    base_ref[...] += jnp.sum(mb.astype(F32), axis=1, keepdims=True)


def _plan(idx_t, counts, tr):
    n = idx_t.shape[1]
    n_exp = counts.shape[1]
    cols = lambda i: (0, i)
    return pl.pallas_call(
        functools.partial(_plan_kernel, n_exp=n_exp), grid=(n // tr,),
        in_specs=[pl.BlockSpec((TOP_K, tr), cols), _const_spec((8, n_exp))],
        out_specs=[pl.BlockSpec((TOP_K, tr), cols), pl.BlockSpec((8, n_exp), lambda i: (0, 0))],
        out_shape=[jax.ShapeDtypeStruct((TOP_K, n), I32), jax.ShapeDtypeStruct((8, n_exp), I32)],
        scratch_shapes=[pltpu.VMEM((n_exp, LANE), F32), pltpu.VMEM((n_exp, tr), F32)],
        compiler_params=_cp(("arbitrary",)), name="plan",
    )(idx_t, counts)


SC_CORES = 2
SC_SUBCORES = 16
SC_AXES = ("core", "subcore")
SC_WORKERS = SC_CORES * SC_SUBCORES
SC_WINDOW = 128


def _sc_mesh():
    return plsc.VectorSubcoreMesh(core_axis_name=SC_AXES[0], subcore_axis_name=SC_AXES[1],
                                  num_cores=SC_CORES, num_subcores=SC_SUBCORES)


def _sc_worker():
    return lax.axis_index(SC_AXES[1]) * SC_CORES + lax.axis_index(SC_AXES[0])


def _dispatch(h_p, h_s, pos_t, n_rows):
    n_p, c = h_p.shape
    n_s = h_s.shape[0]
    win_p, win_s = n_p // SC_WINDOW, n_s // SC_WINDOW
    assert n_p % (SC_WINDOW * SC_WORKERS) == 0 and n_s % SC_WINDOW == 0 and win_s <= SC_WORKERS

    @functools.partial(
        pl.kernel, out_type=jax.ShapeDtypeStruct((n_rows, c), h_p.dtype), mesh=_sc_mesh(), name="dispatch",
        scratch_types=[pltpu.VMEM((TOP_K, SC_WINDOW), I32), pltpu.VMEM((SC_WINDOW, c), h_p.dtype),
                       pltpu.SemaphoreType.DMA])
    def run(hp_hbm, hs_hbm, pos_hbm, xs_hbm, idx_v, rows_v, sem):
        wid = _sc_worker()

        def scatter_window(h_hbm, row0, tok0):
            pltpu.sync_copy(h_hbm.at[pl.ds(row0, SC_WINDOW)], rows_v)
            pltpu.sync_copy(pos_hbm.at[:, pl.ds(tok0, SC_WINDOW)], idx_v)
            copies = [pltpu.async_copy(rows_v, xs_hbm.at[idx_v.at[k]], sem) for k in range(TOP_K)]
            for cp in copies:
                cp.wait()

        def prompt_window(j, carry):
            row0 = pl.multiple_of((wid * (win_p // SC_WORKERS) + j) * SC_WINDOW, SC_WINDOW)
            scatter_window(hp_hbm, row0, row0)
            return carry
        lax.fori_loop(0, win_p // SC_WORKERS, prompt_window, 0)

        @pl.when(wid < win_s)
        def _():
            row0 = pl.multiple_of(wid * SC_WINDOW, SC_WINDOW)
            scatter_window(hs_hbm, row0, n_p + row0)

    return run(h_p, h_s, pos_t)


def _combine_gather(ys, pos_t):
    k_n = pos_t.shape[0] * pos_t.shape[1]
    c = ys.shape[1]
    win = SC_WINDOW // 2
    per_worker = k_n // (win * SC_WORKERS)
    assert k_n % (win * SC_WORKERS) == 0 and per_worker % 2 == 0
    dma = pltpu.SemaphoreType.DMA

    @functools.partial(
        pl.kernel, out_type=jax.ShapeDtypeStruct((k_n, c), ys.dtype), mesh=_sc_mesh(), name="combine_gather",
        scratch_types=[pltpu.VMEM((win,), I32), pltpu.VMEM((win,), I32),
                       pltpu.VMEM((win, c), ys.dtype), pltpu.VMEM((win, c), ys.dtype), dma, dma, dma, dma])
    def run(ys_hbm, idx_hbm, out_hbm, idx0, idx1, buf0, buf1, gsem0, gsem1, ssem0, ssem1):
        wid = _sc_worker()

        def rows(j):
            return pl.ds(pl.multiple_of((wid * per_worker + j) * win, win), win)

        def gather(j, idx_v, buf, sem):
            pltpu.sync_copy(idx_hbm.at[rows(j)], idx_v)
            pltpu.async_copy(ys_hbm.at[idx_v], buf, sem)

        def gather_wait(idx_v, buf, sem):
            pltpu.make_async_copy(ys_hbm.at[idx_v], buf, sem).wait()

        def store(j, buf, sem):
            pltpu.async_copy(buf, out_hbm.at[rows(j)], sem)

        def store_wait(buf, sem):
            pltpu.make_async_copy(buf, out_hbm.at[rows(0)], sem).wait()

        gather(0, idx0, buf0, gsem0)

        def two_windows(i, carry):
            j = 2 * i

            @pl.when(i > 0)
            def _():
                store_wait(buf1, ssem1)

            gather(j + 1, idx1, buf1, gsem1)
            gather_wait(idx0, buf0, gsem0)
            store(j, buf0, ssem0)

            @pl.when(j + 2 < per_worker)
            def _():
                store_wait(buf0, ssem0)
                gather(j + 2, idx0, buf0, gsem0)

            gather_wait(idx1, buf1, gsem1)
            store(j + 1, buf1, ssem1)
            return carry
        lax.fori_loop(0, per_worker // 2, two_windows, 0)
        store_wait(buf0, ssem0)
        store_wait(buf1, ssem1)

    return run(ys, pos_t.reshape(k_n))


MOE_RING = 16
MOE_WEIGHT_RING = 8


def _moe_kernel(st_ref, wg_ref, wu_ref, wd_ref, xs_ref, ys_ref, xbuf, ybuf, wgf, wuf, wdf, wgb, wub, wdb,
                semx, semy, semw):
    e = pl.program_id(0)
    n_e = pl.num_programs(0)

    def w_copies(ex):
        slot = ex % MOE_WEIGHT_RING
        return [pltpu.make_async_copy(src.at[ex], dst.at[slot], semw.at[slot, i])
                for i, (src, dst) in enumerate(((wg_ref, wgf), (wu_ref, wuf), (wd_ref, wdf)))]

    @pl.when(e == 0)
    def _():
        for ex in range(MOE_WEIGHT_RING - 1):
            @pl.when(ex < n_e)
            def _(ex=ex):
                for cp in w_copies(ex):
                    cp.start()

    @pl.when(e + (MOE_WEIGHT_RING - 1) < n_e)
    def _():
        for cp in w_copies(e + (MOE_WEIGHT_RING - 1)):
            cp.start()
    g0 = st_ref[e] // EXPERT_BLOCK
    nchunks = st_ref[e + 1] // EXPERT_BLOCK - g0
    ntot = st_ref[n_e] // EXPERT_BLOCK

    def rows(g):
        return pl.ds(pl.multiple_of(g * EXPERT_BLOCK, EXPERT_BLOCK), EXPERT_BLOCK)

    def x_copy(g):
        slot = g % MOE_RING
        return pltpu.make_async_copy(xs_ref.at[rows(g)], xbuf.at[slot], semx.at[slot])

    def y_copy(g):
        slot = g % MOE_RING
        return pltpu.make_async_copy(ybuf.at[slot], ys_ref.at[rows(g)], semy.at[slot])

    @pl.when(e == 0)
    def _():
        for g in range(MOE_RING - 1):
            @pl.when(g < ntot)
            def _(g=g):
                x_copy(g).start()

    for cp in w_copies(e):
        cp.wait()
    wslot = e % MOE_WEIGHT_RING
    wgb[...] = wgf[wslot].astype(BF16)
    wub[...] = wuf[wslot].astype(BF16)
    wdb[...] = wdf[wslot].astype(BF16)

    def take(g):
        @pl.when(g + (MOE_RING - 1) < ntot)
        def _():
            x_copy(g + (MOE_RING - 1)).start()

        x_copy(g).wait()

        @pl.when(g >= MOE_RING)
        def _():
            y_copy(g - MOE_RING).wait()

        return xbuf[g % MOE_RING]

    def expert_mlp(x_words):
        x = _unpack_rows(x_words).astype(BF16)
        mid = _silu(_dot(x, wgb[...])) * _dot(x, wub[...])
        return _pack_rows(_dot(mid.astype(BF16), wdb[...]))

    def put(g, y_words):
        ybuf[g % MOE_RING] = y_words
        y_copy(g).start()

    def run_chunks(g, m):
        xs = [take(g + i) for i in range(m)]
        y = expert_mlp(xs[0] if m == 1 else jnp.concatenate(xs, axis=0))
        for i in range(m):
            put(g + i, y[i * EXPERT_BLOCK:(i + 1) * EXPERT_BLOCK])

    def quad(p, carry):
        run_chunks(g0 + 4 * p, 4)
        return carry

    rest = nchunks % 4
    five = (rest == 1) & (nchunks >= 5)
    n_quads = nchunks // 4 - five.astype(I32)
    lax.fori_loop(0, n_quads, quad, 0)
    g_rest = g0 + 4 * n_quads

    @pl.when(five)
    def _():
        run_chunks(g_rest, 5)

    @pl.when(rest >= 2)
    def _():
        run_chunks(g_rest, 2)

    @pl.when((rest == 3) | (nchunks == 1))
    def _():
        run_chunks(g0 + nchunks - 1, 1)

    @pl.when(e == n_e - 1)
    def _():
        for j in range(MOE_RING):
            g = ntot - MOE_RING + j

            @pl.when(g >= 0)
            def _(g=g):
                y_copy(g).wait()


def _moe(starts, xs, wg, wu, wd):
    n_rows, c = xs.shape
    n_exp, d, f = wg.shape
    grid_spec = pltpu.PrefetchScalarGridSpec(
        num_scalar_prefetch=1, grid=(n_exp,),
        in_specs=[pl.BlockSpec(memory_space=pl.ANY)] * 4,
        out_specs=pl.BlockSpec(memory_space=pl.ANY),
        scratch_shapes=[pltpu.VMEM((MOE_RING, EXPERT_BLOCK, c), I32), pltpu.VMEM((MOE_RING, EXPERT_BLOCK, c), I32),
                        pltpu.VMEM((MOE_WEIGHT_RING, d, f), F32), pltpu.VMEM((MOE_WEIGHT_RING, d, f), F32),
                        pltpu.VMEM((MOE_WEIGHT_RING, f, d), F32),
                        pltpu.VMEM((d, f), BF16), pltpu.VMEM((d, f), BF16), pltpu.VMEM((f, d), BF16),
                        pltpu.SemaphoreType.DMA((MOE_RING,)), pltpu.SemaphoreType.DMA((MOE_RING,)),
                        pltpu.SemaphoreType.DMA((MOE_WEIGHT_RING, 3))])
    return pl.pallas_call(
        _moe_kernel, grid_spec=grid_spec, out_shape=jax.ShapeDtypeStruct((n_rows, c), I32),
        compiler_params=_cp(("arbitrary",)), name="moe",
    )(starts, wg, wu, wd, xs)


def _final_kernel(yg_ref, w8_ref, x1_ref, h2_ref, mod_ref, wsg_ref, wsu_ref, wsd_ref, nf_ref, y_ref, *, d):
    hb = _unpack_rows(h2_ref[...]).astype(BF16)
    shared = _dot((_silu(_dot(hb, wsg_ref[...])) * _dot(hb, wsu_ref[...])).astype(BF16), wsd_ref[...])
    w8 = w8_ref[...]
    routed = _unpack_rows(yg_ref[0]) * w8[:, 0:1]
    for k in range(1, TOP_K):
        routed = routed + _unpack_rows(yg_ref[k]) * w8[:, k:k + 1]
    g2 = mod_ref[...][:, (N_MOD - 1) * d:N_MOD * d]
    x2 = x1_ref[...] + g2 * (routed + shared)
    y_ref[...] = _rms(x2, nf_ref[...])


def _final(yg3, w8, x1, h2, mod, mod_spec, wts, nf, tf, n_tiles, tile0, out_shape, out_spec):
    d = x1.shape[1]
    rows = lambda i: (i, 0)
    in_specs = [pl.BlockSpec((TOP_K, tf, d // 2), lambda i: (0, tile0 + i, 0)),
                pl.BlockSpec((tf, LANE), rows),
                pl.BlockSpec((tf, d), rows), pl.BlockSpec((tf, d // 2), rows), mod_spec]
    in_specs += [_const_spec(w.shape) for w in wts]
    in_specs += [_const_spec(nf.shape)]
    return pl.pallas_call(
        functools.partial(_final_kernel, d=d), grid=(n_tiles,), in_specs=in_specs,
        out_specs=out_spec, out_shape=out_shape,
        compiler_params=_cp(("arbitrary",)), name="final",
    )(yg3, w8, x1, h2, mod, *wts, nf)


def kernel(x_prompt, x_sample, c_prompt, c_sample, state_pool, state_gla, w_ada, b_ada, norm1, w_in,
           pool_w, pool_scale, gla_w_alpha, gla_b_alpha, gla_norm, w_branch_a, w_branch_b, w_out, norm2,
           w_router, router_bias, w_exp_gate, w_exp_up, w_exp_down, w_sh_gate, w_sh_up, w_sh_down, norm_f):
    depth = w_in.shape[0]
    assert depth == 1, "single-layer trunk"
    bp, seq, d = x_prompt.shape
    bs, t_s, _ = x_sample.shape
    _, _, past, pw = state_pool.shape
    _, _, heads, dk, dv = state_gla.shape
    rank = gla_w_alpha.shape[1]
    n_exp = w_router.shape[2]
    assert past == POOL_HIST - 1 and pw == len(POOL_WINDOWS) * LANE and dv == LANE and rank <= LANE
    dm = _Dims(d, pw, heads, dk, dv)
    n_p, n_s = bp * seq, bs * t_s
    n_all = n_p + n_s

    off = [0]
    for sz in (pw, dm.hk, dm.hk, dm.gw, dm.gw, rank, d, d):
        off.append(off[-1] + sz)
    wi = w_in[0]
    win = (wi[:, :off[5]].astype(BF16), wi[:, off[6]:].astype(BF16),
           jnp.concatenate([wi[:, off[5]:off[6]], jnp.zeros((d, LANE - rank), F32)], axis=1).astype(BF16))
    assert off[5] + 2 * d + LANE == dm.zw
    wal = jnp.concatenate([gla_w_alpha[0], jnp.zeros((LANE - rank, dm.hk), F32)], axis=0).astype(BF16)
    bal = gla_b_alpha[0].reshape(1, dm.hk)
    wr = w_router[0]
    wrh = wr.astype(BF16)
    wrl = (wr - wrh.astype(F32)).astype(BF16)
    pwg = pool_w[0]
    zero = jnp.zeros_like(pwg[0])
    pw_pairs = jnp.stack([jnp.block([[pwg[g], zero], [zero, pwg[g + 1]]]) for g in range(0, len(POOL_WINDOWS), 2)])
    post_w = (pw_pairs.astype(BF16), pool_scale[0].reshape(1, pw), gla_norm[0].reshape(1, dm.gw),
              w_branch_a[0].astype(BF16), w_branch_b[0].astype(BF16), w_out[0].astype(BF16),
              norm2[0].reshape(1, d), wrh, wrl)
    n1 = norm1[0].reshape(1, d)

    mod_p, mod_s = _ada(c_prompt, c_sample, w_ada[0], b_ada[0])
    mod_p = mod_p.reshape(bp, 1, N_MOD * d)

    rbias = router_bias[0].reshape(1, n_exp)
    x1_p, h2_p, idx_p, w8_p, cnt_p, npool_p, ngla_p = _mix_prompt(
        dm, x_prompt.reshape(n_p, d), mod_p, (n1,) + win + (wal, bal) + post_w + (rbias,), bp, seq, n_exp)

    xs2d = x_sample.reshape(bs, t_s * d)
    z_s = _sample_in(dm, xs2d, mod_s, n1, win, bs, t_s)
    mixed_s, o_s, npool_s, ngla_s = _sample_state(
        dm, z_s, jnp.swapaxes(state_pool[0], 0, 1), state_gla[0].reshape(bs, dm.hk, dv), wal, bal, bs, t_s)
    x1_s, h2_s, lg_s = _sample_post(dm, xs2d, mod_s, z_s, mixed_s, o_s, post_w, bs, t_s, n_exp)

    tr = 512 if (n_p % 512 == 0 and n_s % 512 == 0) else LANE
    assert n_p % tr == 0 and n_s % tr == 0 and n_p % bs == 0
    idx_s, w8_s, cnt_s = _route(lg_s, rbias, tr)
    nblk = (n_all * TOP_K) // EXPERT_BLOCK + n_exp
    pos_t, bounds = _plan(jnp.concatenate([idx_p, idx_s], axis=1), cnt_p + cnt_s, tr)
    starts = jnp.concatenate([bounds[0], bounds[1, n_exp - 1:]])

    xs = _dispatch(h2_p, h2_s, pos_t, nblk * EXPERT_BLOCK)
    ys = _moe(starts, xs, w_exp_gate[0], w_exp_up[0], w_exp_down[0])
    yg3 = _combine_gather(ys, pos_t).reshape(TOP_K, n_all, d // 2)

    sh_w = (w_sh_gate[0].astype(BF16), w_sh_up[0].astype(BF16), w_sh_down[0].astype(BF16))
    nf = norm_f.reshape(1, d)
    tf = 512 if seq % 512 == 0 else seq
    n_tp = seq // tf
    y_p = _final(
        yg3, w8_p, x1_p, h2_p, mod_p,
        pl.BlockSpec((None, 1, N_MOD * d), lambda i: (i // n_tp, 0, 0)), sh_w, nf, tf, n_p // tf, 0,
        jax.ShapeDtypeStruct((n_p, d), F32), pl.BlockSpec((tf, d), lambda i: (i, 0)))
    y_s = _final(
        yg3, w8_s, x1_s, h2_s, mod_s,
        _const_spec(mod_s.shape), sh_w, nf, bs, t_s, n_p // bs,
        jax.ShapeDtypeStruct((bs, t_s * d), F32), pl.BlockSpec((bs, d), lambda i: (0, i)))

    return (y_p.reshape(bp, seq, d), y_s.reshape(bs, t_s, d),
            npool_p.reshape(depth, bp, past, pw), ngla_p.reshape(depth, bp, heads, dk, dv),
            jnp.swapaxes(npool_s, 0, 1).reshape(depth, bs, past, pw), ngla_s.reshape(depth, bs, heads, dk, dv))
```

```python
import functools

import jax
import jax.numpy as jnp
from jax import lax
from jax.experimental import pallas as pl
from jax.experimental.pallas import tpu as pltpu
from jax.experimental.pallas import tpu_sc as plsc

F32 = jnp.float32
BF16 = jnp.bfloat16
I32 = jnp.int32

EPS = 1e-6
N_MOD = 6
POOL_WINDOWS = (2, 4, 8, 16)
POOL_HIST = 16
GLA_GATE_TEMP = 16.0
TOP_K = 8
ROUTED_SCALE = 2.5
PAST_LEN = 16384
EXPERT_BLOCK = 128
LANE = 128
GLA_FAST_MAX_DECAY = 40.0
PROJ_PIECE = 512
PROJ_LEAD = 2
VMEM_LIMIT = 56 * 1024 * 1024


def _cp(sem, vmem=VMEM_LIMIT):
    return pltpu.CompilerParams(dimension_semantics=sem, vmem_limit_bytes=vmem)


def _rms(x, g):
    return x * lax.rsqrt(jnp.mean(x * x, axis=-1, keepdims=True) + EPS) * g


def _silu(x):
    return x * jax.nn.sigmoid(x)


def _dot(a, b):
    return jnp.dot(a, b, preferred_element_type=F32)


def _split3(a):
    a0 = a.astype(BF16)
    r1 = a - a0.astype(F32)
    a1 = r1.astype(BF16)
    a2 = (r1 - a1.astype(F32)).astype(BF16)
    return a0, a1, a2


def _mod_parts(mod, d):
    return [mod[:, i * d:(i + 1) * d] for i in range(N_MOD)]


_HI16 = -65536


def _pack_rows(v):
    c = v.shape[1] // 2
    lo = lax.bitcast_convert_type(v[:, :c].astype(BF16).astype(F32), I32)
    hi = lax.bitcast_convert_type(v[:, c:].astype(BF16).astype(F32), I32)
    return (hi & _HI16) | lax.shift_right_logical(lo, 16)


def _unpack_rows(w):
    lo = lax.bitcast_convert_type(lax.shift_left(w, 16), F32)
    hi = lax.bitcast_convert_type(w & _HI16, F32)
    return jnp.concatenate([lo, hi], axis=1)


def _ada_kernel(cp_ref, cs_ref, w_ref, b_ref, op_ref, os_ref):
    w = w_ref[...].astype(BF16)
    for c_ref, o_ref in ((cp_ref, op_ref), (cs_ref, os_ref)):
        o_ref[...] = _dot(_silu(c_ref[...]).astype(BF16), w) + b_ref[...]


def _ada(c_p, c_s, w_ada, b_ada):
    d, cols = w_ada.shape
    bc = d
    whole = lambda c: pl.BlockSpec(c.shape, lambda i: (0, 0))
    col_block = lambda rows: pl.BlockSpec((rows, bc), lambda i: (0, i))
    return pl.pallas_call(
        _ada_kernel,
        grid=(cols // bc,),
        in_specs=[whole(c_p), whole(c_s), col_block(d), col_block(1)],
        out_specs=[col_block(c_p.shape[0]), col_block(c_s.shape[0])],
        out_shape=[jax.ShapeDtypeStruct((c_p.shape[0], cols), F32), jax.ShapeDtypeStruct((c_s.shape[0], cols), F32)],
        compiler_params=_cp(("arbitrary",)),
        name="ada",
    )(c_p, c_s, w_ada, b_ada.reshape(1, cols))


class _Dims:
    def __init__(self, d, pw, heads, dk, dv):
        self.d, self.pw, self.h, self.dk, self.dv = d, pw, heads, dk, dv
        self.hk, self.gw = heads * dk, heads * dv
        self.q0 = pw
        self.k0 = self.q0 + self.hk
        self.v0 = self.k0 + self.hk
        self.go0 = self.v0 + self.gw
        self.ga0 = self.go0 + self.gw
        self.gb0 = self.ga0 + d
        self.al0 = self.gb0 + d
        self.zw = self.al0 + LANE


def _log_decay(z_al, wal_ref, bal_ref):
    xal = _dot(z_al.astype(BF16), wal_ref[...]) + bal_ref[...]
    return jax.nn.log_sigmoid(xal) / GLA_GATE_TEMP


def _pool_project(mixed_groups, pw_ref, ps_ref):
    ys = [_dot(jnp.concatenate(mixed_groups[2 * p:2 * p + 2], axis=1).astype(BF16), pw_ref[p])
          for p in range(len(mixed_groups) // 2)]
    return jnp.concatenate(ys, axis=1) * ps_ref[...]


def _post_mix(dm, x, mod, ya, o, go, ga, gb, gn_ref, wa_ref, wb_ref, wo_ref, n2_ref, wrh_ref, wrl_ref,
              filler=lambda: None):
    _, _, g1, sh2, sc2, _ = _mod_parts(mod, dm.d)
    parts = []
    for h in range(dm.h):
        oh = o[:, h * dm.dv:(h + 1) * dm.dv]
        parts.append(oh * lax.rsqrt(jnp.mean(oh * oh, axis=-1, keepdims=True) + EPS))
    yb = jnp.concatenate(parts, axis=1) * gn_ref[...] * _silu(go)
    filler()
    m = (jax.nn.sigmoid(ga) * _dot(ya.astype(BF16), wa_ref[...])
         + jax.nn.sigmoid(gb) * _dot(yb.astype(BF16), wb_ref[...]))
    filler()
    x1 = x + g1 * _dot(m.astype(BF16), wo_ref[...])
    h2 = _rms(x1, n2_ref[...]) * (1 + sc2) + sh2
    hi = h2.astype(BF16)
    lo = (h2 - hi.astype(F32)).astype(BF16)
    filler()
    logits = _dot(hi, wrh_ref[...]) + (_dot(hi, wrl_ref[...]) + _dot(lo, wrh_ref[...]))
    filler()
    return x1, _pack_rows(h2), logits


def _topk_rows(logits, bias):
    tr, n_exp = logits.shape
    s = jax.nn.sigmoid(logits)
    cur = s + bias
    lane = lax.broadcasted_iota(I32, (1, n_exp), 1).astype(F32)
    slot = lax.broadcasted_iota(I32, (1, LANE), 1)
    idx8 = jnp.zeros((tr, LANE), F32)
    w8 = jnp.zeros((tr, LANE), F32)
    wsum = jnp.zeros((tr, 1), F32)
    cur0 = cur
    for k in range(TOP_K):
        m = jnp.max(cur, axis=-1, keepdims=True)
        am = jnp.min(jnp.where(cur == m, lane, float(n_exp)), axis=-1, keepdims=True)
        oh = lane == am
        sk = jnp.sum(jnp.where(oh, s, 0.0), axis=-1, keepdims=True)
        cur = jnp.where(oh, -jnp.inf, cur)
        idx8 = jnp.where(slot == k, am, idx8)
        w8 = jnp.where(slot == k, sk, w8)
        wsum = wsum + sk
    picked = jnp.where((cur == -jnp.inf) & (cur0 > -jnp.inf), 1.0, 0.0)
    return idx8.T[0:TOP_K, :].astype(I32), w8 / wsum * ROUTED_SCALE, picked.astype(BF16)


def _count_members(picked):
    return _dot(jnp.ones((8, picked.shape[0]), BF16), picked)


def _in_project(dm, x, mod, n1_ref, win_refs, z_ref):
    h = (_rms(x, n1_ref[...]) * (1 + mod[:, dm.d:2 * dm.d]) + mod[:, 0:dm.d]).astype(BF16)
    starts = [0]
    for w_ref in win_refs:
        starts.append(starts[-1] + w_ref.shape[1])
    for i in (0, 2, 1):
        z_ref[:, starts[i]:starts[i + 1]] = _dot(h, win_refs[i][...])


def _mixp_kernel(x_ref, mod_ref, n1_ref, wia_ref, wig_ref, wil_ref, wal_ref, bal_ref, pw_ref, ps_ref, gn_ref,
                 wa_ref, wb_ref, wo_ref, n2_ref, wrh_ref, wrl_ref, rb_ref,
                 x1_ref, h2_ref, idx_ref, w8_ref, cnt_ref, np_ref, ng_ref,
                 za_ref, zb_ref, xk_ref, mk_ref, hn_ref, lg_ref, uext_ref, s_ref, b_ref, oi_ref,
                 *, dm, tt, n_t, tiles):
    n = pl.program_id(0)
    j = jnp.clip(n - 1, 0, tiles - 1) % n_t

    @pl.when(n == 0)
    def _():
        za_ref[...] = jnp.zeros(za_ref.shape, F32)
        zb_ref[...] = jnp.zeros(zb_ref.shape, F32)
        xk_ref[...] = jnp.zeros(xk_ref.shape, F32)
        mk_ref[...] = jnp.zeros(mk_ref.shape, F32)
        lg_ref[...] = jnp.zeros(lg_ref.shape, F32)
        cnt_ref[...] = jnp.zeros(cnt_ref.shape, F32)

    def route_previous():
        rows = LANE if tt % LANE == 0 else tt
        picked = []
        for r0 in range(0, tt, rows):
            idx_ref[:, r0:r0 + rows], w8_ref[r0:r0 + rows, :], pk = _topk_rows(lg_ref[r0:r0 + rows, :], rb_ref[...])
            picked.append(pk)
        return picked

    def count_routed(picked):
        cnt_ref[...] += jnp.where(n >= 2, _count_members(picked), 0.0)

    @pl.when((j == 0) & (n <= tiles))
    def _():
        uext_ref[0:POOL_HIST, :] = jnp.zeros((POOL_HIST, dm.pw), F32)
        s_ref[...] = jnp.zeros(s_ref.shape, F32)

    @pl.when((n % 2 == 0) & (n <= tiles))
    def _():
        _mixp_region(za_ref, zb_ref, x_ref, mod_ref, n1_ref, wia_ref, wig_ref, wil_ref, wal_ref, bal_ref, pw_ref,
                     ps_ref, gn_ref, wa_ref, wb_ref, wo_ref, n2_ref, wrh_ref, wrl_ref, x1_ref, h2_ref, lg_ref,
                     xk_ref, mk_ref, hn_ref, uext_ref, s_ref, b_ref, oi_ref, dm=dm, tt=tt, j=j,
                     route_previous=route_previous, count_routed=count_routed)

    @pl.when((n % 2 == 1) & (n <= tiles))
    def _():
        _mixp_region(zb_ref, za_ref, x_ref, mod_ref, n1_ref, wia_ref, wig_ref, wil_ref, wal_ref, bal_ref, pw_ref,
                     ps_ref, gn_ref, wa_ref, wb_ref, wo_ref, n2_ref, wrh_ref, wrl_ref, x1_ref, h2_ref, lg_ref,
                     xk_ref, mk_ref, hn_ref, uext_ref, s_ref, b_ref, oi_ref, dm=dm, tt=tt, j=j,
                     route_previous=route_previous, count_routed=count_routed)

    @pl.when(n > tiles)
    def _():
        for pk in route_previous():
            count_routed(pk)

    xk_ref[...] = x_ref[...]
    mk_ref[...] = mod_ref[...]

    @pl.when((j == n_t - 1) & (n <= tiles))
    def _():
        np_ref[...] = uext_ref[tt + 1:tt + POOL_HIST, :]
        ng_ref[...] = s_ref[...]


def _mixp_region(zp_ref, z_ref, x_ref, mod_ref, n1_ref, wia_ref, wig_ref, wil_ref, wal_ref, bal_ref, pw_ref,
                 ps_ref, gn_ref, wa_ref, wb_ref, wo_ref, n2_ref, wrh_ref, wrl_ref, x1_ref, h2_ref, lg_ref,
                 xk_ref, mk_ref, hn_ref, uext_ref, s_ref, b_ref, oi_ref, *, dm, tt, j, route_previous, count_routed):
    picked = route_previous()
    scale = dm.dk ** -0.5
    lane = lax.broadcasted_iota(I32, (1, dm.hk), 1)
    head_masks = [(lane >= h * dm.dk) & (lane < (h + 1) * dm.dk) for h in range(dm.h)]
    causal = lax.broadcasted_iota(I32, (tt, tt), 0) >= lax.broadcasted_iota(I32, (tt, tt), 1)

    def pool_branch():
        pos = j * tt + lax.broadcasted_iota(I32, (tt, 1), 0)
        gw = dm.pw // len(POOL_WINDOWS)
        mixed = []
        for g, w in enumerate(POOL_WINDOWS):
            c0 = g * gw
            cur = uext_ref[POOL_HIST:POOL_HIST + tt, c0:c0 + gw]
            acc = cur
            for i in range(1, w):
                acc = acc + uext_ref[POOL_HIST - i:POOL_HIST - i + tt, c0:c0 + gw]
            cnt = jnp.minimum(w, pos + 1).astype(F32)
            mixed.append(acc / cnt - cur)
        return _pool_project(mixed, pw_ref, ps_ref)

    def scaled_queries(bc):
        return (z_ref[:, dm.q0:dm.q0 + dm.hk] * scale) * jnp.exp(bc)

    def inter_chunk(qe):
        s_rows = lax.broadcasted_iota(I32, (dm.hk, dm.gw), 0) // dm.dk
        s_cols = lax.broadcasted_iota(I32, (dm.hk, dm.gw), 1) // dm.dv
        s_all = jnp.where(s_rows == s_cols, jnp.concatenate([s_ref[...]] * dm.h, axis=1), 0.0)
        return _dot(qe.astype(BF16), s_all.astype(BF16))

    def finish(ya, o, filler=lambda: None):
        x1, h2, logits = _post_mix(
            dm, xk_ref[...], mk_ref[...], ya, o, z_ref[:, dm.go0:dm.go0 + dm.gw],
            z_ref[:, dm.ga0:dm.ga0 + dm.d], z_ref[:, dm.gb0:dm.gb0 + dm.d],
            gn_ref, wa_ref, wb_ref, wo_ref, n2_ref, wrh_ref, wrl_ref, filler)
        x1_ref[...] = x1
        h2_ref[...] = h2
        lg_ref[...] = logits

    mod_in = mod_ref[...]
    hn_ref[...] = (_rms(x_ref[...], n1_ref[...]) * (1 + mod_in[:, dm.d:2 * dm.d]) + mod_in[:, 0:dm.d]).astype(BF16)
    pieces = iter([(w_ref, a, min(a + PROJ_PIECE, w_ref.shape[1]), c0)
                   for w_ref, c0 in ((wia_ref, 0), (wil_ref, dm.al0), (wig_ref, dm.ga0))
                   for a in range(0, w_ref.shape[1], PROJ_PIECE)])

    def project_piece():
        piece = next(pieces, None)
        if piece is not None:
            w_ref, a, b, c0 = piece
            zp_ref[:, c0 + a:c0 + b] = _dot(hn_ref[...], w_ref[:, a:b])
        return piece is not None

    for _ in range(PROJ_LEAD):
        project_piece()
    uext_ref[POOL_HIST:POOL_HIST + tt, :] = z_ref[:, 0:dm.pw]
    ya = pool_branch()
    project_piece()
    la = _log_decay(z_ref[:, dm.al0:dm.al0 + LANE], wal_ref, bal_ref)
    tri = jnp.where(causal, 1.0, 0.0).astype(BF16)
    a0, a1, a2 = _split3(la)
    bc = _dot(tri, a0) + (_dot(tri, a1) + _dot(tri, a2))
    b_ref[...] = bc
    project_piece()
    qe = scaled_queries(bc)
    k = z_ref[:, dm.k0:dm.k0 + dm.hk]
    vb = z_ref[:, dm.v0:dm.v0 + dm.gw].astype(BF16)
    blast = bc[tt - 1:tt, :]
    fast = jnp.max(-blast) <= GLA_FAST_MAX_DECAY
    ke = (k * jnp.exp(-bc)).astype(BF16)
    intra = []
    for h in range(dm.h):
        qh = jnp.where(head_masks[h], qe, 0.0).astype(BF16)
        sc = lax.dot_general(qh, ke, (((1,), (1,)), ((), ())), preferred_element_type=F32)
        sc = jnp.where(causal, sc, 0.0).astype(BF16)
        intra.append(_dot(sc, vb[:, h * dm.dv:(h + 1) * dm.dv]))
    project_piece()
    filler_calls = iter(range(4))

    def filler():
        call = next(filler_calls)
        for g, pk in enumerate(picked):
            if call == min(1 + g, 2):
                count_routed(pk)
        project_piece()

    finish(ya, inter_chunk(qe) + jnp.concatenate(intra, axis=1), filler)
    while project_piece():
        pass

    @pl.when(jnp.logical_not(fast))
    def _():
        rows = lax.broadcasted_iota(I32, (tt, 1), 0)
        for h in range(dm.h):
            def body(tb, carry, h=h):
                base = pl.multiple_of(tb * 8, 8)
                q8 = z_ref[pl.ds(base, 8), dm.q0:dm.q0 + dm.hk] * scale
                b8 = b_ref[pl.ds(base, 8), :]
                vh = z_ref[:, dm.v0 + h * dm.dv:dm.v0 + (h + 1) * dm.dv]
                out_rows = []
                for r in range(8):
                    keep = (rows <= base + r) & head_masks[h]
                    dec = jnp.exp(jnp.where(keep, b8[r:r + 1, :] - b_ref[...], -jnp.inf))
                    wgt = (q8[r:r + 1, :] * dec) * z_ref[:, dm.k0:dm.k0 + dm.hk]
                    s_col = jnp.sum(wgt, axis=-1, keepdims=True)
                    out_rows.append(jnp.sum(s_col * vh, axis=0, keepdims=True))
                oi_ref[pl.ds(base, 8), h * dm.dv:(h + 1) * dm.dv] = jnp.concatenate(out_rows, axis=0)
                return carry
            lax.fori_loop(0, tt // 8, body, 0)
        finish(pool_branch(), inter_chunk(scaled_queries(b_ref[...])) + oi_ref[...])

    bc = b_ref[...]
    blast = bc[tt - 1:tt, :]
    kd = z_ref[:, dm.k0:dm.k0 + dm.hk] * jnp.exp(blast - bc)
    u_all = _dot(kd.T.astype(BF16), z_ref[:, dm.v0:dm.v0 + dm.gw].astype(BF16))
    dec_col = jnp.broadcast_to(jnp.exp(blast), (dm.hk, dm.hk)).T[:, 0:dm.dv]
    upd = jnp.concatenate(
        [u_all[h * dm.dk:(h + 1) * dm.dk, h * dm.dv:(h + 1) * dm.dv] for h in range(dm.h)], axis=0)
    s_ref[...] = dec_col * s_ref[...] + upd
    uext_ref[0:POOL_HIST, :] = uext_ref[tt:tt + POOL_HIST, :]


def _const_spec(shape):
    nd = len(shape)
    return pl.BlockSpec(shape, lambda *_: (0,) * nd, pipeline_mode=pl.Buffered(1))


def _mix_prompt(dm, x2d, mod3, wts, bsz, seq, n_exp):
    tt = 256 if seq % 256 == 0 else seq
    n_t = seq // tt
    n = bsz * seq
    d = dm.d
    tiles = bsz * n_t
    kern = functools.partial(_mixp_kernel, dm=dm, tt=tt, n_t=n_t, tiles=tiles)
    proj = lambda i: jnp.minimum(i, tiles - 1)
    fin = lambda i: jnp.clip(i - 1, 0, tiles - 1)
    routed = lambda i: jnp.clip(i - 2, 0, tiles - 1)
    row = lambda i: (fin(i), 0)
    in_specs = [pl.BlockSpec((tt, d), lambda i: (proj(i), 0)),
                pl.BlockSpec((None, 1, N_MOD * d), lambda i: (proj(i) // n_t, 0, 0))]
    in_specs += [_const_spec(w.shape) for w in wts]
    out_specs = [pl.BlockSpec((tt, d), row), pl.BlockSpec((tt, d // 2), row),
                 pl.BlockSpec((TOP_K, tt), lambda i: (0, routed(i))), pl.BlockSpec((tt, LANE), lambda i: (routed(i), 0)),
                 pl.BlockSpec((8, n_exp), lambda i: (0, 0)),
                 pl.BlockSpec((None, POOL_HIST - 1, dm.pw), lambda i: (fin(i) // n_t, 0, 0)),
                 pl.BlockSpec((None, dm.hk, dm.dv), lambda i: (fin(i) // n_t, 0, 0))]
    out_shape = [jax.ShapeDtypeStruct((n, d), F32), jax.ShapeDtypeStruct((n, d // 2), I32),
                 jax.ShapeDtypeStruct((TOP_K, n), I32), jax.ShapeDtypeStruct((n, LANE), F32),
                 jax.ShapeDtypeStruct((8, n_exp), F32),
                 jax.ShapeDtypeStruct((bsz, POOL_HIST - 1, dm.pw), F32),
                 jax.ShapeDtypeStruct((bsz, dm.hk, dm.dv), F32)]
    scratch = [pltpu.VMEM((tt, dm.zw), F32), pltpu.VMEM((tt, dm.zw), F32),
               pltpu.VMEM((tt, d), F32), pltpu.VMEM((1, N_MOD * d), F32), pltpu.VMEM((tt, d), BF16),
               pltpu.VMEM((tt, n_exp), F32), pltpu.VMEM((tt + POOL_HIST, dm.pw), F32),
               pltpu.VMEM((dm.hk, dm.dv), F32), pltpu.VMEM((tt, dm.hk), F32), pltpu.VMEM((tt, dm.gw), F32)]
    return pl.pallas_call(
        kern, grid=(tiles + 2,), in_specs=in_specs, out_specs=out_specs, out_shape=out_shape,
        scratch_shapes=scratch, compiler_params=_cp(("arbitrary",)), name="mixp",
    )(x2d, mod3, *wts)


def _sin_kernel(x_ref, mod_ref, n1_ref, wia_ref, wig_ref, wil_ref, z_ref, *, dm):
    _in_project(dm, x_ref[...], mod_ref[...], n1_ref, (wia_ref, wig_ref, wil_ref), z_ref)


def _sample_in(dm, xs2d, mod_s, n1, win, bs, t_s):
    d = dm.d
    return pl.pallas_call(
        functools.partial(_sin_kernel, dm=dm),
        grid=(t_s,),
        in_specs=[pl.BlockSpec((bs, d), lambda t: (0, t)), _const_spec(mod_s.shape), _const_spec(n1.shape)]
        + [_const_spec(w.shape) for w in win],
        out_specs=pl.BlockSpec((bs, dm.zw), lambda t: (t, 0)),
        out_shape=jax.ShapeDtypeStruct((t_s * bs, dm.zw), F32),
        compiler_params=_cp(("arbitrary",)), name="sample_in",
    )(xs2d, mod_s, n1, *win)


def _sst_kernel(z_ref, pool_ref, gs_ref, wal_ref, bal_ref,
                mixed_ref, o_ref, npool_ref, ngs_ref,
                qt_ref, kt_ref, dt_ref, vt_ref, ot_ref, *, dm, bs, t_s, ch, pos0):
    i = pl.program_id(0)
    n_i = pl.num_programs(0)
    scale = dm.dk ** -0.5
    past = POOL_HIST - 1

    @pl.when(i == 0)
    def _():
        for t in range(t_s):
            r0 = t * bs
            la = _log_decay(z_ref[r0:r0 + bs, dm.al0:dm.al0 + LANE], wal_ref, bal_ref)
            dt_ref[t] = jnp.exp(la).T
            qt_ref[t] = (z_ref[r0:r0 + bs, dm.q0:dm.q0 + dm.hk] * scale).T
            kt_ref[t] = z_ref[r0:r0 + bs, dm.k0:dm.k0 + dm.hk].T
            vt_ref[t] = z_ref[r0:r0 + bs, dm.v0:dm.v0 + dm.gw].T
        ot_ref[...] = jnp.zeros(ot_ref.shape, F32)

        def item(s, c0, c1):
            if s >= 0:
                return z_ref[s * bs:(s + 1) * bs, c0:c1]
            return pool_ref[past + s, :, c0:c1]

        gw = dm.pw // len(POOL_WINDOWS)
        for t in range(t_s):
            for g, w in enumerate(POOL_WINDOWS):
                c0, c1 = g * gw, (g + 1) * gw
                cur = item(t, c0, c1)
                acc = cur
                for r in range(1, w):
                    acc = acc + item(t - r, c0, c1)
                cnt = float(min(w, pos0 + t + 1))
                mixed_ref[t * bs:(t + 1) * bs, c0:c1] = acc / cnt - cur
        for r in range(past):
            npool_ref[r] = item(r - past + t_s, 0, dm.pw)

    per_head = dm.dk // ch
    hoff = pl.multiple_of((i // per_head) * dm.dv, dm.dv)
    for p in range(ch):
        hd = i * ch + p
        st = gs_ref[:, p, :].T
        for t in range(t_s):
            st = dt_ref[t, pl.ds(hd, 1), :] * st + kt_ref[t, pl.ds(hd, 1), :] * vt_ref[t, pl.ds(hoff, dm.dv), :]
            ot_ref[t, pl.ds(hoff, dm.dv), :] += qt_ref[t, pl.ds(hd, 1), :] * st
        ngs_ref[:, p, :] = st.T

    @pl.when(i == n_i - 1)
    def _():
        for t in range(t_s):
            for h in range(dm.h):
                o_ref[t * bs:(t + 1) * bs, h * dm.dv:(h + 1) * dm.dv] = ot_ref[t, h * dm.dv:(h + 1) * dm.dv, :].T


def _sample_state(dm, z_s, pool3, gs3, wal, bal, bs, t_s):
    ch = 16
    n_i = dm.hk // ch
    past = POOL_HIST - 1
    kern = functools.partial(_sst_kernel, dm=dm, bs=bs, t_s=t_s, ch=ch, pos0=PAST_LEN)
    full = lambda shape: pl.BlockSpec(shape, lambda i: (0,) * len(shape))
    return pl.pallas_call(
        kern, grid=(n_i,),
        in_specs=[_const_spec(z_s.shape), _const_spec(pool3.shape),
                  pl.BlockSpec((bs, ch, dm.dv), lambda i: (0, i, 0)),
                  _const_spec(wal.shape), _const_spec(bal.shape)],
        out_specs=[full((t_s * bs, dm.pw)), full((t_s * bs, dm.gw)), full((past, bs, dm.pw)),
                   pl.BlockSpec((bs, ch, dm.dv), lambda i: (0, i, 0))],
        out_shape=[jax.ShapeDtypeStruct((t_s * bs, dm.pw), F32), jax.ShapeDtypeStruct((t_s * bs, dm.gw), F32),
                   jax.ShapeDtypeStruct((past, bs, dm.pw), F32), jax.ShapeDtypeStruct(gs3.shape, F32)],
        scratch_shapes=[pltpu.VMEM((t_s, dm.hk, bs), F32), pltpu.VMEM((t_s, dm.hk, bs), F32),
                        pltpu.VMEM((t_s, dm.hk, bs), F32), pltpu.VMEM((t_s, dm.gw, bs), F32),
                        pltpu.VMEM((t_s, dm.gw, bs), F32)],
        compiler_params=_cp(("arbitrary",)), name="sample_state",
    )(z_s, pool3, gs3, wal, bal)


def _spost_kernel(x_ref, mod_ref, z_ref, mixed_ref, o_ref, pw_ref, ps_ref, gn_ref,
                  wa_ref, wb_ref, wo_ref, n2_ref, wrh_ref, wrl_ref, rb_ref,
                  x1_ref, h2_ref, idx_ref, w8_ref, cnt_ref, *, dm):
    gw = dm.pw // len(POOL_WINDOWS)

    @pl.when(pl.program_id(0) == 0)
    def _():
        cnt_ref[...] = jnp.zeros(cnt_ref.shape, F32)
    ya = _pool_project([mixed_ref[:, g * gw:(g + 1) * gw] for g in range(len(POOL_WINDOWS))], pw_ref, ps_ref)
    x1, h2, logits = _post_mix(
        dm, x_ref[...], mod_ref[...], ya, o_ref[...],
        z_ref[:, dm.go0:dm.go0 + dm.gw], z_ref[:, dm.ga0:dm.ga0 + dm.d], z_ref[:, dm.gb0:dm.gb0 + dm.d],
        gn_ref, wa_ref, wb_ref, wo_ref, n2_ref, wrh_ref, wrl_ref)
    x1_ref[...] = x1
    h2_ref[...] = h2
    idx_ref[...], w8_ref[...], picked = _topk_rows(logits, rb_ref[...])
    cnt_ref[...] += _count_members(picked)


def _sample_post(dm, xs2d, mod_s, z_s, mixed_s, o_s, wts, bs, t_s, n_exp):
    d = dm.d
    rows = lambda t: (t, 0)
    in_specs = [pl.BlockSpec((bs, d), lambda t: (0, t)), _const_spec(mod_s.shape),
                pl.BlockSpec((bs, dm.zw), rows), pl.BlockSpec((bs, dm.pw), rows), pl.BlockSpec((bs, dm.gw), rows)]
    in_specs += [_const_spec(w.shape) for w in wts]
    return pl.pallas_call(
        functools.partial(_spost_kernel, dm=dm), grid=(t_s,), in_specs=in_specs,
        out_specs=[pl.BlockSpec((bs, d), rows), pl.BlockSpec((bs, d // 2), rows),
                   pl.BlockSpec((TOP_K, bs), lambda t: (0, t)), pl.BlockSpec((bs, LANE), rows),
                   pl.BlockSpec((8, n_exp), lambda t: (0, 0))],
        out_shape=[jax.ShapeDtypeStruct((t_s * bs, d), F32), jax.ShapeDtypeStruct((t_s * bs, d // 2), I32),
                   jax.ShapeDtypeStruct((TOP_K, t_s * bs), I32), jax.ShapeDtypeStruct((t_s * bs, LANE), F32),
                   jax.ShapeDtypeStruct((8, n_exp), F32)],
        compiler_params=_cp(("arbitrary",)), name="sample_post",
    )(xs2d, mod_s, z_s, mixed_s, o_s, *wts)


def _route_kernel(lg_ref, bias_ref, idx_ref, w_ref, cnt_ref):
    i = pl.program_id(0)
    idx_ref[...], w_ref[...], picked = _topk_rows(lg_ref[...], bias_ref[...])

    @pl.when(i == 0)
    def _():
        cnt_ref[...] = jnp.zeros(cnt_ref.shape, F32)

    cnt_ref[...] += _count_members(picked)


def _route(logits, bias, tr):
    n, n_exp = logits.shape
    rows = lambda i: (i, 0)
    return pl.pallas_call(
        _route_kernel, grid=(n // tr,),
        in_specs=[pl.BlockSpec((tr, n_exp), rows), _const_spec((1, n_exp))],
        out_specs=[pl.BlockSpec((TOP_K, tr), lambda i: (0, i)), pl.BlockSpec((tr, LANE), rows),
                   pl.BlockSpec((8, n_exp), lambda i: (0, 0))],
        out_shape=[jax.ShapeDtypeStruct((TOP_K, n), I32), jax.ShapeDtypeStruct((n, LANE), F32),
                   jax.ShapeDtypeStruct((8, n_exp), F32)],
        compiler_params=_cp(("arbitrary",)), name="route",
    )(logits, bias)


def _plan_kernel(idx_ref, cnt_ref, pos_ref, bounds_ref, base_ref, dest_ref, *, n_exp):
    i = pl.program_id(0)
    tr = idx_ref.shape[1]

    @pl.when(i == 0)
    def _():
        lane = lax.broadcasted_iota(I32, (1, n_exp), 1)
        cnt = cnt_ref[...].astype(I32)
        padded = ((cnt + (EXPERT_BLOCK - 1)) // EXPERT_BLOCK) * EXPERT_BLOCK
        pe = padded
        s = 1
        while s < n_exp:
            pe = pe + jnp.where(lane >= s, pltpu.roll(pe, s, 1), 0)
            s *= 2
        row = lax.broadcasted_iota(I32, (8, n_exp), 0)
        bounds_ref[...] = jnp.where(row == 0, pe - padded, pe)
        first = (pe - padded).astype(F32)[0:1, :]
        base_ref[...] = jnp.broadcast_to(first, (LANE, n_exp)).T

    eidx = lax.broadcasted_iota(I32, (n_exp, 1), 0)
    idx = idx_ref[...]
    member = eidx == idx[0:1, :]
    for k in range(1, TOP_K):
        member = member | (eidx == idx[k:k + 1, :])
    mb = jnp.where(member, 1.0, 0.0).astype(BF16)
    before = jnp.where(lax.broadcasted_iota(I32, (tr, tr), 0) < lax.broadcasted_iota(I32, (tr, tr), 1),
                       1.0, 0.0).astype(BF16)
    dest_ref[...] = _dot(mb, before) + base_ref[:, 0:1]
    for b in range(tr // LANE):
        cols = slice(b * LANE, (b + 1) * LANE)
        dest = dest_ref[:, cols]
        rows = [jnp.sum(jnp.where(eidx == idx[k:k + 1, cols], dest, 0.0), axis=0, keepdims=True)
                for k in range(TOP_K)]
        pos_ref[:, cols] = jnp.concatenate(rows, axis=0).astype(I32)
    base_ref[...] += jnp.sum(mb.astype(F32), axis=1, keepdims=True)


def _plan(idx_t, counts, tr):
    n = idx_t.shape[1]
    n_exp = counts.shape[1]
    cols = lambda i: (0, i)
    return pl.pallas_call(
        functools.partial(_plan_kernel, n_exp=n_exp), grid=(n // tr,),
        in_specs=[pl.BlockSpec((TOP_K, tr), cols), _const_spec((8, n_exp))],
        out_specs=[pl.BlockSpec((TOP_K, tr), cols), pl.BlockSpec((8, n_exp), lambda i: (0, 0))],
        out_shape=[jax.ShapeDtypeStruct((TOP_K, n), I32), jax.ShapeDtypeStruct((8, n_exp), I32)],
        scratch_shapes=[pltpu.VMEM((n_exp, LANE), F32), pltpu.VMEM((n_exp, tr), F32)],
        compiler_params=_cp(("arbitrary",)), name="plan",
    )(idx_t, counts)


SC_CORES = 2
SC_SUBCORES = 16
SC_AXES = ("core", "subcore")
SC_WORKERS = SC_CORES * SC_SUBCORES
SC_WINDOW = 128


def _sc_mesh():
    return plsc.VectorSubcoreMesh(core_axis_name=SC_AXES[0], subcore_axis_name=SC_AXES[1],
                                  num_cores=SC_CORES, num_subcores=SC_SUBCORES)


def _sc_worker():
    return lax.axis_index(SC_AXES[1]) * SC_CORES + lax.axis_index(SC_AXES[0])


def _dispatch(h_p, h_s, pos_t, n_rows):
    n_p, c = h_p.shape
    n_s = h_s.shape[0]
    win_p, win_s = n_p // SC_WINDOW, n_s // SC_WINDOW
    assert n_p % (SC_WINDOW * SC_WORKERS) == 0 and n_s % SC_WINDOW == 0 and win_s <= SC_WORKERS

    @functools.partial(
        pl.kernel, out_type=jax.ShapeDtypeStruct((n_rows, c), h_p.dtype), mesh=_sc_mesh(), name="dispatch",
        scratch_types=[pltpu.VMEM((TOP_K, SC_WINDOW), I32), pltpu.VMEM((SC_WINDOW, c), h_p.dtype),
                       pltpu.SemaphoreType.DMA])
    def run(hp_hbm, hs_hbm, pos_hbm, xs_hbm, idx_v, rows_v, sem):
        wid = _sc_worker()

        def scatter_window(h_hbm, row0, tok0):
            pltpu.sync_copy(h_hbm.at[pl.ds(row0, SC_WINDOW)], rows_v)
            pltpu.sync_copy(pos_hbm.at[:, pl.ds(tok0, SC_WINDOW)], idx_v)
            copies = [pltpu.async_copy(rows_v, xs_hbm.at[idx_v.at[k]], sem) for k in range(TOP_K)]
            for cp in copies:
                cp.wait()

        def prompt_window(j, carry):
            row0 = pl.multiple_of((wid * (win_p // SC_WORKERS) + j) * SC_WINDOW, SC_WINDOW)
            scatter_window(hp_hbm, row0, row0)
            return carry
        lax.fori_loop(0, win_p // SC_WORKERS, prompt_window, 0)

        @pl.when(wid < win_s)
        def _():
            row0 = pl.multiple_of(wid * SC_WINDOW, SC_WINDOW)
            scatter_window(hs_hbm, row0, n_p + row0)

    return run(h_p, h_s, pos_t)


def _combine_gather(ys, pos_t):
    k_n = pos_t.shape[0] * pos_t.shape[1]
    c = ys.shape[1]
    win = SC_WINDOW // 2
    per_worker = k_n // (win * SC_WORKERS)
    assert k_n % (win * SC_WORKERS) == 0 and per_worker % 2 == 0
    dma = pltpu.SemaphoreType.DMA

    @functools.partial(
        pl.kernel, out_type=jax.ShapeDtypeStruct((k_n, c), ys.dtype), mesh=_sc_mesh(), name="combine_gather",
        scratch_types=[pltpu.VMEM((win,), I32), pltpu.VMEM((win,), I32),
                       pltpu.VMEM((win, c), ys.dtype), pltpu.VMEM((win, c), ys.dtype), dma, dma, dma, dma])
    def run(ys_hbm, idx_hbm, out_hbm, idx0, idx1, buf0, buf1, gsem0, gsem1, ssem0, ssem1):
        wid = _sc_worker()

        def rows(j):
            return pl.ds(pl.multiple_of((wid * per_worker + j) * win, win), win)

        def gather(j, idx_v, buf, sem):
            pltpu.sync_copy(idx_hbm.at[rows(j)], idx_v)
            pltpu.async_copy(ys_hbm.at[idx_v], buf, sem)

        def gather_wait(idx_v, buf, sem):
            pltpu.make_async_copy(ys_hbm.at[idx_v], buf, sem).wait()

        def store(j, buf, sem):
            pltpu.async_copy(buf, out_hbm.at[rows(j)], sem)

        def store_wait(buf, sem):
            pltpu.make_async_copy(buf, out_hbm.at[rows(0)], sem).wait()

        gather(0, idx0, buf0, gsem0)

        def two_windows(i, carry):
            j = 2 * i

            @pl.when(i > 0)
            def _():
                store_wait(buf1, ssem1)

            gather(j + 1, idx1, buf1, gsem1)
            gather_wait(idx0, buf0, gsem0)
            store(j, buf0, ssem0)

            @pl.when(j + 2 < per_worker)
            def _():
                store_wait(buf0, ssem0)
                gather(j + 2, idx0, buf0, gsem0)

            gather_wait(idx1, buf1, gsem1)
            store(j + 1, buf1, ssem1)
            return carry
        lax.fori_loop(0, per_worker // 2, two_windows, 0)
        store_wait(buf0, ssem0)
        store_wait(buf1, ssem1)

    return run(ys, pos_t.reshape(k_n))


MOE_RING = 16
MOE_WEIGHT_RING = 8


def _moe_kernel(st_ref, wg_ref, wu_ref, wd_ref, xs_ref, ys_ref, xbuf, ybuf, wgf, wuf, wdf, wgb, wub, wdb,
                semx, semy, semw):
    e = pl.program_id(0)
    n_e = pl.num_programs(0)

    def w_copies(ex):
        slot = ex % MOE_WEIGHT_RING
        return [pltpu.make_async_copy(src.at[ex], dst.at[slot], semw.at[slot, i])
                for i, (src, dst) in enumerate(((wg_ref, wgf), (wu_ref, wuf), (wd_ref, wdf)))]

    @pl.when(e == 0)
    def _():
        for ex in range(MOE_WEIGHT_RING - 1):
            @pl.when(ex < n_e)
            def _(ex=ex):
                for cp in w_copies(ex):
                    cp.start()

    @pl.when(e + (MOE_WEIGHT_RING - 1) < n_e)
    def _():
        for cp in w_copies(e + (MOE_WEIGHT_RING - 1)):
            cp.start()
    g0 = st_ref[e] // EXPERT_BLOCK
    nchunks = st_ref[e + 1] // EXPERT_BLOCK - g0
    ntot = st_ref[n_e] // EXPERT_BLOCK

    def rows(g):
        return pl.ds(pl.multiple_of(g * EXPERT_BLOCK, EXPERT_BLOCK), EXPERT_BLOCK)

    def x_copy(g):
        slot = g % MOE_RING
        return pltpu.make_async_copy(xs_ref.at[rows(g)], xbuf.at[slot], semx.at[slot])

    def y_copy(g):
        slot = g % MOE_RING
        return pltpu.make_async_copy(ybuf.at[slot], ys_ref.at[rows(g)], semy.at[slot])

    @pl.when(e == 0)
    def _():
        for g in range(MOE_RING - 1):
            @pl.when(g < ntot)
            def _(g=g):
                x_copy(g).start()

    for cp in w_copies(e):
        cp.wait()
    wslot = e % MOE_WEIGHT_RING
    wgb[...] = wgf[wslot].astype(BF16)
    wub[...] = wuf[wslot].astype(BF16)
    wdb[...] = wdf[wslot].astype(BF16)

    def take(g):
        @pl.when(g + (MOE_RING - 1) < ntot)
        def _():
            x_copy(g + (MOE_RING - 1)).start()

        x_copy(g).wait()

        @pl.when(g >= MOE_RING)
        def _():
            y_copy(g - MOE_RING).wait()

        return xbuf[g % MOE_RING]

    def expert_mlp(x_words):
        x = _unpack_rows(x_words).astype(BF16)
        mid = _silu(_dot(x, wgb[...])) * _dot(x, wub[...])
        return _pack_rows(_dot(mid.astype(BF16), wdb[...]))

    def put(g, y_words):
        ybuf[g % MOE_RING] = y_words
        y_copy(g).start()

    def run_chunks(g, m):
        xs = [take(g + i) for i in range(m)]
        y = expert_mlp(xs[0] if m == 1 else jnp.concatenate(xs, axis=0))
        for i in range(m):
            put(g + i, y[i * EXPERT_BLOCK:(i + 1) * EXPERT_BLOCK])

    def quad(p, carry):
        run_chunks(g0 + 4 * p, 4)
        return carry

    rest = nchunks % 4
    five = (rest == 1) & (nchunks >= 5)
    n_quads = nchunks // 4 - five.astype(I32)
    lax.fori_loop(0, n_quads, quad, 0)
    g_rest = g0 + 4 * n_quads

    @pl.when(five)
    def _():
        run_chunks(g_rest, 5)

    @pl.when(rest >= 2)
    def _():
        run_chunks(g_rest, 2)

    @pl.when((rest == 3) | (nchunks == 1))
    def _():
        run_chunks(g0 + nchunks - 1, 1)

    @pl.when(e == n_e - 1)
    def _():
        for j in range(MOE_RING):
            g = ntot - MOE_RING + j

            @pl.when(g >= 0)
            def _(g=g):
                y_copy(g).wait()


def _moe(starts, xs, wg, wu, wd):
    n_rows, c = xs.shape
    n_exp, d, f = wg.shape
    grid_spec = pltpu.PrefetchScalarGridSpec(
        num_scalar_prefetch=1, grid=(n_exp,),
        in_specs=[pl.BlockSpec(memory_space=pl.ANY)] * 4,
        out_specs=pl.BlockSpec(memory_space=pl.ANY),
        scratch_shapes=[pltpu.VMEM((MOE_RING, EXPERT_BLOCK, c), I32), pltpu.VMEM((MOE_RING, EXPERT_BLOCK, c), I32),
                        pltpu.VMEM((MOE_WEIGHT_RING, d, f), F32), pltpu.VMEM((MOE_WEIGHT_RING, d, f), F32),
                        pltpu.VMEM((MOE_WEIGHT_RING, f, d), F32),
                        pltpu.VMEM((d, f), BF16), pltpu.VMEM((d, f), BF16), pltpu.VMEM((f, d), BF16),
                        pltpu.SemaphoreType.DMA((MOE_RING,)), pltpu.SemaphoreType.DMA((MOE_RING,)),
                        pltpu.SemaphoreType.DMA((MOE_WEIGHT_RING, 3))])
    return pl.pallas_call(
        _moe_kernel, grid_spec=grid_spec, out_shape=jax.ShapeDtypeStruct((n_rows, c), I32),
        compiler_params=_cp(("arbitrary",)), name="moe",
    )(starts, wg, wu, wd, xs)


def _final_kernel(yg_ref, w8_ref, x1_ref, h2_ref, mod_ref, wsg_ref, wsu_ref, wsd_ref, nf_ref, y_ref, *, d):
    hb = _unpack_rows(h2_ref[...]).astype(BF16)
    shared = _dot((_silu(_dot(hb, wsg_ref[...])) * _dot(hb, wsu_ref[...])).astype(BF16), wsd_ref[...])
    w8 = w8_ref[...]
    routed = _unpack_rows(yg_ref[0]) * w8[:, 0:1]
    for k in range(1, TOP_K):
        routed = routed + _unpack_rows(yg_ref[k]) * w8[:, k:k + 1]
    g2 = mod_ref[...][:, (N_MOD - 1) * d:N_MOD * d]
    x2 = x1_ref[...] + g2 * (routed + shared)
    y_ref[...] = _rms(x2, nf_ref[...])


def _final(yg3, w8, x1, h2, mod, mod_spec, wts, nf, tf, n_tiles, tile0, out_shape, out_spec):
    d = x1.shape[1]
    rows = lambda i: (i, 0)
    in_specs = [pl.BlockSpec((TOP_K, tf, d // 2), lambda i: (0, tile0 + i, 0)),
                pl.BlockSpec((tf, LANE), rows),
                pl.BlockSpec((tf, d), rows), pl.BlockSpec((tf, d // 2), rows), mod_spec]
    in_specs += [_const_spec(w.shape) for w in wts]
    in_specs += [_const_spec(nf.shape)]
    return pl.pallas_call(
        functools.partial(_final_kernel, d=d), grid=(n_tiles,), in_specs=in_specs,
        out_specs=out_spec, out_shape=out_shape,
        compiler_params=_cp(("arbitrary",)), name="final",
    )(yg3, w8, x1, h2, mod, *wts, nf)


def kernel(x_prompt, x_sample, c_prompt, c_sample, state_pool, state_gla, w_ada, b_ada, norm1, w_in,
           pool_w, pool_scale, gla_w_alpha, gla_b_alpha, gla_norm, w_branch_a, w_branch_b, w_out, norm2,
           w_router, router_bias, w_exp_gate, w_exp_up, w_exp_down, w_sh_gate, w_sh_up, w_sh_down, norm_f):
    depth = w_in.shape[0]
    assert depth == 1, "single-layer trunk"
    bp, seq, d = x_prompt.shape
    bs, t_s, _ = x_sample.shape
    _, _, past, pw = state_pool.shape
    _, _, heads, dk, dv = state_gla.shape
    rank = gla_w_alpha.shape[1]
    n_exp = w_router.shape[2]
    assert past == POOL_HIST - 1 and pw == len(POOL_WINDOWS) * LANE and dv == LANE and rank <= LANE
    dm = _Dims(d, pw, heads, dk, dv)
    n_p, n_s = bp * seq, bs * t_s
    n_all = n_p + n_s

    off = [0]
    for sz in (pw, dm.hk, dm.hk, dm.gw, dm.gw, rank, d, d):
        off.append(off[-1] + sz)
    wi = w_in[0]
    win = (wi[:, :off[5]].astype(BF16), wi[:, off[6]:].astype(BF16),
           jnp.concatenate([wi[:, off[5]:off[6]], jnp.zeros((d, LANE - rank), F32)], axis=1).astype(BF16))
    assert off[5] + 2 * d + LANE == dm.zw
    wal = jnp.concatenate([gla_w_alpha[0], jnp.zeros((LANE - rank, dm.hk), F32)], axis=0).astype(BF16)
    bal = gla_b_alpha[0].reshape(1, dm.hk)
    wr = w_router[0]
    wrh = wr.astype(BF16)
    wrl = (wr - wrh.astype(F32)).astype(BF16)
    pwg = pool_w[0]
    zero = jnp.zeros_like(pwg[0])
    pw_pairs = jnp.stack([jnp.block([[pwg[g], zero], [zero, pwg[g + 1]]]) for g in range(0, len(POOL_WINDOWS), 2)])
    post_w = (pw_pairs.astype(BF16), pool_scale[0].reshape(1, pw), gla_norm[0].reshape(1, dm.gw),
              w_branch_a[0].astype(BF16), w_branch_b[0].astype(BF16), w_out[0].astype(BF16),
              norm2[0].reshape(1, d), wrh, wrl)
    n1 = norm1[0].reshape(1, d)

    mod_p, mod_s = _ada(c_prompt, c_sample, w_ada[0], b_ada[0])
    mod_p = mod_p.reshape(bp, 1, N_MOD * d)

    rbias = router_bias[0].reshape(1, n_exp)
    x1_p, h2_p, idx_p, w8_p, cnt_p, npool_p, ngla_p = _mix_prompt(
        dm, x_prompt.reshape(n_p, d), mod_p, (n1,) + win + (wal, bal) + post_w + (rbias,), bp, seq, n_exp)

    xs2d = x_sample.reshape(bs, t_s * d)
    z_s = _sample_in(dm, xs2d, mod_s, n1, win, bs, t_s)
    mixed_s, o_s, npool_s, ngla_s = _sample_state(
        dm, z_s, jnp.swapaxes(state_pool[0], 0, 1), state_gla[0].reshape(bs, dm.hk, dv), wal, bal, bs, t_s)
    x1_s, h2_s, idx_s, w8_s, cnt_s = _sample_post(
        dm, xs2d, mod_s, z_s, mixed_s, o_s, post_w + (rbias,), bs, t_s, n_exp)

    tr = 512 if (n_p % 512 == 0 and n_s % 512 == 0) else LANE
    assert n_p % tr == 0 and n_s % tr == 0 and n_p % bs == 0
    nblk = (n_all * TOP_K) // EXPERT_BLOCK + n_exp
    pos_t, bounds = _plan(jnp.concatenate([idx_p, idx_s], axis=1), cnt_p + cnt_s, tr)
    starts = jnp.concatenate([bounds[0], bounds[1, n_exp - 1:]])

    xs = _dispatch(h2_p, h2_s, pos_t, nblk * EXPERT_BLOCK)
    ys = _moe(starts, xs, w_exp_gate[0], w_exp_up[0], w_exp_down[0])
    yg3 = _combine_gather(ys, pos_t).reshape(TOP_K, n_all, d // 2)

    sh_w = (w_sh_gate[0].astype(BF16), w_sh_up[0].astype(BF16), w_sh_down[0].astype(BF16))
    nf = norm_f.reshape(1, d)
    tf = 512 if seq % 512 == 0 else seq
    n_tp = seq // tf
    y_p = _final(
        yg3, w8_p, x1_p, h2_p, mod_p,
        pl.BlockSpec((None, 1, N_MOD * d), lambda i: (i // n_tp, 0, 0)), sh_w, nf, tf, n_p // tf, 0,
        jax.ShapeDtypeStruct((n_p, d), F32), pl.BlockSpec((tf, d), lambda i: (i, 0)))
    y_s = _final(
        yg3, w8_s, x1_s, h2_s, mod_s,
        _const_spec(mod_s.shape), sh_w, nf, bs, t_s, n_p // bs,
        jax.ShapeDtypeStruct((bs, t_s * d), F32), pl.BlockSpec((bs, d), lambda i: (0, i)))

    return (y_p.reshape(bp, seq, d), y_s.reshape(bs, t_s, d),
            npool_p.reshape(depth, bp, past, pw), ngla_p.reshape(depth, bp, heads, dk, dv),
            jnp.swapaxes(npool_s, 0, 1).reshape(depth, bs, past, pw), ngla_s.reshape(depth, bs, heads, dk, dv))
```
